```python
import math
import jax, jax.numpy as jnp
from jax import lax
import numpy as np

D_MODEL = 1024
BATCH = 8
SEQ = 8192
DEPTH = 2

CHUNK = 64
HEAD_DIM = 64
N_HEADS_A = 8
N_HEADS_B = 8
WIDTH_A = N_HEADS_A * HEAD_DIM
WIDTH_B = N_HEADS_B * HEAD_DIM
LEFT_CHUNKS = 8
BAND = (LEFT_CHUNKS + 1) * CHUNK
REL_CLIP = 256
N_REL = 2 * REL_CLIP + 1
SB_BLOCK = 128
D_FF = -(-8 * D_MODEL // (3 * 256)) * 256
IN_COLS = 3 * WIDTH_A + 3 * WIDTH_B + 2 * D_MODEL
DEEPNORM_ALPHA = (2 * DEPTH) ** 0.25
DEEPNORM_BETA = (8 * DEPTH) ** -0.25
LN_EPS = 1e-5

kernel_name = "hybrid_chunk_relbias_stickbreaking_deepnorm"


def layer_norm(x, g, b):
    xf = x.astype(jnp.float32)
    mu = jnp.mean(xf, axis=-1, keepdims=True)
    var = jnp.mean(jnp.square(xf - mu), axis=-1, keepdims=True)
    y = (xf - mu) * lax.rsqrt(var + LN_EPS) * g.astype(jnp.float32) + b.astype(jnp.float32)
    return y.astype(x.dtype)


def chunk_band_attention(q, k, v, rel_bias):
    B, S, H, dh = q.shape
    nc = S // CHUNK
    qc = q.reshape(B, nc, CHUNK, H, dh)
    pad = ((0, 0), (LEFT_CHUNKS * CHUNK, 0), (0, 0), (0, 0))
    kp = jnp.pad(k, pad).reshape(B, nc + LEFT_CHUNKS, CHUNK, H, dh)
    vp = jnp.pad(v, pad).reshape(B, nc + LEFT_CHUNKS, CHUNK, H, dh)
    kb = jnp.concatenate([kp[:, j:j + nc] for j in range(LEFT_CHUNKS + 1)], axis=2)
    vb = jnp.concatenate([vp[:, j:j + nc] for j in range(LEFT_CHUNKS + 1)], axis=2)
    scores = jnp.einsum('bcqhd,bckhd->bhcqk', qc, kb).astype(jnp.float32) / math.sqrt(dh)
    i = jnp.arange(CHUNK)[:, None]
    p = jnp.arange(BAND)[None, :]
    dist = LEFT_CHUNKS * CHUNK + i - p
    idx = jnp.clip(dist, -REL_CLIP, REL_CLIP) + REL_CLIP
    bias = rel_bias.astype(jnp.float32)[:, idx]
    valid = (jnp.arange(nc)[:, None] + jnp.arange(BAND)[None, :] // CHUNK - LEFT_CHUNKS) >= 0
    scores = scores + bias[None, :, None, :, :]
    scores = jnp.where(valid[None, None, :, None, :], scores, -jnp.inf)
    probs = jax.nn.softmax(scores, axis=-1).astype(v.dtype)
    out = jnp.einsum('bhcqk,bckhd->bcqhd', probs, vb)
    return out.reshape(B, S, H * dh)


def stick_breaking_attention(q, k, v):
    B, S, H, dh = q.shape
    nb = S // SB_BLOCK
    scale = 1.0 / math.sqrt(dh)
    qb = q.reshape(B, nb, SB_BLOCK, H, dh).transpose(1, 0, 2, 3, 4)
    key_pos = jnp.arange(S)

    def one_block(args):
        q_blk, blk = args
        z = jnp.einsum('bqhd,bshd->bhqs', q_blk, k).astype(jnp.float32) * scale
        t = blk * SB_BLOCK + jnp.arange(SB_BLOCK)
        causal = (key_pos[None, :] < t[:, None])[None, None]
        log_keep = jnp.where(causal, jax.nn.log_sigmoid(-z), 0.0)
        suffix = lax.cumsum(log_keep, axis=3, reverse=True) - log_keep
        log_w = jnp.where(causal, jax.nn.log_sigmoid(z) + suffix, -jnp.inf)
        w = jnp.exp(log_w).astype(v.dtype)
        return jnp.einsum('bhqs,bshd->bqhd', w, v)

    out = lax.map(one_block, (qb, jnp.arange(nb)))
    return out.transpose(1, 0, 2, 3, 4).reshape(B, S, H * dh)


def _fwd_setup_inputs(seed: int = 0) -> dict:
    key = jax.random.key(seed)
    ks = jax.random.split(key, 16)
    D = D_MODEL
    x = jax.random.normal(ks[0], (BATCH, SEQ, D), jnp.float32)
    w_in = jax.random.normal(ks[1], (DEPTH, D, IN_COLS), jnp.float32) * D ** -0.5
    col_scale = jnp.concatenate([
        jnp.ones((2 * WIDTH_A,)), jnp.full((WIDTH_A,), DEEPNORM_BETA),
        jnp.ones((2 * WIDTH_B,)), jnp.full((WIDTH_B,), DEEPNORM_BETA),
        jnp.ones((2 * D,))]).astype(jnp.float32)
    w_in = w_in * col_scale
    b_gate = 0.01 * jax.random.normal(ks[2], (DEPTH, 2 * D), jnp.float32)
    rel_bias = 0.1 * jax.random.normal(ks[3], (DEPTH, N_HEADS_A, N_REL), jnp.float32)
    w_proj_a = jax.random.normal(ks[4], (DEPTH, WIDTH_A, D), jnp.float32) * WIDTH_A ** -0.5
    w_proj_b = jax.random.normal(ks[5], (DEPTH, WIDTH_B, D), jnp.float32) * WIDTH_B ** -0.5
    w_out = jax.random.normal(ks[6], (DEPTH, D, D), jnp.float32) * (D ** -0.5 * DEEPNORM_BETA)
    ln1_g = 1.0 + 0.02 * jax.random.normal(ks[7], (DEPTH, D), jnp.float32)
    ln1_b = 0.02 * jax.random.normal(ks[8], (DEPTH, D), jnp.float32)
    w_ffn_in = jax.random.normal(ks[9], (DEPTH, D, 2 * D_FF), jnp.float32) * D ** -0.5
    w_ffn_out = jax.random.normal(ks[10], (DEPTH, D_FF, D), jnp.float32) * (D_FF ** -0.5 * DEEPNORM_BETA)
    ln2_g = 1.0 + 0.02 * jax.random.normal(ks[11], (DEPTH, D), jnp.float32)
    ln2_b = 0.02 * jax.random.normal(ks[12], (DEPTH, D), jnp.float32)
    return {"x": x, "w_in": w_in, "b_gate": b_gate, "rel_bias": rel_bias,
            "w_proj_a": w_proj_a, "w_proj_b": w_proj_b, "w_out": w_out,
            "ln1_g": ln1_g, "ln1_b": ln1_b, "w_ffn_in": w_ffn_in,
            "w_ffn_out": w_ffn_out, "ln2_g": ln2_g, "ln2_b": ln2_b}


def _fwd_reference(x, w_in, b_gate, rel_bias, w_proj_a, w_proj_b, w_out,
              ln1_g, ln1_b, w_ffn_in, w_ffn_out, ln2_g, ln2_b):
    B, S, D = x.shape
    split_pts = np.cumsum([WIDTH_A, WIDTH_A, WIDTH_A, WIDTH_B, WIDTH_B, WIDTH_B, D]).tolist()
    for l in range(DEPTH):
        h = x @ w_in[l]
        qa, ka, va, qb, kb, vb, ga, gb = jnp.split(h, split_pts, axis=-1)
        heads_a = lambda t: t.reshape(B, S, N_HEADS_A, HEAD_DIM)
        heads_b = lambda t: t.reshape(B, S, N_HEADS_B, HEAD_DIM)
        y_a = chunk_band_attention(heads_a(qa), heads_a(ka), heads_a(va), rel_bias[l]) @ w_proj_a[l]
        y_b = stick_breaking_attention(heads_b(qb), heads_b(kb), heads_b(vb)) @ w_proj_b[l]
        gate_a = jax.nn.sigmoid(ga + b_gate[l, :D])
        gate_b = jax.nn.sigmoid(gb + b_gate[l, D:])
        mix = (gate_a * y_a + gate_b * y_b) @ w_out[l]
        x = layer_norm(DEEPNORM_ALPHA * x + mix, ln1_g[l], ln1_b[l])
        gu = x @ w_ffn_in[l]
        g, u = jnp.split(gu, 2, axis=-1)
        ffn = (jax.nn.silu(g) * u) @ w_ffn_out[l]
        x = layer_norm(DEEPNORM_ALPHA * x + ffn, ln2_g[l], ln2_b[l])
    return x


import jax as _jax
import jax.numpy as _jnp

TWIN_FORMAT = 'train_step'
FWD_PARAMS = ['x', 'w_in', 'b_gate', 'rel_bias', 'w_proj_a', 'w_proj_b', 'w_out', 'ln1_g', 'ln1_b', 'w_ffn_in', 'w_ffn_out', 'ln2_g', 'ln2_b']
TWIN_WEIGHTS = ['w_in', 'b_gate', 'rel_bias', 'w_proj_a', 'w_proj_b', 'w_out', 'ln1_g', 'ln1_b', 'w_ffn_in', 'w_ffn_out', 'ln2_g', 'ln2_b']
TWIN_DIFF_INPUT = 'x'
TWIN_INPUTS = ['x', 'w_in', 'b_gate', 'rel_bias', 'w_proj_a', 'w_proj_b', 'w_out', 'ln1_g', 'ln1_b', 'w_ffn_in', 'w_ffn_out', 'ln2_g', 'ln2_b', 'loss_target', 'm_w_in', 'm_b_gate', 'm_rel_bias', 'm_w_proj_a', 'm_w_proj_b', 'm_w_out', 'm_ln1_g', 'm_ln1_b', 'm_w_ffn_in', 'm_w_ffn_out', 'm_ln2_g', 'm_ln2_b', 'v_w_in', 'v_b_gate', 'v_rel_bias', 'v_w_proj_a', 'v_w_proj_b', 'v_w_out', 'v_ln1_g', 'v_ln1_b', 'v_w_ffn_in', 'v_w_ffn_out', 'v_ln2_g', 'v_ln2_b']
TWIN_OUTPUTS = ['loss', 'grad_x', 'grad_w_in', 'grad_b_gate', 'grad_rel_bias', 'grad_w_proj_a', 'grad_w_proj_b', 'grad_w_out', 'grad_ln1_g', 'grad_ln1_b', 'grad_w_ffn_in', 'grad_w_ffn_out', 'grad_ln2_g', 'grad_ln2_b', 'delta_w_in', 'delta_b_gate', 'delta_rel_bias', 'delta_w_proj_a', 'delta_w_proj_b', 'delta_w_out', 'delta_ln1_g', 'delta_ln1_b', 'delta_w_ffn_in', 'delta_w_ffn_out', 'delta_ln2_g', 'delta_ln2_b', 'new_m_w_in', 'new_m_b_gate', 'new_m_rel_bias', 'new_m_w_proj_a', 'new_m_w_proj_b', 'new_m_w_out', 'new_m_ln1_g', 'new_m_ln1_b', 'new_m_w_ffn_in', 'new_m_w_ffn_out', 'new_m_ln2_g', 'new_m_ln2_b', 'new_v_w_in', 'new_v_b_gate', 'new_v_rel_bias', 'new_v_w_proj_a', 'new_v_w_proj_b', 'new_v_w_out', 'new_v_ln1_g', 'new_v_ln1_b', 'new_v_w_ffn_in', 'new_v_w_ffn_out', 'new_v_ln2_g', 'new_v_ln2_b']
TWIN_LEAF_KINDS = {'loss': 'loss', 'grad_x': 'grad_x', 'grad_w_in': 'grad_w', 'grad_b_gate': 'grad_w', 'grad_rel_bias': 'grad_w', 'grad_w_proj_a': 'grad_w', 'grad_w_proj_b': 'grad_w', 'grad_w_out': 'grad_w', 'grad_ln1_g': 'grad_w', 'grad_ln1_b': 'grad_w', 'grad_w_ffn_in': 'grad_w', 'grad_w_ffn_out': 'grad_w', 'grad_ln2_g': 'grad_w', 'grad_ln2_b': 'grad_w', 'delta_w_in': 'delta_w', 'delta_b_gate': 'delta_w', 'delta_rel_bias': 'delta_w', 'delta_w_proj_a': 'delta_w', 'delta_w_proj_b': 'delta_w', 'delta_w_out': 'delta_w', 'delta_ln1_g': 'delta_w', 'delta_ln1_b': 'delta_w', 'delta_w_ffn_in': 'delta_w', 'delta_w_ffn_out': 'delta_w', 'delta_ln2_g': 'delta_w', 'delta_ln2_b': 'delta_w', 'new_m_w_in': 'new_m', 'new_m_b_gate': 'new_m', 'new_m_rel_bias': 'new_m', 'new_m_w_proj_a': 'new_m', 'new_m_w_proj_b': 'new_m', 'new_m_w_out': 'new_m', 'new_m_ln1_g': 'new_m', 'new_m_ln1_b': 'new_m', 'new_m_w_ffn_in': 'new_m', 'new_m_w_ffn_out': 'new_m', 'new_m_ln2_g': 'new_m', 'new_m_ln2_b': 'new_m', 'new_v_w_in': 'new_v', 'new_v_b_gate': 'new_v', 'new_v_rel_bias': 'new_v', 'new_v_w_proj_a': 'new_v', 'new_v_w_proj_b': 'new_v', 'new_v_w_out': 'new_v', 'new_v_ln1_g': 'new_v', 'new_v_ln1_b': 'new_v', 'new_v_w_ffn_in': 'new_v', 'new_v_w_ffn_out': 'new_v', 'new_v_ln2_g': 'new_v', 'new_v_ln2_b': 'new_v'}


def _forward(args):
    return _fwd_reference(*[args[k] for k in FWD_PARAMS])


def _output_shape():
    def fwd():
        inp = _fwd_setup_inputs(0)
        return _fwd_reference(*[inp[k] for k in FWD_PARAMS])
    out = _jax.eval_shape(fwd)
    return out.shape, out.dtype

N_MICROBATCH = 1
ADAM_LR = 0.001
ADAM_B1 = 0.9
ADAM_B2 = 0.999
ADAM_EPS = 1e-08
ADAM_WD = 0.01
ADAM_STEP = 10
PER_EXAMPLE_BATCH_AXIS = {'x': 0, 'loss_target': 0}
SHARED_INPUTS = []
_WEIGHT_DTYPES = {'w_in': _jnp.float32, 'b_gate': _jnp.float32, 'rel_bias': _jnp.float32, 'w_proj_a': _jnp.float32, 'w_proj_b': _jnp.float32, 'w_out': _jnp.float32, 'ln1_g': _jnp.float32, 'ln1_b': _jnp.float32, 'w_ffn_in': _jnp.float32, 'w_ffn_out': _jnp.float32, 'ln2_g': _jnp.float32, 'ln2_b': _jnp.float32}
MOMENT_SCALE = {'w_in': 1.568856e-02, 'b_gate': 4.497057e-03, 'rel_bias': 1.708200e-03, 'w_proj_a': 3.767275e-03, 'w_proj_b': 1.597592e-02, 'w_out': 3.255453e-02, 'ln1_g': 1.988763e+00, 'ln1_b': 9.757198e-01, 'w_ffn_in': 3.204021e-02, 'w_ffn_out': 1.049123e-01, 'ln2_g': 4.535148e+01, 'ln2_b': 1.636299e+00}


def _to_microbatches(a, axis):
    t = _jnp.moveaxis(a, axis, 0)
    t = t.reshape((N_MICROBATCH, t.shape[0] // N_MICROBATCH) + t.shape[1:])
    return _jnp.moveaxis(t, 1, axis + 1)


def setup_inputs(seed: int = 0) -> dict:
    inp = _fwd_setup_inputs(seed)
    key = _jax.random.fold_in(_jax.random.key(seed), 7919)
    shape, _ = _output_shape()
    out = dict(inp)
    out["loss_target"] = _jax.random.normal(_jax.random.fold_in(key, 0), shape, _jnp.float32)
    for i, name in enumerate(TWIN_WEIGHTS):
        w = inp[name].astype(_jnp.float32)
        if MOMENT_SCALE is None:
            s = _jnp.sqrt(_jnp.mean(_jnp.square(w)) + 1e-30)
        else:
            s = MOMENT_SCALE[name]
        km, kv = _jax.random.split(_jax.random.fold_in(key, i + 1))
        out[name] = w
        out["m_" + name] = s * _jax.random.normal(km, w.shape, _jnp.float32)
        out["v_" + name] = (s * s) * _jax.random.uniform(kv, w.shape, _jnp.float32, 0.5, 1.5)
    if N_MICROBATCH > 1:
        for name, axis in PER_EXAMPLE_BATCH_AXIS.items():
            out[name] = _to_microbatches(out[name], axis)
    return {'x': out['x'], 'w_in': out['w_in'], 'b_gate': out['b_gate'], 'rel_bias': out['rel_bias'], 'w_proj_a': out['w_proj_a'], 'w_proj_b': out['w_proj_b'], 'w_out': out['w_out'], 'ln1_g': out['ln1_g'], 'ln1_b': out['ln1_b'], 'w_ffn_in': out['w_ffn_in'], 'w_ffn_out': out['w_ffn_out'], 'ln2_g': out['ln2_g'], 'ln2_b': out['ln2_b'], 'loss_target': out['loss_target'], 'm_w_in': out['m_w_in'], 'm_b_gate': out['m_b_gate'], 'm_rel_bias': out['m_rel_bias'], 'm_w_proj_a': out['m_w_proj_a'], 'm_w_proj_b': out['m_w_proj_b'], 'm_w_out': out['m_w_out'], 'm_ln1_g': out['m_ln1_g'], 'm_ln1_b': out['m_ln1_b'], 'm_w_ffn_in': out['m_w_ffn_in'], 'm_w_ffn_out': out['m_w_ffn_out'], 'm_ln2_g': out['m_ln2_g'], 'm_ln2_b': out['m_ln2_b'], 'v_w_in': out['v_w_in'], 'v_b_gate': out['v_b_gate'], 'v_rel_bias': out['v_rel_bias'], 'v_w_proj_a': out['v_w_proj_a'], 'v_w_proj_b': out['v_w_proj_b'], 'v_w_out': out['v_w_out'], 'v_ln1_g': out['v_ln1_g'], 'v_ln1_b': out['v_ln1_b'], 'v_w_ffn_in': out['v_w_ffn_in'], 'v_w_ffn_out': out['v_w_ffn_out'], 'v_ln2_g': out['v_ln2_g'], 'v_ln2_b': out['v_ln2_b']}


def _loss(weights, diff, rest, loss_target):
    with _jax.named_scope("forward"):
        args = {**rest, TWIN_DIFF_INPUT: diff, **{k: w.astype(_WEIGHT_DTYPES[k]) for k, w in weights.items()}}
        y = _forward(args)
    with _jax.named_scope("loss_head"):
        err = _jnp.square(y.astype(_jnp.float32) - loss_target)
        return 0.5 * _jnp.sum(_jnp.mean(err, axis=-1)) if err.ndim else 0.5 * err


def _adamw(w, g, m, v):
    m = ADAM_B1 * m + (1.0 - ADAM_B1) * g
    v = ADAM_B2 * v + (1.0 - ADAM_B2) * _jnp.square(g)
    m_hat = m / (1.0 - ADAM_B1 ** ADAM_STEP)
    v_hat = v / (1.0 - ADAM_B2 ** ADAM_STEP)
    delta = -ADAM_LR * (m_hat / (_jnp.sqrt(v_hat) + ADAM_EPS) + ADAM_WD * w)
    return delta, m, v


def reference(x, w_in, b_gate, rel_bias, w_proj_a, w_proj_b, w_out, ln1_g, ln1_b, w_ffn_in, w_ffn_out, ln2_g, ln2_b, loss_target, m_w_in, m_b_gate, m_rel_bias, m_w_proj_a, m_w_proj_b, m_w_out, m_ln1_g, m_ln1_b, m_w_ffn_in, m_w_ffn_out, m_ln2_g, m_ln2_b, v_w_in, v_b_gate, v_rel_bias, v_w_proj_a, v_w_proj_b, v_w_out, v_ln1_g, v_ln1_b, v_w_ffn_in, v_w_ffn_out, v_ln2_g, v_ln2_b):
    given = dict(x=x, w_in=w_in, b_gate=b_gate, rel_bias=rel_bias, w_proj_a=w_proj_a, w_proj_b=w_proj_b, w_out=w_out, ln1_g=ln1_g, ln1_b=ln1_b, w_ffn_in=w_ffn_in, w_ffn_out=w_ffn_out, ln2_g=ln2_g, ln2_b=ln2_b, loss_target=loss_target, m_w_in=m_w_in, m_b_gate=m_b_gate, m_rel_bias=m_rel_bias, m_w_proj_a=m_w_proj_a, m_w_proj_b=m_w_proj_b, m_w_out=m_w_out, m_ln1_g=m_ln1_g, m_ln1_b=m_ln1_b, m_w_ffn_in=m_w_ffn_in, m_w_ffn_out=m_w_ffn_out, m_ln2_g=m_ln2_g, m_ln2_b=m_ln2_b, v_w_in=v_w_in, v_b_gate=v_b_gate, v_rel_bias=v_rel_bias, v_w_proj_a=v_w_proj_a, v_w_proj_b=v_w_proj_b, v_w_out=v_w_out, v_ln1_g=v_ln1_g, v_ln1_b=v_ln1_b, v_w_ffn_in=v_w_ffn_in, v_w_ffn_out=v_w_ffn_out, v_ln2_g=v_ln2_g, v_ln2_b=v_ln2_b)
    weights = {n: given[n] for n in TWIN_WEIGHTS}
    shared = {n: given[n] for n in SHARED_INPUTS}
    per_example = {n: given[n] for n in ['x']}
    grad_fn = _jax.value_and_grad(_loss, argnums=(0, 1))

    def one_microbatch(ex, loss_target):
        ex = dict(ex)
        diff = ex.pop(TWIN_DIFF_INPUT)
        return grad_fn(weights, diff, {**shared, **ex}, loss_target)

    if N_MICROBATCH == 1:
        loss, (grad_w, grad_x) = one_microbatch(per_example, given["loss_target"])
    else:
        def body(carry, xs):
            loss_sum, grad_sum = carry
            l_k, (gw_k, gx_k) = one_microbatch(xs[0], xs[1])
            with _jax.named_scope("update"):
                return (loss_sum + l_k, _jax.tree.map(_jnp.add, grad_sum, gw_k)), gx_k

        init = (_jnp.zeros((), _jnp.float32), _jax.tree.map(_jnp.zeros_like, weights))
        (loss, grad_w), grad_x = _jax.lax.scan(body, init, (per_example, given["loss_target"]))
    with _jax.named_scope("update"):
        delta_w, new_m, new_v = {}, {}, {}
        for n in TWIN_WEIGHTS:
            delta_w[n], new_m[n], new_v[n] = _adamw(weights[n], grad_w[n], given["m_" + n], given["v_" + n])
    return (loss, grad_x, *[grad_w[n] for n in TWIN_WEIGHTS], *[delta_w[n] for n in TWIN_WEIGHTS],
            *[new_m[n] for n in TWIN_WEIGHTS], *[new_v[n] for n in TWIN_WEIGHTS])
```

```python
import functools
import math

import jax
import jax.numpy as jnp
from jax import lax
from jax.experimental import pallas as pl
from jax.experimental.pallas import tpu as pltpu

F32 = jnp.float32
BF16 = jnp.bfloat16

HEAD_DIM = 64
CHUNK = 64
LEFT_CHUNKS = 8
REL_CLIP = 256
N_REL = 2 * REL_CLIP + 1
WIDTH = 512
LANES = 128
A_TQ = 256
A_WIN = A_TQ + LEFT_CHUNKS * CHUNK
B_TQ = 512
B_TS = 256
LN_EPS = 1e-5
QK_SCALE = 1.0 / math.sqrt(HEAD_DIM)
NEG = -1e30

ADAM_LR = 0.001
ADAM_B1 = 0.9
ADAM_B2 = 0.999
ADAM_EPS = 1e-08
ADAM_WD = 0.01
ADAM_STEP = 10

N_DEV = 8
MESH = pl.DeviceIdType.MESH
MIB = 1024 * 1024


def _cparams(sem=None, vmem_mib=48):
    return pltpu.CompilerParams(dimension_semantics=sem, vmem_limit_bytes=vmem_mib * MIB)


def _dot(a, b):
    return jnp.dot(a, b, preferred_element_type=F32)


def _dot_nt(a, b):
    return lax.dot_general(a, b, (((1,), (1,)), ((), ())), preferred_element_type=F32)


def _dot_tn(a, b):
    return lax.dot_general(a, b, (((0,), (0,)), ((), ())), preferred_element_type=F32)


def _tile(n, pref):
    if n <= pref:
        return n
    for t in range(pref - pref % 8, 0, -8):
        if n % t == 0:
            return t
    raise ValueError((n, pref))


def _mm_nn(a, w, layer, *, col_off, n_cols, out_dtype, name):
    M, K = a.shape
    tm = _tile(M, 1024)
    tn = _tile(n_cols, 512)
    assert col_off % tn == 0
    off = col_off // tn

    def body(a_ref, w_ref, o_ref):
        o_ref[...] = _dot(a_ref[...].astype(BF16), w_ref[...]).astype(out_dtype)

    return pl.pallas_call(
        body, name=name, grid=(M // tm, n_cols // tn),
        in_specs=[pl.BlockSpec((tm, K), lambda i, j: (i, 0)),
                  pl.BlockSpec((None, K, tn), lambda i, j: (layer, 0, j + off))],
        out_specs=pl.BlockSpec((tm, tn), lambda i, j: (i, j)),
        out_shape=jax.ShapeDtypeStruct((M, n_cols), out_dtype),
        compiler_params=_cparams(("parallel", "parallel")),
    )(a, w)


def _mm_nt_add(a, w, layer, add, add_scale, *, name):
    M, K = a.shape
    N = w.shape[1]
    tm = _tile(M, 1024)
    tk = _tile(K, 1024)

    def body(a_ref, w_ref, add_ref, o_ref):
        @pl.when(pl.program_id(1) == 0)
        def _():
            o_ref[...] = add_scale * add_ref[...]
        o_ref[...] += _dot_nt(a_ref[...], w_ref[...])

    return pl.pallas_call(
        body, name=name, grid=(M // tm, K // tk),
        in_specs=[pl.BlockSpec((tm, tk), lambda i, k: (i, k)),
                  pl.BlockSpec((None, N, tk), lambda i, k: (layer, 0, k)),
                  pl.BlockSpec((tm, N), lambda i, k: (i, 0))],
        out_specs=pl.BlockSpec((tm, N), lambda i, k: (i, 0)),
        out_shape=jax.ShapeDtypeStruct((M, N), F32),
        compiler_params=_cparams(("parallel", "arbitrary")),
    )(a, w, add)


def _mm_tn(a, b, *, tm, tn, name):
    T, M = a.shape
    N = b.shape[1]
    tk = _tile(T, 512)

    def body(a_ref, b_ref, o_ref):
        @pl.when(pl.program_id(2) == 0)
        def _():
            o_ref[...] = jnp.zeros_like(o_ref)
        o_ref[...] += _dot_tn(a_ref[...].astype(BF16), b_ref[...].astype(BF16))

    return pl.pallas_call(
        body, name=name, grid=(M // tm, N // tn, T // tk),
        in_specs=[pl.BlockSpec((tk, tm), lambda i, j, k: (k, i)),
                  pl.BlockSpec((tk, tn), lambda i, j, k: (k, j))],
        out_specs=pl.BlockSpec((tm, tn), lambda i, j, k: (i, j)),
        out_shape=jax.ShapeDtypeStruct((M, N), F32),
        compiler_params=_cparams(("parallel", "parallel", "arbitrary")),
    )(a, b)


def _mm_tn_blocked(a, b, *, name):
    T, M = a.shape
    S, _, N = b.shape
    tk = _tile(T, 512)

    def body(a_ref, b_ref, o_ref):
        @pl.when(pl.program_id(1) == 0)
        def _():
            o_ref[...] = jnp.zeros_like(o_ref)
        o_ref[...] += _dot_tn(a_ref[...].astype(BF16), b_ref[...].astype(BF16))

    return pl.pallas_call(
        body, name=name, grid=(S, T // tk),
        in_specs=[pl.BlockSpec((tk, M), lambda s, k: (k, 0)),
                  pl.BlockSpec((None, tk, N), lambda s, k: (s, k, 0))],
        out_specs=pl.BlockSpec((None, M, N), lambda s, k: (s, 0, 0)),
        out_shape=jax.ShapeDtypeStruct((S, M, N), F32),
        compiler_params=_cparams(("parallel", "arbitrary")),
    )(a, b)


def _mm_tn_blocked_a(a, b, *, name):
    S, T, M = a.shape
    N = b.shape[1]
    tk = _tile(T, 512)

    def body(a_ref, b_ref, o_ref):
        @pl.when(pl.program_id(1) == 0)
        def _():
            o_ref[...] = jnp.zeros_like(o_ref)
        o_ref[...] += _dot_tn(a_ref[...].astype(BF16), b_ref[...].astype(BF16))

    return pl.pallas_call(
        body, name=name, grid=(S, T // tk),
        in_specs=[pl.BlockSpec((None, tk, M), lambda s, k: (s, k, 0)),
                  pl.BlockSpec((tk, N), lambda s, k: (k, 0))],
        out_specs=pl.BlockSpec((None, M, N), lambda s, k: (s, 0, 0)),
        out_shape=jax.ShapeDtypeStruct((S, M, N), F32),
        compiler_params=_cparams(("parallel", "arbitrary")),
    )(a, b)


def _ln_fwd(r, g, b):
    mu = jnp.mean(r, axis=-1, keepdims=True)
    xc = r - mu
    var = jnp.mean(xc * xc, axis=-1, keepdims=True)
    return xc * lax.rsqrt(var + LN_EPS) * g + b


def _ln_bwd(dy, r, g):
    mu = jnp.mean(r, axis=-1, keepdims=True)
    xc = r - mu
    var = jnp.mean(xc * xc, axis=-1, keepdims=True)
    rstd = lax.rsqrt(var + LN_EPS)
    xhat = xc * rstd
    dxh = dy * g
    m1 = jnp.mean(dxh, axis=-1, keepdims=True)
    m2 = jnp.mean(dxh * xhat, axis=-1, keepdims=True)
    return rstd * (dxh - m1 - xhat * m2), xhat


def _lane_is_head0():
    return lax.broadcasted_iota(jnp.int32, (1, LANES), 1) < HEAD_DIM


def _band_valid(i):
    r = lax.broadcasted_iota(jnp.int32, (A_TQ, A_WIN), 0)
    c = lax.broadcasted_iota(jnp.int32, (A_TQ, A_WIN), 1)
    a = r // CHUNK
    b = c // CHUNK
    return (b >= a) & (b <= a + LEFT_CHUNKS) & (c + i * A_TQ >= LEFT_CHUNKS * CHUNK)


def _band_probs(q2, k2, bias, valid, head0, h):
    qh = jnp.where(head0 if h == 0 else jnp.logical_not(head0), q2, jnp.zeros_like(q2))
    s = _dot_nt(qh, k2) * QK_SCALE + bias
    s = jnp.where(valid, s, NEG)
    m = jnp.max(s, axis=1, keepdims=True)
    e = jnp.exp(s - m)
    return e / jnp.sum(e, axis=1, keepdims=True)


def _attn_a_fwd(qkv, kvpad, bias, *, name):
    T = qkv.shape[0]
    n_hp = WIDTH // LANES

    def body(q_ref, k_ref, v_ref, bias_ref, o_ref):
        i = pl.program_id(1)
        row0 = pl.multiple_of(i * A_TQ, A_TQ)
        q2 = q_ref[...]
        k2 = k_ref[pl.ds(row0, A_WIN), :]
        v2 = v_ref[pl.ds(row0, A_WIN), :]
        head0 = _lane_is_head0()
        valid = _band_valid(i)
        outs = []
        for h in range(2):
            p = _band_probs(q2, k2, bias_ref[h], valid, head0, h)
            outs.append(_dot(p.astype(BF16), v2))
        o_ref[...] = jnp.where(head0, outs[0], outs[1]).astype(o_ref.dtype)

    return pl.pallas_call(
        body, name=name, grid=(n_hp, T // A_TQ),
        in_specs=[pl.BlockSpec((A_TQ, LANES), lambda hp, i: (i, hp)),
                  pl.BlockSpec((T + A_WIN - A_TQ, LANES), lambda hp, i: (0, hp)),
                  pl.BlockSpec((T + A_WIN - A_TQ, LANES), lambda hp, i: (0, hp + n_hp)),
                  pl.BlockSpec((2, A_TQ, A_WIN), lambda hp, i: (hp, 0, 0))],
        out_specs=pl.BlockSpec((A_TQ, LANES), lambda hp, i: (i, hp)),
        out_shape=jax.ShapeDtypeStruct((T, WIDTH), BF16),
        compiler_params=_cparams(("parallel", "arbitrary")),
    )(qkv, kvpad, kvpad, bias)


def _attn_a_bwd(qkv, kvpad, bias, do, *, name):
    T = qkv.shape[0]
    TP = T + A_WIN - A_TQ
    n_hp = WIDTH // LANES

    def body(q_ref, k_ref, v_ref, bias_ref, do_ref, dq_ref, dk_ref, dv_ref, db_ref):
        i = pl.program_id(1)

        @pl.when(i == 0)
        def _():
            dk_ref[...] = jnp.zeros_like(dk_ref)
            dv_ref[...] = jnp.zeros_like(dv_ref)
            db_ref[...] = jnp.zeros_like(db_ref)

        row0 = pl.multiple_of(i * A_TQ, A_TQ)
        q2 = q_ref[...]
        do2 = do_ref[...]
        k2 = k_ref[pl.ds(row0, A_WIN), :]
        v2 = v_ref[pl.ds(row0, A_WIN), :]
        head0 = _lane_is_head0()
        valid = _band_valid(i)
        dq, dk, dv = [], [], []
        for h in range(2):
            hm = head0 if h == 0 else jnp.logical_not(head0)
            p = _band_probs(q2, k2, bias_ref[h], valid, head0, h)
            doh = jnp.where(hm, do2, jnp.zeros_like(do2))
            dp = _dot_nt(doh, v2)
            delta = jnp.sum(p * dp, axis=1, keepdims=True)
            ds = p * (dp - delta)
            db_ref[h] += ds
            dsb = (ds * QK_SCALE).astype(BF16)
            dq.append(_dot(dsb, k2))
            dk.append(_dot_tn(dsb, q2))
            dv.append(_dot_tn(p.astype(BF16), do2))
        dq_ref[...] = jnp.where(head0, dq[0], dq[1]).astype(dq_ref.dtype)
        dk_ref[pl.ds(row0, A_WIN), :] += jnp.where(head0, dk[0], dk[1])
        dv_ref[pl.ds(row0, A_WIN), :] += jnp.where(head0, dv[0], dv[1])

    return pl.pallas_call(
        body, name=name, grid=(n_hp, T // A_TQ),
        in_specs=[pl.BlockSpec((A_TQ, LANES), lambda hp, i: (i, hp)),
                  pl.BlockSpec((TP, LANES), lambda hp, i: (0, hp)),
                  pl.BlockSpec((TP, LANES), lambda hp, i: (0, hp + n_hp)),
                  pl.BlockSpec((2, A_TQ, A_WIN), lambda hp, i: (hp, 0, 0)),
                  pl.BlockSpec((A_TQ, LANES), lambda hp, i: (i, hp))],
        out_specs=[pl.BlockSpec((A_TQ, LANES), lambda hp, i: (i, hp)),
                   pl.BlockSpec((TP, LANES), lambda hp, i: (0, hp)),
                   pl.BlockSpec((TP, LANES), lambda hp, i: (0, hp)),
                   pl.BlockSpec((2, A_TQ, A_WIN), lambda hp, i: (hp, 0, 0))],
        out_shape=[jax.ShapeDtypeStruct((T, WIDTH), BF16),
                   jax.ShapeDtypeStruct((TP, WIDTH), F32),
                   jax.ShapeDtypeStruct((TP, WIDTH), F32),
                   jax.ShapeDtypeStruct((WIDTH // HEAD_DIM, A_TQ, A_WIN), F32)],
        compiler_params=_cparams(("parallel", "arbitrary"), 56),
    )(qkv, kvpad, kvpad, bias, do)


def _toeplitz_bias(rb):
    H = rb.shape[0]
    span = A_TQ + A_WIN - 1
    n_tail = span - (N_REL - 1)
    ext = jnp.concatenate([rb[:, 1:], jnp.broadcast_to(rb[:, N_REL - 1:], (H, n_tail))], axis=1)
    rev = ext[:, ::-1]
    flat = jnp.broadcast_to(rev[:, None, :], (H, A_TQ, span)).reshape(H, A_TQ * span)
    skew = jnp.pad(flat, ((0, 0), (0, A_TQ))).reshape(H, A_TQ, span + 1)
    return skew[:, ::-1, :A_WIN]


def _toeplitz_bias_grad(db):
    H = db.shape[0]
    span = A_TQ + A_WIN - 1
    y = jnp.pad(db[:, ::-1, :], ((0, 0), (0, 0), (0, span + 1 - A_WIN)))
    z = y.reshape(H, A_TQ * (span + 1))[:, :A_TQ * span].reshape(H, A_TQ, span)
    g_ext = jnp.sum(z, axis=1)[:, ::-1]
    last = g_ext[:, N_REL - 2] + jnp.sum(g_ext[:, N_REL - 1:], axis=1)
    return jnp.concatenate([jnp.zeros((H, 1), F32), g_ext[:, :N_REL - 2], last[:, None]], axis=1)


def _split_bf16(x):
    hi = x.astype(BF16)
    lo = (x - hi.astype(F32)).astype(BF16)
    return hi, lo


def _sb_logits(qh, k2, causal, tri_strict, carry):
    z = _dot_nt(qh, k2) * QK_SCALE
    e = jnp.exp(-jnp.abs(z))
    lp = jnp.log(1.0 + e)
    log_beta = jnp.minimum(z, 0.0) - lp
    log_keep = jnp.where(causal, -jnp.maximum(z, 0.0) - lp, 0.0)
    hi, lo = _split_bf16(log_keep)
    suffix = _dot(hi, tri_strict) + _dot(lo, tri_strict)
    return z, e, log_beta + suffix + carry, log_keep


def _sb_causal(i, kb):
    t = i * B_TQ + lax.broadcasted_iota(jnp.int32, (B_TQ, B_TS), 0)
    s = kb * B_TS + lax.broadcasted_iota(jnp.int32, (B_TQ, B_TS), 1)
    return s < t


def _tri(strict):
    j = lax.broadcasted_iota(jnp.int32, (B_TS, B_TS), 0)
    s = lax.broadcasted_iota(jnp.int32, (B_TS, B_TS), 1)
    return jnp.where(j > s if strict else j >= s, 1.0, 0.0).astype(BF16)


def _attn_b_fwd(qkv, *, col0, name):
    T = qkv.shape[0]
    n_hp = WIDTH // LANES
    sub = B_TQ // B_TS

    def body(q_ref, k_ref, v_ref, o_ref, acc_ref, car_ref):
        i = pl.program_id(1)
        q2 = q_ref[...]
        head0 = _lane_is_head0()
        tri_s = _tri(True)
        acc_ref[...] = jnp.zeros_like(acc_ref)
        car_ref[...] = jnp.zeros_like(car_ref)
        n_sub = (i + 1) * sub

        def step(n, carry):
            kb = n_sub - 1 - n
            k0 = pl.multiple_of(kb * B_TS, B_TS)
            k2 = k_ref[pl.ds(k0, B_TS), :]
            v2 = v_ref[pl.ds(k0, B_TS), :]
            causal = _sb_causal(i, kb)
            for h in range(2):
                hm = head0 if h == 0 else jnp.logical_not(head0)
                qh = jnp.where(hm, q2, jnp.zeros_like(q2))
                _, _, logw, log_keep = _sb_logits(qh, k2, causal, tri_s, car_ref[h])
                w = jnp.where(causal, jnp.exp(logw), 0.0)
                acc_ref[h] += _dot(w.astype(BF16), v2)
                car_ref[h] += jnp.sum(log_keep, axis=1, keepdims=True)
            return carry

        lax.fori_loop(0, n_sub, step, 0)
        o_ref[...] = jnp.where(head0, acc_ref[0], acc_ref[1])

    return pl.pallas_call(
        body, name=name, grid=(n_hp, T // B_TQ),
        in_specs=[pl.BlockSpec((B_TQ, LANES), lambda hp, i: (i, hp + col0)),
                  pl.BlockSpec((T, LANES), lambda hp, i: (0, hp + col0 + n_hp)),
                  pl.BlockSpec((T, LANES), lambda hp, i: (0, hp + col0 + 2 * n_hp))],
        out_specs=pl.BlockSpec((B_TQ, LANES), lambda hp, i: (i, hp)),
        out_shape=jax.ShapeDtypeStruct((T, WIDTH), F32),
        scratch_shapes=[pltpu.VMEM((2, B_TQ, LANES), F32), pltpu.VMEM((2, B_TQ, 1), F32)],
        compiler_params=_cparams(("parallel", "arbitrary")),
    )(qkv, qkv, qkv)


def _attn_b_bwd(qkv, out, do, *, col0, name):
    T = qkv.shape[0]
    n_hp = WIDTH // LANES
    sub = B_TQ // B_TS

    def body(q_ref, k_ref, v_ref, o_ref, do_ref, dq_ref, dk_ref, dv_ref, dqa_ref, car_ref, carr_ref, tot_ref):
        i = pl.program_id(1)

        @pl.when(i == 0)
        def _():
            dk_ref[...] = jnp.zeros_like(dk_ref)
            dv_ref[...] = jnp.zeros_like(dv_ref)

        q2 = q_ref[...]
        do2 = do_ref[...]
        head0 = _lane_is_head0()
        tri_s = _tri(True)
        tri_i = _tri(False)
        prod = do2.astype(F32) * o_ref[...]
        tot_ref[0] = jnp.sum(jnp.where(head0, prod, 0.0), axis=1, keepdims=True)
        tot_ref[1] = jnp.sum(jnp.where(head0, 0.0, prod), axis=1, keepdims=True)
        dqa_ref[...] = jnp.zeros_like(dqa_ref)
        car_ref[...] = jnp.zeros_like(car_ref)
        carr_ref[...] = jnp.zeros_like(carr_ref)
        n_sub = (i + 1) * sub

        def step(n, carry):
            kb = n_sub - 1 - n
            k0 = pl.multiple_of(kb * B_TS, B_TS)
            k2 = k_ref[pl.ds(k0, B_TS), :]
            v2 = v_ref[pl.ds(k0, B_TS), :]
            causal = _sb_causal(i, kb)
            dk, dv = [], []
            for h in range(2):
                hm = head0 if h == 0 else jnp.logical_not(head0)
                qh = jnp.where(hm, q2, jnp.zeros_like(q2))
                doh = jnp.where(hm, do2, jnp.zeros_like(do2))
                z, e, logw, log_keep = _sb_logits(qh, k2, causal, tri_s, car_ref[h])
                wb = jnp.where(causal, jnp.exp(logw), 0.0).astype(BF16)
                dlog = wb.astype(F32) * _dot_nt(doh, v2)
                hi, lo = _split_bf16(dlog)
                later = _dot(hi, tri_i) + _dot(lo, tri_i) + carr_ref[h]
                earlier = tot_ref[h] - later
                r = 1.0 / (1.0 + e)
                er = e * r
                beta = jnp.where(z >= 0.0, r, er)
                one_m_beta = jnp.where(z >= 0.0, er, r)
                dz = jnp.where(causal, dlog * one_m_beta - beta * earlier, 0.0)
                dzb = (dz * QK_SCALE).astype(BF16)
                dqa_ref[h] += _dot(dzb, k2)
                dk.append(_dot_tn(dzb, q2))
                dv.append(_dot_tn(wb, do2))
                car_ref[h] += jnp.sum(log_keep, axis=1, keepdims=True)
                carr_ref[h] += jnp.sum(dlog, axis=1, keepdims=True)
            dk_ref[pl.ds(k0, B_TS), :] += jnp.where(head0, dk[0], dk[1])
            dv_ref[pl.ds(k0, B_TS), :] += jnp.where(head0, dv[0], dv[1])
            return carry

        lax.fori_loop(0, n_sub, step, 0)
        dq_ref[...] = jnp.where(head0, dqa_ref[0], dqa_ref[1]).astype(dq_ref.dtype)

    return pl.pallas_call(
        body, name=name, grid=(n_hp, T // B_TQ),
        in_specs=[pl.BlockSpec((B_TQ, LANES), lambda hp, i: (i, hp + col0)),
                  pl.BlockSpec((T, LANES), lambda hp, i: (0, hp + col0 + n_hp)),
                  pl.BlockSpec((T, LANES), lambda hp, i: (0, hp + col0 + 2 * n_hp)),
                  pl.BlockSpec((B_TQ, LANES), lambda hp, i: (i, hp)),
                  pl.BlockSpec((B_TQ, LANES), lambda hp, i: (i, hp))],
        out_specs=[pl.BlockSpec((B_TQ, LANES), lambda hp, i: (i, hp)),
                   pl.BlockSpec((T, LANES), lambda hp, i: (0, hp)),
                   pl.BlockSpec((T, LANES), lambda hp, i: (0, hp))],
        out_shape=[jax.ShapeDtypeStruct((T, WIDTH), BF16),
                   jax.ShapeDtypeStruct((T, WIDTH), F32),
                   jax.ShapeDtypeStruct((T, WIDTH), F32)],
        scratch_shapes=[pltpu.VMEM((2, B_TQ, LANES), F32), pltpu.VMEM((2, B_TQ, 1), F32),
                        pltpu.VMEM((2, B_TQ, 1), F32), pltpu.VMEM((2, B_TQ, 1), F32)],
        compiler_params=_cparams(("parallel", "arbitrary"), 56),
    )(qkv, qkv, qkv, out, do)


def _gated_mix(oa_ref, ob_ref, g_ref, bg_ref, wpa_ref, wpb_ref, D):
    ya = _dot(oa_ref[...].astype(BF16), wpa_ref[...])
    yb = _dot(ob_ref[...].astype(BF16), wpb_ref[...])
    sa = jax.nn.sigmoid(g_ref[:, :D] + bg_ref[:, :D])
    sb = jax.nn.sigmoid(g_ref[:, D:] + bg_ref[:, D:])
    return ya, yb, sa, sb


def _proj_fwd(oa, ob, g, bg, wpa, wpb, wo, xin, lng, lnb, layer, *, alpha, name):
    T, D = xin.shape
    tm = _tile(T, 512)
    row = lambda i: (i, 0)
    wspec = lambda r, c: pl.BlockSpec((None, r, c), lambda i: (layer, 0, 0))
    vec = lambda c: pl.BlockSpec((None, 1, c), lambda i: (layer, 0, 0))

    def body(oa_ref, ob_ref, g_ref, bg_ref, wpa_ref, wpb_ref, wo_ref, x_ref, lg_ref, lb_ref, x1_ref, r1_ref):
        ya, yb, sa, sb = _gated_mix(oa_ref, ob_ref, g_ref, bg_ref, wpa_ref, wpb_ref, D)
        mix = _dot((sa * ya + sb * yb).astype(BF16), wo_ref[...])
        r1 = alpha * x_ref[...] + mix
        r1_ref[...] = r1
        x1_ref[...] = _ln_fwd(r1, lg_ref[...], lb_ref[...])

    return pl.pallas_call(
        body, name=name, grid=(T // tm,),
        in_specs=[pl.BlockSpec((tm, WIDTH), row), pl.BlockSpec((tm, WIDTH), row), pl.BlockSpec((tm, 2 * D), row),
                  vec(2 * D), wspec(WIDTH, D), wspec(WIDTH, D), wspec(D, D),
                  pl.BlockSpec((tm, D), row), vec(D), vec(D)],
        out_specs=[pl.BlockSpec((tm, D), row), pl.BlockSpec((tm, D), row)],
        out_shape=[jax.ShapeDtypeStruct((T, D), F32), jax.ShapeDtypeStruct((T, D), F32)],
        compiler_params=_cparams(("arbitrary",), 56),
    )(oa, ob, g, bg, wpa, wpb, wo, xin, lng, lnb)


def _proj_bwd(dx1, r1, lng, oa, ob, g, bg, wpa, wpb, wo, layer, *, name):
    T, D = dx1.shape
    tm = _tile(T, 512)
    row = lambda i: (i, 0)
    fixed = lambda i: (0, 0)
    wspec = lambda r, c: pl.BlockSpec((None, r, c), lambda i: (layer, 0, 0))
    vec = lambda c: pl.BlockSpec((None, 1, c), lambda i: (layer, 0, 0))

    def body(dx_ref, r1_ref, lg_ref, oa_ref, ob_ref, g_ref, bg_ref, wpa_ref, wpb_ref, wo_ref,
             dr_ref, mix_ref, dya_ref, dyb_ref, dg_ref, doa_ref, dob_ref, dlg_ref, dlb_ref, dbg_ref):
        @pl.when(pl.program_id(0) == 0)
        def _():
            dlg_ref[...] = jnp.zeros_like(dlg_ref)
            dlb_ref[...] = jnp.zeros_like(dlb_ref)
            dbg_ref[...] = jnp.zeros_like(dbg_ref)

        dx = dx_ref[...]
        dr, xhat = _ln_bwd(dx, r1_ref[...], lg_ref[...])
        dr_ref[...] = dr
        dlg_ref[...] += jnp.sum(dx * xhat, axis=0, keepdims=True)
        dlb_ref[...] += jnp.sum(dx, axis=0, keepdims=True)
        dmix = _dot_nt(dr.astype(BF16), wo_ref[...])
        ya, yb, sa, sb = _gated_mix(oa_ref, ob_ref, g_ref, bg_ref, wpa_ref, wpb_ref, D)
        mix_ref[...] = (sa * ya + sb * yb).astype(BF16)
        dya = (dmix * sa).astype(BF16)
        dyb = (dmix * sb).astype(BF16)
        dya_ref[...] = dya
        dyb_ref[...] = dyb
        dga = dmix * ya * (sa * (1.0 - sa))
        dgb = dmix * yb * (sb * (1.0 - sb))
        dg_ref[:, :D] = dga.astype(BF16)
        dg_ref[:, D:] = dgb.astype(BF16)
        dbg_ref[:, :D] += jnp.sum(dga, axis=0, keepdims=True)
        dbg_ref[:, D:] += jnp.sum(dgb, axis=0, keepdims=True)
        doa_ref[...] = _dot_nt(dya, wpa_ref[...]).astype(BF16)
        dob_ref[...] = _dot_nt(dyb, wpb_ref[...]).astype(BF16)

    return pl.pallas_call(
        body, name=name, grid=(T // tm,),
        in_specs=[pl.BlockSpec((tm, D), row), pl.BlockSpec((tm, D), row), vec(D),
                  pl.BlockSpec((tm, WIDTH), row), pl.BlockSpec((tm, WIDTH), row), pl.BlockSpec((tm, 2 * D), row),
                  vec(2 * D), wspec(WIDTH, D), wspec(WIDTH, D), wspec(D, D)],
        out_specs=[pl.BlockSpec((tm, D), row), pl.BlockSpec((tm, D), row), pl.BlockSpec((tm, D), row),
                   pl.BlockSpec((tm, D), row), pl.BlockSpec((tm, 2 * D), row),
                   pl.BlockSpec((tm, WIDTH), row), pl.BlockSpec((tm, WIDTH), row),
                   pl.BlockSpec((1, D), fixed), pl.BlockSpec((1, D), fixed), pl.BlockSpec((1, 2 * D), fixed)],
        out_shape=[jax.ShapeDtypeStruct((T, D), F32), jax.ShapeDtypeStruct((T, D), BF16),
                   jax.ShapeDtypeStruct((T, D), BF16), jax.ShapeDtypeStruct((T, D), BF16),
                   jax.ShapeDtypeStruct((T, 2 * D), BF16),
                   jax.ShapeDtypeStruct((T, WIDTH), BF16), jax.ShapeDtypeStruct((T, WIDTH), BF16),
                   jax.ShapeDtypeStruct((1, D), F32), jax.ShapeDtypeStruct((1, D), F32),
                   jax.ShapeDtypeStruct((1, 2 * D), F32)],
        compiler_params=_cparams(("arbitrary",), 56),
    )(dx1, r1, lng, oa, ob, g, bg, wpa, wpb, wo)


def _ffn_fwd(x1, wfi, wfo, lng, lnb, layer, *, alpha, name):
    T, D = x1.shape
    tf = wfi.shape[-1]
    nj = wfi.shape[0] // 2
    tm = _tile(T, 512)
    vec = lambda c: pl.BlockSpec((None, 1, c), lambda i, j: (layer, 0, 0))

    def body(x_ref, wg_ref, wu_ref, wo_ref, lg_ref, lb_ref, gs_ref, us_ref, r2_ref, x2_ref, acc_ref, xb_ref):
        j = pl.program_id(1)

        @pl.when(j == 0)
        def _():
            xb_ref[...] = x_ref[...].astype(BF16)
            acc_ref[...] = jnp.zeros_like(acc_ref)

        gv = _dot(xb_ref[...], wg_ref[...])
        uv = _dot(xb_ref[...], wu_ref[...])
        gs_ref[...] = gv
        us_ref[...] = uv
        act = gv * jax.nn.sigmoid(gv) * uv
        acc_ref[...] += _dot(act.astype(BF16), wo_ref[...])

        @pl.when(j == nj - 1)
        def _():
            r2 = alpha * x_ref[...] + acc_ref[...]
            r2_ref[...] = r2
            x2_ref[...] = _ln_fwd(r2, lg_ref[...], lb_ref[...])

    return pl.pallas_call(
        body, name=name, grid=(T // tm, nj),
        in_specs=[pl.BlockSpec((tm, D), lambda i, j: (i, 0)),
                  pl.BlockSpec((None, None, D, tf), lambda i, j: (j, layer, 0, 0)),
                  pl.BlockSpec((None, None, D, tf), lambda i, j: (j + nj, layer, 0, 0)),
                  pl.BlockSpec((None, tf, D), lambda i, j: (layer, j, 0)),
                  vec(D), vec(D)],
        out_specs=[pl.BlockSpec((None, tm, tf), lambda i, j: (j, i, 0)),
                   pl.BlockSpec((None, tm, tf), lambda i, j: (j, i, 0)),
                   pl.BlockSpec((tm, D), lambda i, j: (i, 0)),
                   pl.BlockSpec((tm, D), lambda i, j: (i, 0))],
        out_shape=[jax.ShapeDtypeStruct((nj, T, tf), F32), jax.ShapeDtypeStruct((nj, T, tf), F32),
                   jax.ShapeDtypeStruct((T, D), F32), jax.ShapeDtypeStruct((T, D), F32)],
        scratch_shapes=[pltpu.VMEM((tm, D), F32), pltpu.VMEM((tm, D), BF16)],
        compiler_params=_cparams(("parallel", "arbitrary"), 56),
    )(x1, wfi, wfi, wfo, lng, lnb)


def _ffn_bwd(dx2, r2, lng, gs, us, wfi, wfo, layer, *, alpha, name):
    T, D = dx2.shape
    tf = wfi.shape[-1]
    nj = wfi.shape[0] // 2
    tm = _tile(T, 512)
    vec = lambda c: pl.BlockSpec((None, 1, c), lambda i, j: (layer, 0, 0))
    blk = lambda: pl.BlockSpec((None, tm, tf), lambda i, j: (j, i, 0))

    def body(dx_ref, r2_ref, lg_ref, gs_ref, us_ref, wg_ref, wu_ref, wo_ref,
             dr_ref, act_ref, dg_ref, du_ref, dx1_ref, dlg_ref, dlb_ref, acc_ref, drb_ref):
        i = pl.program_id(0)
        j = pl.program_id(1)

        @pl.when((i == 0) & (j == 0))
        def _():
            dlg_ref[...] = jnp.zeros_like(dlg_ref)
            dlb_ref[...] = jnp.zeros_like(dlb_ref)

        @pl.when(j == 0)
        def _():
            dx = dx_ref[...]
            dr, xhat = _ln_bwd(dx, r2_ref[...], lg_ref[...])
            dlg_ref[...] += jnp.sum(dx * xhat, axis=0, keepdims=True)
            dlb_ref[...] += jnp.sum(dx, axis=0, keepdims=True)
            drb_ref[...] = dr.astype(BF16)
            dr_ref[...] = dr.astype(BF16)
            acc_ref[...] = alpha * dr

        dact = _dot_nt(drb_ref[...], wo_ref[...])
        gv = gs_ref[...]
        uv = us_ref[...]
        s = jax.nn.sigmoid(gv)
        silu = gv * s
        act_ref[...] = (silu * uv).astype(BF16)
        dg = (dact * uv * (s * (1.0 + gv * (1.0 - s)))).astype(BF16)
        du = (dact * silu).astype(BF16)
        dg_ref[...] = dg
        du_ref[...] = du
        acc_ref[...] += _dot_nt(dg, wg_ref[...]) + _dot_nt(du, wu_ref[...])

        @pl.when(j == nj - 1)
        def _():
            dx1_ref[...] = acc_ref[...]

    return pl.pallas_call(
        body, name=name, grid=(T // tm, nj),
        in_specs=[pl.BlockSpec((tm, D), lambda i, j: (i, 0)), pl.BlockSpec((tm, D), lambda i, j: (i, 0)), vec(D),
                  blk(), blk(),
                  pl.BlockSpec((None, None, D, tf), lambda i, j: (j, layer, 0, 0)),
                  pl.BlockSpec((None, None, D, tf), lambda i, j: (j + nj, layer, 0, 0)),
                  pl.BlockSpec((None, tf, D), lambda i, j: (layer, j, 0))],
        out_specs=[pl.BlockSpec((tm, D), lambda i, j: (i, 0)), blk(), blk(), blk(),
                   pl.BlockSpec((tm, D), lambda i, j: (i, 0)),
                   pl.BlockSpec((1, D), lambda i, j: (0, 0)), pl.BlockSpec((1, D), lambda i, j: (0, 0))],
        out_shape=[jax.ShapeDtypeStruct((T, D), BF16),
                   jax.ShapeDtypeStruct((nj, T, tf), BF16), jax.ShapeDtypeStruct((nj, T, tf), BF16),
                   jax.ShapeDtypeStruct((nj, T, tf), BF16),
                   jax.ShapeDtypeStruct((T, D), F32),
                   jax.ShapeDtypeStruct((1, D), F32), jax.ShapeDtypeStruct((1, D), F32)],
        scratch_shapes=[pltpu.VMEM((tm, D), F32), pltpu.VMEM((tm, D), BF16)],
        compiler_params=_cparams(("arbitrary", "arbitrary"), 56),
    )(dx2, r2, lng, gs, us, wfi, wfi, wfo)


def _loss_head(y, target, *, name):
    T, D = y.shape
    tm = _tile(T, 1024)

    def body(y_ref, t_ref, dy_ref, sq_ref):
        @pl.when(pl.program_id(0) == 0)
        def _():
            sq_ref[...] = jnp.zeros_like(sq_ref)
        err = y_ref[...] - t_ref[...]
        dy_ref[...] = err * (1.0 / D)
        sq_ref[...] += jnp.sum(err * err, axis=0, keepdims=True)

    return pl.pallas_call(
        body, name=name, grid=(T // tm,),
        in_specs=[pl.BlockSpec((tm, D), lambda i: (i, 0)), pl.BlockSpec((tm, D), lambda i: (i, 0))],
        out_specs=[pl.BlockSpec((tm, D), lambda i: (i, 0)), pl.BlockSpec((1, D), lambda i: (0, 0))],
        out_shape=[jax.ShapeDtypeStruct((T, D), F32), jax.ShapeDtypeStruct((1, D), F32)],
        compiler_params=_cparams(("arbitrary",)),
    )(y, target)


def _my_place():
    return lax.axis_index("x"), lax.axis_index("y"), lax.axis_index("c")


def _peer(place, k):
    x, y, c = place
    return (1 - x if k & 4 else x, 1 - y if k & 2 else y, 1 - c if k & 1 else c)


def _logical(place):
    x, y, c = place
    return 4 * x + 2 * y + c


def _block_of(ref, mode, idx):
    if mode == "blk":
        return ref.at[idx]
    if mode == "col":
        size = ref.shape[2] // N_DEV
        return ref.at[:, :, pl.ds(pl.multiple_of(idx * size, size), size)]
    size = ref.shape[1] // N_DEV
    return ref.at[:, pl.ds(pl.multiple_of(idx * size, size), size), :]


def _full_shape(shard, mode):
    if mode == "blk":
        return (N_DEV,) + shard.shape
    if mode == "col":
        return shard.shape[:2] + (N_DEV * shard.shape[2],)
    return (shard.shape[0], N_DEV * shard.shape[1], shard.shape[2])


def _all_gather(shards, modes, *, name):
    n = len(shards)
    hbm = pl.BlockSpec(memory_space=pltpu.HBM)

    def body(*refs):
        ins, outs = refs[:n], refs[n:2 * n]
        send, recv, local = refs[2 * n:]
        me = _my_place()
        my_id = _logical(me)
        copies = []
        for a in range(n):
            mine = pltpu.make_async_copy(ins[a], _block_of(outs[a], modes[a], my_id), local.at[a])
            mine.start()
            copies.append(mine)
            for k in range(1, N_DEV):
                cp = pltpu.make_async_remote_copy(
                    src_ref=ins[a], dst_ref=_block_of(outs[a], modes[a], my_id),
                    send_sem=send.at[a * N_DEV + k], recv_sem=recv.at[a * N_DEV + k],
                    device_id=_peer(me, k), device_id_type=MESH)
                cp.start()
                copies.append(cp)
        for cp in copies:
            cp.wait()

    return pl.pallas_call(
        body, name=name,
        in_specs=[hbm] * n, out_specs=[hbm] * n,
        out_shape=[jax.ShapeDtypeStruct(_full_shape(s, m), s.dtype) for s, m in zip(shards, modes)],
        scratch_shapes=[pltpu.SemaphoreType.DMA((n * N_DEV,)), pltpu.SemaphoreType.DMA((n * N_DEV,)),
                        pltpu.SemaphoreType.DMA((n,))],
    )(*shards)


def _grad_exchange(grads, modes, n_layers, *, name):
    flat = [g for per_w in grads for g in per_w]
    n_w = len(grads)
    hbm = pl.BlockSpec(memory_space=pltpu.HBM)

    def shard_shape(g, mode):
        if mode == "blk":
            return g.shape[1:]
        if mode == "col":
            return (g.shape[0], g.shape[1] // N_DEV)
        return (g.shape[0] // N_DEV, g.shape[1])

    def block(ref, mode, idx):
        if mode == "blk":
            return ref.at[idx]
        if mode == "col":
            size = ref.shape[1] // N_DEV
            return ref.at[:, pl.ds(pl.multiple_of(idx * size, size), size)]
        size = ref.shape[0] // N_DEV
        return ref.at[pl.ds(pl.multiple_of(idx * size, size), size), :]

    def body(*refs):
        ins = refs[:len(flat)]
        outs = refs[len(flat):len(flat) + n_w]
        send, recv, local = refs[len(flat) + n_w:]
        me = _my_place()
        my_id = _logical(me)
        copies = []
        for w in range(n_w):
            for l in range(n_layers):
                a = w * n_layers + l
                src = ins[a]
                dst = outs[w].at[my_id, l]
                mine = pltpu.make_async_copy(block(src, modes[w], my_id), dst, local.at[a])
                mine.start()
                copies.append(mine)
                for k in range(1, N_DEV):
                    peer = _peer(me, k)
                    cp = pltpu.make_async_remote_copy(
                        src_ref=block(src, modes[w], _logical(peer)), dst_ref=dst,
                        send_sem=send.at[a * N_DEV + k], recv_sem=recv.at[a * N_DEV + k],
                        device_id=peer, device_id_type=MESH)
                    cp.start()
                    copies.append(cp)
        for cp in copies:
            cp.wait()

    return pl.pallas_call(
        body, name=name,
        in_specs=[hbm] * len(flat), out_specs=[hbm] * n_w,
        out_shape=[jax.ShapeDtypeStruct((N_DEV, n_layers) + shard_shape(per_w[0], m), F32)
                   for per_w, m in zip(grads, modes)],
        scratch_shapes=[pltpu.SemaphoreType.DMA((len(flat) * N_DEV,)), pltpu.SemaphoreType.DMA((len(flat) * N_DEV,)),
                        pltpu.SemaphoreType.DMA((len(flat),))],
    )(*flat)


def _adamw(w, g, m, v):
    m = ADAM_B1 * m + (1.0 - ADAM_B1) * g
    v = ADAM_B2 * v + (1.0 - ADAM_B2) * (g * g)
    m_hat = m / (1.0 - ADAM_B1 ** ADAM_STEP)
    v_hat = v / (1.0 - ADAM_B2 ** ADAM_STEP)
    delta = -ADAM_LR * (m_hat / (jnp.sqrt(v_hat) + ADAM_EPS) + ADAM_WD * w)
    return delta, m, v


def _sum_slots_adamw(slots, w, m, v, *, name):
    R, C = w.shape
    tr = _tile(R, 256)

    def body(s_ref, w_ref, m_ref, v_ref, g_out, d_out, m_out, v_out):
        g = s_ref[0]
        for s in range(1, N_DEV):
            g = g + s_ref[s]
        delta, m_new, v_new = _adamw(w_ref[...], g, m_ref[...], v_ref[...])
        g_out[...] = g
        d_out[...] = delta
        m_out[...] = m_new
        v_out[...] = v_new

    spec = pl.BlockSpec((tr, C), lambda i: (i, 0))
    return pl.pallas_call(
        body, name=name, grid=(R // tr,),
        in_specs=[pl.BlockSpec((N_DEV, tr, C), lambda i: (0, i, 0)), spec, spec, spec],
        out_specs=[spec] * 4,
        out_shape=[jax.ShapeDtypeStruct((R, C), F32)] * 4,
        compiler_params=_cparams(("parallel",)),
    )(slots, w, m, v)


def _small_allreduce_adamw(g, w, m, v, *, name):
    R = g.shape[0]
    vmem = pl.BlockSpec(memory_space=pltpu.VMEM)

    def body(g_ref, w_ref, m_ref, v_ref, g_out, d_out, m_out, v_out, slots, send, recv):
        me = _my_place()
        my_id = _logical(me)
        slots[my_id] = g_ref[...]
        copies = []
        for k in range(1, N_DEV):
            cp = pltpu.make_async_remote_copy(
                src_ref=g_ref, dst_ref=slots.at[my_id], send_sem=send.at[k], recv_sem=recv.at[k],
                device_id=_peer(me, k), device_id_type=MESH)
            cp.start()
            copies.append(cp)
        for cp in copies:
            cp.wait()
        total = slots[0]
        for s in range(1, N_DEV):
            total = total + slots[s]
        delta, m_new, v_new = _adamw(w_ref[...], total, m_ref[...], v_ref[...])
        g_out[...] = total
        d_out[...] = delta
        m_out[...] = m_new
        v_out[...] = v_new

    return pl.pallas_call(
        body, name=name,
        in_specs=[vmem] * 4, out_specs=[vmem] * 4,
        out_shape=[jax.ShapeDtypeStruct((R, LANES), F32)] * 4,
        scratch_shapes=[pltpu.VMEM((N_DEV, R, LANES), F32),
                        pltpu.SemaphoreType.DMA((N_DEV,)), pltpu.SemaphoreType.DMA((N_DEV,))],
    )(g, w, m, v)


def _pack(parts):
    flat = jnp.concatenate([p.reshape(-1) for p in parts])
    rows = -(-flat.shape[0] // (8 * LANES)) * 8
    return jnp.pad(flat, (0, rows * LANES - flat.shape[0])).reshape(rows, LANES)


def _unpack(packed, like):
    flat = packed.reshape(-1)
    out, pos = [], 0
    for p in like:
        out.append(flat[pos:pos + p.size].reshape(p.shape))
        pos += p.size
    return out


def kernel(x, w_in, b_gate, rel_bias, w_proj_a, w_proj_b, w_out, ln1_g, ln1_b, w_ffn_in, w_ffn_out, ln2_g, ln2_b, loss_target, m_w_in, m_b_gate, m_rel_bias, m_w_proj_a, m_w_proj_b, m_w_out, m_ln1_g, m_ln1_b, m_w_ffn_in, m_w_ffn_out, m_ln2_g, m_ln2_b, v_w_in, v_b_gate, v_rel_bias, v_w_proj_a, v_w_proj_b, v_w_out, v_ln1_g, v_ln1_b, v_w_ffn_in, v_w_ffn_out, v_ln2_g, v_ln2_b):
    L = w_in.shape[0]
    T, D = x.shape[1], x.shape[2]
    alpha = float((2 * L) ** 0.25)
    n_qkv = 6 * WIDTH

    big = [w_in, w_proj_a, w_proj_b, w_out, w_ffn_in, w_ffn_out]
    modes = ["col", "col", "col", "row", "blk", "row"]
    W_in, W_pa, W_pb, W_o, W_fi, W_fo = _all_gather([w.astype(BF16) for w in big], modes, name="gather_weights")
    vec3 = lambda a: a[:, None, :]
    bg3, l1g, l1b, l2g, l2b = vec3(b_gate), vec3(ln1_g), vec3(ln1_b), vec3(ln2_g), vec3(ln2_b)

    h = x[0]
    saved = []
    for l in range(L):
        qkv = _mm_nn(h, W_in, l, col_off=0, n_cols=n_qkv, out_dtype=BF16, name=f"in_proj_qkv_{l}")
        gates = _mm_nn(h, W_in, l, col_off=n_qkv, n_cols=2 * D, out_dtype=F32, name=f"in_proj_gates_{l}")
        kvpad = jnp.pad(qkv[:, WIDTH:3 * WIDTH], ((A_WIN - A_TQ, 0), (0, 0)))
        bias = _toeplitz_bias(rel_bias[l])
        oa = _attn_a_fwd(qkv, kvpad, bias, name=f"attn_a_fwd_{l}")
        ob = _attn_b_fwd(qkv, col0=3 * WIDTH // LANES, name=f"attn_b_fwd_{l}")
        x1, r1 = _proj_fwd(oa, ob, gates, bg3, W_pa, W_pb, W_o, h, l1g, l1b, l, alpha=alpha, name=f"proj_fwd_{l}")
        gs, us, r2, x2 = _ffn_fwd(x1, W_fi, W_fo, l2g, l2b, l, alpha=alpha, name=f"ffn_fwd_{l}")
        saved.append((h, qkv, gates, kvpad, bias, oa, ob, x1, r1, gs, us, r2))
        h = x2

    d_h, sq = _loss_head(h, loss_target[0], name="loss_head")
    loss = lax.psum((0.5 / D) * jnp.sum(sq), ("x", "y", "c"))

    g_in, g_pa, g_pb, g_o, g_fi, g_fo = ([None] * L for _ in range(6))
    g_bg, g_rb, g_l1g, g_l1b, g_l2g, g_l2b = ([None] * L for _ in range(6))
    for l in reversed(range(L)):
        xin, qkv, gates, kvpad, bias, oa, ob, x1, r1, gs, us, r2 = saved[l]
        dr2, act, dgt, dup, dx1, g_l2g[l], g_l2b[l] = _ffn_bwd(d_h, r2, l2g, gs, us, W_fi, W_fo, l,
                                                              alpha=alpha, name=f"ffn_bwd_{l}")
        g_fo[l] = _mm_tn_blocked_a(act, dr2, name=f"grad_w_ffn_out_{l}").reshape(-1, D)
        g_fi[l] = jnp.concatenate([_mm_tn_blocked(x1, dgt, name=f"grad_w_ffn_gate_{l}"),
                                   _mm_tn_blocked(x1, dup, name=f"grad_w_ffn_up_{l}")], axis=0)
        (dr1, mixin, dya, dyb, dgates, doa, dob, g_l1g[l], g_l1b[l], g_bg[l]) = _proj_bwd(
            dx1, r1, l1g, oa, ob, gates, bg3, W_pa, W_pb, W_o, l, name=f"proj_bwd_{l}")
        g_o[l] = _mm_tn(mixin, dr1, tm=_tile(D, 1024), tn=_tile(D, 1024), name=f"grad_w_out_{l}")
        g_pa[l] = _mm_tn(oa, dya, tm=WIDTH, tn=_tile(D, 1024), name=f"grad_w_proj_a_{l}")
        g_pb[l] = _mm_tn(ob, dyb, tm=WIDTH, tn=_tile(D, 1024), name=f"grad_w_proj_b_{l}")
        dqa, dka, dva, dbias = _attn_a_bwd(qkv, kvpad, bias, doa, name=f"attn_a_bwd_{l}")
        g_rb[l] = _toeplitz_bias_grad(dbias)
        dqb, dkb, dvb = _attn_b_bwd(qkv, ob, dob, col0=3 * WIDTH // LANES, name=f"attn_b_bwd_{l}")
        pad = A_WIN - A_TQ
        d_pre = jnp.concatenate([dqa, dka[pad:].astype(BF16), dva[pad:].astype(BF16),
                                 dqb, dkb.astype(BF16), dvb.astype(BF16), dgates], axis=1)
        g_in[l] = _mm_tn(xin, d_pre, tm=D, tn=w_in.shape[2], name=f"grad_w_in_{l}")
        d_h = _mm_nt_add(d_pre, W_in, l, dr1, alpha, name=f"grad_x_{l}")
    grad_x = d_h[None]

    slots = _grad_exchange([g_in, g_pa, g_pb, g_o, g_fi, g_fo], modes, L, name="exchange_grads")
    moments_m = [m_w_in, m_w_proj_a, m_w_proj_b, m_w_out, m_w_ffn_in, m_w_ffn_out]
    moments_v = [v_w_in, v_w_proj_a, v_w_proj_b, v_w_out, v_w_ffn_in, v_w_ffn_out]
    names = ["w_in", "w_proj_a", "w_proj_b", "w_out", "w_ffn_in", "w_ffn_out"]
    big_out = {}
    for nm, s, w, m, v in zip(names, slots, big, moments_m, moments_v):
        two = lambda a: a.reshape(-1, a.shape[-1])
        res = _sum_slots_adamw(s.reshape(N_DEV, -1, s.shape[-1]), two(w), two(m), two(v), name=f"adamw_{nm}")
        big_out[nm] = [r.reshape(w.shape) for r in res]

    small_w = [b_gate, rel_bias, ln1_g, ln1_b, ln2_g, ln2_b]
    small_g = [jnp.stack(g) for g in (g_bg, g_rb, g_l1g, g_l1b, g_l2g, g_l2b)]
    small_m = [m_b_gate, m_rel_bias, m_ln1_g, m_ln1_b, m_ln2_g, m_ln2_b]
    small_v = [v_b_gate, v_rel_bias, v_ln1_g, v_ln1_b, v_ln2_g, v_ln2_b]
    res = _small_allreduce_adamw(_pack(small_g), _pack(small_w), _pack(small_m), _pack(small_v),
                                 name="allreduce_small_adamw")
    small_names = ["b_gate", "rel_bias", "ln1_g", "ln1_b", "ln2_g", "ln2_b"]
    small_out = {nm: [] for nm in small_names}
    for packed in res:
        for nm, arr in zip(small_names, _unpack(packed, small_w)):
            small_out[nm].append(arr)

    order = ["w_in", "b_gate", "rel_bias", "w_proj_a", "w_proj_b", "w_out", "ln1_g", "ln1_b",
             "w_ffn_in", "w_ffn_out", "ln2_g", "ln2_b"]
    every = {**big_out, **small_out}
    outs = [loss, grad_x]
    for kind in range(4):
        outs += [every[nm][kind] for nm in order]
    return tuple(outs)
```

```python
import functools
import math

import jax
import jax.numpy as jnp
from jax import lax
from jax.experimental import pallas as pl
from jax.experimental.pallas import tpu as pltpu

F32 = jnp.float32
BF16 = jnp.bfloat16

HEAD_DIM = 64
CHUNK = 64
LEFT_CHUNKS = 8
REL_CLIP = 256
N_REL = 2 * REL_CLIP + 1
WIDTH = 512
LANES = 128
A_TQ = 256
A_WIN = A_TQ + LEFT_CHUNKS * CHUNK
B_TQ = 1024
B_TS = 256
LN_EPS = 1e-5
QK_SCALE = 1.0 / math.sqrt(HEAD_DIM)
LOG2E = 1.4426950408889634
NEG = -1e30

ADAM_LR = 0.001
ADAM_B1 = 0.9
ADAM_B2 = 0.999
ADAM_EPS = 1e-08
ADAM_WD = 0.01
ADAM_STEP = 10

N_DEV = 8
MESH = pl.DeviceIdType.MESH
MIB = 1024 * 1024


def _cparams(sem=None, vmem_mib=48):
    return pltpu.CompilerParams(dimension_semantics=sem, vmem_limit_bytes=vmem_mib * MIB)


def _dot(a, b):
    return jnp.dot(a, b, preferred_element_type=F32)


def _dot_nt(a, b):
    return lax.dot_general(a, b, (((1,), (1,)), ((), ())), preferred_element_type=F32)


def _dot_tn(a, b):
    return lax.dot_general(a, b, (((0,), (0,)), ((), ())), preferred_element_type=F32)


def _tile(n, pref):
    if n <= pref:
        return n
    for t in range(pref - pref % 8, 0, -8):
        if n % t == 0:
            return t
    raise ValueError((n, pref))


def _mm_nn(a, w, layer, *, col_off, n_cols, out_dtype, name):
    M, K = a.shape
    tm = _tile(M, 1024)
    tn = _tile(n_cols, 512)
    assert col_off % tn == 0
    off = col_off // tn

    def body(a_ref, w_ref, o_ref):
        o_ref[...] = _dot(a_ref[...].astype(BF16), w_ref[...]).astype(out_dtype)

    return pl.pallas_call(
        body, name=name, grid=(M // tm, n_cols // tn),
        in_specs=[pl.BlockSpec((tm, K), lambda i, j: (i, 0)),
                  pl.BlockSpec((None, K, tn), lambda i, j: (layer, 0, j + off))],
        out_specs=pl.BlockSpec((tm, tn), lambda i, j: (i, j)),
        out_shape=jax.ShapeDtypeStruct((M, n_cols), out_dtype),
        compiler_params=_cparams(("parallel", "parallel")),
    )(a, w)


def _mm_nt_add(a, w, layer, add, add_scale, *, name):
    M, K = a.shape
    N = w.shape[1]
    tm = _tile(M, 1024)
    tk = _tile(K, 1024)

    def body(a_ref, w_ref, add_ref, o_ref):
        @pl.when(pl.program_id(1) == 0)
        def _():
            o_ref[...] = add_scale * add_ref[...]
        o_ref[...] += _dot_nt(a_ref[...], w_ref[...])

    return pl.pallas_call(
        body, name=name, grid=(M // tm, K // tk),
        in_specs=[pl.BlockSpec((tm, tk), lambda i, k: (i, k)),
                  pl.BlockSpec((None, N, tk), lambda i, k: (layer, 0, k)),
                  pl.BlockSpec((tm, N), lambda i, k: (i, 0))],
        out_specs=pl.BlockSpec((tm, N), lambda i, k: (i, 0)),
        out_shape=jax.ShapeDtypeStruct((M, N), F32),
        compiler_params=_cparams(("parallel", "arbitrary")),
    )(a, w, add)


def _tn_body(k_axis, n_k):
    def body(a_ref, b_ref, o_ref, acc_ref):
        k = pl.program_id(k_axis)

        @pl.when(k == 0)
        def _():
            acc_ref[...] = jnp.zeros_like(acc_ref)
        acc_ref[...] += _dot_tn(a_ref[...].astype(BF16), b_ref[...].astype(BF16))

        @pl.when(k == n_k - 1)
        def _():
            o_ref[...] = acc_ref[...].astype(o_ref.dtype)
    return body


def _mm_tn(a, b, *, tm, tn, name):
    T, M = a.shape
    N = b.shape[1]
    tk = _tile(T, 512)
    return pl.pallas_call(
        _tn_body(2, T // tk), name=name, grid=(M // tm, N // tn, T // tk),
        in_specs=[pl.BlockSpec((tk, tm), lambda i, j, k: (k, i)),
                  pl.BlockSpec((tk, tn), lambda i, j, k: (k, j))],
        out_specs=pl.BlockSpec((tm, tn), lambda i, j, k: (i, j)),
        out_shape=jax.ShapeDtypeStruct((M, N), BF16),
        scratch_shapes=[pltpu.VMEM((tm, tn), F32)],
        compiler_params=_cparams(("parallel", "parallel", "arbitrary")),
    )(a, b)


def _mm_tn_blocked(a, b, *, name):
    T, M = a.shape
    S, _, N = b.shape
    tk = _tile(T, 512)
    return pl.pallas_call(
        _tn_body(1, T // tk), name=name, grid=(S, T // tk),
        in_specs=[pl.BlockSpec((tk, M), lambda s, k: (k, 0)),
                  pl.BlockSpec((None, tk, N), lambda s, k: (s, k, 0))],
        out_specs=pl.BlockSpec((None, M, N), lambda s, k: (s, 0, 0)),
        out_shape=jax.ShapeDtypeStruct((S, M, N), BF16),
        scratch_shapes=[pltpu.VMEM((M, N), F32)],
        compiler_params=_cparams(("parallel", "arbitrary")),
    )(a, b)


def _mm_tn_blocked_a(a, b, *, name):
    S, T, M = a.shape
    N = b.shape[1]
    tk = _tile(T, 512)
    return pl.pallas_call(
        _tn_body(1, T // tk), name=name, grid=(S, T // tk),
        in_specs=[pl.BlockSpec((None, tk, M), lambda s, k: (s, k, 0)),
                  pl.BlockSpec((tk, N), lambda s, k: (k, 0))],
        out_specs=pl.BlockSpec((None, M, N), lambda s, k: (s, 0, 0)),
        out_shape=jax.ShapeDtypeStruct((S, M, N), BF16),
        scratch_shapes=[pltpu.VMEM((M, N), F32)],
        compiler_params=_cparams(("parallel", "arbitrary")),
    )(a, b)


def _ln_fwd(r, g, b):
    mu = jnp.mean(r, axis=-1, keepdims=True)
    xc = r - mu
    var = jnp.mean(xc * xc, axis=-1, keepdims=True)
    return xc * lax.rsqrt(var + LN_EPS) * g + b


def _ln_bwd(dy, r, g):
    mu = jnp.mean(r, axis=-1, keepdims=True)
    xc = r - mu
    var = jnp.mean(xc * xc, axis=-1, keepdims=True)
    rstd = lax.rsqrt(var + LN_EPS)
    xhat = xc * rstd
    dxh = dy * g
    m1 = jnp.mean(dxh, axis=-1, keepdims=True)
    m2 = jnp.mean(dxh * xhat, axis=-1, keepdims=True)
    return rstd * (dxh - m1 - xhat * m2), xhat


def _lane_is_head0():
    return lax.broadcasted_iota(jnp.int32, (1, LANES), 1) < HEAD_DIM


def _band_valid(i):
    r = lax.broadcasted_iota(jnp.int32, (A_TQ, A_WIN), 0)
    c = lax.broadcasted_iota(jnp.int32, (A_TQ, A_WIN), 1)
    a = r // CHUNK
    b = c // CHUNK
    return (b >= a) & (b <= a + LEFT_CHUNKS) & (c + i * A_TQ >= LEFT_CHUNKS * CHUNK)


def _band_probs(q2, k2, bias, valid, head0, h):
    qh = jnp.where(head0 if h == 0 else jnp.logical_not(head0), q2, jnp.zeros_like(q2))
    s = _dot_nt(qh, k2) * QK_SCALE + bias
    s = jnp.where(valid, s, NEG)
    m = jnp.max(s, axis=1, keepdims=True)
    e = jnp.exp(s - m)
    return e / jnp.sum(e, axis=1, keepdims=True)


def _attn_a_fwd(qkv, kvpad, bias, *, name):
    T = qkv.shape[0]
    n_hp = WIDTH // LANES

    def body(q_ref, k_ref, v_ref, bias_ref, o_ref):
        i = pl.program_id(1)
        row0 = pl.multiple_of(i * A_TQ, A_TQ)
        q2 = q_ref[...]
        k2 = k_ref[pl.ds(row0, A_WIN), :]
        v2 = v_ref[pl.ds(row0, A_WIN), :]
        head0 = _lane_is_head0()
        valid = _band_valid(i)
        outs = []
        for h in range(2):
            p = _band_probs(q2, k2, bias_ref[h], valid, head0, h)
            outs.append(_dot(p.astype(BF16), v2))
        o_ref[...] = jnp.where(head0, outs[0], outs[1]).astype(o_ref.dtype)

    return pl.pallas_call(
        body, name=name, grid=(n_hp, T // A_TQ),
        in_specs=[pl.BlockSpec((A_TQ, LANES), lambda hp, i: (i, hp)),
                  pl.BlockSpec((T + A_WIN - A_TQ, LANES), lambda hp, i: (0, hp)),
                  pl.BlockSpec((T + A_WIN - A_TQ, LANES), lambda hp, i: (0, hp + n_hp)),
                  pl.BlockSpec((2, A_TQ, A_WIN), lambda hp, i: (hp, 0, 0))],
        out_specs=pl.BlockSpec((A_TQ, LANES), lambda hp, i: (i, hp)),
        out_shape=jax.ShapeDtypeStruct((T, WIDTH), BF16),
        compiler_params=_cparams(("parallel", "arbitrary")),
    )(qkv, kvpad, kvpad, bias)


def _attn_a_bwd(qkv, kvpad, bias, do, *, name):
    T = qkv.shape[0]
    TP = T + A_WIN - A_TQ
    n_hp = WIDTH // LANES

    def body(q_ref, k_ref, v_ref, bias_ref, do_ref, dq_ref, dk_ref, dv_ref, db_ref):
        i = pl.program_id(1)

        @pl.when(i == 0)
        def _():
            dk_ref[...] = jnp.zeros_like(dk_ref)
            dv_ref[...] = jnp.zeros_like(dv_ref)
            db_ref[...] = jnp.zeros_like(db_ref)

        row0 = pl.multiple_of(i * A_TQ, A_TQ)
        q2 = q_ref[...]
        do2 = do_ref[...]
        k2 = k_ref[pl.ds(row0, A_WIN), :]
        v2 = v_ref[pl.ds(row0, A_WIN), :]
        head0 = _lane_is_head0()
        valid = _band_valid(i)
        dq, dk, dv = [], [], []
        for h in range(2):
            hm = head0 if h == 0 else jnp.logical_not(head0)
            p = _band_probs(q2, k2, bias_ref[h], valid, head0, h)
            doh = jnp.where(hm, do2, jnp.zeros_like(do2))
            dp = _dot_nt(doh, v2)
            delta = jnp.sum(p * dp, axis=1, keepdims=True)
            ds = p * (dp - delta)
            db_ref[h] += ds
            dsb = (ds * QK_SCALE).astype(BF16)
            dq.append(_dot(dsb, k2))
            dk.append(_dot_tn(dsb, q2))
            dv.append(_dot_tn(p.astype(BF16), do2))
        dq_ref[...] = jnp.where(head0, dq[0], dq[1]).astype(dq_ref.dtype)
        dk_ref[pl.ds(row0, A_WIN), :] += jnp.where(head0, dk[0], dk[1])
        dv_ref[pl.ds(row0, A_WIN), :] += jnp.where(head0, dv[0], dv[1])

    return pl.pallas_call(
        body, name=name, grid=(n_hp, T // A_TQ),
        in_specs=[pl.BlockSpec((A_TQ, LANES), lambda hp, i: (i, hp)),
                  pl.BlockSpec((TP, LANES), lambda hp, i: (0, hp)),
                  pl.BlockSpec((TP, LANES), lambda hp, i: (0, hp + n_hp)),
                  pl.BlockSpec((2, A_TQ, A_WIN), lambda hp, i: (hp, 0, 0)),
                  pl.BlockSpec((A_TQ, LANES), lambda hp, i: (i, hp))],
        out_specs=[pl.BlockSpec((A_TQ, LANES), lambda hp, i: (i, hp)),
                   pl.BlockSpec((TP, LANES), lambda hp, i: (0, hp)),
                   pl.BlockSpec((TP, LANES), lambda hp, i: (0, hp)),
                   pl.BlockSpec((2, A_TQ, A_WIN), lambda hp, i: (hp, 0, 0))],
        out_shape=[jax.ShapeDtypeStruct((T, WIDTH), BF16),
                   jax.ShapeDtypeStruct((TP, WIDTH), F32),
                   jax.ShapeDtypeStruct((TP, WIDTH), F32),
                   jax.ShapeDtypeStruct((WIDTH // HEAD_DIM, A_TQ, A_WIN), F32)],
        compiler_params=_cparams(("parallel", "arbitrary"), 56),
    )(qkv, kvpad, kvpad, bias, do)


def _toeplitz_bias(rb):
    H = rb.shape[0]
    span = A_TQ + A_WIN - 1
    n_tail = span - (N_REL - 1)
    ext = jnp.concatenate([rb[:, 1:], jnp.broadcast_to(rb[:, N_REL - 1:], (H, n_tail))], axis=1)
    rev = jnp.pad(ext[:, ::-1], ((0, 0), (0, 1)))
    flat = jnp.broadcast_to(rev[:, None, :], (H, A_TQ, span + 1)).reshape(H, A_TQ * (span + 1))
    skew = flat[:, :A_TQ * span].reshape(H, A_TQ, span)
    return skew[:, :, A_TQ - 1:A_TQ - 1 + A_WIN]


def _toeplitz_bias_grad(db):
    H = db.shape[0]
    span = A_TQ + A_WIN - 1
    d_skew = jnp.pad(db, ((0, 0), (0, 0), (A_TQ - 1, span - (A_TQ - 1) - A_WIN)))
    d_flat = jnp.pad(d_skew.reshape(H, A_TQ * span), ((0, 0), (0, A_TQ)))
    g_ext = jnp.sum(d_flat.reshape(H, A_TQ, span + 1), axis=1)[:, :span][:, ::-1]
    last = g_ext[:, N_REL - 2] + jnp.sum(g_ext[:, N_REL - 1:], axis=1)
    return jnp.concatenate([jnp.zeros((H, 1), F32), g_ext[:, :N_REL - 2], last[:, None]], axis=1)


def _split_bf16(x):
    hi = x.astype(BF16)
    lo = (x - hi.astype(F32)).astype(BF16)
    return hi, lo


def _sb_streams(d):
    out = []
    for h in range(2):
        for r in range(B_TQ // B_TS):
            if d is not None and d > r:
                continue
            out.append(dict(h=h, rows=pl.ds(r * B_TS, B_TS), diag=(d is not None and d == r)))
    return out


def _skewed(streams, stages):
    for t in range(len(streams) + len(stages) - 1):
        for s, st in enumerate(streams):
            if 0 <= t - s < len(stages):
                stages[t - s](st)


def _sb_logs(st, z2):
    lp2 = jnp.log(1.0 + jnp.exp2(-jnp.abs(z2))) * LOG2E
    st["log_beta"] = jnp.minimum(z2, 0.0) - lp2
    log_keep = st["log_beta"] - z2
    if st["diag"]:
        log_keep = jnp.where(_strict_lower(), log_keep, 0.0)
    st["log_keep"] = log_keep
    st["hi"], st["lo"] = _split_bf16(log_keep)


def _strict_lower():
    t = lax.broadcasted_iota(jnp.int32, (B_TS, B_TS), 0)
    s = lax.broadcasted_iota(jnp.int32, (B_TS, B_TS), 1)
    return s < t


def _tri(strict):
    j = lax.broadcasted_iota(jnp.int32, (B_TS, B_TS), 0)
    s = lax.broadcasted_iota(jnp.int32, (B_TS, B_TS), 1)
    return jnp.where(j > s if strict else j >= s, 1.0, 0.0).astype(BF16)


def _attn_b_fwd(qkv, *, col0, name):
    T = qkv.shape[0]
    n_hp = WIDTH // LANES
    sub = B_TQ // B_TS

    def body(q_ref, k_ref, v_ref, o_ref, acc_ref, car_ref, qh_ref):
        i = pl.program_id(1)
        q2 = q_ref[...]
        head0 = _lane_is_head0()
        qh_ref[0] = jnp.where(head0, q2, jnp.zeros_like(q2))
        qh_ref[1] = jnp.where(head0, jnp.zeros_like(q2), q2)
        tri_s = _tri(True)
        acc_ref[...] = jnp.zeros_like(acc_ref)
        car_ref[...] = jnp.zeros_like(car_ref)

        def tile(kb, d):
            k0 = pl.multiple_of(kb * B_TS, B_TS)
            k2 = k_ref[pl.ds(k0, B_TS), :]
            v2 = v_ref[pl.ds(k0, B_TS), :]

            def scores(st):
                st["z2"] = _dot_nt(qh_ref[st["h"], st["rows"], :], k2) * (QK_SCALE * LOG2E)

            def logs(st):
                _sb_logs(st, st.pop("z2"))

            def suffix(st):
                st["suffix"] = _dot(st.pop("hi"), tri_s) + _dot(st.pop("lo"), tri_s)

            def weights(st):
                car = car_ref[st["h"], st["rows"], :]
                w = jnp.exp2(st.pop("log_beta") + st.pop("suffix") + car)
                if st["diag"]:
                    w = jnp.where(_strict_lower(), w, 0.0)
                st["wb"] = w.astype(BF16)
                car_ref[st["h"], st["rows"], :] = car + jnp.sum(st.pop("log_keep"), axis=1, keepdims=True)

            def values(st):
                acc_ref[st["h"], st["rows"], :] += _dot(st.pop("wb"), v2)

            _skewed(_sb_streams(d), [scores, logs, suffix, weights, values])

        for d in reversed(range(sub)):
            tile(i * sub + d, d)

        def step(n, carry):
            tile(i * sub - 1 - n, None)
            return carry

        lax.fori_loop(0, i * sub, step, 0)
        o_ref[...] = jnp.where(head0, acc_ref[0], acc_ref[1])

    return pl.pallas_call(
        body, name=name, grid=(n_hp, T // B_TQ),
        in_specs=[pl.BlockSpec((B_TQ, LANES), lambda hp, i: (i, hp + col0)),
                  pl.BlockSpec((T, LANES), lambda hp, i: (0, hp + col0 + n_hp)),
                  pl.BlockSpec((T, LANES), lambda hp, i: (0, hp + col0 + 2 * n_hp))],
        out_specs=pl.BlockSpec((B_TQ, LANES), lambda hp, i: (i, hp)),
        out_shape=jax.ShapeDtypeStruct((T, WIDTH), F32),
        scratch_shapes=[pltpu.VMEM((2, B_TQ, LANES), F32), pltpu.VMEM((2, B_TQ, 1), F32),
                        pltpu.VMEM((2, B_TQ, LANES), BF16)],
        compiler_params=_cparams(("parallel", "arbitrary")),
    )(qkv, qkv, qkv)


def _attn_b_bwd(qkv, out, do, *, col0, name):
    T = qkv.shape[0]
    n_hp = WIDTH // LANES
    sub = B_TQ // B_TS

    def body(q_ref, k_ref, v_ref, o_ref, do_ref, dq_ref, dk_ref, dv_ref,
             dqa_ref, car_ref, carr_ref, tot_ref, qh_ref, doh_ref, qs_ref):
        i = pl.program_id(1)

        @pl.when(i == 0)
        def _():
            dk_ref[...] = jnp.zeros_like(dk_ref)
            dv_ref[...] = jnp.zeros_like(dv_ref)

        q2 = q_ref[...]
        do2 = do_ref[...]
        head0 = _lane_is_head0()
        zero = jnp.zeros_like(q2)
        qh_ref[0] = jnp.where(head0, q2, zero)
        qh_ref[1] = jnp.where(head0, zero, q2)
        doh_ref[0] = jnp.where(head0, do2, zero)
        doh_ref[1] = jnp.where(head0, zero, do2)
        scale = jnp.asarray(QK_SCALE, BF16)
        qs_ref[...] = q2 * scale
        tri_s = _tri(True)
        tri_i = _tri(False)
        prod = do2.astype(F32) * o_ref[...]
        tot_ref[0] = jnp.sum(jnp.where(head0, prod, 0.0), axis=1, keepdims=True)
        tot_ref[1] = jnp.sum(jnp.where(head0, 0.0, prod), axis=1, keepdims=True)
        dqa_ref[...] = jnp.zeros_like(dqa_ref)
        car_ref[...] = jnp.zeros_like(car_ref)
        carr_ref[...] = jnp.zeros_like(carr_ref)

        def tile(kb, d):
            k0 = pl.multiple_of(kb * B_TS, B_TS)
            keys = pl.ds(k0, B_TS)
            k2 = k_ref[keys, :]
            v2 = v_ref[keys, :]
            k2s = k2 * scale

            def scores(st):
                st["z2"] = _dot_nt(qh_ref[st["h"], st["rows"], :], k2) * (QK_SCALE * LOG2E)
                st["dw"] = _dot_nt(doh_ref[st["h"], st["rows"], :], v2)

            def logs(st):
                _sb_logs(st, st.pop("z2"))

            def suffix(st):
                st["suffix"] = _dot(st.pop("hi"), tri_s) + _dot(st.pop("lo"), tri_s)

            def weights(st):
                h, rows = st["h"], st["rows"]
                car = car_ref[h, rows, :]
                w = jnp.exp2(st["log_beta"] + st.pop("suffix") + car)
                if st["diag"]:
                    w = jnp.where(_strict_lower(), w, 0.0)
                st["wb"] = w.astype(BF16)
                st["dlog"] = st["wb"].astype(F32) * st.pop("dw")
                st["hi"], st["lo"] = _split_bf16(st["dlog"])
                car_ref[h, rows, :] = car + jnp.sum(st["log_keep"], axis=1, keepdims=True)

            def later(st):
                st["later"] = _dot(st.pop("hi"), tri_i) + _dot(st.pop("lo"), tri_i)

            def dscores(st):
                h, rows = st["h"], st["rows"]
                carr = carr_ref[h, rows, :]
                earlier = tot_ref[h, rows, :] - (st.pop("later") + carr)
                dlog = st.pop("dlog")
                dz = dlog * jnp.exp2(st.pop("log_keep")) - jnp.exp2(st.pop("log_beta")) * earlier
                if st["diag"]:
                    dz = jnp.where(_strict_lower(), dz, 0.0)
                st["dzb"] = dz.astype(BF16)
                carr_ref[h, rows, :] = carr + jnp.sum(dlog, axis=1, keepdims=True)

            def grads(st):
                h, rows = st["h"], st["rows"]
                mine = head0 if h == 0 else jnp.logical_not(head0)
                dzb = st.pop("dzb")
                dqa_ref[h, rows, :] += _dot(dzb, k2s)
                dk_ref[keys, :] += jnp.where(mine, _dot_tn(dzb, qs_ref[rows, :]), 0.0)
                dv_ref[keys, :] += jnp.where(mine, _dot_tn(st.pop("wb"), do_ref[rows, :]), 0.0)

            _skewed(_sb_streams(d), [scores, logs, suffix, weights, later, dscores, grads])

        for d in reversed(range(sub)):
            tile(i * sub + d, d)

        def step(n, carry):
            tile(i * sub - 1 - n, None)
            return carry

        lax.fori_loop(0, i * sub, step, 0)
        dq_ref[...] = jnp.where(head0, dqa_ref[0], dqa_ref[1]).astype(dq_ref.dtype)

    return pl.pallas_call(
        body, name=name, grid=(n_hp, T // B_TQ),
        in_specs=[pl.BlockSpec((B_TQ, LANES), lambda hp, i: (i, hp + col0)),
                  pl.BlockSpec((T, LANES), lambda hp, i: (0, hp + col0 + n_hp)),
                  pl.BlockSpec((T, LANES), lambda hp, i: (0, hp + col0 + 2 * n_hp)),
                  pl.BlockSpec((B_TQ, LANES), lambda hp, i: (i, hp)),
                  pl.BlockSpec((B_TQ, LANES), lambda hp, i: (i, hp))],
        out_specs=[pl.BlockSpec((B_TQ, LANES), lambda hp, i: (i, hp)),
                   pl.BlockSpec((T, LANES), lambda hp, i: (0, hp)),
                   pl.BlockSpec((T, LANES), lambda hp, i: (0, hp))],
        out_shape=[jax.ShapeDtypeStruct((T, WIDTH), BF16),
                   jax.ShapeDtypeStruct((T, WIDTH), F32),
                   jax.ShapeDtypeStruct((T, WIDTH), F32)],
        scratch_shapes=[pltpu.VMEM((2, B_TQ, LANES), F32), pltpu.VMEM((2, B_TQ, 1), F32),
                        pltpu.VMEM((2, B_TQ, 1), F32), pltpu.VMEM((2, B_TQ, 1), F32),
                        pltpu.VMEM((2, B_TQ, LANES), BF16), pltpu.VMEM((2, B_TQ, LANES), BF16),
                        pltpu.VMEM((B_TQ, LANES), BF16)],
        compiler_params=_cparams(("parallel", "arbitrary"), 56),
    )(qkv, qkv, qkv, out, do)


def _gated_mix(oa_ref, ob_ref, g_ref, bg_ref, wpa_ref, wpb_ref, D):
    ya = _dot(oa_ref[...].astype(BF16), wpa_ref[...])
    yb = _dot(ob_ref[...].astype(BF16), wpb_ref[...])
    sa = jax.nn.sigmoid(g_ref[:, :D] + bg_ref[:, :D])
    sb = jax.nn.sigmoid(g_ref[:, D:] + bg_ref[:, D:])
    return ya, yb, sa, sb


def _proj_fwd(oa, ob, g, bg, wpa, wpb, wo, xin, lng, lnb, layer, *, alpha, name):
    T, D = xin.shape
    tm = _tile(T, 512)
    row = lambda i: (i, 0)
    wspec = lambda r, c: pl.BlockSpec((None, r, c), lambda i: (layer, 0, 0))
    vec = lambda c: pl.BlockSpec((None, 1, c), lambda i: (layer, 0, 0))

    def body(oa_ref, ob_ref, g_ref, bg_ref, wpa_ref, wpb_ref, wo_ref, x_ref, lg_ref, lb_ref, x1_ref, r1_ref):
        ya, yb, sa, sb = _gated_mix(oa_ref, ob_ref, g_ref, bg_ref, wpa_ref, wpb_ref, D)
        mix = _dot((sa * ya + sb * yb).astype(BF16), wo_ref[...])
        r1 = alpha * x_ref[...] + mix
        r1_ref[...] = r1
        x1_ref[...] = _ln_fwd(r1, lg_ref[...], lb_ref[...])

    return pl.pallas_call(
        body, name=name, grid=(T // tm,),
        in_specs=[pl.BlockSpec((tm, WIDTH), row), pl.BlockSpec((tm, WIDTH), row), pl.BlockSpec((tm, 2 * D), row),
                  vec(2 * D), wspec(WIDTH, D), wspec(WIDTH, D), wspec(D, D),
                  pl.BlockSpec((tm, D), row), vec(D), vec(D)],
        out_specs=[pl.BlockSpec((tm, D), row), pl.BlockSpec((tm, D), row)],
        out_shape=[jax.ShapeDtypeStruct((T, D), F32), jax.ShapeDtypeStruct((T, D), F32)],
        compiler_params=_cparams(("arbitrary",), 56),
    )(oa, ob, g, bg, wpa, wpb, wo, xin, lng, lnb)


def _proj_bwd(dx1, r1, lng, oa, ob, g, bg, wpa, wpb, wo, layer, *, name):
    T, D = dx1.shape
    tm = _tile(T, 512)
    row = lambda i: (i, 0)
    fixed = lambda i: (0, 0)
    wspec = lambda r, c: pl.BlockSpec((None, r, c), lambda i: (layer, 0, 0))
    vec = lambda c: pl.BlockSpec((None, 1, c), lambda i: (layer, 0, 0))

    def body(dx_ref, r1_ref, lg_ref, oa_ref, ob_ref, g_ref, bg_ref, wpa_ref, wpb_ref, wo_ref,
             dr_ref, mix_ref, dya_ref, dyb_ref, dg_ref, doa_ref, dob_ref, dlg_ref, dlb_ref, dbg_ref):
        @pl.when(pl.program_id(0) == 0)
        def _():
            dlg_ref[...] = jnp.zeros_like(dlg_ref)
            dlb_ref[...] = jnp.zeros_like(dlb_ref)
            dbg_ref[...] = jnp.zeros_like(dbg_ref)

        dx = dx_ref[...]
        dr, xhat = _ln_bwd(dx, r1_ref[...], lg_ref[...])
        dr_ref[...] = dr
        dlg_ref[...] += jnp.sum(dx * xhat, axis=0, keepdims=True)
        dlb_ref[...] += jnp.sum(dx, axis=0, keepdims=True)
        dmix = _dot_nt(dr.astype(BF16), wo_ref[...])
        ya, yb, sa, sb = _gated_mix(oa_ref, ob_ref, g_ref, bg_ref, wpa_ref, wpb_ref, D)
        mix_ref[...] = (sa * ya + sb * yb).astype(BF16)
        dya = (dmix * sa).astype(BF16)
        dyb = (dmix * sb).astype(BF16)
        dya_ref[...] = dya
        dyb_ref[...] = dyb
        dga = dmix * ya * (sa * (1.0 - sa))
        dgb = dmix * yb * (sb * (1.0 - sb))
        dg_ref[:, :D] = dga.astype(BF16)
        dg_ref[:, D:] = dgb.astype(BF16)
        dbg_ref[:, :D] += jnp.sum(dga, axis=0, keepdims=True)
        dbg_ref[:, D:] += jnp.sum(dgb, axis=0, keepdims=True)
        doa_ref[...] = _dot_nt(dya, wpa_ref[...]).astype(BF16)
        dob_ref[...] = _dot_nt(dyb, wpb_ref[...]).astype(BF16)

    return pl.pallas_call(
        body, name=name, grid=(T // tm,),
        in_specs=[pl.BlockSpec((tm, D), row), pl.BlockSpec((tm, D), row), vec(D),
                  pl.BlockSpec((tm, WIDTH), row), pl.BlockSpec((tm, WIDTH), row), pl.BlockSpec((tm, 2 * D), row),
                  vec(2 * D), wspec(WIDTH, D), wspec(WIDTH, D), wspec(D, D)],
        out_specs=[pl.BlockSpec((tm, D), row), pl.BlockSpec((tm, D), row), pl.BlockSpec((tm, D), row),
                   pl.BlockSpec((tm, D), row), pl.BlockSpec((tm, 2 * D), row),
                   pl.BlockSpec((tm, WIDTH), row), pl.BlockSpec((tm, WIDTH), row),
                   pl.BlockSpec((1, D), fixed), pl.BlockSpec((1, D), fixed), pl.BlockSpec((1, 2 * D), fixed)],
        out_shape=[jax.ShapeDtypeStruct((T, D), F32), jax.ShapeDtypeStruct((T, D), BF16),
                   jax.ShapeDtypeStruct((T, D), BF16), jax.ShapeDtypeStruct((T, D), BF16),
                   jax.ShapeDtypeStruct((T, 2 * D), BF16),
                   jax.ShapeDtypeStruct((T, WIDTH), BF16), jax.ShapeDtypeStruct((T, WIDTH), BF16),
                   jax.ShapeDtypeStruct((1, D), F32), jax.ShapeDtypeStruct((1, D), F32),
                   jax.ShapeDtypeStruct((1, 2 * D), F32)],
        compiler_params=_cparams(("arbitrary",), 56),
    )(dx1, r1, lng, oa, ob, g, bg, wpa, wpb, wo)


def _ffn_fwd(x1, wfi, wfo, lng, lnb, layer, *, alpha, name):
    T, D = x1.shape
    tf = wfi.shape[-1]
    nj = wfi.shape[0] // 2
    tm = _tile(T, 512)
    vec = lambda c: pl.BlockSpec((None, 1, c), lambda i, j: (layer, 0, 0))

    def body(x_ref, wg_ref, wu_ref, wo_ref, lg_ref, lb_ref, gs_ref, us_ref, r2_ref, x2_ref, acc_ref, xb_ref):
        j = pl.program_id(1)

        @pl.when(j == 0)
        def _():
            xb_ref[...] = x_ref[...].astype(BF16)
            acc_ref[...] = jnp.zeros_like(acc_ref)

        gv = _dot(xb_ref[...], wg_ref[...])
        uv = _dot(xb_ref[...], wu_ref[...])
        gs_ref[...] = gv
        us_ref[...] = uv
        act = gv * jax.nn.sigmoid(gv) * uv
        acc_ref[...] += _dot(act.astype(BF16), wo_ref[...])

        @pl.when(j == nj - 1)
        def _():
            r2 = alpha * x_ref[...] + acc_ref[...]
            r2_ref[...] = r2
            x2_ref[...] = _ln_fwd(r2, lg_ref[...], lb_ref[...])

    return pl.pallas_call(
        body, name=name, grid=(T // tm, nj),
        in_specs=[pl.BlockSpec((tm, D), lambda i, j: (i, 0)),
                  pl.BlockSpec((None, None, D, tf), lambda i, j: (j, layer, 0, 0)),
                  pl.BlockSpec((None, None, D, tf), lambda i, j: (j + nj, layer, 0, 0)),
                  pl.BlockSpec((None, tf, D), lambda i, j: (layer, j, 0)),
                  vec(D), vec(D)],
        out_specs=[pl.BlockSpec((None, tm, tf), lambda i, j: (j, i, 0)),
                   pl.BlockSpec((None, tm, tf), lambda i, j: (j, i, 0)),
                   pl.BlockSpec((tm, D), lambda i, j: (i, 0)),
                   pl.BlockSpec((tm, D), lambda i, j: (i, 0))],
        out_shape=[jax.ShapeDtypeStruct((nj, T, tf), F32), jax.ShapeDtypeStruct((nj, T, tf), F32),
                   jax.ShapeDtypeStruct((T, D), F32), jax.ShapeDtypeStruct((T, D), F32)],
        scratch_shapes=[pltpu.VMEM((tm, D), F32), pltpu.VMEM((tm, D), BF16)],
        compiler_params=_cparams(("parallel", "arbitrary"), 56),
    )(x1, wfi, wfi, wfo, lng, lnb)


def _ffn_bwd(dx2, r2, lng, gs, us, wfi, wfo, layer, *, alpha, name):
    T, D = dx2.shape
    tf = wfi.shape[-1]
    nj = wfi.shape[0] // 2
    tm = _tile(T, 512)
    vec = lambda c: pl.BlockSpec((None, 1, c), lambda i, j: (layer, 0, 0))
    blk = lambda: pl.BlockSpec((None, tm, tf), lambda i, j: (j, i, 0))

    def body(dx_ref, r2_ref, lg_ref, gs_ref, us_ref, wg_ref, wu_ref, wo_ref,
             dr_ref, act_ref, dg_ref, du_ref, dx1_ref, dlg_ref, dlb_ref, acc_ref, drb_ref):
        i = pl.program_id(0)
        j = pl.program_id(1)

        @pl.when((i == 0) & (j == 0))
        def _():
            dlg_ref[...] = jnp.zeros_like(dlg_ref)
            dlb_ref[...] = jnp.zeros_like(dlb_ref)

        @pl.when(j == 0)
        def _():
            dx = dx_ref[...]
            dr, xhat = _ln_bwd(dx, r2_ref[...], lg_ref[...])
            dlg_ref[...] += jnp.sum(dx * xhat, axis=0, keepdims=True)
            dlb_ref[...] += jnp.sum(dx, axis=0, keepdims=True)
            drb_ref[...] = dr.astype(BF16)
            dr_ref[...] = dr.astype(BF16)
            acc_ref[...] = alpha * dr

        dact = _dot_nt(drb_ref[...], wo_ref[...])
        gv = gs_ref[...]
        uv = us_ref[...]
        s = jax.nn.sigmoid(gv)
        silu = gv * s
        act_ref[...] = (silu * uv).astype(BF16)
        dg = (dact * uv * (s * (1.0 + gv * (1.0 - s)))).astype(BF16)
        du = (dact * silu).astype(BF16)
        dg_ref[...] = dg
        du_ref[...] = du
        acc_ref[...] += _dot_nt(dg, wg_ref[...]) + _dot_nt(du, wu_ref[...])

        @pl.when(j == nj - 1)
        def _():
            dx1_ref[...] = acc_ref[...]

    return pl.pallas_call(
        body, name=name, grid=(T // tm, nj),
        in_specs=[pl.BlockSpec((tm, D), lambda i, j: (i, 0)), pl.BlockSpec((tm, D), lambda i, j: (i, 0)), vec(D),
                  blk(), blk(),
                  pl.BlockSpec((None, None, D, tf), lambda i, j: (j, layer, 0, 0)),
                  pl.BlockSpec((None, None, D, tf), lambda i, j: (j + nj, layer, 0, 0)),
                  pl.BlockSpec((None, tf, D), lambda i, j: (layer, j, 0))],
        out_specs=[pl.BlockSpec((tm, D), lambda i, j: (i, 0)), blk(), blk(), blk(),
                   pl.BlockSpec((tm, D), lambda i, j: (i, 0)),
                   pl.BlockSpec((1, D), lambda i, j: (0, 0)), pl.BlockSpec((1, D), lambda i, j: (0, 0))],
        out_shape=[jax.ShapeDtypeStruct((T, D), BF16),
                   jax.ShapeDtypeStruct((nj, T, tf), BF16), jax.ShapeDtypeStruct((nj, T, tf), BF16),
                   jax.ShapeDtypeStruct((nj, T, tf), BF16),
                   jax.ShapeDtypeStruct((T, D), F32),
                   jax.ShapeDtypeStruct((1, D), F32), jax.ShapeDtypeStruct((1, D), F32)],
        scratch_shapes=[pltpu.VMEM((tm, D), F32), pltpu.VMEM((tm, D), BF16)],
        compiler_params=_cparams(("arbitrary", "arbitrary"), 56),
    )(dx2, r2, lng, gs, us, wfi, wfi, wfo)


def _loss_head(y, target, *, name):
    T, D = y.shape
    tm = _tile(T, 1024)

    def body(y_ref, t_ref, dy_ref, sq_ref):
        @pl.when(pl.program_id(0) == 0)
        def _():
            sq_ref[...] = jnp.zeros_like(sq_ref)
        err = y_ref[...] - t_ref[...]
        dy_ref[...] = err * (1.0 / D)
        sq_ref[...] += jnp.sum(err * err, axis=0, keepdims=True)

    return pl.pallas_call(
        body, name=name, grid=(T // tm,),
        in_specs=[pl.BlockSpec((tm, D), lambda i: (i, 0)), pl.BlockSpec((tm, D), lambda i: (i, 0))],
        out_specs=[pl.BlockSpec((tm, D), lambda i: (i, 0)), pl.BlockSpec((1, D), lambda i: (0, 0))],
        out_shape=[jax.ShapeDtypeStruct((T, D), F32), jax.ShapeDtypeStruct((1, D), F32)],
        compiler_params=_cparams(("arbitrary",)),
    )(y, target)


def _my_place():
    return lax.axis_index("x"), lax.axis_index("y"), lax.axis_index("c")


def _peer(place, k):
    x, y, c = place
    return (1 - x if k & 4 else x, 1 - y if k & 2 else y, 1 - c if k & 1 else c)


def _logical(place):
    x, y, c = place
    return 4 * x + 2 * y + c


def _block_of(ref, mode, idx):
    if mode == "blk":
        return ref.at[idx]
    if mode == "col":
        size = ref.shape[2] // N_DEV
        return ref.at[:, :, pl.ds(pl.multiple_of(idx * size, size), size)]
    size = ref.shape[1] // N_DEV
    return ref.at[:, pl.ds(pl.multiple_of(idx * size, size), size), :]


def _full_shape(shard, mode):
    if mode == "blk":
        return (N_DEV,) + shard.shape
    if mode == "col":
        return shard.shape[:2] + (N_DEV * shard.shape[2],)
    return (shard.shape[0], N_DEV * shard.shape[1], shard.shape[2])


def _all_gather(shards, modes, *, name):
    n = len(shards)
    hbm = pl.BlockSpec(memory_space=pltpu.HBM)

    def body(*refs):
        ins, outs = refs[:n], refs[n:2 * n]
        send, recv, local = refs[2 * n:]
        me = _my_place()
        my_id = _logical(me)
        copies = []
        for a in range(n):
            mine = pltpu.make_async_copy(ins[a], _block_of(outs[a], modes[a], my_id), local.at[a])
            mine.start()
            copies.append(mine)
            for k in range(1, N_DEV):
                cp = pltpu.make_async_remote_copy(
                    src_ref=ins[a], dst_ref=_block_of(outs[a], modes[a], my_id),
                    send_sem=send.at[a * N_DEV + k], recv_sem=recv.at[a * N_DEV + k],
                    device_id=_peer(me, k), device_id_type=MESH)
                cp.start()
                copies.append(cp)
        for cp in copies:
            cp.wait()

    return pl.pallas_call(
        body, name=name,
        in_specs=[hbm] * n, out_specs=[hbm] * n,
        out_shape=[jax.ShapeDtypeStruct(_full_shape(s, m), s.dtype) for s, m in zip(shards, modes)],
        scratch_shapes=[pltpu.SemaphoreType.DMA((n * N_DEV,)), pltpu.SemaphoreType.DMA((n * N_DEV,)),
                        pltpu.SemaphoreType.DMA((n,))],
    )(*shards)


def _grad_exchange(grads, modes, n_layers, *, name):
    flat = [g for per_w in grads for g in per_w]
    n_w = len(grads)
    hbm = pl.BlockSpec(memory_space=pltpu.HBM)

    def shard_shape(g, mode):
        if mode == "blk":
            return g.shape[1:]
        if mode == "col":
            return (g.shape[0], g.shape[1] // N_DEV)
        return (g.shape[0] // N_DEV, g.shape[1])

    def block(ref, mode, idx):
        if mode == "blk":
            return ref.at[idx]
        if mode == "col":
            size = ref.shape[1] // N_DEV
            return ref.at[:, pl.ds(pl.multiple_of(idx * size, size), size)]
        size = ref.shape[0] // N_DEV
        return ref.at[pl.ds(pl.multiple_of(idx * size, size), size), :]

    def body(*refs):
        ins = refs[:len(flat)]
        outs = refs[len(flat):len(flat) + n_w]
        send, recv, local = refs[len(flat) + n_w:]
        me = _my_place()
        my_id = _logical(me)
        copies = []
        for w in range(n_w):
            for l in range(n_layers):
                a = w * n_layers + l
                src = ins[a]
                dst = outs[w].at[my_id, l]
                mine = pltpu.make_async_copy(block(src, modes[w], my_id), dst, local.at[a])
                mine.start()
                copies.append(mine)
                for k in range(1, N_DEV):
                    peer = _peer(me, k)
                    cp = pltpu.make_async_remote_copy(
                        src_ref=block(src, modes[w], _logical(peer)), dst_ref=dst,
                        send_sem=send.at[a * N_DEV + k], recv_sem=recv.at[a * N_DEV + k],
                        device_id=peer, device_id_type=MESH)
                    cp.start()
                    copies.append(cp)
        for cp in copies:
            cp.wait()

    return pl.pallas_call(
        body, name=name,
        in_specs=[hbm] * len(flat), out_specs=[hbm] * n_w,
        out_shape=[jax.ShapeDtypeStruct((N_DEV, n_layers) + shard_shape(per_w[0], m), per_w[0].dtype)
                   for per_w, m in zip(grads, modes)],
        scratch_shapes=[pltpu.SemaphoreType.DMA((len(flat) * N_DEV,)), pltpu.SemaphoreType.DMA((len(flat) * N_DEV,)),
                        pltpu.SemaphoreType.DMA((len(flat),))],
    )(*flat)


def _adamw(w, g, m, v):
    m = ADAM_B1 * m + (1.0 - ADAM_B1) * g
    v = ADAM_B2 * v + (1.0 - ADAM_B2) * (g * g)
    m_hat = m / (1.0 - ADAM_B1 ** ADAM_STEP)
    v_hat = v / (1.0 - ADAM_B2 ** ADAM_STEP)
    delta = -ADAM_LR * (m_hat / (jnp.sqrt(v_hat) + ADAM_EPS) + ADAM_WD * w)
    return delta, m, v


def _sum_slots_adamw(slots, w, m, v, *, name):
    R, C = w.shape
    tr = _tile(R, 256)

    def body(s_ref, w_ref, m_ref, v_ref, g_out, d_out, m_out, v_out):
        g = s_ref[0].astype(F32)
        for s in range(1, N_DEV):
            g = g + s_ref[s].astype(F32)
        delta, m_new, v_new = _adamw(w_ref[...], g, m_ref[...], v_ref[...])
        g_out[...] = g
        d_out[...] = delta
        m_out[...] = m_new
        v_out[...] = v_new

    spec = pl.BlockSpec((tr, C), lambda i: (i, 0))
    return pl.pallas_call(
        body, name=name, grid=(R // tr,),
        in_specs=[pl.BlockSpec((N_DEV, tr, C), lambda i: (0, i, 0)), spec, spec, spec],
        out_specs=[spec] * 4,
        out_shape=[jax.ShapeDtypeStruct((R, C), F32)] * 4,
        compiler_params=_cparams(("parallel",)),
    )(slots, w, m, v)


def _small_allreduce_adamw(g, w, m, v, *, name):
    R = g.shape[0]
    vmem = pl.BlockSpec(memory_space=pltpu.VMEM)

    def body(g_ref, w_ref, m_ref, v_ref, g_out, d_out, m_out, v_out, slots, send, recv):
        me = _my_place()
        my_id = _logical(me)
        slots[my_id] = g_ref[...]
        copies = []
        for k in range(1, N_DEV):
            cp = pltpu.make_async_remote_copy(
                src_ref=g_ref, dst_ref=slots.at[my_id], send_sem=send.at[k], recv_sem=recv.at[k],
                device_id=_peer(me, k), device_id_type=MESH)
            cp.start()
            copies.append(cp)
        for cp in copies:
            cp.wait()
        total = slots[0]
        for s in range(1, N_DEV):
            total = total + slots[s]
        delta, m_new, v_new = _adamw(w_ref[...], total, m_ref[...], v_ref[...])
        g_out[...] = total
        d_out[...] = delta
        m_out[...] = m_new
        v_out[...] = v_new

    return pl.pallas_call(
        body, name=name,
        in_specs=[vmem] * 4, out_specs=[vmem] * 4,
        out_shape=[jax.ShapeDtypeStruct((R, LANES), F32)] * 4,
        scratch_shapes=[pltpu.VMEM((N_DEV, R, LANES), F32),
                        pltpu.SemaphoreType.DMA((N_DEV,)), pltpu.SemaphoreType.DMA((N_DEV,))],
    )(g, w, m, v)


def _pack(parts):
    flat = jnp.concatenate([p.reshape(-1) for p in parts])
    rows = -(-flat.shape[0] // (8 * LANES)) * 8
    return jnp.pad(flat, (0, rows * LANES - flat.shape[0])).reshape(rows, LANES)


def _unpack(packed, like):
    flat = packed.reshape(-1)
    out, pos = [], 0
    for p in like:
        out.append(flat[pos:pos + p.size].reshape(p.shape))
        pos += p.size
    return out


def kernel(x, w_in, b_gate, rel_bias, w_proj_a, w_proj_b, w_out, ln1_g, ln1_b, w_ffn_in, w_ffn_out, ln2_g, ln2_b, loss_target, m_w_in, m_b_gate, m_rel_bias, m_w_proj_a, m_w_proj_b, m_w_out, m_ln1_g, m_ln1_b, m_w_ffn_in, m_w_ffn_out, m_ln2_g, m_ln2_b, v_w_in, v_b_gate, v_rel_bias, v_w_proj_a, v_w_proj_b, v_w_out, v_ln1_g, v_ln1_b, v_w_ffn_in, v_w_ffn_out, v_ln2_g, v_ln2_b):
    L = w_in.shape[0]
    T, D = x.shape[1], x.shape[2]
    alpha = float((2 * L) ** 0.25)
    n_qkv = 6 * WIDTH

    big = [w_in, w_proj_a, w_proj_b, w_out, w_ffn_in, w_ffn_out]
    modes = ["col", "col", "col", "row", "blk", "row"]
    W_in, W_pa, W_pb, W_o, W_fi, W_fo = _all_gather([w.astype(BF16) for w in big], modes, name="gather_weights")
    vec3 = lambda a: a[:, None, :]
    bg3, l1g, l1b, l2g, l2b = vec3(b_gate), vec3(ln1_g), vec3(ln1_b), vec3(ln2_g), vec3(ln2_b)

    h = x[0]
    saved = []
    for l in range(L):
        qkv = _mm_nn(h, W_in, l, col_off=0, n_cols=n_qkv, out_dtype=BF16, name=f"in_proj_qkv_{l}")
        gates = _mm_nn(h, W_in, l, col_off=n_qkv, n_cols=2 * D, out_dtype=F32, name=f"in_proj_gates_{l}")
        kvpad = jnp.pad(qkv[:, WIDTH:3 * WIDTH], ((A_WIN - A_TQ, 0), (0, 0)))
        bias = _toeplitz_bias(rel_bias[l])
        oa = _attn_a_fwd(qkv, kvpad, bias, name=f"attn_a_fwd_{l}")
        ob = _attn_b_fwd(qkv, col0=3 * WIDTH // LANES, name=f"attn_b_fwd_{l}")
        x1, r1 = _proj_fwd(oa, ob, gates, bg3, W_pa, W_pb, W_o, h, l1g, l1b, l, alpha=alpha, name=f"proj_fwd_{l}")
        gs, us, r2, x2 = _ffn_fwd(x1, W_fi, W_fo, l2g, l2b, l, alpha=alpha, name=f"ffn_fwd_{l}")
        saved.append((h, qkv, gates, kvpad, bias, oa, ob, x1, r1, gs, us, r2))
        h = x2

    d_h, sq = _loss_head(h, loss_target[0], name="loss_head")
    loss = lax.psum((0.5 / D) * jnp.sum(sq), ("x", "y", "c"))

    g_in, g_pa, g_pb, g_o, g_fi, g_fo = ([None] * L for _ in range(6))
    g_bg, g_rb, g_l1g, g_l1b, g_l2g, g_l2b = ([None] * L for _ in range(6))
    for l in reversed(range(L)):
        xin, qkv, gates, kvpad, bias, oa, ob, x1, r1, gs, us, r2 = saved[l]
        dr2, act, dgt, dup, dx1, g_l2g[l], g_l2b[l] = _ffn_bwd(d_h, r2, l2g, gs, us, W_fi, W_fo, l,
                                                              alpha=alpha, name=f"ffn_bwd_{l}")
        g_fo[l] = _mm_tn_blocked_a(act, dr2, name=f"grad_w_ffn_out_{l}").reshape(-1, D)
        g_fi[l] = jnp.concatenate([_mm_tn_blocked(x1, dgt, name=f"grad_w_ffn_gate_{l}"),
                                   _mm_tn_blocked(x1, dup, name=f"grad_w_ffn_up_{l}")], axis=0)
        (dr1, mixin, dya, dyb, dgates, doa, dob, g_l1g[l], g_l1b[l], g_bg[l]) = _proj_bwd(
            dx1, r1, l1g, oa, ob, gates, bg3, W_pa, W_pb, W_o, l, name=f"proj_bwd_{l}")
        g_o[l] = _mm_tn(mixin, dr1, tm=_tile(D, 1024), tn=_tile(D, 1024), name=f"grad_w_out_{l}")
        g_pa[l] = _mm_tn(oa, dya, tm=WIDTH, tn=_tile(D, 1024), name=f"grad_w_proj_a_{l}")
        g_pb[l] = _mm_tn(ob, dyb, tm=WIDTH, tn=_tile(D, 1024), name=f"grad_w_proj_b_{l}")
        dqa, dka, dva, dbias = _attn_a_bwd(qkv, kvpad, bias, doa, name=f"attn_a_bwd_{l}")
        g_rb[l] = _toeplitz_bias_grad(dbias)
        dqb, dkb, dvb = _attn_b_bwd(qkv, ob, dob, col0=3 * WIDTH // LANES, name=f"attn_b_bwd_{l}")
        pad = A_WIN - A_TQ
        d_pre = jnp.concatenate([dqa, dka[pad:].astype(BF16), dva[pad:].astype(BF16),
                                 dqb, dkb.astype(BF16), dvb.astype(BF16), dgates], axis=1)
        g_in[l] = _mm_tn(xin, d_pre, tm=D, tn=w_in.shape[2], name=f"grad_w_in_{l}")
        d_h = _mm_nt_add(d_pre, W_in, l, dr1, alpha, name=f"grad_x_{l}")
    grad_x = d_h[None]

    slots = _grad_exchange([g_in, g_pa, g_pb, g_o, g_fi, g_fo], modes, L, name="exchange_grads")
    moments_m = [m_w_in, m_w_proj_a, m_w_proj_b, m_w_out, m_w_ffn_in, m_w_ffn_out]
    moments_v = [v_w_in, v_w_proj_a, v_w_proj_b, v_w_out, v_w_ffn_in, v_w_ffn_out]
    names = ["w_in", "w_proj_a", "w_proj_b", "w_out", "w_ffn_in", "w_ffn_out"]
    big_out = {}
    for nm, s, w, m, v in zip(names, slots, big, moments_m, moments_v):
        two = lambda a: a.reshape(-1, a.shape[-1])
        res = _sum_slots_adamw(s.reshape(N_DEV, -1, s.shape[-1]), two(w), two(m), two(v), name=f"adamw_{nm}")
        big_out[nm] = [r.reshape(w.shape) for r in res]

    small_w = [b_gate, rel_bias, ln1_g, ln1_b, ln2_g, ln2_b]
    small_g = [jnp.stack(g) for g in (g_bg, g_rb, g_l1g, g_l1b, g_l2g, g_l2b)]
    small_m = [m_b_gate, m_rel_bias, m_ln1_g, m_ln1_b, m_ln2_g, m_ln2_b]
    small_v = [v_b_gate, v_rel_bias, v_ln1_g, v_ln1_b, v_ln2_g, v_ln2_b]
    res = _small_allreduce_adamw(_pack(small_g), _pack(small_w), _pack(small_m), _pack(small_v),
                                 name="allreduce_small_adamw")
    small_names = ["b_gate", "rel_bias", "ln1_g", "ln1_b", "ln2_g", "ln2_b"]
    small_out = {nm: [] for nm in small_names}
    for packed in res:
        for nm, arr in zip(small_names, _unpack(packed, small_w)):
            small_out[nm].append(arr)

    order = ["w_in", "b_gate", "rel_bias", "w_proj_a", "w_proj_b", "w_out", "ln1_g", "ln1_b",
             "w_ffn_in", "w_ffn_out", "ln2_g", "ln2_b"]
    every = {**big_out, **small_out}
    outs = [loss, grad_x]
    for kind in range(4):
        outs += [every[nm][kind] for nm in order]
    return tuple(outs)
```

```python
import functools
import math

import jax
import jax.numpy as jnp
from jax import lax
from jax.experimental import pallas as pl
from jax.experimental.pallas import tpu as pltpu

F32 = jnp.float32
BF16 = jnp.bfloat16

HEAD_DIM = 64
CHUNK = 64
LEFT_CHUNKS = 8
REL_CLIP = 256
N_REL = 2 * REL_CLIP + 1
WIDTH = 512
LANES = 128
A_TQ = 256
A_WIN = A_TQ + LEFT_CHUNKS * CHUNK
B_TQ = 1024
B_TS = 256
B_PIECE = 64
LN_EPS = 1e-5
QK_SCALE = 1.0 / math.sqrt(HEAD_DIM)
LOG2E = 1.4426950408889634
NEG = -1e30

ADAM_LR = 0.001
ADAM_B1 = 0.9
ADAM_B2 = 0.999
ADAM_EPS = 1e-08
ADAM_WD = 0.01
ADAM_STEP = 10

N_DEV = 8
MESH = pl.DeviceIdType.MESH
MIB = 1024 * 1024


def _cparams(sem=None, vmem_mib=48):
    return pltpu.CompilerParams(dimension_semantics=sem, vmem_limit_bytes=vmem_mib * MIB)


def _dot(a, b):
    return jnp.dot(a, b, preferred_element_type=F32)


def _dot_nt(a, b):
    return lax.dot_general(a, b, (((1,), (1,)), ((), ())), preferred_element_type=F32)


def _dot_tn(a, b):
    return lax.dot_general(a, b, (((0,), (0,)), ((), ())), preferred_element_type=F32)


def _tile(n, pref):
    if n <= pref:
        return n
    for t in range(pref - pref % 8, 0, -8):
        if n % t == 0:
            return t
    raise ValueError((n, pref))


def _mm_nn(a, w, layer, *, col_off, n_cols, out_dtype, name):
    M, K = a.shape
    tm = _tile(M, 1024)
    tn = _tile(n_cols, 512)
    assert col_off % tn == 0
    off = col_off // tn

    def body(a_ref, w_ref, o_ref):
        o_ref[...] = _dot(a_ref[...].astype(BF16), w_ref[...]).astype(out_dtype)

    return pl.pallas_call(
        body, name=name, grid=(M // tm, n_cols // tn),
        in_specs=[pl.BlockSpec((tm, K), lambda i, j: (i, 0)),
                  pl.BlockSpec((None, K, tn), lambda i, j: (layer, 0, j + off))],
        out_specs=pl.BlockSpec((tm, tn), lambda i, j: (i, j)),
        out_shape=jax.ShapeDtypeStruct((M, n_cols), out_dtype),
        compiler_params=_cparams(("parallel", "parallel")),
    )(a, w)


def _mm_nt_add(a, w, layer, add, add_scale, *, name):
    M, K = a.shape
    N = w.shape[1]
    tm = _tile(M, 1024)
    tk = _tile(K, 1024)

    def body(a_ref, w_ref, add_ref, o_ref):
        @pl.when(pl.program_id(1) == 0)
        def _():
            o_ref[...] = add_scale * add_ref[...]
        o_ref[...] += _dot_nt(a_ref[...], w_ref[...])

    return pl.pallas_call(
        body, name=name, grid=(M // tm, K // tk),
        in_specs=[pl.BlockSpec((tm, tk), lambda i, k: (i, k)),
                  pl.BlockSpec((None, N, tk), lambda i, k: (layer, 0, k)),
                  pl.BlockSpec((tm, N), lambda i, k: (i, 0))],
        out_specs=pl.BlockSpec((tm, N), lambda i, k: (i, 0)),
        out_shape=jax.ShapeDtypeStruct((M, N), F32),
        compiler_params=_cparams(("parallel", "arbitrary")),
    )(a, w, add)


def _tn_body(k_axis, n_k):
    def body(a_ref, b_ref, o_ref, acc_ref):
        k = pl.program_id(k_axis)

        @pl.when(k == 0)
        def _():
            acc_ref[...] = jnp.zeros_like(acc_ref)
        acc_ref[...] += _dot_tn(a_ref[...].astype(BF16), b_ref[...].astype(BF16))

        @pl.when(k == n_k - 1)
        def _():
            o_ref[...] = acc_ref[...].astype(o_ref.dtype)
    return body


def _mm_tn(a, b, *, tm, tn, name):
    T, M = a.shape
    N = b.shape[1]
    tk = _tile(T, 512)
    return pl.pallas_call(
        _tn_body(2, T // tk), name=name, grid=(M // tm, N // tn, T // tk),
        in_specs=[pl.BlockSpec((tk, tm), lambda i, j, k: (k, i)),
                  pl.BlockSpec((tk, tn), lambda i, j, k: (k, j))],
        out_specs=pl.BlockSpec((tm, tn), lambda i, j, k: (i, j)),
        out_shape=jax.ShapeDtypeStruct((M, N), BF16),
        scratch_shapes=[pltpu.VMEM((tm, tn), F32)],
        compiler_params=_cparams(("parallel", "parallel", "arbitrary")),
    )(a, b)


def _mm_tn_blocked(a, b, *, name):
    T, M = a.shape
    S, _, N = b.shape
    tk = _tile(T, 512)
    return pl.pallas_call(
        _tn_body(1, T // tk), name=name, grid=(S, T // tk),
        in_specs=[pl.BlockSpec((tk, M), lambda s, k: (k, 0)),
                  pl.BlockSpec((None, tk, N), lambda s, k: (s, k, 0))],
        out_specs=pl.BlockSpec((None, M, N), lambda s, k: (s, 0, 0)),
        out_shape=jax.ShapeDtypeStruct((S, M, N), BF16),
        scratch_shapes=[pltpu.VMEM((M, N), F32)],
        compiler_params=_cparams(("parallel", "arbitrary")),
    )(a, b)


def _mm_tn_blocked_a(a, b, *, name):
    S, T, M = a.shape
    N = b.shape[1]
    tk = _tile(T, 512)
    return pl.pallas_call(
        _tn_body(1, T // tk), name=name, grid=(S, T // tk),
        in_specs=[pl.BlockSpec((None, tk, M), lambda s, k: (s, k, 0)),
                  pl.BlockSpec((tk, N), lambda s, k: (k, 0))],
        out_specs=pl.BlockSpec((None, M, N), lambda s, k: (s, 0, 0)),
        out_shape=jax.ShapeDtypeStruct((S, M, N), BF16),
        scratch_shapes=[pltpu.VMEM((M, N), F32)],
        compiler_params=_cparams(("parallel", "arbitrary")),
    )(a, b)


def _ln_fwd(r, g, b):
    mu = jnp.mean(r, axis=-1, keepdims=True)
    xc = r - mu
    var = jnp.mean(xc * xc, axis=-1, keepdims=True)
    return xc * lax.rsqrt(var + LN_EPS) * g + b


def _ln_bwd(dy, r, g):
    mu = jnp.mean(r, axis=-1, keepdims=True)
    xc = r - mu
    var = jnp.mean(xc * xc, axis=-1, keepdims=True)
    rstd = lax.rsqrt(var + LN_EPS)
    xhat = xc * rstd
    dxh = dy * g
    m1 = jnp.mean(dxh, axis=-1, keepdims=True)
    m2 = jnp.mean(dxh * xhat, axis=-1, keepdims=True)
    return rstd * (dxh - m1 - xhat * m2), xhat


def _lane_is_head0():
    return lax.broadcasted_iota(jnp.int32, (1, LANES), 1) < HEAD_DIM


def _band_valid(i):
    r = lax.broadcasted_iota(jnp.int32, (A_TQ, A_WIN), 0)
    c = lax.broadcasted_iota(jnp.int32, (A_TQ, A_WIN), 1)
    a = r // CHUNK
    b = c // CHUNK
    return (b >= a) & (b <= a + LEFT_CHUNKS) & (c + i * A_TQ >= LEFT_CHUNKS * CHUNK)


def _band_probs(q2, k2, bias, valid, head0, h):
    qh = jnp.where(head0 if h == 0 else jnp.logical_not(head0), q2, jnp.zeros_like(q2))
    s = _dot_nt(qh, k2) * QK_SCALE + bias
    s = jnp.where(valid, s, NEG)
    m = jnp.max(s, axis=1, keepdims=True)
    e = jnp.exp(s - m)
    return e / jnp.sum(e, axis=1, keepdims=True)


def _attn_a_fwd(qkv, kvpad, bias, *, name):
    T = qkv.shape[0]
    n_hp = WIDTH // LANES

    def body(q_ref, k_ref, v_ref, bias_ref, o_ref):
        i = pl.program_id(1)
        row0 = pl.multiple_of(i * A_TQ, A_TQ)
        q2 = q_ref[...]
        k2 = k_ref[pl.ds(row0, A_WIN), :]
        v2 = v_ref[pl.ds(row0, A_WIN), :]
        head0 = _lane_is_head0()
        valid = _band_valid(i)
        outs = []
        for h in range(2):
            p = _band_probs(q2, k2, bias_ref[h], valid, head0, h)
            outs.append(_dot(p.astype(BF16), v2))
        o_ref[...] = jnp.where(head0, outs[0], outs[1]).astype(o_ref.dtype)

    return pl.pallas_call(
        body, name=name, grid=(n_hp, T // A_TQ),
        in_specs=[pl.BlockSpec((A_TQ, LANES), lambda hp, i: (i, hp)),
                  pl.BlockSpec((T + A_WIN - A_TQ, LANES), lambda hp, i: (0, hp)),
                  pl.BlockSpec((T + A_WIN - A_TQ, LANES), lambda hp, i: (0, hp + n_hp)),
                  pl.BlockSpec((2, A_TQ, A_WIN), lambda hp, i: (hp, 0, 0))],
        out_specs=pl.BlockSpec((A_TQ, LANES), lambda hp, i: (i, hp)),
        out_shape=jax.ShapeDtypeStruct((T, WIDTH), BF16),
        compiler_params=_cparams(("parallel", "arbitrary")),
    )(qkv, kvpad, kvpad, bias)


def _attn_a_bwd(qkv, kvpad, bias, do, *, name):
    T = qkv.shape[0]
    TP = T + A_WIN - A_TQ
    n_hp = WIDTH // LANES

    def body(q_ref, k_ref, v_ref, bias_ref, do_ref, dq_ref, dk_ref, dv_ref, db_ref):
        i = pl.program_id(1)

        @pl.when(i == 0)
        def _():
            dk_ref[...] = jnp.zeros_like(dk_ref)
            dv_ref[...] = jnp.zeros_like(dv_ref)
            db_ref[...] = jnp.zeros_like(db_ref)

        row0 = pl.multiple_of(i * A_TQ, A_TQ)
        q2 = q_ref[...]
        do2 = do_ref[...]
        k2 = k_ref[pl.ds(row0, A_WIN), :]
        v2 = v_ref[pl.ds(row0, A_WIN), :]
        head0 = _lane_is_head0()
        valid = _band_valid(i)
        dq, dk, dv = [], [], []
        for h in range(2):
            hm = head0 if h == 0 else jnp.logical_not(head0)
            p = _band_probs(q2, k2, bias_ref[h], valid, head0, h)
            doh = jnp.where(hm, do2, jnp.zeros_like(do2))
            dp = _dot_nt(doh, v2)
            delta = jnp.sum(p * dp, axis=1, keepdims=True)
            ds = p * (dp - delta)
            db_ref[h] += ds
            dsb = (ds * QK_SCALE).astype(BF16)
            dq.append(_dot(dsb, k2))
            dk.append(_dot_tn(dsb, q2))
            dv.append(_dot_tn(p.astype(BF16), do2))
        dq_ref[...] = jnp.where(head0, dq[0], dq[1]).astype(dq_ref.dtype)
        dk_ref[pl.ds(row0, A_WIN), :] += jnp.where(head0, dk[0], dk[1])
        dv_ref[pl.ds(row0, A_WIN), :] += jnp.where(head0, dv[0], dv[1])

    return pl.pallas_call(
        body, name=name, grid=(n_hp, T // A_TQ),
        in_specs=[pl.BlockSpec((A_TQ, LANES), lambda hp, i: (i, hp)),
                  pl.BlockSpec((TP, LANES), lambda hp, i: (0, hp)),
                  pl.BlockSpec((TP, LANES), lambda hp, i: (0, hp + n_hp)),
                  pl.BlockSpec((2, A_TQ, A_WIN), lambda hp, i: (hp, 0, 0)),
                  pl.BlockSpec((A_TQ, LANES), lambda hp, i: (i, hp))],
        out_specs=[pl.BlockSpec((A_TQ, LANES), lambda hp, i: (i, hp)),
                   pl.BlockSpec((TP, LANES), lambda hp, i: (0, hp)),
                   pl.BlockSpec((TP, LANES), lambda hp, i: (0, hp)),
                   pl.BlockSpec((2, A_TQ, A_WIN), lambda hp, i: (hp, 0, 0))],
        out_shape=[jax.ShapeDtypeStruct((T, WIDTH), BF16),
                   jax.ShapeDtypeStruct((TP, WIDTH), F32),
                   jax.ShapeDtypeStruct((TP, WIDTH), F32),
                   jax.ShapeDtypeStruct((WIDTH // HEAD_DIM, A_TQ, A_WIN), F32)],
        compiler_params=_cparams(("parallel", "arbitrary"), 56),
    )(qkv, kvpad, kvpad, bias, do)


def _toeplitz_bias(rb):
    H = rb.shape[0]
    span = A_TQ + A_WIN - 1
    n_tail = span - (N_REL - 1)
    ext = jnp.concatenate([rb[:, 1:], jnp.broadcast_to(rb[:, N_REL - 1:], (H, n_tail))], axis=1)
    rev = jnp.pad(ext[:, ::-1], ((0, 0), (0, 1)))
    flat = jnp.broadcast_to(rev[:, None, :], (H, A_TQ, span + 1)).reshape(H, A_TQ * (span + 1))
    skew = flat[:, :A_TQ * span].reshape(H, A_TQ, span)
    return skew[:, :, A_TQ - 1:A_TQ - 1 + A_WIN]


def _toeplitz_bias_grad(db):
    H = db.shape[0]
    span = A_TQ + A_WIN - 1
    d_skew = jnp.pad(db, ((0, 0), (0, 0), (A_TQ - 1, span - (A_TQ - 1) - A_WIN)))
    d_flat = jnp.pad(d_skew.reshape(H, A_TQ * span), ((0, 0), (0, A_TQ)))
    g_ext = jnp.sum(d_flat.reshape(H, A_TQ, span + 1), axis=1)[:, :span][:, ::-1]
    last = g_ext[:, N_REL - 2] + jnp.sum(g_ext[:, N_REL - 1:], axis=1)
    return jnp.concatenate([jnp.zeros((H, 1), F32), g_ext[:, :N_REL - 2], last[:, None]], axis=1)


def _split_bf16(x):
    hi = x.astype(BF16)
    lo = (x - hi.astype(F32)).astype(BF16)
    return hi, lo


def _sb_streams(d):
    out = []
    for h in range(2):
        for r in range(B_TQ // B_TS):
            if d is not None and d > r:
                continue
            out.append(dict(h=h, r=r, rows=pl.ds(r * B_TS, B_TS), diag=(d is not None and d == r)))
    return out


def _piece_rows(st, p):
    return pl.ds(st["r"] * B_TS + p, B_PIECE)


def _rows_cat(parts):
    return jnp.concatenate(parts, axis=0)


def _skewed(streams, stages):
    for t in range(len(streams) + len(stages) - 1):
        for s, st in enumerate(streams):
            if 0 <= t - s < len(stages):
                stages[t - s](st)


def _sb_logs(st, z2):
    log_beta, log_keep, keep_bf = [], [], []
    for p in range(0, B_TS, B_PIECE):
        z = z2[p:p + B_PIECE]
        lp2 = jnp.log(1.0 + jnp.exp2(-jnp.abs(z))) * LOG2E
        lb = jnp.minimum(z, 0.0) - lp2
        lk = lb - z
        if st["diag"]:
            lk = jnp.where(_strict_lower(p), lk, 0.0)
        log_beta.append(lb)
        log_keep.append(lk)
        keep_bf.append(lk.astype(BF16))
    st["log_beta"] = _rows_cat(log_beta)
    st["log_keep"] = _rows_cat(log_keep)
    st["keep_bf"] = _rows_cat(keep_bf)


def _strict_lower(p):
    t = p + lax.broadcasted_iota(jnp.int32, (B_PIECE, B_TS), 0)
    s = lax.broadcasted_iota(jnp.int32, (B_PIECE, B_TS), 1)
    return s < t


def _tri(strict):
    j = lax.broadcasted_iota(jnp.int32, (B_TS, B_TS), 0)
    s = lax.broadcasted_iota(jnp.int32, (B_TS, B_TS), 1)
    return jnp.where(j > s if strict else j >= s, 1.0, 0.0).astype(BF16)


def _call_carrying(job, body, *, name, grid, in_specs, out_specs, out_shape, scratch_shapes, vmem_mib, args):
    n_in, n_out, n_scr = len(in_specs), len(out_specs), len(scratch_shapes)
    if job is None:
        res = pl.pallas_call(body, name=name, grid=grid, in_specs=in_specs, out_specs=out_specs,
                             out_shape=out_shape, scratch_shapes=scratch_shapes,
                             compiler_params=_cparams(("arbitrary",) * len(grid), vmem_mib))(*args)
        return res, []
    j_in, j_out = len(job.arrays), len(job.out_shape)
    hbm = pl.BlockSpec(memory_space=pltpu.HBM)

    def carrying(*refs):
        refs = list(refs)
        ins, refs = refs[:n_in], refs[n_in:]
        j_ins, refs = refs[:j_in], refs[j_in:]
        outs, refs = refs[:n_out], refs[n_out:]
        j_outs, refs = refs[:j_out], refs[j_out:]
        scr, sems = refs[:n_scr], refs[n_scr:]
        first = functools.reduce(jnp.logical_and, [pl.program_id(d) == 0 for d in range(len(grid))])
        last = functools.reduce(jnp.logical_and, [pl.program_id(d) == grid[d] - 1 for d in range(len(grid))])

        @pl.when(first)
        def _():
            job.start(j_ins, j_outs, sems)

        body(*ins, *outs, *scr)

        @pl.when(last)
        def _():
            job.wait(j_ins, j_outs, sems)

    res = pl.pallas_call(
        carrying, name=name, grid=grid,
        in_specs=list(in_specs) + [hbm] * j_in, out_specs=list(out_specs) + [hbm] * j_out,
        out_shape=list(out_shape) + job.out_shape, scratch_shapes=list(scratch_shapes) + job.scratch(),
        compiler_params=_cparams(("arbitrary",) * len(grid), vmem_mib))(*args, *job.arrays)
    return res[:n_out], res[n_out:]


def _attn_b_fwd(qkv, *, col0, name, job=None):
    T = qkv.shape[0]
    n_hp = WIDTH // LANES
    sub = B_TQ // B_TS

    def body(q_ref, k_ref, v_ref, o_ref, acc_ref, car_ref, qh_ref):
        i = pl.program_id(1)
        q2 = q_ref[...]
        head0 = _lane_is_head0()
        qh_ref[0] = jnp.where(head0, q2, jnp.zeros_like(q2))
        qh_ref[1] = jnp.where(head0, jnp.zeros_like(q2), q2)
        tri_s = _tri(True)
        acc_ref[...] = jnp.zeros_like(acc_ref)
        car_ref[...] = jnp.zeros_like(car_ref)

        def tile(kb, d):
            k0 = pl.multiple_of(kb * B_TS, B_TS)
            k2 = k_ref[pl.ds(k0, B_TS), :]
            v2 = v_ref[pl.ds(k0, B_TS), :]

            def scores(st):
                st["z2"] = _dot_nt(qh_ref[st["h"], st["rows"], :], k2) * (QK_SCALE * LOG2E)

            def logs(st):
                _sb_logs(st, st.pop("z2"))

            def suffix(st):
                st["suffix"] = _dot(st.pop("keep_bf"), tri_s)

            def weights(st):
                log_beta, suffix, log_keep = st.pop("log_beta"), st.pop("suffix"), st.pop("log_keep")
                wb = []
                for p in range(0, B_TS, B_PIECE):
                    rows = _piece_rows(st, p)
                    car = car_ref[st["h"], rows, :]
                    w = jnp.exp2(log_beta[p:p + B_PIECE] + suffix[p:p + B_PIECE] + car)
                    if st["diag"]:
                        w = jnp.where(_strict_lower(p), w, 0.0)
                    wb.append(w.astype(BF16))
                    car_ref[st["h"], rows, :] = car + jnp.sum(log_keep[p:p + B_PIECE], axis=1, keepdims=True)
                st["wb"] = _rows_cat(wb)

            def values(st):
                acc_ref[st["h"], st["rows"], :] += _dot(st.pop("wb"), v2)

            _skewed(_sb_streams(d), [scores, logs, suffix, weights, values])

        for d in reversed(range(sub)):
            tile(i * sub + d, d)

        def step(n, carry):
            tile(i * sub - 1 - n, None)
            return carry

        lax.fori_loop(0, i * sub, step, 0)
        o_ref[...] = jnp.where(head0, acc_ref[0], acc_ref[1])

    (out,), rode = _call_carrying(
        job, body, name=name, grid=(n_hp, T // B_TQ),
        in_specs=[pl.BlockSpec((B_TQ, LANES), lambda hp, i: (i, hp + col0)),
                  pl.BlockSpec((T, LANES), lambda hp, i: (0, hp + col0 + n_hp)),
                  pl.BlockSpec((T, LANES), lambda hp, i: (0, hp + col0 + 2 * n_hp))],
        out_specs=[pl.BlockSpec((B_TQ, LANES), lambda hp, i: (i, hp))],
        out_shape=[jax.ShapeDtypeStruct((T, WIDTH), F32)],
        scratch_shapes=[pltpu.VMEM((2, B_TQ, LANES), F32), pltpu.VMEM((2, B_TQ, 1), F32),
                        pltpu.VMEM((2, B_TQ, LANES), BF16)],
        vmem_mib=48, args=(qkv, qkv, qkv))
    return out, rode


def _attn_b_bwd(qkv, out, do, *, col0, name, job=None):
    T = qkv.shape[0]
    n_hp = WIDTH // LANES
    sub = B_TQ // B_TS

    def body(q_ref, k_ref, v_ref, o_ref, do_ref, dq_ref, dk_ref, dv_ref,
             dqa_ref, car_ref, carr_ref, tot_ref, qh_ref, doh_ref, qs_ref):
        i = pl.program_id(1)

        @pl.when(i == 0)
        def _():
            dk_ref[...] = jnp.zeros_like(dk_ref)
            dv_ref[...] = jnp.zeros_like(dv_ref)

        q2 = q_ref[...]
        do2 = do_ref[...]
        head0 = _lane_is_head0()
        zero = jnp.zeros_like(q2)
        qh_ref[0] = jnp.where(head0, q2, zero)
        qh_ref[1] = jnp.where(head0, zero, q2)
        doh_ref[0] = jnp.where(head0, do2, zero)
        doh_ref[1] = jnp.where(head0, zero, do2)
        scale = jnp.asarray(QK_SCALE, BF16)
        qs_ref[...] = q2 * scale
        tri_s = _tri(True)
        tri_i = _tri(False)
        prod = do2.astype(F32) * o_ref[...]
        tot_ref[0] = jnp.sum(jnp.where(head0, prod, 0.0), axis=1, keepdims=True)
        tot_ref[1] = jnp.sum(jnp.where(head0, 0.0, prod), axis=1, keepdims=True)
        dqa_ref[...] = jnp.zeros_like(dqa_ref)
        car_ref[...] = jnp.zeros_like(car_ref)
        carr_ref[...] = jnp.zeros_like(carr_ref)

        def tile(kb, d):
            k0 = pl.multiple_of(kb * B_TS, B_TS)
            keys = pl.ds(k0, B_TS)
            k2 = k_ref[keys, :]
            v2 = v_ref[keys, :]
            k2s = k2 * scale

            def scores(st):
                st["z2"] = _dot_nt(qh_ref[st["h"], st["rows"], :], k2) * (QK_SCALE * LOG2E)
                st["dw"] = _dot_nt(doh_ref[st["h"], st["rows"], :], v2)

            def logs(st):
                _sb_logs(st, st.pop("z2"))

            def suffix(st):
                st["suffix"] = _dot(st.pop("keep_bf"), tri_s)

            def weights(st):
                h = st["h"]
                suffix, dw = st.pop("suffix"), st.pop("dw")
                wb, dlog, hi, lo = [], [], [], []
                for p in range(0, B_TS, B_PIECE):
                    rows = _piece_rows(st, p)
                    car = car_ref[h, rows, :]
                    w = jnp.exp2(st["log_beta"][p:p + B_PIECE] + suffix[p:p + B_PIECE] + car)
                    if st["diag"]:
                        w = jnp.where(_strict_lower(p), w, 0.0)
                    w = w.astype(BF16)
                    dl = w.astype(F32) * dw[p:p + B_PIECE]
                    dl_hi, dl_lo = _split_bf16(dl)
                    wb.append(w)
                    dlog.append(dl)
                    hi.append(dl_hi)
                    lo.append(dl_lo)
                    car_ref[h, rows, :] = car + jnp.sum(st["log_keep"][p:p + B_PIECE], axis=1, keepdims=True)
                st["wb"], st["dlog"], st["hi"], st["lo"] = _rows_cat(wb), _rows_cat(dlog), _rows_cat(hi), _rows_cat(lo)

            def later(st):
                st["later"] = _dot(st.pop("hi"), tri_i) + _dot(st.pop("lo"), tri_i)

            def dscores(st):
                h = st["h"]
                later, dlog = st.pop("later"), st.pop("dlog")
                log_keep, log_beta = st.pop("log_keep"), st.pop("log_beta")
                dzb = []
                for p in range(0, B_TS, B_PIECE):
                    rows = _piece_rows(st, p)
                    pc = slice(p, p + B_PIECE)
                    carr = carr_ref[h, rows, :]
                    earlier = tot_ref[h, rows, :] - (later[pc] + carr)
                    dz = dlog[pc] * jnp.exp2(log_keep[pc]) - jnp.exp2(log_beta[pc]) * earlier
                    if st["diag"]:
                        dz = jnp.where(_strict_lower(p), dz, 0.0)
                    dzb.append(dz.astype(BF16))
                    carr_ref[h, rows, :] = carr + jnp.sum(dlog[pc], axis=1, keepdims=True)
                st["dzb"] = _rows_cat(dzb)

            def grads(st):
                h, rows = st["h"], st["rows"]
                mine = head0 if h == 0 else jnp.logical_not(head0)
                dzb = st.pop("dzb")
                dqa_ref[h, rows, :] += _dot(dzb, k2s)
                dk_ref[keys, :] += jnp.where(mine, _dot_tn(dzb, qs_ref[rows, :]), 0.0)
                dv_ref[keys, :] += jnp.where(mine, _dot_tn(st.pop("wb"), do_ref[rows, :]), 0.0)

            _skewed(_sb_streams(d), [scores, logs, suffix, weights, later, dscores, grads])

        for d in reversed(range(sub)):
            tile(i * sub + d, d)

        def step(n, carry):
            tile(i * sub - 1 - n, None)
            return carry

        lax.fori_loop(0, i * sub, step, 0)
        dq_ref[...] = jnp.where(head0, dqa_ref[0], dqa_ref[1]).astype(dq_ref.dtype)

    return _call_carrying(
        job, body, name=name, grid=(n_hp, T // B_TQ),
        in_specs=[pl.BlockSpec((B_TQ, LANES), lambda hp, i: (i, hp + col0)),
                  pl.BlockSpec((T, LANES), lambda hp, i: (0, hp + col0 + n_hp)),
                  pl.BlockSpec((T, LANES), lambda hp, i: (0, hp + col0 + 2 * n_hp)),
                  pl.BlockSpec((B_TQ, LANES), lambda hp, i: (i, hp)),
                  pl.BlockSpec((B_TQ, LANES), lambda hp, i: (i, hp))],
        out_specs=[pl.BlockSpec((B_TQ, LANES), lambda hp, i: (i, hp)),
                   pl.BlockSpec((T, LANES), lambda hp, i: (0, hp)),
                   pl.BlockSpec((T, LANES), lambda hp, i: (0, hp))],
        out_shape=[jax.ShapeDtypeStruct((T, WIDTH), BF16),
                   jax.ShapeDtypeStruct((T, WIDTH), F32),
                   jax.ShapeDtypeStruct((T, WIDTH), F32)],
        scratch_shapes=[pltpu.VMEM((2, B_TQ, LANES), F32), pltpu.VMEM((2, B_TQ, 1), F32),
                        pltpu.VMEM((2, B_TQ, 1), F32), pltpu.VMEM((2, B_TQ, 1), F32),
                        pltpu.VMEM((2, B_TQ, LANES), BF16), pltpu.VMEM((2, B_TQ, LANES), BF16),
                        pltpu.VMEM((B_TQ, LANES), BF16)],
        vmem_mib=56, args=(qkv, qkv, qkv, out, do))


def _gated_mix(oa_ref, ob_ref, g_ref, bg_ref, wpa_ref, wpb_ref, D):
    ya = _dot(oa_ref[...].astype(BF16), wpa_ref[...])
    yb = _dot(ob_ref[...].astype(BF16), wpb_ref[...])
    sa = jax.nn.sigmoid(g_ref[:, :D] + bg_ref[:, :D])
    sb = jax.nn.sigmoid(g_ref[:, D:] + bg_ref[:, D:])
    return ya, yb, sa, sb


def _proj_fwd(oa, ob, g, bg, wpa, wpb, wo, xin, lng, lnb, layer, *, alpha, name):
    T, D = xin.shape
    tm = _tile(T, 512)
    row = lambda i: (i, 0)
    wspec = lambda r, c: pl.BlockSpec((None, r, c), lambda i: (layer, 0, 0))
    vec = lambda c: pl.BlockSpec((None, 1, c), lambda i: (layer, 0, 0))

    def body(oa_ref, ob_ref, g_ref, bg_ref, wpa_ref, wpb_ref, wo_ref, x_ref, lg_ref, lb_ref, x1_ref, r1_ref):
        ya, yb, sa, sb = _gated_mix(oa_ref, ob_ref, g_ref, bg_ref, wpa_ref, wpb_ref, D)
        mix = _dot((sa * ya + sb * yb).astype(BF16), wo_ref[...])
        r1 = alpha * x_ref[...] + mix
        r1_ref[...] = r1
        x1_ref[...] = _ln_fwd(r1, lg_ref[...], lb_ref[...])

    return pl.pallas_call(
        body, name=name, grid=(T // tm,),
        in_specs=[pl.BlockSpec((tm, WIDTH), row), pl.BlockSpec((tm, WIDTH), row), pl.BlockSpec((tm, 2 * D), row),
                  vec(2 * D), wspec(WIDTH, D), wspec(WIDTH, D), wspec(D, D),
                  pl.BlockSpec((tm, D), row), vec(D), vec(D)],
        out_specs=[pl.BlockSpec((tm, D), row), pl.BlockSpec((tm, D), row)],
        out_shape=[jax.ShapeDtypeStruct((T, D), F32), jax.ShapeDtypeStruct((T, D), F32)],
        compiler_params=_cparams(("arbitrary",), 56),
    )(oa, ob, g, bg, wpa, wpb, wo, xin, lng, lnb)


def _proj_bwd(dx1, r1, lng, oa, ob, g, bg, wpa, wpb, wo, layer, *, name):
    T, D = dx1.shape
    tm = _tile(T, 512)
    row = lambda i: (i, 0)
    fixed = lambda i: (0, 0)
    wspec = lambda r, c: pl.BlockSpec((None, r, c), lambda i: (layer, 0, 0))
    vec = lambda c: pl.BlockSpec((None, 1, c), lambda i: (layer, 0, 0))

    def body(dx_ref, r1_ref, lg_ref, oa_ref, ob_ref, g_ref, bg_ref, wpa_ref, wpb_ref, wo_ref,
             dr_ref, mix_ref, dya_ref, dyb_ref, dg_ref, doa_ref, dob_ref, dlg_ref, dlb_ref, dbg_ref):
        @pl.when(pl.program_id(0) == 0)
        def _():
            dlg_ref[...] = jnp.zeros_like(dlg_ref)
            dlb_ref[...] = jnp.zeros_like(dlb_ref)
            dbg_ref[...] = jnp.zeros_like(dbg_ref)

        dx = dx_ref[...]
        dr, xhat = _ln_bwd(dx, r1_ref[...], lg_ref[...])
        dr_ref[...] = dr
        dlg_ref[...] += jnp.sum(dx * xhat, axis=0, keepdims=True)
        dlb_ref[...] += jnp.sum(dx, axis=0, keepdims=True)
        dmix = _dot_nt(dr.astype(BF16), wo_ref[...])
        ya, yb, sa, sb = _gated_mix(oa_ref, ob_ref, g_ref, bg_ref, wpa_ref, wpb_ref, D)
        mix_ref[...] = (sa * ya + sb * yb).astype(BF16)
        dya = (dmix * sa).astype(BF16)
        dyb = (dmix * sb).astype(BF16)
        dya_ref[...] = dya
        dyb_ref[...] = dyb
        dga = dmix * ya * (sa * (1.0 - sa))
        dgb = dmix * yb * (sb * (1.0 - sb))
        dg_ref[:, :D] = dga.astype(BF16)
        dg_ref[:, D:] = dgb.astype(BF16)
        dbg_ref[:, :D] += jnp.sum(dga, axis=0, keepdims=True)
        dbg_ref[:, D:] += jnp.sum(dgb, axis=0, keepdims=True)
        doa_ref[...] = _dot_nt(dya, wpa_ref[...]).astype(BF16)
        dob_ref[...] = _dot_nt(dyb, wpb_ref[...]).astype(BF16)

    return pl.pallas_call(
        body, name=name, grid=(T // tm,),
        in_specs=[pl.BlockSpec((tm, D), row), pl.BlockSpec((tm, D), row), vec(D),
                  pl.BlockSpec((tm, WIDTH), row), pl.BlockSpec((tm, WIDTH), row), pl.BlockSpec((tm, 2 * D), row),
                  vec(2 * D), wspec(WIDTH, D), wspec(WIDTH, D), wspec(D, D)],
        out_specs=[pl.BlockSpec((tm, D), row), pl.BlockSpec((tm, D), row), pl.BlockSpec((tm, D), row),
                   pl.BlockSpec((tm, D), row), pl.BlockSpec((tm, 2 * D), row),
                   pl.BlockSpec((tm, WIDTH), row), pl.BlockSpec((tm, WIDTH), row),
                   pl.BlockSpec((1, D), fixed), pl.BlockSpec((1, D), fixed), pl.BlockSpec((1, 2 * D), fixed)],
        out_shape=[jax.ShapeDtypeStruct((T, D), F32), jax.ShapeDtypeStruct((T, D), BF16),
                   jax.ShapeDtypeStruct((T, D), BF16), jax.ShapeDtypeStruct((T, D), BF16),
                   jax.ShapeDtypeStruct((T, 2 * D), BF16),
                   jax.ShapeDtypeStruct((T, WIDTH), BF16), jax.ShapeDtypeStruct((T, WIDTH), BF16),
                   jax.ShapeDtypeStruct((1, D), F32), jax.ShapeDtypeStruct((1, D), F32),
                   jax.ShapeDtypeStruct((1, 2 * D), F32)],
        compiler_params=_cparams(("arbitrary",), 56),
    )(dx1, r1, lng, oa, ob, g, bg, wpa, wpb, wo)


def _ffn_fwd(x1, wfi, wfo, lng, lnb, layer, *, alpha, name):
    T, D = x1.shape
    tf = wfi.shape[-1]
    nj = wfi.shape[0] // 2
    tm = _tile(T, 512)
    vec = lambda c: pl.BlockSpec((None, 1, c), lambda i, j: (layer, 0, 0))

    def body(x_ref, wg_ref, wu_ref, wo_ref, lg_ref, lb_ref, gs_ref, us_ref, r2_ref, x2_ref, acc_ref, xb_ref):
        j = pl.program_id(1)

        @pl.when(j == 0)
        def _():
            xb_ref[...] = x_ref[...].astype(BF16)
            acc_ref[...] = jnp.zeros_like(acc_ref)

        gv = _dot(xb_ref[...], wg_ref[...])
        uv = _dot(xb_ref[...], wu_ref[...])
        gs_ref[...] = gv
        us_ref[...] = uv
        act = gv * jax.nn.sigmoid(gv) * uv
        acc_ref[...] += _dot(act.astype(BF16), wo_ref[...])

        @pl.when(j == nj - 1)
        def _():
            r2 = alpha * x_ref[...] + acc_ref[...]
            r2_ref[...] = r2
            x2_ref[...] = _ln_fwd(r2, lg_ref[...], lb_ref[...])

    return pl.pallas_call(
        body, name=name, grid=(T // tm, nj),
        in_specs=[pl.BlockSpec((tm, D), lambda i, j: (i, 0)),
                  pl.BlockSpec((None, None, D, tf), lambda i, j: (j, layer, 0, 0)),
                  pl.BlockSpec((None, None, D, tf), lambda i, j: (j + nj, layer, 0, 0)),
                  pl.BlockSpec((None, tf, D), lambda i, j: (layer, j, 0)),
                  vec(D), vec(D)],
        out_specs=[pl.BlockSpec((None, tm, tf), lambda i, j: (j, i, 0)),
                   pl.BlockSpec((None, tm, tf), lambda i, j: (j, i, 0)),
                   pl.BlockSpec((tm, D), lambda i, j: (i, 0)),
                   pl.BlockSpec((tm, D), lambda i, j: (i, 0))],
        out_shape=[jax.ShapeDtypeStruct((nj, T, tf), F32), jax.ShapeDtypeStruct((nj, T, tf), F32),
                   jax.ShapeDtypeStruct((T, D), F32), jax.ShapeDtypeStruct((T, D), F32)],
        scratch_shapes=[pltpu.VMEM((tm, D), F32), pltpu.VMEM((tm, D), BF16)],
        compiler_params=_cparams(("parallel", "arbitrary"), 56),
    )(x1, wfi, wfi, wfo, lng, lnb)


def _ffn_bwd(dx2, r2, lng, gs, us, wfi, wfo, layer, *, alpha, name):
    T, D = dx2.shape
    tf = wfi.shape[-1]
    nj = wfi.shape[0] // 2
    tm = _tile(T, 512)
    vec = lambda c: pl.BlockSpec((None, 1, c), lambda i, j: (layer, 0, 0))
    blk = lambda: pl.BlockSpec((None, tm, tf), lambda i, j: (j, i, 0))

    def body(dx_ref, r2_ref, lg_ref, gs_ref, us_ref, wg_ref, wu_ref, wo_ref,
             dr_ref, act_ref, dg_ref, du_ref, dx1_ref, dlg_ref, dlb_ref, acc_ref, drb_ref):
        i = pl.program_id(0)
        j = pl.program_id(1)

        @pl.when((i == 0) & (j == 0))
        def _():
            dlg_ref[...] = jnp.zeros_like(dlg_ref)
            dlb_ref[...] = jnp.zeros_like(dlb_ref)

        @pl.when(j == 0)
        def _():
            dx = dx_ref[...]
            dr, xhat = _ln_bwd(dx, r2_ref[...], lg_ref[...])
            dlg_ref[...] += jnp.sum(dx * xhat, axis=0, keepdims=True)
            dlb_ref[...] += jnp.sum(dx, axis=0, keepdims=True)
            drb_ref[...] = dr.astype(BF16)
            dr_ref[...] = dr.astype(BF16)
            acc_ref[...] = alpha * dr

        dact = _dot_nt(drb_ref[...], wo_ref[...])
        gv = gs_ref[...]
        uv = us_ref[...]
        s = jax.nn.sigmoid(gv)
        silu = gv * s
        act_ref[...] = (silu * uv).astype(BF16)
        dg = (dact * uv * (s * (1.0 + gv * (1.0 - s)))).astype(BF16)
        du = (dact * silu).astype(BF16)
        dg_ref[...] = dg
        du_ref[...] = du
        acc_ref[...] += _dot_nt(dg, wg_ref[...]) + _dot_nt(du, wu_ref[...])

        @pl.when(j == nj - 1)
        def _():
            dx1_ref[...] = acc_ref[...]

    return pl.pallas_call(
        body, name=name, grid=(T // tm, nj),
        in_specs=[pl.BlockSpec((tm, D), lambda i, j: (i, 0)), pl.BlockSpec((tm, D), lambda i, j: (i, 0)), vec(D),
                  blk(), blk(),
                  pl.BlockSpec((None, None, D, tf), lambda i, j: (j, layer, 0, 0)),
                  pl.BlockSpec((None, None, D, tf), lambda i, j: (j + nj, layer, 0, 0)),
                  pl.BlockSpec((None, tf, D), lambda i, j: (layer, j, 0))],
        out_specs=[pl.BlockSpec((tm, D), lambda i, j: (i, 0)), blk(), blk(), blk(),
                   pl.BlockSpec((tm, D), lambda i, j: (i, 0)),
                   pl.BlockSpec((1, D), lambda i, j: (0, 0)), pl.BlockSpec((1, D), lambda i, j: (0, 0))],
        out_shape=[jax.ShapeDtypeStruct((T, D), BF16),
                   jax.ShapeDtypeStruct((nj, T, tf), BF16), jax.ShapeDtypeStruct((nj, T, tf), BF16),
                   jax.ShapeDtypeStruct((nj, T, tf), BF16),
                   jax.ShapeDtypeStruct((T, D), F32),
                   jax.ShapeDtypeStruct((1, D), F32), jax.ShapeDtypeStruct((1, D), F32)],
        scratch_shapes=[pltpu.VMEM((tm, D), F32), pltpu.VMEM((tm, D), BF16)],
        compiler_params=_cparams(("arbitrary", "arbitrary"), 56),
    )(dx2, r2, lng, gs, us, wfi, wfi, wfo)


def _loss_head(y, target, *, name):
    T, D = y.shape
    tm = _tile(T, 1024)

    def body(y_ref, t_ref, dy_ref, sq_ref):
        @pl.when(pl.program_id(0) == 0)
        def _():
            sq_ref[...] = jnp.zeros_like(sq_ref)
        err = y_ref[...] - t_ref[...]
        dy_ref[...] = err * (1.0 / D)
        sq_ref[...] += jnp.sum(err * err, axis=0, keepdims=True)

    return pl.pallas_call(
        body, name=name, grid=(T // tm,),
        in_specs=[pl.BlockSpec((tm, D), lambda i: (i, 0)), pl.BlockSpec((tm, D), lambda i: (i, 0))],
        out_specs=[pl.BlockSpec((tm, D), lambda i: (i, 0)), pl.BlockSpec((1, D), lambda i: (0, 0))],
        out_shape=[jax.ShapeDtypeStruct((T, D), F32), jax.ShapeDtypeStruct((1, D), F32)],
        compiler_params=_cparams(("arbitrary",)),
    )(y, target)


def _my_place():
    return lax.axis_index("x"), lax.axis_index("y"), lax.axis_index("c")


def _peer(place, k):
    x, y, c = place
    return (1 - x if k & 4 else x, 1 - y if k & 2 else y, 1 - c if k & 1 else c)


def _logical(place):
    x, y, c = place
    return 4 * x + 2 * y + c


def _block_of(ref, mode, idx):
    if mode == "blk":
        return ref.at[idx]
    if mode == "col":
        size = ref.shape[2] // N_DEV
        return ref.at[:, :, pl.ds(pl.multiple_of(idx * size, size), size)]
    size = ref.shape[1] // N_DEV
    return ref.at[:, pl.ds(pl.multiple_of(idx * size, size), size), :]


def _full_shape(shard, mode):
    if mode == "blk":
        return (N_DEV,) + shard.shape
    if mode == "col":
        return shard.shape[:2] + (N_DEV * shard.shape[2],)
    return (shard.shape[0], N_DEV * shard.shape[1], shard.shape[2])


class _Exchange:
    def __init__(self, arrays, out_shape, build):
        self.arrays = list(arrays)
        self.out_shape = list(out_shape)
        self.build = build

    def scratch(self):
        n = len(self.arrays)
        return [pltpu.SemaphoreType.DMA((n * N_DEV,)), pltpu.SemaphoreType.DMA((n * N_DEV,)),
                pltpu.SemaphoreType.DMA((n,))]

    def start(self, ins, outs, sems):
        for cp in self.build(ins, outs, *sems):
            cp.start()

    def wait(self, ins, outs, sems):
        for cp in self.build(ins, outs, *sems):
            cp.wait()

    def run(self, name):
        n_in, n_out = len(self.arrays), len(self.out_shape)
        hbm = pl.BlockSpec(memory_space=pltpu.HBM)

        def body(*refs):
            ins, outs, sems = refs[:n_in], refs[n_in:n_in + n_out], refs[n_in + n_out:]
            self.start(ins, outs, sems)
            self.wait(ins, outs, sems)

        return pl.pallas_call(
            body, name=name, in_specs=[hbm] * n_in, out_specs=[hbm] * n_out,
            out_shape=self.out_shape, scratch_shapes=self.scratch(),
        )(*self.arrays)


def _copies_to_all(src_of, dst_of, n, send, recv, local):
    me = _my_place()
    copies = []
    for a in range(n):
        copies.append(pltpu.make_async_copy(src_of(a, _logical(me)), dst_of(a), local.at[a]))
        for k in range(1, N_DEV):
            peer = _peer(me, k)
            copies.append(pltpu.make_async_remote_copy(
                src_ref=src_of(a, _logical(peer)), dst_ref=dst_of(a),
                send_sem=send.at[a * N_DEV + k], recv_sem=recv.at[a * N_DEV + k],
                device_id=peer, device_id_type=MESH))
    return copies


def _gather_job(shards, modes):
    def build(ins, outs, send, recv, local):
        my_id = _logical(_my_place())
        return _copies_to_all(lambda a, dev: ins[a], lambda a: _block_of(outs[a], modes[a], my_id),
                              len(shards), send, recv, local)

    return _Exchange(shards, [jax.ShapeDtypeStruct(_full_shape(s, m), s.dtype) for s, m in zip(shards, modes)], build)


def _grad_block(ref, mode, idx):
    if mode == "blk":
        return ref.at[idx]
    if mode == "col":
        size = ref.shape[1] // N_DEV
        return ref.at[:, pl.ds(pl.multiple_of(idx * size, size), size)]
    size = ref.shape[0] // N_DEV
    return ref.at[pl.ds(pl.multiple_of(idx * size, size), size), :]


def _grad_shard_shape(g, mode):
    if mode == "blk":
        return g.shape[1:]
    if mode == "col":
        return (g.shape[0], g.shape[1] // N_DEV)
    return (g.shape[0] // N_DEV, g.shape[1])


def _grads_job(groups, modes):
    flat = [(g, w, l) for w, per_w in enumerate(groups) for l, g in enumerate(per_w)]

    def build(ins, outs, send, recv, local):
        my_id = _logical(_my_place())
        return _copies_to_all(lambda a, dev: _grad_block(ins[a], modes[flat[a][1]], dev),
                              lambda a: outs[flat[a][1]].at[my_id, flat[a][2]],
                              len(flat), send, recv, local)

    out_shape = [jax.ShapeDtypeStruct((N_DEV, len(per_w)) + _grad_shard_shape(per_w[0], m), per_w[0].dtype)
                 for per_w, m in zip(groups, modes)]
    return _Exchange([g for g, _, _ in flat], out_shape, build)


def _adamw(w, g, m, v):
    m = ADAM_B1 * m + (1.0 - ADAM_B1) * g
    v = ADAM_B2 * v + (1.0 - ADAM_B2) * (g * g)
    m_hat = m / (1.0 - ADAM_B1 ** ADAM_STEP)
    v_hat = v / (1.0 - ADAM_B2 ** ADAM_STEP)
    delta = -ADAM_LR * (m_hat / (jnp.sqrt(v_hat) + ADAM_EPS) + ADAM_WD * w)
    return delta, m, v


def _sum_slots_adamw(slots, w, m, v, *, name):
    R, C = w.shape
    tr = _tile(R, 256)

    def body(s_ref, w_ref, m_ref, v_ref, g_out, d_out, m_out, v_out):
        g = s_ref[0].astype(F32)
        for s in range(1, N_DEV):
            g = g + s_ref[s].astype(F32)
        delta, m_new, v_new = _adamw(w_ref[...], g, m_ref[...], v_ref[...])
        g_out[...] = g
        d_out[...] = delta
        m_out[...] = m_new
        v_out[...] = v_new

    spec = pl.BlockSpec((tr, C), lambda i: (i, 0))
    return pl.pallas_call(
        body, name=name, grid=(R // tr,),
        in_specs=[pl.BlockSpec((N_DEV, tr, C), lambda i: (0, i, 0)), spec, spec, spec],
        out_specs=[spec] * 4,
        out_shape=[jax.ShapeDtypeStruct((R, C), F32)] * 4,
        compiler_params=_cparams(("parallel",)),
    )(slots, w, m, v)


def _small_allreduce_adamw(g, w, m, v, *, name):
    R = g.shape[0]
    vmem = pl.BlockSpec(memory_space=pltpu.VMEM)

    def body(g_ref, w_ref, m_ref, v_ref, g_out, d_out, m_out, v_out, slots, send, recv):
        me = _my_place()
        my_id = _logical(me)
        slots[my_id] = g_ref[...]
        copies = []
        for k in range(1, N_DEV):
            cp = pltpu.make_async_remote_copy(
                src_ref=g_ref, dst_ref=slots.at[my_id], send_sem=send.at[k], recv_sem=recv.at[k],
                device_id=_peer(me, k), device_id_type=MESH)
            cp.start()
            copies.append(cp)
        for cp in copies:
            cp.wait()
        total = slots[0]
        for s in range(1, N_DEV):
            total = total + slots[s]
        delta, m_new, v_new = _adamw(w_ref[...], total, m_ref[...], v_ref[...])
        g_out[...] = total
        d_out[...] = delta
        m_out[...] = m_new
        v_out[...] = v_new

    return pl.pallas_call(
        body, name=name,
        in_specs=[vmem] * 4, out_specs=[vmem] * 4,
        out_shape=[jax.ShapeDtypeStruct((R, LANES), F32)] * 4,
        scratch_shapes=[pltpu.VMEM((N_DEV, R, LANES), F32),
                        pltpu.SemaphoreType.DMA((N_DEV,)), pltpu.SemaphoreType.DMA((N_DEV,))],
    )(g, w, m, v)


def _pack(parts):
    flat = jnp.concatenate([p.reshape(-1) for p in parts])
    rows = -(-flat.shape[0] // (8 * LANES)) * 8
    return jnp.pad(flat, (0, rows * LANES - flat.shape[0])).reshape(rows, LANES)


def _unpack(packed, like):
    flat = packed.reshape(-1)
    out, pos = [], 0
    for p in like:
        out.append(flat[pos:pos + p.size].reshape(p.shape))
        pos += p.size
    return out


def kernel(x, w_in, b_gate, rel_bias, w_proj_a, w_proj_b, w_out, ln1_g, ln1_b, w_ffn_in, w_ffn_out, ln2_g, ln2_b, loss_target, m_w_in, m_b_gate, m_rel_bias, m_w_proj_a, m_w_proj_b, m_w_out, m_ln1_g, m_ln1_b, m_w_ffn_in, m_w_ffn_out, m_ln2_g, m_ln2_b, v_w_in, v_b_gate, v_rel_bias, v_w_proj_a, v_w_proj_b, v_w_out, v_ln1_g, v_ln1_b, v_w_ffn_in, v_w_ffn_out, v_ln2_g, v_ln2_b):
    L = w_in.shape[0]
    T, D = x.shape[1], x.shape[2]
    alpha = float((2 * L) ** 0.25)
    n_qkv = 6 * WIDTH

    big = [w_in, w_proj_a, w_proj_b, w_out, w_ffn_in, w_ffn_out]
    modes = ["col", "col", "col", "row", "blk", "row"]
    w_bf = [w.astype(BF16) for w in big]
    (W_in_first,) = _gather_job([w_bf[0][:1]], modes[:1]).run("gather_w_in_first")
    later = ([w_bf[0][1:]] if L > 1 else []) + w_bf[1:]
    later_modes = (modes[:1] if L > 1 else []) + modes[1:]
    vec3 = lambda a: a[:, None, :]
    bg3, l1g, l1b, l2g, l2b = vec3(b_gate), vec3(ln1_g), vec3(ln1_b), vec3(ln2_g), vec3(ln2_b)
    b_col0 = 3 * WIDTH // LANES

    h = x[0]
    saved = []
    for l in range(L):
        W_in, wl = (W_in_first, 0) if l == 0 else (W_in_later, l - 1)
        qkv = _mm_nn(h, W_in, wl, col_off=0, n_cols=n_qkv, out_dtype=BF16, name=f"in_proj_qkv_{l}")
        gates = _mm_nn(h, W_in, wl, col_off=n_qkv, n_cols=2 * D, out_dtype=F32, name=f"in_proj_gates_{l}")
        kvpad = jnp.pad(qkv[:, WIDTH:3 * WIDTH], ((A_WIN - A_TQ, 0), (0, 0)))
        bias = _toeplitz_bias(rel_bias[l])
        oa = _attn_a_fwd(qkv, kvpad, bias, name=f"attn_a_fwd_{l}")
        if l == 0:
            ob, gathered = _attn_b_fwd(qkv, col0=b_col0, name=f"attn_b_fwd_{l}", job=_gather_job(later, later_modes))
            if L > 1:
                W_in_later = gathered[0]
            W_pa, W_pb, W_o, W_fi, W_fo = gathered[-5:]
        else:
            ob, _ = _attn_b_fwd(qkv, col0=b_col0, name=f"attn_b_fwd_{l}")
        x1, r1 = _proj_fwd(oa, ob, gates, bg3, W_pa, W_pb, W_o, h, l1g, l1b, l, alpha=alpha, name=f"proj_fwd_{l}")
        gs, us, r2, x2 = _ffn_fwd(x1, W_fi, W_fo, l2g, l2b, l, alpha=alpha, name=f"ffn_fwd_{l}")
        saved.append((h, qkv, gates, kvpad, bias, oa, ob, x1, r1, gs, us, r2))
        h = x2

    d_h, sq = _loss_head(h, loss_target[0], name="loss_head")
    loss = lax.psum((0.5 / D) * jnp.sum(sq), ("x", "y", "c"))

    g_in, g_pa, g_pb, g_o, g_fi, g_fo = ([None] * L for _ in range(6))
    g_bg, g_rb, g_l1g, g_l1b, g_l2g, g_l2b = ([None] * L for _ in range(6))
    for l in reversed(range(L)):
        xin, qkv, gates, kvpad, bias, oa, ob, x1, r1, gs, us, r2 = saved[l]
        dr2, act, dgt, dup, dx1, g_l2g[l], g_l2b[l] = _ffn_bwd(d_h, r2, l2g, gs, us, W_fi, W_fo, l,
                                                              alpha=alpha, name=f"ffn_bwd_{l}")
        g_fo[l] = _mm_tn_blocked_a(act, dr2, name=f"grad_w_ffn_out_{l}").reshape(-1, D)
        g_fi[l] = jnp.concatenate([_mm_tn_blocked(x1, dgt, name=f"grad_w_ffn_gate_{l}"),
                                   _mm_tn_blocked(x1, dup, name=f"grad_w_ffn_up_{l}")], axis=0)
        (dr1, mixin, dya, dyb, dgates, doa, dob, g_l1g[l], g_l1b[l], g_bg[l]) = _proj_bwd(
            dx1, r1, l1g, oa, ob, gates, bg3, W_pa, W_pb, W_o, l, name=f"proj_bwd_{l}")
        g_o[l] = _mm_tn(mixin, dr1, tm=_tile(D, 1024), tn=_tile(D, 1024), name=f"grad_w_out_{l}")
        g_pa[l] = _mm_tn(oa, dya, tm=WIDTH, tn=_tile(D, 1024), name=f"grad_w_proj_a_{l}")
        g_pb[l] = _mm_tn(ob, dyb, tm=WIDTH, tn=_tile(D, 1024), name=f"grad_w_proj_b_{l}")
        dqa, dka, dva, dbias = _attn_a_bwd(qkv, kvpad, bias, doa, name=f"attn_a_bwd_{l}")
        g_rb[l] = _toeplitz_bias_grad(dbias)
        if l == 0:
            groups = ([g_in[1:]] if L > 1 else []) + [g_pa, g_pb, g_o, g_fi, g_fo]
            job = _grads_job(groups, (modes[:1] if L > 1 else []) + modes[1:])
            (dqb, dkb, dvb), slots_later = _attn_b_bwd(qkv, ob, dob, col0=b_col0, name=f"attn_b_bwd_{l}", job=job)
        else:
            (dqb, dkb, dvb), _ = _attn_b_bwd(qkv, ob, dob, col0=b_col0, name=f"attn_b_bwd_{l}")
        pad = A_WIN - A_TQ
        d_pre = jnp.concatenate([dqa, dka[pad:].astype(BF16), dva[pad:].astype(BF16),
                                 dqb, dkb.astype(BF16), dvb.astype(BF16), dgates], axis=1)
        g_in[l] = _mm_tn(xin, d_pre, tm=D, tn=w_in.shape[2], name=f"grad_w_in_{l}")
        W_in, wl = (W_in_first, 0) if l == 0 else (W_in_later, l - 1)
        d_h = _mm_nt_add(d_pre, W_in, wl, dr1, alpha, name=f"grad_x_{l}")
    grad_x = d_h[None]

    (slots_in,) = _grads_job([g_in[:1]], modes[:1]).run("exchange_grad_w_in_first")
    if L > 1:
        slots_in = jnp.concatenate([slots_in, slots_later[0]], axis=1)
    slots = [slots_in] + list(slots_later[-5:])
    moments_m = [m_w_in, m_w_proj_a, m_w_proj_b, m_w_out, m_w_ffn_in, m_w_ffn_out]
    moments_v = [v_w_in, v_w_proj_a, v_w_proj_b, v_w_out, v_w_ffn_in, v_w_ffn_out]
    names = ["w_in", "w_proj_a", "w_proj_b", "w_out", "w_ffn_in", "w_ffn_out"]
    big_out = {}
    for nm, s, w, m, v in zip(names, slots, big, moments_m, moments_v):
        two = lambda a: a.reshape(-1, a.shape[-1])
        res = _sum_slots_adamw(s.reshape(N_DEV, -1, s.shape[-1]), two(w), two(m), two(v), name=f"adamw_{nm}")
        big_out[nm] = [r.reshape(w.shape) for r in res]

    small_w = [b_gate, rel_bias, ln1_g, ln1_b, ln2_g, ln2_b]
    small_g = [jnp.stack(g) for g in (g_bg, g_rb, g_l1g, g_l1b, g_l2g, g_l2b)]
    small_m = [m_b_gate, m_rel_bias, m_ln1_g, m_ln1_b, m_ln2_g, m_ln2_b]
    small_v = [v_b_gate, v_rel_bias, v_ln1_g, v_ln1_b, v_ln2_g, v_ln2_b]
    res = _small_allreduce_adamw(_pack(small_g), _pack(small_w), _pack(small_m), _pack(small_v),
                                 name="allreduce_small_adamw")
    small_names = ["b_gate", "rel_bias", "ln1_g", "ln1_b", "ln2_g", "ln2_b"]
    small_out = {nm: [] for nm in small_names}
    for packed in res:
        for nm, arr in zip(small_names, _unpack(packed, small_w)):
            small_out[nm].append(arr)

    order = ["w_in", "b_gate", "rel_bias", "w_proj_a", "w_proj_b", "w_out", "ln1_g", "ln1_b",
             "w_ffn_in", "w_ffn_out", "ln2_g", "ln2_b"]
    every = {**big_out, **small_out}
    outs = [loss, grad_x]
    for kind in range(4):
        outs += [every[nm][kind] for nm in order]
    return tuple(outs)
```

```python
import functools
import math

import jax
import jax.numpy as jnp
from jax import lax
from jax.experimental import pallas as pl
from jax.experimental.pallas import tpu as pltpu

F32 = jnp.float32
BF16 = jnp.bfloat16

HEAD_DIM = 64
CHUNK = 64
LEFT_CHUNKS = 8
REL_CLIP = 256
N_REL = 2 * REL_CLIP + 1
WIDTH = 512
LANES = 128
A_TQ = 256
A_WIN = A_TQ + LEFT_CHUNKS * CHUNK
B_TQ = 1024
B_TS = 256
B_PIECE = 64
B_DEAD = -160.0
LN_EPS = 1e-5
QK_SCALE = 1.0 / math.sqrt(HEAD_DIM)
LOG2E = 1.4426950408889634
NEG = -1e30

ADAM_LR = 0.001
ADAM_B1 = 0.9
ADAM_B2 = 0.999
ADAM_EPS = 1e-08
ADAM_WD = 0.01
ADAM_STEP = 10

N_DEV = 8
MESH = pl.DeviceIdType.MESH
MIB = 1024 * 1024


def _cparams(sem=None, vmem_mib=48):
    return pltpu.CompilerParams(dimension_semantics=sem, vmem_limit_bytes=vmem_mib * MIB)


def _dot(a, b):
    return jnp.dot(a, b, preferred_element_type=F32)


def _dot_nt(a, b):
    return lax.dot_general(a, b, (((1,), (1,)), ((), ())), preferred_element_type=F32)


def _dot_tn(a, b):
    return lax.dot_general(a, b, (((0,), (0,)), ((), ())), preferred_element_type=F32)


def _tile(n, pref):
    if n <= pref:
        return n
    for t in range(pref - pref % 8, 0, -8):
        if n % t == 0:
            return t
    raise ValueError((n, pref))


def _mm_nn(a, w, layer, *, col_off, n_cols, out_dtype, name):
    M, K = a.shape
    tm = _tile(M, 1024)
    tn = _tile(n_cols, 512)
    assert col_off % tn == 0
    off = col_off // tn

    def body(a_ref, w_ref, o_ref):
        o_ref[...] = _dot(a_ref[...].astype(BF16), w_ref[...]).astype(out_dtype)

    return pl.pallas_call(
        body, name=name, grid=(M // tm, n_cols // tn),
        in_specs=[pl.BlockSpec((tm, K), lambda i, j: (i, 0)),
                  pl.BlockSpec((None, K, tn), lambda i, j: (layer, 0, j + off))],
        out_specs=pl.BlockSpec((tm, tn), lambda i, j: (i, j)),
        out_shape=jax.ShapeDtypeStruct((M, n_cols), out_dtype),
        compiler_params=_cparams(("parallel", "parallel")),
    )(a, w)


def _mm_nt_add(a, w, layer, add, add_scale, *, name):
    M, K = a.shape
    N = w.shape[1]
    tm = _tile(M, 1024)
    tk = _tile(K, 1024)

    def body(a_ref, w_ref, add_ref, o_ref):
        @pl.when(pl.program_id(1) == 0)
        def _():
            o_ref[...] = add_scale * add_ref[...]
        o_ref[...] += _dot_nt(a_ref[...], w_ref[...])

    return pl.pallas_call(
        body, name=name, grid=(M // tm, K // tk),
        in_specs=[pl.BlockSpec((tm, tk), lambda i, k: (i, k)),
                  pl.BlockSpec((None, N, tk), lambda i, k: (layer, 0, k)),
                  pl.BlockSpec((tm, N), lambda i, k: (i, 0))],
        out_specs=pl.BlockSpec((tm, N), lambda i, k: (i, 0)),
        out_shape=jax.ShapeDtypeStruct((M, N), F32),
        compiler_params=_cparams(("parallel", "arbitrary")),
    )(a, w, add)


def _tn_body(k_axis, n_k):
    def body(a_ref, b_ref, o_ref, acc_ref):
        k = pl.program_id(k_axis)

        @pl.when(k == 0)
        def _():
            acc_ref[...] = jnp.zeros_like(acc_ref)
        acc_ref[...] += _dot_tn(a_ref[...].astype(BF16), b_ref[...].astype(BF16))

        @pl.when(k == n_k - 1)
        def _():
            o_ref[...] = acc_ref[...].astype(o_ref.dtype)
    return body


def _mm_tn(a, b, *, tm, tn, name):
    T, M = a.shape
    N = b.shape[1]
    tk = _tile(T, 512)
    return pl.pallas_call(
        _tn_body(2, T // tk), name=name, grid=(M // tm, N // tn, T // tk),
        in_specs=[pl.BlockSpec((tk, tm), lambda i, j, k: (k, i)),
                  pl.BlockSpec((tk, tn), lambda i, j, k: (k, j))],
        out_specs=pl.BlockSpec((tm, tn), lambda i, j, k: (i, j)),
        out_shape=jax.ShapeDtypeStruct((M, N), BF16),
        scratch_shapes=[pltpu.VMEM((tm, tn), F32)],
        compiler_params=_cparams(("parallel", "parallel", "arbitrary")),
    )(a, b)


def _mm_tn_blocked(a, b, *, name):
    T, M = a.shape
    S, _, N = b.shape
    tk = _tile(T, 512)
    return pl.pallas_call(
        _tn_body(1, T // tk), name=name, grid=(S, T // tk),
        in_specs=[pl.BlockSpec((tk, M), lambda s, k: (k, 0)),
                  pl.BlockSpec((None, tk, N), lambda s, k: (s, k, 0))],
        out_specs=pl.BlockSpec((None, M, N), lambda s, k: (s, 0, 0)),
        out_shape=jax.ShapeDtypeStruct((S, M, N), BF16),
        scratch_shapes=[pltpu.VMEM((M, N), F32)],
        compiler_params=_cparams(("parallel", "arbitrary")),
    )(a, b)


def _mm_tn_blocked_a(a, b, *, name):
    S, T, M = a.shape
    N = b.shape[1]
    tk = _tile(T, 512)
    return pl.pallas_call(
        _tn_body(1, T // tk), name=name, grid=(S, T // tk),
        in_specs=[pl.BlockSpec((None, tk, M), lambda s, k: (s, k, 0)),
                  pl.BlockSpec((tk, N), lambda s, k: (k, 0))],
        out_specs=pl.BlockSpec((None, M, N), lambda s, k: (s, 0, 0)),
        out_shape=jax.ShapeDtypeStruct((S, M, N), BF16),
        scratch_shapes=[pltpu.VMEM((M, N), F32)],
        compiler_params=_cparams(("parallel", "arbitrary")),
    )(a, b)


def _ln_fwd(r, g, b):
    mu = jnp.mean(r, axis=-1, keepdims=True)
    xc = r - mu
    var = jnp.mean(xc * xc, axis=-1, keepdims=True)
    return xc * lax.rsqrt(var + LN_EPS) * g + b


def _ln_bwd(dy, r, g):
    mu = jnp.mean(r, axis=-1, keepdims=True)
    xc = r - mu
    var = jnp.mean(xc * xc, axis=-1, keepdims=True)
    rstd = lax.rsqrt(var + LN_EPS)
    xhat = xc * rstd
    dxh = dy * g
    m1 = jnp.mean(dxh, axis=-1, keepdims=True)
    m2 = jnp.mean(dxh * xhat, axis=-1, keepdims=True)
    return rstd * (dxh - m1 - xhat * m2), xhat


def _lane_is_head0():
    return lax.broadcasted_iota(jnp.int32, (1, LANES), 1) < HEAD_DIM


def _band_valid(i):
    r = lax.broadcasted_iota(jnp.int32, (A_TQ, A_WIN), 0)
    c = lax.broadcasted_iota(jnp.int32, (A_TQ, A_WIN), 1)
    a = r // CHUNK
    b = c // CHUNK
    return (b >= a) & (b <= a + LEFT_CHUNKS) & (c + i * A_TQ >= LEFT_CHUNKS * CHUNK)


def _band_probs(q2, k2, bias, valid, head0, h):
    qh = jnp.where(head0 if h == 0 else jnp.logical_not(head0), q2, jnp.zeros_like(q2))
    s = _dot_nt(qh, k2) * QK_SCALE + bias
    s = jnp.where(valid, s, NEG)
    m = jnp.max(s, axis=1, keepdims=True)
    e = jnp.exp(s - m)
    return e / jnp.sum(e, axis=1, keepdims=True)


def _attn_a_fwd(qkv, kvpad, bias, *, name):
    T = qkv.shape[0]
    n_hp = WIDTH // LANES

    def body(q_ref, k_ref, v_ref, bias_ref, o_ref):
        i = pl.program_id(1)
        row0 = pl.multiple_of(i * A_TQ, A_TQ)
        q2 = q_ref[...]
        k2 = k_ref[pl.ds(row0, A_WIN), :]
        v2 = v_ref[pl.ds(row0, A_WIN), :]
        head0 = _lane_is_head0()
        valid = _band_valid(i)
        outs = []
        for h in range(2):
            p = _band_probs(q2, k2, bias_ref[h], valid, head0, h)
            outs.append(_dot(p.astype(BF16), v2))
        o_ref[...] = jnp.where(head0, outs[0], outs[1]).astype(o_ref.dtype)

    return pl.pallas_call(
        body, name=name, grid=(n_hp, T // A_TQ),
        in_specs=[pl.BlockSpec((A_TQ, LANES), lambda hp, i: (i, hp)),
                  pl.BlockSpec((T + A_WIN - A_TQ, LANES), lambda hp, i: (0, hp)),
                  pl.BlockSpec((T + A_WIN - A_TQ, LANES), lambda hp, i: (0, hp + n_hp)),
                  pl.BlockSpec((2, A_TQ, A_WIN), lambda hp, i: (hp, 0, 0))],
        out_specs=pl.BlockSpec((A_TQ, LANES), lambda hp, i: (i, hp)),
        out_shape=jax.ShapeDtypeStruct((T, WIDTH), BF16),
        compiler_params=_cparams(("parallel", "arbitrary")),
    )(qkv, kvpad, kvpad, bias)


def _attn_a_bwd(qkv, kvpad, bias, do, *, name):
    T = qkv.shape[0]
    TP = T + A_WIN - A_TQ
    n_hp = WIDTH // LANES

    def body(q_ref, k_ref, v_ref, bias_ref, do_ref, dq_ref, dk_ref, dv_ref, db_ref):
        i = pl.program_id(1)

        @pl.when(i == 0)
        def _():
            dk_ref[...] = jnp.zeros_like(dk_ref)
            dv_ref[...] = jnp.zeros_like(dv_ref)
            db_ref[...] = jnp.zeros_like(db_ref)

        row0 = pl.multiple_of(i * A_TQ, A_TQ)
        q2 = q_ref[...]
        do2 = do_ref[...]
        k2 = k_ref[pl.ds(row0, A_WIN), :]
        v2 = v_ref[pl.ds(row0, A_WIN), :]
        head0 = _lane_is_head0()
        valid = _band_valid(i)
        dq, dk, dv = [], [], []
        for h in range(2):
            hm = head0 if h == 0 else jnp.logical_not(head0)
            p = _band_probs(q2, k2, bias_ref[h], valid, head0, h)
            doh = jnp.where(hm, do2, jnp.zeros_like(do2))
            dp = _dot_nt(doh, v2)
            delta = jnp.sum(p * dp, axis=1, keepdims=True)
            ds = p * (dp - delta)
            db_ref[h] += ds
            dsb = (ds * QK_SCALE).astype(BF16)
            dq.append(_dot(dsb, k2))
            dk.append(_dot_tn(dsb, q2))
            dv.append(_dot_tn(p.astype(BF16), do2))
        dq_ref[...] = jnp.where(head0, dq[0], dq[1]).astype(dq_ref.dtype)
        dk_ref[pl.ds(row0, A_WIN), :] += jnp.where(head0, dk[0], dk[1])
        dv_ref[pl.ds(row0, A_WIN), :] += jnp.where(head0, dv[0], dv[1])

    return pl.pallas_call(
        body, name=name, grid=(n_hp, T // A_TQ),
        in_specs=[pl.BlockSpec((A_TQ, LANES), lambda hp, i: (i, hp)),
                  pl.BlockSpec((TP, LANES), lambda hp, i: (0, hp)),
                  pl.BlockSpec((TP, LANES), lambda hp, i: (0, hp + n_hp)),
                  pl.BlockSpec((2, A_TQ, A_WIN), lambda hp, i: (hp, 0, 0)),
                  pl.BlockSpec((A_TQ, LANES), lambda hp, i: (i, hp))],
        out_specs=[pl.BlockSpec((A_TQ, LANES), lambda hp, i: (i, hp)),
                   pl.BlockSpec((TP, LANES), lambda hp, i: (0, hp)),
                   pl.BlockSpec((TP, LANES), lambda hp, i: (0, hp)),
                   pl.BlockSpec((2, A_TQ, A_WIN), lambda hp, i: (hp, 0, 0))],
        out_shape=[jax.ShapeDtypeStruct((T, WIDTH), BF16),
                   jax.ShapeDtypeStruct((TP, WIDTH), F32),
                   jax.ShapeDtypeStruct((TP, WIDTH), F32),
                   jax.ShapeDtypeStruct((WIDTH // HEAD_DIM, A_TQ, A_WIN), F32)],
        compiler_params=_cparams(("parallel", "arbitrary"), 56),
    )(qkv, kvpad, kvpad, bias, do)


def _toeplitz_bias(rb):
    H = rb.shape[0]
    span = A_TQ + A_WIN - 1
    n_tail = span - (N_REL - 1)
    ext = jnp.concatenate([rb[:, 1:], jnp.broadcast_to(rb[:, N_REL - 1:], (H, n_tail))], axis=1)
    rev = jnp.pad(ext[:, ::-1], ((0, 0), (0, 1)))
    flat = jnp.broadcast_to(rev[:, None, :], (H, A_TQ, span + 1)).reshape(H, A_TQ * (span + 1))
    skew = flat[:, :A_TQ * span].reshape(H, A_TQ, span)
    return skew[:, :, A_TQ - 1:A_TQ - 1 + A_WIN]


def _toeplitz_bias_grad(db):
    H = db.shape[0]
    span = A_TQ + A_WIN - 1
    d_skew = jnp.pad(db, ((0, 0), (0, 0), (A_TQ - 1, span - (A_TQ - 1) - A_WIN)))
    d_flat = jnp.pad(d_skew.reshape(H, A_TQ * span), ((0, 0), (0, A_TQ)))
    g_ext = jnp.sum(d_flat.reshape(H, A_TQ, span + 1), axis=1)[:, :span][:, ::-1]
    last = g_ext[:, N_REL - 2] + jnp.sum(g_ext[:, N_REL - 1:], axis=1)
    return jnp.concatenate([jnp.zeros((H, 1), F32), g_ext[:, :N_REL - 2], last[:, None]], axis=1)


def _split_bf16(x):
    hi = x.astype(BF16)
    lo = (x - hi.astype(F32)).astype(BF16)
    return hi, lo


def _sb_streams(d):
    out = []
    for h in range(2):
        for r in range(B_TQ // B_TS):
            if d is not None and d > r:
                continue
            out.append(dict(h=h, r=r, rows=pl.ds(r * B_TS, B_TS), diag=(d is not None and d == r)))
    return out


def _piece_rows(st, p):
    return pl.ds(st["r"] * B_TS + p, B_PIECE)


def _rows_cat(parts):
    return jnp.concatenate(parts, axis=0)


def _skewed(streams, stages):
    for t in range(len(streams) + len(stages) - 1):
        for s, st in enumerate(streams):
            if 0 <= t - s < len(stages):
                stages[t - s](st)


def _sb_logs(st, z2):
    log_beta, log_keep, keep_bf = [], [], []
    for p in range(0, B_TS, B_PIECE):
        z = z2[p:p + B_PIECE]
        lp2 = jnp.log(1.0 + jnp.exp2(-jnp.abs(z))) * LOG2E
        lb = jnp.minimum(z, 0.0) - lp2
        lk = lb - z
        if st["diag"]:
            lk = jnp.where(_strict_lower(p), lk, 0.0)
        log_beta.append(lb)
        log_keep.append(lk)
        keep_bf.append(lk.astype(BF16))
    st["log_beta"] = _rows_cat(log_beta)
    st["log_keep"] = _rows_cat(log_keep)
    st["keep_bf"] = _rows_cat(keep_bf)


def _strict_lower(p):
    t = p + lax.broadcasted_iota(jnp.int32, (B_PIECE, B_TS), 0)
    s = lax.broadcasted_iota(jnp.int32, (B_PIECE, B_TS), 1)
    return s < t


def _tri(strict):
    j = lax.broadcasted_iota(jnp.int32, (B_TS, B_TS), 0)
    s = lax.broadcasted_iota(jnp.int32, (B_TS, B_TS), 1)
    return jnp.where(j > s if strict else j >= s, 1.0, 0.0).astype(BF16)


def _call_carrying(job, body, *, name, grid, in_specs, out_specs, out_shape, scratch_shapes, vmem_mib, args):
    n_in, n_out, n_scr = len(in_specs), len(out_specs), len(scratch_shapes)
    if job is None:
        res = pl.pallas_call(body, name=name, grid=grid, in_specs=in_specs, out_specs=out_specs,
                             out_shape=out_shape, scratch_shapes=scratch_shapes,
                             compiler_params=_cparams(("arbitrary",) * len(grid), vmem_mib))(*args)
        return res, []
    j_in, j_out = len(job.arrays), len(job.out_shape)
    hbm = pl.BlockSpec(memory_space=pltpu.HBM)

    def carrying(*refs):
        refs = list(refs)
        ins, refs = refs[:n_in], refs[n_in:]
        j_ins, refs = refs[:j_in], refs[j_in:]
        outs, refs = refs[:n_out], refs[n_out:]
        j_outs, refs = refs[:j_out], refs[j_out:]
        scr, sems = refs[:n_scr], refs[n_scr:]
        first = functools.reduce(jnp.logical_and, [pl.program_id(d) == 0 for d in range(len(grid))])
        last = functools.reduce(jnp.logical_and, [pl.program_id(d) == grid[d] - 1 for d in range(len(grid))])

        @pl.when(first)
        def _():
            job.start(j_ins, j_outs, sems)

        body(*ins, *outs, *scr)

        @pl.when(last)
        def _():
            job.wait(j_ins, j_outs, sems)

    res = pl.pallas_call(
        carrying, name=name, grid=grid,
        in_specs=list(in_specs) + [hbm] * j_in, out_specs=list(out_specs) + [hbm] * j_out,
        out_shape=list(out_shape) + job.out_shape, scratch_shapes=list(scratch_shapes) + job.scratch(),
        compiler_params=_cparams(("arbitrary",) * len(grid), vmem_mib))(*args, *job.arrays)
    return res[:n_out], res[n_out:]


def _attn_b_fwd(qkv, *, col0, name, job=None):
    T = qkv.shape[0]
    n_hp = WIDTH // LANES
    sub = B_TQ // B_TS

    def body(q_ref, k_ref, v_ref, o_ref, acc_ref, car_ref, qh_ref):
        i = pl.program_id(1)
        q2 = q_ref[...]
        head0 = _lane_is_head0()
        qh_ref[0] = jnp.where(head0, q2, jnp.zeros_like(q2))
        qh_ref[1] = jnp.where(head0, jnp.zeros_like(q2), q2)
        tri_s = _tri(True)
        acc_ref[...] = jnp.zeros_like(acc_ref)
        car_ref[...] = jnp.zeros_like(car_ref)

        def tile(kb, d):
            k0 = pl.multiple_of(kb * B_TS, B_TS)
            k2 = k_ref[pl.ds(k0, B_TS), :]
            v2 = v_ref[pl.ds(k0, B_TS), :]

            def scores(st):
                st["z2"] = _dot_nt(qh_ref[st["h"], st["rows"], :], k2) * (QK_SCALE * LOG2E)

            def logs(st):
                _sb_logs(st, st.pop("z2"))

            def suffix(st):
                st["suffix"] = _dot(st.pop("keep_bf"), tri_s)

            def weights(st):
                log_beta, suffix, log_keep = st.pop("log_beta"), st.pop("suffix"), st.pop("log_keep")
                wb = []
                for p in range(0, B_TS, B_PIECE):
                    rows = _piece_rows(st, p)
                    car = car_ref[st["h"], rows, :]
                    w = jnp.exp2(log_beta[p:p + B_PIECE] + suffix[p:p + B_PIECE] + car)
                    if st["diag"]:
                        w = jnp.where(_strict_lower(p), w, 0.0)
                    wb.append(w.astype(BF16))
                    car_ref[st["h"], rows, :] = car + jnp.sum(log_keep[p:p + B_PIECE], axis=1, keepdims=True)
                st["wb"] = _rows_cat(wb)

            def values(st):
                acc_ref[st["h"], st["rows"], :] += _dot(st.pop("wb"), v2)

            _skewed(_sb_streams(d), [scores, logs, suffix, weights, values])

        for d in reversed(range(sub)):
            tile(i * sub + d, d)

        def alive(c):
            return (c[0] < i * sub) & (c[1] > B_DEAD)

        def step(c):
            tile(i * sub - 1 - c[0], None)
            return c[0] + 1, jnp.max(car_ref[...])

        lax.while_loop(alive, step, (jnp.int32(0), jnp.float32(0.0)))
        o_ref[...] = jnp.where(head0, acc_ref[0], acc_ref[1])

    (out,), rode = _call_carrying(
        job, body, name=name, grid=(n_hp, T // B_TQ),
        in_specs=[pl.BlockSpec((B_TQ, LANES), lambda hp, i: (i, hp + col0)),
                  pl.BlockSpec((T, LANES), lambda hp, i: (0, hp + col0 + n_hp)),
                  pl.BlockSpec((T, LANES), lambda hp, i: (0, hp + col0 + 2 * n_hp))],
        out_specs=[pl.BlockSpec((B_TQ, LANES), lambda hp, i: (i, hp))],
        out_shape=[jax.ShapeDtypeStruct((T, WIDTH), F32)],
        scratch_shapes=[pltpu.VMEM((2, B_TQ, LANES), F32), pltpu.VMEM((2, B_TQ, 1), F32),
                        pltpu.VMEM((2, B_TQ, LANES), BF16)],
        vmem_mib=48, args=(qkv, qkv, qkv))
    return out, rode


def _attn_b_bwd(qkv, out, do, *, col0, name, job=None):
    T = qkv.shape[0]
    n_hp = WIDTH // LANES
    sub = B_TQ // B_TS

    def body(q_ref, k_ref, v_ref, o_ref, do_ref, dq_ref, dk_ref, dv_ref,
             dqa_ref, car_ref, carr_ref, tot_ref, qh_ref, doh_ref, qs_ref):
        i = pl.program_id(1)

        @pl.when(i == 0)
        def _():
            dk_ref[...] = jnp.zeros_like(dk_ref)
            dv_ref[...] = jnp.zeros_like(dv_ref)

        q2 = q_ref[...]
        do2 = do_ref[...]
        head0 = _lane_is_head0()
        zero = jnp.zeros_like(q2)
        qh_ref[0] = jnp.where(head0, q2, zero)
        qh_ref[1] = jnp.where(head0, zero, q2)
        doh_ref[0] = jnp.where(head0, do2, zero)
        doh_ref[1] = jnp.where(head0, zero, do2)
        scale = jnp.asarray(QK_SCALE, BF16)
        qs_ref[...] = q2 * scale
        tri_s = _tri(True)
        tri_i = _tri(False)
        prod = do2.astype(F32) * o_ref[...]
        tot_ref[0] = jnp.sum(jnp.where(head0, prod, 0.0), axis=1, keepdims=True)
        tot_ref[1] = jnp.sum(jnp.where(head0, 0.0, prod), axis=1, keepdims=True)
        dqa_ref[...] = jnp.zeros_like(dqa_ref)
        car_ref[...] = jnp.zeros_like(car_ref)
        carr_ref[...] = jnp.zeros_like(carr_ref)

        def tile(kb, d):
            k0 = pl.multiple_of(kb * B_TS, B_TS)
            keys = pl.ds(k0, B_TS)
            k2 = k_ref[keys, :]
            v2 = v_ref[keys, :]
            k2s = k2 * scale

            def scores(st):
                st["z2"] = _dot_nt(qh_ref[st["h"], st["rows"], :], k2) * (QK_SCALE * LOG2E)
                st["dw"] = _dot_nt(doh_ref[st["h"], st["rows"], :], v2)

            def logs(st):
                _sb_logs(st, st.pop("z2"))

            def suffix(st):
                st["suffix"] = _dot(st.pop("keep_bf"), tri_s)

            def weights(st):
                h = st["h"]
                suffix, dw = st.pop("suffix"), st.pop("dw")
                wb, dlog, hi, lo = [], [], [], []
                for p in range(0, B_TS, B_PIECE):
                    rows = _piece_rows(st, p)
                    car = car_ref[h, rows, :]
                    w = jnp.exp2(st["log_beta"][p:p + B_PIECE] + suffix[p:p + B_PIECE] + car)
                    if st["diag"]:
                        w = jnp.where(_strict_lower(p), w, 0.0)
                    w = w.astype(BF16)
                    dl = w.astype(F32) * dw[p:p + B_PIECE]
                    dl_hi, dl_lo = _split_bf16(dl)
                    wb.append(w)
                    dlog.append(dl)
                    hi.append(dl_hi)
                    lo.append(dl_lo)
                    car_ref[h, rows, :] = car + jnp.sum(st["log_keep"][p:p + B_PIECE], axis=1, keepdims=True)
                st["wb"], st["dlog"], st["hi"], st["lo"] = _rows_cat(wb), _rows_cat(dlog), _rows_cat(hi), _rows_cat(lo)

            def later(st):
                st["later"] = _dot(st.pop("hi"), tri_i) + _dot(st.pop("lo"), tri_i)

            def dscores(st):
                h = st["h"]
                later, dlog = st.pop("later"), st.pop("dlog")
                log_keep, log_beta = st.pop("log_keep"), st.pop("log_beta")
                dzb = []
                for p in range(0, B_TS, B_PIECE):
                    rows = _piece_rows(st, p)
                    pc = slice(p, p + B_PIECE)
                    carr = carr_ref[h, rows, :]
                    earlier = tot_ref[h, rows, :] - (later[pc] + carr)
                    dz = dlog[pc] * jnp.exp2(log_keep[pc]) - jnp.exp2(log_beta[pc]) * earlier
                    if st["diag"]:
                        dz = jnp.where(_strict_lower(p), dz, 0.0)
                    dzb.append(dz.astype(BF16))
                    carr_ref[h, rows, :] = carr + jnp.sum(dlog[pc], axis=1, keepdims=True)
                st["dzb"] = _rows_cat(dzb)

            def grads(st):
                h, rows = st["h"], st["rows"]
                mine = head0 if h == 0 else jnp.logical_not(head0)
                dzb = st.pop("dzb")
                dqa_ref[h, rows, :] += _dot(dzb, k2s)
                dk_ref[keys, :] += jnp.where(mine, _dot_tn(dzb, qs_ref[rows, :]), 0.0)
                dv_ref[keys, :] += jnp.where(mine, _dot_tn(st.pop("wb"), do_ref[rows, :]), 0.0)

            _skewed(_sb_streams(d), [scores, logs, suffix, weights, later, dscores, grads])

        for d in reversed(range(sub)):
            tile(i * sub + d, d)

        def alive(c):
            return (c[0] < i * sub) & (c[1] > B_DEAD)

        def step(c):
            tile(i * sub - 1 - c[0], None)
            return c[0] + 1, jnp.max(car_ref[...])

        lax.while_loop(alive, step, (jnp.int32(0), jnp.float32(0.0)))
        dq_ref[...] = jnp.where(head0, dqa_ref[0], dqa_ref[1]).astype(dq_ref.dtype)

    return _call_carrying(
        job, body, name=name, grid=(n_hp, T // B_TQ),
        in_specs=[pl.BlockSpec((B_TQ, LANES), lambda hp, i: (i, hp + col0)),
                  pl.BlockSpec((T, LANES), lambda hp, i: (0, hp + col0 + n_hp)),
                  pl.BlockSpec((T, LANES), lambda hp, i: (0, hp + col0 + 2 * n_hp)),
                  pl.BlockSpec((B_TQ, LANES), lambda hp, i: (i, hp)),
                  pl.BlockSpec((B_TQ, LANES), lambda hp, i: (i, hp))],
        out_specs=[pl.BlockSpec((B_TQ, LANES), lambda hp, i: (i, hp)),
                   pl.BlockSpec((T, LANES), lambda hp, i: (0, hp)),
                   pl.BlockSpec((T, LANES), lambda hp, i: (0, hp))],
        out_shape=[jax.ShapeDtypeStruct((T, WIDTH), BF16),
                   jax.ShapeDtypeStruct((T, WIDTH), F32),
                   jax.ShapeDtypeStruct((T, WIDTH), F32)],
        scratch_shapes=[pltpu.VMEM((2, B_TQ, LANES), F32), pltpu.VMEM((2, B_TQ, 1), F32),
                        pltpu.VMEM((2, B_TQ, 1), F32), pltpu.VMEM((2, B_TQ, 1), F32),
                        pltpu.VMEM((2, B_TQ, LANES), BF16), pltpu.VMEM((2, B_TQ, LANES), BF16),
                        pltpu.VMEM((B_TQ, LANES), BF16)],
        vmem_mib=56, args=(qkv, qkv, qkv, out, do))


def _gated_mix(oa_ref, ob_ref, g_ref, bg_ref, wpa_ref, wpb_ref, D):
    ya = _dot(oa_ref[...].astype(BF16), wpa_ref[...])
    yb = _dot(ob_ref[...].astype(BF16), wpb_ref[...])
    sa = jax.nn.sigmoid(g_ref[:, :D] + bg_ref[:, :D])
    sb = jax.nn.sigmoid(g_ref[:, D:] + bg_ref[:, D:])
    return ya, yb, sa, sb


def _proj_fwd(oa, ob, g, bg, wpa, wpb, wo, xin, lng, lnb, layer, *, alpha, name):
    T, D = xin.shape
    tm = _tile(T, 512)
    row = lambda i: (i, 0)
    wspec = lambda r, c: pl.BlockSpec((None, r, c), lambda i: (layer, 0, 0))
    vec = lambda c: pl.BlockSpec((None, 1, c), lambda i: (layer, 0, 0))

    def body(oa_ref, ob_ref, g_ref, bg_ref, wpa_ref, wpb_ref, wo_ref, x_ref, lg_ref, lb_ref, x1_ref, r1_ref):
        ya, yb, sa, sb = _gated_mix(oa_ref, ob_ref, g_ref, bg_ref, wpa_ref, wpb_ref, D)
        mix = _dot((sa * ya + sb * yb).astype(BF16), wo_ref[...])
        r1 = alpha * x_ref[...] + mix
        r1_ref[...] = r1
        x1_ref[...] = _ln_fwd(r1, lg_ref[...], lb_ref[...])

    return pl.pallas_call(
        body, name=name, grid=(T // tm,),
        in_specs=[pl.BlockSpec((tm, WIDTH), row), pl.BlockSpec((tm, WIDTH), row), pl.BlockSpec((tm, 2 * D), row),
                  vec(2 * D), wspec(WIDTH, D), wspec(WIDTH, D), wspec(D, D),
                  pl.BlockSpec((tm, D), row), vec(D), vec(D)],
        out_specs=[pl.BlockSpec((tm, D), row), pl.BlockSpec((tm, D), row)],
        out_shape=[jax.ShapeDtypeStruct((T, D), F32), jax.ShapeDtypeStruct((T, D), F32)],
        compiler_params=_cparams(("arbitrary",), 56),
    )(oa, ob, g, bg, wpa, wpb, wo, xin, lng, lnb)


def _proj_bwd(dx1, r1, lng, oa, ob, g, bg, wpa, wpb, wo, layer, *, name):
    T, D = dx1.shape
    tm = _tile(T, 512)
    row = lambda i: (i, 0)
    fixed = lambda i: (0, 0)
    wspec = lambda r, c: pl.BlockSpec((None, r, c), lambda i: (layer, 0, 0))
    vec = lambda c: pl.BlockSpec((None, 1, c), lambda i: (layer, 0, 0))

    def body(dx_ref, r1_ref, lg_ref, oa_ref, ob_ref, g_ref, bg_ref, wpa_ref, wpb_ref, wo_ref,
             dr_ref, mix_ref, dya_ref, dyb_ref, dg_ref, doa_ref, dob_ref, dlg_ref, dlb_ref, dbg_ref):
        @pl.when(pl.program_id(0) == 0)
        def _():
            dlg_ref[...] = jnp.zeros_like(dlg_ref)
            dlb_ref[...] = jnp.zeros_like(dlb_ref)
            dbg_ref[...] = jnp.zeros_like(dbg_ref)

        dx = dx_ref[...]
        dr, xhat = _ln_bwd(dx, r1_ref[...], lg_ref[...])
        dr_ref[...] = dr
        dlg_ref[...] += jnp.sum(dx * xhat, axis=0, keepdims=True)
        dlb_ref[...] += jnp.sum(dx, axis=0, keepdims=True)
        dmix = _dot_nt(dr.astype(BF16), wo_ref[...])
        ya, yb, sa, sb = _gated_mix(oa_ref, ob_ref, g_ref, bg_ref, wpa_ref, wpb_ref, D)
        mix_ref[...] = (sa * ya + sb * yb).astype(BF16)
        dya = (dmix * sa).astype(BF16)
        dyb = (dmix * sb).astype(BF16)
        dya_ref[...] = dya
        dyb_ref[...] = dyb
        dga = dmix * ya * (sa * (1.0 - sa))
        dgb = dmix * yb * (sb * (1.0 - sb))
        dg_ref[:, :D] = dga.astype(BF16)
        dg_ref[:, D:] = dgb.astype(BF16)
        dbg_ref[:, :D] += jnp.sum(dga, axis=0, keepdims=True)
        dbg_ref[:, D:] += jnp.sum(dgb, axis=0, keepdims=True)
        doa_ref[...] = _dot_nt(dya, wpa_ref[...]).astype(BF16)
        dob_ref[...] = _dot_nt(dyb, wpb_ref[...]).astype(BF16)

    return pl.pallas_call(
        body, name=name, grid=(T // tm,),
        in_specs=[pl.BlockSpec((tm, D), row), pl.BlockSpec((tm, D), row), vec(D),
                  pl.BlockSpec((tm, WIDTH), row), pl.BlockSpec((tm, WIDTH), row), pl.BlockSpec((tm, 2 * D), row),
                  vec(2 * D), wspec(WIDTH, D), wspec(WIDTH, D), wspec(D, D)],
        out_specs=[pl.BlockSpec((tm, D), row), pl.BlockSpec((tm, D), row), pl.BlockSpec((tm, D), row),
                   pl.BlockSpec((tm, D), row), pl.BlockSpec((tm, 2 * D), row),
                   pl.BlockSpec((tm, WIDTH), row), pl.BlockSpec((tm, WIDTH), row),
                   pl.BlockSpec((1, D), fixed), pl.BlockSpec((1, D), fixed), pl.BlockSpec((1, 2 * D), fixed)],
        out_shape=[jax.ShapeDtypeStruct((T, D), F32), jax.ShapeDtypeStruct((T, D), BF16),
                   jax.ShapeDtypeStruct((T, D), BF16), jax.ShapeDtypeStruct((T, D), BF16),
                   jax.ShapeDtypeStruct((T, 2 * D), BF16),
                   jax.ShapeDtypeStruct((T, WIDTH), BF16), jax.ShapeDtypeStruct((T, WIDTH), BF16),
                   jax.ShapeDtypeStruct((1, D), F32), jax.ShapeDtypeStruct((1, D), F32),
                   jax.ShapeDtypeStruct((1, 2 * D), F32)],
        compiler_params=_cparams(("arbitrary",), 56),
    )(dx1, r1, lng, oa, ob, g, bg, wpa, wpb, wo)


def _ffn_fwd(x1, wfi, wfo, lng, lnb, layer, *, alpha, name):
    T, D = x1.shape
    tf = wfi.shape[-1]
    nj = wfi.shape[0] // 2
    tm = _tile(T, 512)
    vec = lambda c: pl.BlockSpec((None, 1, c), lambda i, j: (layer, 0, 0))

    def body(x_ref, wg_ref, wu_ref, wo_ref, lg_ref, lb_ref, gs_ref, us_ref, r2_ref, x2_ref, acc_ref, xb_ref):
        j = pl.program_id(1)

        @pl.when(j == 0)
        def _():
            xb_ref[...] = x_ref[...].astype(BF16)
            acc_ref[...] = jnp.zeros_like(acc_ref)

        gv = _dot(xb_ref[...], wg_ref[...])
        uv = _dot(xb_ref[...], wu_ref[...])
        gs_ref[...] = gv
        us_ref[...] = uv
        act = gv * jax.nn.sigmoid(gv) * uv
        acc_ref[...] += _dot(act.astype(BF16), wo_ref[...])

        @pl.when(j == nj - 1)
        def _():
            r2 = alpha * x_ref[...] + acc_ref[...]
            r2_ref[...] = r2
            x2_ref[...] = _ln_fwd(r2, lg_ref[...], lb_ref[...])

    return pl.pallas_call(
        body, name=name, grid=(T // tm, nj),
        in_specs=[pl.BlockSpec((tm, D), lambda i, j: (i, 0)),
                  pl.BlockSpec((None, None, D, tf), lambda i, j: (j, layer, 0, 0)),
                  pl.BlockSpec((None, None, D, tf), lambda i, j: (j + nj, layer, 0, 0)),
                  pl.BlockSpec((None, tf, D), lambda i, j: (layer, j, 0)),
                  vec(D), vec(D)],
        out_specs=[pl.BlockSpec((None, tm, tf), lambda i, j: (j, i, 0)),
                   pl.BlockSpec((None, tm, tf), lambda i, j: (j, i, 0)),
                   pl.BlockSpec((tm, D), lambda i, j: (i, 0)),
                   pl.BlockSpec((tm, D), lambda i, j: (i, 0))],
        out_shape=[jax.ShapeDtypeStruct((nj, T, tf), F32), jax.ShapeDtypeStruct((nj, T, tf), F32),
                   jax.ShapeDtypeStruct((T, D), F32), jax.ShapeDtypeStruct((T, D), F32)],
        scratch_shapes=[pltpu.VMEM((tm, D), F32), pltpu.VMEM((tm, D), BF16)],
        compiler_params=_cparams(("parallel", "arbitrary"), 56),
    )(x1, wfi, wfi, wfo, lng, lnb)


def _ffn_bwd(dx2, r2, lng, gs, us, wfi, wfo, layer, *, alpha, name):
    T, D = dx2.shape
    tf = wfi.shape[-1]
    nj = wfi.shape[0] // 2
    tm = _tile(T, 512)
    vec = lambda c: pl.BlockSpec((None, 1, c), lambda i, j: (layer, 0, 0))
    blk = lambda: pl.BlockSpec((None, tm, tf), lambda i, j: (j, i, 0))

    def body(dx_ref, r2_ref, lg_ref, gs_ref, us_ref, wg_ref, wu_ref, wo_ref,
             dr_ref, act_ref, dg_ref, du_ref, dx1_ref, dlg_ref, dlb_ref, acc_ref, drb_ref):
        i = pl.program_id(0)
        j = pl.program_id(1)

        @pl.when((i == 0) & (j == 0))
        def _():
            dlg_ref[...] = jnp.zeros_like(dlg_ref)
            dlb_ref[...] = jnp.zeros_like(dlb_ref)

        @pl.when(j == 0)
        def _():
            dx = dx_ref[...]
            dr, xhat = _ln_bwd(dx, r2_ref[...], lg_ref[...])
            dlg_ref[...] += jnp.sum(dx * xhat, axis=0, keepdims=True)
            dlb_ref[...] += jnp.sum(dx, axis=0, keepdims=True)
            drb_ref[...] = dr.astype(BF16)
            dr_ref[...] = dr.astype(BF16)
            acc_ref[...] = alpha * dr

        dact = _dot_nt(drb_ref[...], wo_ref[...])
        gv = gs_ref[...]
        uv = us_ref[...]
        s = jax.nn.sigmoid(gv)
        silu = gv * s
        act_ref[...] = (silu * uv).astype(BF16)
        dg = (dact * uv * (s * (1.0 + gv * (1.0 - s)))).astype(BF16)
        du = (dact * silu).astype(BF16)
        dg_ref[...] = dg
        du_ref[...] = du
        acc_ref[...] += _dot_nt(dg, wg_ref[...]) + _dot_nt(du, wu_ref[...])

        @pl.when(j == nj - 1)
        def _():
            dx1_ref[...] = acc_ref[...]

    return pl.pallas_call(
        body, name=name, grid=(T // tm, nj),
        in_specs=[pl.BlockSpec((tm, D), lambda i, j: (i, 0)), pl.BlockSpec((tm, D), lambda i, j: (i, 0)), vec(D),
                  blk(), blk(),
                  pl.BlockSpec((None, None, D, tf), lambda i, j: (j, layer, 0, 0)),
                  pl.BlockSpec((None, None, D, tf), lambda i, j: (j + nj, layer, 0, 0)),
                  pl.BlockSpec((None, tf, D), lambda i, j: (layer, j, 0))],
        out_specs=[pl.BlockSpec((tm, D), lambda i, j: (i, 0)), blk(), blk(), blk(),
                   pl.BlockSpec((tm, D), lambda i, j: (i, 0)),
                   pl.BlockSpec((1, D), lambda i, j: (0, 0)), pl.BlockSpec((1, D), lambda i, j: (0, 0))],
        out_shape=[jax.ShapeDtypeStruct((T, D), BF16),
                   jax.ShapeDtypeStruct((nj, T, tf), BF16), jax.ShapeDtypeStruct((nj, T, tf), BF16),
                   jax.ShapeDtypeStruct((nj, T, tf), BF16),
                   jax.ShapeDtypeStruct((T, D), F32),
                   jax.ShapeDtypeStruct((1, D), F32), jax.ShapeDtypeStruct((1, D), F32)],
        scratch_shapes=[pltpu.VMEM((tm, D), F32), pltpu.VMEM((tm, D), BF16)],
        compiler_params=_cparams(("arbitrary", "arbitrary"), 56),
    )(dx2, r2, lng, gs, us, wfi, wfi, wfo)


def _loss_head(y, target, *, name):
    T, D = y.shape
    tm = _tile(T, 1024)

    def body(y_ref, t_ref, dy_ref, sq_ref):
        @pl.when(pl.program_id(0) == 0)
        def _():
            sq_ref[...] = jnp.zeros_like(sq_ref)
        err = y_ref[...] - t_ref[...]
        dy_ref[...] = err * (1.0 / D)
        sq_ref[...] += jnp.sum(err * err, axis=0, keepdims=True)

    return pl.pallas_call(
        body, name=name, grid=(T // tm,),
        in_specs=[pl.BlockSpec((tm, D), lambda i: (i, 0)), pl.BlockSpec((tm, D), lambda i: (i, 0))],
        out_specs=[pl.BlockSpec((tm, D), lambda i: (i, 0)), pl.BlockSpec((1, D), lambda i: (0, 0))],
        out_shape=[jax.ShapeDtypeStruct((T, D), F32), jax.ShapeDtypeStruct((1, D), F32)],
        compiler_params=_cparams(("arbitrary",)),
    )(y, target)


def _my_place():
    return lax.axis_index("x"), lax.axis_index("y"), lax.axis_index("c")


def _peer(place, k):
    x, y, c = place
    return (1 - x if k & 4 else x, 1 - y if k & 2 else y, 1 - c if k & 1 else c)


def _logical(place):
    x, y, c = place
    return 4 * x + 2 * y + c


def _block_of(ref, mode, idx):
    if mode == "blk":
        return ref.at[idx]
    if mode == "col":
        size = ref.shape[2] // N_DEV
        return ref.at[:, :, pl.ds(pl.multiple_of(idx * size, size), size)]
    size = ref.shape[1] // N_DEV
    return ref.at[:, pl.ds(pl.multiple_of(idx * size, size), size), :]


def _full_shape(shard, mode):
    if mode == "blk":
        return (N_DEV,) + shard.shape
    if mode == "col":
        return shard.shape[:2] + (N_DEV * shard.shape[2],)
    return (shard.shape[0], N_DEV * shard.shape[1], shard.shape[2])


class _Exchange:
    def __init__(self, arrays, out_shape, build):
        self.arrays = list(arrays)
        self.out_shape = list(out_shape)
        self.build = build

    def scratch(self):
        n = len(self.arrays)
        return [pltpu.SemaphoreType.DMA((n * N_DEV,)), pltpu.SemaphoreType.DMA((n * N_DEV,)),
                pltpu.SemaphoreType.DMA((n,))]

    def start(self, ins, outs, sems):
        for cp in self.build(ins, outs, *sems):
            cp.start()

    def wait(self, ins, outs, sems):
        for cp in self.build(ins, outs, *sems):
            cp.wait()

    def run(self, name):
        n_in, n_out = len(self.arrays), len(self.out_shape)
        hbm = pl.BlockSpec(memory_space=pltpu.HBM)

        def body(*refs):
            ins, outs, sems = refs[:n_in], refs[n_in:n_in + n_out], refs[n_in + n_out:]
            self.start(ins, outs, sems)
            self.wait(ins, outs, sems)

        return pl.pallas_call(
            body, name=name, in_specs=[hbm] * n_in, out_specs=[hbm] * n_out,
            out_shape=self.out_shape, scratch_shapes=self.scratch(),
        )(*self.arrays)


def _copies_to_all(src_of, dst_of, n, send, recv, local):
    me = _my_place()
    copies = []
    for a in range(n):
        copies.append(pltpu.make_async_copy(src_of(a, _logical(me)), dst_of(a), local.at[a]))
        for k in range(1, N_DEV):
            peer = _peer(me, k)
            copies.append(pltpu.make_async_remote_copy(
                src_ref=src_of(a, _logical(peer)), dst_ref=dst_of(a),
                send_sem=send.at[a * N_DEV + k], recv_sem=recv.at[a * N_DEV + k],
                device_id=peer, device_id_type=MESH))
    return copies


def _gather_job(shards, modes):
    def build(ins, outs, send, recv, local):
        my_id = _logical(_my_place())
        return _copies_to_all(lambda a, dev: ins[a], lambda a: _block_of(outs[a], modes[a], my_id),
                              len(shards), send, recv, local)

    return _Exchange(shards, [jax.ShapeDtypeStruct(_full_shape(s, m), s.dtype) for s, m in zip(shards, modes)], build)


def _grad_block(ref, mode, idx):
    if mode == "blk":
        return ref.at[idx]
    if mode == "col":
        size = ref.shape[1] // N_DEV
        return ref.at[:, pl.ds(pl.multiple_of(idx * size, size), size)]
    size = ref.shape[0] // N_DEV
    return ref.at[pl.ds(pl.multiple_of(idx * size, size), size), :]


def _grad_shard_shape(g, mode):
    if mode == "blk":
        return g.shape[1:]
    if mode == "col":
        return (g.shape[0], g.shape[1] // N_DEV)
    return (g.shape[0] // N_DEV, g.shape[1])


def _grads_job(groups, modes):
    flat = [(g, w, l) for w, per_w in enumerate(groups) for l, g in enumerate(per_w)]

    def build(ins, outs, send, recv, local):
        my_id = _logical(_my_place())
        return _copies_to_all(lambda a, dev: _grad_block(ins[a], modes[flat[a][1]], dev),
                              lambda a: outs[flat[a][1]].at[my_id, flat[a][2]],
                              len(flat), send, recv, local)

    out_shape = [jax.ShapeDtypeStruct((N_DEV, len(per_w)) + _grad_shard_shape(per_w[0], m), per_w[0].dtype)
                 for per_w, m in zip(groups, modes)]
    return _Exchange([g for g, _, _ in flat], out_shape, build)


def _adamw(w, g, m, v):
    m = ADAM_B1 * m + (1.0 - ADAM_B1) * g
    v = ADAM_B2 * v + (1.0 - ADAM_B2) * (g * g)
    m_hat = m / (1.0 - ADAM_B1 ** ADAM_STEP)
    v_hat = v / (1.0 - ADAM_B2 ** ADAM_STEP)
    delta = -ADAM_LR * (m_hat / (jnp.sqrt(v_hat) + ADAM_EPS) + ADAM_WD * w)
    return delta, m, v


def _sum_slots_adamw(slots, w, m, v, *, name):
    R, C = w.shape
    tr = _tile(R, 256)

    def body(s_ref, w_ref, m_ref, v_ref, g_out, d_out, m_out, v_out):
        g = s_ref[0].astype(F32)
        for s in range(1, N_DEV):
            g = g + s_ref[s].astype(F32)
        delta, m_new, v_new = _adamw(w_ref[...], g, m_ref[...], v_ref[...])
        g_out[...] = g
        d_out[...] = delta
        m_out[...] = m_new
        v_out[...] = v_new

    spec = pl.BlockSpec((tr, C), lambda i: (i, 0))
    return pl.pallas_call(
        body, name=name, grid=(R // tr,),
        in_specs=[pl.BlockSpec((N_DEV, tr, C), lambda i: (0, i, 0)), spec, spec, spec],
        out_specs=[spec] * 4,
        out_shape=[jax.ShapeDtypeStruct((R, C), F32)] * 4,
        compiler_params=_cparams(("parallel",)),
    )(slots, w, m, v)


def _small_allreduce_adamw(g, w, m, v, *, name):
    R = g.shape[0]
    vmem = pl.BlockSpec(memory_space=pltpu.VMEM)

    def body(g_ref, w_ref, m_ref, v_ref, g_out, d_out, m_out, v_out, slots, send, recv):
        me = _my_place()
        my_id = _logical(me)
        slots[my_id] = g_ref[...]
        copies = []
        for k in range(1, N_DEV):
            cp = pltpu.make_async_remote_copy(
                src_ref=g_ref, dst_ref=slots.at[my_id], send_sem=send.at[k], recv_sem=recv.at[k],
                device_id=_peer(me, k), device_id_type=MESH)
            cp.start()
            copies.append(cp)
        for cp in copies:
            cp.wait()
        total = slots[0]
        for s in range(1, N_DEV):
            total = total + slots[s]
        delta, m_new, v_new = _adamw(w_ref[...], total, m_ref[...], v_ref[...])
        g_out[...] = total
        d_out[...] = delta
        m_out[...] = m_new
        v_out[...] = v_new

    return pl.pallas_call(
        body, name=name,
        in_specs=[vmem] * 4, out_specs=[vmem] * 4,
        out_shape=[jax.ShapeDtypeStruct((R, LANES), F32)] * 4,
        scratch_shapes=[pltpu.VMEM((N_DEV, R, LANES), F32),
                        pltpu.SemaphoreType.DMA((N_DEV,)), pltpu.SemaphoreType.DMA((N_DEV,))],
    )(g, w, m, v)


def _pack(parts):
    flat = jnp.concatenate([p.reshape(-1) for p in parts])
    rows = -(-flat.shape[0] // (8 * LANES)) * 8
    return jnp.pad(flat, (0, rows * LANES - flat.shape[0])).reshape(rows, LANES)


def _unpack(packed, like):
    flat = packed.reshape(-1)
    out, pos = [], 0
    for p in like:
        out.append(flat[pos:pos + p.size].reshape(p.shape))
        pos += p.size
    return out


def kernel(x, w_in, b_gate, rel_bias, w_proj_a, w_proj_b, w_out, ln1_g, ln1_b, w_ffn_in, w_ffn_out, ln2_g, ln2_b, loss_target, m_w_in, m_b_gate, m_rel_bias, m_w_proj_a, m_w_proj_b, m_w_out, m_ln1_g, m_ln1_b, m_w_ffn_in, m_w_ffn_out, m_ln2_g, m_ln2_b, v_w_in, v_b_gate, v_rel_bias, v_w_proj_a, v_w_proj_b, v_w_out, v_ln1_g, v_ln1_b, v_w_ffn_in, v_w_ffn_out, v_ln2_g, v_ln2_b):
    L = w_in.shape[0]
    T, D = x.shape[1], x.shape[2]
    alpha = float((2 * L) ** 0.25)
    n_qkv = 6 * WIDTH

    big = [w_in, w_proj_a, w_proj_b, w_out, w_ffn_in, w_ffn_out]
    modes = ["col", "col", "col", "row", "blk", "row"]
    w_bf = [w.astype(BF16) for w in big]
    (W_in_first,) = _gather_job([w_bf[0][:1]], modes[:1]).run("gather_w_in_first")
    later = ([w_bf[0][1:]] if L > 1 else []) + w_bf[1:]
    later_modes = (modes[:1] if L > 1 else []) + modes[1:]
    vec3 = lambda a: a[:, None, :]
    bg3, l1g, l1b, l2g, l2b = vec3(b_gate), vec3(ln1_g), vec3(ln1_b), vec3(ln2_g), vec3(ln2_b)
    b_col0 = 3 * WIDTH // LANES

    h = x[0]
    saved = []
    for l in range(L):
        W_in, wl = (W_in_first, 0) if l == 0 else (W_in_later, l - 1)
        qkv = _mm_nn(h, W_in, wl, col_off=0, n_cols=n_qkv, out_dtype=BF16, name=f"in_proj_qkv_{l}")
        gates = _mm_nn(h, W_in, wl, col_off=n_qkv, n_cols=2 * D, out_dtype=F32, name=f"in_proj_gates_{l}")
        kvpad = jnp.pad(qkv[:, WIDTH:3 * WIDTH], ((A_WIN - A_TQ, 0), (0, 0)))
        bias = _toeplitz_bias(rel_bias[l])
        oa = _attn_a_fwd(qkv, kvpad, bias, name=f"attn_a_fwd_{l}")
        if l == 0:
            ob, gathered = _attn_b_fwd(qkv, col0=b_col0, name=f"attn_b_fwd_{l}", job=_gather_job(later, later_modes))
            if L > 1:
                W_in_later = gathered[0]
            W_pa, W_pb, W_o, W_fi, W_fo = gathered[-5:]
        else:
            ob, _ = _attn_b_fwd(qkv, col0=b_col0, name=f"attn_b_fwd_{l}")
        x1, r1 = _proj_fwd(oa, ob, gates, bg3, W_pa, W_pb, W_o, h, l1g, l1b, l, alpha=alpha, name=f"proj_fwd_{l}")
        gs, us, r2, x2 = _ffn_fwd(x1, W_fi, W_fo, l2g, l2b, l, alpha=alpha, name=f"ffn_fwd_{l}")
        saved.append((h, qkv, gates, kvpad, bias, oa, ob, x1, r1, gs, us, r2))
        h = x2

    d_h, sq = _loss_head(h, loss_target[0], name="loss_head")
    loss = lax.psum((0.5 / D) * jnp.sum(sq), ("x", "y", "c"))

    g_in, g_pa, g_pb, g_o, g_fi, g_fo = ([None] * L for _ in range(6))
    g_bg, g_rb, g_l1g, g_l1b, g_l2g, g_l2b = ([None] * L for _ in range(6))
    for l in reversed(range(L)):
        xin, qkv, gates, kvpad, bias, oa, ob, x1, r1, gs, us, r2 = saved[l]
        dr2, act, dgt, dup, dx1, g_l2g[l], g_l2b[l] = _ffn_bwd(d_h, r2, l2g, gs, us, W_fi, W_fo, l,
                                                              alpha=alpha, name=f"ffn_bwd_{l}")
        g_fo[l] = _mm_tn_blocked_a(act, dr2, name=f"grad_w_ffn_out_{l}").reshape(-1, D)
        g_fi[l] = jnp.concatenate([_mm_tn_blocked(x1, dgt, name=f"grad_w_ffn_gate_{l}"),
                                   _mm_tn_blocked(x1, dup, name=f"grad_w_ffn_up_{l}")], axis=0)
        (dr1, mixin, dya, dyb, dgates, doa, dob, g_l1g[l], g_l1b[l], g_bg[l]) = _proj_bwd(
            dx1, r1, l1g, oa, ob, gates, bg3, W_pa, W_pb, W_o, l, name=f"proj_bwd_{l}")
        g_o[l] = _mm_tn(mixin, dr1, tm=_tile(D, 1024), tn=_tile(D, 1024), name=f"grad_w_out_{l}")
        g_pa[l] = _mm_tn(oa, dya, tm=WIDTH, tn=_tile(D, 1024), name=f"grad_w_proj_a_{l}")
        g_pb[l] = _mm_tn(ob, dyb, tm=WIDTH, tn=_tile(D, 1024), name=f"grad_w_proj_b_{l}")
        dqa, dka, dva, dbias = _attn_a_bwd(qkv, kvpad, bias, doa, name=f"attn_a_bwd_{l}")
        g_rb[l] = _toeplitz_bias_grad(dbias)
        if l == 0:
            groups = ([g_in[1:]] if L > 1 else []) + [g_pa, g_pb, g_o, g_fi, g_fo]
            job = _grads_job(groups, (modes[:1] if L > 1 else []) + modes[1:])
            (dqb, dkb, dvb), slots_later = _attn_b_bwd(qkv, ob, dob, col0=b_col0, name=f"attn_b_bwd_{l}", job=job)
        else:
            (dqb, dkb, dvb), _ = _attn_b_bwd(qkv, ob, dob, col0=b_col0, name=f"attn_b_bwd_{l}")
        pad = A_WIN - A_TQ
        d_pre = jnp.concatenate([dqa, dka[pad:].astype(BF16), dva[pad:].astype(BF16),
                                 dqb, dkb.astype(BF16), dvb.astype(BF16), dgates], axis=1)
        g_in[l] = _mm_tn(xin, d_pre, tm=D, tn=w_in.shape[2], name=f"grad_w_in_{l}")
        W_in, wl = (W_in_first, 0) if l == 0 else (W_in_later, l - 1)
        d_h = _mm_nt_add(d_pre, W_in, wl, dr1, alpha, name=f"grad_x_{l}")
    grad_x = d_h[None]

    (slots_in,) = _grads_job([g_in[:1]], modes[:1]).run("exchange_grad_w_in_first")
    if L > 1:
        slots_in = jnp.concatenate([slots_in, slots_later[0]], axis=1)
    slots = [slots_in] + list(slots_later[-5:])
    moments_m = [m_w_in, m_w_proj_a, m_w_proj_b, m_w_out, m_w_ffn_in, m_w_ffn_out]
    moments_v = [v_w_in, v_w_proj_a, v_w_proj_b, v_w_out, v_w_ffn_in, v_w_ffn_out]
    names = ["w_in", "w_proj_a", "w_proj_b", "w_out", "w_ffn_in", "w_ffn_out"]
    big_out = {}
    for nm, s, w, m, v in zip(names, slots, big, moments_m, moments_v):
        two = lambda a: a.reshape(-1, a.shape[-1])
        res = _sum_slots_adamw(s.reshape(N_DEV, -1, s.shape[-1]), two(w), two(m), two(v), name=f"adamw_{nm}")
        big_out[nm] = [r.reshape(w.shape) for r in res]

    small_w = [b_gate, rel_bias, ln1_g, ln1_b, ln2_g, ln2_b]
    small_g = [jnp.stack(g) for g in (g_bg, g_rb, g_l1g, g_l1b, g_l2g, g_l2b)]
    small_m = [m_b_gate, m_rel_bias, m_ln1_g, m_ln1_b, m_ln2_g, m_ln2_b]
    small_v = [v_b_gate, v_rel_bias, v_ln1_g, v_ln1_b, v_ln2_g, v_ln2_b]
    res = _small_allreduce_adamw(_pack(small_g), _pack(small_w), _pack(small_m), _pack(small_v),
                                 name="allreduce_small_adamw")
    small_names = ["b_gate", "rel_bias", "ln1_g", "ln1_b", "ln2_g", "ln2_b"]
    small_out = {nm: [] for nm in small_names}
    for packed in res:
        for nm, arr in zip(small_names, _unpack(packed, small_w)):
            small_out[nm].append(arr)

    order = ["w_in", "b_gate", "rel_bias", "w_proj_a", "w_proj_b", "w_out", "ln1_g", "ln1_b",
             "w_ffn_in", "w_ffn_out", "ln2_g", "ln2_b"]
    every = {**big_out, **small_out}
    outs = [loss, grad_x]
    for kind in range(4):
        outs += [every[nm][kind] for nm in order]
    return tuple(outs)
```

```python
import functools
import math

import jax
import jax.numpy as jnp
from jax import lax
from jax.experimental import pallas as pl
from jax.experimental.pallas import tpu as pltpu

F32 = jnp.float32
BF16 = jnp.bfloat16

HEAD_DIM = 64
CHUNK = 64
LEFT_CHUNKS = 8
REL_CLIP = 256
N_REL = 2 * REL_CLIP + 1
WIDTH = 512
LANES = 128
A_TQ = 256
A_WIN = A_TQ + LEFT_CHUNKS * CHUNK
B_TQ = 1024
B_TS = 256
B_PIECE = 64
B_DEAD = -160.0
LN_EPS = 1e-5
QK_SCALE = 1.0 / math.sqrt(HEAD_DIM)
LOG2E = 1.4426950408889634
NEG = -1e30

ADAM_LR = 0.001
ADAM_B1 = 0.9
ADAM_B2 = 0.999
ADAM_EPS = 1e-08
ADAM_WD = 0.01
ADAM_STEP = 10

N_DEV = 8
MESH = pl.DeviceIdType.MESH
MIB = 1024 * 1024


def _cparams(sem=None, vmem_mib=48):
    return pltpu.CompilerParams(dimension_semantics=sem, vmem_limit_bytes=vmem_mib * MIB)


def _dot(a, b):
    return jnp.dot(a, b, preferred_element_type=F32)


def _dot_nt(a, b):
    return lax.dot_general(a, b, (((1,), (1,)), ((), ())), preferred_element_type=F32)


def _dot_tn(a, b):
    return lax.dot_general(a, b, (((0,), (0,)), ((), ())), preferred_element_type=F32)


def _tile(n, pref):
    if n <= pref:
        return n
    for t in range(pref - pref % 8, 0, -8):
        if n % t == 0:
            return t
    raise ValueError((n, pref))


def _mm_nn(a, w, layer, *, col_off, n_cols, out_dtype, name):
    M, K = a.shape
    tm = _tile(M, 1024)
    tn = _tile(n_cols, 512)
    assert col_off % tn == 0
    off = col_off // tn

    def body(a_ref, w_ref, o_ref):
        o_ref[...] = _dot(a_ref[...].astype(BF16), w_ref[...]).astype(out_dtype)

    return pl.pallas_call(
        body, name=name, grid=(M // tm, n_cols // tn),
        in_specs=[pl.BlockSpec((tm, K), lambda i, j: (i, 0)),
                  pl.BlockSpec((None, K, tn), lambda i, j: (layer, 0, j + off))],
        out_specs=pl.BlockSpec((tm, tn), lambda i, j: (i, j)),
        out_shape=jax.ShapeDtypeStruct((M, n_cols), out_dtype),
        compiler_params=_cparams(("parallel", "parallel")),
    )(a, w)


def _mm_nt_add(a, w, layer, add, add_scale, *, name, job=None):
    M, K = a.shape
    N = w.shape[1]
    tm = _tile(M, 1024)
    tk = _tile(K, 1024)

    def body(a_ref, w_ref, add_ref, o_ref):
        @pl.when(pl.program_id(1) == 0)
        def _():
            o_ref[...] = add_scale * add_ref[...]
        o_ref[...] += _dot_nt(a_ref[...], w_ref[...])

    (out,), rode = _call_carrying(
        job, body, name=name, grid=(M // tm, K // tk),
        in_specs=[pl.BlockSpec((tm, tk), lambda i, k: (i, k)),
                  pl.BlockSpec((None, N, tk), lambda i, k: (layer, 0, k)),
                  pl.BlockSpec((tm, N), lambda i, k: (i, 0))],
        out_specs=[pl.BlockSpec((tm, N), lambda i, k: (i, 0))],
        out_shape=[jax.ShapeDtypeStruct((M, N), F32)],
        scratch_shapes=[], vmem_mib=48, args=(a, w, add))
    return out, rode


def _tn_body(k_axis, n_k):
    def body(a_ref, b_ref, o_ref, acc_ref):
        k = pl.program_id(k_axis)

        @pl.when(k == 0)
        def _():
            acc_ref[...] = jnp.zeros_like(acc_ref)
        acc_ref[...] += _dot_tn(a_ref[...].astype(BF16), b_ref[...].astype(BF16))

        @pl.when(k == n_k - 1)
        def _():
            o_ref[...] = acc_ref[...].astype(o_ref.dtype)
    return body


def _mm_tn(a, b, *, tm, tn, name):
    T, M = a.shape
    N = b.shape[1]
    tk = _tile(T, 512)
    return pl.pallas_call(
        _tn_body(2, T // tk), name=name, grid=(M // tm, N // tn, T // tk),
        in_specs=[pl.BlockSpec((tk, tm), lambda i, j, k: (k, i)),
                  pl.BlockSpec((tk, tn), lambda i, j, k: (k, j))],
        out_specs=pl.BlockSpec((tm, tn), lambda i, j, k: (i, j)),
        out_shape=jax.ShapeDtypeStruct((M, N), BF16),
        scratch_shapes=[pltpu.VMEM((tm, tn), F32)],
        compiler_params=_cparams(("parallel", "parallel", "arbitrary")),
    )(a, b)


def _mm_tn_blocked(a, b, *, name):
    T, M = a.shape
    S, _, N = b.shape
    tk = _tile(T, 512)
    return pl.pallas_call(
        _tn_body(1, T // tk), name=name, grid=(S, T // tk),
        in_specs=[pl.BlockSpec((tk, M), lambda s, k: (k, 0)),
                  pl.BlockSpec((None, tk, N), lambda s, k: (s, k, 0))],
        out_specs=pl.BlockSpec((None, M, N), lambda s, k: (s, 0, 0)),
        out_shape=jax.ShapeDtypeStruct((S, M, N), BF16),
        scratch_shapes=[pltpu.VMEM((M, N), F32)],
        compiler_params=_cparams(("parallel", "arbitrary")),
    )(a, b)


def _mm_tn_blocked_a(a, b, *, name):
    S, T, M = a.shape
    N = b.shape[1]
    tk = _tile(T, 512)
    return pl.pallas_call(
        _tn_body(1, T // tk), name=name, grid=(S, T // tk),
        in_specs=[pl.BlockSpec((None, tk, M), lambda s, k: (s, k, 0)),
                  pl.BlockSpec((tk, N), lambda s, k: (k, 0))],
        out_specs=pl.BlockSpec((None, M, N), lambda s, k: (s, 0, 0)),
        out_shape=jax.ShapeDtypeStruct((S, M, N), BF16),
        scratch_shapes=[pltpu.VMEM((M, N), F32)],
        compiler_params=_cparams(("parallel", "arbitrary")),
    )(a, b)


def _ln_fwd(r, g, b):
    mu = jnp.mean(r, axis=-1, keepdims=True)
    xc = r - mu
    var = jnp.mean(xc * xc, axis=-1, keepdims=True)
    return xc * lax.rsqrt(var + LN_EPS) * g + b


def _ln_bwd(dy, r, g):
    mu = jnp.mean(r, axis=-1, keepdims=True)
    xc = r - mu
    var = jnp.mean(xc * xc, axis=-1, keepdims=True)
    rstd = lax.rsqrt(var + LN_EPS)
    xhat = xc * rstd
    dxh = dy * g
    m1 = jnp.mean(dxh, axis=-1, keepdims=True)
    m2 = jnp.mean(dxh * xhat, axis=-1, keepdims=True)
    return rstd * (dxh - m1 - xhat * m2), xhat


def _lane_is_head0():
    return lax.broadcasted_iota(jnp.int32, (1, LANES), 1) < HEAD_DIM


def _band_valid(i):
    r = lax.broadcasted_iota(jnp.int32, (A_TQ, A_WIN), 0)
    c = lax.broadcasted_iota(jnp.int32, (A_TQ, A_WIN), 1)
    a = r // CHUNK
    b = c // CHUNK
    return (b >= a) & (b <= a + LEFT_CHUNKS) & (c + i * A_TQ >= LEFT_CHUNKS * CHUNK)


def _band_probs(q2, k2, bias, valid, head0, h):
    qh = jnp.where(head0 if h == 0 else jnp.logical_not(head0), q2, jnp.zeros_like(q2))
    s = _dot_nt(qh, k2) * QK_SCALE + bias
    s = jnp.where(valid, s, NEG)
    m = jnp.max(s, axis=1, keepdims=True)
    e = jnp.exp(s - m)
    return e / jnp.sum(e, axis=1, keepdims=True)


def _attn_a_fwd(qkv, kvpad, bias, *, name, job=None):
    T = qkv.shape[0]
    n_hp = WIDTH // LANES

    def body(q_ref, k_ref, v_ref, bias_ref, o_ref):
        i = pl.program_id(1)
        row0 = pl.multiple_of(i * A_TQ, A_TQ)
        q2 = q_ref[...]
        k2 = k_ref[pl.ds(row0, A_WIN), :]
        v2 = v_ref[pl.ds(row0, A_WIN), :]
        head0 = _lane_is_head0()
        valid = _band_valid(i)
        outs = []
        for h in range(2):
            p = _band_probs(q2, k2, bias_ref[h], valid, head0, h)
            outs.append(_dot(p.astype(BF16), v2))
        o_ref[...] = jnp.where(head0, outs[0], outs[1]).astype(o_ref.dtype)

    (out,), rode = _call_carrying(
        job, body, name=name, grid=(n_hp, T // A_TQ),
        in_specs=[pl.BlockSpec((A_TQ, LANES), lambda hp, i: (i, hp)),
                  pl.BlockSpec((T + A_WIN - A_TQ, LANES), lambda hp, i: (0, hp)),
                  pl.BlockSpec((T + A_WIN - A_TQ, LANES), lambda hp, i: (0, hp + n_hp)),
                  pl.BlockSpec((2, A_TQ, A_WIN), lambda hp, i: (hp, 0, 0))],
        out_specs=[pl.BlockSpec((A_TQ, LANES), lambda hp, i: (i, hp))],
        out_shape=[jax.ShapeDtypeStruct((T, WIDTH), BF16)],
        scratch_shapes=[], vmem_mib=48, args=(qkv, kvpad, kvpad, bias))
    return out, rode


def _attn_a_bwd(qkv, kvpad, bias, do, *, name, job=None):
    T = qkv.shape[0]
    TP = T + A_WIN - A_TQ
    n_hp = WIDTH // LANES

    def body(q_ref, k_ref, v_ref, bias_ref, do_ref, dq_ref, dk_ref, dv_ref, db_ref):
        i = pl.program_id(1)

        @pl.when(i == 0)
        def _():
            dk_ref[...] = jnp.zeros_like(dk_ref)
            dv_ref[...] = jnp.zeros_like(dv_ref)
            db_ref[...] = jnp.zeros_like(db_ref)

        row0 = pl.multiple_of(i * A_TQ, A_TQ)
        q2 = q_ref[...]
        do2 = do_ref[...]
        k2 = k_ref[pl.ds(row0, A_WIN), :]
        v2 = v_ref[pl.ds(row0, A_WIN), :]
        head0 = _lane_is_head0()
        valid = _band_valid(i)
        dq, dk, dv = [], [], []
        for h in range(2):
            hm = head0 if h == 0 else jnp.logical_not(head0)
            p = _band_probs(q2, k2, bias_ref[h], valid, head0, h)
            doh = jnp.where(hm, do2, jnp.zeros_like(do2))
            dp = _dot_nt(doh, v2)
            delta = jnp.sum(p * dp, axis=1, keepdims=True)
            ds = p * (dp - delta)
            db_ref[h] += ds
            dsb = (ds * QK_SCALE).astype(BF16)
            dq.append(_dot(dsb, k2))
            dk.append(_dot_tn(dsb, q2))
            dv.append(_dot_tn(p.astype(BF16), do2))
        dq_ref[...] = jnp.where(head0, dq[0], dq[1]).astype(dq_ref.dtype)
        dk_ref[pl.ds(row0, A_WIN), :] += jnp.where(head0, dk[0], dk[1])
        dv_ref[pl.ds(row0, A_WIN), :] += jnp.where(head0, dv[0], dv[1])

    return _call_carrying(
        job, body, name=name, grid=(n_hp, T // A_TQ),
        in_specs=[pl.BlockSpec((A_TQ, LANES), lambda hp, i: (i, hp)),
                  pl.BlockSpec((TP, LANES), lambda hp, i: (0, hp)),
                  pl.BlockSpec((TP, LANES), lambda hp, i: (0, hp + n_hp)),
                  pl.BlockSpec((2, A_TQ, A_WIN), lambda hp, i: (hp, 0, 0)),
                  pl.BlockSpec((A_TQ, LANES), lambda hp, i: (i, hp))],
        out_specs=[pl.BlockSpec((A_TQ, LANES), lambda hp, i: (i, hp)),
                   pl.BlockSpec((TP, LANES), lambda hp, i: (0, hp)),
                   pl.BlockSpec((TP, LANES), lambda hp, i: (0, hp)),
                   pl.BlockSpec((2, A_TQ, A_WIN), lambda hp, i: (hp, 0, 0))],
        out_shape=[jax.ShapeDtypeStruct((T, WIDTH), BF16),
                   jax.ShapeDtypeStruct((TP, WIDTH), F32),
                   jax.ShapeDtypeStruct((TP, WIDTH), F32),
                   jax.ShapeDtypeStruct((WIDTH // HEAD_DIM, A_TQ, A_WIN), F32)],
        scratch_shapes=[], vmem_mib=56, args=(qkv, kvpad, kvpad, bias, do))


def _toeplitz_bias(rb):
    H = rb.shape[0]
    span = A_TQ + A_WIN - 1
    n_tail = span - (N_REL - 1)
    ext = jnp.concatenate([rb[:, 1:], jnp.broadcast_to(rb[:, N_REL - 1:], (H, n_tail))], axis=1)
    rev = jnp.pad(ext[:, ::-1], ((0, 0), (0, 1)))
    flat = jnp.broadcast_to(rev[:, None, :], (H, A_TQ, span + 1)).reshape(H, A_TQ * (span + 1))
    skew = flat[:, :A_TQ * span].reshape(H, A_TQ, span)
    return skew[:, :, A_TQ - 1:A_TQ - 1 + A_WIN]


def _toeplitz_bias_grad(db):
    H = db.shape[0]
    span = A_TQ + A_WIN - 1
    d_skew = jnp.pad(db, ((0, 0), (0, 0), (A_TQ - 1, span - (A_TQ - 1) - A_WIN)))
    d_flat = jnp.pad(d_skew.reshape(H, A_TQ * span), ((0, 0), (0, A_TQ)))
    g_ext = jnp.sum(d_flat.reshape(H, A_TQ, span + 1), axis=1)[:, :span][:, ::-1]
    last = g_ext[:, N_REL - 2] + jnp.sum(g_ext[:, N_REL - 1:], axis=1)
    return jnp.concatenate([jnp.zeros((H, 1), F32), g_ext[:, :N_REL - 2], last[:, None]], axis=1)


def _split_bf16(x):
    hi = x.astype(BF16)
    lo = (x - hi.astype(F32)).astype(BF16)
    return hi, lo


def _sb_streams(d):
    out = []
    for h in range(2):
        for r in range(B_TQ // B_TS):
            if d is not None and d > r:
                continue
            out.append(dict(h=h, r=r, rows=pl.ds(r * B_TS, B_TS), diag=(d is not None and d == r)))
    return out


def _piece_rows(st, p):
    return pl.ds(st["r"] * B_TS + p, B_PIECE)


def _rows_cat(parts):
    return jnp.concatenate(parts, axis=0)


def _skewed(streams, stages):
    for t in range(len(streams) + len(stages) - 1):
        for s, st in enumerate(streams):
            if 0 <= t - s < len(stages):
                stages[t - s](st)


def _sb_logs(st, z2):
    log_beta, log_keep, keep_bf = [], [], []
    for p in range(0, B_TS, B_PIECE):
        z = z2[p:p + B_PIECE]
        lp2 = jnp.log(1.0 + jnp.exp2(-jnp.abs(z))) * LOG2E
        lb = jnp.minimum(z, 0.0) - lp2
        lk = lb - z
        if st["diag"]:
            lk = jnp.where(_strict_lower(p), lk, 0.0)
        log_beta.append(lb)
        log_keep.append(lk)
        keep_bf.append(lk.astype(BF16))
    st["log_beta"] = _rows_cat(log_beta)
    st["log_keep"] = _rows_cat(log_keep)
    st["keep_bf"] = _rows_cat(keep_bf)


def _strict_lower(p):
    t = p + lax.broadcasted_iota(jnp.int32, (B_PIECE, B_TS), 0)
    s = lax.broadcasted_iota(jnp.int32, (B_PIECE, B_TS), 1)
    return s < t


def _tri(strict):
    j = lax.broadcasted_iota(jnp.int32, (B_TS, B_TS), 0)
    s = lax.broadcasted_iota(jnp.int32, (B_TS, B_TS), 1)
    return jnp.where(j > s if strict else j >= s, 1.0, 0.0).astype(BF16)


def _call_carrying(job, body, *, name, grid, in_specs, out_specs, out_shape, scratch_shapes, vmem_mib, args):
    n_in, n_out, n_scr = len(in_specs), len(out_specs), len(scratch_shapes)
    if job is None:
        res = pl.pallas_call(body, name=name, grid=grid, in_specs=in_specs, out_specs=out_specs,
                             out_shape=out_shape, scratch_shapes=scratch_shapes,
                             compiler_params=_cparams(("arbitrary",) * len(grid), vmem_mib))(*args)
        return res, []
    j_in, j_out = len(job.arrays), len(job.out_shape)
    hbm = pl.BlockSpec(memory_space=pltpu.HBM)

    def carrying(*refs):
        refs = list(refs)
        ins, refs = refs[:n_in], refs[n_in:]
        j_ins, refs = refs[:j_in], refs[j_in:]
        outs, refs = refs[:n_out], refs[n_out:]
        j_outs, refs = refs[:j_out], refs[j_out:]
        scr, sems = refs[:n_scr], refs[n_scr:]
        first = functools.reduce(jnp.logical_and, [pl.program_id(d) == 0 for d in range(len(grid))])
        last = functools.reduce(jnp.logical_and, [pl.program_id(d) == grid[d] - 1 for d in range(len(grid))])

        @pl.when(first)
        def _():
            job.start(j_ins, j_outs, sems)

        body(*ins, *outs, *scr)

        @pl.when(last)
        def _():
            job.wait(j_ins, j_outs, sems)

    res = pl.pallas_call(
        carrying, name=name, grid=grid,
        in_specs=list(in_specs) + [hbm] * j_in, out_specs=list(out_specs) + [hbm] * j_out,
        out_shape=list(out_shape) + job.out_shape, scratch_shapes=list(scratch_shapes) + job.scratch(),
        compiler_params=_cparams(("arbitrary",) * len(grid), vmem_mib))(*args, *job.arrays)
    return res[:n_out], res[n_out:]


def _attn_b_fwd(qkv, *, col0, name, job=None):
    T = qkv.shape[0]
    n_hp = WIDTH // LANES
    sub = B_TQ // B_TS

    def body(q_ref, k_ref, v_ref, o_ref, acc_ref, car_ref, qh_ref):
        i = pl.program_id(1)
        q2 = q_ref[...]
        head0 = _lane_is_head0()
        qh_ref[0] = jnp.where(head0, q2, jnp.zeros_like(q2))
        qh_ref[1] = jnp.where(head0, jnp.zeros_like(q2), q2)
        tri_s = _tri(True)
        acc_ref[...] = jnp.zeros_like(acc_ref)
        car_ref[...] = jnp.zeros_like(car_ref)

        def tile(kb, d):
            k0 = pl.multiple_of(kb * B_TS, B_TS)
            k2 = k_ref[pl.ds(k0, B_TS), :]
            v2 = v_ref[pl.ds(k0, B_TS), :]

            def scores(st):
                st["z2"] = _dot_nt(qh_ref[st["h"], st["rows"], :], k2) * (QK_SCALE * LOG2E)

            def logs(st):
                _sb_logs(st, st.pop("z2"))

            def suffix(st):
                st["suffix"] = _dot(st.pop("keep_bf"), tri_s)

            def weights(st):
                log_beta, suffix, log_keep = st.pop("log_beta"), st.pop("suffix"), st.pop("log_keep")
                wb = []
                for p in range(0, B_TS, B_PIECE):
                    rows = _piece_rows(st, p)
                    car = car_ref[st["h"], rows, :]
                    w = jnp.exp2(log_beta[p:p + B_PIECE] + suffix[p:p + B_PIECE] + car)
                    if st["diag"]:
                        w = jnp.where(_strict_lower(p), w, 0.0)
                    wb.append(w.astype(BF16))
                    car_ref[st["h"], rows, :] = car + jnp.sum(log_keep[p:p + B_PIECE], axis=1, keepdims=True)
                st["wb"] = _rows_cat(wb)

            def values(st):
                acc_ref[st["h"], st["rows"], :] += _dot(st.pop("wb"), v2)

            _skewed(_sb_streams(d), [scores, logs, suffix, weights, values])

        for d in reversed(range(sub)):
            tile(i * sub + d, d)

        def alive(c):
            return (c[0] < i * sub) & (c[1] > B_DEAD)

        def step(c):
            tile(i * sub - 1 - c[0], None)
            return c[0] + 1, jnp.max(car_ref[...])

        lax.while_loop(alive, step, (jnp.int32(0), jnp.float32(0.0)))
        o_ref[...] = jnp.where(head0, acc_ref[0], acc_ref[1])

    (out,), rode = _call_carrying(
        job, body, name=name, grid=(n_hp, T // B_TQ),
        in_specs=[pl.BlockSpec((B_TQ, LANES), lambda hp, i: (i, hp + col0)),
                  pl.BlockSpec((T, LANES), lambda hp, i: (0, hp + col0 + n_hp)),
                  pl.BlockSpec((T, LANES), lambda hp, i: (0, hp + col0 + 2 * n_hp))],
        out_specs=[pl.BlockSpec((B_TQ, LANES), lambda hp, i: (i, hp))],
        out_shape=[jax.ShapeDtypeStruct((T, WIDTH), F32)],
        scratch_shapes=[pltpu.VMEM((2, B_TQ, LANES), F32), pltpu.VMEM((2, B_TQ, 1), F32),
                        pltpu.VMEM((2, B_TQ, LANES), BF16)],
        vmem_mib=48, args=(qkv, qkv, qkv))
    return out, rode


def _attn_b_bwd(qkv, out, do, *, col0, name, job=None):
    T = qkv.shape[0]
    n_hp = WIDTH // LANES
    sub = B_TQ // B_TS

    def body(q_ref, k_ref, v_ref, o_ref, do_ref, dq_ref, dk_ref, dv_ref,
             dqa_ref, car_ref, carr_ref, tot_ref, qh_ref, doh_ref, qs_ref):
        i = pl.program_id(1)

        @pl.when(i == 0)
        def _():
            dk_ref[...] = jnp.zeros_like(dk_ref)
            dv_ref[...] = jnp.zeros_like(dv_ref)

        q2 = q_ref[...]
        do2 = do_ref[...]
        head0 = _lane_is_head0()
        zero = jnp.zeros_like(q2)
        qh_ref[0] = jnp.where(head0, q2, zero)
        qh_ref[1] = jnp.where(head0, zero, q2)
        doh_ref[0] = jnp.where(head0, do2, zero)
        doh_ref[1] = jnp.where(head0, zero, do2)
        scale = jnp.asarray(QK_SCALE, BF16)
        qs_ref[...] = q2 * scale
        tri_s = _tri(True)
        tri_i = _tri(False)
        prod = do2.astype(F32) * o_ref[...]
        tot_ref[0] = jnp.sum(jnp.where(head0, prod, 0.0), axis=1, keepdims=True)
        tot_ref[1] = jnp.sum(jnp.where(head0, 0.0, prod), axis=1, keepdims=True)
        dqa_ref[...] = jnp.zeros_like(dqa_ref)
        car_ref[...] = jnp.zeros_like(car_ref)
        carr_ref[...] = jnp.zeros_like(carr_ref)

        def tile(kb, d):
            k0 = pl.multiple_of(kb * B_TS, B_TS)
            keys = pl.ds(k0, B_TS)
            k2 = k_ref[keys, :]
            v2 = v_ref[keys, :]
            k2s = k2 * scale

            def scores(st):
                st["z2"] = _dot_nt(qh_ref[st["h"], st["rows"], :], k2) * (QK_SCALE * LOG2E)
                st["dw"] = _dot_nt(doh_ref[st["h"], st["rows"], :], v2)

            def logs(st):
                _sb_logs(st, st.pop("z2"))

            def suffix(st):
                st["suffix"] = _dot(st.pop("keep_bf"), tri_s)

            def weights(st):
                h = st["h"]
                suffix, dw = st.pop("suffix"), st.pop("dw")
                wb, dlog, hi, lo = [], [], [], []
                for p in range(0, B_TS, B_PIECE):
                    rows = _piece_rows(st, p)
                    car = car_ref[h, rows, :]
                    w = jnp.exp2(st["log_beta"][p:p + B_PIECE] + suffix[p:p + B_PIECE] + car)
                    if st["diag"]:
                        w = jnp.where(_strict_lower(p), w, 0.0)
                    w = w.astype(BF16)
                    dl = w.astype(F32) * dw[p:p + B_PIECE]
                    dl_hi, dl_lo = _split_bf16(dl)
                    wb.append(w)
                    dlog.append(dl)
                    hi.append(dl_hi)
                    lo.append(dl_lo)
                    car_ref[h, rows, :] = car + jnp.sum(st["log_keep"][p:p + B_PIECE], axis=1, keepdims=True)
                st["wb"], st["dlog"], st["hi"], st["lo"] = _rows_cat(wb), _rows_cat(dlog), _rows_cat(hi), _rows_cat(lo)

            def later(st):
                st["later"] = _dot(st.pop("hi"), tri_i) + _dot(st.pop("lo"), tri_i)

            def dscores(st):
                h = st["h"]
                later, dlog = st.pop("later"), st.pop("dlog")
                log_keep, log_beta = st.pop("log_keep"), st.pop("log_beta")
                dzb = []
                for p in range(0, B_TS, B_PIECE):
                    rows = _piece_rows(st, p)
                    pc = slice(p, p + B_PIECE)
                    carr = carr_ref[h, rows, :]
                    earlier = tot_ref[h, rows, :] - (later[pc] + carr)
                    dz = dlog[pc] * jnp.exp2(log_keep[pc]) - jnp.exp2(log_beta[pc]) * earlier
                    if st["diag"]:
                        dz = jnp.where(_strict_lower(p), dz, 0.0)
                    dzb.append(dz.astype(BF16))
                    carr_ref[h, rows, :] = carr + jnp.sum(dlog[pc], axis=1, keepdims=True)
                st["dzb"] = _rows_cat(dzb)

            def grads(st):
                h, rows = st["h"], st["rows"]
                mine = head0 if h == 0 else jnp.logical_not(head0)
                dzb = st.pop("dzb")
                dqa_ref[h, rows, :] += _dot(dzb, k2s)
                dk_ref[keys, :] += jnp.where(mine, _dot_tn(dzb, qs_ref[rows, :]), 0.0)
                dv_ref[keys, :] += jnp.where(mine, _dot_tn(st.pop("wb"), do_ref[rows, :]), 0.0)

            _skewed(_sb_streams(d), [scores, logs, suffix, weights, later, dscores, grads])

        for d in reversed(range(sub)):
            tile(i * sub + d, d)

        def alive(c):
            return (c[0] < i * sub) & (c[1] > B_DEAD)

        def step(c):
            tile(i * sub - 1 - c[0], None)
            return c[0] + 1, jnp.max(car_ref[...])

        lax.while_loop(alive, step, (jnp.int32(0), jnp.float32(0.0)))
        dq_ref[...] = jnp.where(head0, dqa_ref[0], dqa_ref[1]).astype(dq_ref.dtype)

    return _call_carrying(
        job, body, name=name, grid=(n_hp, T // B_TQ),
        in_specs=[pl.BlockSpec((B_TQ, LANES), lambda hp, i: (i, hp + col0)),
                  pl.BlockSpec((T, LANES), lambda hp, i: (0, hp + col0 + n_hp)),
                  pl.BlockSpec((T, LANES), lambda hp, i: (0, hp + col0 + 2 * n_hp)),
                  pl.BlockSpec((B_TQ, LANES), lambda hp, i: (i, hp)),
                  pl.BlockSpec((B_TQ, LANES), lambda hp, i: (i, hp))],
        out_specs=[pl.BlockSpec((B_TQ, LANES), lambda hp, i: (i, hp)),
                   pl.BlockSpec((T, LANES), lambda hp, i: (0, hp)),
                   pl.BlockSpec((T, LANES), lambda hp, i: (0, hp))],
        out_shape=[jax.ShapeDtypeStruct((T, WIDTH), BF16),
                   jax.ShapeDtypeStruct((T, WIDTH), F32),
                   jax.ShapeDtypeStruct((T, WIDTH), F32)],
        scratch_shapes=[pltpu.VMEM((2, B_TQ, LANES), F32), pltpu.VMEM((2, B_TQ, 1), F32),
                        pltpu.VMEM((2, B_TQ, 1), F32), pltpu.VMEM((2, B_TQ, 1), F32),
                        pltpu.VMEM((2, B_TQ, LANES), BF16), pltpu.VMEM((2, B_TQ, LANES), BF16),
                        pltpu.VMEM((B_TQ, LANES), BF16)],
        vmem_mib=56, args=(qkv, qkv, qkv, out, do))


def _gated_mix(oa_ref, ob_ref, g_ref, bg_ref, wpa_ref, wpb_ref, D):
    ya = _dot(oa_ref[...].astype(BF16), wpa_ref[...])
    yb = _dot(ob_ref[...].astype(BF16), wpb_ref[...])
    sa = jax.nn.sigmoid(g_ref[:, :D] + bg_ref[:, :D])
    sb = jax.nn.sigmoid(g_ref[:, D:] + bg_ref[:, D:])
    return ya, yb, sa, sb


def _proj_fwd(oa, ob, g, bg, wpa, wpb, wo, wl, xin, lng, lnb, layer, *, alpha, name):
    T, D = xin.shape
    tm = _tile(T, 512)
    row = lambda i: (i, 0)
    wspec = lambda r, c: pl.BlockSpec((None, r, c), lambda i: (wl, 0, 0))
    vec = lambda c: pl.BlockSpec((None, 1, c), lambda i: (layer, 0, 0))

    def body(oa_ref, ob_ref, g_ref, bg_ref, wpa_ref, wpb_ref, wo_ref, x_ref, lg_ref, lb_ref, x1_ref, r1_ref):
        ya, yb, sa, sb = _gated_mix(oa_ref, ob_ref, g_ref, bg_ref, wpa_ref, wpb_ref, D)
        mix = _dot((sa * ya + sb * yb).astype(BF16), wo_ref[...])
        r1 = alpha * x_ref[...] + mix
        r1_ref[...] = r1
        x1_ref[...] = _ln_fwd(r1, lg_ref[...], lb_ref[...])

    return pl.pallas_call(
        body, name=name, grid=(T // tm,),
        in_specs=[pl.BlockSpec((tm, WIDTH), row), pl.BlockSpec((tm, WIDTH), row), pl.BlockSpec((tm, 2 * D), row),
                  vec(2 * D), wspec(WIDTH, D), wspec(WIDTH, D), wspec(D, D),
                  pl.BlockSpec((tm, D), row), vec(D), vec(D)],
        out_specs=[pl.BlockSpec((tm, D), row), pl.BlockSpec((tm, D), row)],
        out_shape=[jax.ShapeDtypeStruct((T, D), F32), jax.ShapeDtypeStruct((T, D), F32)],
        compiler_params=_cparams(("arbitrary",), 56),
    )(oa, ob, g, bg, wpa, wpb, wo, xin, lng, lnb)


def _proj_bwd(dx1, r1, lng, oa, ob, g, bg, wpa, wpb, wo, wl, layer, *, name):
    T, D = dx1.shape
    tm = _tile(T, 512)
    row = lambda i: (i, 0)
    fixed = lambda i: (0, 0)
    wspec = lambda r, c: pl.BlockSpec((None, r, c), lambda i: (wl, 0, 0))
    vec = lambda c: pl.BlockSpec((None, 1, c), lambda i: (layer, 0, 0))

    def body(dx_ref, r1_ref, lg_ref, oa_ref, ob_ref, g_ref, bg_ref, wpa_ref, wpb_ref, wo_ref,
             dr_ref, mix_ref, dya_ref, dyb_ref, dg_ref, doa_ref, dob_ref, dlg_ref, dlb_ref, dbg_ref):
        @pl.when(pl.program_id(0) == 0)
        def _():
            dlg_ref[...] = jnp.zeros_like(dlg_ref)
            dlb_ref[...] = jnp.zeros_like(dlb_ref)
            dbg_ref[...] = jnp.zeros_like(dbg_ref)

        dx = dx_ref[...]
        dr, xhat = _ln_bwd(dx, r1_ref[...], lg_ref[...])
        dr_ref[...] = dr
        dlg_ref[...] += jnp.sum(dx * xhat, axis=0, keepdims=True)
        dlb_ref[...] += jnp.sum(dx, axis=0, keepdims=True)
        dmix = _dot_nt(dr.astype(BF16), wo_ref[...])
        ya, yb, sa, sb = _gated_mix(oa_ref, ob_ref, g_ref, bg_ref, wpa_ref, wpb_ref, D)
        mix_ref[...] = (sa * ya + sb * yb).astype(BF16)
        dya = (dmix * sa).astype(BF16)
        dyb = (dmix * sb).astype(BF16)
        dya_ref[...] = dya
        dyb_ref[...] = dyb
        dga = dmix * ya * (sa * (1.0 - sa))
        dgb = dmix * yb * (sb * (1.0 - sb))
        dg_ref[:, :D] = dga.astype(BF16)
        dg_ref[:, D:] = dgb.astype(BF16)
        dbg_ref[:, :D] += jnp.sum(dga, axis=0, keepdims=True)
        dbg_ref[:, D:] += jnp.sum(dgb, axis=0, keepdims=True)
        doa_ref[...] = _dot_nt(dya, wpa_ref[...]).astype(BF16)
        dob_ref[...] = _dot_nt(dyb, wpb_ref[...]).astype(BF16)

    return pl.pallas_call(
        body, name=name, grid=(T // tm,),
        in_specs=[pl.BlockSpec((tm, D), row), pl.BlockSpec((tm, D), row), vec(D),
                  pl.BlockSpec((tm, WIDTH), row), pl.BlockSpec((tm, WIDTH), row), pl.BlockSpec((tm, 2 * D), row),
                  vec(2 * D), wspec(WIDTH, D), wspec(WIDTH, D), wspec(D, D)],
        out_specs=[pl.BlockSpec((tm, D), row), pl.BlockSpec((tm, D), row), pl.BlockSpec((tm, D), row),
                   pl.BlockSpec((tm, D), row), pl.BlockSpec((tm, 2 * D), row),
                   pl.BlockSpec((tm, WIDTH), row), pl.BlockSpec((tm, WIDTH), row),
                   pl.BlockSpec((1, D), fixed), pl.BlockSpec((1, D), fixed), pl.BlockSpec((1, 2 * D), fixed)],
        out_shape=[jax.ShapeDtypeStruct((T, D), F32), jax.ShapeDtypeStruct((T, D), BF16),
                   jax.ShapeDtypeStruct((T, D), BF16), jax.ShapeDtypeStruct((T, D), BF16),
                   jax.ShapeDtypeStruct((T, 2 * D), BF16),
                   jax.ShapeDtypeStruct((T, WIDTH), BF16), jax.ShapeDtypeStruct((T, WIDTH), BF16),
                   jax.ShapeDtypeStruct((1, D), F32), jax.ShapeDtypeStruct((1, D), F32),
                   jax.ShapeDtypeStruct((1, 2 * D), F32)],
        compiler_params=_cparams(("arbitrary",), 56),
    )(dx1, r1, lng, oa, ob, g, bg, wpa, wpb, wo)


def _ffn_fwd(x1, wfi, wfo, wl, lng, lnb, layer, *, alpha, name, job=None):
    T, D = x1.shape
    tf = wfi.shape[-1]
    nj = wfi.shape[0] // 2
    tm = _tile(T, 512)
    vec = lambda c: pl.BlockSpec((None, 1, c), lambda i, j: (layer, 0, 0))

    def body(x_ref, wg_ref, wu_ref, wo_ref, lg_ref, lb_ref, gs_ref, us_ref, r2_ref, x2_ref, acc_ref, xb_ref):
        j = pl.program_id(1)

        @pl.when(j == 0)
        def _():
            xb_ref[...] = x_ref[...].astype(BF16)
            acc_ref[...] = jnp.zeros_like(acc_ref)

        gv = _dot(xb_ref[...], wg_ref[...])
        uv = _dot(xb_ref[...], wu_ref[...])
        gs_ref[...] = gv
        us_ref[...] = uv
        act = gv * jax.nn.sigmoid(gv) * uv
        acc_ref[...] += _dot(act.astype(BF16), wo_ref[...])

        @pl.when(j == nj - 1)
        def _():
            r2 = alpha * x_ref[...] + acc_ref[...]
            r2_ref[...] = r2
            x2_ref[...] = _ln_fwd(r2, lg_ref[...], lb_ref[...])

    return _call_carrying(
        job, body, name=name, grid=(T // tm, nj),
        in_specs=[pl.BlockSpec((tm, D), lambda i, j: (i, 0)),
                  pl.BlockSpec((None, None, D, tf), lambda i, j: (j, wl, 0, 0)),
                  pl.BlockSpec((None, None, D, tf), lambda i, j: (j + nj, wl, 0, 0)),
                  pl.BlockSpec((None, tf, D), lambda i, j: (wl, j, 0)),
                  vec(D), vec(D)],
        out_specs=[pl.BlockSpec((None, tm, tf), lambda i, j: (j, i, 0)),
                   pl.BlockSpec((None, tm, tf), lambda i, j: (j, i, 0)),
                   pl.BlockSpec((tm, D), lambda i, j: (i, 0)),
                   pl.BlockSpec((tm, D), lambda i, j: (i, 0))],
        out_shape=[jax.ShapeDtypeStruct((nj, T, tf), F32), jax.ShapeDtypeStruct((nj, T, tf), F32),
                   jax.ShapeDtypeStruct((T, D), F32), jax.ShapeDtypeStruct((T, D), F32)],
        scratch_shapes=[pltpu.VMEM((tm, D), F32), pltpu.VMEM((tm, D), BF16)],
        vmem_mib=56, args=(x1, wfi, wfi, wfo, lng, lnb))


def _ffn_bwd(dx2, r2, lng, gs, us, wfi, wfo, wl, layer, *, alpha, name, job=None):
    T, D = dx2.shape
    tf = wfi.shape[-1]
    nj = wfi.shape[0] // 2
    tm = _tile(T, 512)
    vec = lambda c: pl.BlockSpec((None, 1, c), lambda i, j: (layer, 0, 0))
    blk = lambda: pl.BlockSpec((None, tm, tf), lambda i, j: (j, i, 0))

    def body(dx_ref, r2_ref, lg_ref, gs_ref, us_ref, wg_ref, wu_ref, wo_ref,
             dr_ref, act_ref, dg_ref, du_ref, dx1_ref, dlg_ref, dlb_ref, acc_ref, drb_ref):
        i = pl.program_id(0)
        j = pl.program_id(1)

        @pl.when((i == 0) & (j == 0))
        def _():
            dlg_ref[...] = jnp.zeros_like(dlg_ref)
            dlb_ref[...] = jnp.zeros_like(dlb_ref)

        @pl.when(j == 0)
        def _():
            dx = dx_ref[...]
            dr, xhat = _ln_bwd(dx, r2_ref[...], lg_ref[...])
            dlg_ref[...] += jnp.sum(dx * xhat, axis=0, keepdims=True)
            dlb_ref[...] += jnp.sum(dx, axis=0, keepdims=True)
            drb_ref[...] = dr.astype(BF16)
            dr_ref[...] = dr.astype(BF16)
            acc_ref[...] = alpha * dr

        dact = _dot_nt(drb_ref[...], wo_ref[...])
        gv = gs_ref[...]
        uv = us_ref[...]
        s = jax.nn.sigmoid(gv)
        silu = gv * s
        act_ref[...] = (silu * uv).astype(BF16)
        dg = (dact * uv * (s * (1.0 + gv * (1.0 - s)))).astype(BF16)
        du = (dact * silu).astype(BF16)
        dg_ref[...] = dg
        du_ref[...] = du
        acc_ref[...] += _dot_nt(dg, wg_ref[...]) + _dot_nt(du, wu_ref[...])

        @pl.when(j == nj - 1)
        def _():
            dx1_ref[...] = acc_ref[...]

    return _call_carrying(
        job, body, name=name, grid=(T // tm, nj),
        in_specs=[pl.BlockSpec((tm, D), lambda i, j: (i, 0)), pl.BlockSpec((tm, D), lambda i, j: (i, 0)), vec(D),
                  blk(), blk(),
                  pl.BlockSpec((None, None, D, tf), lambda i, j: (j, wl, 0, 0)),
                  pl.BlockSpec((None, None, D, tf), lambda i, j: (j + nj, wl, 0, 0)),
                  pl.BlockSpec((None, tf, D), lambda i, j: (wl, j, 0))],
        out_specs=[pl.BlockSpec((tm, D), lambda i, j: (i, 0)), blk(), blk(), blk(),
                   pl.BlockSpec((tm, D), lambda i, j: (i, 0)),
                   pl.BlockSpec((1, D), lambda i, j: (0, 0)), pl.BlockSpec((1, D), lambda i, j: (0, 0))],
        out_shape=[jax.ShapeDtypeStruct((T, D), BF16),
                   jax.ShapeDtypeStruct((nj, T, tf), BF16), jax.ShapeDtypeStruct((nj, T, tf), BF16),
                   jax.ShapeDtypeStruct((nj, T, tf), BF16),
                   jax.ShapeDtypeStruct((T, D), F32),
                   jax.ShapeDtypeStruct((1, D), F32), jax.ShapeDtypeStruct((1, D), F32)],
        scratch_shapes=[pltpu.VMEM((tm, D), F32), pltpu.VMEM((tm, D), BF16)],
        vmem_mib=56, args=(dx2, r2, lng, gs, us, wfi, wfi, wfo))


def _loss_head(y, target, *, name):
    T, D = y.shape
    tm = _tile(T, 1024)

    def body(y_ref, t_ref, dy_ref, sq_ref):
        @pl.when(pl.program_id(0) == 0)
        def _():
            sq_ref[...] = jnp.zeros_like(sq_ref)
        err = y_ref[...] - t_ref[...]
        dy_ref[...] = err * (1.0 / D)
        sq_ref[...] += jnp.sum(err * err, axis=0, keepdims=True)

    return pl.pallas_call(
        body, name=name, grid=(T // tm,),
        in_specs=[pl.BlockSpec((tm, D), lambda i: (i, 0)), pl.BlockSpec((tm, D), lambda i: (i, 0))],
        out_specs=[pl.BlockSpec((tm, D), lambda i: (i, 0)), pl.BlockSpec((1, D), lambda i: (0, 0))],
        out_shape=[jax.ShapeDtypeStruct((T, D), F32), jax.ShapeDtypeStruct((1, D), F32)],
        compiler_params=_cparams(("arbitrary",)),
    )(y, target)


def _my_place():
    return lax.axis_index("x"), lax.axis_index("y"), lax.axis_index("c")


def _peer(place, k):
    x, y, c = place
    return (1 - x if k & 4 else x, 1 - y if k & 2 else y, 1 - c if k & 1 else c)


def _logical(place):
    x, y, c = place
    return 4 * x + 2 * y + c


def _block_of(ref, mode, idx):
    if mode == "blk":
        return ref.at[idx]
    if mode == "col":
        size = ref.shape[2] // N_DEV
        return ref.at[:, :, pl.ds(pl.multiple_of(idx * size, size), size)]
    size = ref.shape[1] // N_DEV
    return ref.at[:, pl.ds(pl.multiple_of(idx * size, size), size), :]


def _full_shape(shard, mode):
    if mode == "blk":
        return (N_DEV,) + shard.shape
    if mode == "col":
        return shard.shape[:2] + (N_DEV * shard.shape[2],)
    return (shard.shape[0], N_DEV * shard.shape[1], shard.shape[2])


class _Exchange:
    def __init__(self, arrays, out_shape, build):
        self.arrays = list(arrays)
        self.out_shape = list(out_shape)
        self.build = build

    def scratch(self):
        n = len(self.arrays)
        return [pltpu.SemaphoreType.DMA((n * N_DEV,)), pltpu.SemaphoreType.DMA((n * N_DEV,)),
                pltpu.SemaphoreType.DMA((n,))]

    def start(self, ins, outs, sems):
        for cp in self.build(ins, outs, *sems):
            cp.start()

    def wait(self, ins, outs, sems):
        for cp in self.build(ins, outs, *sems):
            cp.wait()

    def run(self, name):
        n_in, n_out = len(self.arrays), len(self.out_shape)
        hbm = pl.BlockSpec(memory_space=pltpu.HBM)

        def body(*refs):
            ins, outs, sems = refs[:n_in], refs[n_in:n_in + n_out], refs[n_in + n_out:]
            self.start(ins, outs, sems)
            self.wait(ins, outs, sems)

        return pl.pallas_call(
            body, name=name, in_specs=[hbm] * n_in, out_specs=[hbm] * n_out,
            out_shape=self.out_shape, scratch_shapes=self.scratch(),
        )(*self.arrays)


def _copies_to_all(src_of, dst_of, n, send, recv, local):
    me = _my_place()
    copies = []
    for a in range(n):
        copies.append(pltpu.make_async_copy(src_of(a, _logical(me)), dst_of(a), local.at[a]))
        for k in range(1, N_DEV):
            peer = _peer(me, k)
            copies.append(pltpu.make_async_remote_copy(
                src_ref=src_of(a, _logical(peer)), dst_ref=dst_of(a),
                send_sem=send.at[a * N_DEV + k], recv_sem=recv.at[a * N_DEV + k],
                device_id=peer, device_id_type=MESH))
    return copies


def _gather_job(shards, modes):
    def build(ins, outs, send, recv, local):
        my_id = _logical(_my_place())
        return _copies_to_all(lambda a, dev: ins[a], lambda a: _block_of(outs[a], modes[a], my_id),
                              len(shards), send, recv, local)

    return _Exchange(shards, [jax.ShapeDtypeStruct(_full_shape(s, m), s.dtype) for s, m in zip(shards, modes)], build)


def _grad_block(ref, mode, idx):
    if mode == "blk":
        return ref.at[idx]
    if mode == "col":
        size = ref.shape[1] // N_DEV
        return ref.at[:, pl.ds(pl.multiple_of(idx * size, size), size)]
    size = ref.shape[0] // N_DEV
    return ref.at[pl.ds(pl.multiple_of(idx * size, size), size), :]


def _grad_shard_shape(g, mode):
    if mode == "blk":
        return g.shape[1:]
    if mode == "col":
        return (g.shape[0], g.shape[1] // N_DEV)
    return (g.shape[0] // N_DEV, g.shape[1])


def _grads_job(groups, modes):
    flat = [(g, w, l) for w, per_w in enumerate(groups) for l, g in enumerate(per_w)]

    def build(ins, outs, send, recv, local):
        my_id = _logical(_my_place())
        return _copies_to_all(lambda a, dev: _grad_block(ins[a], modes[flat[a][1]], dev),
                              lambda a: outs[flat[a][1]].at[my_id, flat[a][2]],
                              len(flat), send, recv, local)

    out_shape = [jax.ShapeDtypeStruct((N_DEV, len(per_w)) + _grad_shard_shape(per_w[0], m), per_w[0].dtype)
                 for per_w, m in zip(groups, modes)]
    return _Exchange([g for g, _, _ in flat], out_shape, build)


def _adamw(w, g, m, v):
    m = ADAM_B1 * m + (1.0 - ADAM_B1) * g
    v = ADAM_B2 * v + (1.0 - ADAM_B2) * (g * g)
    m_hat = m / (1.0 - ADAM_B1 ** ADAM_STEP)
    v_hat = v / (1.0 - ADAM_B2 ** ADAM_STEP)
    delta = -ADAM_LR * (m_hat / (jnp.sqrt(v_hat) + ADAM_EPS) + ADAM_WD * w)
    return delta, m, v


def _sum_slots_adamw(slots, w, m, v, *, name):
    R, C = w.shape
    tr = _tile(R, 256)

    def body(s_ref, w_ref, m_ref, v_ref, g_out, d_out, m_out, v_out):
        g = s_ref[0].astype(F32)
        for s in range(1, N_DEV):
            g = g + s_ref[s].astype(F32)
        delta, m_new, v_new = _adamw(w_ref[...], g, m_ref[...], v_ref[...])
        g_out[...] = g
        d_out[...] = delta
        m_out[...] = m_new
        v_out[...] = v_new

    spec = pl.BlockSpec((tr, C), lambda i: (i, 0))
    return pl.pallas_call(
        body, name=name, grid=(R // tr,),
        in_specs=[pl.BlockSpec((N_DEV, tr, C), lambda i: (0, i, 0)), spec, spec, spec],
        out_specs=[spec] * 4,
        out_shape=[jax.ShapeDtypeStruct((R, C), F32)] * 4,
        compiler_params=_cparams(("parallel",)),
    )(slots, w, m, v)


def _small_allreduce_adamw(g, w, m, v, *, name):
    R = g.shape[0]
    vmem = pl.BlockSpec(memory_space=pltpu.VMEM)

    def body(g_ref, w_ref, m_ref, v_ref, g_out, d_out, m_out, v_out, slots, send, recv):
        me = _my_place()
        my_id = _logical(me)
        slots[my_id] = g_ref[...]
        copies = []
        for k in range(1, N_DEV):
            cp = pltpu.make_async_remote_copy(
                src_ref=g_ref, dst_ref=slots.at[my_id], send_sem=send.at[k], recv_sem=recv.at[k],
                device_id=_peer(me, k), device_id_type=MESH)
            cp.start()
            copies.append(cp)
        for cp in copies:
            cp.wait()
        total = slots[0]
        for s in range(1, N_DEV):
            total = total + slots[s]
        delta, m_new, v_new = _adamw(w_ref[...], total, m_ref[...], v_ref[...])
        g_out[...] = total
        d_out[...] = delta
        m_out[...] = m_new
        v_out[...] = v_new

    return pl.pallas_call(
        body, name=name,
        in_specs=[vmem] * 4, out_specs=[vmem] * 4,
        out_shape=[jax.ShapeDtypeStruct((R, LANES), F32)] * 4,
        scratch_shapes=[pltpu.VMEM((N_DEV, R, LANES), F32),
                        pltpu.SemaphoreType.DMA((N_DEV,)), pltpu.SemaphoreType.DMA((N_DEV,))],
    )(g, w, m, v)


def _pack(parts):
    flat = jnp.concatenate([p.reshape(-1) for p in parts])
    rows = -(-flat.shape[0] // (8 * LANES)) * 8
    return jnp.pad(flat, (0, rows * LANES - flat.shape[0])).reshape(rows, LANES)


def _unpack(packed, like):
    flat = packed.reshape(-1)
    out, pos = [], 0
    for p in like:
        out.append(flat[pos:pos + p.size].reshape(p.shape))
        pos += p.size
    return out


def kernel(x, w_in, b_gate, rel_bias, w_proj_a, w_proj_b, w_out, ln1_g, ln1_b, w_ffn_in, w_ffn_out, ln2_g, ln2_b, loss_target, m_w_in, m_b_gate, m_rel_bias, m_w_proj_a, m_w_proj_b, m_w_out, m_ln1_g, m_ln1_b, m_w_ffn_in, m_w_ffn_out, m_ln2_g, m_ln2_b, v_w_in, v_b_gate, v_rel_bias, v_w_proj_a, v_w_proj_b, v_w_out, v_ln1_g, v_ln1_b, v_w_ffn_in, v_w_ffn_out, v_ln2_g, v_ln2_b):
    L = w_in.shape[0]
    T, D = x.shape[1], x.shape[2]
    alpha = float((2 * L) ** 0.25)
    n_qkv = 6 * WIDTH

    big = [w_in, w_proj_a, w_proj_b, w_out, w_ffn_in, w_ffn_out]
    kinds = ["in", "pa", "pb", "o", "fi", "fo"]
    modes = ["col", "col", "col", "row", "blk", "row"]
    mode_of = dict(zip(kinds, modes))
    w_bf = dict(zip(kinds, [w.astype(BF16) for w in big]))

    def gather_of(ks, l):
        return _gather_job([w_bf[k][l:l + 1] for k in ks], [mode_of[k] for k in ks])

    W = [dict() for _ in range(L)]
    (W[0]["in"],) = gather_of(["in"], 0).run("gather_w_in_first")
    vec3 = lambda a: a[:, None, :]
    bg3, l1g, l1b, l2g, l2b = vec3(b_gate), vec3(ln1_g), vec3(ln1_b), vec3(ln2_g), vec3(ln2_b)
    b_col0 = 3 * WIDTH // LANES

    h = x[0]
    saved = []
    for l in range(L):
        ahead = l + 1 < L
        qkv = _mm_nn(h, W[l]["in"], 0, col_off=0, n_cols=n_qkv, out_dtype=BF16, name=f"in_proj_qkv_{l}")
        gates = _mm_nn(h, W[l]["in"], 0, col_off=n_qkv, n_cols=2 * D, out_dtype=F32, name=f"in_proj_gates_{l}")
        kvpad = jnp.pad(qkv[:, WIDTH:3 * WIDTH], ((A_WIN - A_TQ, 0), (0, 0)))
        bias = _toeplitz_bias(rel_bias[l])
        rest = ["pa", "pb", "o", "fi", "fo"]
        oa, got = _attn_a_fwd(qkv, kvpad, bias, name=f"attn_a_fwd_{l}", job=gather_of(rest, 0) if l == 0 else None)
        W[l].update(zip(rest, got))
        early = ["in", "pa", "pb", "o"]
        ob, got = _attn_b_fwd(qkv, col0=b_col0, name=f"attn_b_fwd_{l}", job=gather_of(early, l + 1) if ahead else None)
        W[l + 1 if ahead else l].update(zip(early, got))
        x1, r1 = _proj_fwd(oa, ob, gates, bg3, W[l]["pa"], W[l]["pb"], W[l]["o"], 0, h, l1g, l1b, l,
                           alpha=alpha, name=f"proj_fwd_{l}")
        (gs, us, r2, x2), got = _ffn_fwd(x1, W[l]["fi"], W[l]["fo"], 0, l2g, l2b, l, alpha=alpha, name=f"ffn_fwd_{l}",
                                         job=gather_of(["fi", "fo"], l + 1) if ahead else None)
        W[l + 1 if ahead else l].update(zip(["fi", "fo"], got))
        saved.append((h, qkv, gates, kvpad, bias, oa, ob, x1, r1, gs, us, r2))
        h = x2

    d_h, sq = _loss_head(h, loss_target[0], name="loss_head")
    loss = lax.psum((0.5 / D) * jnp.sum(sq), ("x", "y", "c"))

    g_bg, g_rb, g_l1g, g_l1b, g_l2g, g_l2b = ([None] * L for _ in range(6))
    slot = {k: [None] * L for k in kinds}

    def exchange_of(ks, grads):
        return _grads_job([[g] for g in grads], [mode_of[k] for k in ks])

    w_in_above = None
    for l in reversed(range(L)):
        xin, qkv, gates, kvpad, bias, oa, ob, x1, r1, gs, us, r2 = saved[l]
        (dr2, act, dgt, dup, dx1, g_l2g[l], g_l2b[l]), got = _ffn_bwd(
            d_h, r2, l2g, gs, us, W[l]["fi"], W[l]["fo"], 0, l, alpha=alpha, name=f"ffn_bwd_{l}", job=w_in_above)
        if w_in_above is not None:
            (slot["in"][l + 1],) = got
        g_fo = _mm_tn_blocked_a(act, dr2, name=f"grad_w_ffn_out_{l}").reshape(-1, D)
        g_fi = jnp.concatenate([_mm_tn_blocked(x1, dgt, name=f"grad_w_ffn_gate_{l}"),
                                _mm_tn_blocked(x1, dup, name=f"grad_w_ffn_up_{l}")], axis=0)
        (dr1, mixin, dya, dyb, dgates, doa, dob, g_l1g[l], g_l1b[l], g_bg[l]) = _proj_bwd(
            dx1, r1, l1g, oa, ob, gates, bg3, W[l]["pa"], W[l]["pb"], W[l]["o"], 0, l, name=f"proj_bwd_{l}")
        g_o = _mm_tn(mixin, dr1, tm=_tile(D, 1024), tn=_tile(D, 1024), name=f"grad_w_out_{l}")
        g_pa = _mm_tn(oa, dya, tm=WIDTH, tn=_tile(D, 1024), name=f"grad_w_proj_a_{l}")
        g_pb = _mm_tn(ob, dyb, tm=WIDTH, tn=_tile(D, 1024), name=f"grad_w_proj_b_{l}")
        (dqa, dka, dva, dbias), (slot["fi"][l], slot["fo"][l]) = _attn_a_bwd(
            qkv, kvpad, bias, doa, name=f"attn_a_bwd_{l}", job=exchange_of(["fi", "fo"], [g_fi, g_fo]))
        g_rb[l] = _toeplitz_bias_grad(dbias)
        (dqb, dkb, dvb), (slot["pa"][l], slot["pb"][l], slot["o"][l]) = _attn_b_bwd(
            qkv, ob, dob, col0=b_col0, name=f"attn_b_bwd_{l}", job=exchange_of(["pa", "pb", "o"], [g_pa, g_pb, g_o]))
        pad = A_WIN - A_TQ
        d_pre = jnp.concatenate([dqa, dka[pad:].astype(BF16), dva[pad:].astype(BF16),
                                 dqb, dkb.astype(BF16), dvb.astype(BF16), dgates], axis=1)
        g_in = _mm_tn(xin, d_pre, tm=D, tn=w_in.shape[2], name=f"grad_w_in_{l}")
        w_in_above = exchange_of(["in"], [g_in])
        d_h, got = _mm_nt_add(d_pre, W[l]["in"], 0, dr1, alpha, name=f"grad_x_{l}", job=w_in_above if l == 0 else None)
        if l == 0:
            (slot["in"][0],) = got
    grad_x = d_h[None]

    slots = [jnp.concatenate(slot[k], axis=1) if L > 1 else slot[k][0] for k in kinds]
    moments_m = [m_w_in, m_w_proj_a, m_w_proj_b, m_w_out, m_w_ffn_in, m_w_ffn_out]
    moments_v = [v_w_in, v_w_proj_a, v_w_proj_b, v_w_out, v_w_ffn_in, v_w_ffn_out]
    names = ["w_in", "w_proj_a", "w_proj_b", "w_out", "w_ffn_in", "w_ffn_out"]
    big_out = {}
    for nm, s, w, m, v in zip(names, slots, big, moments_m, moments_v):
        two = lambda a: a.reshape(-1, a.shape[-1])
        res = _sum_slots_adamw(s.reshape(N_DEV, -1, s.shape[-1]), two(w), two(m), two(v), name=f"adamw_{nm}")
        big_out[nm] = [r.reshape(w.shape) for r in res]

    small_w = [b_gate, rel_bias, ln1_g, ln1_b, ln2_g, ln2_b]
    small_g = [jnp.stack(g) for g in (g_bg, g_rb, g_l1g, g_l1b, g_l2g, g_l2b)]
    small_m = [m_b_gate, m_rel_bias, m_ln1_g, m_ln1_b, m_ln2_g, m_ln2_b]
    small_v = [v_b_gate, v_rel_bias, v_ln1_g, v_ln1_b, v_ln2_g, v_ln2_b]
    res = _small_allreduce_adamw(_pack(small_g), _pack(small_w), _pack(small_m), _pack(small_v),
                                 name="allreduce_small_adamw")
    small_names = ["b_gate", "rel_bias", "ln1_g", "ln1_b", "ln2_g", "ln2_b"]
    small_out = {nm: [] for nm in small_names}
    for packed in res:
        for nm, arr in zip(small_names, _unpack(packed, small_w)):
            small_out[nm].append(arr)

    order = ["w_in", "b_gate", "rel_bias", "w_proj_a", "w_proj_b", "w_out", "ln1_g", "ln1_b",
             "w_ffn_in", "w_ffn_out", "ln2_g", "ln2_b"]
    every = {**big_out, **small_out}
    outs = [loss, grad_x]
    for kind in range(4):
        outs += [every[nm][kind] for nm in order]
    return tuple(outs)
```

```python
import functools
import math

import jax
import jax.numpy as jnp
from jax import lax
from jax.experimental import pallas as pl
from jax.experimental.pallas import tpu as pltpu

F32 = jnp.float32
BF16 = jnp.bfloat16

HEAD_DIM = 64
CHUNK = 64
LEFT_CHUNKS = 8
REL_CLIP = 256
N_REL = 2 * REL_CLIP + 1
WIDTH = 512
LANES = 128
A_TQ = 256
A_WIN = A_TQ + LEFT_CHUNKS * CHUNK
A_STRIP = 128
B_TQ = 512
B_TS = 256
B_PIECE = 64
B_DEAD = -160.0
LN_EPS = 1e-5
QK_SCALE = 1.0 / math.sqrt(HEAD_DIM)
LOG2E = 1.4426950408889634
NEG = -1e30

ADAM_LR = 0.001
ADAM_B1 = 0.9
ADAM_B2 = 0.999
ADAM_EPS = 1e-08
ADAM_WD = 0.01
ADAM_STEP = 10

N_DEV = 8
MESH = pl.DeviceIdType.MESH
MIB = 1024 * 1024


def _cparams(sem=None, vmem_mib=48):
    return pltpu.CompilerParams(dimension_semantics=sem, vmem_limit_bytes=vmem_mib * MIB)


def _dot(a, b):
    return jnp.dot(a, b, preferred_element_type=F32)


def _dot_nt(a, b):
    return lax.dot_general(a, b, (((1,), (1,)), ((), ())), preferred_element_type=F32)


def _dot_tn(a, b):
    return lax.dot_general(a, b, (((0,), (0,)), ((), ())), preferred_element_type=F32)


def _tile(n, pref):
    if n <= pref:
        return n
    for t in range(pref - pref % 8, 0, -8):
        if n % t == 0:
            return t
    raise ValueError((n, pref))


def _mm_nn(a, w, layer, *, col_off, n_cols, out_dtype, name):
    M, K = a.shape
    tm = _tile(M, 1024)
    tn = _tile(n_cols, 512)
    assert col_off % tn == 0
    off = col_off // tn

    def body(a_ref, w_ref, o_ref):
        o_ref[...] = _dot(a_ref[...].astype(BF16), w_ref[...]).astype(out_dtype)

    return pl.pallas_call(
        body, name=name, grid=(M // tm, n_cols // tn),
        in_specs=[pl.BlockSpec((tm, K), lambda i, j: (i, 0)),
                  pl.BlockSpec((None, K, tn), lambda i, j: (layer, 0, j + off))],
        out_specs=pl.BlockSpec((tm, tn), lambda i, j: (i, j)),
        out_shape=jax.ShapeDtypeStruct((M, n_cols), out_dtype),
        compiler_params=_cparams(("parallel", "parallel")),
    )(a, w)


def _mm_nt_add(a, w, layer, add, add_scale, *, name, job=None):
    M, K = a.shape
    N = w.shape[1]
    tm = _tile(M, 1024)
    tk = _tile(K, 1024)

    def body(a_ref, w_ref, add_ref, o_ref):
        @pl.when(pl.program_id(1) == 0)
        def _():
            o_ref[...] = add_scale * add_ref[...]
        o_ref[...] += _dot_nt(a_ref[...], w_ref[...])

    (out,), rode = _call_carrying(
        job, body, name=name, grid=(M // tm, K // tk),
        in_specs=[pl.BlockSpec((tm, tk), lambda i, k: (i, k)),
                  pl.BlockSpec((None, N, tk), lambda i, k: (layer, 0, k)),
                  pl.BlockSpec((tm, N), lambda i, k: (i, 0))],
        out_specs=[pl.BlockSpec((tm, N), lambda i, k: (i, 0))],
        out_shape=[jax.ShapeDtypeStruct((M, N), F32)],
        scratch_shapes=[], vmem_mib=48, args=(a, w, add))
    return out, rode


def _tn_body(k_axis, n_k):
    def body(a_ref, b_ref, o_ref, acc_ref):
        k = pl.program_id(k_axis)

        @pl.when(k == 0)
        def _():
            acc_ref[...] = jnp.zeros_like(acc_ref)
        acc_ref[...] += _dot_tn(a_ref[...].astype(BF16), b_ref[...].astype(BF16))

        @pl.when(k == n_k - 1)
        def _():
            o_ref[...] = acc_ref[...].astype(o_ref.dtype)
    return body


def _mm_tn(a, b, *, tm, tn, name):
    T, M = a.shape
    N = b.shape[1]
    tk = _tile(T, 512)
    return pl.pallas_call(
        _tn_body(2, T // tk), name=name, grid=(M // tm, N // tn, T // tk),
        in_specs=[pl.BlockSpec((tk, tm), lambda i, j, k: (k, i)),
                  pl.BlockSpec((tk, tn), lambda i, j, k: (k, j))],
        out_specs=pl.BlockSpec((tm, tn), lambda i, j, k: (i, j)),
        out_shape=jax.ShapeDtypeStruct((M, N), BF16),
        scratch_shapes=[pltpu.VMEM((tm, tn), F32)],
        compiler_params=_cparams(("parallel", "parallel", "arbitrary")),
    )(a, b)


def _mm_tn_blocked(a, b, *, name):
    T, M = a.shape
    S, _, N = b.shape
    tk = _tile(T, 512)
    return pl.pallas_call(
        _tn_body(1, T // tk), name=name, grid=(S, T // tk),
        in_specs=[pl.BlockSpec((tk, M), lambda s, k: (k, 0)),
                  pl.BlockSpec((None, tk, N), lambda s, k: (s, k, 0))],
        out_specs=pl.BlockSpec((None, M, N), lambda s, k: (s, 0, 0)),
        out_shape=jax.ShapeDtypeStruct((S, M, N), BF16),
        scratch_shapes=[pltpu.VMEM((M, N), F32)],
        compiler_params=_cparams(("parallel", "arbitrary")),
    )(a, b)


def _mm_tn_blocked_a(a, b, *, name):
    S, T, M = a.shape
    N = b.shape[1]
    tk = _tile(T, 512)
    return pl.pallas_call(
        _tn_body(1, T // tk), name=name, grid=(S, T // tk),
        in_specs=[pl.BlockSpec((None, tk, M), lambda s, k: (s, k, 0)),
                  pl.BlockSpec((tk, N), lambda s, k: (k, 0))],
        out_specs=pl.BlockSpec((None, M, N), lambda s, k: (s, 0, 0)),
        out_shape=jax.ShapeDtypeStruct((S, M, N), BF16),
        scratch_shapes=[pltpu.VMEM((M, N), F32)],
        compiler_params=_cparams(("parallel", "arbitrary")),
    )(a, b)


def _ln_fwd(r, g, b):
    mu = jnp.mean(r, axis=-1, keepdims=True)
    xc = r - mu
    var = jnp.mean(xc * xc, axis=-1, keepdims=True)
    return xc * lax.rsqrt(var + LN_EPS) * g + b


def _ln_bwd(dy, r, g):
    mu = jnp.mean(r, axis=-1, keepdims=True)
    xc = r - mu
    var = jnp.mean(xc * xc, axis=-1, keepdims=True)
    rstd = lax.rsqrt(var + LN_EPS)
    xhat = xc * rstd
    dxh = dy * g
    m1 = jnp.mean(dxh, axis=-1, keepdims=True)
    m2 = jnp.mean(dxh * xhat, axis=-1, keepdims=True)
    return rstd * (dxh - m1 - xhat * m2), xhat


def _lane_is_head0():
    return lax.broadcasted_iota(jnp.int32, (1, LANES), 1) < HEAD_DIM


def _band_valid(i, r0):
    r = r0 + lax.broadcasted_iota(jnp.int32, (A_STRIP, A_WIN), 0)
    c = lax.broadcasted_iota(jnp.int32, (A_STRIP, A_WIN), 1)
    a = r // CHUNK
    b = c // CHUNK
    return (b >= a) & (b <= a + LEFT_CHUNKS) & (c + i * A_TQ >= LEFT_CHUNKS * CHUNK)


def _band_streams():
    return [dict(h=h, r0=r0, rows=pl.ds(r0, A_STRIP)) for h in range(2) for r0 in range(0, A_TQ, A_STRIP)]


def _band_scores(st, i, qh_ref, k2, bias_ref):
    s = _dot_nt(qh_ref[st["h"], st["rows"], :], k2) * QK_SCALE + bias_ref[st["h"], st["rows"], :]
    st["s"] = jnp.where(_band_valid(i, st["r0"]), s, NEG)


def _band_softmax(st):
    s = st.pop("s")
    e = jnp.exp(s - jnp.max(s, axis=1, keepdims=True))
    st["p"] = e / jnp.sum(e, axis=1, keepdims=True)


def _attn_a_fwd(qkv, kvpad, bias, *, name, job=None):
    T = qkv.shape[0]
    n_hp = WIDTH // LANES

    def body(q_ref, k_ref, v_ref, bias_ref, o_ref, qh_ref, acc_ref):
        i = pl.program_id(1)
        row0 = pl.multiple_of(i * A_TQ, A_TQ)
        q2 = q_ref[...]
        k2 = k_ref[pl.ds(row0, A_WIN), :]
        v2 = v_ref[pl.ds(row0, A_WIN), :]
        head0 = _lane_is_head0()
        qh_ref[0] = jnp.where(head0, q2, jnp.zeros_like(q2))
        qh_ref[1] = jnp.where(head0, jnp.zeros_like(q2), q2)

        def scores(st):
            _band_scores(st, i, qh_ref, k2, bias_ref)

        def values(st):
            acc_ref[st["h"], st["rows"], :] = _dot(st.pop("p").astype(BF16), v2)

        _skewed(_band_streams(), [scores, _band_softmax, values])
        o_ref[...] = jnp.where(head0, acc_ref[0], acc_ref[1]).astype(o_ref.dtype)

    (out,), rode = _call_carrying(
        job, body, name=name, grid=(n_hp, T // A_TQ),
        in_specs=[pl.BlockSpec((A_TQ, LANES), lambda hp, i: (i, hp)),
                  pl.BlockSpec((T + A_WIN - A_TQ, LANES), lambda hp, i: (0, hp)),
                  pl.BlockSpec((T + A_WIN - A_TQ, LANES), lambda hp, i: (0, hp + n_hp)),
                  pl.BlockSpec((2, A_TQ, A_WIN), lambda hp, i: (hp, 0, 0))],
        out_specs=[pl.BlockSpec((A_TQ, LANES), lambda hp, i: (i, hp))],
        out_shape=[jax.ShapeDtypeStruct((T, WIDTH), BF16)],
        scratch_shapes=[pltpu.VMEM((2, A_TQ, LANES), BF16), pltpu.VMEM((2, A_TQ, LANES), F32)],
        vmem_mib=48, args=(qkv, kvpad, kvpad, bias))
    return out, rode


def _attn_a_bwd(qkv, kvpad, bias, do, *, name, job=None):
    T = qkv.shape[0]
    TP = T + A_WIN - A_TQ
    n_hp = WIDTH // LANES

    def body(q_ref, k_ref, v_ref, bias_ref, do_ref, dq_ref, dk_ref, dv_ref, db_ref, qh_ref, doh_ref, dqa_ref):
        i = pl.program_id(1)

        @pl.when(i == 0)
        def _():
            dk_ref[...] = jnp.zeros_like(dk_ref)
            dv_ref[...] = jnp.zeros_like(dv_ref)
            db_ref[...] = jnp.zeros_like(db_ref)

        row0 = pl.multiple_of(i * A_TQ, A_TQ)
        window = pl.ds(row0, A_WIN)
        q2 = q_ref[...]
        do2 = do_ref[...]
        k2 = k_ref[window, :]
        v2 = v_ref[window, :]
        head0 = _lane_is_head0()
        zero = jnp.zeros_like(q2)
        qh_ref[0] = jnp.where(head0, q2, zero)
        qh_ref[1] = jnp.where(head0, zero, q2)
        doh_ref[0] = jnp.where(head0, do2, zero)
        doh_ref[1] = jnp.where(head0, zero, do2)
        dk = [[], []]
        dv = [[], []]

        def scores(st):
            _band_scores(st, i, qh_ref, k2, bias_ref)
            st["dp"] = _dot_nt(doh_ref[st["h"], st["rows"], :], v2)

        def dscores(st):
            _band_softmax(st)
            p, dp = st.pop("p"), st.pop("dp")
            ds = p * (dp - jnp.sum(p * dp, axis=1, keepdims=True))
            db_ref[st["h"], st["rows"], :] += ds
            st["dsb"] = (ds * QK_SCALE).astype(BF16)
            st["pb"] = p.astype(BF16)

        def grads(st):
            h, rows = st["h"], st["rows"]
            dsb = st.pop("dsb")
            dqa_ref[h, rows, :] = _dot(dsb, k2)
            dk[h].append(_dot_tn(dsb, q_ref[rows, :]))
            dv[h].append(_dot_tn(st.pop("pb"), do_ref[rows, :]))

        _skewed(_band_streams(), [scores, dscores, grads])
        dq_ref[...] = jnp.where(head0, dqa_ref[0], dqa_ref[1]).astype(dq_ref.dtype)
        dk_ref[window, :] += jnp.where(head0, sum(dk[0]), sum(dk[1]))
        dv_ref[window, :] += jnp.where(head0, sum(dv[0]), sum(dv[1]))

    return _call_carrying(
        job, body, name=name, grid=(n_hp, T // A_TQ),
        in_specs=[pl.BlockSpec((A_TQ, LANES), lambda hp, i: (i, hp)),
                  pl.BlockSpec((TP, LANES), lambda hp, i: (0, hp)),
                  pl.BlockSpec((TP, LANES), lambda hp, i: (0, hp + n_hp)),
                  pl.BlockSpec((2, A_TQ, A_WIN), lambda hp, i: (hp, 0, 0)),
                  pl.BlockSpec((A_TQ, LANES), lambda hp, i: (i, hp))],
        out_specs=[pl.BlockSpec((A_TQ, LANES), lambda hp, i: (i, hp)),
                   pl.BlockSpec((TP, LANES), lambda hp, i: (0, hp)),
                   pl.BlockSpec((TP, LANES), lambda hp, i: (0, hp)),
                   pl.BlockSpec((2, A_TQ, A_WIN), lambda hp, i: (hp, 0, 0))],
        out_shape=[jax.ShapeDtypeStruct((T, WIDTH), BF16),
                   jax.ShapeDtypeStruct((TP, WIDTH), F32),
                   jax.ShapeDtypeStruct((TP, WIDTH), F32),
                   jax.ShapeDtypeStruct((WIDTH // HEAD_DIM, A_TQ, A_WIN), F32)],
        scratch_shapes=[pltpu.VMEM((2, A_TQ, LANES), BF16), pltpu.VMEM((2, A_TQ, LANES), BF16),
                        pltpu.VMEM((2, A_TQ, LANES), F32)],
        vmem_mib=56, args=(qkv, kvpad, kvpad, bias, do))


def _toeplitz_bias(rb):
    H = rb.shape[0]
    span = A_TQ + A_WIN - 1
    n_tail = span - (N_REL - 1)
    ext = jnp.concatenate([rb[:, 1:], jnp.broadcast_to(rb[:, N_REL - 1:], (H, n_tail))], axis=1)
    rev = jnp.pad(ext[:, ::-1], ((0, 0), (0, 1)))
    flat = jnp.broadcast_to(rev[:, None, :], (H, A_TQ, span + 1)).reshape(H, A_TQ * (span + 1))
    skew = flat[:, :A_TQ * span].reshape(H, A_TQ, span)
    return skew[:, :, A_TQ - 1:A_TQ - 1 + A_WIN]


def _toeplitz_bias_grad(db):
    H = db.shape[0]
    span = A_TQ + A_WIN - 1
    d_skew = jnp.pad(db, ((0, 0), (0, 0), (A_TQ - 1, span - (A_TQ - 1) - A_WIN)))
    d_flat = jnp.pad(d_skew.reshape(H, A_TQ * span), ((0, 0), (0, A_TQ)))
    g_ext = jnp.sum(d_flat.reshape(H, A_TQ, span + 1), axis=1)[:, :span][:, ::-1]
    last = g_ext[:, N_REL - 2] + jnp.sum(g_ext[:, N_REL - 1:], axis=1)
    return jnp.concatenate([jnp.zeros((H, 1), F32), g_ext[:, :N_REL - 2], last[:, None]], axis=1)


def _split_bf16(x):
    hi = x.astype(BF16)
    lo = (x - hi.astype(F32)).astype(BF16)
    return hi, lo


def _sb_streams(d):
    out = []
    for h in range(2):
        for r in range(B_TQ // B_TS):
            if d is not None and d > r:
                continue
            out.append(dict(h=h, r=r, rows=pl.ds(r * B_TS, B_TS), diag=(d is not None and d == r)))
    return out


def _piece_rows(st, p):
    return pl.ds(st["r"] * B_TS + p, B_PIECE)


def _rows_cat(parts):
    return jnp.concatenate(parts, axis=0)


def _skewed(streams, stages):
    for t in range(len(streams) + len(stages) - 1):
        for s, st in enumerate(streams):
            if 0 <= t - s < len(stages):
                stages[t - s](st)


def _sb_logs(st, z2):
    log_beta, log_keep, keep_bf = [], [], []
    for p in range(0, B_TS, B_PIECE):
        z = z2[p:p + B_PIECE]
        lp2 = jnp.log(1.0 + jnp.exp2(-jnp.abs(z))) * LOG2E
        lb = jnp.minimum(z, 0.0) - lp2
        lk = lb - z
        if st["diag"]:
            lk = jnp.where(_strict_lower(p), lk, 0.0)
        log_beta.append(lb)
        log_keep.append(lk)
        keep_bf.append(lk.astype(BF16))
    st["log_beta"] = _rows_cat(log_beta)
    st["log_keep"] = _rows_cat(log_keep)
    st["keep_bf"] = _rows_cat(keep_bf)


def _strict_lower(p):
    t = p + lax.broadcasted_iota(jnp.int32, (B_PIECE, B_TS), 0)
    s = lax.broadcasted_iota(jnp.int32, (B_PIECE, B_TS), 1)
    return s < t


def _tri(strict):
    j = lax.broadcasted_iota(jnp.int32, (B_TS, B_TS), 0)
    s = lax.broadcasted_iota(jnp.int32, (B_TS, B_TS), 1)
    return jnp.where(j > s if strict else j >= s, 1.0, 0.0).astype(BF16)


def _call_carrying(job, body, *, name, grid, in_specs, out_specs, out_shape, scratch_shapes, vmem_mib, args):
    n_in, n_out, n_scr = len(in_specs), len(out_specs), len(scratch_shapes)
    if job is None:
        res = pl.pallas_call(body, name=name, grid=grid, in_specs=in_specs, out_specs=out_specs,
                             out_shape=out_shape, scratch_shapes=scratch_shapes,
                             compiler_params=_cparams(("arbitrary",) * len(grid), vmem_mib))(*args)
        return res, []
    j_in, j_out = len(job.arrays), len(job.out_shape)
    hbm = pl.BlockSpec(memory_space=pltpu.HBM)

    def carrying(*refs):
        refs = list(refs)
        ins, refs = refs[:n_in], refs[n_in:]
        j_ins, refs = refs[:j_in], refs[j_in:]
        outs, refs = refs[:n_out], refs[n_out:]
        j_outs, refs = refs[:j_out], refs[j_out:]
        scr, sems = refs[:n_scr], refs[n_scr:]
        first = functools.reduce(jnp.logical_and, [pl.program_id(d) == 0 for d in range(len(grid))])
        last = functools.reduce(jnp.logical_and, [pl.program_id(d) == grid[d] - 1 for d in range(len(grid))])

        @pl.when(first)
        def _():
            job.start(j_ins, j_outs, sems)

        body(*ins, *outs, *scr)

        @pl.when(last)
        def _():
            job.wait(j_ins, j_outs, sems)

    res = pl.pallas_call(
        carrying, name=name, grid=grid,
        in_specs=list(in_specs) + [hbm] * j_in, out_specs=list(out_specs) + [hbm] * j_out,
        out_shape=list(out_shape) + job.out_shape, scratch_shapes=list(scratch_shapes) + job.scratch(),
        compiler_params=_cparams(("arbitrary",) * len(grid), vmem_mib))(*args, *job.arrays)
    return res[:n_out], res[n_out:]


def _attn_b_fwd(qkv, *, col0, name, job=None):
    T = qkv.shape[0]
    n_hp = WIDTH // LANES
    sub = B_TQ // B_TS

    def body(q_ref, k_ref, v_ref, o_ref, acc_ref, car_ref, qh_ref):
        i = pl.program_id(1)
        q2 = q_ref[...]
        head0 = _lane_is_head0()
        qh_ref[0] = jnp.where(head0, q2, jnp.zeros_like(q2))
        qh_ref[1] = jnp.where(head0, jnp.zeros_like(q2), q2)
        tri_s = _tri(True)
        acc_ref[...] = jnp.zeros_like(acc_ref)
        car_ref[...] = jnp.zeros_like(car_ref)

        def tile(kb, d):
            k0 = pl.multiple_of(kb * B_TS, B_TS)
            k2 = k_ref[pl.ds(k0, B_TS), :]
            v2 = v_ref[pl.ds(k0, B_TS), :]

            def scores(st):
                st["z2"] = _dot_nt(qh_ref[st["h"], st["rows"], :], k2) * (QK_SCALE * LOG2E)

            def logs(st):
                _sb_logs(st, st.pop("z2"))

            def suffix(st):
                st["suffix"] = _dot(st.pop("keep_bf"), tri_s)

            def weights(st):
                log_beta, suffix, log_keep = st.pop("log_beta"), st.pop("suffix"), st.pop("log_keep")
                wb = []
                for p in range(0, B_TS, B_PIECE):
                    rows = _piece_rows(st, p)
                    car = car_ref[st["h"], rows, :]
                    w = jnp.exp2(log_beta[p:p + B_PIECE] + suffix[p:p + B_PIECE] + car)
                    if st["diag"]:
                        w = jnp.where(_strict_lower(p), w, 0.0)
                    wb.append(w.astype(BF16))
                    car_ref[st["h"], rows, :] = car + jnp.sum(log_keep[p:p + B_PIECE], axis=1, keepdims=True)
                st["wb"] = _rows_cat(wb)

            def values(st):
                acc_ref[st["h"], st["rows"], :] += _dot(st.pop("wb"), v2)

            _skewed(_sb_streams(d), [scores, logs, suffix, weights, values])

        for d in reversed(range(sub)):
            tile(i * sub + d, d)

        def alive(c):
            return (c[0] < i * sub) & (c[1] > B_DEAD)

        def step(c):
            tile(i * sub - 1 - c[0], None)
            return c[0] + 1, jnp.max(car_ref[...])

        lax.while_loop(alive, step, (jnp.int32(0), jnp.float32(0.0)))
        o_ref[...] = jnp.where(head0, acc_ref[0], acc_ref[1])

    (out,), rode = _call_carrying(
        job, body, name=name, grid=(n_hp, T // B_TQ),
        in_specs=[pl.BlockSpec((B_TQ, LANES), lambda hp, i: (i, hp + col0)),
                  pl.BlockSpec((T, LANES), lambda hp, i: (0, hp + col0 + n_hp)),
                  pl.BlockSpec((T, LANES), lambda hp, i: (0, hp + col0 + 2 * n_hp))],
        out_specs=[pl.BlockSpec((B_TQ, LANES), lambda hp, i: (i, hp))],
        out_shape=[jax.ShapeDtypeStruct((T, WIDTH), F32)],
        scratch_shapes=[pltpu.VMEM((2, B_TQ, LANES), F32), pltpu.VMEM((2, B_TQ, 1), F32),
                        pltpu.VMEM((2, B_TQ, LANES), BF16)],
        vmem_mib=48, args=(qkv, qkv, qkv))
    return out, rode


def _attn_b_bwd(qkv, out, do, *, col0, name, job=None):
    T = qkv.shape[0]
    n_hp = WIDTH // LANES
    sub = B_TQ // B_TS

    def body(q_ref, k_ref, v_ref, o_ref, do_ref, dq_ref, dk_ref, dv_ref,
             dqa_ref, car_ref, carr_ref, tot_ref, qh_ref, doh_ref, qs_ref):
        i = pl.program_id(1)

        @pl.when(i == 0)
        def _():
            dk_ref[...] = jnp.zeros_like(dk_ref)
            dv_ref[...] = jnp.zeros_like(dv_ref)

        q2 = q_ref[...]
        do2 = do_ref[...]
        head0 = _lane_is_head0()
        zero = jnp.zeros_like(q2)
        qh_ref[0] = jnp.where(head0, q2, zero)
        qh_ref[1] = jnp.where(head0, zero, q2)
        doh_ref[0] = jnp.where(head0, do2, zero)
        doh_ref[1] = jnp.where(head0, zero, do2)
        scale = jnp.asarray(QK_SCALE, BF16)
        qs_ref[...] = q2 * scale
        tri_s = _tri(True)
        tri_i = _tri(False)
        prod = do2.astype(F32) * o_ref[...]
        tot_ref[0] = jnp.sum(jnp.where(head0, prod, 0.0), axis=1, keepdims=True)
        tot_ref[1] = jnp.sum(jnp.where(head0, 0.0, prod), axis=1, keepdims=True)
        dqa_ref[...] = jnp.zeros_like(dqa_ref)
        car_ref[...] = jnp.zeros_like(car_ref)
        carr_ref[...] = jnp.zeros_like(carr_ref)

        def tile(kb, d):
            k0 = pl.multiple_of(kb * B_TS, B_TS)
            keys = pl.ds(k0, B_TS)
            k2 = k_ref[keys, :]
            v2 = v_ref[keys, :]
            k2s = k2 * scale

            def scores(st):
                st["z2"] = _dot_nt(qh_ref[st["h"], st["rows"], :], k2) * (QK_SCALE * LOG2E)
                st["dw"] = _dot_nt(doh_ref[st["h"], st["rows"], :], v2)

            def logs(st):
                _sb_logs(st, st.pop("z2"))

            def suffix(st):
                st["suffix"] = _dot(st.pop("keep_bf"), tri_s)

            def weights(st):
                h = st["h"]
                suffix, dw = st.pop("suffix"), st.pop("dw")
                wb, dlog, hi, lo = [], [], [], []
                for p in range(0, B_TS, B_PIECE):
                    rows = _piece_rows(st, p)
                    car = car_ref[h, rows, :]
                    w = jnp.exp2(st["log_beta"][p:p + B_PIECE] + suffix[p:p + B_PIECE] + car)
                    if st["diag"]:
                        w = jnp.where(_strict_lower(p), w, 0.0)
                    w = w.astype(BF16)
                    dl = w.astype(F32) * dw[p:p + B_PIECE]
                    dl_hi, dl_lo = _split_bf16(dl)
                    wb.append(w)
                    dlog.append(dl)
                    hi.append(dl_hi)
                    lo.append(dl_lo)
                    car_ref[h, rows, :] = car + jnp.sum(st["log_keep"][p:p + B_PIECE], axis=1, keepdims=True)
                st["wb"], st["dlog"], st["hi"], st["lo"] = _rows_cat(wb), _rows_cat(dlog), _rows_cat(hi), _rows_cat(lo)

            def later(st):
                st["later"] = _dot(st.pop("hi"), tri_i) + _dot(st.pop("lo"), tri_i)

            def dscores(st):
                h = st["h"]
                later, dlog = st.pop("later"), st.pop("dlog")
                log_keep, log_beta = st.pop("log_keep"), st.pop("log_beta")
                dzb = []
                for p in range(0, B_TS, B_PIECE):
                    rows = _piece_rows(st, p)
                    pc = slice(p, p + B_PIECE)
                    carr = carr_ref[h, rows, :]
                    earlier = tot_ref[h, rows, :] - (later[pc] + carr)
                    dz = dlog[pc] * jnp.exp2(log_keep[pc]) - jnp.exp2(log_beta[pc]) * earlier
                    if st["diag"]:
                        dz = jnp.where(_strict_lower(p), dz, 0.0)
                    dzb.append(dz.astype(BF16))
                    carr_ref[h, rows, :] = carr + jnp.sum(dlog[pc], axis=1, keepdims=True)
                st["dzb"] = _rows_cat(dzb)

            def grads(st):
                h, rows = st["h"], st["rows"]
                mine = head0 if h == 0 else jnp.logical_not(head0)
                dzb = st.pop("dzb")
                dqa_ref[h, rows, :] += _dot(dzb, k2s)
                dk_ref[keys, :] += jnp.where(mine, _dot_tn(dzb, qs_ref[rows, :]), 0.0)
                dv_ref[keys, :] += jnp.where(mine, _dot_tn(st.pop("wb"), do_ref[rows, :]), 0.0)

            _skewed(_sb_streams(d), [scores, logs, suffix, weights, later, dscores, grads])

        for d in reversed(range(sub)):
            tile(i * sub + d, d)

        def alive(c):
            return (c[0] < i * sub) & (c[1] > B_DEAD)

        def step(c):
            tile(i * sub - 1 - c[0], None)
            return c[0] + 1, jnp.max(car_ref[...])

        lax.while_loop(alive, step, (jnp.int32(0), jnp.float32(0.0)))
        dq_ref[...] = jnp.where(head0, dqa_ref[0], dqa_ref[1]).astype(dq_ref.dtype)

    return _call_carrying(
        job, body, name=name, grid=(n_hp, T // B_TQ),
        in_specs=[pl.BlockSpec((B_TQ, LANES), lambda hp, i: (i, hp + col0)),
                  pl.BlockSpec((T, LANES), lambda hp, i: (0, hp + col0 + n_hp)),
                  pl.BlockSpec((T, LANES), lambda hp, i: (0, hp + col0 + 2 * n_hp)),
                  pl.BlockSpec((B_TQ, LANES), lambda hp, i: (i, hp)),
                  pl.BlockSpec((B_TQ, LANES), lambda hp, i: (i, hp))],
        out_specs=[pl.BlockSpec((B_TQ, LANES), lambda hp, i: (i, hp)),
                   pl.BlockSpec((T, LANES), lambda hp, i: (0, hp)),
                   pl.BlockSpec((T, LANES), lambda hp, i: (0, hp))],
        out_shape=[jax.ShapeDtypeStruct((T, WIDTH), BF16),
                   jax.ShapeDtypeStruct((T, WIDTH), F32),
                   jax.ShapeDtypeStruct((T, WIDTH), F32)],
        scratch_shapes=[pltpu.VMEM((2, B_TQ, LANES), F32), pltpu.VMEM((2, B_TQ, 1), F32),
                        pltpu.VMEM((2, B_TQ, 1), F32), pltpu.VMEM((2, B_TQ, 1), F32),
                        pltpu.VMEM((2, B_TQ, LANES), BF16), pltpu.VMEM((2, B_TQ, LANES), BF16),
                        pltpu.VMEM((B_TQ, LANES), BF16)],
        vmem_mib=56, args=(qkv, qkv, qkv, out, do))


def _gated_mix(oa_ref, ob_ref, g_ref, bg_ref, wpa_ref, wpb_ref, D):
    ya = _dot(oa_ref[...].astype(BF16), wpa_ref[...])
    yb = _dot(ob_ref[...].astype(BF16), wpb_ref[...])
    sa = jax.nn.sigmoid(g_ref[:, :D] + bg_ref[:, :D])
    sb = jax.nn.sigmoid(g_ref[:, D:] + bg_ref[:, D:])
    return ya, yb, sa, sb


def _proj_fwd(oa, ob, g, bg, wpa, wpb, wo, wl, xin, lng, lnb, layer, *, alpha, name):
    T, D = xin.shape
    tm = _tile(T, 512)
    row = lambda i: (i, 0)
    wspec = lambda r, c: pl.BlockSpec((None, r, c), lambda i: (wl, 0, 0))
    vec = lambda c: pl.BlockSpec((None, 1, c), lambda i: (layer, 0, 0))

    def body(oa_ref, ob_ref, g_ref, bg_ref, wpa_ref, wpb_ref, wo_ref, x_ref, lg_ref, lb_ref, x1_ref, r1_ref):
        ya, yb, sa, sb = _gated_mix(oa_ref, ob_ref, g_ref, bg_ref, wpa_ref, wpb_ref, D)
        mix = _dot((sa * ya + sb * yb).astype(BF16), wo_ref[...])
        r1 = alpha * x_ref[...] + mix
        r1_ref[...] = r1
        x1_ref[...] = _ln_fwd(r1, lg_ref[...], lb_ref[...])

    return pl.pallas_call(
        body, name=name, grid=(T // tm,),
        in_specs=[pl.BlockSpec((tm, WIDTH), row), pl.BlockSpec((tm, WIDTH), row), pl.BlockSpec((tm, 2 * D), row),
                  vec(2 * D), wspec(WIDTH, D), wspec(WIDTH, D), wspec(D, D),
                  pl.BlockSpec((tm, D), row), vec(D), vec(D)],
        out_specs=[pl.BlockSpec((tm, D), row), pl.BlockSpec((tm, D), row)],
        out_shape=[jax.ShapeDtypeStruct((T, D), F32), jax.ShapeDtypeStruct((T, D), F32)],
        compiler_params=_cparams(("arbitrary",), 56),
    )(oa, ob, g, bg, wpa, wpb, wo, xin, lng, lnb)


def _proj_bwd(dx1, r1, lng, oa, ob, g, bg, wpa, wpb, wo, wl, layer, *, name):
    T, D = dx1.shape
    tm = _tile(T, 512)
    row = lambda i: (i, 0)
    fixed = lambda i: (0, 0)
    wspec = lambda r, c: pl.BlockSpec((None, r, c), lambda i: (wl, 0, 0))
    vec = lambda c: pl.BlockSpec((None, 1, c), lambda i: (layer, 0, 0))

    def body(dx_ref, r1_ref, lg_ref, oa_ref, ob_ref, g_ref, bg_ref, wpa_ref, wpb_ref, wo_ref,
             dr_ref, mix_ref, dya_ref, dyb_ref, dg_ref, doa_ref, dob_ref, dlg_ref, dlb_ref, dbg_ref):
        @pl.when(pl.program_id(0) == 0)
        def _():
            dlg_ref[...] = jnp.zeros_like(dlg_ref)
            dlb_ref[...] = jnp.zeros_like(dlb_ref)
            dbg_ref[...] = jnp.zeros_like(dbg_ref)

        dx = dx_ref[...]
        dr, xhat = _ln_bwd(dx, r1_ref[...], lg_ref[...])
        dr_ref[...] = dr
        dlg_ref[...] += jnp.sum(dx * xhat, axis=0, keepdims=True)
        dlb_ref[...] += jnp.sum(dx, axis=0, keepdims=True)
        dmix = _dot_nt(dr.astype(BF16), wo_ref[...])
        ya, yb, sa, sb = _gated_mix(oa_ref, ob_ref, g_ref, bg_ref, wpa_ref, wpb_ref, D)
        mix_ref[...] = (sa * ya + sb * yb).astype(BF16)
        dya = (dmix * sa).astype(BF16)
        dyb = (dmix * sb).astype(BF16)
        dya_ref[...] = dya
        dyb_ref[...] = dyb
        dga = dmix * ya * (sa * (1.0 - sa))
        dgb = dmix * yb * (sb * (1.0 - sb))
        dg_ref[:, :D] = dga.astype(BF16)
        dg_ref[:, D:] = dgb.astype(BF16)
        dbg_ref[:, :D] += jnp.sum(dga, axis=0, keepdims=True)
        dbg_ref[:, D:] += jnp.sum(dgb, axis=0, keepdims=True)
        doa_ref[...] = _dot_nt(dya, wpa_ref[...]).astype(BF16)
        dob_ref[...] = _dot_nt(dyb, wpb_ref[...]).astype(BF16)

    return pl.pallas_call(
        body, name=name, grid=(T // tm,),
        in_specs=[pl.BlockSpec((tm, D), row), pl.BlockSpec((tm, D), row), vec(D),
                  pl.BlockSpec((tm, WIDTH), row), pl.BlockSpec((tm, WIDTH), row), pl.BlockSpec((tm, 2 * D), row),
                  vec(2 * D), wspec(WIDTH, D), wspec(WIDTH, D), wspec(D, D)],
        out_specs=[pl.BlockSpec((tm, D), row), pl.BlockSpec((tm, D), row), pl.BlockSpec((tm, D), row),
                   pl.BlockSpec((tm, D), row), pl.BlockSpec((tm, 2 * D), row),
                   pl.BlockSpec((tm, WIDTH), row), pl.BlockSpec((tm, WIDTH), row),
                   pl.BlockSpec((1, D), fixed), pl.BlockSpec((1, D), fixed), pl.BlockSpec((1, 2 * D), fixed)],
        out_shape=[jax.ShapeDtypeStruct((T, D), F32), jax.ShapeDtypeStruct((T, D), BF16),
                   jax.ShapeDtypeStruct((T, D), BF16), jax.ShapeDtypeStruct((T, D), BF16),
                   jax.ShapeDtypeStruct((T, 2 * D), BF16),
                   jax.ShapeDtypeStruct((T, WIDTH), BF16), jax.ShapeDtypeStruct((T, WIDTH), BF16),
                   jax.ShapeDtypeStruct((1, D), F32), jax.ShapeDtypeStruct((1, D), F32),
                   jax.ShapeDtypeStruct((1, 2 * D), F32)],
        compiler_params=_cparams(("arbitrary",), 56),
    )(dx1, r1, lng, oa, ob, g, bg, wpa, wpb, wo)


def _ffn_fwd(x1, wfi, wfo, wl, lng, lnb, layer, *, alpha, name, job=None):
    T, D = x1.shape
    tf = wfi.shape[-1]
    nj = wfi.shape[0] // 2
    tm = _tile(T, 512)
    vec = lambda c: pl.BlockSpec((None, 1, c), lambda i, j: (layer, 0, 0))

    def body(x_ref, wg_ref, wu_ref, wo_ref, lg_ref, lb_ref, gs_ref, us_ref, r2_ref, x2_ref, acc_ref, xb_ref):
        j = pl.program_id(1)

        @pl.when(j == 0)
        def _():
            xb_ref[...] = x_ref[...].astype(BF16)
            acc_ref[...] = jnp.zeros_like(acc_ref)

        gv = _dot(xb_ref[...], wg_ref[...])
        uv = _dot(xb_ref[...], wu_ref[...])
        gs_ref[...] = gv
        us_ref[...] = uv
        act = gv * jax.nn.sigmoid(gv) * uv
        acc_ref[...] += _dot(act.astype(BF16), wo_ref[...])

        @pl.when(j == nj - 1)
        def _():
            r2 = alpha * x_ref[...] + acc_ref[...]
            r2_ref[...] = r2
            x2_ref[...] = _ln_fwd(r2, lg_ref[...], lb_ref[...])

    return _call_carrying(
        job, body, name=name, grid=(T // tm, nj),
        in_specs=[pl.BlockSpec((tm, D), lambda i, j: (i, 0)),
                  pl.BlockSpec((None, None, D, tf), lambda i, j: (j, wl, 0, 0)),
                  pl.BlockSpec((None, None, D, tf), lambda i, j: (j + nj, wl, 0, 0)),
                  pl.BlockSpec((None, tf, D), lambda i, j: (wl, j, 0)),
                  vec(D), vec(D)],
        out_specs=[pl.BlockSpec((None, tm, tf), lambda i, j: (j, i, 0)),
                   pl.BlockSpec((None, tm, tf), lambda i, j: (j, i, 0)),
                   pl.BlockSpec((tm, D), lambda i, j: (i, 0)),
                   pl.BlockSpec((tm, D), lambda i, j: (i, 0))],
        out_shape=[jax.ShapeDtypeStruct((nj, T, tf), F32), jax.ShapeDtypeStruct((nj, T, tf), F32),
                   jax.ShapeDtypeStruct((T, D), F32), jax.ShapeDtypeStruct((T, D), F32)],
        scratch_shapes=[pltpu.VMEM((tm, D), F32), pltpu.VMEM((tm, D), BF16)],
        vmem_mib=56, args=(x1, wfi, wfi, wfo, lng, lnb))


def _ffn_bwd(dx2, r2, lng, gs, us, wfi, wfo, wl, layer, *, alpha, name, job=None):
    T, D = dx2.shape
    tf = wfi.shape[-1]
    nj = wfi.shape[0] // 2
    tm = _tile(T, 512)
    vec = lambda c: pl.BlockSpec((None, 1, c), lambda i, j: (layer, 0, 0))
    blk = lambda: pl.BlockSpec((None, tm, tf), lambda i, j: (j, i, 0))

    def body(dx_ref, r2_ref, lg_ref, gs_ref, us_ref, wg_ref, wu_ref, wo_ref,
             dr_ref, act_ref, dg_ref, du_ref, dx1_ref, dlg_ref, dlb_ref, acc_ref, drb_ref):
        i = pl.program_id(0)
        j = pl.program_id(1)

        @pl.when((i == 0) & (j == 0))
        def _():
            dlg_ref[...] = jnp.zeros_like(dlg_ref)
            dlb_ref[...] = jnp.zeros_like(dlb_ref)

        @pl.when(j == 0)
        def _():
            dx = dx_ref[...]
            dr, xhat = _ln_bwd(dx, r2_ref[...], lg_ref[...])
            dlg_ref[...] += jnp.sum(dx * xhat, axis=0, keepdims=True)
            dlb_ref[...] += jnp.sum(dx, axis=0, keepdims=True)
            drb_ref[...] = dr.astype(BF16)
            dr_ref[...] = dr.astype(BF16)
            acc_ref[...] = alpha * dr

        dact = _dot_nt(drb_ref[...], wo_ref[...])
        gv = gs_ref[...]
        uv = us_ref[...]
        s = jax.nn.sigmoid(gv)
        silu = gv * s
        act_ref[...] = (silu * uv).astype(BF16)
        dg = (dact * uv * (s * (1.0 + gv * (1.0 - s)))).astype(BF16)
        du = (dact * silu).astype(BF16)
        dg_ref[...] = dg
        du_ref[...] = du
        acc_ref[...] += _dot_nt(dg, wg_ref[...]) + _dot_nt(du, wu_ref[...])

        @pl.when(j == nj - 1)
        def _():
            dx1_ref[...] = acc_ref[...]

    return _call_carrying(
        job, body, name=name, grid=(T // tm, nj),
        in_specs=[pl.BlockSpec((tm, D), lambda i, j: (i, 0)), pl.BlockSpec((tm, D), lambda i, j: (i, 0)), vec(D),
                  blk(), blk(),
                  pl.BlockSpec((None, None, D, tf), lambda i, j: (j, wl, 0, 0)),
                  pl.BlockSpec((None, None, D, tf), lambda i, j: (j + nj, wl, 0, 0)),
                  pl.BlockSpec((None, tf, D), lambda i, j: (wl, j, 0))],
        out_specs=[pl.BlockSpec((tm, D), lambda i, j: (i, 0)), blk(), blk(), blk(),
                   pl.BlockSpec((tm, D), lambda i, j: (i, 0)),
                   pl.BlockSpec((1, D), lambda i, j: (0, 0)), pl.BlockSpec((1, D), lambda i, j: (0, 0))],
        out_shape=[jax.ShapeDtypeStruct((T, D), BF16),
                   jax.ShapeDtypeStruct((nj, T, tf), BF16), jax.ShapeDtypeStruct((nj, T, tf), BF16),
                   jax.ShapeDtypeStruct((nj, T, tf), BF16),
                   jax.ShapeDtypeStruct((T, D), F32),
                   jax.ShapeDtypeStruct((1, D), F32), jax.ShapeDtypeStruct((1, D), F32)],
        scratch_shapes=[pltpu.VMEM((tm, D), F32), pltpu.VMEM((tm, D), BF16)],
        vmem_mib=56, args=(dx2, r2, lng, gs, us, wfi, wfi, wfo))


def _loss_head(y, target, *, name):
    T, D = y.shape
    tm = _tile(T, 1024)

    def body(y_ref, t_ref, dy_ref, sq_ref):
        @pl.when(pl.program_id(0) == 0)
        def _():
            sq_ref[...] = jnp.zeros_like(sq_ref)
        err = y_ref[...] - t_ref[...]
        dy_ref[...] = err * (1.0 / D)
        sq_ref[...] += jnp.sum(err * err, axis=0, keepdims=True)

    return pl.pallas_call(
        body, name=name, grid=(T // tm,),
        in_specs=[pl.BlockSpec((tm, D), lambda i: (i, 0)), pl.BlockSpec((tm, D), lambda i: (i, 0))],
        out_specs=[pl.BlockSpec((tm, D), lambda i: (i, 0)), pl.BlockSpec((1, D), lambda i: (0, 0))],
        out_shape=[jax.ShapeDtypeStruct((T, D), F32), jax.ShapeDtypeStruct((1, D), F32)],
        compiler_params=_cparams(("arbitrary",)),
    )(y, target)


def _my_place():
    return lax.axis_index("x"), lax.axis_index("y"), lax.axis_index("c")


def _peer(place, k):
    x, y, c = place
    return (1 - x if k & 4 else x, 1 - y if k & 2 else y, 1 - c if k & 1 else c)


def _logical(place):
    x, y, c = place
    return 4 * x + 2 * y + c


def _block_of(ref, mode, idx):
    if mode == "blk":
        return ref.at[idx]
    if mode == "col":
        size = ref.shape[2] // N_DEV
        return ref.at[:, :, pl.ds(pl.multiple_of(idx * size, size), size)]
    size = ref.shape[1] // N_DEV
    return ref.at[:, pl.ds(pl.multiple_of(idx * size, size), size), :]


def _full_shape(shard, mode):
    if mode == "blk":
        return (N_DEV,) + shard.shape
    if mode == "col":
        return shard.shape[:2] + (N_DEV * shard.shape[2],)
    return (shard.shape[0], N_DEV * shard.shape[1], shard.shape[2])


class _Exchange:
    def __init__(self, arrays, out_shape, build):
        self.arrays = list(arrays)
        self.out_shape = list(out_shape)
        self.build = build

    def scratch(self):
        n = len(self.arrays)
        return [pltpu.SemaphoreType.DMA((n * N_DEV,)), pltpu.SemaphoreType.DMA((n * N_DEV,)),
                pltpu.SemaphoreType.DMA((n,))]

    def start(self, ins, outs, sems):
        for cp in self.build(ins, outs, *sems):
            cp.start()

    def wait(self, ins, outs, sems):
        for cp in self.build(ins, outs, *sems):
            cp.wait()

    def run(self, name):
        n_in, n_out = len(self.arrays), len(self.out_shape)
        hbm = pl.BlockSpec(memory_space=pltpu.HBM)

        def body(*refs):
            ins, outs, sems = refs[:n_in], refs[n_in:n_in + n_out], refs[n_in + n_out:]
            self.start(ins, outs, sems)
            self.wait(ins, outs, sems)

        return pl.pallas_call(
            body, name=name, in_specs=[hbm] * n_in, out_specs=[hbm] * n_out,
            out_shape=self.out_shape, scratch_shapes=self.scratch(),
        )(*self.arrays)


def _copies_to_all(src_of, dst_of, n, send, recv, local):
    me = _my_place()
    copies = []
    for a in range(n):
        copies.append(pltpu.make_async_copy(src_of(a, _logical(me)), dst_of(a), local.at[a]))
        for k in range(1, N_DEV):
            peer = _peer(me, k)
            copies.append(pltpu.make_async_remote_copy(
                src_ref=src_of(a, _logical(peer)), dst_ref=dst_of(a),
                send_sem=send.at[a * N_DEV + k], recv_sem=recv.at[a * N_DEV + k],
                device_id=peer, device_id_type=MESH))
    return copies


def _gather_job(shards, modes):
    def build(ins, outs, send, recv, local):
        my_id = _logical(_my_place())
        return _copies_to_all(lambda a, dev: ins[a], lambda a: _block_of(outs[a], modes[a], my_id),
                              len(shards), send, recv, local)

    return _Exchange(shards, [jax.ShapeDtypeStruct(_full_shape(s, m), s.dtype) for s, m in zip(shards, modes)], build)


def _grad_block(ref, mode, idx):
    if mode == "blk":
        return ref.at[idx]
    if mode == "col":
        size = ref.shape[1] // N_DEV
        return ref.at[:, pl.ds(pl.multiple_of(idx * size, size), size)]
    size = ref.shape[0] // N_DEV
    return ref.at[pl.ds(pl.multiple_of(idx * size, size), size), :]


def _grad_shard_shape(g, mode):
    if mode == "blk":
        return g.shape[1:]
    if mode == "col":
        return (g.shape[0], g.shape[1] // N_DEV)
    return (g.shape[0] // N_DEV, g.shape[1])


def _grads_job(groups, modes):
    flat = [(g, w, l) for w, per_w in enumerate(groups) for l, g in enumerate(per_w)]

    def build(ins, outs, send, recv, local):
        my_id = _logical(_my_place())
        return _copies_to_all(lambda a, dev: _grad_block(ins[a], modes[flat[a][1]], dev),
                              lambda a: outs[flat[a][1]].at[my_id, flat[a][2]],
                              len(flat), send, recv, local)

    out_shape = [jax.ShapeDtypeStruct((N_DEV, len(per_w)) + _grad_shard_shape(per_w[0], m), per_w[0].dtype)
                 for per_w, m in zip(groups, modes)]
    return _Exchange([g for g, _, _ in flat], out_shape, build)


def _adamw(w, g, m, v):
    m = ADAM_B1 * m + (1.0 - ADAM_B1) * g
    v = ADAM_B2 * v + (1.0 - ADAM_B2) * (g * g)
    m_hat = m / (1.0 - ADAM_B1 ** ADAM_STEP)
    v_hat = v / (1.0 - ADAM_B2 ** ADAM_STEP)
    delta = -ADAM_LR * (m_hat / (jnp.sqrt(v_hat) + ADAM_EPS) + ADAM_WD * w)
    return delta, m, v


def _sum_slots_adamw(slots, w, m, v, *, name):
    R, C = w.shape
    tr = _tile(R, 256)

    def body(s_ref, w_ref, m_ref, v_ref, g_out, d_out, m_out, v_out):
        g = s_ref[0].astype(F32)
        for s in range(1, N_DEV):
            g = g + s_ref[s].astype(F32)
        delta, m_new, v_new = _adamw(w_ref[...], g, m_ref[...], v_ref[...])
        g_out[...] = g
        d_out[...] = delta
        m_out[...] = m_new
        v_out[...] = v_new

    spec = pl.BlockSpec((tr, C), lambda i: (i, 0))
    return pl.pallas_call(
        body, name=name, grid=(R // tr,),
        in_specs=[pl.BlockSpec((N_DEV, tr, C), lambda i: (0, i, 0)), spec, spec, spec],
        out_specs=[spec] * 4,
        out_shape=[jax.ShapeDtypeStruct((R, C), F32)] * 4,
        compiler_params=_cparams(("parallel",)),
    )(slots, w, m, v)


def _small_allreduce_adamw(g, w, m, v, *, name):
    R = g.shape[0]
    vmem = pl.BlockSpec(memory_space=pltpu.VMEM)

    def body(g_ref, w_ref, m_ref, v_ref, g_out, d_out, m_out, v_out, slots, send, recv):
        me = _my_place()
        my_id = _logical(me)
        slots[my_id] = g_ref[...]
        copies = []
        for k in range(1, N_DEV):
            cp = pltpu.make_async_remote_copy(
                src_ref=g_ref, dst_ref=slots.at[my_id], send_sem=send.at[k], recv_sem=recv.at[k],
                device_id=_peer(me, k), device_id_type=MESH)
            cp.start()
            copies.append(cp)
        for cp in copies:
            cp.wait()
        total = slots[0]
        for s in range(1, N_DEV):
            total = total + slots[s]
        delta, m_new, v_new = _adamw(w_ref[...], total, m_ref[...], v_ref[...])
        g_out[...] = total
        d_out[...] = delta
        m_out[...] = m_new
        v_out[...] = v_new

    return pl.pallas_call(
        body, name=name,
        in_specs=[vmem] * 4, out_specs=[vmem] * 4,
        out_shape=[jax.ShapeDtypeStruct((R, LANES), F32)] * 4,
        scratch_shapes=[pltpu.VMEM((N_DEV, R, LANES), F32),
                        pltpu.SemaphoreType.DMA((N_DEV,)), pltpu.SemaphoreType.DMA((N_DEV,))],
    )(g, w, m, v)


def _pack(parts):
    flat = jnp.concatenate([p.reshape(-1) for p in parts])
    rows = -(-flat.shape[0] // (8 * LANES)) * 8
    return jnp.pad(flat, (0, rows * LANES - flat.shape[0])).reshape(rows, LANES)


def _unpack(packed, like):
    flat = packed.reshape(-1)
    out, pos = [], 0
    for p in like:
        out.append(flat[pos:pos + p.size].reshape(p.shape))
        pos += p.size
    return out


def kernel(x, w_in, b_gate, rel_bias, w_proj_a, w_proj_b, w_out, ln1_g, ln1_b, w_ffn_in, w_ffn_out, ln2_g, ln2_b, loss_target, m_w_in, m_b_gate, m_rel_bias, m_w_proj_a, m_w_proj_b, m_w_out, m_ln1_g, m_ln1_b, m_w_ffn_in, m_w_ffn_out, m_ln2_g, m_ln2_b, v_w_in, v_b_gate, v_rel_bias, v_w_proj_a, v_w_proj_b, v_w_out, v_ln1_g, v_ln1_b, v_w_ffn_in, v_w_ffn_out, v_ln2_g, v_ln2_b):
    L = w_in.shape[0]
    T, D = x.shape[1], x.shape[2]
    alpha = float((2 * L) ** 0.25)
    n_qkv = 6 * WIDTH

    big = [w_in, w_proj_a, w_proj_b, w_out, w_ffn_in, w_ffn_out]
    kinds = ["in", "pa", "pb", "o", "fi", "fo"]
    modes = ["col", "col", "col", "row", "blk", "row"]
    mode_of = dict(zip(kinds, modes))
    w_bf = dict(zip(kinds, [w.astype(BF16) for w in big]))

    def gather_of(ks, l):
        return _gather_job([w_bf[k][l:l + 1] for k in ks], [mode_of[k] for k in ks])

    W = [dict() for _ in range(L)]
    (W[0]["in"],) = gather_of(["in"], 0).run("gather_w_in_first")
    vec3 = lambda a: a[:, None, :]
    bg3, l1g, l1b, l2g, l2b = vec3(b_gate), vec3(ln1_g), vec3(ln1_b), vec3(ln2_g), vec3(ln2_b)
    b_col0 = 3 * WIDTH // LANES

    h = x[0]
    saved = []
    for l in range(L):
        ahead = l + 1 < L
        qkv = _mm_nn(h, W[l]["in"], 0, col_off=0, n_cols=n_qkv, out_dtype=BF16, name=f"in_proj_qkv_{l}")
        gates = _mm_nn(h, W[l]["in"], 0, col_off=n_qkv, n_cols=2 * D, out_dtype=F32, name=f"in_proj_gates_{l}")
        kvpad = jnp.pad(qkv[:, WIDTH:3 * WIDTH], ((A_WIN - A_TQ, 0), (0, 0)))
        bias = _toeplitz_bias(rel_bias[l])
        rest = ["pa", "pb", "o", "fi", "fo"]
        oa, got = _attn_a_fwd(qkv, kvpad, bias, name=f"attn_a_fwd_{l}", job=gather_of(rest, 0) if l == 0 else None)
        W[l].update(zip(rest, got))
        early = ["in", "pa", "pb", "o"]
        ob, got = _attn_b_fwd(qkv, col0=b_col0, name=f"attn_b_fwd_{l}", job=gather_of(early, l + 1) if ahead else None)
        W[l + 1 if ahead else l].update(zip(early, got))
        x1, r1 = _proj_fwd(oa, ob, gates, bg3, W[l]["pa"], W[l]["pb"], W[l]["o"], 0, h, l1g, l1b, l,
                           alpha=alpha, name=f"proj_fwd_{l}")
        (gs, us, r2, x2), got = _ffn_fwd(x1, W[l]["fi"], W[l]["fo"], 0, l2g, l2b, l, alpha=alpha, name=f"ffn_fwd_{l}",
                                         job=gather_of(["fi", "fo"], l + 1) if ahead else None)
        W[l + 1 if ahead else l].update(zip(["fi", "fo"], got))
        saved.append((h, qkv, gates, kvpad, bias, oa, ob, x1, r1, gs, us, r2))
        h = x2

    d_h, sq = _loss_head(h, loss_target[0], name="loss_head")
    loss = lax.psum((0.5 / D) * jnp.sum(sq), ("x", "y", "c"))

    g_bg, g_rb, g_l1g, g_l1b, g_l2g, g_l2b = ([None] * L for _ in range(6))
    slot = {k: [None] * L for k in kinds}

    def exchange_of(ks, grads):
        return _grads_job([[g] for g in grads], [mode_of[k] for k in ks])

    w_in_above = None
    for l in reversed(range(L)):
        xin, qkv, gates, kvpad, bias, oa, ob, x1, r1, gs, us, r2 = saved[l]
        (dr2, act, dgt, dup, dx1, g_l2g[l], g_l2b[l]), got = _ffn_bwd(
            d_h, r2, l2g, gs, us, W[l]["fi"], W[l]["fo"], 0, l, alpha=alpha, name=f"ffn_bwd_{l}", job=w_in_above)
        if w_in_above is not None:
            (slot["in"][l + 1],) = got
        g_fo = _mm_tn_blocked_a(act, dr2, name=f"grad_w_ffn_out_{l}").reshape(-1, D)
        g_fi = jnp.concatenate([_mm_tn_blocked(x1, dgt, name=f"grad_w_ffn_gate_{l}"),
                                _mm_tn_blocked(x1, dup, name=f"grad_w_ffn_up_{l}")], axis=0)
        (dr1, mixin, dya, dyb, dgates, doa, dob, g_l1g[l], g_l1b[l], g_bg[l]) = _proj_bwd(
            dx1, r1, l1g, oa, ob, gates, bg3, W[l]["pa"], W[l]["pb"], W[l]["o"], 0, l, name=f"proj_bwd_{l}")
        g_o = _mm_tn(mixin, dr1, tm=_tile(D, 1024), tn=_tile(D, 1024), name=f"grad_w_out_{l}")
        g_pa = _mm_tn(oa, dya, tm=WIDTH, tn=_tile(D, 1024), name=f"grad_w_proj_a_{l}")
        g_pb = _mm_tn(ob, dyb, tm=WIDTH, tn=_tile(D, 1024), name=f"grad_w_proj_b_{l}")
        (dqa, dka, dva, dbias), (slot["fi"][l], slot["fo"][l]) = _attn_a_bwd(
            qkv, kvpad, bias, doa, name=f"attn_a_bwd_{l}", job=exchange_of(["fi", "fo"], [g_fi, g_fo]))
        g_rb[l] = _toeplitz_bias_grad(dbias)
        (dqb, dkb, dvb), (slot["pa"][l], slot["pb"][l], slot["o"][l]) = _attn_b_bwd(
            qkv, ob, dob, col0=b_col0, name=f"attn_b_bwd_{l}", job=exchange_of(["pa", "pb", "o"], [g_pa, g_pb, g_o]))
        pad = A_WIN - A_TQ
        d_pre = jnp.concatenate([dqa, dka[pad:].astype(BF16), dva[pad:].astype(BF16),
                                 dqb, dkb.astype(BF16), dvb.astype(BF16), dgates], axis=1)
        g_in = _mm_tn(xin, d_pre, tm=D, tn=4 * w_in.shape[2], name=f"grad_w_in_{l}")
        w_in_above = exchange_of(["in"], [g_in])
        d_h, got = _mm_nt_add(d_pre, W[l]["in"], 0, dr1, alpha, name=f"grad_x_{l}", job=w_in_above if l == 0 else None)
        if l == 0:
            (slot["in"][0],) = got
    grad_x = d_h[None]

    slots = [jnp.concatenate(slot[k], axis=1) if L > 1 else slot[k][0] for k in kinds]
    moments_m = [m_w_in, m_w_proj_a, m_w_proj_b, m_w_out, m_w_ffn_in, m_w_ffn_out]
    moments_v = [v_w_in, v_w_proj_a, v_w_proj_b, v_w_out, v_w_ffn_in, v_w_ffn_out]
    names = ["w_in", "w_proj_a", "w_proj_b", "w_out", "w_ffn_in", "w_ffn_out"]
    big_out = {}
    for nm, s, w, m, v in zip(names, slots, big, moments_m, moments_v):
        two = lambda a: a.reshape(-1, a.shape[-1])
        res = _sum_slots_adamw(s.reshape(N_DEV, -1, s.shape[-1]), two(w), two(m), two(v), name=f"adamw_{nm}")
        big_out[nm] = [r.reshape(w.shape) for r in res]

    small_w = [b_gate, rel_bias, ln1_g, ln1_b, ln2_g, ln2_b]
    small_g = [jnp.stack(g) for g in (g_bg, g_rb, g_l1g, g_l1b, g_l2g, g_l2b)]
    small_m = [m_b_gate, m_rel_bias, m_ln1_g, m_ln1_b, m_ln2_g, m_ln2_b]
    small_v = [v_b_gate, v_rel_bias, v_ln1_g, v_ln1_b, v_ln2_g, v_ln2_b]
    res = _small_allreduce_adamw(_pack(small_g), _pack(small_w), _pack(small_m), _pack(small_v),
                                 name="allreduce_small_adamw")
    small_names = ["b_gate", "rel_bias", "ln1_g", "ln1_b", "ln2_g", "ln2_b"]
    small_out = {nm: [] for nm in small_names}
    for packed in res:
        for nm, arr in zip(small_names, _unpack(packed, small_w)):
            small_out[nm].append(arr)

    order = ["w_in", "b_gate", "rel_bias", "w_proj_a", "w_proj_b", "w_out", "ln1_g", "ln1_b",
             "w_ffn_in", "w_ffn_out", "ln2_g", "ln2_b"]
    every = {**big_out, **small_out}
    outs = [loss, grad_x]
    for kind in range(4):
        outs += [every[nm][kind] for nm in order]
    return tuple(outs)
```

```python
import functools
import math

import jax
import jax.numpy as jnp
import numpy as np
from jax import lax
from jax.experimental import pallas as pl
from jax.experimental.pallas import tpu as pltpu

F32 = jnp.float32
BF16 = jnp.bfloat16

HEAD_DIM = 64
CHUNK = 64
LEFT_CHUNKS = 8
REL_CLIP = 256
N_REL = 2 * REL_CLIP + 1
WIDTH = 512
LANES = 128
A_TQ = 256
A_WIN = A_TQ + LEFT_CHUNKS * CHUNK
A_STRIP = 128
B_TQ = 512
B_TS = 256
B_PIECE = 64
B_DEAD = -160.0
LN_EPS = 1e-5
QK_SCALE = 1.0 / math.sqrt(HEAD_DIM)
LOG2E = 1.4426950408889634
NEG = -1e30

ADAM_LR = 0.001
ADAM_B1 = 0.9
ADAM_B2 = 0.999
ADAM_EPS = 1e-08
ADAM_WD = 0.01
ADAM_STEP = 10

N_DEV = 8
MESH = pl.DeviceIdType.MESH
MIB = 1024 * 1024


def _cparams(sem=None, vmem_mib=48):
    return pltpu.CompilerParams(dimension_semantics=sem, vmem_limit_bytes=vmem_mib * MIB)


def _dot(a, b):
    return jnp.dot(a, b, preferred_element_type=F32)


def _dot_nt(a, b):
    return lax.dot_general(a, b, (((1,), (1,)), ((), ())), preferred_element_type=F32)


def _dot_tn(a, b):
    return lax.dot_general(a, b, (((0,), (0,)), ((), ())), preferred_element_type=F32)


def _tile(n, pref):
    if n <= pref:
        return n
    for t in range(pref - pref % 8, 0, -8):
        if n % t == 0:
            return t
    raise ValueError((n, pref))


def _in_proj(a, w, layer, *, n_qkv, name, job=None):
    M, K = a.shape
    N = w.shape[2]
    tm = _tile(M, 1024)
    tn = 512
    assert n_qkv % tn == 0 and (N - n_qkv) % tn == 0
    n_q = n_qkv // tn

    def body(a_ref, w_ref, q_ref, g_ref, ab_ref):
        j = pl.program_id(1)

        @pl.when(j == 0)
        def _():
            ab_ref[...] = a_ref[...].astype(BF16)

        res = _dot(ab_ref[...], w_ref[...])

        @pl.when(j < n_q)
        def _():
            q_ref[...] = res.astype(BF16)

        @pl.when(j >= n_q)
        def _():
            g_ref[...] = res

    return _call_carrying(
        job, body, name=name, grid=(M // tm, N // tn),
        in_specs=[pl.BlockSpec((tm, K), lambda i, j: (i, 0)),
                  pl.BlockSpec((None, K, tn), lambda i, j: (layer, 0, j))],
        out_specs=[pl.BlockSpec((tm, tn), lambda i, j: (i, jnp.minimum(j, n_q - 1))),
                   pl.BlockSpec((tm, tn), lambda i, j: (i, jnp.maximum(j - n_q, 0)))],
        out_shape=[jax.ShapeDtypeStruct((M, n_qkv), BF16), jax.ShapeDtypeStruct((M, N - n_qkv), F32)],
        scratch_shapes=[pltpu.VMEM((tm, K), BF16)], vmem_mib=48, args=(a, w))


def _mm_nt_add(a, w, layer, add, add_scale, *, name, job=None):
    M, K = a.shape
    N = w.shape[1]
    tm = _tile(M, 1024)
    tk = _tile(K, 1024)

    def body(a_ref, w_ref, add_ref, o_ref):
        @pl.when(pl.program_id(1) == 0)
        def _():
            o_ref[...] = add_scale * add_ref[...]
        o_ref[...] += _dot_nt(a_ref[...], w_ref[...])

    (out,), rode = _call_carrying(
        job, body, name=name, grid=(M // tm, K // tk),
        in_specs=[pl.BlockSpec((tm, tk), lambda i, k: (i, k)),
                  pl.BlockSpec((None, N, tk), lambda i, k: (layer, 0, k)),
                  pl.BlockSpec((tm, N), lambda i, k: (i, 0))],
        out_specs=[pl.BlockSpec((tm, N), lambda i, k: (i, 0))],
        out_shape=[jax.ShapeDtypeStruct((M, N), F32)],
        scratch_shapes=[], vmem_mib=48, args=(a, w, add))
    return out, rode


def _tn_body(k_axis, n_k):
    def body(a_ref, b_ref, o_ref, acc_ref):
        k = pl.program_id(k_axis)

        @pl.when(k == 0)
        def _():
            acc_ref[...] = jnp.zeros_like(acc_ref)
        acc_ref[...] += _dot_tn(a_ref[...].astype(BF16), b_ref[...].astype(BF16))

        @pl.when(k == n_k - 1)
        def _():
            o_ref[...] = acc_ref[...].astype(o_ref.dtype)
    return body


def _mm_tn(a, b, *, tm, tn, name):
    T, M = a.shape
    N = b.shape[1]
    tk = _tile(T, 512)
    return pl.pallas_call(
        _tn_body(2, T // tk), name=name, grid=(M // tm, N // tn, T // tk),
        in_specs=[pl.BlockSpec((tk, tm), lambda i, j, k: (k, i)),
                  pl.BlockSpec((tk, tn), lambda i, j, k: (k, j))],
        out_specs=pl.BlockSpec((tm, tn), lambda i, j, k: (i, j)),
        out_shape=jax.ShapeDtypeStruct((M, N), BF16),
        scratch_shapes=[pltpu.VMEM((tm, tn), F32)],
        compiler_params=_cparams(("parallel", "parallel", "arbitrary")),
    )(a, b)


def _mm_tn_blocked(a, b, *, name):
    T, M = a.shape
    S, _, N = b.shape
    tk = _tile(T, 512)
    return pl.pallas_call(
        _tn_body(1, T // tk), name=name, grid=(S, T // tk),
        in_specs=[pl.BlockSpec((tk, M), lambda s, k: (k, 0)),
                  pl.BlockSpec((None, tk, N), lambda s, k: (s, k, 0))],
        out_specs=pl.BlockSpec((None, M, N), lambda s, k: (s, 0, 0)),
        out_shape=jax.ShapeDtypeStruct((S, M, N), BF16),
        scratch_shapes=[pltpu.VMEM((M, N), F32)],
        compiler_params=_cparams(("parallel", "arbitrary")),
    )(a, b)


def _mm_tn_blocked_a(a, b, *, name):
    S, T, M = a.shape
    N = b.shape[1]
    tk = _tile(T, 512)
    return pl.pallas_call(
        _tn_body(1, T // tk), name=name, grid=(S, T // tk),
        in_specs=[pl.BlockSpec((None, tk, M), lambda s, k: (s, k, 0)),
                  pl.BlockSpec((tk, N), lambda s, k: (k, 0))],
        out_specs=pl.BlockSpec((None, M, N), lambda s, k: (s, 0, 0)),
        out_shape=jax.ShapeDtypeStruct((S, M, N), BF16),
        scratch_shapes=[pltpu.VMEM((M, N), F32)],
        compiler_params=_cparams(("parallel", "arbitrary")),
    )(a, b)


def _ln_fwd(r, g, b):
    mu = jnp.mean(r, axis=-1, keepdims=True)
    xc = r - mu
    var = jnp.mean(xc * xc, axis=-1, keepdims=True)
    return xc * lax.rsqrt(var + LN_EPS) * g + b


def _ln_bwd(dy, r, g):
    mu = jnp.mean(r, axis=-1, keepdims=True)
    xc = r - mu
    var = jnp.mean(xc * xc, axis=-1, keepdims=True)
    rstd = lax.rsqrt(var + LN_EPS)
    xhat = xc * rstd
    dxh = dy * g
    m1 = jnp.mean(dxh, axis=-1, keepdims=True)
    m2 = jnp.mean(dxh * xhat, axis=-1, keepdims=True)
    return rstd * (dxh - m1 - xhat * m2), xhat


def _lane_is_head0():
    return lax.broadcasted_iota(jnp.int32, (1, LANES), 1) < HEAD_DIM


def _band_shape():
    a = np.arange(A_TQ)[:, None] // CHUNK
    b = np.arange(A_WIN)[None, :] // CHUNK
    return (b >= a) & (b <= a + LEFT_CHUNKS)


def _band_streams(strip):
    return [dict(h=h, n=strip, rows=pl.ds(r0, strip)) for h in range(2) for r0 in range(0, A_TQ, strip)]


def _band_scores(st, i, qh_ref, k2, bias_ref):
    s = _dot_nt(qh_ref[st["h"], st["rows"], :], k2) * QK_SCALE + bias_ref[st["h"], st["rows"], :]
    c = lax.broadcasted_iota(jnp.int32, (st["n"], A_WIN), 1)
    st["s"] = jnp.where(c >= LEFT_CHUNKS * CHUNK - i * A_TQ, s, NEG)


def _band_softmax(st):
    s = st.pop("s")
    e = jnp.exp(s - jnp.max(s, axis=1, keepdims=True))
    st["p"] = e / jnp.sum(e, axis=1, keepdims=True)


def _attn_a_fwd(qkv, kvpad, bias, *, name, job=None):
    T = qkv.shape[0]
    n_hp = WIDTH // LANES

    def body(q_ref, k_ref, v_ref, bias_ref, o_ref, qh_ref, acc_ref):
        i = pl.program_id(1)
        row0 = pl.multiple_of(i * A_TQ, A_TQ)
        q2 = q_ref[...]
        k2 = k_ref[pl.ds(row0, A_WIN), :]
        v2 = v_ref[pl.ds(row0, A_WIN), :]
        head0 = _lane_is_head0()
        qh_ref[0] = jnp.where(head0, q2, jnp.zeros_like(q2))
        qh_ref[1] = jnp.where(head0, jnp.zeros_like(q2), q2)

        def scores(st):
            _band_scores(st, i, qh_ref, k2, bias_ref)

        def values(st):
            acc_ref[st["h"], st["rows"], :] = _dot(st.pop("p").astype(BF16), v2)

        _skewed(_band_streams(A_STRIP), [scores, _band_softmax, values])
        o_ref[...] = jnp.where(head0, acc_ref[0], acc_ref[1]).astype(o_ref.dtype)

    (out,), rode = _call_carrying(
        job, body, name=name, grid=(n_hp, T // A_TQ),
        in_specs=[pl.BlockSpec((A_TQ, LANES), lambda hp, i: (i, hp)),
                  pl.BlockSpec((T + A_WIN - A_TQ, LANES), lambda hp, i: (0, hp)),
                  pl.BlockSpec((T + A_WIN - A_TQ, LANES), lambda hp, i: (0, hp + n_hp)),
                  pl.BlockSpec((2, A_TQ, A_WIN), lambda hp, i: (hp, 0, 0))],
        out_specs=[pl.BlockSpec((A_TQ, LANES), lambda hp, i: (i, hp))],
        out_shape=[jax.ShapeDtypeStruct((T, WIDTH), BF16)],
        scratch_shapes=[pltpu.VMEM((2, A_TQ, LANES), BF16), pltpu.VMEM((2, A_TQ, LANES), F32)],
        vmem_mib=48, args=(qkv, kvpad, kvpad, bias))
    return out, rode


def _attn_a_bwd(qkv, kvpad, bias, do, *, name, job=None):
    T = qkv.shape[0]
    TP = T + A_WIN - A_TQ
    n_hp = WIDTH // LANES

    def body(q_ref, k_ref, v_ref, bias_ref, do_ref, dq_ref, dk_ref, dv_ref, db_ref, qh_ref, doh_ref, dqa_ref):
        i = pl.program_id(1)

        @pl.when(i == 0)
        def _():
            dk_ref[...] = jnp.zeros_like(dk_ref)
            dv_ref[...] = jnp.zeros_like(dv_ref)
            db_ref[...] = jnp.zeros_like(db_ref)

        row0 = pl.multiple_of(i * A_TQ, A_TQ)
        window = pl.ds(row0, A_WIN)
        q2 = q_ref[...]
        do2 = do_ref[...]
        k2 = k_ref[window, :]
        v2 = v_ref[window, :]
        head0 = _lane_is_head0()
        zero = jnp.zeros_like(q2)
        qh_ref[0] = jnp.where(head0, q2, zero)
        qh_ref[1] = jnp.where(head0, zero, q2)
        doh_ref[0] = jnp.where(head0, do2, zero)
        doh_ref[1] = jnp.where(head0, zero, do2)
        dk = [[], []]
        dv = [[], []]

        def scores(st):
            _band_scores(st, i, qh_ref, k2, bias_ref)
            st["dp"] = _dot_nt(doh_ref[st["h"], st["rows"], :], v2)

        def dscores(st):
            _band_softmax(st)
            p, dp = st.pop("p"), st.pop("dp")
            ds = p * (dp - jnp.sum(p * dp, axis=1, keepdims=True))
            db_ref[st["h"], st["rows"], :] += ds
            st["dsb"] = (ds * QK_SCALE).astype(BF16)
            st["pb"] = p.astype(BF16)

        def grads(st):
            h, rows = st["h"], st["rows"]
            dsb = st.pop("dsb")
            dqa_ref[h, rows, :] = _dot(dsb, k2)
            dk[h].append(_dot_tn(dsb, q_ref[rows, :]))
            dv[h].append(_dot_tn(st.pop("pb"), do_ref[rows, :]))

        _skewed(_band_streams(A_TQ), [scores, dscores, grads])
        dq_ref[...] = jnp.where(head0, dqa_ref[0], dqa_ref[1]).astype(dq_ref.dtype)
        dk_ref[window, :] += jnp.where(head0, sum(dk[0]), sum(dk[1]))
        dv_ref[window, :] += jnp.where(head0, sum(dv[0]), sum(dv[1]))

    return _call_carrying(
        job, body, name=name, grid=(n_hp, T // A_TQ),
        in_specs=[pl.BlockSpec((A_TQ, LANES), lambda hp, i: (i, hp)),
                  pl.BlockSpec((TP, LANES), lambda hp, i: (0, hp)),
                  pl.BlockSpec((TP, LANES), lambda hp, i: (0, hp + n_hp)),
                  pl.BlockSpec((2, A_TQ, A_WIN), lambda hp, i: (hp, 0, 0)),
                  pl.BlockSpec((A_TQ, LANES), lambda hp, i: (i, hp))],
        out_specs=[pl.BlockSpec((A_TQ, LANES), lambda hp, i: (i, hp)),
                   pl.BlockSpec((TP, LANES), lambda hp, i: (0, hp)),
                   pl.BlockSpec((TP, LANES), lambda hp, i: (0, hp)),
                   pl.BlockSpec((2, A_TQ, A_WIN), lambda hp, i: (hp, 0, 0))],
        out_shape=[jax.ShapeDtypeStruct((T, WIDTH), BF16),
                   jax.ShapeDtypeStruct((TP, WIDTH), F32),
                   jax.ShapeDtypeStruct((TP, WIDTH), F32),
                   jax.ShapeDtypeStruct((WIDTH // HEAD_DIM, A_TQ, A_WIN), F32)],
        scratch_shapes=[pltpu.VMEM((2, A_TQ, LANES), BF16), pltpu.VMEM((2, A_TQ, LANES), BF16),
                        pltpu.VMEM((2, A_TQ, LANES), F32)],
        vmem_mib=56, args=(qkv, kvpad, kvpad, bias, do))


def _toeplitz_bias(rb):
    H = rb.shape[0]
    span = A_TQ + A_WIN - 1
    n_tail = span - (N_REL - 1)
    ext = jnp.concatenate([rb[:, 1:], jnp.broadcast_to(rb[:, N_REL - 1:], (H, n_tail))], axis=1)
    rev = jnp.pad(ext[:, ::-1], ((0, 0), (0, 1)))
    flat = jnp.broadcast_to(rev[:, None, :], (H, A_TQ, span + 1)).reshape(H, A_TQ * (span + 1))
    skew = flat[:, :A_TQ * span].reshape(H, A_TQ, span)
    return jnp.where(_band_shape()[None], skew[:, :, A_TQ - 1:A_TQ - 1 + A_WIN], NEG)


def _toeplitz_bias_grad(db):
    H = db.shape[0]
    span = A_TQ + A_WIN - 1
    d_skew = jnp.pad(db, ((0, 0), (0, 0), (A_TQ - 1, span - (A_TQ - 1) - A_WIN)))
    d_flat = jnp.pad(d_skew.reshape(H, A_TQ * span), ((0, 0), (0, A_TQ)))
    g_ext = jnp.sum(d_flat.reshape(H, A_TQ, span + 1), axis=1)[:, :span][:, ::-1]
    last = g_ext[:, N_REL - 2] + jnp.sum(g_ext[:, N_REL - 1:], axis=1)
    return jnp.concatenate([jnp.zeros((H, 1), F32), g_ext[:, :N_REL - 2], last[:, None]], axis=1)


def _split_bf16(x):
    hi = x.astype(BF16)
    lo = (x - hi.astype(F32)).astype(BF16)
    return hi, lo


def _sb_streams(d):
    out = []
    for h in range(2):
        for r in range(B_TQ // B_TS):
            if d is not None and d > r:
                continue
            out.append(dict(h=h, r=r, rows=pl.ds(r * B_TS, B_TS), diag=(d is not None and d == r)))
    return out


def _piece_rows(st, p):
    return pl.ds(st["r"] * B_TS + p, B_PIECE)


def _rows_cat(parts):
    return jnp.concatenate(parts, axis=0)


def _skewed(streams, stages):
    for t in range(len(streams) + len(stages) - 1):
        for s, st in enumerate(streams):
            if 0 <= t - s < len(stages):
                stages[t - s](st)


def _sb_logs(st, z2):
    log_beta, log_keep, keep_bf = [], [], []
    for p in range(0, B_TS, B_PIECE):
        z = z2[p:p + B_PIECE]
        lp2 = jnp.log(1.0 + jnp.exp2(-jnp.abs(z))) * LOG2E
        lb = jnp.minimum(z, 0.0) - lp2
        lk = lb - z
        if st["diag"]:
            lk = jnp.where(_strict_lower(p), lk, 0.0)
        log_beta.append(lb)
        log_keep.append(lk)
        keep_bf.append(lk.astype(BF16))
    st["log_beta"] = _rows_cat(log_beta)
    st["log_keep"] = _rows_cat(log_keep)
    st["keep_bf"] = _rows_cat(keep_bf)


def _strict_lower(p):
    t = p + lax.broadcasted_iota(jnp.int32, (B_PIECE, B_TS), 0)
    s = lax.broadcasted_iota(jnp.int32, (B_PIECE, B_TS), 1)
    return s < t


def _tri(strict):
    j = lax.broadcasted_iota(jnp.int32, (B_TS, B_TS), 0)
    s = lax.broadcasted_iota(jnp.int32, (B_TS, B_TS), 1)
    return jnp.where(j > s if strict else j >= s, 1.0, 0.0).astype(BF16)


def _call_carrying(job, body, *, name, grid, in_specs, out_specs, out_shape, scratch_shapes, vmem_mib, args):
    n_in, n_out, n_scr = len(in_specs), len(out_specs), len(scratch_shapes)
    if job is None:
        res = pl.pallas_call(body, name=name, grid=grid, in_specs=in_specs, out_specs=out_specs,
                             out_shape=out_shape, scratch_shapes=scratch_shapes,
                             compiler_params=_cparams(("arbitrary",) * len(grid), vmem_mib))(*args)
        return res, []
    j_in, j_out = len(job.arrays), len(job.out_shape)
    hbm = pl.BlockSpec(memory_space=pltpu.HBM)

    def carrying(*refs):
        refs = list(refs)
        ins, refs = refs[:n_in], refs[n_in:]
        j_ins, refs = refs[:j_in], refs[j_in:]
        outs, refs = refs[:n_out], refs[n_out:]
        j_outs, refs = refs[:j_out], refs[j_out:]
        scr, sems = refs[:n_scr], refs[n_scr:]
        first = functools.reduce(jnp.logical_and, [pl.program_id(d) == 0 for d in range(len(grid))])
        last = functools.reduce(jnp.logical_and, [pl.program_id(d) == grid[d] - 1 for d in range(len(grid))])

        @pl.when(first)
        def _():
            job.start(j_ins, j_outs, sems)

        body(*ins, *outs, *scr)

        @pl.when(last)
        def _():
            job.wait(j_ins, j_outs, sems)

    res = pl.pallas_call(
        carrying, name=name, grid=grid,
        in_specs=list(in_specs) + [hbm] * j_in, out_specs=list(out_specs) + [hbm] * j_out,
        out_shape=list(out_shape) + job.out_shape, scratch_shapes=list(scratch_shapes) + job.scratch(),
        compiler_params=_cparams(("arbitrary",) * len(grid), vmem_mib))(*args, *job.arrays)
    return res[:n_out], res[n_out:]


def _attn_b_fwd(qkv, *, col0, name, job=None):
    T = qkv.shape[0]
    n_hp = WIDTH // LANES
    sub = B_TQ // B_TS

    def body(q_ref, k_ref, v_ref, o_ref, acc_ref, car_ref, qh_ref):
        i = pl.program_id(1)
        q2 = q_ref[...]
        head0 = _lane_is_head0()
        qh_ref[0] = jnp.where(head0, q2, jnp.zeros_like(q2))
        qh_ref[1] = jnp.where(head0, jnp.zeros_like(q2), q2)
        tri_s = _tri(True)
        acc_ref[...] = jnp.zeros_like(acc_ref)
        car_ref[...] = jnp.zeros_like(car_ref)

        def tile(kb, d):
            k0 = pl.multiple_of(kb * B_TS, B_TS)
            k2 = k_ref[pl.ds(k0, B_TS), :]
            v2 = v_ref[pl.ds(k0, B_TS), :]

            def scores(st):
                st["z2"] = _dot_nt(qh_ref[st["h"], st["rows"], :], k2) * (QK_SCALE * LOG2E)

            def logs(st):
                _sb_logs(st, st.pop("z2"))

            def suffix(st):
                st["suffix"] = _dot(st.pop("keep_bf"), tri_s)

            def weights(st):
                log_beta, suffix, log_keep = st.pop("log_beta"), st.pop("suffix"), st.pop("log_keep")
                wb = []
                for p in range(0, B_TS, B_PIECE):
                    rows = _piece_rows(st, p)
                    car = car_ref[st["h"], rows, :]
                    w = jnp.exp2(log_beta[p:p + B_PIECE] + suffix[p:p + B_PIECE] + car)
                    if st["diag"]:
                        w = jnp.where(_strict_lower(p), w, 0.0)
                    wb.append(w.astype(BF16))
                    car_ref[st["h"], rows, :] = car + jnp.sum(log_keep[p:p + B_PIECE], axis=1, keepdims=True)
                st["wb"] = _rows_cat(wb)

            def values(st):
                acc_ref[st["h"], st["rows"], :] += _dot(st.pop("wb"), v2)

            _skewed(_sb_streams(d), [scores, logs, suffix, weights, values])

        for d in reversed(range(sub)):
            tile(i * sub + d, d)

        def alive(c):
            return (c[0] < i * sub) & (c[1] > B_DEAD)

        def step(c):
            tile(i * sub - 1 - c[0], None)
            return c[0] + 1, jnp.max(car_ref[...])

        lax.while_loop(alive, step, (jnp.int32(0), jnp.float32(0.0)))
        o_ref[...] = jnp.where(head0, acc_ref[0], acc_ref[1])

    (out,), rode = _call_carrying(
        job, body, name=name, grid=(n_hp, T // B_TQ),
        in_specs=[pl.BlockSpec((B_TQ, LANES), lambda hp, i: (i, hp + col0)),
                  pl.BlockSpec((T, LANES), lambda hp, i: (0, hp + col0 + n_hp)),
                  pl.BlockSpec((T, LANES), lambda hp, i: (0, hp + col0 + 2 * n_hp))],
        out_specs=[pl.BlockSpec((B_TQ, LANES), lambda hp, i: (i, hp))],
        out_shape=[jax.ShapeDtypeStruct((T, WIDTH), F32)],
        scratch_shapes=[pltpu.VMEM((2, B_TQ, LANES), F32), pltpu.VMEM((2, B_TQ, 1), F32),
                        pltpu.VMEM((2, B_TQ, LANES), BF16)],
        vmem_mib=48, args=(qkv, qkv, qkv))
    return out, rode


def _attn_b_bwd(qkv, out, do, *, col0, name, job=None):
    T = qkv.shape[0]
    n_hp = WIDTH // LANES
    sub = B_TQ // B_TS

    def body(q_ref, k_ref, v_ref, o_ref, do_ref, dq_ref, dk_ref, dv_ref,
             dqa_ref, car_ref, carr_ref, tot_ref, qh_ref, doh_ref, qs_ref):
        i = pl.program_id(1)

        @pl.when(i == 0)
        def _():
            dk_ref[...] = jnp.zeros_like(dk_ref)
            dv_ref[...] = jnp.zeros_like(dv_ref)

        q2 = q_ref[...]
        do2 = do_ref[...]
        head0 = _lane_is_head0()
        zero = jnp.zeros_like(q2)
        qh_ref[0] = jnp.where(head0, q2, zero)
        qh_ref[1] = jnp.where(head0, zero, q2)
        doh_ref[0] = jnp.where(head0, do2, zero)
        doh_ref[1] = jnp.where(head0, zero, do2)
        scale = jnp.asarray(QK_SCALE, BF16)
        qs_ref[...] = q2 * scale
        tri_s = _tri(True)
        tri_i = _tri(False)
        prod = do2.astype(F32) * o_ref[...]
        tot_ref[0] = jnp.sum(jnp.where(head0, prod, 0.0), axis=1, keepdims=True)
        tot_ref[1] = jnp.sum(jnp.where(head0, 0.0, prod), axis=1, keepdims=True)
        dqa_ref[...] = jnp.zeros_like(dqa_ref)
        car_ref[...] = jnp.zeros_like(car_ref)
        carr_ref[...] = jnp.zeros_like(carr_ref)

        def tile(kb, d):
            k0 = pl.multiple_of(kb * B_TS, B_TS)
            keys = pl.ds(k0, B_TS)
            k2 = k_ref[keys, :]
            v2 = v_ref[keys, :]
            k2s = k2 * scale

            def scores(st):
                st["z2"] = _dot_nt(qh_ref[st["h"], st["rows"], :], k2) * (QK_SCALE * LOG2E)
                st["dw"] = _dot_nt(doh_ref[st["h"], st["rows"], :], v2)

            def logs(st):
                _sb_logs(st, st.pop("z2"))

            def suffix(st):
                st["suffix"] = _dot(st.pop("keep_bf"), tri_s)

            def weights(st):
                h = st["h"]
                suffix, dw = st.pop("suffix"), st.pop("dw")
                wb, dlog, hi, lo = [], [], [], []
                for p in range(0, B_TS, B_PIECE):
                    rows = _piece_rows(st, p)
                    car = car_ref[h, rows, :]
                    w = jnp.exp2(st["log_beta"][p:p + B_PIECE] + suffix[p:p + B_PIECE] + car)
                    if st["diag"]:
                        w = jnp.where(_strict_lower(p), w, 0.0)
                    w = w.astype(BF16)
                    dl = w.astype(F32) * dw[p:p + B_PIECE]
                    dl_hi, dl_lo = _split_bf16(dl)
                    wb.append(w)
                    dlog.append(dl)
                    hi.append(dl_hi)
                    lo.append(dl_lo)
                    car_ref[h, rows, :] = car + jnp.sum(st["log_keep"][p:p + B_PIECE], axis=1, keepdims=True)
                st["wb"], st["dlog"], st["hi"], st["lo"] = _rows_cat(wb), _rows_cat(dlog), _rows_cat(hi), _rows_cat(lo)

            def later(st):
                st["later"] = _dot(st.pop("hi"), tri_i) + _dot(st.pop("lo"), tri_i)

            def dscores(st):
                h = st["h"]
                later, dlog = st.pop("later"), st.pop("dlog")
                log_keep, log_beta = st.pop("log_keep"), st.pop("log_beta")
                dzb = []
                for p in range(0, B_TS, B_PIECE):
                    rows = _piece_rows(st, p)
                    pc = slice(p, p + B_PIECE)
                    carr = carr_ref[h, rows, :]
                    earlier = tot_ref[h, rows, :] - (later[pc] + carr)
                    dz = dlog[pc] * jnp.exp2(log_keep[pc]) - jnp.exp2(log_beta[pc]) * earlier
                    if st["diag"]:
                        dz = jnp.where(_strict_lower(p), dz, 0.0)
                    dzb.append(dz.astype(BF16))
                    carr_ref[h, rows, :] = carr + jnp.sum(dlog[pc], axis=1, keepdims=True)
                st["dzb"] = _rows_cat(dzb)

            def grads(st):
                h, rows = st["h"], st["rows"]
                mine = head0 if h == 0 else jnp.logical_not(head0)
                dzb = st.pop("dzb")
                dqa_ref[h, rows, :] += _dot(dzb, k2s)
                dk_ref[keys, :] += jnp.where(mine, _dot_tn(dzb, qs_ref[rows, :]), 0.0)
                dv_ref[keys, :] += jnp.where(mine, _dot_tn(st.pop("wb"), do_ref[rows, :]), 0.0)

            _skewed(_sb_streams(d), [scores, logs, suffix, weights, later, dscores, grads])

        for d in reversed(range(sub)):
            tile(i * sub + d, d)

        def alive(c):
            return (c[0] < i * sub) & (c[1] > B_DEAD)

        def step(c):
            tile(i * sub - 1 - c[0], None)
            return c[0] + 1, jnp.max(car_ref[...])

        lax.while_loop(alive, step, (jnp.int32(0), jnp.float32(0.0)))
        dq_ref[...] = jnp.where(head0, dqa_ref[0], dqa_ref[1]).astype(dq_ref.dtype)

    return _call_carrying(
        job, body, name=name, grid=(n_hp, T // B_TQ),
        in_specs=[pl.BlockSpec((B_TQ, LANES), lambda hp, i: (i, hp + col0)),
                  pl.BlockSpec((T, LANES), lambda hp, i: (0, hp + col0 + n_hp)),
                  pl.BlockSpec((T, LANES), lambda hp, i: (0, hp + col0 + 2 * n_hp)),
                  pl.BlockSpec((B_TQ, LANES), lambda hp, i: (i, hp)),
                  pl.BlockSpec((B_TQ, LANES), lambda hp, i: (i, hp))],
        out_specs=[pl.BlockSpec((B_TQ, LANES), lambda hp, i: (i, hp)),
                   pl.BlockSpec((T, LANES), lambda hp, i: (0, hp)),
                   pl.BlockSpec((T, LANES), lambda hp, i: (0, hp))],
        out_shape=[jax.ShapeDtypeStruct((T, WIDTH), BF16),
                   jax.ShapeDtypeStruct((T, WIDTH), F32),
                   jax.ShapeDtypeStruct((T, WIDTH), F32)],
        scratch_shapes=[pltpu.VMEM((2, B_TQ, LANES), F32), pltpu.VMEM((2, B_TQ, 1), F32),
                        pltpu.VMEM((2, B_TQ, 1), F32), pltpu.VMEM((2, B_TQ, 1), F32),
                        pltpu.VMEM((2, B_TQ, LANES), BF16), pltpu.VMEM((2, B_TQ, LANES), BF16),
                        pltpu.VMEM((B_TQ, LANES), BF16)],
        vmem_mib=56, args=(qkv, qkv, qkv, out, do))


def _gated_mix(oa_ref, ob_ref, g_ref, bg_ref, wpa_ref, wpb_ref, D):
    ya = _dot(oa_ref[...].astype(BF16), wpa_ref[...])
    yb = _dot(ob_ref[...].astype(BF16), wpb_ref[...])
    sa = jax.nn.sigmoid(g_ref[:, :D] + bg_ref[:, :D])
    sb = jax.nn.sigmoid(g_ref[:, D:] + bg_ref[:, D:])
    return ya, yb, sa, sb


def _proj_fwd(oa, ob, g, bg, wpa, wpb, wo, wl, xin, lng, lnb, layer, *, alpha, name):
    T, D = xin.shape
    tm = _tile(T, 512)
    row = lambda i: (i, 0)
    wspec = lambda r, c: pl.BlockSpec((None, r, c), lambda i: (wl, 0, 0))
    vec = lambda c: pl.BlockSpec((None, 1, c), lambda i: (layer, 0, 0))

    def body(oa_ref, ob_ref, g_ref, bg_ref, wpa_ref, wpb_ref, wo_ref, x_ref, lg_ref, lb_ref, x1_ref, r1_ref):
        ya, yb, sa, sb = _gated_mix(oa_ref, ob_ref, g_ref, bg_ref, wpa_ref, wpb_ref, D)
        mix = _dot((sa * ya + sb * yb).astype(BF16), wo_ref[...])
        r1 = alpha * x_ref[...] + mix
        r1_ref[...] = r1
        x1_ref[...] = _ln_fwd(r1, lg_ref[...], lb_ref[...])

    return pl.pallas_call(
        body, name=name, grid=(T // tm,),
        in_specs=[pl.BlockSpec((tm, WIDTH), row), pl.BlockSpec((tm, WIDTH), row), pl.BlockSpec((tm, 2 * D), row),
                  vec(2 * D), wspec(WIDTH, D), wspec(WIDTH, D), wspec(D, D),
                  pl.BlockSpec((tm, D), row), vec(D), vec(D)],
        out_specs=[pl.BlockSpec((tm, D), row), pl.BlockSpec((tm, D), row)],
        out_shape=[jax.ShapeDtypeStruct((T, D), F32), jax.ShapeDtypeStruct((T, D), F32)],
        compiler_params=_cparams(("arbitrary",), 56),
    )(oa, ob, g, bg, wpa, wpb, wo, xin, lng, lnb)


def _proj_bwd(dx1, r1, lng, oa, ob, g, bg, wpa, wpb, wo, wl, layer, *, name):
    T, D = dx1.shape
    tm = _tile(T, 512)
    row = lambda i: (i, 0)
    fixed = lambda i: (0, 0)
    wspec = lambda r, c: pl.BlockSpec((None, r, c), lambda i: (wl, 0, 0))
    vec = lambda c: pl.BlockSpec((None, 1, c), lambda i: (layer, 0, 0))

    def body(dx_ref, r1_ref, lg_ref, oa_ref, ob_ref, g_ref, bg_ref, wpa_ref, wpb_ref, wo_ref,
             dr_ref, mix_ref, dya_ref, dyb_ref, dg_ref, doa_ref, dob_ref, dlg_ref, dlb_ref, dbg_ref):
        @pl.when(pl.program_id(0) == 0)
        def _():
            dlg_ref[...] = jnp.zeros_like(dlg_ref)
            dlb_ref[...] = jnp.zeros_like(dlb_ref)
            dbg_ref[...] = jnp.zeros_like(dbg_ref)

        dx = dx_ref[...]
        dr, xhat = _ln_bwd(dx, r1_ref[...], lg_ref[...])
        dr_ref[...] = dr
        dlg_ref[...] += jnp.sum(dx * xhat, axis=0, keepdims=True)
        dlb_ref[...] += jnp.sum(dx, axis=0, keepdims=True)
        dmix = _dot_nt(dr.astype(BF16), wo_ref[...])
        ya, yb, sa, sb = _gated_mix(oa_ref, ob_ref, g_ref, bg_ref, wpa_ref, wpb_ref, D)
        mix_ref[...] = (sa * ya + sb * yb).astype(BF16)
        dya = (dmix * sa).astype(BF16)
        dyb = (dmix * sb).astype(BF16)
        dya_ref[...] = dya
        dyb_ref[...] = dyb
        dga = dmix * ya * (sa * (1.0 - sa))
        dgb = dmix * yb * (sb * (1.0 - sb))
        dg_ref[:, :D] = dga.astype(BF16)
        dg_ref[:, D:] = dgb.astype(BF16)
        dbg_ref[:, :D] += jnp.sum(dga, axis=0, keepdims=True)
        dbg_ref[:, D:] += jnp.sum(dgb, axis=0, keepdims=True)
        doa_ref[...] = _dot_nt(dya, wpa_ref[...]).astype(BF16)
        dob_ref[...] = _dot_nt(dyb, wpb_ref[...]).astype(BF16)

    return pl.pallas_call(
        body, name=name, grid=(T // tm,),
        in_specs=[pl.BlockSpec((tm, D), row), pl.BlockSpec((tm, D), row), vec(D),
                  pl.BlockSpec((tm, WIDTH), row), pl.BlockSpec((tm, WIDTH), row), pl.BlockSpec((tm, 2 * D), row),
                  vec(2 * D), wspec(WIDTH, D), wspec(WIDTH, D), wspec(D, D)],
        out_specs=[pl.BlockSpec((tm, D), row), pl.BlockSpec((tm, D), row), pl.BlockSpec((tm, D), row),
                   pl.BlockSpec((tm, D), row), pl.BlockSpec((tm, 2 * D), row),
                   pl.BlockSpec((tm, WIDTH), row), pl.BlockSpec((tm, WIDTH), row),
                   pl.BlockSpec((1, D), fixed), pl.BlockSpec((1, D), fixed), pl.BlockSpec((1, 2 * D), fixed)],
        out_shape=[jax.ShapeDtypeStruct((T, D), F32), jax.ShapeDtypeStruct((T, D), BF16),
                   jax.ShapeDtypeStruct((T, D), BF16), jax.ShapeDtypeStruct((T, D), BF16),
                   jax.ShapeDtypeStruct((T, 2 * D), BF16),
                   jax.ShapeDtypeStruct((T, WIDTH), BF16), jax.ShapeDtypeStruct((T, WIDTH), BF16),
                   jax.ShapeDtypeStruct((1, D), F32), jax.ShapeDtypeStruct((1, D), F32),
                   jax.ShapeDtypeStruct((1, 2 * D), F32)],
        compiler_params=_cparams(("arbitrary",), 56),
    )(dx1, r1, lng, oa, ob, g, bg, wpa, wpb, wo)


def _ffn_fwd(x1, wfi, wfo, wl, lng, lnb, layer, *, alpha, name, job=None):
    T, D = x1.shape
    tf = wfi.shape[-1]
    nj = wfi.shape[0] // 2
    tm = _tile(T, 512)
    vec = lambda c: pl.BlockSpec((None, 1, c), lambda i, j: (layer, 0, 0))

    def body(x_ref, wg_ref, wu_ref, wo_ref, lg_ref, lb_ref, gs_ref, us_ref, r2_ref, x2_ref, acc_ref, xb_ref):
        j = pl.program_id(1)

        @pl.when(j == 0)
        def _():
            xb_ref[...] = x_ref[...].astype(BF16)
            acc_ref[...] = jnp.zeros_like(acc_ref)

        gv = _dot(xb_ref[...], wg_ref[...])
        uv = _dot(xb_ref[...], wu_ref[...])
        gs_ref[...] = gv
        us_ref[...] = uv
        act = gv * jax.nn.sigmoid(gv) * uv
        acc_ref[...] += _dot(act.astype(BF16), wo_ref[...])

        @pl.when(j == nj - 1)
        def _():
            r2 = alpha * x_ref[...] + acc_ref[...]
            r2_ref[...] = r2
            x2_ref[...] = _ln_fwd(r2, lg_ref[...], lb_ref[...])

    return _call_carrying(
        job, body, name=name, grid=(T // tm, nj),
        in_specs=[pl.BlockSpec((tm, D), lambda i, j: (i, 0)),
                  pl.BlockSpec((None, None, D, tf), lambda i, j: (j, wl, 0, 0)),
                  pl.BlockSpec((None, None, D, tf), lambda i, j: (j + nj, wl, 0, 0)),
                  pl.BlockSpec((None, tf, D), lambda i, j: (wl, j, 0)),
                  vec(D), vec(D)],
        out_specs=[pl.BlockSpec((None, tm, tf), lambda i, j: (j, i, 0)),
                   pl.BlockSpec((None, tm, tf), lambda i, j: (j, i, 0)),
                   pl.BlockSpec((tm, D), lambda i, j: (i, 0)),
                   pl.BlockSpec((tm, D), lambda i, j: (i, 0))],
        out_shape=[jax.ShapeDtypeStruct((nj, T, tf), F32), jax.ShapeDtypeStruct((nj, T, tf), F32),
                   jax.ShapeDtypeStruct((T, D), F32), jax.ShapeDtypeStruct((T, D), F32)],
        scratch_shapes=[pltpu.VMEM((tm, D), F32), pltpu.VMEM((tm, D), BF16)],
        vmem_mib=56, args=(x1, wfi, wfi, wfo, lng, lnb))


def _ffn_bwd(dx2, r2, lng, gs, us, wfi, wfo, wl, layer, *, alpha, name, job=None):
    T, D = dx2.shape
    tf = wfi.shape[-1]
    nj = wfi.shape[0] // 2
    tm = _tile(T, 512)
    vec = lambda c: pl.BlockSpec((None, 1, c), lambda i, j: (layer, 0, 0))
    blk = lambda: pl.BlockSpec((None, tm, tf), lambda i, j: (j, i, 0))

    def body(dx_ref, r2_ref, lg_ref, gs_ref, us_ref, wg_ref, wu_ref, wo_ref,
             dr_ref, act_ref, dg_ref, du_ref, dx1_ref, dlg_ref, dlb_ref, acc_ref, drb_ref):
        i = pl.program_id(0)
        j = pl.program_id(1)

        @pl.when((i == 0) & (j == 0))
        def _():
            dlg_ref[...] = jnp.zeros_like(dlg_ref)
            dlb_ref[...] = jnp.zeros_like(dlb_ref)

        @pl.when(j == 0)
        def _():
            dx = dx_ref[...]
            dr, xhat = _ln_bwd(dx, r2_ref[...], lg_ref[...])
            dlg_ref[...] += jnp.sum(dx * xhat, axis=0, keepdims=True)
            dlb_ref[...] += jnp.sum(dx, axis=0, keepdims=True)
            drb_ref[...] = dr.astype(BF16)
            dr_ref[...] = dr.astype(BF16)
            acc_ref[...] = alpha * dr

        dact = _dot_nt(drb_ref[...], wo_ref[...])
        gv = gs_ref[...]
        uv = us_ref[...]
        s = jax.nn.sigmoid(gv)
        silu = gv * s
        act_ref[...] = (silu * uv).astype(BF16)
        dg = (dact * uv * (s * (1.0 + gv * (1.0 - s)))).astype(BF16)
        du = (dact * silu).astype(BF16)
        dg_ref[...] = dg
        du_ref[...] = du
        acc_ref[...] += _dot_nt(dg, wg_ref[...]) + _dot_nt(du, wu_ref[...])

        @pl.when(j == nj - 1)
        def _():
            dx1_ref[...] = acc_ref[...]

    return _call_carrying(
        job, body, name=name, grid=(T // tm, nj),
        in_specs=[pl.BlockSpec((tm, D), lambda i, j: (i, 0)), pl.BlockSpec((tm, D), lambda i, j: (i, 0)), vec(D),
                  blk(), blk(),
                  pl.BlockSpec((None, None, D, tf), lambda i, j: (j, wl, 0, 0)),
                  pl.BlockSpec((None, None, D, tf), lambda i, j: (j + nj, wl, 0, 0)),
                  pl.BlockSpec((None, tf, D), lambda i, j: (wl, j, 0))],
        out_specs=[pl.BlockSpec((tm, D), lambda i, j: (i, 0)), blk(), blk(), blk(),
                   pl.BlockSpec((tm, D), lambda i, j: (i, 0)),
                   pl.BlockSpec((1, D), lambda i, j: (0, 0)), pl.BlockSpec((1, D), lambda i, j: (0, 0))],
        out_shape=[jax.ShapeDtypeStruct((T, D), BF16),
                   jax.ShapeDtypeStruct((nj, T, tf), BF16), jax.ShapeDtypeStruct((nj, T, tf), BF16),
                   jax.ShapeDtypeStruct((nj, T, tf), BF16),
                   jax.ShapeDtypeStruct((T, D), F32),
                   jax.ShapeDtypeStruct((1, D), F32), jax.ShapeDtypeStruct((1, D), F32)],
        scratch_shapes=[pltpu.VMEM((tm, D), F32), pltpu.VMEM((tm, D), BF16)],
        vmem_mib=56, args=(dx2, r2, lng, gs, us, wfi, wfi, wfo))


def _loss_head(y, target, *, name):
    T, D = y.shape
    tm = _tile(T, 1024)

    def body(y_ref, t_ref, dy_ref, sq_ref):
        @pl.when(pl.program_id(0) == 0)
        def _():
            sq_ref[...] = jnp.zeros_like(sq_ref)
        err = y_ref[...] - t_ref[...]
        dy_ref[...] = err * (1.0 / D)
        sq_ref[...] += jnp.sum(err * err, axis=0, keepdims=True)

    return pl.pallas_call(
        body, name=name, grid=(T // tm,),
        in_specs=[pl.BlockSpec((tm, D), lambda i: (i, 0)), pl.BlockSpec((tm, D), lambda i: (i, 0))],
        out_specs=[pl.BlockSpec((tm, D), lambda i: (i, 0)), pl.BlockSpec((1, D), lambda i: (0, 0))],
        out_shape=[jax.ShapeDtypeStruct((T, D), F32), jax.ShapeDtypeStruct((1, D), F32)],
        compiler_params=_cparams(("arbitrary",)),
    )(y, target)


def _my_place():
    return lax.axis_index("x"), lax.axis_index("y"), lax.axis_index("c")


def _peer(place, k):
    x, y, c = place
    return (1 - x if k & 4 else x, 1 - y if k & 2 else y, 1 - c if k & 1 else c)


def _logical(place):
    x, y, c = place
    return 4 * x + 2 * y + c


def _block_of(ref, mode, idx):
    if mode == "blk":
        return ref.at[idx]
    if mode == "col":
        size = ref.shape[2] // N_DEV
        return ref.at[:, :, pl.ds(pl.multiple_of(idx * size, size), size)]
    size = ref.shape[1] // N_DEV
    return ref.at[:, pl.ds(pl.multiple_of(idx * size, size), size), :]


def _full_shape(shard, mode):
    if mode == "blk":
        return (N_DEV,) + shard.shape
    if mode == "col":
        return shard.shape[:2] + (N_DEV * shard.shape[2],)
    return (shard.shape[0], N_DEV * shard.shape[1], shard.shape[2])


class _Exchange:
    def __init__(self, arrays, out_shape, build):
        self.arrays = list(arrays)
        self.out_shape = list(out_shape)
        self.build = build

    def scratch(self):
        n = len(self.arrays)
        return [pltpu.SemaphoreType.DMA((n * N_DEV,)), pltpu.SemaphoreType.DMA((n * N_DEV,)),
                pltpu.SemaphoreType.DMA((n,))]

    def start(self, ins, outs, sems):
        for cp in self.build(ins, outs, *sems):
            cp.start()

    def wait(self, ins, outs, sems):
        for cp in self.build(ins, outs, *sems):
            cp.wait()

    def run(self, name):
        n_in, n_out = len(self.arrays), len(self.out_shape)
        hbm = pl.BlockSpec(memory_space=pltpu.HBM)

        def body(*refs):
            ins, outs, sems = refs[:n_in], refs[n_in:n_in + n_out], refs[n_in + n_out:]
            self.start(ins, outs, sems)
            self.wait(ins, outs, sems)

        return pl.pallas_call(
            body, name=name, in_specs=[hbm] * n_in, out_specs=[hbm] * n_out,
            out_shape=self.out_shape, scratch_shapes=self.scratch(),
        )(*self.arrays)


def _copies_to_all(src_of, dst_of, n, send, recv, local):
    me = _my_place()
    copies = []
    for a in range(n):
        copies.append(pltpu.make_async_copy(src_of(a, _logical(me)), dst_of(a), local.at[a]))
        for k in range(1, N_DEV):
            peer = _peer(me, k)
            copies.append(pltpu.make_async_remote_copy(
                src_ref=src_of(a, _logical(peer)), dst_ref=dst_of(a),
                send_sem=send.at[a * N_DEV + k], recv_sem=recv.at[a * N_DEV + k],
                device_id=peer, device_id_type=MESH))
    return copies


def _gather_job(shards, modes):
    def build(ins, outs, send, recv, local):
        my_id = _logical(_my_place())
        return _copies_to_all(lambda a, dev: ins[a], lambda a: _block_of(outs[a], modes[a], my_id),
                              len(shards), send, recv, local)

    return _Exchange(shards, [jax.ShapeDtypeStruct(_full_shape(s, m), s.dtype) for s, m in zip(shards, modes)], build)


def _grad_block(ref, mode, idx):
    if mode == "blk":
        return ref.at[idx]
    if mode == "col":
        size = ref.shape[1] // N_DEV
        return ref.at[:, pl.ds(pl.multiple_of(idx * size, size), size)]
    size = ref.shape[0] // N_DEV
    return ref.at[pl.ds(pl.multiple_of(idx * size, size), size), :]


def _grad_shard_shape(g, mode):
    if mode == "blk":
        return g.shape[1:]
    if mode == "col":
        return (g.shape[0], g.shape[1] // N_DEV)
    return (g.shape[0] // N_DEV, g.shape[1])


def _grads_job(groups, modes):
    flat = [(g, w, l) for w, per_w in enumerate(groups) for l, g in enumerate(per_w)]

    def build(ins, outs, send, recv, local):
        my_id = _logical(_my_place())
        return _copies_to_all(lambda a, dev: _grad_block(ins[a], modes[flat[a][1]], dev),
                              lambda a: outs[flat[a][1]].at[my_id, flat[a][2]],
                              len(flat), send, recv, local)

    out_shape = [jax.ShapeDtypeStruct((N_DEV, len(per_w)) + _grad_shard_shape(per_w[0], m), per_w[0].dtype)
                 for per_w, m in zip(groups, modes)]
    return _Exchange([g for g, _, _ in flat], out_shape, build)


def _adamw(w, g, m, v):
    m = ADAM_B1 * m + (1.0 - ADAM_B1) * g
    v = ADAM_B2 * v + (1.0 - ADAM_B2) * (g * g)
    m_hat = m / (1.0 - ADAM_B1 ** ADAM_STEP)
    v_hat = v / (1.0 - ADAM_B2 ** ADAM_STEP)
    delta = -ADAM_LR * (m_hat / (jnp.sqrt(v_hat) + ADAM_EPS) + ADAM_WD * w)
    return delta, m, v


def _sum_slots_adamw(slots, w, m, v, *, name):
    R, C = w.shape
    tr = _tile(R, 256)

    def body(s_ref, w_ref, m_ref, v_ref, g_out, d_out, m_out, v_out):
        g = s_ref[0].astype(F32)
        for s in range(1, N_DEV):
            g = g + s_ref[s].astype(F32)
        delta, m_new, v_new = _adamw(w_ref[...], g, m_ref[...], v_ref[...])
        g_out[...] = g
        d_out[...] = delta
        m_out[...] = m_new
        v_out[...] = v_new

    spec = pl.BlockSpec((tr, C), lambda i: (i, 0))
    return pl.pallas_call(
        body, name=name, grid=(R // tr,),
        in_specs=[pl.BlockSpec((N_DEV, tr, C), lambda i: (0, i, 0)), spec, spec, spec],
        out_specs=[spec] * 4,
        out_shape=[jax.ShapeDtypeStruct((R, C), F32)] * 4,
        compiler_params=_cparams(("parallel",)),
    )(slots, w, m, v)


def _small_allreduce_adamw(g, w, m, v, *, name):
    R = g.shape[0]
    vmem = pl.BlockSpec(memory_space=pltpu.VMEM)

    def body(g_ref, w_ref, m_ref, v_ref, g_out, d_out, m_out, v_out, slots, send, recv):
        me = _my_place()
        my_id = _logical(me)
        slots[my_id] = g_ref[...]
        copies = []
        for k in range(1, N_DEV):
            cp = pltpu.make_async_remote_copy(
                src_ref=g_ref, dst_ref=slots.at[my_id], send_sem=send.at[k], recv_sem=recv.at[k],
                device_id=_peer(me, k), device_id_type=MESH)
            cp.start()
            copies.append(cp)
        for cp in copies:
            cp.wait()
        total = slots[0]
        for s in range(1, N_DEV):
            total = total + slots[s]
        delta, m_new, v_new = _adamw(w_ref[...], total, m_ref[...], v_ref[...])
        g_out[...] = total
        d_out[...] = delta
        m_out[...] = m_new
        v_out[...] = v_new

    return pl.pallas_call(
        body, name=name,
        in_specs=[vmem] * 4, out_specs=[vmem] * 4,
        out_shape=[jax.ShapeDtypeStruct((R, LANES), F32)] * 4,
        scratch_shapes=[pltpu.VMEM((N_DEV, R, LANES), F32),
                        pltpu.SemaphoreType.DMA((N_DEV,)), pltpu.SemaphoreType.DMA((N_DEV,))],
    )(g, w, m, v)


def _pack(parts):
    flat = jnp.concatenate([p.reshape(-1) for p in parts])
    rows = -(-flat.shape[0] // (8 * LANES)) * 8
    return jnp.pad(flat, (0, rows * LANES - flat.shape[0])).reshape(rows, LANES)


def _unpack(packed, like):
    flat = packed.reshape(-1)
    out, pos = [], 0
    for p in like:
        out.append(flat[pos:pos + p.size].reshape(p.shape))
        pos += p.size
    return out


def kernel(x, w_in, b_gate, rel_bias, w_proj_a, w_proj_b, w_out, ln1_g, ln1_b, w_ffn_in, w_ffn_out, ln2_g, ln2_b, loss_target, m_w_in, m_b_gate, m_rel_bias, m_w_proj_a, m_w_proj_b, m_w_out, m_ln1_g, m_ln1_b, m_w_ffn_in, m_w_ffn_out, m_ln2_g, m_ln2_b, v_w_in, v_b_gate, v_rel_bias, v_w_proj_a, v_w_proj_b, v_w_out, v_ln1_g, v_ln1_b, v_w_ffn_in, v_w_ffn_out, v_ln2_g, v_ln2_b):
    L = w_in.shape[0]
    T, D = x.shape[1], x.shape[2]
    alpha = float((2 * L) ** 0.25)
    n_qkv = 6 * WIDTH

    big = [w_in, w_proj_a, w_proj_b, w_out, w_ffn_in, w_ffn_out]
    kinds = ["in", "pa", "pb", "o", "fi", "fo"]
    modes = ["col", "col", "col", "row", "blk", "row"]
    mode_of = dict(zip(kinds, modes))
    w_bf = dict(zip(kinds, [w.astype(BF16) for w in big]))

    def gather_of(ks, l):
        return _gather_job([w_bf[k][l:l + 1] for k in ks], [mode_of[k] for k in ks])

    W = [dict() for _ in range(L)]
    (W[0]["in"],) = gather_of(["in"], 0).run("gather_w_in_first")
    vec3 = lambda a: a[:, None, :]
    bg3, l1g, l1b, l2g, l2b = vec3(b_gate), vec3(ln1_g), vec3(ln1_b), vec3(ln2_g), vec3(ln2_b)
    b_col0 = 3 * WIDTH // LANES

    h = x[0]
    saved = []
    for l in range(L):
        ahead = l + 1 < L
        soon = ["pa", "pb", "o", "fo"]
        (qkv, gates), got = _in_proj(h, W[l]["in"], 0, n_qkv=n_qkv, name=f"in_proj_{l}",
                                     job=gather_of(soon, 0) if l == 0 else None)
        W[l].update(zip(soon, got))
        kvpad = jnp.pad(qkv[:, WIDTH:3 * WIDTH], ((A_WIN - A_TQ, 0), (0, 0)))
        bias = _toeplitz_bias(rel_bias[l])
        oa, got = _attn_a_fwd(qkv, kvpad, bias, name=f"attn_a_fwd_{l}", job=gather_of(["fi"], 0) if l == 0 else None)
        W[l].update(zip(["fi"], got))
        early = ["in", "pa", "pb", "o"]
        ob, got = _attn_b_fwd(qkv, col0=b_col0, name=f"attn_b_fwd_{l}", job=gather_of(early, l + 1) if ahead else None)
        W[l + 1 if ahead else l].update(zip(early, got))
        x1, r1 = _proj_fwd(oa, ob, gates, bg3, W[l]["pa"], W[l]["pb"], W[l]["o"], 0, h, l1g, l1b, l,
                           alpha=alpha, name=f"proj_fwd_{l}")
        (gs, us, r2, x2), got = _ffn_fwd(x1, W[l]["fi"], W[l]["fo"], 0, l2g, l2b, l, alpha=alpha, name=f"ffn_fwd_{l}",
                                         job=gather_of(["fi", "fo"], l + 1) if ahead else None)
        W[l + 1 if ahead else l].update(zip(["fi", "fo"], got))
        saved.append((h, qkv, gates, kvpad, bias, oa, ob, x1, r1, gs, us, r2))
        h = x2

    d_h, sq = _loss_head(h, loss_target[0], name="loss_head")
    loss = lax.psum((0.5 / D) * jnp.sum(sq), ("x", "y", "c"))

    g_bg, g_rb, g_l1g, g_l1b, g_l2g, g_l2b = ([None] * L for _ in range(6))
    slot = {k: [None] * L for k in kinds}

    def exchange_of(ks, grads):
        return _grads_job([[g] for g in grads], [mode_of[k] for k in ks])

    w_in_above = None
    for l in reversed(range(L)):
        xin, qkv, gates, kvpad, bias, oa, ob, x1, r1, gs, us, r2 = saved[l]
        (dr2, act, dgt, dup, dx1, g_l2g[l], g_l2b[l]), got = _ffn_bwd(
            d_h, r2, l2g, gs, us, W[l]["fi"], W[l]["fo"], 0, l, alpha=alpha, name=f"ffn_bwd_{l}", job=w_in_above)
        if w_in_above is not None:
            (slot["in"][l + 1],) = got
        g_fo = _mm_tn_blocked_a(act, dr2, name=f"grad_w_ffn_out_{l}").reshape(-1, D)
        g_fi = jnp.concatenate([_mm_tn_blocked(x1, dgt, name=f"grad_w_ffn_gate_{l}"),
                                _mm_tn_blocked(x1, dup, name=f"grad_w_ffn_up_{l}")], axis=0)
        (dr1, mixin, dya, dyb, dgates, doa, dob, g_l1g[l], g_l1b[l], g_bg[l]) = _proj_bwd(
            dx1, r1, l1g, oa, ob, gates, bg3, W[l]["pa"], W[l]["pb"], W[l]["o"], 0, l, name=f"proj_bwd_{l}")
        g_o = _mm_tn(mixin, dr1, tm=_tile(D, 1024), tn=_tile(D, 1024), name=f"grad_w_out_{l}")
        g_pa = _mm_tn(oa, dya, tm=WIDTH, tn=_tile(D, 1024), name=f"grad_w_proj_a_{l}")
        g_pb = _mm_tn(ob, dyb, tm=WIDTH, tn=_tile(D, 1024), name=f"grad_w_proj_b_{l}")
        (dqa, dka, dva, dbias), (slot["fi"][l], slot["fo"][l]) = _attn_a_bwd(
            qkv, kvpad, bias, doa, name=f"attn_a_bwd_{l}", job=exchange_of(["fi", "fo"], [g_fi, g_fo]))
        g_rb[l] = _toeplitz_bias_grad(dbias)
        (dqb, dkb, dvb), (slot["pa"][l], slot["pb"][l], slot["o"][l]) = _attn_b_bwd(
            qkv, ob, dob, col0=b_col0, name=f"attn_b_bwd_{l}", job=exchange_of(["pa", "pb", "o"], [g_pa, g_pb, g_o]))
        pad = A_WIN - A_TQ
        d_pre = jnp.concatenate([dqa, dka[pad:].astype(BF16), dva[pad:].astype(BF16),
                                 dqb, dkb.astype(BF16), dvb.astype(BF16), dgates], axis=1)
        g_in = _mm_tn(xin, d_pre, tm=D, tn=4 * w_in.shape[2], name=f"grad_w_in_{l}")
        w_in_above = exchange_of(["in"], [g_in])
        d_h, got = _mm_nt_add(d_pre, W[l]["in"], 0, dr1, alpha, name=f"grad_x_{l}", job=w_in_above if l == 0 else None)
        if l == 0:
            (slot["in"][0],) = got
    grad_x = d_h[None]

    slots = [jnp.concatenate(slot[k], axis=1) if L > 1 else slot[k][0] for k in kinds]
    moments_m = [m_w_in, m_w_proj_a, m_w_proj_b, m_w_out, m_w_ffn_in, m_w_ffn_out]
    moments_v = [v_w_in, v_w_proj_a, v_w_proj_b, v_w_out, v_w_ffn_in, v_w_ffn_out]
    names = ["w_in", "w_proj_a", "w_proj_b", "w_out", "w_ffn_in", "w_ffn_out"]
    big_out = {}
    for nm, s, w, m, v in zip(names, slots, big, moments_m, moments_v):
        two = lambda a: a.reshape(-1, a.shape[-1])
        res = _sum_slots_adamw(s.reshape(N_DEV, -1, s.shape[-1]), two(w), two(m), two(v), name=f"adamw_{nm}")
        big_out[nm] = [r.reshape(w.shape) for r in res]

    small_w = [b_gate, rel_bias, ln1_g, ln1_b, ln2_g, ln2_b]
    small_g = [jnp.stack(g) for g in (g_bg, g_rb, g_l1g, g_l1b, g_l2g, g_l2b)]
    small_m = [m_b_gate, m_rel_bias, m_ln1_g, m_ln1_b, m_ln2_g, m_ln2_b]
    small_v = [v_b_gate, v_rel_bias, v_ln1_g, v_ln1_b, v_ln2_g, v_ln2_b]
    res = _small_allreduce_adamw(_pack(small_g), _pack(small_w), _pack(small_m), _pack(small_v),
                                 name="allreduce_small_adamw")
    small_names = ["b_gate", "rel_bias", "ln1_g", "ln1_b", "ln2_g", "ln2_b"]
    small_out = {nm: [] for nm in small_names}
    for packed in res:
        for nm, arr in zip(small_names, _unpack(packed, small_w)):
            small_out[nm].append(arr)

    order = ["w_in", "b_gate", "rel_bias", "w_proj_a", "w_proj_b", "w_out", "ln1_g", "ln1_b",
             "w_ffn_in", "w_ffn_out", "ln2_g", "ln2_b"]
    every = {**big_out, **small_out}
    outs = [loss, grad_x]
    for kind in range(4):
        outs += [every[nm][kind] for nm in order]
    return tuple(outs)
```

```python
import functools
import math

import jax
import jax.numpy as jnp
import numpy as np
from jax import lax
from jax.experimental import pallas as pl
from jax.experimental.pallas import tpu as pltpu

F32 = jnp.float32
BF16 = jnp.bfloat16

HEAD_DIM = 64
CHUNK = 64
LEFT_CHUNKS = 8
REL_CLIP = 256
N_REL = 2 * REL_CLIP + 1
WIDTH = 512
LANES = 128
A_TQ = 256
A_WIN = A_TQ + LEFT_CHUNKS * CHUNK
A_STRIP = 128
B_TQ = 512
B_TS = 256
B_PIECE = 64
B_DEAD = -160.0
LN_EPS = 1e-5
QK_SCALE = 1.0 / math.sqrt(HEAD_DIM)
LOG2E = 1.4426950408889634
NEG = -1e30

ADAM_LR = 0.001
ADAM_B1 = 0.9
ADAM_B2 = 0.999
ADAM_EPS = 1e-08
ADAM_WD = 0.01
ADAM_STEP = 10

N_DEV = 8
MESH = pl.DeviceIdType.MESH
MIB = 1024 * 1024


def _cparams(sem=None, vmem_mib=48):
    return pltpu.CompilerParams(dimension_semantics=sem, vmem_limit_bytes=vmem_mib * MIB)


def _dot(a, b):
    return jnp.dot(a, b, preferred_element_type=F32)


def _dot_nt(a, b):
    return lax.dot_general(a, b, (((1,), (1,)), ((), ())), preferred_element_type=F32)


def _dot_tn(a, b):
    return lax.dot_general(a, b, (((0,), (0,)), ((), ())), preferred_element_type=F32)


def _tile(n, pref):
    if n <= pref:
        return n
    for t in range(pref - pref % 8, 0, -8):
        if n % t == 0:
            return t
    raise ValueError((n, pref))


def _in_proj(a, w, layer, *, n_qkv, name, job=None):
    M, K = a.shape
    N = w.shape[2]
    tm = _tile(M, 1024)
    tn = 512
    assert n_qkv % tn == 0 and (N - n_qkv) % tn == 0
    n_q = n_qkv // tn

    def body(a_ref, w_ref, q_ref, g_ref, ab_ref):
        j = pl.program_id(1)

        @pl.when(j == 0)
        def _():
            ab_ref[...] = a_ref[...].astype(BF16)

        res = _dot(ab_ref[...], w_ref[...])

        @pl.when(j < n_q)
        def _():
            q_ref[...] = res.astype(BF16)

        @pl.when(j >= n_q)
        def _():
            g_ref[...] = res

    return _call_carrying(
        job, body, name=name, grid=(M // tm, N // tn),
        in_specs=[pl.BlockSpec((tm, K), lambda i, j: (i, 0)),
                  pl.BlockSpec((None, K, tn), lambda i, j: (layer, 0, j))],
        out_specs=[pl.BlockSpec((tm, tn), lambda i, j: (i, jnp.minimum(j, n_q - 1))),
                   pl.BlockSpec((tm, tn), lambda i, j: (i, jnp.maximum(j - n_q, 0)))],
        out_shape=[jax.ShapeDtypeStruct((M, n_qkv), BF16), jax.ShapeDtypeStruct((M, N - n_qkv), F32)],
        scratch_shapes=[pltpu.VMEM((tm, K), BF16)], vmem_mib=48, args=(a, w))


def _mm_nt_add(a, w, layer, add, add_scale, *, name, job=None):
    M, K = a.shape
    N = w.shape[1]
    tm = _tile(M, 1024)
    tk = _tile(K, 1024)

    def body(a_ref, w_ref, add_ref, o_ref):
        @pl.when(pl.program_id(1) == 0)
        def _():
            o_ref[...] = add_scale * add_ref[...]
        o_ref[...] += _dot_nt(a_ref[...], w_ref[...])

    (out,), rode = _call_carrying(
        job, body, name=name, grid=(M // tm, K // tk),
        in_specs=[pl.BlockSpec((tm, tk), lambda i, k: (i, k)),
                  pl.BlockSpec((None, N, tk), lambda i, k: (layer, 0, k)),
                  pl.BlockSpec((tm, N), lambda i, k: (i, 0))],
        out_specs=[pl.BlockSpec((tm, N), lambda i, k: (i, 0))],
        out_shape=[jax.ShapeDtypeStruct((M, N), F32)],
        scratch_shapes=[], vmem_mib=48, args=(a, w, add))
    return out, rode


def _tn_body(k_axis, n_k):
    def body(a_ref, b_ref, o_ref, acc_ref):
        k = pl.program_id(k_axis)

        @pl.when(k == 0)
        def _():
            acc_ref[...] = jnp.zeros_like(acc_ref)
        acc_ref[...] += _dot_tn(a_ref[...].astype(BF16), b_ref[...].astype(BF16))

        @pl.when(k == n_k - 1)
        def _():
            o_ref[...] = acc_ref[...].astype(o_ref.dtype)
    return body


def _mm_tn(a, b, *, tm, tn, name):
    T, M = a.shape
    N = b.shape[1]
    tk = _tile(T, 512)
    return pl.pallas_call(
        _tn_body(2, T // tk), name=name, grid=(M // tm, N // tn, T // tk),
        in_specs=[pl.BlockSpec((tk, tm), lambda i, j, k: (k, i)),
                  pl.BlockSpec((tk, tn), lambda i, j, k: (k, j))],
        out_specs=pl.BlockSpec((tm, tn), lambda i, j, k: (i, j)),
        out_shape=jax.ShapeDtypeStruct((M, N), BF16),
        scratch_shapes=[pltpu.VMEM((tm, tn), F32)],
        compiler_params=_cparams(("parallel", "parallel", "arbitrary")),
    )(a, b)


def _mm_tn_blocked_pair(a, b1, b2, *, name):
    T, M = a.shape
    S, _, N = b1.shape
    tk = _tile(T, 1024)
    n_k = T // tk

    def body(a_ref, b1_ref, b2_ref, o1_ref, o2_ref, acc1_ref, acc2_ref):
        k = pl.program_id(1)

        @pl.when(k == 0)
        def _():
            acc1_ref[...] = jnp.zeros_like(acc1_ref)
            acc2_ref[...] = jnp.zeros_like(acc2_ref)

        a_t = a_ref[...].astype(BF16)
        acc1_ref[...] += _dot_tn(a_t, b1_ref[...])
        acc2_ref[...] += _dot_tn(a_t, b2_ref[...])

        @pl.when(k == n_k - 1)
        def _():
            o1_ref[...] = acc1_ref[...].astype(o1_ref.dtype)
            o2_ref[...] = acc2_ref[...].astype(o2_ref.dtype)

    blk = lambda: pl.BlockSpec((None, tk, N), lambda s, k: (s, k, 0))
    out = lambda: pl.BlockSpec((None, M, N), lambda s, k: (s, 0, 0))
    return pl.pallas_call(
        body, name=name, grid=(S, n_k),
        in_specs=[pl.BlockSpec((tk, M), lambda s, k: (k, 0)), blk(), blk()],
        out_specs=[out(), out()],
        out_shape=[jax.ShapeDtypeStruct((S, M, N), BF16)] * 2,
        scratch_shapes=[pltpu.VMEM((M, N), F32), pltpu.VMEM((M, N), F32)],
        compiler_params=_cparams(("parallel", "arbitrary")),
    )(a, b1, b2)


def _mm_tn_blocked_a(a, b, *, name):
    S, T, M = a.shape
    N = b.shape[1]
    tk = _tile(T, 512)
    return pl.pallas_call(
        _tn_body(1, T // tk), name=name, grid=(S, T // tk),
        in_specs=[pl.BlockSpec((None, tk, M), lambda s, k: (s, k, 0)),
                  pl.BlockSpec((tk, N), lambda s, k: (k, 0))],
        out_specs=pl.BlockSpec((None, M, N), lambda s, k: (s, 0, 0)),
        out_shape=jax.ShapeDtypeStruct((S, M, N), BF16),
        scratch_shapes=[pltpu.VMEM((M, N), F32)],
        compiler_params=_cparams(("parallel", "arbitrary")),
    )(a, b)


def _ln_fwd(r, g, b):
    mu = jnp.mean(r, axis=-1, keepdims=True)
    xc = r - mu
    var = jnp.mean(xc * xc, axis=-1, keepdims=True)
    return xc * lax.rsqrt(var + LN_EPS) * g + b


def _ln_bwd(dy, r, g):
    mu = jnp.mean(r, axis=-1, keepdims=True)
    xc = r - mu
    var = jnp.mean(xc * xc, axis=-1, keepdims=True)
    rstd = lax.rsqrt(var + LN_EPS)
    xhat = xc * rstd
    dxh = dy * g
    m1 = jnp.mean(dxh, axis=-1, keepdims=True)
    m2 = jnp.mean(dxh * xhat, axis=-1, keepdims=True)
    return rstd * (dxh - m1 - xhat * m2), xhat


def _lane_is_head0():
    return lax.broadcasted_iota(jnp.int32, (1, LANES), 1) < HEAD_DIM


def _band_shape():
    a = np.arange(A_TQ)[:, None] // CHUNK
    b = np.arange(A_WIN)[None, :] // CHUNK
    return (b >= a) & (b <= a + LEFT_CHUNKS)


def _band_streams(strip):
    return [dict(h=h, n=strip, rows=pl.ds(r0, strip)) for h in range(2) for r0 in range(0, A_TQ, strip)]


def _band_scores(st, i, qh_ref, k2, bias_ref):
    s = _dot_nt(qh_ref[st["h"], st["rows"], :], k2) * QK_SCALE + bias_ref[st["h"], st["rows"], :]
    c = lax.broadcasted_iota(jnp.int32, (st["n"], A_WIN), 1)
    st["s"] = jnp.where(c >= LEFT_CHUNKS * CHUNK - i * A_TQ, s, NEG)


def _band_softmax(st):
    s = st.pop("s")
    e = jnp.exp(s - jnp.max(s, axis=1, keepdims=True))
    st["p"] = e / jnp.sum(e, axis=1, keepdims=True)


def _attn_a_fwd(qkv, kvpad, bias, *, name, job=None):
    T = qkv.shape[0]
    n_hp = WIDTH // LANES

    def body(q_ref, k_ref, v_ref, bias_ref, o_ref, qh_ref, acc_ref):
        i = pl.program_id(1)
        row0 = pl.multiple_of(i * A_TQ, A_TQ)
        q2 = q_ref[...]
        k2 = k_ref[pl.ds(row0, A_WIN), :]
        v2 = v_ref[pl.ds(row0, A_WIN), :]
        head0 = _lane_is_head0()
        qh_ref[0] = jnp.where(head0, q2, jnp.zeros_like(q2))
        qh_ref[1] = jnp.where(head0, jnp.zeros_like(q2), q2)

        def scores(st):
            _band_scores(st, i, qh_ref, k2, bias_ref)

        def values(st):
            acc_ref[st["h"], st["rows"], :] = _dot(st.pop("p").astype(BF16), v2)

        _skewed(_band_streams(A_STRIP), [scores, _band_softmax, values])
        o_ref[...] = jnp.where(head0, acc_ref[0], acc_ref[1]).astype(o_ref.dtype)

    (out,), rode = _call_carrying(
        job, body, name=name, grid=(n_hp, T // A_TQ),
        in_specs=[pl.BlockSpec((A_TQ, LANES), lambda hp, i: (i, hp)),
                  pl.BlockSpec((T + A_WIN - A_TQ, LANES), lambda hp, i: (0, hp)),
                  pl.BlockSpec((T + A_WIN - A_TQ, LANES), lambda hp, i: (0, hp + n_hp)),
                  pl.BlockSpec((2, A_TQ, A_WIN), lambda hp, i: (hp, 0, 0))],
        out_specs=[pl.BlockSpec((A_TQ, LANES), lambda hp, i: (i, hp))],
        out_shape=[jax.ShapeDtypeStruct((T, WIDTH), BF16)],
        scratch_shapes=[pltpu.VMEM((2, A_TQ, LANES), BF16), pltpu.VMEM((2, A_TQ, LANES), F32)],
        vmem_mib=48, args=(qkv, kvpad, kvpad, bias))
    return out, rode


def _attn_a_bwd(qkv, kvpad, bias, do, *, name, job=None):
    T = qkv.shape[0]
    TP = T + A_WIN - A_TQ
    n_hp = WIDTH // LANES

    def body(q_ref, k_ref, v_ref, bias_ref, do_ref, dq_ref, dk_ref, dv_ref, db_ref, qh_ref, doh_ref, dqa_ref):
        i = pl.program_id(1)

        @pl.when(i == 0)
        def _():
            dk_ref[...] = jnp.zeros_like(dk_ref)
            dv_ref[...] = jnp.zeros_like(dv_ref)
            db_ref[...] = jnp.zeros_like(db_ref)

        row0 = pl.multiple_of(i * A_TQ, A_TQ)
        window = pl.ds(row0, A_WIN)
        q2 = q_ref[...]
        do2 = do_ref[...]
        k2 = k_ref[window, :]
        v2 = v_ref[window, :]
        head0 = _lane_is_head0()
        zero = jnp.zeros_like(q2)
        qh_ref[0] = jnp.where(head0, q2, zero)
        qh_ref[1] = jnp.where(head0, zero, q2)
        doh_ref[0] = jnp.where(head0, do2, zero)
        doh_ref[1] = jnp.where(head0, zero, do2)
        dk = [[], []]
        dv = [[], []]

        def scores(st):
            _band_scores(st, i, qh_ref, k2, bias_ref)
            st["dp"] = _dot_nt(doh_ref[st["h"], st["rows"], :], v2)

        def dscores(st):
            _band_softmax(st)
            p, dp = st.pop("p"), st.pop("dp")
            ds = p * (dp - jnp.sum(p * dp, axis=1, keepdims=True))
            db_ref[st["h"], st["rows"], :] += ds
            st["dsb"] = (ds * QK_SCALE).astype(BF16)
            st["pb"] = p.astype(BF16)

        def grads(st):
            h, rows = st["h"], st["rows"]
            dsb = st.pop("dsb")
            dqa_ref[h, rows, :] = _dot(dsb, k2)
            dk[h].append(_dot_tn(dsb, q_ref[rows, :]))
            dv[h].append(_dot_tn(st.pop("pb"), do_ref[rows, :]))

        _skewed(_band_streams(A_TQ), [scores, dscores, grads])
        dq_ref[...] = jnp.where(head0, dqa_ref[0], dqa_ref[1]).astype(dq_ref.dtype)
        dk_ref[window, :] += jnp.where(head0, sum(dk[0]), sum(dk[1]))
        dv_ref[window, :] += jnp.where(head0, sum(dv[0]), sum(dv[1]))

    return _call_carrying(
        job, body, name=name, grid=(n_hp, T // A_TQ),
        in_specs=[pl.BlockSpec((A_TQ, LANES), lambda hp, i: (i, hp)),
                  pl.BlockSpec((TP, LANES), lambda hp, i: (0, hp)),
                  pl.BlockSpec((TP, LANES), lambda hp, i: (0, hp + n_hp)),
                  pl.BlockSpec((2, A_TQ, A_WIN), lambda hp, i: (hp, 0, 0)),
                  pl.BlockSpec((A_TQ, LANES), lambda hp, i: (i, hp))],
        out_specs=[pl.BlockSpec((A_TQ, LANES), lambda hp, i: (i, hp)),
                   pl.BlockSpec((TP, LANES), lambda hp, i: (0, hp)),
                   pl.BlockSpec((TP, LANES), lambda hp, i: (0, hp)),
                   pl.BlockSpec((2, A_TQ, A_WIN), lambda hp, i: (hp, 0, 0))],
        out_shape=[jax.ShapeDtypeStruct((T, WIDTH), BF16),
                   jax.ShapeDtypeStruct((TP, WIDTH), F32),
                   jax.ShapeDtypeStruct((TP, WIDTH), F32),
                   jax.ShapeDtypeStruct((WIDTH // HEAD_DIM, A_TQ, A_WIN), F32)],
        scratch_shapes=[pltpu.VMEM((2, A_TQ, LANES), BF16), pltpu.VMEM((2, A_TQ, LANES), BF16),
                        pltpu.VMEM((2, A_TQ, LANES), F32)],
        vmem_mib=56, args=(qkv, kvpad, kvpad, bias, do))


def _toeplitz_bias(rb):
    H = rb.shape[0]
    span = A_TQ + A_WIN - 1
    n_tail = span - (N_REL - 1)
    ext = jnp.concatenate([rb[:, 1:], jnp.broadcast_to(rb[:, N_REL - 1:], (H, n_tail))], axis=1)
    rev = jnp.pad(ext[:, ::-1], ((0, 0), (0, 1)))
    flat = jnp.broadcast_to(rev[:, None, :], (H, A_TQ, span + 1)).reshape(H, A_TQ * (span + 1))
    skew = flat[:, :A_TQ * span].reshape(H, A_TQ, span)
    return jnp.where(_band_shape()[None], skew[:, :, A_TQ - 1:A_TQ - 1 + A_WIN], NEG)


def _toeplitz_bias_grad(db):
    H = db.shape[0]
    span = A_TQ + A_WIN - 1
    d_skew = jnp.pad(db, ((0, 0), (0, 0), (A_TQ - 1, span - (A_TQ - 1) - A_WIN)))
    d_flat = jnp.pad(d_skew.reshape(H, A_TQ * span), ((0, 0), (0, A_TQ)))
    g_ext = jnp.sum(d_flat.reshape(H, A_TQ, span + 1), axis=1)[:, :span][:, ::-1]
    last = g_ext[:, N_REL - 2] + jnp.sum(g_ext[:, N_REL - 1:], axis=1)
    return jnp.concatenate([jnp.zeros((H, 1), F32), g_ext[:, :N_REL - 2], last[:, None]], axis=1)


def _split_bf16(x):
    hi = x.astype(BF16)
    lo = (x - hi.astype(F32)).astype(BF16)
    return hi, lo


def _sb_streams(d):
    out = []
    for h in range(2):
        for r in range(B_TQ // B_TS):
            if d is not None and d > r:
                continue
            out.append(dict(h=h, r=r, rows=pl.ds(r * B_TS, B_TS), diag=(d is not None and d == r)))
    return out


def _piece_rows(st, p):
    return pl.ds(st["r"] * B_TS + p, B_PIECE)


def _rows_cat(parts):
    return jnp.concatenate(parts, axis=0)


def _skewed(streams, stages):
    for t in range(len(streams) + len(stages) - 1):
        for s, st in enumerate(streams):
            if 0 <= t - s < len(stages):
                stages[t - s](st)


def _sb_logs(st, z2):
    log_beta, log_keep, keep_bf = [], [], []
    for p in range(0, B_TS, B_PIECE):
        z = z2[p:p + B_PIECE]
        lp2 = jnp.log(1.0 + jnp.exp2(-jnp.abs(z))) * LOG2E
        lb = jnp.minimum(z, 0.0) - lp2
        lk = lb - z
        if st["diag"]:
            lk = jnp.where(_strict_lower(p), lk, 0.0)
        log_beta.append(lb)
        log_keep.append(lk)
        keep_bf.append(lk.astype(BF16))
    st["log_beta"] = _rows_cat(log_beta)
    st["log_keep"] = _rows_cat(log_keep)
    st["keep_bf"] = _rows_cat(keep_bf)


def _strict_lower(p):
    t = p + lax.broadcasted_iota(jnp.int32, (B_PIECE, B_TS), 0)
    s = lax.broadcasted_iota(jnp.int32, (B_PIECE, B_TS), 1)
    return s < t


def _tri(strict):
    j = lax.broadcasted_iota(jnp.int32, (B_TS, B_TS), 0)
    s = lax.broadcasted_iota(jnp.int32, (B_TS, B_TS), 1)
    return jnp.where(j > s if strict else j >= s, 1.0, 0.0).astype(BF16)


def _call_carrying(job, body, *, name, grid, in_specs, out_specs, out_shape, scratch_shapes, vmem_mib, args):
    n_in, n_out, n_scr = len(in_specs), len(out_specs), len(scratch_shapes)
    if job is None:
        res = pl.pallas_call(body, name=name, grid=grid, in_specs=in_specs, out_specs=out_specs,
                             out_shape=out_shape, scratch_shapes=scratch_shapes,
                             compiler_params=_cparams(("arbitrary",) * len(grid), vmem_mib))(*args)
        return res, []
    j_in, j_out = len(job.arrays), len(job.out_shape)
    hbm = pl.BlockSpec(memory_space=pltpu.HBM)

    def carrying(*refs):
        refs = list(refs)
        ins, refs = refs[:n_in], refs[n_in:]
        j_ins, refs = refs[:j_in], refs[j_in:]
        outs, refs = refs[:n_out], refs[n_out:]
        j_outs, refs = refs[:j_out], refs[j_out:]
        scr, sems = refs[:n_scr], refs[n_scr:]
        first = functools.reduce(jnp.logical_and, [pl.program_id(d) == 0 for d in range(len(grid))])
        last = functools.reduce(jnp.logical_and, [pl.program_id(d) == grid[d] - 1 for d in range(len(grid))])

        @pl.when(first)
        def _():
            job.start(j_ins, j_outs, sems)

        body(*ins, *outs, *scr)

        @pl.when(last)
        def _():
            job.wait(j_ins, j_outs, sems)

    res = pl.pallas_call(
        carrying, name=name, grid=grid,
        in_specs=list(in_specs) + [hbm] * j_in, out_specs=list(out_specs) + [hbm] * j_out,
        out_shape=list(out_shape) + job.out_shape, scratch_shapes=list(scratch_shapes) + job.scratch(),
        compiler_params=_cparams(("arbitrary",) * len(grid), vmem_mib))(*args, *job.arrays)
    return res[:n_out], res[n_out:]


def _attn_b_fwd(qkv, *, col0, name, job=None):
    T = qkv.shape[0]
    n_hp = WIDTH // LANES
    sub = B_TQ // B_TS

    def body(q_ref, k_ref, v_ref, o_ref, acc_ref, car_ref, qh_ref):
        i = pl.program_id(1)
        q2 = q_ref[...]
        head0 = _lane_is_head0()
        qh_ref[0] = jnp.where(head0, q2, jnp.zeros_like(q2))
        qh_ref[1] = jnp.where(head0, jnp.zeros_like(q2), q2)
        tri_s = _tri(True)
        acc_ref[...] = jnp.zeros_like(acc_ref)
        car_ref[...] = jnp.zeros_like(car_ref)

        def tile(kb, d):
            k0 = pl.multiple_of(kb * B_TS, B_TS)
            k2 = k_ref[pl.ds(k0, B_TS), :]
            v2 = v_ref[pl.ds(k0, B_TS), :]

            def scores(st):
                st["z2"] = _dot_nt(qh_ref[st["h"], st["rows"], :], k2) * (QK_SCALE * LOG2E)

            def logs(st):
                _sb_logs(st, st.pop("z2"))

            def suffix(st):
                st["suffix"] = _dot(st.pop("keep_bf"), tri_s)

            def weights(st):
                log_beta, suffix, log_keep = st.pop("log_beta"), st.pop("suffix"), st.pop("log_keep")
                wb = []
                for p in range(0, B_TS, B_PIECE):
                    rows = _piece_rows(st, p)
                    car = car_ref[st["h"], rows, :]
                    w = jnp.exp2(log_beta[p:p + B_PIECE] + suffix[p:p + B_PIECE] + car)
                    if st["diag"]:
                        w = jnp.where(_strict_lower(p), w, 0.0)
                    wb.append(w.astype(BF16))
                    car_ref[st["h"], rows, :] = car + jnp.sum(log_keep[p:p + B_PIECE], axis=1, keepdims=True)
                st["wb"] = _rows_cat(wb)

            def values(st):
                acc_ref[st["h"], st["rows"], :] += _dot(st.pop("wb"), v2)

            _skewed(_sb_streams(d), [scores, logs, suffix, weights, values])

        for d in reversed(range(sub)):
            tile(i * sub + d, d)

        def alive(c):
            return (c[0] < i * sub) & (c[1] > B_DEAD)

        def step(c):
            tile(i * sub - 1 - c[0], None)
            return c[0] + 1, jnp.max(car_ref[...])

        lax.while_loop(alive, step, (jnp.int32(0), jnp.float32(0.0)))
        o_ref[...] = jnp.where(head0, acc_ref[0], acc_ref[1])

    (out,), rode = _call_carrying(
        job, body, name=name, grid=(n_hp, T // B_TQ),
        in_specs=[pl.BlockSpec((B_TQ, LANES), lambda hp, i: (i, hp + col0)),
                  pl.BlockSpec((T, LANES), lambda hp, i: (0, hp + col0 + n_hp)),
                  pl.BlockSpec((T, LANES), lambda hp, i: (0, hp + col0 + 2 * n_hp))],
        out_specs=[pl.BlockSpec((B_TQ, LANES), lambda hp, i: (i, hp))],
        out_shape=[jax.ShapeDtypeStruct((T, WIDTH), F32)],
        scratch_shapes=[pltpu.VMEM((2, B_TQ, LANES), F32), pltpu.VMEM((2, B_TQ, 1), F32),
                        pltpu.VMEM((2, B_TQ, LANES), BF16)],
        vmem_mib=48, args=(qkv, qkv, qkv))
    return out, rode


def _attn_b_bwd(qkv, out, do, *, col0, name, job=None):
    T = qkv.shape[0]
    n_hp = WIDTH // LANES
    sub = B_TQ // B_TS

    def body(q_ref, k_ref, v_ref, o_ref, do_ref, dq_ref, dk_ref, dv_ref,
             dqa_ref, car_ref, carr_ref, tot_ref, qh_ref, doh_ref, qs_ref):
        i = pl.program_id(1)

        @pl.when(i == 0)
        def _():
            dk_ref[...] = jnp.zeros_like(dk_ref)
            dv_ref[...] = jnp.zeros_like(dv_ref)

        q2 = q_ref[...]
        do2 = do_ref[...]
        head0 = _lane_is_head0()
        zero = jnp.zeros_like(q2)
        qh_ref[0] = jnp.where(head0, q2, zero)
        qh_ref[1] = jnp.where(head0, zero, q2)
        doh_ref[0] = jnp.where(head0, do2, zero)
        doh_ref[1] = jnp.where(head0, zero, do2)
        scale = jnp.asarray(QK_SCALE, BF16)
        qs_ref[...] = q2 * scale
        tri_s = _tri(True)
        tri_i = _tri(False)
        prod = do2.astype(F32) * o_ref[...]
        tot_ref[0] = jnp.sum(jnp.where(head0, prod, 0.0), axis=1, keepdims=True)
        tot_ref[1] = jnp.sum(jnp.where(head0, 0.0, prod), axis=1, keepdims=True)
        dqa_ref[...] = jnp.zeros_like(dqa_ref)
        car_ref[...] = jnp.zeros_like(car_ref)
        carr_ref[...] = jnp.zeros_like(carr_ref)

        def tile(kb, d):
            k0 = pl.multiple_of(kb * B_TS, B_TS)
            keys = pl.ds(k0, B_TS)
            k2 = k_ref[keys, :]
            v2 = v_ref[keys, :]
            k2s = k2 * scale

            def scores(st):
                st["z2"] = _dot_nt(qh_ref[st["h"], st["rows"], :], k2) * (QK_SCALE * LOG2E)
                st["dw"] = _dot_nt(doh_ref[st["h"], st["rows"], :], v2)

            def logs(st):
                _sb_logs(st, st.pop("z2"))

            def suffix(st):
                st["suffix"] = _dot(st.pop("keep_bf"), tri_s)

            def weights(st):
                h = st["h"]
                suffix, dw = st.pop("suffix"), st.pop("dw")
                wb, dlog, hi, lo = [], [], [], []
                for p in range(0, B_TS, B_PIECE):
                    rows = _piece_rows(st, p)
                    car = car_ref[h, rows, :]
                    w = jnp.exp2(st["log_beta"][p:p + B_PIECE] + suffix[p:p + B_PIECE] + car)
                    if st["diag"]:
                        w = jnp.where(_strict_lower(p), w, 0.0)
                    w = w.astype(BF16)
                    dl = w.astype(F32) * dw[p:p + B_PIECE]
                    dl_hi, dl_lo = _split_bf16(dl)
                    wb.append(w)
                    dlog.append(dl)
                    hi.append(dl_hi)
                    lo.append(dl_lo)
                    car_ref[h, rows, :] = car + jnp.sum(st["log_keep"][p:p + B_PIECE], axis=1, keepdims=True)
                st["wb"], st["dlog"], st["hi"], st["lo"] = _rows_cat(wb), _rows_cat(dlog), _rows_cat(hi), _rows_cat(lo)

            def later(st):
                st["later"] = _dot(st.pop("hi"), tri_i) + _dot(st.pop("lo"), tri_i)

            def dscores(st):
                h = st["h"]
                later, dlog = st.pop("later"), st.pop("dlog")
                log_keep, log_beta = st.pop("log_keep"), st.pop("log_beta")
                dzb = []
                for p in range(0, B_TS, B_PIECE):
                    rows = _piece_rows(st, p)
                    pc = slice(p, p + B_PIECE)
                    carr = carr_ref[h, rows, :]
                    earlier = tot_ref[h, rows, :] - (later[pc] + carr)
                    dz = dlog[pc] * jnp.exp2(log_keep[pc]) - jnp.exp2(log_beta[pc]) * earlier
                    if st["diag"]:
                        dz = jnp.where(_strict_lower(p), dz, 0.0)
                    dzb.append(dz.astype(BF16))
                    carr_ref[h, rows, :] = carr + jnp.sum(dlog[pc], axis=1, keepdims=True)
                st["dzb"] = _rows_cat(dzb)

            def grads(st):
                h, rows = st["h"], st["rows"]
                mine = head0 if h == 0 else jnp.logical_not(head0)
                dzb = st.pop("dzb")
                dqa_ref[h, rows, :] += _dot(dzb, k2s)
                dk_ref[keys, :] += jnp.where(mine, _dot_tn(dzb, qs_ref[rows, :]), 0.0)
                dv_ref[keys, :] += jnp.where(mine, _dot_tn(st.pop("wb"), do_ref[rows, :]), 0.0)

            _skewed(_sb_streams(d), [scores, logs, suffix, weights, later, dscores, grads])

        for d in reversed(range(sub)):
            tile(i * sub + d, d)

        def alive(c):
            return (c[0] < i * sub) & (c[1] > B_DEAD)

        def step(c):
            tile(i * sub - 1 - c[0], None)
            return c[0] + 1, jnp.max(car_ref[...])

        lax.while_loop(alive, step, (jnp.int32(0), jnp.float32(0.0)))
        dq_ref[...] = jnp.where(head0, dqa_ref[0], dqa_ref[1]).astype(dq_ref.dtype)

    return _call_carrying(
        job, body, name=name, grid=(n_hp, T // B_TQ),
        in_specs=[pl.BlockSpec((B_TQ, LANES), lambda hp, i: (i, hp + col0)),
                  pl.BlockSpec((T, LANES), lambda hp, i: (0, hp + col0 + n_hp)),
                  pl.BlockSpec((T, LANES), lambda hp, i: (0, hp + col0 + 2 * n_hp)),
                  pl.BlockSpec((B_TQ, LANES), lambda hp, i: (i, hp)),
                  pl.BlockSpec((B_TQ, LANES), lambda hp, i: (i, hp))],
        out_specs=[pl.BlockSpec((B_TQ, LANES), lambda hp, i: (i, hp)),
                   pl.BlockSpec((T, LANES), lambda hp, i: (0, hp)),
                   pl.BlockSpec((T, LANES), lambda hp, i: (0, hp))],
        out_shape=[jax.ShapeDtypeStruct((T, WIDTH), BF16),
                   jax.ShapeDtypeStruct((T, WIDTH), F32),
                   jax.ShapeDtypeStruct((T, WIDTH), F32)],
        scratch_shapes=[pltpu.VMEM((2, B_TQ, LANES), F32), pltpu.VMEM((2, B_TQ, 1), F32),
                        pltpu.VMEM((2, B_TQ, 1), F32), pltpu.VMEM((2, B_TQ, 1), F32),
                        pltpu.VMEM((2, B_TQ, LANES), BF16), pltpu.VMEM((2, B_TQ, LANES), BF16),
                        pltpu.VMEM((B_TQ, LANES), BF16)],
        vmem_mib=56, args=(qkv, qkv, qkv, out, do))


def _gated_mix(oa_ref, ob_ref, g_ref, bg_ref, wpa_ref, wpb_ref, D):
    ya = _dot(oa_ref[...].astype(BF16), wpa_ref[...])
    yb = _dot(ob_ref[...].astype(BF16), wpb_ref[...])
    sa = jax.nn.sigmoid(g_ref[:, :D] + bg_ref[:, :D])
    sb = jax.nn.sigmoid(g_ref[:, D:] + bg_ref[:, D:])
    return ya, yb, sa, sb


def _proj_fwd(oa, ob, g, bg, wpa, wpb, wo, wl, xin, lng, lnb, layer, *, alpha, name):
    T, D = xin.shape
    tm = _tile(T, 512)
    row = lambda i: (i, 0)
    wspec = lambda r, c: pl.BlockSpec((None, r, c), lambda i: (wl, 0, 0))
    vec = lambda c: pl.BlockSpec((None, 1, c), lambda i: (layer, 0, 0))

    def body(oa_ref, ob_ref, g_ref, bg_ref, wpa_ref, wpb_ref, wo_ref, x_ref, lg_ref, lb_ref, x1_ref, r1_ref, x1b_ref):
        ya, yb, sa, sb = _gated_mix(oa_ref, ob_ref, g_ref, bg_ref, wpa_ref, wpb_ref, D)
        mix = _dot((sa * ya + sb * yb).astype(BF16), wo_ref[...])
        r1 = alpha * x_ref[...] + mix
        r1_ref[...] = r1
        x1 = _ln_fwd(r1, lg_ref[...], lb_ref[...])
        x1_ref[...] = x1
        x1b_ref[...] = x1.astype(BF16)

    return pl.pallas_call(
        body, name=name, grid=(T // tm,),
        in_specs=[pl.BlockSpec((tm, WIDTH), row), pl.BlockSpec((tm, WIDTH), row), pl.BlockSpec((tm, 2 * D), row),
                  vec(2 * D), wspec(WIDTH, D), wspec(WIDTH, D), wspec(D, D),
                  pl.BlockSpec((tm, D), row), vec(D), vec(D)],
        out_specs=[pl.BlockSpec((tm, D), row), pl.BlockSpec((tm, D), row), pl.BlockSpec((tm, D), row)],
        out_shape=[jax.ShapeDtypeStruct((T, D), F32), jax.ShapeDtypeStruct((T, D), F32),
                   jax.ShapeDtypeStruct((T, D), BF16)],
        compiler_params=_cparams(("arbitrary",), 56),
    )(oa, ob, g, bg, wpa, wpb, wo, xin, lng, lnb)


def _proj_bwd(dx1, r1, lng, oa, ob, g, bg, wpa, wpb, wo, wl, layer, *, name):
    T, D = dx1.shape
    tm = _tile(T, 512)
    row = lambda i: (i, 0)
    fixed = lambda i: (0, 0)
    wspec = lambda r, c: pl.BlockSpec((None, r, c), lambda i: (wl, 0, 0))
    vec = lambda c: pl.BlockSpec((None, 1, c), lambda i: (layer, 0, 0))

    def body(dx_ref, r1_ref, lg_ref, oa_ref, ob_ref, g_ref, bg_ref, wpa_ref, wpb_ref, wo_ref,
             dr_ref, mix_ref, dya_ref, dyb_ref, dg_ref, doa_ref, dob_ref, dlg_ref, dlb_ref, dbg_ref):
        @pl.when(pl.program_id(0) == 0)
        def _():
            dlg_ref[...] = jnp.zeros_like(dlg_ref)
            dlb_ref[...] = jnp.zeros_like(dlb_ref)
            dbg_ref[...] = jnp.zeros_like(dbg_ref)

        dx = dx_ref[...]
        dr, xhat = _ln_bwd(dx, r1_ref[...], lg_ref[...])
        dr_ref[...] = dr
        dlg_ref[...] += jnp.sum(dx * xhat, axis=0, keepdims=True)
        dlb_ref[...] += jnp.sum(dx, axis=0, keepdims=True)
        dmix = _dot_nt(dr.astype(BF16), wo_ref[...])
        ya, yb, sa, sb = _gated_mix(oa_ref, ob_ref, g_ref, bg_ref, wpa_ref, wpb_ref, D)
        mix_ref[...] = (sa * ya + sb * yb).astype(BF16)
        dya = (dmix * sa).astype(BF16)
        dyb = (dmix * sb).astype(BF16)
        dya_ref[...] = dya
        dyb_ref[...] = dyb
        dga = dmix * ya * (sa * (1.0 - sa))
        dgb = dmix * yb * (sb * (1.0 - sb))
        dg_ref[:, :D] = dga.astype(BF16)
        dg_ref[:, D:] = dgb.astype(BF16)
        dbg_ref[:, :D] += jnp.sum(dga, axis=0, keepdims=True)
        dbg_ref[:, D:] += jnp.sum(dgb, axis=0, keepdims=True)
        doa_ref[...] = _dot_nt(dya, wpa_ref[...]).astype(BF16)
        dob_ref[...] = _dot_nt(dyb, wpb_ref[...]).astype(BF16)

    return pl.pallas_call(
        body, name=name, grid=(T // tm,),
        in_specs=[pl.BlockSpec((tm, D), row), pl.BlockSpec((tm, D), row), vec(D),
                  pl.BlockSpec((tm, WIDTH), row), pl.BlockSpec((tm, WIDTH), row), pl.BlockSpec((tm, 2 * D), row),
                  vec(2 * D), wspec(WIDTH, D), wspec(WIDTH, D), wspec(D, D)],
        out_specs=[pl.BlockSpec((tm, D), row), pl.BlockSpec((tm, D), row), pl.BlockSpec((tm, D), row),
                   pl.BlockSpec((tm, D), row), pl.BlockSpec((tm, 2 * D), row),
                   pl.BlockSpec((tm, WIDTH), row), pl.BlockSpec((tm, WIDTH), row),
                   pl.BlockSpec((1, D), fixed), pl.BlockSpec((1, D), fixed), pl.BlockSpec((1, 2 * D), fixed)],
        out_shape=[jax.ShapeDtypeStruct((T, D), F32), jax.ShapeDtypeStruct((T, D), BF16),
                   jax.ShapeDtypeStruct((T, D), BF16), jax.ShapeDtypeStruct((T, D), BF16),
                   jax.ShapeDtypeStruct((T, 2 * D), BF16),
                   jax.ShapeDtypeStruct((T, WIDTH), BF16), jax.ShapeDtypeStruct((T, WIDTH), BF16),
                   jax.ShapeDtypeStruct((1, D), F32), jax.ShapeDtypeStruct((1, D), F32),
                   jax.ShapeDtypeStruct((1, 2 * D), F32)],
        compiler_params=_cparams(("arbitrary",), 56),
    )(dx1, r1, lng, oa, ob, g, bg, wpa, wpb, wo)


def _ffn_fwd(x1, wfi, wfo, wl, lng, lnb, layer, *, alpha, name, job=None):
    T, D = x1.shape
    tf = wfi.shape[-1]
    nj = wfi.shape[0] // 2
    tm = _tile(T, 512)
    vec = lambda c: pl.BlockSpec((None, 1, c), lambda i, j: (layer, 0, 0))

    def body(x_ref, wg_ref, wu_ref, wo_ref, lg_ref, lb_ref, gs_ref, us_ref, r2_ref, x2_ref, acc_ref, xb_ref):
        j = pl.program_id(1)

        @pl.when(j == 0)
        def _():
            xb_ref[...] = x_ref[...].astype(BF16)
            acc_ref[...] = jnp.zeros_like(acc_ref)

        gv = _dot(xb_ref[...], wg_ref[...])
        uv = _dot(xb_ref[...], wu_ref[...])
        gs_ref[...] = gv
        us_ref[...] = uv
        act = gv * jax.nn.sigmoid(gv) * uv
        acc_ref[...] += _dot(act.astype(BF16), wo_ref[...])

        @pl.when(j == nj - 1)
        def _():
            r2 = alpha * x_ref[...] + acc_ref[...]
            r2_ref[...] = r2
            x2_ref[...] = _ln_fwd(r2, lg_ref[...], lb_ref[...])

    return _call_carrying(
        job, body, name=name, grid=(T // tm, nj),
        in_specs=[pl.BlockSpec((tm, D), lambda i, j: (i, 0)),
                  pl.BlockSpec((None, None, D, tf), lambda i, j: (j, wl, 0, 0)),
                  pl.BlockSpec((None, None, D, tf), lambda i, j: (j + nj, wl, 0, 0)),
                  pl.BlockSpec((None, tf, D), lambda i, j: (wl, j, 0)),
                  vec(D), vec(D)],
        out_specs=[pl.BlockSpec((None, tm, tf), lambda i, j: (j, i, 0)),
                   pl.BlockSpec((None, tm, tf), lambda i, j: (j, i, 0)),
                   pl.BlockSpec((tm, D), lambda i, j: (i, 0)),
                   pl.BlockSpec((tm, D), lambda i, j: (i, 0))],
        out_shape=[jax.ShapeDtypeStruct((nj, T, tf), F32), jax.ShapeDtypeStruct((nj, T, tf), F32),
                   jax.ShapeDtypeStruct((T, D), F32), jax.ShapeDtypeStruct((T, D), F32)],
        scratch_shapes=[pltpu.VMEM((tm, D), F32), pltpu.VMEM((tm, D), BF16)],
        vmem_mib=56, args=(x1, wfi, wfi, wfo, lng, lnb))


def _ffn_bwd(dx2, r2, lng, gs, us, wfi, wfo, wl, layer, *, alpha, name, job=None):
    T, D = dx2.shape
    tf = wfi.shape[-1]
    nj = wfi.shape[0] // 2
    tm = _tile(T, 512)
    vec = lambda c: pl.BlockSpec((None, 1, c), lambda i, j: (layer, 0, 0))
    blk = lambda: pl.BlockSpec((None, tm, tf), lambda i, j: (j, i, 0))

    def body(dx_ref, r2_ref, lg_ref, gs_ref, us_ref, wg_ref, wu_ref, wo_ref,
             dr_ref, act_ref, dg_ref, du_ref, dx1_ref, dlg_ref, dlb_ref, acc_ref, drb_ref):
        i = pl.program_id(0)
        j = pl.program_id(1)

        @pl.when((i == 0) & (j == 0))
        def _():
            dlg_ref[...] = jnp.zeros_like(dlg_ref)
            dlb_ref[...] = jnp.zeros_like(dlb_ref)

        @pl.when(j == 0)
        def _():
            dx = dx_ref[...]
            dr, xhat = _ln_bwd(dx, r2_ref[...], lg_ref[...])
            dlg_ref[...] += jnp.sum(dx * xhat, axis=0, keepdims=True)
            dlb_ref[...] += jnp.sum(dx, axis=0, keepdims=True)
            drb_ref[...] = dr.astype(BF16)
            dr_ref[...] = dr.astype(BF16)
            acc_ref[...] = alpha * dr

        dact = _dot_nt(drb_ref[...], wo_ref[...])
        gv = gs_ref[...]
        uv = us_ref[...]
        s = jax.nn.sigmoid(gv)
        silu = gv * s
        act_ref[...] = (silu * uv).astype(BF16)
        dg = (dact * uv * (s * (1.0 + gv * (1.0 - s)))).astype(BF16)
        du = (dact * silu).astype(BF16)
        dg_ref[...] = dg
        du_ref[...] = du
        acc_ref[...] += _dot_nt(dg, wg_ref[...]) + _dot_nt(du, wu_ref[...])

        @pl.when(j == nj - 1)
        def _():
            dx1_ref[...] = acc_ref[...]

    return _call_carrying(
        job, body, name=name, grid=(T // tm, nj),
        in_specs=[pl.BlockSpec((tm, D), lambda i, j: (i, 0)), pl.BlockSpec((tm, D), lambda i, j: (i, 0)), vec(D),
                  blk(), blk(),
                  pl.BlockSpec((None, None, D, tf), lambda i, j: (j, wl, 0, 0)),
                  pl.BlockSpec((None, None, D, tf), lambda i, j: (j + nj, wl, 0, 0)),
                  pl.BlockSpec((None, tf, D), lambda i, j: (wl, j, 0))],
        out_specs=[pl.BlockSpec((tm, D), lambda i, j: (i, 0)), blk(), blk(), blk(),
                   pl.BlockSpec((tm, D), lambda i, j: (i, 0)),
                   pl.BlockSpec((1, D), lambda i, j: (0, 0)), pl.BlockSpec((1, D), lambda i, j: (0, 0))],
        out_shape=[jax.ShapeDtypeStruct((T, D), BF16),
                   jax.ShapeDtypeStruct((nj, T, tf), BF16), jax.ShapeDtypeStruct((nj, T, tf), BF16),
                   jax.ShapeDtypeStruct((nj, T, tf), BF16),
                   jax.ShapeDtypeStruct((T, D), F32),
                   jax.ShapeDtypeStruct((1, D), F32), jax.ShapeDtypeStruct((1, D), F32)],
        scratch_shapes=[pltpu.VMEM((tm, D), F32), pltpu.VMEM((tm, D), BF16)],
        vmem_mib=56, args=(dx2, r2, lng, gs, us, wfi, wfi, wfo))


def _loss_head(y, target, *, name):
    T, D = y.shape
    tm = _tile(T, 1024)

    def body(y_ref, t_ref, dy_ref, sq_ref):
        @pl.when(pl.program_id(0) == 0)
        def _():
            sq_ref[...] = jnp.zeros_like(sq_ref)
        err = y_ref[...] - t_ref[...]
        dy_ref[...] = err * (1.0 / D)
        sq_ref[...] += jnp.sum(err * err, axis=0, keepdims=True)

    return pl.pallas_call(
        body, name=name, grid=(T // tm,),
        in_specs=[pl.BlockSpec((tm, D), lambda i: (i, 0)), pl.BlockSpec((tm, D), lambda i: (i, 0))],
        out_specs=[pl.BlockSpec((tm, D), lambda i: (i, 0)), pl.BlockSpec((1, D), lambda i: (0, 0))],
        out_shape=[jax.ShapeDtypeStruct((T, D), F32), jax.ShapeDtypeStruct((1, D), F32)],
        compiler_params=_cparams(("arbitrary",)),
    )(y, target)


def _my_place():
    return lax.axis_index("x"), lax.axis_index("y"), lax.axis_index("c")


def _peer(place, k):
    x, y, c = place
    return (1 - x if k & 4 else x, 1 - y if k & 2 else y, 1 - c if k & 1 else c)


def _logical(place):
    x, y, c = place
    return 4 * x + 2 * y + c


def _block_of(ref, mode, idx):
    if mode == "blk":
        return ref.at[idx]
    if mode == "col":
        size = ref.shape[2] // N_DEV
        return ref.at[:, :, pl.ds(pl.multiple_of(idx * size, size), size)]
    size = ref.shape[1] // N_DEV
    return ref.at[:, pl.ds(pl.multiple_of(idx * size, size), size), :]


def _full_shape(shard, mode):
    if mode == "blk":
        return (N_DEV,) + shard.shape
    if mode == "col":
        return shard.shape[:2] + (N_DEV * shard.shape[2],)
    return (shard.shape[0], N_DEV * shard.shape[1], shard.shape[2])


class _Exchange:
    def __init__(self, arrays, out_shape, build):
        self.arrays = list(arrays)
        self.out_shape = list(out_shape)
        self.build = build

    def scratch(self):
        n = len(self.arrays)
        return [pltpu.SemaphoreType.DMA((n * N_DEV,)), pltpu.SemaphoreType.DMA((n * N_DEV,)),
                pltpu.SemaphoreType.DMA((n,))]

    def start(self, ins, outs, sems):
        for cp in self.build(ins, outs, *sems):
            cp.start()

    def wait(self, ins, outs, sems):
        for cp in self.build(ins, outs, *sems):
            cp.wait()

    def run(self, name):
        n_in, n_out = len(self.arrays), len(self.out_shape)
        hbm = pl.BlockSpec(memory_space=pltpu.HBM)

        def body(*refs):
            ins, outs, sems = refs[:n_in], refs[n_in:n_in + n_out], refs[n_in + n_out:]
            self.start(ins, outs, sems)
            self.wait(ins, outs, sems)

        return pl.pallas_call(
            body, name=name, in_specs=[hbm] * n_in, out_specs=[hbm] * n_out,
            out_shape=self.out_shape, scratch_shapes=self.scratch(),
        )(*self.arrays)


def _copies_to_all(src_of, dst_of, n, send, recv, local):
    me = _my_place()
    copies = []
    for a in range(n):
        copies.append(pltpu.make_async_copy(src_of(a, _logical(me)), dst_of(a), local.at[a]))
        for k in range(1, N_DEV):
            peer = _peer(me, k)
            copies.append(pltpu.make_async_remote_copy(
                src_ref=src_of(a, _logical(peer)), dst_ref=dst_of(a),
                send_sem=send.at[a * N_DEV + k], recv_sem=recv.at[a * N_DEV + k],
                device_id=peer, device_id_type=MESH))
    return copies


def _gather_job(shards, modes):
    def build(ins, outs, send, recv, local):
        my_id = _logical(_my_place())
        return _copies_to_all(lambda a, dev: ins[a], lambda a: _block_of(outs[a], modes[a], my_id),
                              len(shards), send, recv, local)

    return _Exchange(shards, [jax.ShapeDtypeStruct(_full_shape(s, m), s.dtype) for s, m in zip(shards, modes)], build)


def _grad_block(ref, mode, idx):
    if mode == "blk":
        return ref.at[idx]
    if mode == "col":
        size = ref.shape[1] // N_DEV
        return ref.at[:, pl.ds(pl.multiple_of(idx * size, size), size)]
    size = ref.shape[0] // N_DEV
    return ref.at[pl.ds(pl.multiple_of(idx * size, size), size), :]


def _grad_shard_shape(g, mode):
    if mode == "blk":
        return g.shape[1:]
    if mode == "col":
        return (g.shape[0], g.shape[1] // N_DEV)
    return (g.shape[0] // N_DEV, g.shape[1])


def _grads_job(groups, modes):
    flat = [(g, w, l) for w, per_w in enumerate(groups) for l, g in enumerate(per_w)]

    def build(ins, outs, send, recv, local):
        my_id = _logical(_my_place())
        return _copies_to_all(lambda a, dev: _grad_block(ins[a], modes[flat[a][1]], dev),
                              lambda a: outs[flat[a][1]].at[my_id, flat[a][2]],
                              len(flat), send, recv, local)

    out_shape = [jax.ShapeDtypeStruct((N_DEV, len(per_w)) + _grad_shard_shape(per_w[0], m), per_w[0].dtype)
                 for per_w, m in zip(groups, modes)]
    return _Exchange([g for g, _, _ in flat], out_shape, build)


def _adamw(w, g, m, v):
    m = ADAM_B1 * m + (1.0 - ADAM_B1) * g
    v = ADAM_B2 * v + (1.0 - ADAM_B2) * (g * g)
    m_hat = m / (1.0 - ADAM_B1 ** ADAM_STEP)
    v_hat = v / (1.0 - ADAM_B2 ** ADAM_STEP)
    delta = -ADAM_LR * (m_hat / (jnp.sqrt(v_hat) + ADAM_EPS) + ADAM_WD * w)
    return delta, m, v


def _sum_slots_adamw(slots, w, m, v, *, name):
    R, C = w.shape
    tr = _tile(R, 256)

    def body(s_ref, w_ref, m_ref, v_ref, g_out, d_out, m_out, v_out):
        g = s_ref[0].astype(F32)
        for s in range(1, N_DEV):
            g = g + s_ref[s].astype(F32)
        delta, m_new, v_new = _adamw(w_ref[...], g, m_ref[...], v_ref[...])
        g_out[...] = g
        d_out[...] = delta
        m_out[...] = m_new
        v_out[...] = v_new

    spec = pl.BlockSpec((tr, C), lambda i: (i, 0))
    return pl.pallas_call(
        body, name=name, grid=(R // tr,),
        in_specs=[pl.BlockSpec((N_DEV, tr, C), lambda i: (0, i, 0)), spec, spec, spec],
        out_specs=[spec] * 4,
        out_shape=[jax.ShapeDtypeStruct((R, C), F32)] * 4,
        compiler_params=_cparams(("parallel",)),
    )(slots, w, m, v)


def _small_allreduce_adamw(g, w, m, v, *, name):
    R = g.shape[0]
    vmem = pl.BlockSpec(memory_space=pltpu.VMEM)

    def body(g_ref, w_ref, m_ref, v_ref, g_out, d_out, m_out, v_out, slots, send, recv):
        me = _my_place()
        my_id = _logical(me)
        slots[my_id] = g_ref[...]
        copies = []
        for k in range(1, N_DEV):
            cp = pltpu.make_async_remote_copy(
                src_ref=g_ref, dst_ref=slots.at[my_id], send_sem=send.at[k], recv_sem=recv.at[k],
                device_id=_peer(me, k), device_id_type=MESH)
            cp.start()
            copies.append(cp)
        for cp in copies:
            cp.wait()
        total = slots[0]
        for s in range(1, N_DEV):
            total = total + slots[s]
        delta, m_new, v_new = _adamw(w_ref[...], total, m_ref[...], v_ref[...])
        g_out[...] = total
        d_out[...] = delta
        m_out[...] = m_new
        v_out[...] = v_new

    return pl.pallas_call(
        body, name=name,
        in_specs=[vmem] * 4, out_specs=[vmem] * 4,
        out_shape=[jax.ShapeDtypeStruct((R, LANES), F32)] * 4,
        scratch_shapes=[pltpu.VMEM((N_DEV, R, LANES), F32),
                        pltpu.SemaphoreType.DMA((N_DEV,)), pltpu.SemaphoreType.DMA((N_DEV,))],
    )(g, w, m, v)


def _pack(parts):
    flat = jnp.concatenate([p.reshape(-1) for p in parts])
    rows = -(-flat.shape[0] // (8 * LANES)) * 8
    return jnp.pad(flat, (0, rows * LANES - flat.shape[0])).reshape(rows, LANES)


def _unpack(packed, like):
    flat = packed.reshape(-1)
    out, pos = [], 0
    for p in like:
        out.append(flat[pos:pos + p.size].reshape(p.shape))
        pos += p.size
    return out


def kernel(x, w_in, b_gate, rel_bias, w_proj_a, w_proj_b, w_out, ln1_g, ln1_b, w_ffn_in, w_ffn_out, ln2_g, ln2_b, loss_target, m_w_in, m_b_gate, m_rel_bias, m_w_proj_a, m_w_proj_b, m_w_out, m_ln1_g, m_ln1_b, m_w_ffn_in, m_w_ffn_out, m_ln2_g, m_ln2_b, v_w_in, v_b_gate, v_rel_bias, v_w_proj_a, v_w_proj_b, v_w_out, v_ln1_g, v_ln1_b, v_w_ffn_in, v_w_ffn_out, v_ln2_g, v_ln2_b):
    L = w_in.shape[0]
    T, D = x.shape[1], x.shape[2]
    alpha = float((2 * L) ** 0.25)
    n_qkv = 6 * WIDTH

    big = [w_in, w_proj_a, w_proj_b, w_out, w_ffn_in, w_ffn_out]
    kinds = ["in", "pa", "pb", "o", "fi", "fo"]
    modes = ["col", "col", "col", "row", "blk", "row"]
    mode_of = dict(zip(kinds, modes))
    w_bf = dict(zip(kinds, [w.astype(BF16) for w in big]))

    def gather_of(ks, l):
        return _gather_job([w_bf[k][l:l + 1] for k in ks], [mode_of[k] for k in ks])

    W = [dict() for _ in range(L)]
    (W[0]["in"],) = gather_of(["in"], 0).run("gather_w_in_first")
    vec3 = lambda a: a[:, None, :]
    bg3, l1g, l1b, l2g, l2b = vec3(b_gate), vec3(ln1_g), vec3(ln1_b), vec3(ln2_g), vec3(ln2_b)
    b_col0 = 3 * WIDTH // LANES

    h = x[0]
    saved = []
    for l in range(L):
        ahead = l + 1 < L
        soon = ["pa", "pb", "o", "fo"]
        (qkv, gates), got = _in_proj(h, W[l]["in"], 0, n_qkv=n_qkv, name=f"in_proj_{l}",
                                     job=gather_of(soon, 0) if l == 0 else None)
        W[l].update(zip(soon, got))
        kvpad = jnp.pad(qkv[:, WIDTH:3 * WIDTH], ((A_WIN - A_TQ, 0), (0, 0)))
        bias = _toeplitz_bias(rel_bias[l])
        oa, got = _attn_a_fwd(qkv, kvpad, bias, name=f"attn_a_fwd_{l}", job=gather_of(["fi"], 0) if l == 0 else None)
        W[l].update(zip(["fi"], got))
        early = ["in", "pa", "pb", "o"]
        ob, got = _attn_b_fwd(qkv, col0=b_col0, name=f"attn_b_fwd_{l}", job=gather_of(early, l + 1) if ahead else None)
        W[l + 1 if ahead else l].update(zip(early, got))
        x1, r1, x1b = _proj_fwd(oa, ob, gates, bg3, W[l]["pa"], W[l]["pb"], W[l]["o"], 0, h, l1g, l1b, l,
                           alpha=alpha, name=f"proj_fwd_{l}")
        (gs, us, r2, x2), got = _ffn_fwd(x1, W[l]["fi"], W[l]["fo"], 0, l2g, l2b, l, alpha=alpha, name=f"ffn_fwd_{l}",
                                         job=gather_of(["fi", "fo"], l + 1) if ahead else None)
        W[l + 1 if ahead else l].update(zip(["fi", "fo"], got))
        saved.append((h, qkv, gates, kvpad, bias, oa, ob, x1b, r1, gs, us, r2))
        h = x2

    d_h, sq = _loss_head(h, loss_target[0], name="loss_head")
    loss = lax.psum((0.5 / D) * jnp.sum(sq), ("x", "y", "c"))

    g_bg, g_rb, g_l1g, g_l1b, g_l2g, g_l2b = ([None] * L for _ in range(6))
    slot = {k: [None] * L for k in kinds}

    def exchange_of(ks, grads):
        return _grads_job([[g] for g in grads], [mode_of[k] for k in ks])

    w_in_above = None
    for l in reversed(range(L)):
        xin, qkv, gates, kvpad, bias, oa, ob, x1b, r1, gs, us, r2 = saved[l]
        (dr2, act, dgt, dup, dx1, g_l2g[l], g_l2b[l]), got = _ffn_bwd(
            d_h, r2, l2g, gs, us, W[l]["fi"], W[l]["fo"], 0, l, alpha=alpha, name=f"ffn_bwd_{l}", job=w_in_above)
        if w_in_above is not None:
            (slot["in"][l + 1],) = got
        g_fo = _mm_tn_blocked_a(act, dr2, name=f"grad_w_ffn_out_{l}").reshape(-1, D)
        g_fi = jnp.concatenate(_mm_tn_blocked_pair(x1b, dgt, dup, name=f"grad_w_ffn_in_{l}"), axis=0)
        (dr1, mixin, dya, dyb, dgates, doa, dob, g_l1g[l], g_l1b[l], g_bg[l]) = _proj_bwd(
            dx1, r1, l1g, oa, ob, gates, bg3, W[l]["pa"], W[l]["pb"], W[l]["o"], 0, l, name=f"proj_bwd_{l}")
        g_o = _mm_tn(mixin, dr1, tm=_tile(D, 1024), tn=_tile(D, 1024), name=f"grad_w_out_{l}")
        g_pa = _mm_tn(oa, dya, tm=WIDTH, tn=_tile(D, 1024), name=f"grad_w_proj_a_{l}")
        g_pb = _mm_tn(ob, dyb, tm=WIDTH, tn=_tile(D, 1024), name=f"grad_w_proj_b_{l}")
        (dqa, dka, dva, dbias), (slot["fi"][l], slot["fo"][l]) = _attn_a_bwd(
            qkv, kvpad, bias, doa, name=f"attn_a_bwd_{l}", job=exchange_of(["fi", "fo"], [g_fi, g_fo]))
        g_rb[l] = _toeplitz_bias_grad(dbias)
        (dqb, dkb, dvb), (slot["pa"][l], slot["pb"][l], slot["o"][l]) = _attn_b_bwd(
            qkv, ob, dob, col0=b_col0, name=f"attn_b_bwd_{l}", job=exchange_of(["pa", "pb", "o"], [g_pa, g_pb, g_o]))
        pad = A_WIN - A_TQ
        d_pre = jnp.concatenate([dqa, dka[pad:].astype(BF16), dva[pad:].astype(BF16),
                                 dqb, dkb.astype(BF16), dvb.astype(BF16), dgates], axis=1)
        g_in = _mm_tn(xin, d_pre, tm=D, tn=4 * w_in.shape[2], name=f"grad_w_in_{l}")
        w_in_above = exchange_of(["in"], [g_in])
        d_h, got = _mm_nt_add(d_pre, W[l]["in"], 0, dr1, alpha, name=f"grad_x_{l}", job=w_in_above if l == 0 else None)
        if l == 0:
            (slot["in"][0],) = got
    grad_x = d_h[None]

    slots = [jnp.concatenate(slot[k], axis=1) if L > 1 else slot[k][0] for k in kinds]
    moments_m = [m_w_in, m_w_proj_a, m_w_proj_b, m_w_out, m_w_ffn_in, m_w_ffn_out]
    moments_v = [v_w_in, v_w_proj_a, v_w_proj_b, v_w_out, v_w_ffn_in, v_w_ffn_out]
    names = ["w_in", "w_proj_a", "w_proj_b", "w_out", "w_ffn_in", "w_ffn_out"]
    big_out = {}
    for nm, s, w, m, v in zip(names, slots, big, moments_m, moments_v):
        two = lambda a: a.reshape(-1, a.shape[-1])
        res = _sum_slots_adamw(s.reshape(N_DEV, -1, s.shape[-1]), two(w), two(m), two(v), name=f"adamw_{nm}")
        big_out[nm] = [r.reshape(w.shape) for r in res]

    small_w = [b_gate, rel_bias, ln1_g, ln1_b, ln2_g, ln2_b]
    small_g = [jnp.stack(g) for g in (g_bg, g_rb, g_l1g, g_l1b, g_l2g, g_l2b)]
    small_m = [m_b_gate, m_rel_bias, m_ln1_g, m_ln1_b, m_ln2_g, m_ln2_b]
    small_v = [v_b_gate, v_rel_bias, v_ln1_g, v_ln1_b, v_ln2_g, v_ln2_b]
    res = _small_allreduce_adamw(_pack(small_g), _pack(small_w), _pack(small_m), _pack(small_v),
                                 name="allreduce_small_adamw")
    small_names = ["b_gate", "rel_bias", "ln1_g", "ln1_b", "ln2_g", "ln2_b"]
    small_out = {nm: [] for nm in small_names}
    for packed in res:
        for nm, arr in zip(small_names, _unpack(packed, small_w)):
            small_out[nm].append(arr)

    order = ["w_in", "b_gate", "rel_bias", "w_proj_a", "w_proj_b", "w_out", "ln1_g", "ln1_b",
             "w_ffn_in", "w_ffn_out", "ln2_g", "ln2_b"]
    every = {**big_out, **small_out}
    outs = [loss, grad_x]
    for kind in range(4):
        outs += [every[nm][kind] for nm in order]
    return tuple(outs)
```

```python
import functools
import math

import jax
import jax.numpy as jnp
import numpy as np
from jax import lax
from jax.experimental import pallas as pl
from jax.experimental.pallas import tpu as pltpu

F32 = jnp.float32
BF16 = jnp.bfloat16

HEAD_DIM = 64
CHUNK = 64
LEFT_CHUNKS = 8
REL_CLIP = 256
N_REL = 2 * REL_CLIP + 1
WIDTH = 512
LANES = 128
A_TQ = 256
A_WIN = A_TQ + LEFT_CHUNKS * CHUNK
A_STRIP = 128
B_TQ = 512
B_TS = 256
B_PIECE = 64
B_DEAD = -160.0
LN_EPS = 1e-5
QK_SCALE = 1.0 / math.sqrt(HEAD_DIM)
LOG2E = 1.4426950408889634
NEG = -1e30

ADAM_LR = 0.001
ADAM_B1 = 0.9
ADAM_B2 = 0.999
ADAM_EPS = 1e-08
ADAM_WD = 0.01
ADAM_STEP = 10

N_DEV = 8
MESH = pl.DeviceIdType.MESH
MIB = 1024 * 1024


def _cparams(sem=None, vmem_mib=48):
    return pltpu.CompilerParams(dimension_semantics=sem, vmem_limit_bytes=vmem_mib * MIB)


def _dot(a, b):
    return jnp.dot(a, b, preferred_element_type=F32)


def _dot_nt(a, b):
    return lax.dot_general(a, b, (((1,), (1,)), ((), ())), preferred_element_type=F32)


def _dot_tn(a, b):
    return lax.dot_general(a, b, (((0,), (0,)), ((), ())), preferred_element_type=F32)


def _tile(n, pref):
    if n <= pref:
        return n
    for t in range(pref - pref % 8, 0, -8):
        if n % t == 0:
            return t
    raise ValueError((n, pref))


def _in_proj(a, w, layer, *, n_qkv, name, job=None):
    M, K = a.shape
    N = w.shape[2]
    tm = _tile(M, 1024)
    tn = 512
    assert n_qkv % tn == 0 and (N - n_qkv) % tn == 0
    n_q = n_qkv // tn

    def body(a_ref, w_ref, q_ref, g_ref, ab_ref):
        j = pl.program_id(1)

        @pl.when(j == 0)
        def _():
            ab_ref[...] = a_ref[...].astype(BF16)

        res = _dot(ab_ref[...], w_ref[...])

        @pl.when(j < n_q)
        def _():
            q_ref[...] = res.astype(BF16)

        @pl.when(j >= n_q)
        def _():
            g_ref[...] = res

    return _call_carrying(
        job, body, name=name, grid=(M // tm, N // tn),
        in_specs=[pl.BlockSpec((tm, K), lambda i, j: (i, 0)),
                  pl.BlockSpec((None, K, tn), lambda i, j: (layer, 0, j))],
        out_specs=[pl.BlockSpec((tm, tn), lambda i, j: (i, jnp.minimum(j, n_q - 1))),
                   pl.BlockSpec((tm, tn), lambda i, j: (i, jnp.maximum(j - n_q, 0)))],
        out_shape=[jax.ShapeDtypeStruct((M, n_qkv), BF16), jax.ShapeDtypeStruct((M, N - n_qkv), F32)],
        scratch_shapes=[pltpu.VMEM((tm, K), BF16)], vmem_mib=48, args=(a, w))


def _mm_nt_add(a, w, layer, add, add_scale, *, name, job=None):
    M, K = a.shape
    N = w.shape[1]
    tm = _tile(M, 1024)
    tk = _tile(K, 1024)

    def body(a_ref, w_ref, add_ref, o_ref):
        @pl.when(pl.program_id(1) == 0)
        def _():
            o_ref[...] = add_scale * add_ref[...]
        o_ref[...] += _dot_nt(a_ref[...], w_ref[...])

    (out,), rode = _call_carrying(
        job, body, name=name, grid=(M // tm, K // tk),
        in_specs=[pl.BlockSpec((tm, tk), lambda i, k: (i, k)),
                  pl.BlockSpec((None, N, tk), lambda i, k: (layer, 0, k)),
                  pl.BlockSpec((tm, N), lambda i, k: (i, 0))],
        out_specs=[pl.BlockSpec((tm, N), lambda i, k: (i, 0))],
        out_shape=[jax.ShapeDtypeStruct((M, N), F32)],
        scratch_shapes=[], vmem_mib=48, args=(a, w, add))
    return out, rode


def _tn_body(k_axis, n_k):
    def body(a_ref, b_ref, o_ref, acc_ref):
        k = pl.program_id(k_axis)

        @pl.when(k == 0)
        def _():
            acc_ref[...] = jnp.zeros_like(acc_ref)
        acc_ref[...] += _dot_tn(a_ref[...].astype(BF16), b_ref[...].astype(BF16))

        @pl.when(k == n_k - 1)
        def _():
            o_ref[...] = acc_ref[...].astype(o_ref.dtype)
    return body


def _mm_tn(a, b, *, tm, tn, name):
    T, M = a.shape
    N = b.shape[1]
    tk = _tile(T, 512)
    return pl.pallas_call(
        _tn_body(2, T // tk), name=name, grid=(M // tm, N // tn, T // tk),
        in_specs=[pl.BlockSpec((tk, tm), lambda i, j, k: (k, i)),
                  pl.BlockSpec((tk, tn), lambda i, j, k: (k, j))],
        out_specs=pl.BlockSpec((tm, tn), lambda i, j, k: (i, j)),
        out_shape=jax.ShapeDtypeStruct((M, N), BF16),
        scratch_shapes=[pltpu.VMEM((tm, tn), F32)],
        compiler_params=_cparams(("parallel", "parallel", "arbitrary")),
    )(a, b)


def _mm_tn_blocked_pair(a, b1, b2, *, name):
    T, M = a.shape
    S, _, N = b1.shape
    tk = _tile(T, 1024)
    n_k = T // tk

    def body(a_ref, b1_ref, b2_ref, o1_ref, o2_ref, acc1_ref, acc2_ref):
        k = pl.program_id(1)

        @pl.when(k == 0)
        def _():
            acc1_ref[...] = jnp.zeros_like(acc1_ref)
            acc2_ref[...] = jnp.zeros_like(acc2_ref)

        a_t = a_ref[...].astype(BF16)
        acc1_ref[...] += _dot_tn(a_t, b1_ref[...])
        acc2_ref[...] += _dot_tn(a_t, b2_ref[...])

        @pl.when(k == n_k - 1)
        def _():
            o1_ref[...] = acc1_ref[...].astype(o1_ref.dtype)
            o2_ref[...] = acc2_ref[...].astype(o2_ref.dtype)

    blk = lambda: pl.BlockSpec((None, tk, N), lambda s, k: (s, k, 0))
    out = lambda: pl.BlockSpec((None, M, N), lambda s, k: (s, 0, 0))
    return pl.pallas_call(
        body, name=name, grid=(S, n_k),
        in_specs=[pl.BlockSpec((tk, M), lambda s, k: (k, 0)), blk(), blk()],
        out_specs=[out(), out()],
        out_shape=[jax.ShapeDtypeStruct((S, M, N), BF16)] * 2,
        scratch_shapes=[pltpu.VMEM((M, N), F32), pltpu.VMEM((M, N), F32)],
        compiler_params=_cparams(("parallel", "arbitrary")),
    )(a, b1, b2)


def _mm_tn_blocked_a(a, b, *, name):
    S, T, M = a.shape
    N = b.shape[1]
    tk = _tile(T, 512)
    return pl.pallas_call(
        _tn_body(1, T // tk), name=name, grid=(S, T // tk),
        in_specs=[pl.BlockSpec((None, tk, M), lambda s, k: (s, k, 0)),
                  pl.BlockSpec((tk, N), lambda s, k: (k, 0))],
        out_specs=pl.BlockSpec((None, M, N), lambda s, k: (s, 0, 0)),
        out_shape=jax.ShapeDtypeStruct((S, M, N), BF16),
        scratch_shapes=[pltpu.VMEM((M, N), F32)],
        compiler_params=_cparams(("parallel", "arbitrary")),
    )(a, b)


def _ln_fwd(r, g, b):
    mu = jnp.mean(r, axis=-1, keepdims=True)
    xc = r - mu
    var = jnp.mean(xc * xc, axis=-1, keepdims=True)
    return xc * lax.rsqrt(var + LN_EPS) * g + b


def _ln_bwd(dy, r, g):
    mu = jnp.mean(r, axis=-1, keepdims=True)
    xc = r - mu
    var = jnp.mean(xc * xc, axis=-1, keepdims=True)
    rstd = lax.rsqrt(var + LN_EPS)
    xhat = xc * rstd
    dxh = dy * g
    m1 = jnp.mean(dxh, axis=-1, keepdims=True)
    m2 = jnp.mean(dxh * xhat, axis=-1, keepdims=True)
    return rstd * (dxh - m1 - xhat * m2), xhat


def _lane_is_head0():
    return lax.broadcasted_iota(jnp.int32, (1, LANES), 1) < HEAD_DIM


def _band_shape():
    a = np.arange(A_TQ)[:, None] // CHUNK
    b = np.arange(A_WIN)[None, :] // CHUNK
    return (b >= a) & (b <= a + LEFT_CHUNKS)


def _band_streams(strip):
    return [dict(h=h, n=strip, rows=pl.ds(r0, strip)) for h in range(2) for r0 in range(0, A_TQ, strip)]


def _band_scores(st, i, qh_ref, k2, bias_ref):
    s = _dot_nt(qh_ref[st["h"], st["rows"], :], k2) * QK_SCALE + bias_ref[st["h"], st["rows"], :]
    c = lax.broadcasted_iota(jnp.int32, (st["n"], A_WIN), 1)
    st["s"] = jnp.where(c >= LEFT_CHUNKS * CHUNK - i * A_TQ, s, NEG)


def _band_softmax(st):
    s = st.pop("s")
    e = jnp.exp(s - jnp.max(s, axis=1, keepdims=True))
    st["p"] = e / jnp.sum(e, axis=1, keepdims=True)


def _attn_a_fwd(qkv, kvpad, bias, *, name, job=None):
    T = qkv.shape[0]
    n_hp = WIDTH // LANES

    def body(q_ref, k_ref, v_ref, bias_ref, o_ref, qh_ref, acc_ref):
        i = pl.program_id(1)
        row0 = pl.multiple_of(i * A_TQ, A_TQ)
        q2 = q_ref[...]
        k2 = k_ref[pl.ds(row0, A_WIN), :]
        v2 = v_ref[pl.ds(row0, A_WIN), :]
        head0 = _lane_is_head0()
        qh_ref[0] = jnp.where(head0, q2, jnp.zeros_like(q2))
        qh_ref[1] = jnp.where(head0, jnp.zeros_like(q2), q2)

        def scores(st):
            _band_scores(st, i, qh_ref, k2, bias_ref)

        def values(st):
            acc_ref[st["h"], st["rows"], :] = _dot(st.pop("p").astype(BF16), v2)

        _skewed(_band_streams(A_STRIP), [scores, _band_softmax, values])
        o_ref[...] = jnp.where(head0, acc_ref[0], acc_ref[1]).astype(o_ref.dtype)

    (out,), rode = _call_carrying(
        job, body, name=name, grid=(n_hp, T // A_TQ),
        in_specs=[pl.BlockSpec((A_TQ, LANES), lambda hp, i: (i, hp)),
                  pl.BlockSpec((T + A_WIN - A_TQ, LANES), lambda hp, i: (0, hp)),
                  pl.BlockSpec((T + A_WIN - A_TQ, LANES), lambda hp, i: (0, hp + n_hp)),
                  pl.BlockSpec((2, A_TQ, A_WIN), lambda hp, i: (hp, 0, 0))],
        out_specs=[pl.BlockSpec((A_TQ, LANES), lambda hp, i: (i, hp))],
        out_shape=[jax.ShapeDtypeStruct((T, WIDTH), BF16)],
        scratch_shapes=[pltpu.VMEM((2, A_TQ, LANES), BF16), pltpu.VMEM((2, A_TQ, LANES), F32)],
        vmem_mib=48, args=(qkv, kvpad, kvpad, bias))
    return out, rode


def _attn_a_bwd(qkv, kvpad, bias, do, *, name, job=None):
    T = qkv.shape[0]
    TP = T + A_WIN - A_TQ
    n_hp = WIDTH // LANES

    def body(q_ref, k_ref, v_ref, bias_ref, do_ref, dq_ref, dk_ref, dv_ref, db_ref, qh_ref, doh_ref, dqa_ref):
        i = pl.program_id(1)

        @pl.when(i == 0)
        def _():
            dk_ref[...] = jnp.zeros_like(dk_ref)
            dv_ref[...] = jnp.zeros_like(dv_ref)
            db_ref[...] = jnp.zeros_like(db_ref)

        row0 = pl.multiple_of(i * A_TQ, A_TQ)
        window = pl.ds(row0, A_WIN)
        q2 = q_ref[...]
        do2 = do_ref[...]
        k2 = k_ref[window, :]
        v2 = v_ref[window, :]
        head0 = _lane_is_head0()
        zero = jnp.zeros_like(q2)
        qh_ref[0] = jnp.where(head0, q2, zero)
        qh_ref[1] = jnp.where(head0, zero, q2)
        doh_ref[0] = jnp.where(head0, do2, zero)
        doh_ref[1] = jnp.where(head0, zero, do2)
        dk = [[], []]
        dv = [[], []]

        def scores(st):
            _band_scores(st, i, qh_ref, k2, bias_ref)
            st["dp"] = _dot_nt(doh_ref[st["h"], st["rows"], :], v2)

        def dscores(st):
            _band_softmax(st)
            p, dp = st.pop("p"), st.pop("dp")
            ds = p * (dp - jnp.sum(p * dp, axis=1, keepdims=True))
            db_ref[st["h"], st["rows"], :] += ds
            st["dsb"] = (ds * QK_SCALE).astype(BF16)
            st["pb"] = p.astype(BF16)

        def grads(st):
            h, rows = st["h"], st["rows"]
            dsb = st.pop("dsb")
            dqa_ref[h, rows, :] = _dot(dsb, k2)
            dk[h].append(_dot_tn(dsb, q_ref[rows, :]))
            dv[h].append(_dot_tn(st.pop("pb"), do_ref[rows, :]))

        _skewed(_band_streams(A_TQ), [scores, dscores, grads])
        dq_ref[...] = jnp.where(head0, dqa_ref[0], dqa_ref[1]).astype(dq_ref.dtype)
        dk_ref[window, :] += jnp.where(head0, sum(dk[0]), sum(dk[1]))
        dv_ref[window, :] += jnp.where(head0, sum(dv[0]), sum(dv[1]))

    return _call_carrying(
        job, body, name=name, grid=(n_hp, T // A_TQ),
        in_specs=[pl.BlockSpec((A_TQ, LANES), lambda hp, i: (i, hp)),
                  pl.BlockSpec((TP, LANES), lambda hp, i: (0, hp)),
                  pl.BlockSpec((TP, LANES), lambda hp, i: (0, hp + n_hp)),
                  pl.BlockSpec((2, A_TQ, A_WIN), lambda hp, i: (hp, 0, 0)),
                  pl.BlockSpec((A_TQ, LANES), lambda hp, i: (i, hp))],
        out_specs=[pl.BlockSpec((A_TQ, LANES), lambda hp, i: (i, hp)),
                   pl.BlockSpec((TP, LANES), lambda hp, i: (0, hp)),
                   pl.BlockSpec((TP, LANES), lambda hp, i: (0, hp)),
                   pl.BlockSpec((2, A_TQ, A_WIN), lambda hp, i: (hp, 0, 0))],
        out_shape=[jax.ShapeDtypeStruct((T, WIDTH), BF16),
                   jax.ShapeDtypeStruct((TP, WIDTH), F32),
                   jax.ShapeDtypeStruct((TP, WIDTH), F32),
                   jax.ShapeDtypeStruct((WIDTH // HEAD_DIM, A_TQ, A_WIN), F32)],
        scratch_shapes=[pltpu.VMEM((2, A_TQ, LANES), BF16), pltpu.VMEM((2, A_TQ, LANES), BF16),
                        pltpu.VMEM((2, A_TQ, LANES), F32)],
        vmem_mib=56, args=(qkv, kvpad, kvpad, bias, do))


def _toeplitz_bias(rb):
    H = rb.shape[0]
    span = A_TQ + A_WIN - 1
    n_tail = span - (N_REL - 1)
    ext = jnp.concatenate([rb[:, 1:], jnp.broadcast_to(rb[:, N_REL - 1:], (H, n_tail))], axis=1)
    rev = jnp.pad(ext[:, ::-1], ((0, 0), (0, 1)))
    flat = jnp.broadcast_to(rev[:, None, :], (H, A_TQ, span + 1)).reshape(H, A_TQ * (span + 1))
    skew = flat[:, :A_TQ * span].reshape(H, A_TQ, span)
    return jnp.where(_band_shape()[None], skew[:, :, A_TQ - 1:A_TQ - 1 + A_WIN], NEG)


def _toeplitz_bias_grad(db):
    H = db.shape[0]
    span = A_TQ + A_WIN - 1
    d_skew = jnp.pad(db, ((0, 0), (0, 0), (A_TQ - 1, span - (A_TQ - 1) - A_WIN)))
    d_flat = jnp.pad(d_skew.reshape(H, A_TQ * span), ((0, 0), (0, A_TQ)))
    g_ext = jnp.sum(d_flat.reshape(H, A_TQ, span + 1), axis=1)[:, :span][:, ::-1]
    last = g_ext[:, N_REL - 2] + jnp.sum(g_ext[:, N_REL - 1:], axis=1)
    return jnp.concatenate([jnp.zeros((H, 1), F32), g_ext[:, :N_REL - 2], last[:, None]], axis=1)


def _split_bf16(x):
    hi = x.astype(BF16)
    lo = (x - hi.astype(F32)).astype(BF16)
    return hi, lo


def _sb_streams(d, strips=None, **tile):
    out = []
    for h in range(2):
        for r in (range(B_TQ // B_TS) if strips is None else strips):
            if d is not None and d > r:
                continue
            out.append(dict(h=h, r=r, rows=pl.ds(r * B_TS, B_TS), diag=(d is not None and d == r), **tile))
    return out


def _sb_sweep(i, car_ref, streams_of, run):
    sub = B_TQ // B_TS
    run([st for d in reversed(range(sub)) for st in streams_of(i * sub + d, d, None)])

    def alive(c):
        return (c[0] < i * sub) & (c[1] > B_DEAD)

    def step(c):
        kb = i * sub - 1 - c[0]
        if sub > 1:
            lower_alive = jnp.max(car_ref[:, B_TS:, :]) > B_DEAD
            lax.cond(lower_alive, lambda: run(streams_of(kb, None, None)), lambda: run(streams_of(kb, None, [0])))
        else:
            run(streams_of(kb, None, None))
        return c[0] + 1, jnp.max(car_ref[...])

    lax.while_loop(alive, step, (jnp.int32(0), jnp.float32(0.0)))


def _piece_rows(st, p):
    return pl.ds(st["r"] * B_TS + p, B_PIECE)


def _rows_cat(parts):
    return jnp.concatenate(parts, axis=0)


def _skewed(streams, stages):
    for t in range(len(streams) + len(stages) - 1):
        for s, st in enumerate(streams):
            if 0 <= t - s < len(stages):
                stages[t - s](st)


def _sb_logs(st, z2):
    log_beta, log_keep, keep_bf = [], [], []
    for p in range(0, B_TS, B_PIECE):
        z = z2[p:p + B_PIECE]
        lp2 = jnp.log(1.0 + jnp.exp2(-jnp.abs(z))) * LOG2E
        lb = jnp.minimum(z, 0.0) - lp2
        lk = lb - z
        if st["diag"]:
            lk = jnp.where(_strict_lower(p), lk, 0.0)
        log_beta.append(lb)
        log_keep.append(lk)
        keep_bf.append(lk.astype(BF16))
    st["log_beta"] = _rows_cat(log_beta)
    st["log_keep"] = _rows_cat(log_keep)
    st["keep_bf"] = _rows_cat(keep_bf)


def _strict_lower(p):
    t = p + lax.broadcasted_iota(jnp.int32, (B_PIECE, B_TS), 0)
    s = lax.broadcasted_iota(jnp.int32, (B_PIECE, B_TS), 1)
    return s < t


def _tri(strict):
    j = lax.broadcasted_iota(jnp.int32, (B_TS, B_TS), 0)
    s = lax.broadcasted_iota(jnp.int32, (B_TS, B_TS), 1)
    return jnp.where(j > s if strict else j >= s, 1.0, 0.0).astype(BF16)


def _call_carrying(job, body, *, name, grid, in_specs, out_specs, out_shape, scratch_shapes, vmem_mib, args):
    n_in, n_out, n_scr = len(in_specs), len(out_specs), len(scratch_shapes)
    if job is None:
        res = pl.pallas_call(body, name=name, grid=grid, in_specs=in_specs, out_specs=out_specs,
                             out_shape=out_shape, scratch_shapes=scratch_shapes,
                             compiler_params=_cparams(("arbitrary",) * len(grid), vmem_mib))(*args)
        return res, []
    j_in, j_out = len(job.arrays), len(job.out_shape)
    hbm = pl.BlockSpec(memory_space=pltpu.HBM)

    def carrying(*refs):
        refs = list(refs)
        ins, refs = refs[:n_in], refs[n_in:]
        j_ins, refs = refs[:j_in], refs[j_in:]
        outs, refs = refs[:n_out], refs[n_out:]
        j_outs, refs = refs[:j_out], refs[j_out:]
        scr, sems = refs[:n_scr], refs[n_scr:]
        first = functools.reduce(jnp.logical_and, [pl.program_id(d) == 0 for d in range(len(grid))])
        last = functools.reduce(jnp.logical_and, [pl.program_id(d) == grid[d] - 1 for d in range(len(grid))])

        @pl.when(first)
        def _():
            job.start(j_ins, j_outs, sems)

        body(*ins, *outs, *scr)

        @pl.when(last)
        def _():
            job.wait(j_ins, j_outs, sems)

    res = pl.pallas_call(
        carrying, name=name, grid=grid,
        in_specs=list(in_specs) + [hbm] * j_in, out_specs=list(out_specs) + [hbm] * j_out,
        out_shape=list(out_shape) + job.out_shape, scratch_shapes=list(scratch_shapes) + job.scratch(),
        compiler_params=_cparams(("arbitrary",) * len(grid), vmem_mib))(*args, *job.arrays)
    return res[:n_out], res[n_out:]


def _attn_b_fwd(qkv, *, col0, name, job=None):
    T = qkv.shape[0]
    n_hp = WIDTH // LANES
    sub = B_TQ // B_TS

    def body(q_ref, k_ref, v_ref, o_ref, acc_ref, car_ref, qh_ref):
        i = pl.program_id(1)
        q2 = q_ref[...]
        head0 = _lane_is_head0()
        qh_ref[0] = jnp.where(head0, q2, jnp.zeros_like(q2))
        qh_ref[1] = jnp.where(head0, jnp.zeros_like(q2), q2)
        tri_s = _tri(True)
        acc_ref[...] = jnp.zeros_like(acc_ref)
        car_ref[...] = jnp.zeros_like(car_ref)

        def streams_of(kb, d, strips):
            keys = pl.ds(pl.multiple_of(kb * B_TS, B_TS), B_TS)
            return _sb_streams(d, strips, k2=k_ref[keys, :], v2=v_ref[keys, :])

        def scores(st):
            st["z2"] = _dot_nt(qh_ref[st["h"], st["rows"], :], st.pop("k2")) * (QK_SCALE * LOG2E)

        def logs(st):
            _sb_logs(st, st.pop("z2"))

        def suffix(st):
            st["suffix"] = _dot(st.pop("keep_bf"), tri_s)

        def weights(st):
            log_beta, suffix, log_keep = st.pop("log_beta"), st.pop("suffix"), st.pop("log_keep")
            wb = []
            for p in range(0, B_TS, B_PIECE):
                rows = _piece_rows(st, p)
                car = car_ref[st["h"], rows, :]
                w = jnp.exp2(log_beta[p:p + B_PIECE] + suffix[p:p + B_PIECE] + car)
                if st["diag"]:
                    w = jnp.where(_strict_lower(p), w, 0.0)
                wb.append(w.astype(BF16))
                car_ref[st["h"], rows, :] = car + jnp.sum(log_keep[p:p + B_PIECE], axis=1, keepdims=True)
            st["wb"] = _rows_cat(wb)

        def values(st):
            acc_ref[st["h"], st["rows"], :] += _dot(st.pop("wb"), st.pop("v2"))

        _sb_sweep(i, car_ref, streams_of, lambda sts: _skewed(sts, [scores, logs, suffix, weights, values]))
        o_ref[...] = jnp.where(head0, acc_ref[0], acc_ref[1])

    (out,), rode = _call_carrying(
        job, body, name=name, grid=(n_hp, T // B_TQ),
        in_specs=[pl.BlockSpec((B_TQ, LANES), lambda hp, i: (i, hp + col0)),
                  pl.BlockSpec((T, LANES), lambda hp, i: (0, hp + col0 + n_hp)),
                  pl.BlockSpec((T, LANES), lambda hp, i: (0, hp + col0 + 2 * n_hp))],
        out_specs=[pl.BlockSpec((B_TQ, LANES), lambda hp, i: (i, hp))],
        out_shape=[jax.ShapeDtypeStruct((T, WIDTH), F32)],
        scratch_shapes=[pltpu.VMEM((2, B_TQ, LANES), F32), pltpu.VMEM((2, B_TQ, 1), F32),
                        pltpu.VMEM((2, B_TQ, LANES), BF16)],
        vmem_mib=48, args=(qkv, qkv, qkv))
    return out, rode


def _attn_b_bwd(qkv, out, do, *, col0, name, job=None):
    T = qkv.shape[0]
    n_hp = WIDTH // LANES
    sub = B_TQ // B_TS

    def body(q_ref, k_ref, v_ref, o_ref, do_ref, dq_ref, dk_ref, dv_ref,
             dqa_ref, car_ref, carr_ref, tot_ref, qh_ref, doh_ref, qs_ref):
        i = pl.program_id(1)

        @pl.when(i == 0)
        def _():
            dk_ref[...] = jnp.zeros_like(dk_ref)
            dv_ref[...] = jnp.zeros_like(dv_ref)

        q2 = q_ref[...]
        do2 = do_ref[...]
        head0 = _lane_is_head0()
        zero = jnp.zeros_like(q2)
        qh_ref[0] = jnp.where(head0, q2, zero)
        qh_ref[1] = jnp.where(head0, zero, q2)
        doh_ref[0] = jnp.where(head0, do2, zero)
        doh_ref[1] = jnp.where(head0, zero, do2)
        scale = jnp.asarray(QK_SCALE, BF16)
        qs_ref[...] = q2 * scale
        tri_s = _tri(True)
        tri_i = _tri(False)
        prod = do2.astype(F32) * o_ref[...]
        tot_ref[0] = jnp.sum(jnp.where(head0, prod, 0.0), axis=1, keepdims=True)
        tot_ref[1] = jnp.sum(jnp.where(head0, 0.0, prod), axis=1, keepdims=True)
        dqa_ref[...] = jnp.zeros_like(dqa_ref)
        car_ref[...] = jnp.zeros_like(car_ref)
        carr_ref[...] = jnp.zeros_like(carr_ref)

        def streams_of(kb, d, strips):
            keys = pl.ds(pl.multiple_of(kb * B_TS, B_TS), B_TS)
            k2 = k_ref[keys, :]
            return _sb_streams(d, strips, keys=keys, k2=k2, v2=v_ref[keys, :], k2s=k2 * scale)

        def run(streams):
            def scores(st):
                st["z2"] = _dot_nt(qh_ref[st["h"], st["rows"], :], st.pop("k2")) * (QK_SCALE * LOG2E)
                st["dw"] = _dot_nt(doh_ref[st["h"], st["rows"], :], st.pop("v2"))

            def logs(st):
                _sb_logs(st, st.pop("z2"))

            def suffix(st):
                st["suffix"] = _dot(st.pop("keep_bf"), tri_s)

            def weights(st):
                h = st["h"]
                suffix, dw = st.pop("suffix"), st.pop("dw")
                wb, dlog, hi, lo = [], [], [], []
                for p in range(0, B_TS, B_PIECE):
                    rows = _piece_rows(st, p)
                    car = car_ref[h, rows, :]
                    w = jnp.exp2(st["log_beta"][p:p + B_PIECE] + suffix[p:p + B_PIECE] + car)
                    if st["diag"]:
                        w = jnp.where(_strict_lower(p), w, 0.0)
                    w = w.astype(BF16)
                    dl = w.astype(F32) * dw[p:p + B_PIECE]
                    dl_hi, dl_lo = _split_bf16(dl)
                    wb.append(w)
                    dlog.append(dl)
                    hi.append(dl_hi)
                    lo.append(dl_lo)
                    car_ref[h, rows, :] = car + jnp.sum(st["log_keep"][p:p + B_PIECE], axis=1, keepdims=True)
                st["wb"], st["dlog"], st["hi"], st["lo"] = _rows_cat(wb), _rows_cat(dlog), _rows_cat(hi), _rows_cat(lo)

            def later(st):
                st["later"] = _dot(st.pop("hi"), tri_i) + _dot(st.pop("lo"), tri_i)

            def dscores(st):
                h = st["h"]
                later, dlog = st.pop("later"), st.pop("dlog")
                log_keep, log_beta = st.pop("log_keep"), st.pop("log_beta")
                dzb = []
                for p in range(0, B_TS, B_PIECE):
                    rows = _piece_rows(st, p)
                    pc = slice(p, p + B_PIECE)
                    carr = carr_ref[h, rows, :]
                    earlier = tot_ref[h, rows, :] - (later[pc] + carr)
                    dz = dlog[pc] * jnp.exp2(log_keep[pc]) - jnp.exp2(log_beta[pc]) * earlier
                    if st["diag"]:
                        dz = jnp.where(_strict_lower(p), dz, 0.0)
                    dzb.append(dz.astype(BF16))
                    carr_ref[h, rows, :] = carr + jnp.sum(dlog[pc], axis=1, keepdims=True)
                st["dzb"] = _rows_cat(dzb)

            def grads(st):
                h, rows, keys = st["h"], st["rows"], st["keys"]
                mine = head0 if h == 0 else jnp.logical_not(head0)
                dzb = st.pop("dzb")
                dqa_ref[h, rows, :] += _dot(dzb, st.pop("k2s"))
                dk_ref[keys, :] += jnp.where(mine, _dot_tn(dzb, qs_ref[rows, :]), 0.0)
                dv_ref[keys, :] += jnp.where(mine, _dot_tn(st.pop("wb"), do_ref[rows, :]), 0.0)

            _skewed(streams, [scores, logs, suffix, weights, later, dscores, grads])

        _sb_sweep(i, car_ref, streams_of, run)
        dq_ref[...] = jnp.where(head0, dqa_ref[0], dqa_ref[1]).astype(dq_ref.dtype)

    return _call_carrying(
        job, body, name=name, grid=(n_hp, T // B_TQ),
        in_specs=[pl.BlockSpec((B_TQ, LANES), lambda hp, i: (i, hp + col0)),
                  pl.BlockSpec((T, LANES), lambda hp, i: (0, hp + col0 + n_hp)),
                  pl.BlockSpec((T, LANES), lambda hp, i: (0, hp + col0 + 2 * n_hp)),
                  pl.BlockSpec((B_TQ, LANES), lambda hp, i: (i, hp)),
                  pl.BlockSpec((B_TQ, LANES), lambda hp, i: (i, hp))],
        out_specs=[pl.BlockSpec((B_TQ, LANES), lambda hp, i: (i, hp)),
                   pl.BlockSpec((T, LANES), lambda hp, i: (0, hp)),
                   pl.BlockSpec((T, LANES), lambda hp, i: (0, hp))],
        out_shape=[jax.ShapeDtypeStruct((T, WIDTH), BF16),
                   jax.ShapeDtypeStruct((T, WIDTH), F32),
                   jax.ShapeDtypeStruct((T, WIDTH), F32)],
        scratch_shapes=[pltpu.VMEM((2, B_TQ, LANES), F32), pltpu.VMEM((2, B_TQ, 1), F32),
                        pltpu.VMEM((2, B_TQ, 1), F32), pltpu.VMEM((2, B_TQ, 1), F32),
                        pltpu.VMEM((2, B_TQ, LANES), BF16), pltpu.VMEM((2, B_TQ, LANES), BF16),
                        pltpu.VMEM((B_TQ, LANES), BF16)],
        vmem_mib=56, args=(qkv, qkv, qkv, out, do))


def _gated_mix(oa_ref, ob_ref, g_ref, bg_ref, wpa_ref, wpb_ref, D):
    ya = _dot(oa_ref[...].astype(BF16), wpa_ref[...])
    yb = _dot(ob_ref[...].astype(BF16), wpb_ref[...])
    sa = jax.nn.sigmoid(g_ref[:, :D] + bg_ref[:, :D])
    sb = jax.nn.sigmoid(g_ref[:, D:] + bg_ref[:, D:])
    return ya, yb, sa, sb


def _proj_fwd(oa, ob, g, bg, wpa, wpb, wo, wl, xin, lng, lnb, layer, *, alpha, name):
    T, D = xin.shape
    tm = _tile(T, 512)
    row = lambda i: (i, 0)
    wspec = lambda r, c: pl.BlockSpec((None, r, c), lambda i: (wl, 0, 0))
    vec = lambda c: pl.BlockSpec((None, 1, c), lambda i: (layer, 0, 0))

    def body(oa_ref, ob_ref, g_ref, bg_ref, wpa_ref, wpb_ref, wo_ref, x_ref, lg_ref, lb_ref, x1_ref, r1_ref, x1b_ref):
        ya, yb, sa, sb = _gated_mix(oa_ref, ob_ref, g_ref, bg_ref, wpa_ref, wpb_ref, D)
        mix = _dot((sa * ya + sb * yb).astype(BF16), wo_ref[...])
        r1 = alpha * x_ref[...] + mix
        r1_ref[...] = r1
        x1 = _ln_fwd(r1, lg_ref[...], lb_ref[...])
        x1_ref[...] = x1
        x1b_ref[...] = x1.astype(BF16)

    return pl.pallas_call(
        body, name=name, grid=(T // tm,),
        in_specs=[pl.BlockSpec((tm, WIDTH), row), pl.BlockSpec((tm, WIDTH), row), pl.BlockSpec((tm, 2 * D), row),
                  vec(2 * D), wspec(WIDTH, D), wspec(WIDTH, D), wspec(D, D),
                  pl.BlockSpec((tm, D), row), vec(D), vec(D)],
        out_specs=[pl.BlockSpec((tm, D), row), pl.BlockSpec((tm, D), row), pl.BlockSpec((tm, D), row)],
        out_shape=[jax.ShapeDtypeStruct((T, D), F32), jax.ShapeDtypeStruct((T, D), F32),
                   jax.ShapeDtypeStruct((T, D), BF16)],
        compiler_params=_cparams(("arbitrary",), 56),
    )(oa, ob, g, bg, wpa, wpb, wo, xin, lng, lnb)


def _proj_bwd(dx1, r1, lng, oa, ob, g, bg, wpa, wpb, wo, wl, layer, *, name):
    T, D = dx1.shape
    tm = _tile(T, 512)
    row = lambda i: (i, 0)
    fixed = lambda i: (0, 0)
    wspec = lambda r, c: pl.BlockSpec((None, r, c), lambda i: (wl, 0, 0))
    vec = lambda c: pl.BlockSpec((None, 1, c), lambda i: (layer, 0, 0))

    def body(dx_ref, r1_ref, lg_ref, oa_ref, ob_ref, g_ref, bg_ref, wpa_ref, wpb_ref, wo_ref,
             dr_ref, mix_ref, dya_ref, dyb_ref, dg_ref, doa_ref, dob_ref, dlg_ref, dlb_ref, dbg_ref):
        @pl.when(pl.program_id(0) == 0)
        def _():
            dlg_ref[...] = jnp.zeros_like(dlg_ref)
            dlb_ref[...] = jnp.zeros_like(dlb_ref)
            dbg_ref[...] = jnp.zeros_like(dbg_ref)

        dx = dx_ref[...]
        dr, xhat = _ln_bwd(dx, r1_ref[...], lg_ref[...])
        dr_ref[...] = dr
        dlg_ref[...] += jnp.sum(dx * xhat, axis=0, keepdims=True)
        dlb_ref[...] += jnp.sum(dx, axis=0, keepdims=True)
        dmix = _dot_nt(dr.astype(BF16), wo_ref[...])
        ya, yb, sa, sb = _gated_mix(oa_ref, ob_ref, g_ref, bg_ref, wpa_ref, wpb_ref, D)
        mix_ref[...] = (sa * ya + sb * yb).astype(BF16)
        dya = (dmix * sa).astype(BF16)
        dyb = (dmix * sb).astype(BF16)
        dya_ref[...] = dya
        dyb_ref[...] = dyb
        dga = dmix * ya * (sa * (1.0 - sa))
        dgb = dmix * yb * (sb * (1.0 - sb))
        dg_ref[:, :D] = dga.astype(BF16)
        dg_ref[:, D:] = dgb.astype(BF16)
        dbg_ref[:, :D] += jnp.sum(dga, axis=0, keepdims=True)
        dbg_ref[:, D:] += jnp.sum(dgb, axis=0, keepdims=True)
        doa_ref[...] = _dot_nt(dya, wpa_ref[...]).astype(BF16)
        dob_ref[...] = _dot_nt(dyb, wpb_ref[...]).astype(BF16)

    return pl.pallas_call(
        body, name=name, grid=(T // tm,),
        in_specs=[pl.BlockSpec((tm, D), row), pl.BlockSpec((tm, D), row), vec(D),
                  pl.BlockSpec((tm, WIDTH), row), pl.BlockSpec((tm, WIDTH), row), pl.BlockSpec((tm, 2 * D), row),
                  vec(2 * D), wspec(WIDTH, D), wspec(WIDTH, D), wspec(D, D)],
        out_specs=[pl.BlockSpec((tm, D), row), pl.BlockSpec((tm, D), row), pl.BlockSpec((tm, D), row),
                   pl.BlockSpec((tm, D), row), pl.BlockSpec((tm, 2 * D), row),
                   pl.BlockSpec((tm, WIDTH), row), pl.BlockSpec((tm, WIDTH), row),
                   pl.BlockSpec((1, D), fixed), pl.BlockSpec((1, D), fixed), pl.BlockSpec((1, 2 * D), fixed)],
        out_shape=[jax.ShapeDtypeStruct((T, D), F32), jax.ShapeDtypeStruct((T, D), BF16),
                   jax.ShapeDtypeStruct((T, D), BF16), jax.ShapeDtypeStruct((T, D), BF16),
                   jax.ShapeDtypeStruct((T, 2 * D), BF16),
                   jax.ShapeDtypeStruct((T, WIDTH), BF16), jax.ShapeDtypeStruct((T, WIDTH), BF16),
                   jax.ShapeDtypeStruct((1, D), F32), jax.ShapeDtypeStruct((1, D), F32),
                   jax.ShapeDtypeStruct((1, 2 * D), F32)],
        compiler_params=_cparams(("arbitrary",), 56),
    )(dx1, r1, lng, oa, ob, g, bg, wpa, wpb, wo)


def _ffn_fwd(x1, wfi, wfo, wl, lng, lnb, layer, *, alpha, name, job=None):
    T, D = x1.shape
    tf = wfi.shape[-1]
    nj = wfi.shape[0] // 2
    tm = _tile(T, 512)
    vec = lambda c: pl.BlockSpec((None, 1, c), lambda i, j: (layer, 0, 0))

    def body(x_ref, wg_ref, wu_ref, wo_ref, lg_ref, lb_ref, gs_ref, us_ref, r2_ref, x2_ref, acc_ref, xb_ref):
        j = pl.program_id(1)

        @pl.when(j == 0)
        def _():
            xb_ref[...] = x_ref[...].astype(BF16)
            acc_ref[...] = jnp.zeros_like(acc_ref)

        gv = _dot(xb_ref[...], wg_ref[...])
        uv = _dot(xb_ref[...], wu_ref[...])
        gs_ref[...] = gv
        us_ref[...] = uv
        act = gv * jax.nn.sigmoid(gv) * uv
        acc_ref[...] += _dot(act.astype(BF16), wo_ref[...])

        @pl.when(j == nj - 1)
        def _():
            r2 = alpha * x_ref[...] + acc_ref[...]
            r2_ref[...] = r2
            x2_ref[...] = _ln_fwd(r2, lg_ref[...], lb_ref[...])

    return _call_carrying(
        job, body, name=name, grid=(T // tm, nj),
        in_specs=[pl.BlockSpec((tm, D), lambda i, j: (i, 0)),
                  pl.BlockSpec((None, None, D, tf), lambda i, j: (j, wl, 0, 0)),
                  pl.BlockSpec((None, None, D, tf), lambda i, j: (j + nj, wl, 0, 0)),
                  pl.BlockSpec((None, tf, D), lambda i, j: (wl, j, 0)),
                  vec(D), vec(D)],
        out_specs=[pl.BlockSpec((None, tm, tf), lambda i, j: (j, i, 0)),
                   pl.BlockSpec((None, tm, tf), lambda i, j: (j, i, 0)),
                   pl.BlockSpec((tm, D), lambda i, j: (i, 0)),
                   pl.BlockSpec((tm, D), lambda i, j: (i, 0))],
        out_shape=[jax.ShapeDtypeStruct((nj, T, tf), F32), jax.ShapeDtypeStruct((nj, T, tf), F32),
                   jax.ShapeDtypeStruct((T, D), F32), jax.ShapeDtypeStruct((T, D), F32)],
        scratch_shapes=[pltpu.VMEM((tm, D), F32), pltpu.VMEM((tm, D), BF16)],
        vmem_mib=56, args=(x1, wfi, wfi, wfo, lng, lnb))


def _ffn_bwd(dx2, r2, lng, gs, us, wfi, wfo, wl, layer, *, alpha, name, job=None):
    T, D = dx2.shape
    tf = wfi.shape[-1]
    nj = wfi.shape[0] // 2
    tm = _tile(T, 512)
    vec = lambda c: pl.BlockSpec((None, 1, c), lambda i, j: (layer, 0, 0))
    blk = lambda: pl.BlockSpec((None, tm, tf), lambda i, j: (j, i, 0))

    def body(dx_ref, r2_ref, lg_ref, gs_ref, us_ref, wg_ref, wu_ref, wo_ref,
             dr_ref, act_ref, dg_ref, du_ref, dx1_ref, dlg_ref, dlb_ref, acc_ref, drb_ref):
        i = pl.program_id(0)
        j = pl.program_id(1)

        @pl.when((i == 0) & (j == 0))
        def _():
            dlg_ref[...] = jnp.zeros_like(dlg_ref)
            dlb_ref[...] = jnp.zeros_like(dlb_ref)

        @pl.when(j == 0)
        def _():
            dx = dx_ref[...]
            dr, xhat = _ln_bwd(dx, r2_ref[...], lg_ref[...])
            dlg_ref[...] += jnp.sum(dx * xhat, axis=0, keepdims=True)
            dlb_ref[...] += jnp.sum(dx, axis=0, keepdims=True)
            drb_ref[...] = dr.astype(BF16)
            dr_ref[...] = dr.astype(BF16)
            acc_ref[...] = alpha * dr

        dact = _dot_nt(drb_ref[...], wo_ref[...])
        gv = gs_ref[...]
        uv = us_ref[...]
        s = jax.nn.sigmoid(gv)
        silu = gv * s
        act_ref[...] = (silu * uv).astype(BF16)
        dg = (dact * uv * (s * (1.0 + gv * (1.0 - s)))).astype(BF16)
        du = (dact * silu).astype(BF16)
        dg_ref[...] = dg
        du_ref[...] = du
        acc_ref[...] += _dot_nt(dg, wg_ref[...]) + _dot_nt(du, wu_ref[...])

        @pl.when(j == nj - 1)
        def _():
            dx1_ref[...] = acc_ref[...]

    return _call_carrying(
        job, body, name=name, grid=(T // tm, nj),
        in_specs=[pl.BlockSpec((tm, D), lambda i, j: (i, 0)), pl.BlockSpec((tm, D), lambda i, j: (i, 0)), vec(D),
                  blk(), blk(),
                  pl.BlockSpec((None, None, D, tf), lambda i, j: (j, wl, 0, 0)),
                  pl.BlockSpec((None, None, D, tf), lambda i, j: (j + nj, wl, 0, 0)),
                  pl.BlockSpec((None, tf, D), lambda i, j: (wl, j, 0))],
        out_specs=[pl.BlockSpec((tm, D), lambda i, j: (i, 0)), blk(), blk(), blk(),
                   pl.BlockSpec((tm, D), lambda i, j: (i, 0)),
                   pl.BlockSpec((1, D), lambda i, j: (0, 0)), pl.BlockSpec((1, D), lambda i, j: (0, 0))],
        out_shape=[jax.ShapeDtypeStruct((T, D), BF16),
                   jax.ShapeDtypeStruct((nj, T, tf), BF16), jax.ShapeDtypeStruct((nj, T, tf), BF16),
                   jax.ShapeDtypeStruct((nj, T, tf), BF16),
                   jax.ShapeDtypeStruct((T, D), F32),
                   jax.ShapeDtypeStruct((1, D), F32), jax.ShapeDtypeStruct((1, D), F32)],
        scratch_shapes=[pltpu.VMEM((tm, D), F32), pltpu.VMEM((tm, D), BF16)],
        vmem_mib=56, args=(dx2, r2, lng, gs, us, wfi, wfi, wfo))


def _loss_head(y, target, *, name):
    T, D = y.shape
    tm = _tile(T, 1024)

    def body(y_ref, t_ref, dy_ref, sq_ref):
        @pl.when(pl.program_id(0) == 0)
        def _():
            sq_ref[...] = jnp.zeros_like(sq_ref)
        err = y_ref[...] - t_ref[...]
        dy_ref[...] = err * (1.0 / D)
        sq_ref[...] += jnp.sum(err * err, axis=0, keepdims=True)

    return pl.pallas_call(
        body, name=name, grid=(T // tm,),
        in_specs=[pl.BlockSpec((tm, D), lambda i: (i, 0)), pl.BlockSpec((tm, D), lambda i: (i, 0))],
        out_specs=[pl.BlockSpec((tm, D), lambda i: (i, 0)), pl.BlockSpec((1, D), lambda i: (0, 0))],
        out_shape=[jax.ShapeDtypeStruct((T, D), F32), jax.ShapeDtypeStruct((1, D), F32)],
        compiler_params=_cparams(("arbitrary",)),
    )(y, target)


def _my_place():
    return lax.axis_index("x"), lax.axis_index("y"), lax.axis_index("c")


def _peer(place, k):
    x, y, c = place
    return (1 - x if k & 4 else x, 1 - y if k & 2 else y, 1 - c if k & 1 else c)


def _logical(place):
    x, y, c = place
    return 4 * x + 2 * y + c


def _block_of(ref, mode, idx):
    if mode == "blk":
        return ref.at[idx]
    if mode == "col":
        size = ref.shape[2] // N_DEV
        return ref.at[:, :, pl.ds(pl.multiple_of(idx * size, size), size)]
    size = ref.shape[1] // N_DEV
    return ref.at[:, pl.ds(pl.multiple_of(idx * size, size), size), :]


def _full_shape(shard, mode):
    if mode == "blk":
        return (N_DEV,) + shard.shape
    if mode == "col":
        return shard.shape[:2] + (N_DEV * shard.shape[2],)
    return (shard.shape[0], N_DEV * shard.shape[1], shard.shape[2])


class _Exchange:
    def __init__(self, arrays, out_shape, build):
        self.arrays = list(arrays)
        self.out_shape = list(out_shape)
        self.build = build

    def scratch(self):
        n = len(self.arrays)
        return [pltpu.SemaphoreType.DMA((n * N_DEV,)), pltpu.SemaphoreType.DMA((n * N_DEV,)),
                pltpu.SemaphoreType.DMA((n,))]

    def start(self, ins, outs, sems):
        for cp in self.build(ins, outs, *sems):
            cp.start()

    def wait(self, ins, outs, sems):
        for cp in self.build(ins, outs, *sems):
            cp.wait()

    def run(self, name):
        n_in, n_out = len(self.arrays), len(self.out_shape)
        hbm = pl.BlockSpec(memory_space=pltpu.HBM)

        def body(*refs):
            ins, outs, sems = refs[:n_in], refs[n_in:n_in + n_out], refs[n_in + n_out:]
            self.start(ins, outs, sems)
            self.wait(ins, outs, sems)

        return pl.pallas_call(
            body, name=name, in_specs=[hbm] * n_in, out_specs=[hbm] * n_out,
            out_shape=self.out_shape, scratch_shapes=self.scratch(),
        )(*self.arrays)


def _copies_to_all(src_of, dst_of, n, send, recv, local):
    me = _my_place()
    copies = []
    for a in range(n):
        copies.append(pltpu.make_async_copy(src_of(a, _logical(me)), dst_of(a), local.at[a]))
        for k in range(1, N_DEV):
            peer = _peer(me, k)
            copies.append(pltpu.make_async_remote_copy(
                src_ref=src_of(a, _logical(peer)), dst_ref=dst_of(a),
                send_sem=send.at[a * N_DEV + k], recv_sem=recv.at[a * N_DEV + k],
                device_id=peer, device_id_type=MESH))
    return copies


def _gather_job(shards, modes):
    def build(ins, outs, send, recv, local):
        my_id = _logical(_my_place())
        return _copies_to_all(lambda a, dev: ins[a], lambda a: _block_of(outs[a], modes[a], my_id),
                              len(shards), send, recv, local)

    return _Exchange(shards, [jax.ShapeDtypeStruct(_full_shape(s, m), s.dtype) for s, m in zip(shards, modes)], build)


def _grad_block(ref, mode, idx):
    if mode == "blk":
        return ref.at[idx]
    if mode == "col":
        size = ref.shape[1] // N_DEV
        return ref.at[:, pl.ds(pl.multiple_of(idx * size, size), size)]
    size = ref.shape[0] // N_DEV
    return ref.at[pl.ds(pl.multiple_of(idx * size, size), size), :]


def _grad_shard_shape(g, mode):
    if mode == "blk":
        return g.shape[1:]
    if mode == "col":
        return (g.shape[0], g.shape[1] // N_DEV)
    return (g.shape[0] // N_DEV, g.shape[1])


def _grads_job(groups, modes):
    flat = [(g, w, l) for w, per_w in enumerate(groups) for l, g in enumerate(per_w)]

    def build(ins, outs, send, recv, local):
        my_id = _logical(_my_place())
        return _copies_to_all(lambda a, dev: _grad_block(ins[a], modes[flat[a][1]], dev),
                              lambda a: outs[flat[a][1]].at[my_id, flat[a][2]],
                              len(flat), send, recv, local)

    out_shape = [jax.ShapeDtypeStruct((N_DEV, len(per_w)) + _grad_shard_shape(per_w[0], m), per_w[0].dtype)
                 for per_w, m in zip(groups, modes)]
    return _Exchange([g for g, _, _ in flat], out_shape, build)


def _adamw(w, g, m, v):
    m = ADAM_B1 * m + (1.0 - ADAM_B1) * g
    v = ADAM_B2 * v + (1.0 - ADAM_B2) * (g * g)
    m_hat = m / (1.0 - ADAM_B1 ** ADAM_STEP)
    v_hat = v / (1.0 - ADAM_B2 ** ADAM_STEP)
    delta = -ADAM_LR * (m_hat / (jnp.sqrt(v_hat) + ADAM_EPS) + ADAM_WD * w)
    return delta, m, v


def _sum_slots_adamw(slots, w, m, v, *, name):
    R, C = w.shape
    tr = _tile(R, 256)

    def body(s_ref, w_ref, m_ref, v_ref, g_out, d_out, m_out, v_out):
        g = s_ref[0].astype(F32)
        for s in range(1, N_DEV):
            g = g + s_ref[s].astype(F32)
        delta, m_new, v_new = _adamw(w_ref[...], g, m_ref[...], v_ref[...])
        g_out[...] = g
        d_out[...] = delta
        m_out[...] = m_new
        v_out[...] = v_new

    spec = pl.BlockSpec((tr, C), lambda i: (i, 0))
    return pl.pallas_call(
        body, name=name, grid=(R // tr,),
        in_specs=[pl.BlockSpec((N_DEV, tr, C), lambda i: (0, i, 0)), spec, spec, spec],
        out_specs=[spec] * 4,
        out_shape=[jax.ShapeDtypeStruct((R, C), F32)] * 4,
        compiler_params=_cparams(("parallel",)),
    )(slots, w, m, v)


def _small_allreduce_adamw(g, w, m, v, *, name):
    R = g.shape[0]
    vmem = pl.BlockSpec(memory_space=pltpu.VMEM)

    def body(g_ref, w_ref, m_ref, v_ref, g_out, d_out, m_out, v_out, slots, send, recv):
        me = _my_place()
        my_id = _logical(me)
        slots[my_id] = g_ref[...]
        copies = []
        for k in range(1, N_DEV):
            cp = pltpu.make_async_remote_copy(
                src_ref=g_ref, dst_ref=slots.at[my_id], send_sem=send.at[k], recv_sem=recv.at[k],
                device_id=_peer(me, k), device_id_type=MESH)
            cp.start()
            copies.append(cp)
        for cp in copies:
            cp.wait()
        total = slots[0]
        for s in range(1, N_DEV):
            total = total + slots[s]
        delta, m_new, v_new = _adamw(w_ref[...], total, m_ref[...], v_ref[...])
        g_out[...] = total
        d_out[...] = delta
        m_out[...] = m_new
        v_out[...] = v_new

    return pl.pallas_call(
        body, name=name,
        in_specs=[vmem] * 4, out_specs=[vmem] * 4,
        out_shape=[jax.ShapeDtypeStruct((R, LANES), F32)] * 4,
        scratch_shapes=[pltpu.VMEM((N_DEV, R, LANES), F32),
                        pltpu.SemaphoreType.DMA((N_DEV,)), pltpu.SemaphoreType.DMA((N_DEV,))],
    )(g, w, m, v)


def _pack(parts):
    flat = jnp.concatenate([p.reshape(-1) for p in parts])
    rows = -(-flat.shape[0] // (8 * LANES)) * 8
    return jnp.pad(flat, (0, rows * LANES - flat.shape[0])).reshape(rows, LANES)


def _unpack(packed, like):
    flat = packed.reshape(-1)
    out, pos = [], 0
    for p in like:
        out.append(flat[pos:pos + p.size].reshape(p.shape))
        pos += p.size
    return out


def kernel(x, w_in, b_gate, rel_bias, w_proj_a, w_proj_b, w_out, ln1_g, ln1_b, w_ffn_in, w_ffn_out, ln2_g, ln2_b, loss_target, m_w_in, m_b_gate, m_rel_bias, m_w_proj_a, m_w_proj_b, m_w_out, m_ln1_g, m_ln1_b, m_w_ffn_in, m_w_ffn_out, m_ln2_g, m_ln2_b, v_w_in, v_b_gate, v_rel_bias, v_w_proj_a, v_w_proj_b, v_w_out, v_ln1_g, v_ln1_b, v_w_ffn_in, v_w_ffn_out, v_ln2_g, v_ln2_b):
    L = w_in.shape[0]
    T, D = x.shape[1], x.shape[2]
    alpha = float((2 * L) ** 0.25)
    n_qkv = 6 * WIDTH

    big = [w_in, w_proj_a, w_proj_b, w_out, w_ffn_in, w_ffn_out]
    kinds = ["in", "pa", "pb", "o", "fi", "fo"]
    modes = ["col", "col", "col", "row", "blk", "row"]
    mode_of = dict(zip(kinds, modes))
    w_bf = dict(zip(kinds, [w.astype(BF16) for w in big]))

    def gather_of(ks, l):
        return _gather_job([w_bf[k][l:l + 1] for k in ks], [mode_of[k] for k in ks])

    W = [dict() for _ in range(L)]
    (W[0]["in"],) = gather_of(["in"], 0).run("gather_w_in_first")
    vec3 = lambda a: a[:, None, :]
    bg3, l1g, l1b, l2g, l2b = vec3(b_gate), vec3(ln1_g), vec3(ln1_b), vec3(ln2_g), vec3(ln2_b)
    b_col0 = 3 * WIDTH // LANES

    h = x[0]
    saved = []
    for l in range(L):
        ahead = l + 1 < L
        soon = ["pa", "pb", "o", "fo"]
        (qkv, gates), got = _in_proj(h, W[l]["in"], 0, n_qkv=n_qkv, name=f"in_proj_{l}",
                                     job=gather_of(soon, 0) if l == 0 else None)
        W[l].update(zip(soon, got))
        kvpad = jnp.pad(qkv[:, WIDTH:3 * WIDTH], ((A_WIN - A_TQ, 0), (0, 0)))
        bias = _toeplitz_bias(rel_bias[l])
        oa, got = _attn_a_fwd(qkv, kvpad, bias, name=f"attn_a_fwd_{l}", job=gather_of(["fi"], 0) if l == 0 else None)
        W[l].update(zip(["fi"], got))
        early = ["in", "pa", "pb", "o"]
        ob, got = _attn_b_fwd(qkv, col0=b_col0, name=f"attn_b_fwd_{l}", job=gather_of(early, l + 1) if ahead else None)
        W[l + 1 if ahead else l].update(zip(early, got))
        x1, r1, x1b = _proj_fwd(oa, ob, gates, bg3, W[l]["pa"], W[l]["pb"], W[l]["o"], 0, h, l1g, l1b, l,
                           alpha=alpha, name=f"proj_fwd_{l}")
        (gs, us, r2, x2), got = _ffn_fwd(x1, W[l]["fi"], W[l]["fo"], 0, l2g, l2b, l, alpha=alpha, name=f"ffn_fwd_{l}",
                                         job=gather_of(["fi", "fo"], l + 1) if ahead else None)
        W[l + 1 if ahead else l].update(zip(["fi", "fo"], got))
        saved.append((h, qkv, gates, kvpad, bias, oa, ob, x1b, r1, gs, us, r2))
        h = x2

    d_h, sq = _loss_head(h, loss_target[0], name="loss_head")
    loss = lax.psum((0.5 / D) * jnp.sum(sq), ("x", "y", "c"))

    g_bg, g_rb, g_l1g, g_l1b, g_l2g, g_l2b = ([None] * L for _ in range(6))
    slot = {k: [None] * L for k in kinds}

    def exchange_of(ks, grads):
        return _grads_job([[g] for g in grads], [mode_of[k] for k in ks])

    w_in_above = None
    for l in reversed(range(L)):
        xin, qkv, gates, kvpad, bias, oa, ob, x1b, r1, gs, us, r2 = saved[l]
        (dr2, act, dgt, dup, dx1, g_l2g[l], g_l2b[l]), got = _ffn_bwd(
            d_h, r2, l2g, gs, us, W[l]["fi"], W[l]["fo"], 0, l, alpha=alpha, name=f"ffn_bwd_{l}", job=w_in_above)
        if w_in_above is not None:
            (slot["in"][l + 1],) = got
        g_fo = _mm_tn_blocked_a(act, dr2, name=f"grad_w_ffn_out_{l}").reshape(-1, D)
        g_fi = jnp.concatenate(_mm_tn_blocked_pair(x1b, dgt, dup, name=f"grad_w_ffn_in_{l}"), axis=0)
        (dr1, mixin, dya, dyb, dgates, doa, dob, g_l1g[l], g_l1b[l], g_bg[l]) = _proj_bwd(
            dx1, r1, l1g, oa, ob, gates, bg3, W[l]["pa"], W[l]["pb"], W[l]["o"], 0, l, name=f"proj_bwd_{l}")
        g_o = _mm_tn(mixin, dr1, tm=_tile(D, 1024), tn=_tile(D, 1024), name=f"grad_w_out_{l}")
        g_pa = _mm_tn(oa, dya, tm=WIDTH, tn=_tile(D, 1024), name=f"grad_w_proj_a_{l}")
        g_pb = _mm_tn(ob, dyb, tm=WIDTH, tn=_tile(D, 1024), name=f"grad_w_proj_b_{l}")
        (dqa, dka, dva, dbias), (slot["fi"][l], slot["fo"][l]) = _attn_a_bwd(
            qkv, kvpad, bias, doa, name=f"attn_a_bwd_{l}", job=exchange_of(["fi", "fo"], [g_fi, g_fo]))
        g_rb[l] = _toeplitz_bias_grad(dbias)
        (dqb, dkb, dvb), (slot["pa"][l], slot["pb"][l], slot["o"][l]) = _attn_b_bwd(
            qkv, ob, dob, col0=b_col0, name=f"attn_b_bwd_{l}", job=exchange_of(["pa", "pb", "o"], [g_pa, g_pb, g_o]))
        pad = A_WIN - A_TQ
        d_pre = jnp.concatenate([dqa, dka[pad:].astype(BF16), dva[pad:].astype(BF16),
                                 dqb, dkb.astype(BF16), dvb.astype(BF16), dgates], axis=1)
        g_in = _mm_tn(xin, d_pre, tm=D, tn=4 * w_in.shape[2], name=f"grad_w_in_{l}")
        w_in_above = exchange_of(["in"], [g_in])
        d_h, got = _mm_nt_add(d_pre, W[l]["in"], 0, dr1, alpha, name=f"grad_x_{l}", job=w_in_above if l == 0 else None)
        if l == 0:
            (slot["in"][0],) = got
    grad_x = d_h[None]

    slots = [jnp.concatenate(slot[k], axis=1) if L > 1 else slot[k][0] for k in kinds]
    moments_m = [m_w_in, m_w_proj_a, m_w_proj_b, m_w_out, m_w_ffn_in, m_w_ffn_out]
    moments_v = [v_w_in, v_w_proj_a, v_w_proj_b, v_w_out, v_w_ffn_in, v_w_ffn_out]
    names = ["w_in", "w_proj_a", "w_proj_b", "w_out", "w_ffn_in", "w_ffn_out"]
    big_out = {}
    for nm, s, w, m, v in zip(names, slots, big, moments_m, moments_v):
        two = lambda a: a.reshape(-1, a.shape[-1])
        res = _sum_slots_adamw(s.reshape(N_DEV, -1, s.shape[-1]), two(w), two(m), two(v), name=f"adamw_{nm}")
        big_out[nm] = [r.reshape(w.shape) for r in res]

    small_w = [b_gate, rel_bias, ln1_g, ln1_b, ln2_g, ln2_b]
    small_g = [jnp.stack(g) for g in (g_bg, g_rb, g_l1g, g_l1b, g_l2g, g_l2b)]
    small_m = [m_b_gate, m_rel_bias, m_ln1_g, m_ln1_b, m_ln2_g, m_ln2_b]
    small_v = [v_b_gate, v_rel_bias, v_ln1_g, v_ln1_b, v_ln2_g, v_ln2_b]
    res = _small_allreduce_adamw(_pack(small_g), _pack(small_w), _pack(small_m), _pack(small_v),
                                 name="allreduce_small_adamw")
    small_names = ["b_gate", "rel_bias", "ln1_g", "ln1_b", "ln2_g", "ln2_b"]
    small_out = {nm: [] for nm in small_names}
    for packed in res:
        for nm, arr in zip(small_names, _unpack(packed, small_w)):
            small_out[nm].append(arr)

    order = ["w_in", "b_gate", "rel_bias", "w_proj_a", "w_proj_b", "w_out", "ln1_g", "ln1_b",
             "w_ffn_in", "w_ffn_out", "ln2_g", "ln2_b"]
    every = {**big_out, **small_out}
    outs = [loss, grad_x]
    for kind in range(4):
        outs += [every[nm][kind] for nm in order]
    return tuple(outs)
```

```python
import functools
import math

import jax
import jax.numpy as jnp
import numpy as np
from jax import lax
from jax.experimental import pallas as pl
from jax.experimental.pallas import tpu as pltpu

F32 = jnp.float32
BF16 = jnp.bfloat16

HEAD_DIM = 64
CHUNK = 64
LEFT_CHUNKS = 8
REL_CLIP = 256
N_REL = 2 * REL_CLIP + 1
WIDTH = 512
LANES = 128
A_TQ = 256
A_WIN = A_TQ + LEFT_CHUNKS * CHUNK
A_STRIP = 128
B_TQ = 512
B_TS = 256
B_PIECE = 64
B_DEAD = -160.0
LN_EPS = 1e-5
QK_SCALE = 1.0 / math.sqrt(HEAD_DIM)
LOG2E = 1.4426950408889634
NEG = -1e30

ADAM_LR = 0.001
ADAM_B1 = 0.9
ADAM_B2 = 0.999
ADAM_EPS = 1e-08
ADAM_WD = 0.01
ADAM_STEP = 10

N_DEV = 8
MESH = pl.DeviceIdType.MESH
MIB = 1024 * 1024


def _cparams(sem=None, vmem_mib=48):
    return pltpu.CompilerParams(dimension_semantics=sem, vmem_limit_bytes=vmem_mib * MIB)


def _dot(a, b):
    return jnp.dot(a, b, preferred_element_type=F32)


def _dot_nt(a, b):
    return lax.dot_general(a, b, (((1,), (1,)), ((), ())), preferred_element_type=F32)


def _dot_tn(a, b):
    return lax.dot_general(a, b, (((0,), (0,)), ((), ())), preferred_element_type=F32)


def _tile(n, pref):
    if n <= pref:
        return n
    for t in range(pref - pref % 8, 0, -8):
        if n % t == 0:
            return t
    raise ValueError((n, pref))


def _in_proj(a, w, layer, *, n_qkv, name, job=None):
    M, K = a.shape
    N = w.shape[2]
    tm = _tile(M, 1024)
    tn = 1024
    assert n_qkv % tn == 0 and (N - n_qkv) % tn == 0
    n_q = n_qkv // tn

    def body(a_ref, w_ref, q_ref, g_ref, ab_ref):
        j = pl.program_id(1)

        @pl.when(j == 0)
        def _():
            ab_ref[...] = a_ref[...].astype(BF16)

        res = _dot(ab_ref[...], w_ref[...])

        @pl.when(j < n_q)
        def _():
            q_ref[...] = res.astype(BF16)

        @pl.when(j >= n_q)
        def _():
            g_ref[...] = res

    return _call_carrying(
        job, body, name=name, grid=(M // tm, N // tn),
        in_specs=[pl.BlockSpec((tm, K), lambda i, j: (i, 0)),
                  pl.BlockSpec((None, K, tn), lambda i, j: (layer, 0, j))],
        out_specs=[pl.BlockSpec((tm, tn), lambda i, j: (i, jnp.minimum(j, n_q - 1))),
                   pl.BlockSpec((tm, tn), lambda i, j: (i, jnp.maximum(j - n_q, 0)))],
        out_shape=[jax.ShapeDtypeStruct((M, n_qkv), BF16), jax.ShapeDtypeStruct((M, N - n_qkv), F32)],
        scratch_shapes=[pltpu.VMEM((tm, K), BF16)], vmem_mib=48, args=(a, w))


def _mm_nt_add(a, w, layer, add, add_scale, *, name, job=None):
    M, K = a.shape
    N = w.shape[1]
    tm = _tile(M, 1024)
    tk = _tile(K, 1024)

    def body(a_ref, w_ref, add_ref, o_ref):
        @pl.when(pl.program_id(1) == 0)
        def _():
            o_ref[...] = add_scale * add_ref[...]
        o_ref[...] += _dot_nt(a_ref[...], w_ref[...])

    (out,), rode = _call_carrying(
        job, body, name=name, grid=(M // tm, K // tk),
        in_specs=[pl.BlockSpec((tm, tk), lambda i, k: (i, k)),
                  pl.BlockSpec((None, N, tk), lambda i, k: (layer, 0, k)),
                  pl.BlockSpec((tm, N), lambda i, k: (i, 0))],
        out_specs=[pl.BlockSpec((tm, N), lambda i, k: (i, 0))],
        out_shape=[jax.ShapeDtypeStruct((M, N), F32)],
        scratch_shapes=[], vmem_mib=48, args=(a, w, add))
    return out, rode


def _tn_body(k_axis, n_k):
    def body(a_ref, b_ref, o_ref, acc_ref):
        k = pl.program_id(k_axis)

        @pl.when(k == 0)
        def _():
            acc_ref[...] = jnp.zeros_like(acc_ref)
        acc_ref[...] += _dot_tn(a_ref[...].astype(BF16), b_ref[...].astype(BF16))

        @pl.when(k == n_k - 1)
        def _():
            o_ref[...] = acc_ref[...].astype(o_ref.dtype)
    return body


def _mm_tn(a, b, *, tm, tn, name):
    T, M = a.shape
    N = b.shape[1]
    tk = _tile(T, 512)
    return pl.pallas_call(
        _tn_body(2, T // tk), name=name, grid=(M // tm, N // tn, T // tk),
        in_specs=[pl.BlockSpec((tk, tm), lambda i, j, k: (k, i)),
                  pl.BlockSpec((tk, tn), lambda i, j, k: (k, j))],
        out_specs=pl.BlockSpec((tm, tn), lambda i, j, k: (i, j)),
        out_shape=jax.ShapeDtypeStruct((M, N), BF16),
        scratch_shapes=[pltpu.VMEM((tm, tn), F32)],
        compiler_params=_cparams(("parallel", "parallel", "arbitrary")),
    )(a, b)


def _mm_tn_blocked_pair(a, b1, b2, *, name):
    T, M = a.shape
    S, _, N = b1.shape
    tk = _tile(T, 1024)
    n_k = T // tk

    def body(a_ref, b1_ref, b2_ref, o1_ref, o2_ref, acc1_ref, acc2_ref):
        k = pl.program_id(1)

        @pl.when(k == 0)
        def _():
            acc1_ref[...] = jnp.zeros_like(acc1_ref)
            acc2_ref[...] = jnp.zeros_like(acc2_ref)

        a_t = a_ref[...].astype(BF16)
        acc1_ref[...] += _dot_tn(a_t, b1_ref[...])
        acc2_ref[...] += _dot_tn(a_t, b2_ref[...])

        @pl.when(k == n_k - 1)
        def _():
            o1_ref[...] = acc1_ref[...].astype(o1_ref.dtype)
            o2_ref[...] = acc2_ref[...].astype(o2_ref.dtype)

    blk = lambda: pl.BlockSpec((None, tk, N), lambda s, k: (s, k, 0))
    out = lambda: pl.BlockSpec((None, M, N), lambda s, k: (s, 0, 0))
    return pl.pallas_call(
        body, name=name, grid=(S, n_k),
        in_specs=[pl.BlockSpec((tk, M), lambda s, k: (k, 0)), blk(), blk()],
        out_specs=[out(), out()],
        out_shape=[jax.ShapeDtypeStruct((S, M, N), BF16)] * 2,
        scratch_shapes=[pltpu.VMEM((M, N), F32), pltpu.VMEM((M, N), F32)],
        compiler_params=_cparams(("parallel", "arbitrary")),
    )(a, b1, b2)


def _mm_tn_blocked_a(a, b, *, name):
    S, T, M = a.shape
    N = b.shape[1]
    tk = _tile(T, 512)
    return pl.pallas_call(
        _tn_body(1, T // tk), name=name, grid=(S, T // tk),
        in_specs=[pl.BlockSpec((None, tk, M), lambda s, k: (s, k, 0)),
                  pl.BlockSpec((tk, N), lambda s, k: (k, 0))],
        out_specs=pl.BlockSpec((None, M, N), lambda s, k: (s, 0, 0)),
        out_shape=jax.ShapeDtypeStruct((S, M, N), BF16),
        scratch_shapes=[pltpu.VMEM((M, N), F32)],
        compiler_params=_cparams(("parallel", "arbitrary")),
    )(a, b)


def _ln_fwd(r, g, b):
    mu = jnp.mean(r, axis=-1, keepdims=True)
    xc = r - mu
    var = jnp.mean(xc * xc, axis=-1, keepdims=True)
    return xc * lax.rsqrt(var + LN_EPS) * g + b


def _ln_bwd(dy, r, g):
    mu = jnp.mean(r, axis=-1, keepdims=True)
    xc = r - mu
    var = jnp.mean(xc * xc, axis=-1, keepdims=True)
    rstd = lax.rsqrt(var + LN_EPS)
    xhat = xc * rstd
    dxh = dy * g
    m1 = jnp.mean(dxh, axis=-1, keepdims=True)
    m2 = jnp.mean(dxh * xhat, axis=-1, keepdims=True)
    return rstd * (dxh - m1 - xhat * m2), xhat


def _lane_is_head0():
    return lax.broadcasted_iota(jnp.int32, (1, LANES), 1) < HEAD_DIM


def _band_shape():
    a = np.arange(A_TQ)[:, None] // CHUNK
    b = np.arange(A_WIN)[None, :] // CHUNK
    return (b >= a) & (b <= a + LEFT_CHUNKS)


def _band_streams(strip):
    return [dict(h=h, n=strip, rows=pl.ds(r0, strip)) for h in range(2) for r0 in range(0, A_TQ, strip)]


def _band_scores(st, i, qh_ref, k2, bias_ref):
    s = _dot_nt(qh_ref[st["h"], st["rows"], :], k2) * QK_SCALE + bias_ref[st["h"], st["rows"], :]
    c = lax.broadcasted_iota(jnp.int32, (st["n"], A_WIN), 1)
    st["s"] = jnp.where(c >= LEFT_CHUNKS * CHUNK - i * A_TQ, s, NEG)


def _band_softmax(st):
    s = st.pop("s")
    e = jnp.exp(s - jnp.max(s, axis=1, keepdims=True))
    st["p"] = e / jnp.sum(e, axis=1, keepdims=True)


def _attn_a_fwd(qkv, kvpad, bias, *, name, job=None):
    T = qkv.shape[0]
    n_hp = WIDTH // LANES

    def body(q_ref, k_ref, v_ref, bias_ref, o_ref, qh_ref, acc_ref):
        i = pl.program_id(1)
        row0 = pl.multiple_of(i * A_TQ, A_TQ)
        q2 = q_ref[...]
        k2 = k_ref[pl.ds(row0, A_WIN), :]
        v2 = v_ref[pl.ds(row0, A_WIN), :]
        head0 = _lane_is_head0()
        qh_ref[0] = jnp.where(head0, q2, jnp.zeros_like(q2))
        qh_ref[1] = jnp.where(head0, jnp.zeros_like(q2), q2)

        def scores(st):
            _band_scores(st, i, qh_ref, k2, bias_ref)

        def values(st):
            acc_ref[st["h"], st["rows"], :] = _dot(st.pop("p").astype(BF16), v2)

        _skewed(_band_streams(A_STRIP), [scores, _band_softmax, values])
        o_ref[...] = jnp.where(head0, acc_ref[0], acc_ref[1]).astype(o_ref.dtype)

    (out,), rode = _call_carrying(
        job, body, name=name, grid=(n_hp, T // A_TQ),
        in_specs=[pl.BlockSpec((A_TQ, LANES), lambda hp, i: (i, hp)),
                  pl.BlockSpec((T + A_WIN - A_TQ, LANES), lambda hp, i: (0, hp)),
                  pl.BlockSpec((T + A_WIN - A_TQ, LANES), lambda hp, i: (0, hp + n_hp)),
                  pl.BlockSpec((2, A_TQ, A_WIN), lambda hp, i: (hp, 0, 0))],
        out_specs=[pl.BlockSpec((A_TQ, LANES), lambda hp, i: (i, hp))],
        out_shape=[jax.ShapeDtypeStruct((T, WIDTH), BF16)],
        scratch_shapes=[pltpu.VMEM((2, A_TQ, LANES), BF16), pltpu.VMEM((2, A_TQ, LANES), F32)],
        vmem_mib=48, args=(qkv, kvpad, kvpad, bias))
    return out, rode


def _attn_a_bwd(qkv, kvpad, bias, do, *, name, job=None):
    T = qkv.shape[0]
    TP = T + A_WIN - A_TQ
    n_hp = WIDTH // LANES

    def body(q_ref, k_ref, v_ref, bias_ref, do_ref, dq_ref, dk_ref, dv_ref, db_ref, qh_ref, doh_ref, dqa_ref):
        i = pl.program_id(1)

        @pl.when(i == 0)
        def _():
            dk_ref[...] = jnp.zeros_like(dk_ref)
            dv_ref[...] = jnp.zeros_like(dv_ref)
            db_ref[...] = jnp.zeros_like(db_ref)

        row0 = pl.multiple_of(i * A_TQ, A_TQ)
        window = pl.ds(row0, A_WIN)
        q2 = q_ref[...]
        do2 = do_ref[...]
        k2 = k_ref[window, :]
        v2 = v_ref[window, :]
        head0 = _lane_is_head0()
        zero = jnp.zeros_like(q2)
        qh_ref[0] = jnp.where(head0, q2, zero)
        qh_ref[1] = jnp.where(head0, zero, q2)
        doh_ref[0] = jnp.where(head0, do2, zero)
        doh_ref[1] = jnp.where(head0, zero, do2)
        dk = [[], []]
        dv = [[], []]

        def scores(st):
            _band_scores(st, i, qh_ref, k2, bias_ref)
            st["dp"] = _dot_nt(doh_ref[st["h"], st["rows"], :], v2)

        def dscores(st):
            _band_softmax(st)
            p, dp = st.pop("p"), st.pop("dp")
            ds = p * (dp - jnp.sum(p * dp, axis=1, keepdims=True))
            db_ref[st["h"], st["rows"], :] += ds
            st["dsb"] = (ds * QK_SCALE).astype(BF16)
            st["pb"] = p.astype(BF16)

        def grads(st):
            h, rows = st["h"], st["rows"]
            dsb = st.pop("dsb")
            dqa_ref[h, rows, :] = _dot(dsb, k2)
            dk[h].append(_dot_tn(dsb, q_ref[rows, :]))
            dv[h].append(_dot_tn(st.pop("pb"), do_ref[rows, :]))

        _skewed(_band_streams(A_TQ), [scores, dscores, grads])
        dq_ref[...] = jnp.where(head0, dqa_ref[0], dqa_ref[1]).astype(dq_ref.dtype)
        dk_ref[window, :] += jnp.where(head0, sum(dk[0]), sum(dk[1]))
        dv_ref[window, :] += jnp.where(head0, sum(dv[0]), sum(dv[1]))

    return _call_carrying(
        job, body, name=name, grid=(n_hp, T // A_TQ),
        in_specs=[pl.BlockSpec((A_TQ, LANES), lambda hp, i: (i, hp)),
                  pl.BlockSpec((TP, LANES), lambda hp, i: (0, hp)),
                  pl.BlockSpec((TP, LANES), lambda hp, i: (0, hp + n_hp)),
                  pl.BlockSpec((2, A_TQ, A_WIN), lambda hp, i: (hp, 0, 0)),
                  pl.BlockSpec((A_TQ, LANES), lambda hp, i: (i, hp))],
        out_specs=[pl.BlockSpec((A_TQ, LANES), lambda hp, i: (i, hp)),
                   pl.BlockSpec((TP, LANES), lambda hp, i: (0, hp)),
                   pl.BlockSpec((TP, LANES), lambda hp, i: (0, hp)),
                   pl.BlockSpec((2, A_TQ, A_WIN), lambda hp, i: (hp, 0, 0))],
        out_shape=[jax.ShapeDtypeStruct((T, WIDTH), BF16),
                   jax.ShapeDtypeStruct((TP, WIDTH), F32),
                   jax.ShapeDtypeStruct((TP, WIDTH), F32),
                   jax.ShapeDtypeStruct((WIDTH // HEAD_DIM, A_TQ, A_WIN), F32)],
        scratch_shapes=[pltpu.VMEM((2, A_TQ, LANES), BF16), pltpu.VMEM((2, A_TQ, LANES), BF16),
                        pltpu.VMEM((2, A_TQ, LANES), F32)],
        vmem_mib=56, args=(qkv, kvpad, kvpad, bias, do))


def _toeplitz_bias(rb):
    H = rb.shape[0]
    span = A_TQ + A_WIN - 1
    n_tail = span - (N_REL - 1)
    ext = jnp.concatenate([rb[:, 1:], jnp.broadcast_to(rb[:, N_REL - 1:], (H, n_tail))], axis=1)
    rev = jnp.pad(ext[:, ::-1], ((0, 0), (0, 1)))
    flat = jnp.broadcast_to(rev[:, None, :], (H, A_TQ, span + 1)).reshape(H, A_TQ * (span + 1))
    skew = flat[:, :A_TQ * span].reshape(H, A_TQ, span)
    return jnp.where(_band_shape()[None], skew[:, :, A_TQ - 1:A_TQ - 1 + A_WIN], NEG)


def _toeplitz_bias_grad(db):
    H = db.shape[0]
    span = A_TQ + A_WIN - 1
    d_skew = jnp.pad(db, ((0, 0), (0, 0), (A_TQ - 1, span - (A_TQ - 1) - A_WIN)))
    d_flat = jnp.pad(d_skew.reshape(H, A_TQ * span), ((0, 0), (0, A_TQ)))
    g_ext = jnp.sum(d_flat.reshape(H, A_TQ, span + 1), axis=1)[:, :span][:, ::-1]
    last = g_ext[:, N_REL - 2] + jnp.sum(g_ext[:, N_REL - 1:], axis=1)
    return jnp.concatenate([jnp.zeros((H, 1), F32), g_ext[:, :N_REL - 2], last[:, None]], axis=1)


def _split_bf16(x):
    hi = x.astype(BF16)
    lo = (x - hi.astype(F32)).astype(BF16)
    return hi, lo


def _sb_streams(d, strips=None, **tile):
    out = []
    for h in range(2):
        for r in (range(B_TQ // B_TS) if strips is None else strips):
            if d is not None and d > r:
                continue
            out.append(dict(h=h, r=r, rows=pl.ds(r * B_TS, B_TS), diag=(d is not None and d == r), **tile))
    return out


def _sb_sweep(i, car_ref, streams_of, run):
    sub = B_TQ // B_TS
    run([st for d in reversed(range(sub)) for st in streams_of(i * sub + d, d, None)])

    def alive(c):
        return (c[0] < i * sub) & (c[1] > B_DEAD)

    def step(c):
        kb = i * sub - 1 - c[0]
        if sub > 1:
            lower_alive = jnp.max(car_ref[:, B_TS:, :]) > B_DEAD
            lax.cond(lower_alive, lambda: run(streams_of(kb, None, None)), lambda: run(streams_of(kb, None, [0])))
        else:
            run(streams_of(kb, None, None))
        return c[0] + 1, jnp.max(car_ref[...])

    lax.while_loop(alive, step, (jnp.int32(0), jnp.float32(0.0)))


def _piece_rows(st, p):
    return pl.ds(st["r"] * B_TS + p, B_PIECE)


def _rows_cat(parts):
    return jnp.concatenate(parts, axis=0)


def _skewed(streams, stages):
    for t in range(len(streams) + len(stages) - 1):
        for s, st in enumerate(streams):
            if 0 <= t - s < len(stages):
                stages[t - s](st)


def _sb_logs(st, z2):
    log_beta, log_keep, keep_bf = [], [], []
    for p in range(0, B_TS, B_PIECE):
        z = z2[p:p + B_PIECE]
        lp2 = jnp.log(1.0 + jnp.exp2(-jnp.abs(z))) * LOG2E
        lb = jnp.minimum(z, 0.0) - lp2
        lk = lb - z
        if st["diag"]:
            lk = jnp.where(_strict_lower(p), lk, 0.0)
        log_beta.append(lb)
        log_keep.append(lk)
        keep_bf.append(lk.astype(BF16))
    st["log_beta"] = _rows_cat(log_beta)
    st["log_keep"] = _rows_cat(log_keep)
    st["keep_bf"] = _rows_cat(keep_bf)


def _strict_lower(p):
    t = p + lax.broadcasted_iota(jnp.int32, (B_PIECE, B_TS), 0)
    s = lax.broadcasted_iota(jnp.int32, (B_PIECE, B_TS), 1)
    return s < t


def _tri(strict):
    j = lax.broadcasted_iota(jnp.int32, (B_TS, B_TS), 0)
    s = lax.broadcasted_iota(jnp.int32, (B_TS, B_TS), 1)
    return jnp.where(j > s if strict else j >= s, 1.0, 0.0).astype(BF16)


def _call_carrying(job, body, *, name, grid, in_specs, out_specs, out_shape, scratch_shapes, vmem_mib, args):
    n_in, n_out, n_scr = len(in_specs), len(out_specs), len(scratch_shapes)
    if job is None:
        res = pl.pallas_call(body, name=name, grid=grid, in_specs=in_specs, out_specs=out_specs,
                             out_shape=out_shape, scratch_shapes=scratch_shapes,
                             compiler_params=_cparams(("arbitrary",) * len(grid), vmem_mib))(*args)
        return res, []
    j_in, j_out = len(job.arrays), len(job.out_shape)
    hbm = pl.BlockSpec(memory_space=pltpu.HBM)

    def carrying(*refs):
        refs = list(refs)
        ins, refs = refs[:n_in], refs[n_in:]
        j_ins, refs = refs[:j_in], refs[j_in:]
        outs, refs = refs[:n_out], refs[n_out:]
        j_outs, refs = refs[:j_out], refs[j_out:]
        scr, sems = refs[:n_scr], refs[n_scr:]
        first = functools.reduce(jnp.logical_and, [pl.program_id(d) == 0 for d in range(len(grid))])
        last = functools.reduce(jnp.logical_and, [pl.program_id(d) == grid[d] - 1 for d in range(len(grid))])

        @pl.when(first)
        def _():
            job.start(j_ins, j_outs, sems)

        body(*ins, *outs, *scr)

        @pl.when(last)
        def _():
            job.wait(j_ins, j_outs, sems)

    res = pl.pallas_call(
        carrying, name=name, grid=grid,
        in_specs=list(in_specs) + [hbm] * j_in, out_specs=list(out_specs) + [hbm] * j_out,
        out_shape=list(out_shape) + job.out_shape, scratch_shapes=list(scratch_shapes) + job.scratch(),
        compiler_params=_cparams(("arbitrary",) * len(grid), vmem_mib))(*args, *job.arrays)
    return res[:n_out], res[n_out:]


def _attn_b_fwd(qkv, *, col0, name, job=None):
    T = qkv.shape[0]
    n_hp = WIDTH // LANES
    sub = B_TQ // B_TS

    def body(q_ref, k_ref, v_ref, o_ref, acc_ref, car_ref, qh_ref):
        i = pl.program_id(1)
        q2 = q_ref[...]
        head0 = _lane_is_head0()
        qh_ref[0] = jnp.where(head0, q2, jnp.zeros_like(q2))
        qh_ref[1] = jnp.where(head0, jnp.zeros_like(q2), q2)
        tri_s = _tri(True)
        acc_ref[...] = jnp.zeros_like(acc_ref)
        car_ref[...] = jnp.zeros_like(car_ref)

        def streams_of(kb, d, strips):
            keys = pl.ds(pl.multiple_of(kb * B_TS, B_TS), B_TS)
            return _sb_streams(d, strips, k2=k_ref[keys, :], v2=v_ref[keys, :])

        def scores(st):
            st["z2"] = _dot_nt(qh_ref[st["h"], st["rows"], :], st.pop("k2")) * (QK_SCALE * LOG2E)

        def logs(st):
            _sb_logs(st, st.pop("z2"))

        def suffix(st):
            st["suffix"] = _dot(st.pop("keep_bf"), tri_s)

        def weights(st):
            log_beta, suffix, log_keep = st.pop("log_beta"), st.pop("suffix"), st.pop("log_keep")
            wb = []
            for p in range(0, B_TS, B_PIECE):
                rows = _piece_rows(st, p)
                car = car_ref[st["h"], rows, :]
                w = jnp.exp2(log_beta[p:p + B_PIECE] + suffix[p:p + B_PIECE] + car)
                if st["diag"]:
                    w = jnp.where(_strict_lower(p), w, 0.0)
                wb.append(w.astype(BF16))
                car_ref[st["h"], rows, :] = car + jnp.sum(log_keep[p:p + B_PIECE], axis=1, keepdims=True)
            st["wb"] = _rows_cat(wb)

        def values(st):
            acc_ref[st["h"], st["rows"], :] += _dot(st.pop("wb"), st.pop("v2"))

        _sb_sweep(i, car_ref, streams_of, lambda sts: _skewed(sts, [scores, logs, suffix, weights, values]))
        o_ref[...] = jnp.where(head0, acc_ref[0], acc_ref[1])

    (out,), rode = _call_carrying(
        job, body, name=name, grid=(n_hp, T // B_TQ),
        in_specs=[pl.BlockSpec((B_TQ, LANES), lambda hp, i: (i, hp + col0)),
                  pl.BlockSpec((T, LANES), lambda hp, i: (0, hp + col0 + n_hp)),
                  pl.BlockSpec((T, LANES), lambda hp, i: (0, hp + col0 + 2 * n_hp))],
        out_specs=[pl.BlockSpec((B_TQ, LANES), lambda hp, i: (i, hp))],
        out_shape=[jax.ShapeDtypeStruct((T, WIDTH), F32)],
        scratch_shapes=[pltpu.VMEM((2, B_TQ, LANES), F32), pltpu.VMEM((2, B_TQ, 1), F32),
                        pltpu.VMEM((2, B_TQ, LANES), BF16)],
        vmem_mib=48, args=(qkv, qkv, qkv))
    return out, rode


def _attn_b_bwd(qkv, out, do, *, col0, name, job=None):
    T = qkv.shape[0]
    n_hp = WIDTH // LANES
    sub = B_TQ // B_TS

    def body(q_ref, k_ref, v_ref, o_ref, do_ref, dq_ref, dk_ref, dv_ref,
             dqa_ref, car_ref, carr_ref, tot_ref, qh_ref, doh_ref, qs_ref):
        i = pl.program_id(1)

        @pl.when(i == 0)
        def _():
            dk_ref[...] = jnp.zeros_like(dk_ref)
            dv_ref[...] = jnp.zeros_like(dv_ref)

        q2 = q_ref[...]
        do2 = do_ref[...]
        head0 = _lane_is_head0()
        zero = jnp.zeros_like(q2)
        qh_ref[0] = jnp.where(head0, q2, zero)
        qh_ref[1] = jnp.where(head0, zero, q2)
        doh_ref[0] = jnp.where(head0, do2, zero)
        doh_ref[1] = jnp.where(head0, zero, do2)
        scale = jnp.asarray(QK_SCALE, BF16)
        qs_ref[...] = q2 * scale
        tri_s = _tri(True)
        tri_i = _tri(False)
        prod = do2.astype(F32) * o_ref[...]
        tot_ref[0] = jnp.sum(jnp.where(head0, prod, 0.0), axis=1, keepdims=True)
        tot_ref[1] = jnp.sum(jnp.where(head0, 0.0, prod), axis=1, keepdims=True)
        dqa_ref[...] = jnp.zeros_like(dqa_ref)
        car_ref[...] = jnp.zeros_like(car_ref)
        carr_ref[...] = jnp.zeros_like(carr_ref)

        def streams_of(kb, d, strips):
            keys = pl.ds(pl.multiple_of(kb * B_TS, B_TS), B_TS)
            k2 = k_ref[keys, :]
            return _sb_streams(d, strips, keys=keys, k2=k2, v2=v_ref[keys, :], k2s=k2 * scale)

        def run(streams):
            def scores(st):
                st["z2"] = _dot_nt(qh_ref[st["h"], st["rows"], :], st.pop("k2")) * (QK_SCALE * LOG2E)
                st["dw"] = _dot_nt(doh_ref[st["h"], st["rows"], :], st.pop("v2"))

            def logs(st):
                _sb_logs(st, st.pop("z2"))

            def suffix(st):
                st["suffix"] = _dot(st.pop("keep_bf"), tri_s)

            def weights(st):
                h = st["h"]
                suffix, dw = st.pop("suffix"), st.pop("dw")
                wb, dlog, hi, lo = [], [], [], []
                for p in range(0, B_TS, B_PIECE):
                    rows = _piece_rows(st, p)
                    car = car_ref[h, rows, :]
                    w = jnp.exp2(st["log_beta"][p:p + B_PIECE] + suffix[p:p + B_PIECE] + car)
                    if st["diag"]:
                        w = jnp.where(_strict_lower(p), w, 0.0)
                    w = w.astype(BF16)
                    dl = w.astype(F32) * dw[p:p + B_PIECE]
                    dl_hi, dl_lo = _split_bf16(dl)
                    wb.append(w)
                    dlog.append(dl)
                    hi.append(dl_hi)
                    lo.append(dl_lo)
                    car_ref[h, rows, :] = car + jnp.sum(st["log_keep"][p:p + B_PIECE], axis=1, keepdims=True)
                st["wb"], st["dlog"], st["hi"], st["lo"] = _rows_cat(wb), _rows_cat(dlog), _rows_cat(hi), _rows_cat(lo)

            def later(st):
                st["later"] = _dot(st.pop("hi"), tri_i) + _dot(st.pop("lo"), tri_i)

            def dscores(st):
                h = st["h"]
                later, dlog = st.pop("later"), st.pop("dlog")
                log_keep, log_beta = st.pop("log_keep"), st.pop("log_beta")
                dzb = []
                for p in range(0, B_TS, B_PIECE):
                    rows = _piece_rows(st, p)
                    pc = slice(p, p + B_PIECE)
                    carr = carr_ref[h, rows, :]
                    earlier = tot_ref[h, rows, :] - (later[pc] + carr)
                    dz = dlog[pc] * jnp.exp2(log_keep[pc]) - jnp.exp2(log_beta[pc]) * earlier
                    if st["diag"]:
                        dz = jnp.where(_strict_lower(p), dz, 0.0)
                    dzb.append(dz.astype(BF16))
                    carr_ref[h, rows, :] = carr + jnp.sum(dlog[pc], axis=1, keepdims=True)
                st["dzb"] = _rows_cat(dzb)

            def grads(st):
                h, rows, keys = st["h"], st["rows"], st["keys"]
                mine = head0 if h == 0 else jnp.logical_not(head0)
                dzb = st.pop("dzb")
                dqa_ref[h, rows, :] += _dot(dzb, st.pop("k2s"))
                dk_ref[keys, :] += jnp.where(mine, _dot_tn(dzb, qs_ref[rows, :]), 0.0)
                dv_ref[keys, :] += jnp.where(mine, _dot_tn(st.pop("wb"), do_ref[rows, :]), 0.0)

            _skewed(streams, [scores, logs, suffix, weights, later, dscores, grads])

        _sb_sweep(i, car_ref, streams_of, run)
        dq_ref[...] = jnp.where(head0, dqa_ref[0], dqa_ref[1]).astype(dq_ref.dtype)

    return _call_carrying(
        job, body, name=name, grid=(n_hp, T // B_TQ),
        in_specs=[pl.BlockSpec((B_TQ, LANES), lambda hp, i: (i, hp + col0)),
                  pl.BlockSpec((T, LANES), lambda hp, i: (0, hp + col0 + n_hp)),
                  pl.BlockSpec((T, LANES), lambda hp, i: (0, hp + col0 + 2 * n_hp)),
                  pl.BlockSpec((B_TQ, LANES), lambda hp, i: (i, hp)),
                  pl.BlockSpec((B_TQ, LANES), lambda hp, i: (i, hp))],
        out_specs=[pl.BlockSpec((B_TQ, LANES), lambda hp, i: (i, hp)),
                   pl.BlockSpec((T, LANES), lambda hp, i: (0, hp)),
                   pl.BlockSpec((T, LANES), lambda hp, i: (0, hp))],
        out_shape=[jax.ShapeDtypeStruct((T, WIDTH), BF16),
                   jax.ShapeDtypeStruct((T, WIDTH), F32),
                   jax.ShapeDtypeStruct((T, WIDTH), F32)],
        scratch_shapes=[pltpu.VMEM((2, B_TQ, LANES), F32), pltpu.VMEM((2, B_TQ, 1), F32),
                        pltpu.VMEM((2, B_TQ, 1), F32), pltpu.VMEM((2, B_TQ, 1), F32),
                        pltpu.VMEM((2, B_TQ, LANES), BF16), pltpu.VMEM((2, B_TQ, LANES), BF16),
                        pltpu.VMEM((B_TQ, LANES), BF16)],
        vmem_mib=56, args=(qkv, qkv, qkv, out, do))


def _gated_mix(oa_ref, ob_ref, g_ref, bg_ref, wpa_ref, wpb_ref, D):
    ya = _dot(oa_ref[...].astype(BF16), wpa_ref[...])
    yb = _dot(ob_ref[...].astype(BF16), wpb_ref[...])
    sa = jax.nn.sigmoid(g_ref[:, :D] + bg_ref[:, :D])
    sb = jax.nn.sigmoid(g_ref[:, D:] + bg_ref[:, D:])
    return ya, yb, sa, sb


def _proj_fwd(oa, ob, g, bg, wpa, wpb, wo, wl, xin, lng, lnb, layer, *, alpha, name):
    T, D = xin.shape
    tm = _tile(T, 512)
    row = lambda i: (i, 0)
    wspec = lambda r, c: pl.BlockSpec((None, r, c), lambda i: (wl, 0, 0))
    vec = lambda c: pl.BlockSpec((None, 1, c), lambda i: (layer, 0, 0))

    def body(oa_ref, ob_ref, g_ref, bg_ref, wpa_ref, wpb_ref, wo_ref, x_ref, lg_ref, lb_ref, x1_ref, r1_ref, x1b_ref):
        ya, yb, sa, sb = _gated_mix(oa_ref, ob_ref, g_ref, bg_ref, wpa_ref, wpb_ref, D)
        mix = _dot((sa * ya + sb * yb).astype(BF16), wo_ref[...])
        r1 = alpha * x_ref[...] + mix
        r1_ref[...] = r1
        x1 = _ln_fwd(r1, lg_ref[...], lb_ref[...])
        x1_ref[...] = x1
        x1b_ref[...] = x1.astype(BF16)

    return pl.pallas_call(
        body, name=name, grid=(T // tm,),
        in_specs=[pl.BlockSpec((tm, WIDTH), row), pl.BlockSpec((tm, WIDTH), row), pl.BlockSpec((tm, 2 * D), row),
                  vec(2 * D), wspec(WIDTH, D), wspec(WIDTH, D), wspec(D, D),
                  pl.BlockSpec((tm, D), row), vec(D), vec(D)],
        out_specs=[pl.BlockSpec((tm, D), row), pl.BlockSpec((tm, D), row), pl.BlockSpec((tm, D), row)],
        out_shape=[jax.ShapeDtypeStruct((T, D), F32), jax.ShapeDtypeStruct((T, D), F32),
                   jax.ShapeDtypeStruct((T, D), BF16)],
        compiler_params=_cparams(("arbitrary",), 56),
    )(oa, ob, g, bg, wpa, wpb, wo, xin, lng, lnb)


def _proj_bwd(dx1, r1, lng, oa, ob, g, bg, wpa, wpb, wo, wl, layer, *, name):
    T, D = dx1.shape
    tm = _tile(T, 512)
    row = lambda i: (i, 0)
    fixed = lambda i: (0, 0)
    wspec = lambda r, c: pl.BlockSpec((None, r, c), lambda i: (wl, 0, 0))
    vec = lambda c: pl.BlockSpec((None, 1, c), lambda i: (layer, 0, 0))

    def body(dx_ref, r1_ref, lg_ref, oa_ref, ob_ref, g_ref, bg_ref, wpa_ref, wpb_ref, wo_ref,
             dr_ref, mix_ref, dya_ref, dyb_ref, dg_ref, doa_ref, dob_ref, dlg_ref, dlb_ref, dbg_ref):
        @pl.when(pl.program_id(0) == 0)
        def _():
            dlg_ref[...] = jnp.zeros_like(dlg_ref)
            dlb_ref[...] = jnp.zeros_like(dlb_ref)
            dbg_ref[...] = jnp.zeros_like(dbg_ref)

        dx = dx_ref[...]
        dr, xhat = _ln_bwd(dx, r1_ref[...], lg_ref[...])
        dr_ref[...] = dr
        dlg_ref[...] += jnp.sum(dx * xhat, axis=0, keepdims=True)
        dlb_ref[...] += jnp.sum(dx, axis=0, keepdims=True)
        dmix = _dot_nt(dr.astype(BF16), wo_ref[...])
        ya, yb, sa, sb = _gated_mix(oa_ref, ob_ref, g_ref, bg_ref, wpa_ref, wpb_ref, D)
        mix_ref[...] = (sa * ya + sb * yb).astype(BF16)
        dya = (dmix * sa).astype(BF16)
        dyb = (dmix * sb).astype(BF16)
        dya_ref[...] = dya
        dyb_ref[...] = dyb
        dga = dmix * ya * (sa * (1.0 - sa))
        dgb = dmix * yb * (sb * (1.0 - sb))
        dg_ref[:, :D] = dga.astype(BF16)
        dg_ref[:, D:] = dgb.astype(BF16)
        dbg_ref[:, :D] += jnp.sum(dga, axis=0, keepdims=True)
        dbg_ref[:, D:] += jnp.sum(dgb, axis=0, keepdims=True)
        doa_ref[...] = _dot_nt(dya, wpa_ref[...]).astype(BF16)
        dob_ref[...] = _dot_nt(dyb, wpb_ref[...]).astype(BF16)

    return pl.pallas_call(
        body, name=name, grid=(T // tm,),
        in_specs=[pl.BlockSpec((tm, D), row), pl.BlockSpec((tm, D), row), vec(D),
                  pl.BlockSpec((tm, WIDTH), row), pl.BlockSpec((tm, WIDTH), row), pl.BlockSpec((tm, 2 * D), row),
                  vec(2 * D), wspec(WIDTH, D), wspec(WIDTH, D), wspec(D, D)],
        out_specs=[pl.BlockSpec((tm, D), row), pl.BlockSpec((tm, D), row), pl.BlockSpec((tm, D), row),
                   pl.BlockSpec((tm, D), row), pl.BlockSpec((tm, 2 * D), row),
                   pl.BlockSpec((tm, WIDTH), row), pl.BlockSpec((tm, WIDTH), row),
                   pl.BlockSpec((1, D), fixed), pl.BlockSpec((1, D), fixed), pl.BlockSpec((1, 2 * D), fixed)],
        out_shape=[jax.ShapeDtypeStruct((T, D), F32), jax.ShapeDtypeStruct((T, D), BF16),
                   jax.ShapeDtypeStruct((T, D), BF16), jax.ShapeDtypeStruct((T, D), BF16),
                   jax.ShapeDtypeStruct((T, 2 * D), BF16),
                   jax.ShapeDtypeStruct((T, WIDTH), BF16), jax.ShapeDtypeStruct((T, WIDTH), BF16),
                   jax.ShapeDtypeStruct((1, D), F32), jax.ShapeDtypeStruct((1, D), F32),
                   jax.ShapeDtypeStruct((1, 2 * D), F32)],
        compiler_params=_cparams(("arbitrary",), 56),
    )(dx1, r1, lng, oa, ob, g, bg, wpa, wpb, wo)


def _ffn_fwd(x1, wfi, wfo, wl, lng, lnb, layer, *, alpha, name, job=None):
    T, D = x1.shape
    tf = wfi.shape[-1]
    nj = wfi.shape[0] // 2
    tm = _tile(T, 512)
    vec = lambda c: pl.BlockSpec((None, 1, c), lambda i, j: (layer, 0, 0))

    def body(x_ref, wg_ref, wu_ref, wo_ref, lg_ref, lb_ref, gs_ref, us_ref, r2_ref, x2_ref, acc_ref, xb_ref):
        j = pl.program_id(1)

        @pl.when(j == 0)
        def _():
            xb_ref[...] = x_ref[...].astype(BF16)
            acc_ref[...] = jnp.zeros_like(acc_ref)

        gv = _dot(xb_ref[...], wg_ref[...])
        uv = _dot(xb_ref[...], wu_ref[...])
        gs_ref[...] = gv.astype(gs_ref.dtype)
        us_ref[...] = uv.astype(us_ref.dtype)
        act = gv * jax.nn.sigmoid(gv) * uv
        acc_ref[...] += _dot(act.astype(BF16), wo_ref[...])

        @pl.when(j == nj - 1)
        def _():
            r2 = alpha * x_ref[...] + acc_ref[...]
            r2_ref[...] = r2
            x2_ref[...] = _ln_fwd(r2, lg_ref[...], lb_ref[...])

    return _call_carrying(
        job, body, name=name, grid=(T // tm, nj),
        in_specs=[pl.BlockSpec((tm, D), lambda i, j: (i, 0)),
                  pl.BlockSpec((None, None, D, tf), lambda i, j: (j, wl, 0, 0)),
                  pl.BlockSpec((None, None, D, tf), lambda i, j: (j + nj, wl, 0, 0)),
                  pl.BlockSpec((None, tf, D), lambda i, j: (wl, j, 0)),
                  vec(D), vec(D)],
        out_specs=[pl.BlockSpec((None, tm, tf), lambda i, j: (j, i, 0)),
                   pl.BlockSpec((None, tm, tf), lambda i, j: (j, i, 0)),
                   pl.BlockSpec((tm, D), lambda i, j: (i, 0)),
                   pl.BlockSpec((tm, D), lambda i, j: (i, 0))],
        out_shape=[jax.ShapeDtypeStruct((nj, T, tf), BF16), jax.ShapeDtypeStruct((nj, T, tf), BF16),
                   jax.ShapeDtypeStruct((T, D), F32), jax.ShapeDtypeStruct((T, D), F32)],
        scratch_shapes=[pltpu.VMEM((tm, D), F32), pltpu.VMEM((tm, D), BF16)],
        vmem_mib=56, args=(x1, wfi, wfi, wfo, lng, lnb))


def _ffn_bwd(dx2, r2, lng, gs, us, wfi, wfo, wl, layer, *, alpha, name, job=None):
    T, D = dx2.shape
    tf = wfi.shape[-1]
    nj = wfi.shape[0] // 2
    tm = _tile(T, 512)
    vec = lambda c: pl.BlockSpec((None, 1, c), lambda i, j: (layer, 0, 0))
    blk = lambda: pl.BlockSpec((None, tm, tf), lambda i, j: (j, i, 0))

    def body(dx_ref, r2_ref, lg_ref, gs_ref, us_ref, wg_ref, wu_ref, wo_ref,
             dr_ref, act_ref, dg_ref, du_ref, dx1_ref, dlg_ref, dlb_ref, acc_ref, drb_ref):
        i = pl.program_id(0)
        j = pl.program_id(1)

        @pl.when((i == 0) & (j == 0))
        def _():
            dlg_ref[...] = jnp.zeros_like(dlg_ref)
            dlb_ref[...] = jnp.zeros_like(dlb_ref)

        @pl.when(j == 0)
        def _():
            dx = dx_ref[...]
            dr, xhat = _ln_bwd(dx, r2_ref[...], lg_ref[...])
            dlg_ref[...] += jnp.sum(dx * xhat, axis=0, keepdims=True)
            dlb_ref[...] += jnp.sum(dx, axis=0, keepdims=True)
            drb_ref[...] = dr.astype(BF16)
            dr_ref[...] = dr.astype(BF16)
            acc_ref[...] = alpha * dr

        dact = _dot_nt(drb_ref[...], wo_ref[...])
        gv = gs_ref[...].astype(F32)
        uv = us_ref[...].astype(F32)
        s = jax.nn.sigmoid(gv)
        silu = gv * s
        act_ref[...] = (silu * uv).astype(BF16)
        dg = (dact * uv * (s * (1.0 + gv * (1.0 - s)))).astype(BF16)
        du = (dact * silu).astype(BF16)
        dg_ref[...] = dg
        du_ref[...] = du
        acc_ref[...] += _dot_nt(dg, wg_ref[...]) + _dot_nt(du, wu_ref[...])

        @pl.when(j == nj - 1)
        def _():
            dx1_ref[...] = acc_ref[...]

    return _call_carrying(
        job, body, name=name, grid=(T // tm, nj),
        in_specs=[pl.BlockSpec((tm, D), lambda i, j: (i, 0)), pl.BlockSpec((tm, D), lambda i, j: (i, 0)), vec(D),
                  blk(), blk(),
                  pl.BlockSpec((None, None, D, tf), lambda i, j: (j, wl, 0, 0)),
                  pl.BlockSpec((None, None, D, tf), lambda i, j: (j + nj, wl, 0, 0)),
                  pl.BlockSpec((None, tf, D), lambda i, j: (wl, j, 0))],
        out_specs=[pl.BlockSpec((tm, D), lambda i, j: (i, 0)), blk(), blk(), blk(),
                   pl.BlockSpec((tm, D), lambda i, j: (i, 0)),
                   pl.BlockSpec((1, D), lambda i, j: (0, 0)), pl.BlockSpec((1, D), lambda i, j: (0, 0))],
        out_shape=[jax.ShapeDtypeStruct((T, D), BF16),
                   jax.ShapeDtypeStruct((nj, T, tf), BF16), jax.ShapeDtypeStruct((nj, T, tf), BF16),
                   jax.ShapeDtypeStruct((nj, T, tf), BF16),
                   jax.ShapeDtypeStruct((T, D), F32),
                   jax.ShapeDtypeStruct((1, D), F32), jax.ShapeDtypeStruct((1, D), F32)],
        scratch_shapes=[pltpu.VMEM((tm, D), F32), pltpu.VMEM((tm, D), BF16)],
        vmem_mib=56, args=(dx2, r2, lng, gs, us, wfi, wfi, wfo))


def _loss_head(y, target, *, name):
    T, D = y.shape
    tm = _tile(T, 1024)

    def body(y_ref, t_ref, dy_ref, sq_ref):
        @pl.when(pl.program_id(0) == 0)
        def _():
            sq_ref[...] = jnp.zeros_like(sq_ref)
        err = y_ref[...] - t_ref[...]
        dy_ref[...] = err * (1.0 / D)
        sq_ref[...] += jnp.sum(err * err, axis=0, keepdims=True)

    return pl.pallas_call(
        body, name=name, grid=(T // tm,),
        in_specs=[pl.BlockSpec((tm, D), lambda i: (i, 0)), pl.BlockSpec((tm, D), lambda i: (i, 0))],
        out_specs=[pl.BlockSpec((tm, D), lambda i: (i, 0)), pl.BlockSpec((1, D), lambda i: (0, 0))],
        out_shape=[jax.ShapeDtypeStruct((T, D), F32), jax.ShapeDtypeStruct((1, D), F32)],
        compiler_params=_cparams(("arbitrary",)),
    )(y, target)


def _my_place():
    return lax.axis_index("x"), lax.axis_index("y"), lax.axis_index("c")


def _peer(place, k):
    x, y, c = place
    return (1 - x if k & 4 else x, 1 - y if k & 2 else y, 1 - c if k & 1 else c)


def _logical(place):
    x, y, c = place
    return 4 * x + 2 * y + c


def _block_of(ref, mode, idx):
    if mode == "blk":
        return ref.at[idx]
    if mode == "col":
        size = ref.shape[2] // N_DEV
        return ref.at[:, :, pl.ds(pl.multiple_of(idx * size, size), size)]
    size = ref.shape[1] // N_DEV
    return ref.at[:, pl.ds(pl.multiple_of(idx * size, size), size), :]


def _full_shape(shard, mode):
    if mode == "blk":
        return (N_DEV,) + shard.shape
    if mode == "col":
        return shard.shape[:2] + (N_DEV * shard.shape[2],)
    return (shard.shape[0], N_DEV * shard.shape[1], shard.shape[2])


class _Exchange:
    def __init__(self, arrays, out_shape, build):
        self.arrays = list(arrays)
        self.out_shape = list(out_shape)
        self.build = build

    def scratch(self):
        n = len(self.arrays)
        return [pltpu.SemaphoreType.DMA((n * N_DEV,)), pltpu.SemaphoreType.DMA((n * N_DEV,)),
                pltpu.SemaphoreType.DMA((n,))]

    def start(self, ins, outs, sems):
        for cp in self.build(ins, outs, *sems):
            cp.start()

    def wait(self, ins, outs, sems):
        for cp in self.build(ins, outs, *sems):
            cp.wait()

    def run(self, name):
        n_in, n_out = len(self.arrays), len(self.out_shape)
        hbm = pl.BlockSpec(memory_space=pltpu.HBM)

        def body(*refs):
            ins, outs, sems = refs[:n_in], refs[n_in:n_in + n_out], refs[n_in + n_out:]
            self.start(ins, outs, sems)
            self.wait(ins, outs, sems)

        return pl.pallas_call(
            body, name=name, in_specs=[hbm] * n_in, out_specs=[hbm] * n_out,
            out_shape=self.out_shape, scratch_shapes=self.scratch(),
        )(*self.arrays)


def _copies_to_all(src_of, dst_of, n, send, recv, local):
    me = _my_place()
    copies = []
    for a in range(n):
        copies.append(pltpu.make_async_copy(src_of(a, _logical(me)), dst_of(a), local.at[a]))
        for k in range(1, N_DEV):
            peer = _peer(me, k)
            copies.append(pltpu.make_async_remote_copy(
                src_ref=src_of(a, _logical(peer)), dst_ref=dst_of(a),
                send_sem=send.at[a * N_DEV + k], recv_sem=recv.at[a * N_DEV + k],
                device_id=peer, device_id_type=MESH))
    return copies


def _gather_job(shards, modes):
    def build(ins, outs, send, recv, local):
        my_id = _logical(_my_place())
        return _copies_to_all(lambda a, dev: ins[a], lambda a: _block_of(outs[a], modes[a], my_id),
                              len(shards), send, recv, local)

    return _Exchange(shards, [jax.ShapeDtypeStruct(_full_shape(s, m), s.dtype) for s, m in zip(shards, modes)], build)


def _grad_block(ref, mode, idx):
    if mode == "blk":
        return ref.at[idx]
    if mode == "col":
        size = ref.shape[1] // N_DEV
        return ref.at[:, pl.ds(pl.multiple_of(idx * size, size), size)]
    size = ref.shape[0] // N_DEV
    return ref.at[pl.ds(pl.multiple_of(idx * size, size), size), :]


def _grad_shard_shape(g, mode):
    if mode == "blk":
        return g.shape[1:]
    if mode == "col":
        return (g.shape[0], g.shape[1] // N_DEV)
    return (g.shape[0] // N_DEV, g.shape[1])


def _grads_job(groups, modes):
    flat = [(g, w, l) for w, per_w in enumerate(groups) for l, g in enumerate(per_w)]

    def build(ins, outs, send, recv, local):
        my_id = _logical(_my_place())
        return _copies_to_all(lambda a, dev: _grad_block(ins[a], modes[flat[a][1]], dev),
                              lambda a: outs[flat[a][1]].at[my_id, flat[a][2]],
                              len(flat), send, recv, local)

    out_shape = [jax.ShapeDtypeStruct((N_DEV, len(per_w)) + _grad_shard_shape(per_w[0], m), per_w[0].dtype)
                 for per_w, m in zip(groups, modes)]
    return _Exchange([g for g, _, _ in flat], out_shape, build)


def _adamw(w, g, m, v):
    m = ADAM_B1 * m + (1.0 - ADAM_B1) * g
    v = ADAM_B2 * v + (1.0 - ADAM_B2) * (g * g)
    m_hat = m / (1.0 - ADAM_B1 ** ADAM_STEP)
    v_hat = v / (1.0 - ADAM_B2 ** ADAM_STEP)
    delta = -ADAM_LR * (m_hat / (jnp.sqrt(v_hat) + ADAM_EPS) + ADAM_WD * w)
    return delta, m, v


def _sum_slots_adamw(slots, w, m, v, *, name):
    R, C = w.shape
    tr = _tile(R, 256)

    def body(s_ref, w_ref, m_ref, v_ref, g_out, d_out, m_out, v_out):
        g = s_ref[0].astype(F32)
        for s in range(1, N_DEV):
            g = g + s_ref[s].astype(F32)
        delta, m_new, v_new = _adamw(w_ref[...], g, m_ref[...], v_ref[...])
        g_out[...] = g
        d_out[...] = delta
        m_out[...] = m_new
        v_out[...] = v_new

    spec = pl.BlockSpec((tr, C), lambda i: (i, 0))
    return pl.pallas_call(
        body, name=name, grid=(R // tr,),
        in_specs=[pl.BlockSpec((N_DEV, tr, C), lambda i: (0, i, 0)), spec, spec, spec],
        out_specs=[spec] * 4,
        out_shape=[jax.ShapeDtypeStruct((R, C), F32)] * 4,
        compiler_params=_cparams(("parallel",)),
    )(slots, w, m, v)


def _small_allreduce_adamw(g, w, m, v, *, name):
    R = g.shape[0]
    vmem = pl.BlockSpec(memory_space=pltpu.VMEM)

    def body(g_ref, w_ref, m_ref, v_ref, g_out, d_out, m_out, v_out, slots, send, recv):
        me = _my_place()
        my_id = _logical(me)
        slots[my_id] = g_ref[...]
        copies = []
        for k in range(1, N_DEV):
            cp = pltpu.make_async_remote_copy(
                src_ref=g_ref, dst_ref=slots.at[my_id], send_sem=send.at[k], recv_sem=recv.at[k],
                device_id=_peer(me, k), device_id_type=MESH)
            cp.start()
            copies.append(cp)
        for cp in copies:
            cp.wait()
        total = slots[0]
        for s in range(1, N_DEV):
            total = total + slots[s]
        delta, m_new, v_new = _adamw(w_ref[...], total, m_ref[...], v_ref[...])
        g_out[...] = total
        d_out[...] = delta
        m_out[...] = m_new
        v_out[...] = v_new

    return pl.pallas_call(
        body, name=name,
        in_specs=[vmem] * 4, out_specs=[vmem] * 4,
        out_shape=[jax.ShapeDtypeStruct((R, LANES), F32)] * 4,
        scratch_shapes=[pltpu.VMEM((N_DEV, R, LANES), F32),
                        pltpu.SemaphoreType.DMA((N_DEV,)), pltpu.SemaphoreType.DMA((N_DEV,))],
    )(g, w, m, v)


def _pack(parts):
    flat = jnp.concatenate([p.reshape(-1) for p in parts])
    rows = -(-flat.shape[0] // (8 * LANES)) * 8
    return jnp.pad(flat, (0, rows * LANES - flat.shape[0])).reshape(rows, LANES)


def _unpack(packed, like):
    flat = packed.reshape(-1)
    out, pos = [], 0
    for p in like:
        out.append(flat[pos:pos + p.size].reshape(p.shape))
        pos += p.size
    return out


def kernel(x, w_in, b_gate, rel_bias, w_proj_a, w_proj_b, w_out, ln1_g, ln1_b, w_ffn_in, w_ffn_out, ln2_g, ln2_b, loss_target, m_w_in, m_b_gate, m_rel_bias, m_w_proj_a, m_w_proj_b, m_w_out, m_ln1_g, m_ln1_b, m_w_ffn_in, m_w_ffn_out, m_ln2_g, m_ln2_b, v_w_in, v_b_gate, v_rel_bias, v_w_proj_a, v_w_proj_b, v_w_out, v_ln1_g, v_ln1_b, v_w_ffn_in, v_w_ffn_out, v_ln2_g, v_ln2_b):
    L = w_in.shape[0]
    T, D = x.shape[1], x.shape[2]
    alpha = float((2 * L) ** 0.25)
    n_qkv = 6 * WIDTH

    big = [w_in, w_proj_a, w_proj_b, w_out, w_ffn_in, w_ffn_out]
    kinds = ["in", "pa", "pb", "o", "fi", "fo"]
    modes = ["col", "col", "col", "row", "blk", "row"]
    mode_of = dict(zip(kinds, modes))
    w_bf = dict(zip(kinds, [w.astype(BF16) for w in big]))

    def gather_of(ks, l):
        return _gather_job([w_bf[k][l:l + 1] for k in ks], [mode_of[k] for k in ks])

    W = [dict() for _ in range(L)]
    (W[0]["in"],) = gather_of(["in"], 0).run("gather_w_in_first")
    vec3 = lambda a: a[:, None, :]
    bg3, l1g, l1b, l2g, l2b = vec3(b_gate), vec3(ln1_g), vec3(ln1_b), vec3(ln2_g), vec3(ln2_b)
    b_col0 = 3 * WIDTH // LANES

    h = x[0]
    saved = []
    for l in range(L):
        ahead = l + 1 < L
        soon = ["pa", "pb", "o", "fo"]
        (qkv, gates), got = _in_proj(h, W[l]["in"], 0, n_qkv=n_qkv, name=f"in_proj_{l}",
                                     job=gather_of(soon, 0) if l == 0 else None)
        W[l].update(zip(soon, got))
        kvpad = jnp.pad(qkv[:, WIDTH:3 * WIDTH], ((A_WIN - A_TQ, 0), (0, 0)))
        bias = _toeplitz_bias(rel_bias[l])
        oa, got = _attn_a_fwd(qkv, kvpad, bias, name=f"attn_a_fwd_{l}", job=gather_of(["fi"], 0) if l == 0 else None)
        W[l].update(zip(["fi"], got))
        early = ["in", "pa", "pb", "o"]
        ob, got = _attn_b_fwd(qkv, col0=b_col0, name=f"attn_b_fwd_{l}", job=gather_of(early, l + 1) if ahead else None)
        W[l + 1 if ahead else l].update(zip(early, got))
        x1, r1, x1b = _proj_fwd(oa, ob, gates, bg3, W[l]["pa"], W[l]["pb"], W[l]["o"], 0, h, l1g, l1b, l,
                           alpha=alpha, name=f"proj_fwd_{l}")
        (gs, us, r2, x2), got = _ffn_fwd(x1, W[l]["fi"], W[l]["fo"], 0, l2g, l2b, l, alpha=alpha, name=f"ffn_fwd_{l}",
                                         job=gather_of(["fi", "fo"], l + 1) if ahead else None)
        W[l + 1 if ahead else l].update(zip(["fi", "fo"], got))
        saved.append((h, qkv, gates, kvpad, bias, oa, ob, x1b, r1, gs, us, r2))
        h = x2

    d_h, sq = _loss_head(h, loss_target[0], name="loss_head")
    loss = lax.psum((0.5 / D) * jnp.sum(sq), ("x", "y", "c"))

    g_bg, g_rb, g_l1g, g_l1b, g_l2g, g_l2b = ([None] * L for _ in range(6))
    slot = {k: [None] * L for k in kinds}

    def exchange_of(ks, grads):
        return _grads_job([[g] for g in grads], [mode_of[k] for k in ks])

    w_in_above = None
    for l in reversed(range(L)):
        xin, qkv, gates, kvpad, bias, oa, ob, x1b, r1, gs, us, r2 = saved[l]
        (dr2, act, dgt, dup, dx1, g_l2g[l], g_l2b[l]), got = _ffn_bwd(
            d_h, r2, l2g, gs, us, W[l]["fi"], W[l]["fo"], 0, l, alpha=alpha, name=f"ffn_bwd_{l}", job=w_in_above)
        if w_in_above is not None:
            (slot["in"][l + 1],) = got
        g_fo = _mm_tn_blocked_a(act, dr2, name=f"grad_w_ffn_out_{l}").reshape(-1, D)
        g_fi = jnp.concatenate(_mm_tn_blocked_pair(x1b, dgt, dup, name=f"grad_w_ffn_in_{l}"), axis=0)
        (dr1, mixin, dya, dyb, dgates, doa, dob, g_l1g[l], g_l1b[l], g_bg[l]) = _proj_bwd(
            dx1, r1, l1g, oa, ob, gates, bg3, W[l]["pa"], W[l]["pb"], W[l]["o"], 0, l, name=f"proj_bwd_{l}")
        g_o = _mm_tn(mixin, dr1, tm=_tile(D, 1024), tn=_tile(D, 1024), name=f"grad_w_out_{l}")
        g_pa = _mm_tn(oa, dya, tm=WIDTH, tn=_tile(D, 1024), name=f"grad_w_proj_a_{l}")
        g_pb = _mm_tn(ob, dyb, tm=WIDTH, tn=_tile(D, 1024), name=f"grad_w_proj_b_{l}")
        (dqa, dka, dva, dbias), (slot["fi"][l], slot["fo"][l]) = _attn_a_bwd(
            qkv, kvpad, bias, doa, name=f"attn_a_bwd_{l}", job=exchange_of(["fi", "fo"], [g_fi, g_fo]))
        g_rb[l] = _toeplitz_bias_grad(dbias)
        (dqb, dkb, dvb), (slot["pa"][l], slot["pb"][l], slot["o"][l]) = _attn_b_bwd(
            qkv, ob, dob, col0=b_col0, name=f"attn_b_bwd_{l}", job=exchange_of(["pa", "pb", "o"], [g_pa, g_pb, g_o]))
        pad = A_WIN - A_TQ
        d_pre = jnp.concatenate([dqa, dka[pad:].astype(BF16), dva[pad:].astype(BF16),
                                 dqb, dkb.astype(BF16), dvb.astype(BF16), dgates], axis=1)
        g_in = _mm_tn(xin, d_pre, tm=D, tn=4 * w_in.shape[2], name=f"grad_w_in_{l}")
        w_in_above = exchange_of(["in"], [g_in])
        d_h, got = _mm_nt_add(d_pre, W[l]["in"], 0, dr1, alpha, name=f"grad_x_{l}", job=w_in_above if l == 0 else None)
        if l == 0:
            (slot["in"][0],) = got
    grad_x = d_h[None]

    slots = [jnp.concatenate(slot[k], axis=1) if L > 1 else slot[k][0] for k in kinds]
    moments_m = [m_w_in, m_w_proj_a, m_w_proj_b, m_w_out, m_w_ffn_in, m_w_ffn_out]
    moments_v = [v_w_in, v_w_proj_a, v_w_proj_b, v_w_out, v_w_ffn_in, v_w_ffn_out]
    names = ["w_in", "w_proj_a", "w_proj_b", "w_out", "w_ffn_in", "w_ffn_out"]
    big_out = {}
    for nm, s, w, m, v in zip(names, slots, big, moments_m, moments_v):
        two = lambda a: a.reshape(-1, a.shape[-1])
        res = _sum_slots_adamw(s.reshape(N_DEV, -1, s.shape[-1]), two(w), two(m), two(v), name=f"adamw_{nm}")
        big_out[nm] = [r.reshape(w.shape) for r in res]

    small_w = [b_gate, rel_bias, ln1_g, ln1_b, ln2_g, ln2_b]
    small_g = [jnp.stack(g) for g in (g_bg, g_rb, g_l1g, g_l1b, g_l2g, g_l2b)]
    small_m = [m_b_gate, m_rel_bias, m_ln1_g, m_ln1_b, m_ln2_g, m_ln2_b]
    small_v = [v_b_gate, v_rel_bias, v_ln1_g, v_ln1_b, v_ln2_g, v_ln2_b]
    res = _small_allreduce_adamw(_pack(small_g), _pack(small_w), _pack(small_m), _pack(small_v),
                                 name="allreduce_small_adamw")
    small_names = ["b_gate", "rel_bias", "ln1_g", "ln1_b", "ln2_g", "ln2_b"]
    small_out = {nm: [] for nm in small_names}
    for packed in res:
        for nm, arr in zip(small_names, _unpack(packed, small_w)):
            small_out[nm].append(arr)

    order = ["w_in", "b_gate", "rel_bias", "w_proj_a", "w_proj_b", "w_out", "ln1_g", "ln1_b",
             "w_ffn_in", "w_ffn_out", "ln2_g", "ln2_b"]
    every = {**big_out, **small_out}
    outs = [loss, grad_x]
    for kind in range(4):
        outs += [every[nm][kind] for nm in order]
    return tuple(outs)
```

```python
import functools
import math

import jax
import jax.numpy as jnp
import numpy as np
from jax import lax
from jax.experimental import pallas as pl
from jax.experimental.pallas import tpu as pltpu

F32 = jnp.float32
BF16 = jnp.bfloat16

HEAD_DIM = 64
CHUNK = 64
LEFT_CHUNKS = 8
REL_CLIP = 256
N_REL = 2 * REL_CLIP + 1
WIDTH = 512
LANES = 128
A_TQ = 256
A_WIN = A_TQ + LEFT_CHUNKS * CHUNK
A_STRIP = 128
B_TQ = 512
B_TS = 256
B_PIECE = 64
B_DEAD = -160.0
LN_EPS = 1e-5
QK_SCALE = 1.0 / math.sqrt(HEAD_DIM)
LOG2E = 1.4426950408889634
NEG = -1e30

ADAM_LR = 0.001
ADAM_B1 = 0.9
ADAM_B2 = 0.999
ADAM_EPS = 1e-08
ADAM_WD = 0.01
ADAM_STEP = 10

N_DEV = 8
MESH = pl.DeviceIdType.MESH
MIB = 1024 * 1024


def _cparams(sem=None, vmem_mib=48):
    return pltpu.CompilerParams(dimension_semantics=sem, vmem_limit_bytes=vmem_mib * MIB)


def _dot(a, b):
    return jnp.dot(a, b, preferred_element_type=F32)


def _dot_nt(a, b):
    return lax.dot_general(a, b, (((1,), (1,)), ((), ())), preferred_element_type=F32)


def _dot_tn(a, b):
    return lax.dot_general(a, b, (((0,), (0,)), ((), ())), preferred_element_type=F32)


def _tile(n, pref):
    if n <= pref:
        return n
    for t in range(pref - pref % 8, 0, -8):
        if n % t == 0:
            return t
    raise ValueError((n, pref))


def _in_proj(a, w, layer, *, n_qkv, name, job=None):
    M, K = a.shape
    N = w.shape[2]
    tm = _tile(M, 1024)
    tn = 1024
    assert n_qkv % tn == 0 and (N - n_qkv) % tn == 0
    n_q = n_qkv // tn

    def body(a_ref, w_ref, q_ref, g_ref, ab_ref):
        j = pl.program_id(1)

        @pl.when(j == 0)
        def _():
            ab_ref[...] = a_ref[...].astype(BF16)

        res = _dot(ab_ref[...], w_ref[...])

        @pl.when(j < n_q)
        def _():
            q_ref[...] = res.astype(BF16)

        @pl.when(j >= n_q)
        def _():
            g_ref[...] = res

    return _call_carrying(
        job, body, name=name, grid=(M // tm, N // tn),
        in_specs=[pl.BlockSpec((tm, K), lambda i, j: (i, 0)),
                  pl.BlockSpec((None, K, tn), lambda i, j: (layer, 0, j))],
        out_specs=[pl.BlockSpec((tm, tn), lambda i, j: (i, jnp.minimum(j, n_q - 1))),
                   pl.BlockSpec((tm, tn), lambda i, j: (i, jnp.maximum(j - n_q, 0)))],
        out_shape=[jax.ShapeDtypeStruct((M, n_qkv), BF16), jax.ShapeDtypeStruct((M, N - n_qkv), F32)],
        scratch_shapes=[pltpu.VMEM((tm, K), BF16)], vmem_mib=48, args=(a, w))


def _mm_nt_add(a, w, layer, add, add_scale, *, name, job=None):
    M, K = a.shape
    N = w.shape[1]
    tm = _tile(M, 1024)
    tk = _tile(K, 1024)

    def body(a_ref, w_ref, add_ref, o_ref):
        @pl.when(pl.program_id(1) == 0)
        def _():
            o_ref[...] = add_scale * add_ref[...]
        o_ref[...] += _dot_nt(a_ref[...], w_ref[...])

    (out,), rode = _call_carrying(
        job, body, name=name, grid=(M // tm, K // tk),
        in_specs=[pl.BlockSpec((tm, tk), lambda i, k: (i, k)),
                  pl.BlockSpec((None, N, tk), lambda i, k: (layer, 0, k)),
                  pl.BlockSpec((tm, N), lambda i, k: (i, 0))],
        out_specs=[pl.BlockSpec((tm, N), lambda i, k: (i, 0))],
        out_shape=[jax.ShapeDtypeStruct((M, N), F32)],
        scratch_shapes=[], vmem_mib=48, args=(a, w, add))
    return out, rode


def _tn_body(k_axis, n_k):
    def body(a_ref, b_ref, o_ref, acc_ref):
        k = pl.program_id(k_axis)

        @pl.when(k == 0)
        def _():
            acc_ref[...] = jnp.zeros_like(acc_ref)
        acc_ref[...] += _dot_tn(a_ref[...].astype(BF16), b_ref[...].astype(BF16))

        @pl.when(k == n_k - 1)
        def _():
            o_ref[...] = acc_ref[...].astype(o_ref.dtype)
    return body


def _mm_tn(a, b, *, tm, tn, name):
    T, M = a.shape
    N = b.shape[1]
    tk = _tile(T, 512)
    return pl.pallas_call(
        _tn_body(2, T // tk), name=name, grid=(M // tm, N // tn, T // tk),
        in_specs=[pl.BlockSpec((tk, tm), lambda i, j, k: (k, i)),
                  pl.BlockSpec((tk, tn), lambda i, j, k: (k, j))],
        out_specs=pl.BlockSpec((tm, tn), lambda i, j, k: (i, j)),
        out_shape=jax.ShapeDtypeStruct((M, N), BF16),
        scratch_shapes=[pltpu.VMEM((tm, tn), F32)],
        compiler_params=_cparams(("parallel", "parallel", "arbitrary")),
    )(a, b)


def _mm_tn_blocked_pair(a, b1, b2, *, name):
    T, M = a.shape
    S, _, N = b1.shape
    tk = _tile(T, 1024)
    n_k = T // tk

    def body(a_ref, b1_ref, b2_ref, o1_ref, o2_ref, acc1_ref, acc2_ref):
        k = pl.program_id(1)

        @pl.when(k == 0)
        def _():
            acc1_ref[...] = jnp.zeros_like(acc1_ref)
            acc2_ref[...] = jnp.zeros_like(acc2_ref)

        a_t = a_ref[...].astype(BF16)
        acc1_ref[...] += _dot_tn(a_t, b1_ref[...])
        acc2_ref[...] += _dot_tn(a_t, b2_ref[...])

        @pl.when(k == n_k - 1)
        def _():
            o1_ref[...] = acc1_ref[...].astype(o1_ref.dtype)
            o2_ref[...] = acc2_ref[...].astype(o2_ref.dtype)

    blk = lambda: pl.BlockSpec((None, tk, N), lambda s, k: (s, k, 0))
    out = lambda: pl.BlockSpec((None, M, N), lambda s, k: (s, 0, 0))
    return pl.pallas_call(
        body, name=name, grid=(S, n_k),
        in_specs=[pl.BlockSpec((tk, M), lambda s, k: (k, 0)), blk(), blk()],
        out_specs=[out(), out()],
        out_shape=[jax.ShapeDtypeStruct((S, M, N), BF16)] * 2,
        scratch_shapes=[pltpu.VMEM((M, N), F32), pltpu.VMEM((M, N), F32)],
        compiler_params=_cparams(("parallel", "arbitrary")),
    )(a, b1, b2)


def _mm_tn_blocked_a(a, b, *, name):
    S, T, M = a.shape
    N = b.shape[1]
    tk = _tile(T, 512)
    return pl.pallas_call(
        _tn_body(1, T // tk), name=name, grid=(S, T // tk),
        in_specs=[pl.BlockSpec((None, tk, M), lambda s, k: (s, k, 0)),
                  pl.BlockSpec((tk, N), lambda s, k: (k, 0))],
        out_specs=pl.BlockSpec((None, M, N), lambda s, k: (s, 0, 0)),
        out_shape=jax.ShapeDtypeStruct((S, M, N), BF16),
        scratch_shapes=[pltpu.VMEM((M, N), F32)],
        compiler_params=_cparams(("parallel", "arbitrary")),
    )(a, b)


def _ln_fwd(r, g, b):
    mu = jnp.mean(r, axis=-1, keepdims=True)
    xc = r - mu
    var = jnp.mean(xc * xc, axis=-1, keepdims=True)
    return xc * lax.rsqrt(var + LN_EPS) * g + b


def _ln_bwd(dy, r, g):
    mu = jnp.mean(r, axis=-1, keepdims=True)
    xc = r - mu
    var = jnp.mean(xc * xc, axis=-1, keepdims=True)
    rstd = lax.rsqrt(var + LN_EPS)
    xhat = xc * rstd
    dxh = dy * g
    m1 = jnp.mean(dxh, axis=-1, keepdims=True)
    m2 = jnp.mean(dxh * xhat, axis=-1, keepdims=True)
    return rstd * (dxh - m1 - xhat * m2), xhat


def _lane_is_head0():
    return lax.broadcasted_iota(jnp.int32, (1, LANES), 1) < HEAD_DIM


def _band_shape():
    a = np.arange(A_TQ)[:, None] // CHUNK
    b = np.arange(A_WIN)[None, :] // CHUNK
    return (b >= a) & (b <= a + LEFT_CHUNKS)


def _band_streams(strip):
    return [dict(h=h, n=strip, rows=pl.ds(r0, strip)) for h in range(2) for r0 in range(0, A_TQ, strip)]


def _band_scores(st, i, qh_ref, k2, bias_ref):
    s = _dot_nt(qh_ref[st["h"], st["rows"], :], k2) * QK_SCALE + bias_ref[st["h"], st["rows"], :]
    c = lax.broadcasted_iota(jnp.int32, (st["n"], A_WIN), 1)
    st["s"] = jnp.where(c >= LEFT_CHUNKS * CHUNK - i * A_TQ, s, NEG)


def _band_softmax(st):
    s = st.pop("s")
    e = jnp.exp(s - jnp.max(s, axis=1, keepdims=True))
    st["p"] = e / jnp.sum(e, axis=1, keepdims=True)


def _attn_a_fwd(qkv, kvpad, bias, *, name, job=None):
    T = qkv.shape[0]
    n_hp = WIDTH // LANES

    def body(q_ref, k_ref, v_ref, bias_ref, o_ref, qh_ref, acc_ref):
        i = pl.program_id(1)
        row0 = pl.multiple_of(i * A_TQ, A_TQ)
        q2 = q_ref[...]
        k2 = k_ref[pl.ds(row0, A_WIN), :]
        v2 = v_ref[pl.ds(row0, A_WIN), :]
        head0 = _lane_is_head0()
        qh_ref[0] = jnp.where(head0, q2, jnp.zeros_like(q2))
        qh_ref[1] = jnp.where(head0, jnp.zeros_like(q2), q2)

        def scores(st):
            _band_scores(st, i, qh_ref, k2, bias_ref)

        def values(st):
            acc_ref[st["h"], st["rows"], :] = _dot(st.pop("p").astype(BF16), v2)

        _skewed(_band_streams(A_STRIP), [scores, _band_softmax, values])
        o_ref[...] = jnp.where(head0, acc_ref[0], acc_ref[1]).astype(o_ref.dtype)

    (out,), rode = _call_carrying(
        job, body, name=name, grid=(n_hp, T // A_TQ),
        in_specs=[pl.BlockSpec((A_TQ, LANES), lambda hp, i: (i, hp)),
                  pl.BlockSpec((T + A_WIN - A_TQ, LANES), lambda hp, i: (0, hp)),
                  pl.BlockSpec((T + A_WIN - A_TQ, LANES), lambda hp, i: (0, hp + n_hp)),
                  pl.BlockSpec((2, A_TQ, A_WIN), lambda hp, i: (hp, 0, 0))],
        out_specs=[pl.BlockSpec((A_TQ, LANES), lambda hp, i: (i, hp))],
        out_shape=[jax.ShapeDtypeStruct((T, WIDTH), BF16)],
        scratch_shapes=[pltpu.VMEM((2, A_TQ, LANES), BF16), pltpu.VMEM((2, A_TQ, LANES), F32)],
        vmem_mib=48, args=(qkv, kvpad, kvpad, bias))
    return out, rode


def _attn_a_bwd(qkv, kvpad, bias, do, *, name, job=None):
    T = qkv.shape[0]
    TP = T + A_WIN - A_TQ
    n_hp = WIDTH // LANES

    def body(q_ref, k_ref, v_ref, bias_ref, do_ref, dq_ref, dko_ref, dvo_ref, db_ref,
             qh_ref, doh_ref, dqa_ref, dk_ref, dv_ref):
        i = pl.program_id(1)

        @pl.when(i == 0)
        def _():
            dk_ref[...] = jnp.zeros_like(dk_ref)
            dv_ref[...] = jnp.zeros_like(dv_ref)
            db_ref[...] = jnp.zeros_like(db_ref)

        row0 = pl.multiple_of(i * A_TQ, A_TQ)
        window = pl.ds(row0, A_WIN)
        q2 = q_ref[...]
        do2 = do_ref[...]
        k2 = k_ref[window, :]
        v2 = v_ref[window, :]
        head0 = _lane_is_head0()
        zero = jnp.zeros_like(q2)
        qh_ref[0] = jnp.where(head0, q2, zero)
        qh_ref[1] = jnp.where(head0, zero, q2)
        doh_ref[0] = jnp.where(head0, do2, zero)
        doh_ref[1] = jnp.where(head0, zero, do2)
        dk = [[], []]
        dv = [[], []]

        def scores(st):
            _band_scores(st, i, qh_ref, k2, bias_ref)
            st["dp"] = _dot_nt(doh_ref[st["h"], st["rows"], :], v2)

        def dscores(st):
            _band_softmax(st)
            p, dp = st.pop("p"), st.pop("dp")
            ds = p * (dp - jnp.sum(p * dp, axis=1, keepdims=True))
            db_ref[st["h"], st["rows"], :] += ds
            st["dsb"] = (ds * QK_SCALE).astype(BF16)
            st["pb"] = p.astype(BF16)

        def grads(st):
            h, rows = st["h"], st["rows"]
            dsb = st.pop("dsb")
            dqa_ref[h, rows, :] = _dot(dsb, k2)
            dk[h].append(_dot_tn(dsb, q_ref[rows, :]))
            dv[h].append(_dot_tn(st.pop("pb"), do_ref[rows, :]))

        _skewed(_band_streams(A_TQ), [scores, dscores, grads])
        dq_ref[...] = jnp.where(head0, dqa_ref[0], dqa_ref[1]).astype(dq_ref.dtype)
        dk_ref[window, :] += jnp.where(head0, sum(dk[0]), sum(dk[1]))
        dv_ref[window, :] += jnp.where(head0, sum(dv[0]), sum(dv[1]))

        @pl.when(i == T // A_TQ - 1)
        def _():
            dko_ref[...] = dk_ref[TP - T:, :].astype(BF16)
            dvo_ref[...] = dv_ref[TP - T:, :].astype(BF16)

    return _call_carrying(
        job, body, name=name, grid=(n_hp, T // A_TQ),
        in_specs=[pl.BlockSpec((A_TQ, LANES), lambda hp, i: (i, hp)),
                  pl.BlockSpec((TP, LANES), lambda hp, i: (0, hp)),
                  pl.BlockSpec((TP, LANES), lambda hp, i: (0, hp + n_hp)),
                  pl.BlockSpec((2, A_TQ, A_WIN), lambda hp, i: (hp, 0, 0)),
                  pl.BlockSpec((A_TQ, LANES), lambda hp, i: (i, hp))],
        out_specs=[pl.BlockSpec((A_TQ, LANES), lambda hp, i: (i, hp)),
                   pl.BlockSpec((T, LANES), lambda hp, i: (0, hp)),
                   pl.BlockSpec((T, LANES), lambda hp, i: (0, hp)),
                   pl.BlockSpec((2, A_TQ, A_WIN), lambda hp, i: (hp, 0, 0))],
        out_shape=[jax.ShapeDtypeStruct((T, WIDTH), BF16),
                   jax.ShapeDtypeStruct((T, WIDTH), BF16),
                   jax.ShapeDtypeStruct((T, WIDTH), BF16),
                   jax.ShapeDtypeStruct((WIDTH // HEAD_DIM, A_TQ, A_WIN), F32)],
        scratch_shapes=[pltpu.VMEM((2, A_TQ, LANES), BF16), pltpu.VMEM((2, A_TQ, LANES), BF16),
                        pltpu.VMEM((2, A_TQ, LANES), F32),
                        pltpu.VMEM((TP, LANES), F32), pltpu.VMEM((TP, LANES), F32)],
        vmem_mib=56, args=(qkv, kvpad, kvpad, bias, do))


_N_DIAG = 2 * CHUNK - 1
_EXT_TOP = LEFT_CHUNKS * CHUNK + CHUNK - 1 + REL_CLIP


def _toeplitz_bias(rb):
    H = rb.shape[0]
    ext = jnp.concatenate([rb, jnp.broadcast_to(rb[:, N_REL - 1:], (H, _EXT_TOP + 1 - N_REL))], axis=1)
    vec = jnp.stack([ext[:, _EXT_TOP - (_N_DIAG - 1) - CHUNK * k:_EXT_TOP - CHUNK * k + 1]
                     for k in range(LEFT_CHUNKS + 1)], axis=1)
    rev = jnp.pad(vec[:, :, ::-1], ((0, 0), (0, 0), (0, 1)))
    flat = jnp.broadcast_to(rev[:, :, None, :], (H, LEFT_CHUNKS + 1, CHUNK, _N_DIAG + 1))
    skew = flat.reshape(H, LEFT_CHUNKS + 1, -1)[:, :, :CHUNK * _N_DIAG].reshape(H, LEFT_CHUNKS + 1, CHUNK, _N_DIAG)
    blocks = skew[:, :, :, CHUNK - 1:]
    neg = jnp.full((H, CHUNK, CHUNK), NEG, F32)
    rows = [jnp.concatenate([blocks[:, b - a] if 0 <= b - a <= LEFT_CHUNKS else neg for b in range(A_WIN // CHUNK)],
                            axis=2) for a in range(A_TQ // CHUNK)]
    return jnp.concatenate(rows, axis=1)


def _toeplitz_bias_grad(db):
    H = db.shape[0]
    d5 = db.reshape(H, A_TQ // CHUNK, CHUNK, A_WIN // CHUNK, CHUNK)
    g_blocks = jnp.stack([sum(d5[:, a, :, a + k, :] for a in range(A_TQ // CHUNK))
                          for k in range(LEFT_CHUNKS + 1)], axis=1)
    d_skew = jnp.pad(g_blocks, ((0, 0), (0, 0), (0, 0), (CHUNK - 1, 0)))
    d_flat = jnp.pad(d_skew.reshape(H, LEFT_CHUNKS + 1, CHUNK * _N_DIAG), ((0, 0), (0, 0), (0, CHUNK)))
    g_vec = jnp.sum(d_flat.reshape(H, LEFT_CHUNKS + 1, CHUNK, _N_DIAG + 1), axis=2)[:, :, :_N_DIAG][:, :, ::-1]
    g_ext = sum(jnp.pad(g_vec[:, k], ((0, 0), (_EXT_TOP - (_N_DIAG - 1) - CHUNK * k, CHUNK * k)))
                for k in range(LEFT_CHUNKS + 1))
    return jnp.concatenate([g_ext[:, :N_REL - 1], jnp.sum(g_ext[:, N_REL - 1:], axis=1, keepdims=True)], axis=1)


def _split_bf16(x):
    hi = x.astype(BF16)
    lo = (x - hi.astype(F32)).astype(BF16)
    return hi, lo


def _sb_streams(d, strips=None, **tile):
    out = []
    for h in range(2):
        for r in (range(B_TQ // B_TS) if strips is None else strips):
            if d is not None and d > r:
                continue
            out.append(dict(h=h, r=r, rows=pl.ds(r * B_TS, B_TS), diag=(d is not None and d == r), **tile))
    return out


def _sb_sweep(i, car_ref, streams_of, run):
    sub = B_TQ // B_TS
    run([st for d in reversed(range(sub)) for st in streams_of(i * sub + d, d, None)])

    def alive(c):
        return (c[0] < i * sub) & (c[1] > B_DEAD)

    def step(c):
        kb = i * sub - 1 - c[0]
        if sub > 1:
            lower_alive = jnp.max(car_ref[:, B_TS:, :]) > B_DEAD
            lax.cond(lower_alive, lambda: run(streams_of(kb, None, None)), lambda: run(streams_of(kb, None, [0])))
        else:
            run(streams_of(kb, None, None))
        return c[0] + 1, jnp.max(car_ref[...])

    lax.while_loop(alive, step, (jnp.int32(0), jnp.float32(0.0)))


def _piece_rows(st, p):
    return pl.ds(st["r"] * B_TS + p, B_PIECE)


def _rows_cat(parts):
    return jnp.concatenate(parts, axis=0)


def _skewed(streams, stages):
    for t in range(len(streams) + len(stages) - 1):
        for s, st in enumerate(streams):
            if 0 <= t - s < len(stages):
                stages[t - s](st)


def _sb_logs(st, z2):
    log_beta, log_keep, keep_bf = [], [], []
    for p in range(0, B_TS, B_PIECE):
        z = z2[p:p + B_PIECE]
        lp2 = jnp.log(1.0 + jnp.exp2(-jnp.abs(z))) * LOG2E
        lb = jnp.minimum(z, 0.0) - lp2
        lk = lb - z
        if st["diag"]:
            lk = jnp.where(_strict_lower(p), lk, 0.0)
        log_beta.append(lb)
        log_keep.append(lk)
        keep_bf.append(lk.astype(BF16))
    st["log_beta"] = _rows_cat(log_beta)
    st["log_keep"] = _rows_cat(log_keep)
    st["keep_bf"] = _rows_cat(keep_bf)


def _strict_lower(p):
    t = p + lax.broadcasted_iota(jnp.int32, (B_PIECE, B_TS), 0)
    s = lax.broadcasted_iota(jnp.int32, (B_PIECE, B_TS), 1)
    return s < t


def _tri(strict):
    j = lax.broadcasted_iota(jnp.int32, (B_TS, B_TS), 0)
    s = lax.broadcasted_iota(jnp.int32, (B_TS, B_TS), 1)
    return jnp.where(j > s if strict else j >= s, 1.0, 0.0).astype(BF16)


def _call_carrying(job, body, *, name, grid, in_specs, out_specs, out_shape, scratch_shapes, vmem_mib, args):
    n_in, n_out, n_scr = len(in_specs), len(out_specs), len(scratch_shapes)
    if job is None:
        res = pl.pallas_call(body, name=name, grid=grid, in_specs=in_specs, out_specs=out_specs,
                             out_shape=out_shape, scratch_shapes=scratch_shapes,
                             compiler_params=_cparams(("arbitrary",) * len(grid), vmem_mib))(*args)
        return res, []
    j_in, j_out = len(job.arrays), len(job.out_shape)
    hbm = pl.BlockSpec(memory_space=pltpu.HBM)

    def carrying(*refs):
        refs = list(refs)
        ins, refs = refs[:n_in], refs[n_in:]
        j_ins, refs = refs[:j_in], refs[j_in:]
        outs, refs = refs[:n_out], refs[n_out:]
        j_outs, refs = refs[:j_out], refs[j_out:]
        scr, sems = refs[:n_scr], refs[n_scr:]
        first = functools.reduce(jnp.logical_and, [pl.program_id(d) == 0 for d in range(len(grid))])
        last = functools.reduce(jnp.logical_and, [pl.program_id(d) == grid[d] - 1 for d in range(len(grid))])

        @pl.when(first)
        def _():
            job.start(j_ins, j_outs, sems)

        body(*ins, *outs, *scr)

        @pl.when(last)
        def _():
            job.wait(j_ins, j_outs, sems)

    res = pl.pallas_call(
        carrying, name=name, grid=grid,
        in_specs=list(in_specs) + [hbm] * j_in, out_specs=list(out_specs) + [hbm] * j_out,
        out_shape=list(out_shape) + job.out_shape, scratch_shapes=list(scratch_shapes) + job.scratch(),
        compiler_params=_cparams(("arbitrary",) * len(grid), vmem_mib))(*args, *job.arrays)
    return res[:n_out], res[n_out:]


def _attn_b_fwd(qkv, *, col0, name, job=None):
    T = qkv.shape[0]
    n_hp = WIDTH // LANES
    sub = B_TQ // B_TS

    def body(q_ref, k_ref, v_ref, o_ref, acc_ref, car_ref, qh_ref):
        i = pl.program_id(1)
        q2 = q_ref[...]
        head0 = _lane_is_head0()
        qh_ref[0] = jnp.where(head0, q2, jnp.zeros_like(q2))
        qh_ref[1] = jnp.where(head0, jnp.zeros_like(q2), q2)
        tri_s = _tri(True)
        acc_ref[...] = jnp.zeros_like(acc_ref)
        car_ref[...] = jnp.zeros_like(car_ref)

        def streams_of(kb, d, strips):
            keys = pl.ds(pl.multiple_of(kb * B_TS, B_TS), B_TS)
            return _sb_streams(d, strips, k2=k_ref[keys, :], v2=v_ref[keys, :])

        def scores(st):
            st["z2"] = _dot_nt(qh_ref[st["h"], st["rows"], :], st.pop("k2")) * (QK_SCALE * LOG2E)

        def logs(st):
            _sb_logs(st, st.pop("z2"))

        def suffix(st):
            st["suffix"] = _dot(st.pop("keep_bf"), tri_s)

        def weights(st):
            log_beta, suffix, log_keep = st.pop("log_beta"), st.pop("suffix"), st.pop("log_keep")
            wb = []
            for p in range(0, B_TS, B_PIECE):
                rows = _piece_rows(st, p)
                car = car_ref[st["h"], rows, :]
                w = jnp.exp2(log_beta[p:p + B_PIECE] + suffix[p:p + B_PIECE] + car)
                if st["diag"]:
                    w = jnp.where(_strict_lower(p), w, 0.0)
                wb.append(w.astype(BF16))
                car_ref[st["h"], rows, :] = car + jnp.sum(log_keep[p:p + B_PIECE], axis=1, keepdims=True)
            st["wb"] = _rows_cat(wb)

        def values(st):
            acc_ref[st["h"], st["rows"], :] += _dot(st.pop("wb"), st.pop("v2"))

        _sb_sweep(i, car_ref, streams_of, lambda sts: _skewed(sts, [scores, logs, suffix, weights, values]))
        o_ref[...] = jnp.where(head0, acc_ref[0], acc_ref[1])

    (out,), rode = _call_carrying(
        job, body, name=name, grid=(n_hp, T // B_TQ),
        in_specs=[pl.BlockSpec((B_TQ, LANES), lambda hp, i: (i, hp + col0)),
                  pl.BlockSpec((T, LANES), lambda hp, i: (0, hp + col0 + n_hp)),
                  pl.BlockSpec((T, LANES), lambda hp, i: (0, hp + col0 + 2 * n_hp))],
        out_specs=[pl.BlockSpec((B_TQ, LANES), lambda hp, i: (i, hp))],
        out_shape=[jax.ShapeDtypeStruct((T, WIDTH), F32)],
        scratch_shapes=[pltpu.VMEM((2, B_TQ, LANES), F32), pltpu.VMEM((2, B_TQ, 1), F32),
                        pltpu.VMEM((2, B_TQ, LANES), BF16)],
        vmem_mib=48, args=(qkv, qkv, qkv))
    return out, rode


def _attn_b_bwd(qkv, out, do, *, col0, name, job=None):
    T = qkv.shape[0]
    n_hp = WIDTH // LANES
    sub = B_TQ // B_TS

    def body(q_ref, k_ref, v_ref, o_ref, do_ref, dq_ref, dko_ref, dvo_ref,
             dqa_ref, car_ref, carr_ref, tot_ref, qh_ref, doh_ref, qs_ref, dk_ref, dv_ref):
        i = pl.program_id(1)

        @pl.when(i == 0)
        def _():
            dk_ref[...] = jnp.zeros_like(dk_ref)
            dv_ref[...] = jnp.zeros_like(dv_ref)

        q2 = q_ref[...]
        do2 = do_ref[...]
        head0 = _lane_is_head0()
        zero = jnp.zeros_like(q2)
        qh_ref[0] = jnp.where(head0, q2, zero)
        qh_ref[1] = jnp.where(head0, zero, q2)
        doh_ref[0] = jnp.where(head0, do2, zero)
        doh_ref[1] = jnp.where(head0, zero, do2)
        scale = jnp.asarray(QK_SCALE, BF16)
        qs_ref[...] = q2 * scale
        tri_s = _tri(True)
        tri_i = _tri(False)
        prod = do2.astype(F32) * o_ref[...]
        tot_ref[0] = jnp.sum(jnp.where(head0, prod, 0.0), axis=1, keepdims=True)
        tot_ref[1] = jnp.sum(jnp.where(head0, 0.0, prod), axis=1, keepdims=True)
        dqa_ref[...] = jnp.zeros_like(dqa_ref)
        car_ref[...] = jnp.zeros_like(car_ref)
        carr_ref[...] = jnp.zeros_like(carr_ref)

        def streams_of(kb, d, strips):
            keys = pl.ds(pl.multiple_of(kb * B_TS, B_TS), B_TS)
            k2 = k_ref[keys, :]
            return _sb_streams(d, strips, keys=keys, k2=k2, v2=v_ref[keys, :], k2s=k2 * scale)

        def run(streams):
            def scores(st):
                st["z2"] = _dot_nt(qh_ref[st["h"], st["rows"], :], st.pop("k2")) * (QK_SCALE * LOG2E)
                st["dw"] = _dot_nt(doh_ref[st["h"], st["rows"], :], st.pop("v2"))

            def logs(st):
                _sb_logs(st, st.pop("z2"))

            def suffix(st):
                st["suffix"] = _dot(st.pop("keep_bf"), tri_s)

            def weights(st):
                h = st["h"]
                suffix, dw = st.pop("suffix"), st.pop("dw")
                wb, dlog, hi, lo = [], [], [], []
                for p in range(0, B_TS, B_PIECE):
                    rows = _piece_rows(st, p)
                    car = car_ref[h, rows, :]
                    w = jnp.exp2(st["log_beta"][p:p + B_PIECE] + suffix[p:p + B_PIECE] + car)
                    if st["diag"]:
                        w = jnp.where(_strict_lower(p), w, 0.0)
                    w = w.astype(BF16)
                    dl = w.astype(F32) * dw[p:p + B_PIECE]
                    dl_hi, dl_lo = _split_bf16(dl)
                    wb.append(w)
                    dlog.append(dl)
                    hi.append(dl_hi)
                    lo.append(dl_lo)
                    car_ref[h, rows, :] = car + jnp.sum(st["log_keep"][p:p + B_PIECE], axis=1, keepdims=True)
                st["wb"], st["dlog"], st["hi"], st["lo"] = _rows_cat(wb), _rows_cat(dlog), _rows_cat(hi), _rows_cat(lo)

            def later(st):
                st["later"] = _dot(st.pop("hi"), tri_i) + _dot(st.pop("lo"), tri_i)

            def dscores(st):
                h = st["h"]
                later, dlog = st.pop("later"), st.pop("dlog")
                log_keep, log_beta = st.pop("log_keep"), st.pop("log_beta")
                dzb = []
                for p in range(0, B_TS, B_PIECE):
                    rows = _piece_rows(st, p)
                    pc = slice(p, p + B_PIECE)
                    carr = carr_ref[h, rows, :]
                    earlier = tot_ref[h, rows, :] - (later[pc] + carr)
                    dz = dlog[pc] * jnp.exp2(log_keep[pc]) - jnp.exp2(log_beta[pc]) * earlier
                    if st["diag"]:
                        dz = jnp.where(_strict_lower(p), dz, 0.0)
                    dzb.append(dz.astype(BF16))
                    carr_ref[h, rows, :] = carr + jnp.sum(dlog[pc], axis=1, keepdims=True)
                st["dzb"] = _rows_cat(dzb)

            def grads(st):
                h, rows, keys = st["h"], st["rows"], st["keys"]
                mine = head0 if h == 0 else jnp.logical_not(head0)
                dzb = st.pop("dzb")
                dqa_ref[h, rows, :] += _dot(dzb, st.pop("k2s"))
                dk_ref[keys, :] += jnp.where(mine, _dot_tn(dzb, qs_ref[rows, :]), 0.0)
                dv_ref[keys, :] += jnp.where(mine, _dot_tn(st.pop("wb"), do_ref[rows, :]), 0.0)

            _skewed(streams, [scores, logs, suffix, weights, later, dscores, grads])

        _sb_sweep(i, car_ref, streams_of, run)
        dq_ref[...] = jnp.where(head0, dqa_ref[0], dqa_ref[1]).astype(dq_ref.dtype)

        @pl.when(i == T // B_TQ - 1)
        def _():
            dko_ref[...] = dk_ref[...].astype(BF16)
            dvo_ref[...] = dv_ref[...].astype(BF16)

    return _call_carrying(
        job, body, name=name, grid=(n_hp, T // B_TQ),
        in_specs=[pl.BlockSpec((B_TQ, LANES), lambda hp, i: (i, hp + col0)),
                  pl.BlockSpec((T, LANES), lambda hp, i: (0, hp + col0 + n_hp)),
                  pl.BlockSpec((T, LANES), lambda hp, i: (0, hp + col0 + 2 * n_hp)),
                  pl.BlockSpec((B_TQ, LANES), lambda hp, i: (i, hp)),
                  pl.BlockSpec((B_TQ, LANES), lambda hp, i: (i, hp))],
        out_specs=[pl.BlockSpec((B_TQ, LANES), lambda hp, i: (i, hp)),
                   pl.BlockSpec((T, LANES), lambda hp, i: (0, hp)),
                   pl.BlockSpec((T, LANES), lambda hp, i: (0, hp))],
        out_shape=[jax.ShapeDtypeStruct((T, WIDTH), BF16),
                   jax.ShapeDtypeStruct((T, WIDTH), BF16),
                   jax.ShapeDtypeStruct((T, WIDTH), BF16)],
        scratch_shapes=[pltpu.VMEM((2, B_TQ, LANES), F32), pltpu.VMEM((2, B_TQ, 1), F32),
                        pltpu.VMEM((2, B_TQ, 1), F32), pltpu.VMEM((2, B_TQ, 1), F32),
                        pltpu.VMEM((2, B_TQ, LANES), BF16), pltpu.VMEM((2, B_TQ, LANES), BF16),
                        pltpu.VMEM((B_TQ, LANES), BF16),
                        pltpu.VMEM((T, LANES), F32), pltpu.VMEM((T, LANES), F32)],
        vmem_mib=56, args=(qkv, qkv, qkv, out, do))


def _gated_mix(oa_ref, ob_ref, g_ref, bg_ref, wpa_ref, wpb_ref, D):
    ya = _dot(oa_ref[...].astype(BF16), wpa_ref[...])
    yb = _dot(ob_ref[...].astype(BF16), wpb_ref[...])
    sa = jax.nn.sigmoid(g_ref[:, :D] + bg_ref[:, :D])
    sb = jax.nn.sigmoid(g_ref[:, D:] + bg_ref[:, D:])
    return ya, yb, sa, sb


def _proj_fwd(oa, ob, g, bg, wpa, wpb, wo, wl, xin, lng, lnb, layer, *, alpha, name):
    T, D = xin.shape
    tm = _tile(T, 512)
    row = lambda i: (i, 0)
    wspec = lambda r, c: pl.BlockSpec((None, r, c), lambda i: (wl, 0, 0))
    vec = lambda c: pl.BlockSpec((None, 1, c), lambda i: (layer, 0, 0))

    def body(oa_ref, ob_ref, g_ref, bg_ref, wpa_ref, wpb_ref, wo_ref, x_ref, lg_ref, lb_ref, x1_ref, r1_ref, x1b_ref):
        ya, yb, sa, sb = _gated_mix(oa_ref, ob_ref, g_ref, bg_ref, wpa_ref, wpb_ref, D)
        mix = _dot((sa * ya + sb * yb).astype(BF16), wo_ref[...])
        r1 = alpha * x_ref[...] + mix
        r1_ref[...] = r1
        x1 = _ln_fwd(r1, lg_ref[...], lb_ref[...])
        x1_ref[...] = x1
        x1b_ref[...] = x1.astype(BF16)

    return pl.pallas_call(
        body, name=name, grid=(T // tm,),
        in_specs=[pl.BlockSpec((tm, WIDTH), row), pl.BlockSpec((tm, WIDTH), row), pl.BlockSpec((tm, 2 * D), row),
                  vec(2 * D), wspec(WIDTH, D), wspec(WIDTH, D), wspec(D, D),
                  pl.BlockSpec((tm, D), row), vec(D), vec(D)],
        out_specs=[pl.BlockSpec((tm, D), row), pl.BlockSpec((tm, D), row), pl.BlockSpec((tm, D), row)],
        out_shape=[jax.ShapeDtypeStruct((T, D), F32), jax.ShapeDtypeStruct((T, D), F32),
                   jax.ShapeDtypeStruct((T, D), BF16)],
        compiler_params=_cparams(("arbitrary",), 56),
    )(oa, ob, g, bg, wpa, wpb, wo, xin, lng, lnb)


def _proj_bwd(dx1, r1, lng, oa, ob, g, bg, wpa, wpb, wo, wl, layer, *, name):
    T, D = dx1.shape
    tm = _tile(T, 512)
    row = lambda i: (i, 0)
    fixed = lambda i: (0, 0)
    wspec = lambda r, c: pl.BlockSpec((None, r, c), lambda i: (wl, 0, 0))
    vec = lambda c: pl.BlockSpec((None, 1, c), lambda i: (layer, 0, 0))

    def body(dx_ref, r1_ref, lg_ref, oa_ref, ob_ref, g_ref, bg_ref, wpa_ref, wpb_ref, wo_ref,
             dr_ref, mix_ref, dya_ref, dyb_ref, dg_ref, doa_ref, dob_ref, dlg_ref, dlb_ref, dbg_ref):
        @pl.when(pl.program_id(0) == 0)
        def _():
            dlg_ref[...] = jnp.zeros_like(dlg_ref)
            dlb_ref[...] = jnp.zeros_like(dlb_ref)
            dbg_ref[...] = jnp.zeros_like(dbg_ref)

        dx = dx_ref[...]
        dr, xhat = _ln_bwd(dx, r1_ref[...], lg_ref[...])
        dr_ref[...] = dr
        dlg_ref[...] += jnp.sum(dx * xhat, axis=0, keepdims=True)
        dlb_ref[...] += jnp.sum(dx, axis=0, keepdims=True)
        dmix = _dot_nt(dr.astype(BF16), wo_ref[...])
        ya, yb, sa, sb = _gated_mix(oa_ref, ob_ref, g_ref, bg_ref, wpa_ref, wpb_ref, D)
        mix_ref[...] = (sa * ya + sb * yb).astype(BF16)
        dya = (dmix * sa).astype(BF16)
        dyb = (dmix * sb).astype(BF16)
        dya_ref[...] = dya
        dyb_ref[...] = dyb
        dga = dmix * ya * (sa * (1.0 - sa))
        dgb = dmix * yb * (sb * (1.0 - sb))
        dg_ref[:, :D] = dga.astype(BF16)
        dg_ref[:, D:] = dgb.astype(BF16)
        dbg_ref[:, :D] += jnp.sum(dga, axis=0, keepdims=True)
        dbg_ref[:, D:] += jnp.sum(dgb, axis=0, keepdims=True)
        doa_ref[...] = _dot_nt(dya, wpa_ref[...]).astype(BF16)
        dob_ref[...] = _dot_nt(dyb, wpb_ref[...]).astype(BF16)

    return pl.pallas_call(
        body, name=name, grid=(T // tm,),
        in_specs=[pl.BlockSpec((tm, D), row), pl.BlockSpec((tm, D), row), vec(D),
                  pl.BlockSpec((tm, WIDTH), row), pl.BlockSpec((tm, WIDTH), row), pl.BlockSpec((tm, 2 * D), row),
                  vec(2 * D), wspec(WIDTH, D), wspec(WIDTH, D), wspec(D, D)],
        out_specs=[pl.BlockSpec((tm, D), row), pl.BlockSpec((tm, D), row), pl.BlockSpec((tm, D), row),
                   pl.BlockSpec((tm, D), row), pl.BlockSpec((tm, 2 * D), row),
                   pl.BlockSpec((tm, WIDTH), row), pl.BlockSpec((tm, WIDTH), row),
                   pl.BlockSpec((1, D), fixed), pl.BlockSpec((1, D), fixed), pl.BlockSpec((1, 2 * D), fixed)],
        out_shape=[jax.ShapeDtypeStruct((T, D), F32), jax.ShapeDtypeStruct((T, D), BF16),
                   jax.ShapeDtypeStruct((T, D), BF16), jax.ShapeDtypeStruct((T, D), BF16),
                   jax.ShapeDtypeStruct((T, 2 * D), BF16),
                   jax.ShapeDtypeStruct((T, WIDTH), BF16), jax.ShapeDtypeStruct((T, WIDTH), BF16),
                   jax.ShapeDtypeStruct((1, D), F32), jax.ShapeDtypeStruct((1, D), F32),
                   jax.ShapeDtypeStruct((1, 2 * D), F32)],
        compiler_params=_cparams(("arbitrary",), 56),
    )(dx1, r1, lng, oa, ob, g, bg, wpa, wpb, wo)


def _ffn_fwd(x1, wfi, wfo, wl, lng, lnb, layer, *, alpha, name, job=None):
    T, D = x1.shape
    tf = wfi.shape[-1]
    nj = wfi.shape[0] // 2
    tm = _tile(T, 512)
    vec = lambda c: pl.BlockSpec((None, 1, c), lambda i, j: (layer, 0, 0))

    def body(x_ref, wg_ref, wu_ref, wo_ref, lg_ref, lb_ref, gs_ref, us_ref, r2_ref, x2_ref, acc_ref, xb_ref):
        j = pl.program_id(1)

        @pl.when(j == 0)
        def _():
            xb_ref[...] = x_ref[...].astype(BF16)
            acc_ref[...] = jnp.zeros_like(acc_ref)

        gv = _dot(xb_ref[...], wg_ref[...])
        uv = _dot(xb_ref[...], wu_ref[...])
        gs_ref[...] = gv.astype(gs_ref.dtype)
        us_ref[...] = uv.astype(us_ref.dtype)
        act = gv * jax.nn.sigmoid(gv) * uv
        acc_ref[...] += _dot(act.astype(BF16), wo_ref[...])

        @pl.when(j == nj - 1)
        def _():
            r2 = alpha * x_ref[...] + acc_ref[...]
            r2_ref[...] = r2
            x2_ref[...] = _ln_fwd(r2, lg_ref[...], lb_ref[...])

    return _call_carrying(
        job, body, name=name, grid=(T // tm, nj),
        in_specs=[pl.BlockSpec((tm, D), lambda i, j: (i, 0)),
                  pl.BlockSpec((None, None, D, tf), lambda i, j: (j, wl, 0, 0)),
                  pl.BlockSpec((None, None, D, tf), lambda i, j: (j + nj, wl, 0, 0)),
                  pl.BlockSpec((None, tf, D), lambda i, j: (wl, j, 0)),
                  vec(D), vec(D)],
        out_specs=[pl.BlockSpec((None, tm, tf), lambda i, j: (j, i, 0)),
                   pl.BlockSpec((None, tm, tf), lambda i, j: (j, i, 0)),
                   pl.BlockSpec((tm, D), lambda i, j: (i, 0)),
                   pl.BlockSpec((tm, D), lambda i, j: (i, 0))],
        out_shape=[jax.ShapeDtypeStruct((nj, T, tf), BF16), jax.ShapeDtypeStruct((nj, T, tf), BF16),
                   jax.ShapeDtypeStruct((T, D), F32), jax.ShapeDtypeStruct((T, D), F32)],
        scratch_shapes=[pltpu.VMEM((tm, D), F32), pltpu.VMEM((tm, D), BF16)],
        vmem_mib=56, args=(x1, wfi, wfi, wfo, lng, lnb))


def _ffn_bwd(dx2, r2, lng, gs, us, wfi, wfo, wl, layer, *, alpha, name, job=None):
    T, D = dx2.shape
    tf = wfi.shape[-1]
    nj = wfi.shape[0] // 2
    tm = _tile(T, 512)
    vec = lambda c: pl.BlockSpec((None, 1, c), lambda i, j: (layer, 0, 0))
    blk = lambda: pl.BlockSpec((None, tm, tf), lambda i, j: (j, i, 0))

    def body(dx_ref, r2_ref, lg_ref, gs_ref, us_ref, wg_ref, wu_ref, wo_ref,
             dr_ref, act_ref, dg_ref, du_ref, dx1_ref, dlg_ref, dlb_ref, acc_ref, drb_ref):
        i = pl.program_id(0)
        j = pl.program_id(1)

        @pl.when((i == 0) & (j == 0))
        def _():
            dlg_ref[...] = jnp.zeros_like(dlg_ref)
            dlb_ref[...] = jnp.zeros_like(dlb_ref)

        @pl.when(j == 0)
        def _():
            dx = dx_ref[...]
            dr, xhat = _ln_bwd(dx, r2_ref[...], lg_ref[...])
            dlg_ref[...] += jnp.sum(dx * xhat, axis=0, keepdims=True)
            dlb_ref[...] += jnp.sum(dx, axis=0, keepdims=True)
            drb_ref[...] = dr.astype(BF16)
            dr_ref[...] = dr.astype(BF16)
            acc_ref[...] = alpha * dr

        dact = _dot_nt(drb_ref[...], wo_ref[...])
        gv = gs_ref[...].astype(F32)
        uv = us_ref[...].astype(F32)
        s = jax.nn.sigmoid(gv)
        silu = gv * s
        act_ref[...] = (silu * uv).astype(BF16)
        dg = (dact * uv * (s * (1.0 + gv * (1.0 - s)))).astype(BF16)
        du = (dact * silu).astype(BF16)
        dg_ref[...] = dg
        du_ref[...] = du
        acc_ref[...] += _dot_nt(dg, wg_ref[...]) + _dot_nt(du, wu_ref[...])

        @pl.when(j == nj - 1)
        def _():
            dx1_ref[...] = acc_ref[...]

    return _call_carrying(
        job, body, name=name, grid=(T // tm, nj),
        in_specs=[pl.BlockSpec((tm, D), lambda i, j: (i, 0)), pl.BlockSpec((tm, D), lambda i, j: (i, 0)), vec(D),
                  blk(), blk(),
                  pl.BlockSpec((None, None, D, tf), lambda i, j: (j, wl, 0, 0)),
                  pl.BlockSpec((None, None, D, tf), lambda i, j: (j + nj, wl, 0, 0)),
                  pl.BlockSpec((None, tf, D), lambda i, j: (wl, j, 0))],
        out_specs=[pl.BlockSpec((tm, D), lambda i, j: (i, 0)), blk(), blk(), blk(),
                   pl.BlockSpec((tm, D), lambda i, j: (i, 0)),
                   pl.BlockSpec((1, D), lambda i, j: (0, 0)), pl.BlockSpec((1, D), lambda i, j: (0, 0))],
        out_shape=[jax.ShapeDtypeStruct((T, D), BF16),
                   jax.ShapeDtypeStruct((nj, T, tf), BF16), jax.ShapeDtypeStruct((nj, T, tf), BF16),
                   jax.ShapeDtypeStruct((nj, T, tf), BF16),
                   jax.ShapeDtypeStruct((T, D), F32),
                   jax.ShapeDtypeStruct((1, D), F32), jax.ShapeDtypeStruct((1, D), F32)],
        scratch_shapes=[pltpu.VMEM((tm, D), F32), pltpu.VMEM((tm, D), BF16)],
        vmem_mib=56, args=(dx2, r2, lng, gs, us, wfi, wfi, wfo))


def _loss_head(y, target, *, name):
    T, D = y.shape
    tm = _tile(T, 1024)

    def body(y_ref, t_ref, dy_ref, sq_ref):
        @pl.when(pl.program_id(0) == 0)
        def _():
            sq_ref[...] = jnp.zeros_like(sq_ref)
        err = y_ref[...] - t_ref[...]
        dy_ref[...] = err * (1.0 / D)
        sq_ref[...] += jnp.sum(err * err, axis=0, keepdims=True)

    return pl.pallas_call(
        body, name=name, grid=(T // tm,),
        in_specs=[pl.BlockSpec((tm, D), lambda i: (i, 0)), pl.BlockSpec((tm, D), lambda i: (i, 0))],
        out_specs=[pl.BlockSpec((tm, D), lambda i: (i, 0)), pl.BlockSpec((1, D), lambda i: (0, 0))],
        out_shape=[jax.ShapeDtypeStruct((T, D), F32), jax.ShapeDtypeStruct((1, D), F32)],
        compiler_params=_cparams(("arbitrary",)),
    )(y, target)


def _my_place():
    return lax.axis_index("x"), lax.axis_index("y"), lax.axis_index("c")


def _peer(place, k):
    x, y, c = place
    return (1 - x if k & 4 else x, 1 - y if k & 2 else y, 1 - c if k & 1 else c)


def _logical(place):
    x, y, c = place
    return 4 * x + 2 * y + c


def _block_of(ref, mode, idx):
    if mode == "blk":
        return ref.at[idx]
    if mode == "col":
        size = ref.shape[2] // N_DEV
        return ref.at[:, :, pl.ds(pl.multiple_of(idx * size, size), size)]
    size = ref.shape[1] // N_DEV
    return ref.at[:, pl.ds(pl.multiple_of(idx * size, size), size), :]


def _full_shape(shard, mode):
    if mode == "blk":
        return (N_DEV,) + shard.shape
    if mode == "col":
        return shard.shape[:2] + (N_DEV * shard.shape[2],)
    return (shard.shape[0], N_DEV * shard.shape[1], shard.shape[2])


class _Exchange:
    def __init__(self, arrays, out_shape, build):
        self.arrays = list(arrays)
        self.out_shape = list(out_shape)
        self.build = build

    def scratch(self):
        n = len(self.arrays)
        return [pltpu.SemaphoreType.DMA((n * N_DEV,)), pltpu.SemaphoreType.DMA((n * N_DEV,)),
                pltpu.SemaphoreType.DMA((n,))]

    def start(self, ins, outs, sems):
        for cp in self.build(ins, outs, *sems):
            cp.start()

    def wait(self, ins, outs, sems):
        for cp in self.build(ins, outs, *sems):
            cp.wait()

    def run(self, name):
        n_in, n_out = len(self.arrays), len(self.out_shape)
        hbm = pl.BlockSpec(memory_space=pltpu.HBM)

        def body(*refs):
            ins, outs, sems = refs[:n_in], refs[n_in:n_in + n_out], refs[n_in + n_out:]
            self.start(ins, outs, sems)
            self.wait(ins, outs, sems)

        return pl.pallas_call(
            body, name=name, in_specs=[hbm] * n_in, out_specs=[hbm] * n_out,
            out_shape=self.out_shape, scratch_shapes=self.scratch(),
        )(*self.arrays)


def _copies_to_all(src_of, dst_of, n, send, recv, local):
    me = _my_place()
    copies = []
    for a in range(n):
        copies.append(pltpu.make_async_copy(src_of(a, _logical(me)), dst_of(a), local.at[a]))
        for k in range(1, N_DEV):
            peer = _peer(me, k)
            copies.append(pltpu.make_async_remote_copy(
                src_ref=src_of(a, _logical(peer)), dst_ref=dst_of(a),
                send_sem=send.at[a * N_DEV + k], recv_sem=recv.at[a * N_DEV + k],
                device_id=peer, device_id_type=MESH))
    return copies


def _gather_job(shards, modes):
    def build(ins, outs, send, recv, local):
        my_id = _logical(_my_place())
        return _copies_to_all(lambda a, dev: ins[a], lambda a: _block_of(outs[a], modes[a], my_id),
                              len(shards), send, recv, local)

    return _Exchange(shards, [jax.ShapeDtypeStruct(_full_shape(s, m), s.dtype) for s, m in zip(shards, modes)], build)


def _grad_block(ref, mode, idx):
    if mode == "blk":
        return ref.at[idx]
    if mode == "col":
        size = ref.shape[1] // N_DEV
        return ref.at[:, pl.ds(pl.multiple_of(idx * size, size), size)]
    size = ref.shape[0] // N_DEV
    return ref.at[pl.ds(pl.multiple_of(idx * size, size), size), :]


def _grad_shard_shape(g, mode):
    if mode == "blk":
        return g.shape[1:]
    if mode == "col":
        return (g.shape[0], g.shape[1] // N_DEV)
    return (g.shape[0] // N_DEV, g.shape[1])


def _grads_job(groups, modes):
    flat = [(g, w, l) for w, per_w in enumerate(groups) for l, g in enumerate(per_w)]

    def build(ins, outs, send, recv, local):
        my_id = _logical(_my_place())
        return _copies_to_all(lambda a, dev: _grad_block(ins[a], modes[flat[a][1]], dev),
                              lambda a: outs[flat[a][1]].at[my_id, flat[a][2]],
                              len(flat), send, recv, local)

    out_shape = [jax.ShapeDtypeStruct((N_DEV, len(per_w)) + _grad_shard_shape(per_w[0], m), per_w[0].dtype)
                 for per_w, m in zip(groups, modes)]
    return _Exchange([g for g, _, _ in flat], out_shape, build)


def _adamw(w, g, m, v):
    m = ADAM_B1 * m + (1.0 - ADAM_B1) * g
    v = ADAM_B2 * v + (1.0 - ADAM_B2) * (g * g)
    m_hat = m / (1.0 - ADAM_B1 ** ADAM_STEP)
    v_hat = v / (1.0 - ADAM_B2 ** ADAM_STEP)
    delta = -ADAM_LR * (m_hat / (jnp.sqrt(v_hat) + ADAM_EPS) + ADAM_WD * w)
    return delta, m, v


def _sum_slots_adamw(slots, w, m, v, *, name):
    R, C = w.shape
    tr = _tile(R, 256)

    def body(s_ref, w_ref, m_ref, v_ref, g_out, d_out, m_out, v_out):
        g = s_ref[0].astype(F32)
        for s in range(1, N_DEV):
            g = g + s_ref[s].astype(F32)
        delta, m_new, v_new = _adamw(w_ref[...], g, m_ref[...], v_ref[...])
        g_out[...] = g
        d_out[...] = delta
        m_out[...] = m_new
        v_out[...] = v_new

    spec = pl.BlockSpec((tr, C), lambda i: (i, 0))
    return pl.pallas_call(
        body, name=name, grid=(R // tr,),
        in_specs=[pl.BlockSpec((N_DEV, tr, C), lambda i: (0, i, 0)), spec, spec, spec],
        out_specs=[spec] * 4,
        out_shape=[jax.ShapeDtypeStruct((R, C), F32)] * 4,
        compiler_params=_cparams(("parallel",)),
    )(slots, w, m, v)


def _small_allreduce_adamw(g, w, m, v, *, name):
    R = g.shape[0]
    vmem = pl.BlockSpec(memory_space=pltpu.VMEM)

    def body(g_ref, w_ref, m_ref, v_ref, g_out, d_out, m_out, v_out, slots, send, recv):
        me = _my_place()
        my_id = _logical(me)
        slots[my_id] = g_ref[...]
        copies = []
        for k in range(1, N_DEV):
            cp = pltpu.make_async_remote_copy(
                src_ref=g_ref, dst_ref=slots.at[my_id], send_sem=send.at[k], recv_sem=recv.at[k],
                device_id=_peer(me, k), device_id_type=MESH)
            cp.start()
            copies.append(cp)
        for cp in copies:
            cp.wait()
        total = slots[0]
        for s in range(1, N_DEV):
            total = total + slots[s]
        delta, m_new, v_new = _adamw(w_ref[...], total, m_ref[...], v_ref[...])
        g_out[...] = total
        d_out[...] = delta
        m_out[...] = m_new
        v_out[...] = v_new

    return pl.pallas_call(
        body, name=name,
        in_specs=[vmem] * 4, out_specs=[vmem] * 4,
        out_shape=[jax.ShapeDtypeStruct((R, LANES), F32)] * 4,
        scratch_shapes=[pltpu.VMEM((N_DEV, R, LANES), F32),
                        pltpu.SemaphoreType.DMA((N_DEV,)), pltpu.SemaphoreType.DMA((N_DEV,))],
    )(g, w, m, v)


def _pack(parts):
    flat = jnp.concatenate([p.reshape(-1) for p in parts])
    rows = -(-flat.shape[0] // (8 * LANES)) * 8
    return jnp.pad(flat, (0, rows * LANES - flat.shape[0])).reshape(rows, LANES)


def _unpack(packed, like):
    flat = packed.reshape(-1)
    out, pos = [], 0
    for p in like:
        out.append(flat[pos:pos + p.size].reshape(p.shape))
        pos += p.size
    return out


def kernel(x, w_in, b_gate, rel_bias, w_proj_a, w_proj_b, w_out, ln1_g, ln1_b, w_ffn_in, w_ffn_out, ln2_g, ln2_b, loss_target, m_w_in, m_b_gate, m_rel_bias, m_w_proj_a, m_w_proj_b, m_w_out, m_ln1_g, m_ln1_b, m_w_ffn_in, m_w_ffn_out, m_ln2_g, m_ln2_b, v_w_in, v_b_gate, v_rel_bias, v_w_proj_a, v_w_proj_b, v_w_out, v_ln1_g, v_ln1_b, v_w_ffn_in, v_w_ffn_out, v_ln2_g, v_ln2_b):
    L = w_in.shape[0]
    T, D = x.shape[1], x.shape[2]
    alpha = float((2 * L) ** 0.25)
    n_qkv = 6 * WIDTH

    big = [w_in, w_proj_a, w_proj_b, w_out, w_ffn_in, w_ffn_out]
    kinds = ["in", "pa", "pb", "o", "fi", "fo"]
    modes = ["col", "col", "col", "row", "blk", "row"]
    mode_of = dict(zip(kinds, modes))
    w_bf = dict(zip(kinds, [w.astype(BF16) for w in big]))

    def gather_of(ks, l):
        return _gather_job([w_bf[k][l:l + 1] for k in ks], [mode_of[k] for k in ks])

    W = [dict() for _ in range(L)]
    (W[0]["in"],) = gather_of(["in"], 0).run("gather_w_in_first")
    vec3 = lambda a: a[:, None, :]
    bg3, l1g, l1b, l2g, l2b = vec3(b_gate), vec3(ln1_g), vec3(ln1_b), vec3(ln2_g), vec3(ln2_b)
    b_col0 = 3 * WIDTH // LANES

    h = x[0]
    saved = []
    for l in range(L):
        ahead = l + 1 < L
        soon = ["pa", "pb", "o", "fo"]
        (qkv, gates), got = _in_proj(h, W[l]["in"], 0, n_qkv=n_qkv, name=f"in_proj_{l}",
                                     job=gather_of(soon, 0) if l == 0 else None)
        W[l].update(zip(soon, got))
        kvpad = jnp.pad(qkv[:, WIDTH:3 * WIDTH], ((A_WIN - A_TQ, 0), (0, 0)))
        bias = _toeplitz_bias(rel_bias[l])
        oa, got = _attn_a_fwd(qkv, kvpad, bias, name=f"attn_a_fwd_{l}", job=gather_of(["fi"], 0) if l == 0 else None)
        W[l].update(zip(["fi"], got))
        early = ["in", "pa", "pb", "o"]
        ob, got = _attn_b_fwd(qkv, col0=b_col0, name=f"attn_b_fwd_{l}", job=gather_of(early, l + 1) if ahead else None)
        W[l + 1 if ahead else l].update(zip(early, got))
        x1, r1, x1b = _proj_fwd(oa, ob, gates, bg3, W[l]["pa"], W[l]["pb"], W[l]["o"], 0, h, l1g, l1b, l,
                           alpha=alpha, name=f"proj_fwd_{l}")
        (gs, us, r2, x2), got = _ffn_fwd(x1, W[l]["fi"], W[l]["fo"], 0, l2g, l2b, l, alpha=alpha, name=f"ffn_fwd_{l}",
                                         job=gather_of(["fi", "fo"], l + 1) if ahead else None)
        W[l + 1 if ahead else l].update(zip(["fi", "fo"], got))
        saved.append((h, qkv, gates, kvpad, bias, oa, ob, x1b, r1, gs, us, r2))
        h = x2

    d_h, sq = _loss_head(h, loss_target[0], name="loss_head")
    loss = lax.psum((0.5 / D) * jnp.sum(sq), ("x", "y", "c"))

    g_bg, g_rb, g_l1g, g_l1b, g_l2g, g_l2b = ([None] * L for _ in range(6))
    slot = {k: [None] * L for k in kinds}

    def exchange_of(ks, grads):
        return _grads_job([[g] for g in grads], [mode_of[k] for k in ks])

    w_in_above = None
    for l in reversed(range(L)):
        xin, qkv, gates, kvpad, bias, oa, ob, x1b, r1, gs, us, r2 = saved[l]
        (dr2, act, dgt, dup, dx1, g_l2g[l], g_l2b[l]), got = _ffn_bwd(
            d_h, r2, l2g, gs, us, W[l]["fi"], W[l]["fo"], 0, l, alpha=alpha, name=f"ffn_bwd_{l}", job=w_in_above)
        if w_in_above is not None:
            (slot["in"][l + 1],) = got
        g_fo = _mm_tn_blocked_a(act, dr2, name=f"grad_w_ffn_out_{l}").reshape(-1, D)
        g_fi = jnp.concatenate(_mm_tn_blocked_pair(x1b, dgt, dup, name=f"grad_w_ffn_in_{l}"), axis=0)
        (dr1, mixin, dya, dyb, dgates, doa, dob, g_l1g[l], g_l1b[l], g_bg[l]) = _proj_bwd(
            dx1, r1, l1g, oa, ob, gates, bg3, W[l]["pa"], W[l]["pb"], W[l]["o"], 0, l, name=f"proj_bwd_{l}")
        g_o = _mm_tn(mixin, dr1, tm=_tile(D, 1024), tn=_tile(D, 1024), name=f"grad_w_out_{l}")
        g_pa = _mm_tn(oa, dya, tm=WIDTH, tn=_tile(D, 1024), name=f"grad_w_proj_a_{l}")
        g_pb = _mm_tn(ob, dyb, tm=WIDTH, tn=_tile(D, 1024), name=f"grad_w_proj_b_{l}")
        (dqa, dka, dva, dbias), (slot["fi"][l], slot["fo"][l]) = _attn_a_bwd(
            qkv, kvpad, bias, doa, name=f"attn_a_bwd_{l}", job=exchange_of(["fi", "fo"], [g_fi, g_fo]))
        g_rb[l] = _toeplitz_bias_grad(dbias)
        (dqb, dkb, dvb), (slot["pa"][l], slot["pb"][l], slot["o"][l]) = _attn_b_bwd(
            qkv, ob, dob, col0=b_col0, name=f"attn_b_bwd_{l}", job=exchange_of(["pa", "pb", "o"], [g_pa, g_pb, g_o]))
        d_pre = jnp.concatenate([dqa, dka, dva, dqb, dkb, dvb, dgates], axis=1)
        g_in = _mm_tn(xin, d_pre, tm=D, tn=4 * w_in.shape[2], name=f"grad_w_in_{l}")
        w_in_above = exchange_of(["in"], [g_in])
        d_h, got = _mm_nt_add(d_pre, W[l]["in"], 0, dr1, alpha, name=f"grad_x_{l}", job=w_in_above if l == 0 else None)
        if l == 0:
            (slot["in"][0],) = got
    grad_x = d_h[None]

    slots = [jnp.concatenate(slot[k], axis=1) if L > 1 else slot[k][0] for k in kinds]
    moments_m = [m_w_in, m_w_proj_a, m_w_proj_b, m_w_out, m_w_ffn_in, m_w_ffn_out]
    moments_v = [v_w_in, v_w_proj_a, v_w_proj_b, v_w_out, v_w_ffn_in, v_w_ffn_out]
    names = ["w_in", "w_proj_a", "w_proj_b", "w_out", "w_ffn_in", "w_ffn_out"]
    big_out = {}
    for nm, s, w, m, v in zip(names, slots, big, moments_m, moments_v):
        two = lambda a: a.reshape(-1, a.shape[-1])
        res = _sum_slots_adamw(s.reshape(N_DEV, -1, s.shape[-1]), two(w), two(m), two(v), name=f"adamw_{nm}")
        big_out[nm] = [r.reshape(w.shape) for r in res]

    small_w = [b_gate, rel_bias, ln1_g, ln1_b, ln2_g, ln2_b]
    small_g = [jnp.stack(g) for g in (g_bg, g_rb, g_l1g, g_l1b, g_l2g, g_l2b)]
    small_m = [m_b_gate, m_rel_bias, m_ln1_g, m_ln1_b, m_ln2_g, m_ln2_b]
    small_v = [v_b_gate, v_rel_bias, v_ln1_g, v_ln1_b, v_ln2_g, v_ln2_b]
    res = _small_allreduce_adamw(_pack(small_g), _pack(small_w), _pack(small_m), _pack(small_v),
                                 name="allreduce_small_adamw")
    small_names = ["b_gate", "rel_bias", "ln1_g", "ln1_b", "ln2_g", "ln2_b"]
    small_out = {nm: [] for nm in small_names}
    for packed in res:
        for nm, arr in zip(small_names, _unpack(packed, small_w)):
            small_out[nm].append(arr)

    order = ["w_in", "b_gate", "rel_bias", "w_proj_a", "w_proj_b", "w_out", "ln1_g", "ln1_b",
             "w_ffn_in", "w_ffn_out", "ln2_g", "ln2_b"]
    every = {**big_out, **small_out}
    outs = [loss, grad_x]
    for kind in range(4):
        outs += [every[nm][kind] for nm in order]
    return tuple(outs)
```

```python
import functools
import math

import jax
import jax.numpy as jnp
import numpy as np
from jax import lax
from jax.experimental import pallas as pl
from jax.experimental.pallas import tpu as pltpu

F32 = jnp.float32
BF16 = jnp.bfloat16

HEAD_DIM = 64
CHUNK = 64
LEFT_CHUNKS = 8
REL_CLIP = 256
N_REL = 2 * REL_CLIP + 1
WIDTH = 512
LANES = 128
A_TQ = 256
A_WIN = A_TQ + LEFT_CHUNKS * CHUNK
A_STRIP = 128
B_TQ = 512
B_TS = 256
B_PIECE = 64
B_DEAD = -160.0
LN_EPS = 1e-5
QK_SCALE = 1.0 / math.sqrt(HEAD_DIM)
LOG2E = 1.4426950408889634
NEG = -1e30

ADAM_LR = 0.001
ADAM_B1 = 0.9
ADAM_B2 = 0.999
ADAM_EPS = 1e-08
ADAM_WD = 0.01
ADAM_STEP = 10

N_DEV = 8
MESH = pl.DeviceIdType.MESH
MIB = 1024 * 1024


def _cparams(sem=None, vmem_mib=48):
    return pltpu.CompilerParams(dimension_semantics=sem, vmem_limit_bytes=vmem_mib * MIB)


def _dot(a, b):
    return jnp.dot(a, b, preferred_element_type=F32)


def _dot_nt(a, b):
    return lax.dot_general(a, b, (((1,), (1,)), ((), ())), preferred_element_type=F32)


def _dot_tn(a, b):
    return lax.dot_general(a, b, (((0,), (0,)), ((), ())), preferred_element_type=F32)


def _tile(n, pref):
    if n <= pref:
        return n
    for t in range(pref - pref % 8, 0, -8):
        if n % t == 0:
            return t
    raise ValueError((n, pref))


def _in_proj(a, w, layer, *, n_qkv, name, job=None):
    M, K = a.shape
    N = w.shape[2]
    tm = _tile(M, 1024)
    tn = 1024
    assert n_qkv % tn == 0 and (N - n_qkv) % tn == 0
    n_q = n_qkv // tn

    def body(a_ref, w_ref, q_ref, g_ref, ab_ref):
        j = pl.program_id(1)

        @pl.when(j == 0)
        def _():
            ab_ref[...] = a_ref[...].astype(BF16)

        res = _dot(ab_ref[...], w_ref[...])

        @pl.when(j < n_q)
        def _():
            q_ref[...] = res.astype(BF16)

        @pl.when(j >= n_q)
        def _():
            g_ref[...] = res

    return _call_carrying(
        job, body, name=name, grid=(M // tm, N // tn),
        in_specs=[pl.BlockSpec((tm, K), lambda i, j: (i, 0)),
                  pl.BlockSpec((None, K, tn), lambda i, j: (layer, 0, j))],
        out_specs=[pl.BlockSpec((tm, tn), lambda i, j: (i, jnp.minimum(j, n_q - 1))),
                   pl.BlockSpec((tm, tn), lambda i, j: (i, jnp.maximum(j - n_q, 0)))],
        out_shape=[jax.ShapeDtypeStruct((M, n_qkv), BF16), jax.ShapeDtypeStruct((M, N - n_qkv), F32)],
        scratch_shapes=[pltpu.VMEM((tm, K), BF16)], vmem_mib=48, args=(a, w))


def _mm_nt_add(a, w, layer, add, add_scale, *, name, job=None):
    M, K = a.shape
    N = w.shape[1]
    tm = _tile(M, 1024)
    tk = _tile(K, 1024)

    def body(a_ref, w_ref, add_ref, o_ref):
        @pl.when(pl.program_id(1) == 0)
        def _():
            o_ref[...] = add_scale * add_ref[...]
        o_ref[...] += _dot_nt(a_ref[...], w_ref[...])

    (out,), rode = _call_carrying(
        job, body, name=name, grid=(M // tm, K // tk),
        in_specs=[pl.BlockSpec((tm, tk), lambda i, k: (i, k)),
                  pl.BlockSpec((None, N, tk), lambda i, k: (layer, 0, k)),
                  pl.BlockSpec((tm, N), lambda i, k: (i, 0))],
        out_specs=[pl.BlockSpec((tm, N), lambda i, k: (i, 0))],
        out_shape=[jax.ShapeDtypeStruct((M, N), F32)],
        scratch_shapes=[], vmem_mib=48, args=(a, w, add))
    return out, rode


def _tn_body(k_axis, n_k):
    def body(a_ref, b_ref, o_ref, acc_ref):
        k = pl.program_id(k_axis)

        @pl.when(k == 0)
        def _():
            acc_ref[...] = jnp.zeros_like(acc_ref)
        acc_ref[...] += _dot_tn(a_ref[...].astype(BF16), b_ref[...].astype(BF16))

        @pl.when(k == n_k - 1)
        def _():
            o_ref[...] = acc_ref[...].astype(o_ref.dtype)
    return body


def _mm_tn(a, b, *, tm, tn, name):
    T, M = a.shape
    N = b.shape[1]
    tk = _tile(T, 512)
    return pl.pallas_call(
        _tn_body(2, T // tk), name=name, grid=(M // tm, N // tn, T // tk),
        in_specs=[pl.BlockSpec((tk, tm), lambda i, j, k: (k, i)),
                  pl.BlockSpec((tk, tn), lambda i, j, k: (k, j))],
        out_specs=pl.BlockSpec((tm, tn), lambda i, j, k: (i, j)),
        out_shape=jax.ShapeDtypeStruct((M, N), BF16),
        scratch_shapes=[pltpu.VMEM((tm, tn), F32)],
        compiler_params=_cparams(("parallel", "parallel", "arbitrary")),
    )(a, b)


def _mm_tn_blocked_pair(a, b1, b2, *, name):
    T, M = a.shape
    S, _, N = b1.shape
    tk = _tile(T, 1024)
    n_k = T // tk

    def body(a_ref, b1_ref, b2_ref, o1_ref, o2_ref, acc1_ref, acc2_ref):
        k = pl.program_id(1)

        @pl.when(k == 0)
        def _():
            acc1_ref[...] = jnp.zeros_like(acc1_ref)
            acc2_ref[...] = jnp.zeros_like(acc2_ref)

        a_t = a_ref[...].astype(BF16)
        acc1_ref[...] += _dot_tn(a_t, b1_ref[...])
        acc2_ref[...] += _dot_tn(a_t, b2_ref[...])

        @pl.when(k == n_k - 1)
        def _():
            o1_ref[...] = acc1_ref[...].astype(o1_ref.dtype)
            o2_ref[...] = acc2_ref[...].astype(o2_ref.dtype)

    blk = lambda: pl.BlockSpec((None, tk, N), lambda s, k: (s, k, 0))
    out = lambda: pl.BlockSpec((None, M, N), lambda s, k: (s, 0, 0))
    return pl.pallas_call(
        body, name=name, grid=(S, n_k),
        in_specs=[pl.BlockSpec((tk, M), lambda s, k: (k, 0)), blk(), blk()],
        out_specs=[out(), out()],
        out_shape=[jax.ShapeDtypeStruct((S, M, N), BF16)] * 2,
        scratch_shapes=[pltpu.VMEM((M, N), F32), pltpu.VMEM((M, N), F32)],
        compiler_params=_cparams(("parallel", "arbitrary")),
    )(a, b1, b2)


def _mm_tn_blocked_a(a, b, *, name):
    S, T, M = a.shape
    N = b.shape[1]
    tk = _tile(T, 512)
    return pl.pallas_call(
        _tn_body(1, T // tk), name=name, grid=(S, T // tk),
        in_specs=[pl.BlockSpec((None, tk, M), lambda s, k: (s, k, 0)),
                  pl.BlockSpec((tk, N), lambda s, k: (k, 0))],
        out_specs=pl.BlockSpec((None, M, N), lambda s, k: (s, 0, 0)),
        out_shape=jax.ShapeDtypeStruct((S, M, N), BF16),
        scratch_shapes=[pltpu.VMEM((M, N), F32)],
        compiler_params=_cparams(("parallel", "arbitrary")),
    )(a, b)


def _ln_fwd(r, g, b):
    mu = jnp.mean(r, axis=-1, keepdims=True)
    xc = r - mu
    var = jnp.mean(xc * xc, axis=-1, keepdims=True)
    return xc * lax.rsqrt(var + LN_EPS) * g + b


def _ln_bwd(dy, r, g):
    mu = jnp.mean(r, axis=-1, keepdims=True)
    xc = r - mu
    var = jnp.mean(xc * xc, axis=-1, keepdims=True)
    rstd = lax.rsqrt(var + LN_EPS)
    xhat = xc * rstd
    dxh = dy * g
    m1 = jnp.mean(dxh, axis=-1, keepdims=True)
    m2 = jnp.mean(dxh * xhat, axis=-1, keepdims=True)
    return rstd * (dxh - m1 - xhat * m2), xhat


def _lane_is_head0():
    return lax.broadcasted_iota(jnp.int32, (1, LANES), 1) < HEAD_DIM


def _band_shape():
    a = np.arange(A_TQ)[:, None] // CHUNK
    b = np.arange(A_WIN)[None, :] // CHUNK
    return (b >= a) & (b <= a + LEFT_CHUNKS)


def _band_streams(strip):
    return [dict(h=h, n=strip, rows=pl.ds(r0, strip)) for h in range(2) for r0 in range(0, A_TQ, strip)]


def _band_scores(st, i, qh_ref, k2, bias_ref):
    s = _dot_nt(qh_ref[st["h"], st["rows"], :], k2) * QK_SCALE + bias_ref[st["h"], st["rows"], :]
    c = lax.broadcasted_iota(jnp.int32, (st["n"], A_WIN), 1)
    st["s"] = jnp.where(c >= LEFT_CHUNKS * CHUNK - i * A_TQ, s, NEG)


def _band_softmax(st):
    s = st.pop("s")
    e = jnp.exp(s - jnp.max(s, axis=1, keepdims=True))
    st["p"] = e / jnp.sum(e, axis=1, keepdims=True)


def _attn_a_fwd(qkv, kvpad, bias, *, name, job=None):
    T = qkv.shape[0]
    n_hp = WIDTH // LANES

    def body(q_ref, k_ref, v_ref, bias_ref, o_ref, qh_ref, acc_ref):
        i = pl.program_id(1)
        row0 = pl.multiple_of(i * A_TQ, A_TQ)
        q2 = q_ref[...]
        k2 = k_ref[pl.ds(row0, A_WIN), :]
        v2 = v_ref[pl.ds(row0, A_WIN), :]
        head0 = _lane_is_head0()
        qh_ref[0] = jnp.where(head0, q2, jnp.zeros_like(q2))
        qh_ref[1] = jnp.where(head0, jnp.zeros_like(q2), q2)

        def scores(st):
            _band_scores(st, i, qh_ref, k2, bias_ref)

        def values(st):
            acc_ref[st["h"], st["rows"], :] = _dot(st.pop("p").astype(BF16), v2)

        _skewed(_band_streams(A_STRIP), [scores, _band_softmax, values])
        o_ref[...] = jnp.where(head0, acc_ref[0], acc_ref[1]).astype(o_ref.dtype)

    (out,), rode = _call_carrying(
        job, body, name=name, grid=(n_hp, T // A_TQ),
        in_specs=[pl.BlockSpec((A_TQ, LANES), lambda hp, i: (i, hp)),
                  pl.BlockSpec((T + A_WIN - A_TQ, LANES), lambda hp, i: (0, hp)),
                  pl.BlockSpec((T + A_WIN - A_TQ, LANES), lambda hp, i: (0, hp + n_hp)),
                  pl.BlockSpec((2, A_TQ, A_WIN), lambda hp, i: (hp, 0, 0))],
        out_specs=[pl.BlockSpec((A_TQ, LANES), lambda hp, i: (i, hp))],
        out_shape=[jax.ShapeDtypeStruct((T, WIDTH), BF16)],
        scratch_shapes=[pltpu.VMEM((2, A_TQ, LANES), BF16), pltpu.VMEM((2, A_TQ, LANES), F32)],
        vmem_mib=48, args=(qkv, kvpad, kvpad, bias))
    return out, rode


def _attn_a_bwd(qkv, kvpad, bias, do, *, name, job=None):
    T = qkv.shape[0]
    TP = T + A_WIN - A_TQ
    n_hp = WIDTH // LANES

    def body(q_ref, k_ref, v_ref, bias_ref, do_ref, dq_ref, dko_ref, dvo_ref, db_ref,
             qh_ref, doh_ref, dqa_ref, dk_ref, dv_ref):
        i = pl.program_id(1)

        @pl.when(i == 0)
        def _():
            dk_ref[...] = jnp.zeros_like(dk_ref)
            dv_ref[...] = jnp.zeros_like(dv_ref)
            db_ref[...] = jnp.zeros_like(db_ref)

        row0 = pl.multiple_of(i * A_TQ, A_TQ)
        window = pl.ds(row0, A_WIN)
        q2 = q_ref[...]
        do2 = do_ref[...]
        k2 = k_ref[window, :]
        v2 = v_ref[window, :]
        head0 = _lane_is_head0()
        zero = jnp.zeros_like(q2)
        qh_ref[0] = jnp.where(head0, q2, zero)
        qh_ref[1] = jnp.where(head0, zero, q2)
        doh_ref[0] = jnp.where(head0, do2, zero)
        doh_ref[1] = jnp.where(head0, zero, do2)
        dk = [[], []]
        dv = [[], []]

        def scores(st):
            _band_scores(st, i, qh_ref, k2, bias_ref)
            st["dp"] = _dot_nt(doh_ref[st["h"], st["rows"], :], v2)

        def dscores(st):
            _band_softmax(st)
            p, dp = st.pop("p"), st.pop("dp")
            ds = p * (dp - jnp.sum(p * dp, axis=1, keepdims=True))
            db_ref[st["h"], st["rows"], :] += ds
            st["dsb"] = (ds * QK_SCALE).astype(BF16)
            st["pb"] = p.astype(BF16)

        def grads(st):
            h, rows = st["h"], st["rows"]
            dsb = st.pop("dsb")
            dqa_ref[h, rows, :] = _dot(dsb, k2)
            dk[h].append(_dot_tn(dsb, q_ref[rows, :]))
            dv[h].append(_dot_tn(st.pop("pb"), do_ref[rows, :]))

        _skewed(_band_streams(A_TQ), [scores, dscores, grads])
        dq_ref[...] = jnp.where(head0, dqa_ref[0], dqa_ref[1]).astype(dq_ref.dtype)
        dk_ref[window, :] += jnp.where(head0, sum(dk[0]), sum(dk[1]))
        dv_ref[window, :] += jnp.where(head0, sum(dv[0]), sum(dv[1]))

        @pl.when(i == T // A_TQ - 1)
        def _():
            dko_ref[...] = dk_ref[TP - T:, :].astype(BF16)
            dvo_ref[...] = dv_ref[TP - T:, :].astype(BF16)

    return _call_carrying(
        job, body, name=name, grid=(n_hp, T // A_TQ),
        in_specs=[pl.BlockSpec((A_TQ, LANES), lambda hp, i: (i, hp)),
                  pl.BlockSpec((TP, LANES), lambda hp, i: (0, hp)),
                  pl.BlockSpec((TP, LANES), lambda hp, i: (0, hp + n_hp)),
                  pl.BlockSpec((2, A_TQ, A_WIN), lambda hp, i: (hp, 0, 0)),
                  pl.BlockSpec((A_TQ, LANES), lambda hp, i: (i, hp))],
        out_specs=[pl.BlockSpec((A_TQ, LANES), lambda hp, i: (i, hp)),
                   pl.BlockSpec((T, LANES), lambda hp, i: (0, hp)),
                   pl.BlockSpec((T, LANES), lambda hp, i: (0, hp)),
                   pl.BlockSpec((2, A_TQ, A_WIN), lambda hp, i: (hp, 0, 0))],
        out_shape=[jax.ShapeDtypeStruct((T, WIDTH), BF16),
                   jax.ShapeDtypeStruct((T, WIDTH), BF16),
                   jax.ShapeDtypeStruct((T, WIDTH), BF16),
                   jax.ShapeDtypeStruct((WIDTH // HEAD_DIM, A_TQ, A_WIN), F32)],
        scratch_shapes=[pltpu.VMEM((2, A_TQ, LANES), BF16), pltpu.VMEM((2, A_TQ, LANES), BF16),
                        pltpu.VMEM((2, A_TQ, LANES), F32),
                        pltpu.VMEM((TP, LANES), F32), pltpu.VMEM((TP, LANES), F32)],
        vmem_mib=56, args=(qkv, kvpad, kvpad, bias, do))


_N_DIAG = 2 * CHUNK - 1
_EXT_TOP = LEFT_CHUNKS * CHUNK + CHUNK - 1 + REL_CLIP


def _toeplitz_bias(rb):
    H = rb.shape[0]
    ext = jnp.concatenate([rb, jnp.broadcast_to(rb[:, N_REL - 1:], (H, _EXT_TOP + 1 - N_REL))], axis=1)
    vec = jnp.stack([ext[:, _EXT_TOP - (_N_DIAG - 1) - CHUNK * k:_EXT_TOP - CHUNK * k + 1]
                     for k in range(LEFT_CHUNKS + 1)], axis=1)
    rev = jnp.pad(vec[:, :, ::-1], ((0, 0), (0, 0), (0, 1)))
    flat = jnp.broadcast_to(rev[:, :, None, :], (H, LEFT_CHUNKS + 1, CHUNK, _N_DIAG + 1))
    skew = flat.reshape(H, LEFT_CHUNKS + 1, -1)[:, :, :CHUNK * _N_DIAG].reshape(H, LEFT_CHUNKS + 1, CHUNK, _N_DIAG)
    blocks = skew[:, :, :, CHUNK - 1:]
    neg = jnp.full((H, CHUNK, CHUNK), NEG, F32)
    rows = [jnp.concatenate([blocks[:, b - a] if 0 <= b - a <= LEFT_CHUNKS else neg for b in range(A_WIN // CHUNK)],
                            axis=2) for a in range(A_TQ // CHUNK)]
    return jnp.concatenate(rows, axis=1)


def _toeplitz_bias_grad(db):
    H = db.shape[0]
    d5 = db.reshape(H, A_TQ // CHUNK, CHUNK, A_WIN // CHUNK, CHUNK)
    g_blocks = jnp.stack([sum(d5[:, a, :, a + k, :] for a in range(A_TQ // CHUNK))
                          for k in range(LEFT_CHUNKS + 1)], axis=1)
    d_skew = jnp.pad(g_blocks, ((0, 0), (0, 0), (0, 0), (CHUNK - 1, 0)))
    d_flat = jnp.pad(d_skew.reshape(H, LEFT_CHUNKS + 1, CHUNK * _N_DIAG), ((0, 0), (0, 0), (0, CHUNK)))
    g_vec = jnp.sum(d_flat.reshape(H, LEFT_CHUNKS + 1, CHUNK, _N_DIAG + 1), axis=2)[:, :, :_N_DIAG][:, :, ::-1]
    g_ext = sum(jnp.pad(g_vec[:, k], ((0, 0), (_EXT_TOP - (_N_DIAG - 1) - CHUNK * k, CHUNK * k)))
                for k in range(LEFT_CHUNKS + 1))
    return jnp.concatenate([g_ext[:, :N_REL - 1], jnp.sum(g_ext[:, N_REL - 1:], axis=1, keepdims=True)], axis=1)


def _split_bf16(x):
    hi = x.astype(BF16)
    lo = (x - hi.astype(F32)).astype(BF16)
    return hi, lo


def _sb_streams(d, strips=None, **tile):
    out = []
    for h in range(2):
        for r in (range(B_TQ // B_TS) if strips is None else strips):
            if d is not None and d > r:
                continue
            out.append(dict(h=h, r=r, rows=pl.ds(r * B_TS, B_TS), diag=(d is not None and d == r), **tile))
    return out


def _sb_sweep(i, car_ref, streams_of, run):
    sub = B_TQ // B_TS
    run([st for d in reversed(range(sub)) for st in streams_of(i * sub + d, d, None)])

    def alive(c):
        return (c[0] < i * sub) & (c[1] > B_DEAD)

    def step(c):
        kb = i * sub - 1 - c[0]
        if sub > 1:
            lower_alive = jnp.max(car_ref[:, B_TS:, :]) > B_DEAD
            lax.cond(lower_alive, lambda: run(streams_of(kb, None, None)), lambda: run(streams_of(kb, None, [0])))
        else:
            run(streams_of(kb, None, None))
        return c[0] + 1, jnp.max(car_ref[...])

    lax.while_loop(alive, step, (jnp.int32(0), jnp.float32(0.0)))


def _piece_rows(st, p):
    return pl.ds(st["r"] * B_TS + p, B_PIECE)


def _rows_cat(parts):
    return jnp.concatenate(parts, axis=0)


def _skewed(streams, stages):
    for t in range(len(streams) + len(stages) - 1):
        for s, st in enumerate(streams):
            if 0 <= t - s < len(stages):
                stages[t - s](st)


def _sb_logs(st, z2):
    log_beta, log_keep, keep_bf = [], [], []
    for p in range(0, B_TS, B_PIECE):
        z = z2[p:p + B_PIECE]
        lp2 = jnp.log(1.0 + jnp.exp2(-jnp.abs(z))) * LOG2E
        lb = jnp.minimum(z, 0.0) - lp2
        lk = lb - z
        if st["diag"]:
            lk = jnp.where(_strict_lower(p), lk, 0.0)
        log_beta.append(lb)
        log_keep.append(lk)
        keep_bf.append(lk.astype(BF16))
    st["log_beta"] = _rows_cat(log_beta)
    st["log_keep"] = _rows_cat(log_keep)
    st["keep_bf"] = _rows_cat(keep_bf)


def _strict_lower(p):
    t = p + lax.broadcasted_iota(jnp.int32, (B_PIECE, B_TS), 0)
    s = lax.broadcasted_iota(jnp.int32, (B_PIECE, B_TS), 1)
    return s < t


def _tri(strict):
    j = lax.broadcasted_iota(jnp.int32, (B_TS, B_TS), 0)
    s = lax.broadcasted_iota(jnp.int32, (B_TS, B_TS), 1)
    return jnp.where(j > s if strict else j >= s, 1.0, 0.0).astype(BF16)


def _call_carrying(job, body, *, name, grid, in_specs, out_specs, out_shape, scratch_shapes, vmem_mib, args):
    n_in, n_out, n_scr = len(in_specs), len(out_specs), len(scratch_shapes)
    if job is None:
        res = pl.pallas_call(body, name=name, grid=grid, in_specs=in_specs, out_specs=out_specs,
                             out_shape=out_shape, scratch_shapes=scratch_shapes,
                             compiler_params=_cparams(("arbitrary",) * len(grid), vmem_mib))(*args)
        return res, []
    j_in, j_out = len(job.arrays), len(job.out_shape)
    hbm = pl.BlockSpec(memory_space=pltpu.HBM)

    def carrying(*refs):
        refs = list(refs)
        ins, refs = refs[:n_in], refs[n_in:]
        j_ins, refs = refs[:j_in], refs[j_in:]
        outs, refs = refs[:n_out], refs[n_out:]
        j_outs, refs = refs[:j_out], refs[j_out:]
        scr, sems = refs[:n_scr], refs[n_scr:]
        first = functools.reduce(jnp.logical_and, [pl.program_id(d) == 0 for d in range(len(grid))])
        last = functools.reduce(jnp.logical_and, [pl.program_id(d) == grid[d] - 1 for d in range(len(grid))])

        @pl.when(first)
        def _():
            job.start(j_ins, j_outs, sems)

        body(*ins, *outs, *scr)

        @pl.when(last)
        def _():
            job.wait(j_ins, j_outs, sems)

    res = pl.pallas_call(
        carrying, name=name, grid=grid,
        in_specs=list(in_specs) + [hbm] * j_in, out_specs=list(out_specs) + [hbm] * j_out,
        out_shape=list(out_shape) + job.out_shape, scratch_shapes=list(scratch_shapes) + job.scratch(),
        compiler_params=_cparams(("arbitrary",) * len(grid), vmem_mib))(*args, *job.arrays)
    return res[:n_out], res[n_out:]


def _attn_b_fwd(qkv, *, col0, name, job=None):
    T = qkv.shape[0]
    n_hp = WIDTH // LANES
    sub = B_TQ // B_TS

    def body(q_ref, k_ref, v_ref, o_ref, acc_ref, car_ref, qh_ref):
        i = pl.program_id(1)
        q2 = q_ref[...]
        head0 = _lane_is_head0()
        qh_ref[0] = jnp.where(head0, q2, jnp.zeros_like(q2))
        qh_ref[1] = jnp.where(head0, jnp.zeros_like(q2), q2)
        tri_s = _tri(True)
        acc_ref[...] = jnp.zeros_like(acc_ref)
        car_ref[...] = jnp.zeros_like(car_ref)

        def streams_of(kb, d, strips):
            keys = pl.ds(pl.multiple_of(kb * B_TS, B_TS), B_TS)
            return _sb_streams(d, strips, k2=k_ref[keys, :], v2=v_ref[keys, :])

        def scores(st):
            st["z2"] = _dot_nt(qh_ref[st["h"], st["rows"], :], st.pop("k2")) * (QK_SCALE * LOG2E)

        def logs(st):
            _sb_logs(st, st.pop("z2"))

        def suffix(st):
            st["suffix"] = _dot(st.pop("keep_bf"), tri_s)

        def weights(st):
            log_beta, suffix, log_keep = st.pop("log_beta"), st.pop("suffix"), st.pop("log_keep")
            wb = []
            for p in range(0, B_TS, B_PIECE):
                rows = _piece_rows(st, p)
                car = car_ref[st["h"], rows, :]
                w = jnp.exp2(log_beta[p:p + B_PIECE] + suffix[p:p + B_PIECE] + car)
                if st["diag"]:
                    w = jnp.where(_strict_lower(p), w, 0.0)
                wb.append(w.astype(BF16))
                car_ref[st["h"], rows, :] = car + jnp.sum(log_keep[p:p + B_PIECE], axis=1, keepdims=True)
            st["wb"] = _rows_cat(wb)

        def values(st):
            acc_ref[st["h"], st["rows"], :] += _dot(st.pop("wb"), st.pop("v2"))

        _sb_sweep(i, car_ref, streams_of, lambda sts: _skewed(sts, [scores, logs, suffix, weights, values]))
        o_ref[...] = jnp.where(head0, acc_ref[0], acc_ref[1])

    (out,), rode = _call_carrying(
        job, body, name=name, grid=(n_hp, T // B_TQ),
        in_specs=[pl.BlockSpec((B_TQ, LANES), lambda hp, i: (i, hp + col0)),
                  pl.BlockSpec((T, LANES), lambda hp, i: (0, hp + col0 + n_hp)),
                  pl.BlockSpec((T, LANES), lambda hp, i: (0, hp + col0 + 2 * n_hp))],
        out_specs=[pl.BlockSpec((B_TQ, LANES), lambda hp, i: (i, hp))],
        out_shape=[jax.ShapeDtypeStruct((T, WIDTH), F32)],
        scratch_shapes=[pltpu.VMEM((2, B_TQ, LANES), F32), pltpu.VMEM((2, B_TQ, 1), F32),
                        pltpu.VMEM((2, B_TQ, LANES), BF16)],
        vmem_mib=48, args=(qkv, qkv, qkv))
    return out, rode


def _attn_b_bwd(qkv, out, do, *, col0, name, job=None):
    T = qkv.shape[0]
    n_hp = WIDTH // LANES
    sub = B_TQ // B_TS

    def body(q_ref, k_ref, v_ref, o_ref, do_ref, dq_ref, dko_ref, dvo_ref,
             dqa_ref, car_ref, carr_ref, tot_ref, qh_ref, doh_ref, qs_ref, dk_ref, dv_ref):
        i = pl.program_id(1)

        @pl.when(i == 0)
        def _():
            dk_ref[...] = jnp.zeros_like(dk_ref)
            dv_ref[...] = jnp.zeros_like(dv_ref)

        q2 = q_ref[...]
        do2 = do_ref[...]
        head0 = _lane_is_head0()
        zero = jnp.zeros_like(q2)
        qh_ref[0] = jnp.where(head0, q2, zero)
        qh_ref[1] = jnp.where(head0, zero, q2)
        doh_ref[0] = jnp.where(head0, do2, zero)
        doh_ref[1] = jnp.where(head0, zero, do2)
        scale = jnp.asarray(QK_SCALE, BF16)
        qs_ref[...] = q2 * scale
        tri_s = _tri(True)
        tri_i = _tri(False)
        prod = do2.astype(F32) * o_ref[...]
        tot_ref[0] = jnp.sum(jnp.where(head0, prod, 0.0), axis=1, keepdims=True)
        tot_ref[1] = jnp.sum(jnp.where(head0, 0.0, prod), axis=1, keepdims=True)
        dqa_ref[...] = jnp.zeros_like(dqa_ref)
        car_ref[...] = jnp.zeros_like(car_ref)
        carr_ref[...] = jnp.zeros_like(carr_ref)

        def streams_of(kb, d, strips):
            keys = pl.ds(pl.multiple_of(kb * B_TS, B_TS), B_TS)
            k2 = k_ref[keys, :]
            return _sb_streams(d, strips, keys=keys, k2=k2, v2=v_ref[keys, :], k2s=k2 * scale)

        def run(streams):
            def scores(st):
                st["z2"] = _dot_nt(qh_ref[st["h"], st["rows"], :], st.pop("k2")) * (QK_SCALE * LOG2E)
                st["dw"] = _dot_nt(doh_ref[st["h"], st["rows"], :], st.pop("v2"))

            def logs(st):
                _sb_logs(st, st.pop("z2"))

            def suffix(st):
                st["suffix"] = _dot(st.pop("keep_bf"), tri_s)

            def weights(st):
                h = st["h"]
                suffix, dw = st.pop("suffix"), st.pop("dw")
                wb, dlog, hi, lo = [], [], [], []
                for p in range(0, B_TS, B_PIECE):
                    rows = _piece_rows(st, p)
                    car = car_ref[h, rows, :]
                    w = jnp.exp2(st["log_beta"][p:p + B_PIECE] + suffix[p:p + B_PIECE] + car)
                    if st["diag"]:
                        w = jnp.where(_strict_lower(p), w, 0.0)
                    w = w.astype(BF16)
                    dl = w.astype(F32) * dw[p:p + B_PIECE]
                    dl_hi, dl_lo = _split_bf16(dl)
                    wb.append(w)
                    dlog.append(dl)
                    hi.append(dl_hi)
                    lo.append(dl_lo)
                    car_ref[h, rows, :] = car + jnp.sum(st["log_keep"][p:p + B_PIECE], axis=1, keepdims=True)
                st["wb"], st["dlog"], st["hi"], st["lo"] = _rows_cat(wb), _rows_cat(dlog), _rows_cat(hi), _rows_cat(lo)

            def later(st):
                st["later"] = _dot(st.pop("hi"), tri_i) + _dot(st.pop("lo"), tri_i)

            def dscores(st):
                h = st["h"]
                later, dlog = st.pop("later"), st.pop("dlog")
                log_keep, log_beta = st.pop("log_keep"), st.pop("log_beta")
                dzb = []
                for p in range(0, B_TS, B_PIECE):
                    rows = _piece_rows(st, p)
                    pc = slice(p, p + B_PIECE)
                    carr = carr_ref[h, rows, :]
                    earlier = tot_ref[h, rows, :] - (later[pc] + carr)
                    dz = dlog[pc] * jnp.exp2(log_keep[pc]) - jnp.exp2(log_beta[pc]) * earlier
                    if st["diag"]:
                        dz = jnp.where(_strict_lower(p), dz, 0.0)
                    dzb.append(dz.astype(BF16))
                    carr_ref[h, rows, :] = carr + jnp.sum(dlog[pc], axis=1, keepdims=True)
                st["dzb"] = _rows_cat(dzb)

            def grads(st):
                h, rows, keys = st["h"], st["rows"], st["keys"]
                mine = head0 if h == 0 else jnp.logical_not(head0)
                dzb = st.pop("dzb")
                dqa_ref[h, rows, :] += _dot(dzb, st.pop("k2s"))
                dk_ref[keys, :] += jnp.where(mine, _dot_tn(dzb, qs_ref[rows, :]), 0.0)
                dv_ref[keys, :] += jnp.where(mine, _dot_tn(st.pop("wb"), do_ref[rows, :]), 0.0)

            _skewed(streams, [scores, logs, suffix, weights, later, dscores, grads])

        _sb_sweep(i, car_ref, streams_of, run)
        dq_ref[...] = jnp.where(head0, dqa_ref[0], dqa_ref[1]).astype(dq_ref.dtype)

        @pl.when(i == T // B_TQ - 1)
        def _():
            dko_ref[...] = dk_ref[...].astype(BF16)
            dvo_ref[...] = dv_ref[...].astype(BF16)

    return _call_carrying(
        job, body, name=name, grid=(n_hp, T // B_TQ),
        in_specs=[pl.BlockSpec((B_TQ, LANES), lambda hp, i: (i, hp + col0)),
                  pl.BlockSpec((T, LANES), lambda hp, i: (0, hp + col0 + n_hp)),
                  pl.BlockSpec((T, LANES), lambda hp, i: (0, hp + col0 + 2 * n_hp)),
                  pl.BlockSpec((B_TQ, LANES), lambda hp, i: (i, hp)),
                  pl.BlockSpec((B_TQ, LANES), lambda hp, i: (i, hp))],
        out_specs=[pl.BlockSpec((B_TQ, LANES), lambda hp, i: (i, hp)),
                   pl.BlockSpec((T, LANES), lambda hp, i: (0, hp)),
                   pl.BlockSpec((T, LANES), lambda hp, i: (0, hp))],
        out_shape=[jax.ShapeDtypeStruct((T, WIDTH), BF16),
                   jax.ShapeDtypeStruct((T, WIDTH), BF16),
                   jax.ShapeDtypeStruct((T, WIDTH), BF16)],
        scratch_shapes=[pltpu.VMEM((2, B_TQ, LANES), F32), pltpu.VMEM((2, B_TQ, 1), F32),
                        pltpu.VMEM((2, B_TQ, 1), F32), pltpu.VMEM((2, B_TQ, 1), F32),
                        pltpu.VMEM((2, B_TQ, LANES), BF16), pltpu.VMEM((2, B_TQ, LANES), BF16),
                        pltpu.VMEM((B_TQ, LANES), BF16),
                        pltpu.VMEM((T, LANES), F32), pltpu.VMEM((T, LANES), F32)],
        vmem_mib=56, args=(qkv, qkv, qkv, out, do))


def _gated_mix(oa_ref, ob_ref, g_ref, bg_ref, wpa_ref, wpb_ref, D):
    ya = _dot(oa_ref[...].astype(BF16), wpa_ref[...])
    yb = _dot(ob_ref[...].astype(BF16), wpb_ref[...])
    sa = jax.nn.sigmoid(g_ref[:, :D] + bg_ref[:, :D])
    sb = jax.nn.sigmoid(g_ref[:, D:] + bg_ref[:, D:])
    return ya, yb, sa, sb


def _proj_fwd(oa, ob, g, bg, wpa, wpb, wo, wl, xin, lng, lnb, layer, *, alpha, name):
    T, D = xin.shape
    tm = _tile(T, 512)
    row = lambda i: (i, 0)
    wspec = lambda r, c: pl.BlockSpec((None, r, c), lambda i: (wl, 0, 0))
    vec = lambda c: pl.BlockSpec((None, 1, c), lambda i: (layer, 0, 0))

    def body(oa_ref, ob_ref, g_ref, bg_ref, wpa_ref, wpb_ref, wo_ref, x_ref, lg_ref, lb_ref, x1_ref, r1_ref, x1b_ref):
        ya, yb, sa, sb = _gated_mix(oa_ref, ob_ref, g_ref, bg_ref, wpa_ref, wpb_ref, D)
        mix = _dot((sa * ya + sb * yb).astype(BF16), wo_ref[...])
        r1 = alpha * x_ref[...] + mix
        r1_ref[...] = r1
        x1 = _ln_fwd(r1, lg_ref[...], lb_ref[...])
        x1_ref[...] = x1
        x1b_ref[...] = x1.astype(BF16)

    return pl.pallas_call(
        body, name=name, grid=(T // tm,),
        in_specs=[pl.BlockSpec((tm, WIDTH), row), pl.BlockSpec((tm, WIDTH), row), pl.BlockSpec((tm, 2 * D), row),
                  vec(2 * D), wspec(WIDTH, D), wspec(WIDTH, D), wspec(D, D),
                  pl.BlockSpec((tm, D), row), vec(D), vec(D)],
        out_specs=[pl.BlockSpec((tm, D), row), pl.BlockSpec((tm, D), row), pl.BlockSpec((tm, D), row)],
        out_shape=[jax.ShapeDtypeStruct((T, D), F32), jax.ShapeDtypeStruct((T, D), F32),
                   jax.ShapeDtypeStruct((T, D), BF16)],
        compiler_params=_cparams(("arbitrary",), 56),
    )(oa, ob, g, bg, wpa, wpb, wo, xin, lng, lnb)


def _proj_bwd(dx1, r1, lng, oa, ob, g, bg, wpa, wpb, wo, wl, layer, *, name):
    T, D = dx1.shape
    tm = _tile(T, 512)
    row = lambda i: (i, 0)
    fixed = lambda i: (0, 0)
    wspec = lambda r, c: pl.BlockSpec((None, r, c), lambda i: (wl, 0, 0))
    vec = lambda c: pl.BlockSpec((None, 1, c), lambda i: (layer, 0, 0))

    def body(dx_ref, r1_ref, lg_ref, oa_ref, ob_ref, g_ref, bg_ref, wpa_ref, wpb_ref, wo_ref,
             dr_ref, mix_ref, dya_ref, dyb_ref, dg_ref, doa_ref, dob_ref, dlg_ref, dlb_ref, dbg_ref):
        @pl.when(pl.program_id(0) == 0)
        def _():
            dlg_ref[...] = jnp.zeros_like(dlg_ref)
            dlb_ref[...] = jnp.zeros_like(dlb_ref)
            dbg_ref[...] = jnp.zeros_like(dbg_ref)

        dx = dx_ref[...]
        dr, xhat = _ln_bwd(dx, r1_ref[...], lg_ref[...])
        dr_ref[...] = dr
        dlg_ref[...] += jnp.sum(dx * xhat, axis=0, keepdims=True)
        dlb_ref[...] += jnp.sum(dx, axis=0, keepdims=True)
        dmix = _dot_nt(dr.astype(BF16), wo_ref[...])
        ya, yb, sa, sb = _gated_mix(oa_ref, ob_ref, g_ref, bg_ref, wpa_ref, wpb_ref, D)
        mix_ref[...] = (sa * ya + sb * yb).astype(BF16)
        dya = (dmix * sa).astype(BF16)
        dyb = (dmix * sb).astype(BF16)
        dya_ref[...] = dya
        dyb_ref[...] = dyb
        dga = dmix * ya * (sa * (1.0 - sa))
        dgb = dmix * yb * (sb * (1.0 - sb))
        dg_ref[:, :D] = dga.astype(BF16)
        dg_ref[:, D:] = dgb.astype(BF16)
        dbg_ref[:, :D] += jnp.sum(dga, axis=0, keepdims=True)
        dbg_ref[:, D:] += jnp.sum(dgb, axis=0, keepdims=True)
        doa_ref[...] = _dot_nt(dya, wpa_ref[...]).astype(BF16)
        dob_ref[...] = _dot_nt(dyb, wpb_ref[...]).astype(BF16)

    return pl.pallas_call(
        body, name=name, grid=(T // tm,),
        in_specs=[pl.BlockSpec((tm, D), row), pl.BlockSpec((tm, D), row), vec(D),
                  pl.BlockSpec((tm, WIDTH), row), pl.BlockSpec((tm, WIDTH), row), pl.BlockSpec((tm, 2 * D), row),
                  vec(2 * D), wspec(WIDTH, D), wspec(WIDTH, D), wspec(D, D)],
        out_specs=[pl.BlockSpec((tm, D), row), pl.BlockSpec((tm, D), row), pl.BlockSpec((tm, D), row),
                   pl.BlockSpec((tm, D), row), pl.BlockSpec((tm, 2 * D), row),
                   pl.BlockSpec((tm, WIDTH), row), pl.BlockSpec((tm, WIDTH), row),
                   pl.BlockSpec((1, D), fixed), pl.BlockSpec((1, D), fixed), pl.BlockSpec((1, 2 * D), fixed)],
        out_shape=[jax.ShapeDtypeStruct((T, D), F32), jax.ShapeDtypeStruct((T, D), BF16),
                   jax.ShapeDtypeStruct((T, D), BF16), jax.ShapeDtypeStruct((T, D), BF16),
                   jax.ShapeDtypeStruct((T, 2 * D), BF16),
                   jax.ShapeDtypeStruct((T, WIDTH), BF16), jax.ShapeDtypeStruct((T, WIDTH), BF16),
                   jax.ShapeDtypeStruct((1, D), F32), jax.ShapeDtypeStruct((1, D), F32),
                   jax.ShapeDtypeStruct((1, 2 * D), F32)],
        compiler_params=_cparams(("arbitrary",), 56),
    )(dx1, r1, lng, oa, ob, g, bg, wpa, wpb, wo)


def _ffn_fwd(x1, wfi, wfo, wl, lng, lnb, layer, *, alpha, name, job=None):
    T, D = x1.shape
    tf = wfi.shape[-1]
    nj = wfi.shape[0] // 2
    tm = _tile(T, 1024)
    vec = lambda c: pl.BlockSpec((None, 1, c), lambda i, j: (layer, 0, 0))

    def body(x_ref, wg_ref, wu_ref, wo_ref, lg_ref, lb_ref, gs_ref, us_ref, r2_ref, x2_ref, acc_ref, xb_ref):
        j = pl.program_id(1)

        @pl.when(j == 0)
        def _():
            xb_ref[...] = x_ref[...].astype(BF16)
            acc_ref[...] = jnp.zeros_like(acc_ref)

        gv = _dot(xb_ref[...], wg_ref[...])
        uv = _dot(xb_ref[...], wu_ref[...])
        gs_ref[...] = gv.astype(gs_ref.dtype)
        us_ref[...] = uv.astype(us_ref.dtype)
        act = gv * jax.nn.sigmoid(gv) * uv
        acc_ref[...] += _dot(act.astype(BF16), wo_ref[...])

        @pl.when(j == nj - 1)
        def _():
            r2 = alpha * x_ref[...] + acc_ref[...]
            r2_ref[...] = r2
            x2_ref[...] = _ln_fwd(r2, lg_ref[...], lb_ref[...])

    return _call_carrying(
        job, body, name=name, grid=(T // tm, nj),
        in_specs=[pl.BlockSpec((tm, D), lambda i, j: (i, 0)),
                  pl.BlockSpec((None, None, D, tf), lambda i, j: (j, wl, 0, 0)),
                  pl.BlockSpec((None, None, D, tf), lambda i, j: (j + nj, wl, 0, 0)),
                  pl.BlockSpec((None, tf, D), lambda i, j: (wl, j, 0)),
                  vec(D), vec(D)],
        out_specs=[pl.BlockSpec((None, tm, tf), lambda i, j: (j, i, 0)),
                   pl.BlockSpec((None, tm, tf), lambda i, j: (j, i, 0)),
                   pl.BlockSpec((tm, D), lambda i, j: (i, 0)),
                   pl.BlockSpec((tm, D), lambda i, j: (i, 0))],
        out_shape=[jax.ShapeDtypeStruct((nj, T, tf), BF16), jax.ShapeDtypeStruct((nj, T, tf), BF16),
                   jax.ShapeDtypeStruct((T, D), F32), jax.ShapeDtypeStruct((T, D), F32)],
        scratch_shapes=[pltpu.VMEM((tm, D), F32), pltpu.VMEM((tm, D), BF16)],
        vmem_mib=56, args=(x1, wfi, wfi, wfo, lng, lnb))


def _ffn_bwd(dx2, r2, lng, gs, us, wfi, wfo, wl, layer, *, alpha, name, job=None):
    T, D = dx2.shape
    tf = wfi.shape[-1]
    nj = wfi.shape[0] // 2
    tm = _tile(T, 512)
    vec = lambda c: pl.BlockSpec((None, 1, c), lambda i, j: (layer, 0, 0))
    blk = lambda: pl.BlockSpec((None, tm, tf), lambda i, j: (j, i, 0))

    def body(dx_ref, r2_ref, lg_ref, gs_ref, us_ref, wg_ref, wu_ref, wo_ref,
             dr_ref, act_ref, dg_ref, du_ref, dx1_ref, dlg_ref, dlb_ref, acc_ref, drb_ref):
        i = pl.program_id(0)
        j = pl.program_id(1)

        @pl.when((i == 0) & (j == 0))
        def _():
            dlg_ref[...] = jnp.zeros_like(dlg_ref)
            dlb_ref[...] = jnp.zeros_like(dlb_ref)

        @pl.when(j == 0)
        def _():
            dx = dx_ref[...]
            dr, xhat = _ln_bwd(dx, r2_ref[...], lg_ref[...])
            dlg_ref[...] += jnp.sum(dx * xhat, axis=0, keepdims=True)
            dlb_ref[...] += jnp.sum(dx, axis=0, keepdims=True)
            drb_ref[...] = dr.astype(BF16)
            dr_ref[...] = dr.astype(BF16)
            acc_ref[...] = alpha * dr

        dact = _dot_nt(drb_ref[...], wo_ref[...])
        gv = gs_ref[...].astype(F32)
        uv = us_ref[...].astype(F32)
        s = jax.nn.sigmoid(gv)
        silu = gv * s
        act_ref[...] = (silu * uv).astype(BF16)
        dg = (dact * uv * (s * (1.0 + gv * (1.0 - s)))).astype(BF16)
        du = (dact * silu).astype(BF16)
        dg_ref[...] = dg
        du_ref[...] = du
        acc_ref[...] += _dot_nt(dg, wg_ref[...]) + _dot_nt(du, wu_ref[...])

        @pl.when(j == nj - 1)
        def _():
            dx1_ref[...] = acc_ref[...]

    return _call_carrying(
        job, body, name=name, grid=(T // tm, nj),
        in_specs=[pl.BlockSpec((tm, D), lambda i, j: (i, 0)), pl.BlockSpec((tm, D), lambda i, j: (i, 0)), vec(D),
                  blk(), blk(),
                  pl.BlockSpec((None, None, D, tf), lambda i, j: (j, wl, 0, 0)),
                  pl.BlockSpec((None, None, D, tf), lambda i, j: (j + nj, wl, 0, 0)),
                  pl.BlockSpec((None, tf, D), lambda i, j: (wl, j, 0))],
        out_specs=[pl.BlockSpec((tm, D), lambda i, j: (i, 0)), blk(), blk(), blk(),
                   pl.BlockSpec((tm, D), lambda i, j: (i, 0)),
                   pl.BlockSpec((1, D), lambda i, j: (0, 0)), pl.BlockSpec((1, D), lambda i, j: (0, 0))],
        out_shape=[jax.ShapeDtypeStruct((T, D), BF16),
                   jax.ShapeDtypeStruct((nj, T, tf), BF16), jax.ShapeDtypeStruct((nj, T, tf), BF16),
                   jax.ShapeDtypeStruct((nj, T, tf), BF16),
                   jax.ShapeDtypeStruct((T, D), F32),
                   jax.ShapeDtypeStruct((1, D), F32), jax.ShapeDtypeStruct((1, D), F32)],
        scratch_shapes=[pltpu.VMEM((tm, D), F32), pltpu.VMEM((tm, D), BF16)],
        vmem_mib=56, args=(dx2, r2, lng, gs, us, wfi, wfi, wfo))


def _loss_head(y, target, *, name):
    T, D = y.shape
    tm = _tile(T, 1024)

    def body(y_ref, t_ref, dy_ref, sq_ref):
        @pl.when(pl.program_id(0) == 0)
        def _():
            sq_ref[...] = jnp.zeros_like(sq_ref)
        err = y_ref[...] - t_ref[...]
        dy_ref[...] = err * (1.0 / D)
        sq_ref[...] += jnp.sum(err * err, axis=0, keepdims=True)

    return pl.pallas_call(
        body, name=name, grid=(T // tm,),
        in_specs=[pl.BlockSpec((tm, D), lambda i: (i, 0)), pl.BlockSpec((tm, D), lambda i: (i, 0))],
        out_specs=[pl.BlockSpec((tm, D), lambda i: (i, 0)), pl.BlockSpec((1, D), lambda i: (0, 0))],
        out_shape=[jax.ShapeDtypeStruct((T, D), F32), jax.ShapeDtypeStruct((1, D), F32)],
        compiler_params=_cparams(("arbitrary",)),
    )(y, target)


def _my_place():
    return lax.axis_index("x"), lax.axis_index("y"), lax.axis_index("c")


def _peer(place, k):
    x, y, c = place
    return (1 - x if k & 4 else x, 1 - y if k & 2 else y, 1 - c if k & 1 else c)


def _logical(place):
    x, y, c = place
    return 4 * x + 2 * y + c


def _block_of(ref, mode, idx):
    if mode == "blk":
        return ref.at[idx]
    if mode == "col":
        size = ref.shape[2] // N_DEV
        return ref.at[:, :, pl.ds(pl.multiple_of(idx * size, size), size)]
    size = ref.shape[1] // N_DEV
    return ref.at[:, pl.ds(pl.multiple_of(idx * size, size), size), :]


def _full_shape(shard, mode):
    if mode == "blk":
        return (N_DEV,) + shard.shape
    if mode == "col":
        return shard.shape[:2] + (N_DEV * shard.shape[2],)
    return (shard.shape[0], N_DEV * shard.shape[1], shard.shape[2])


class _Exchange:
    def __init__(self, arrays, out_shape, build):
        self.arrays = list(arrays)
        self.out_shape = list(out_shape)
        self.build = build

    def scratch(self):
        n = len(self.arrays)
        return [pltpu.SemaphoreType.DMA((n * N_DEV,)), pltpu.SemaphoreType.DMA((n * N_DEV,)),
                pltpu.SemaphoreType.DMA((n,))]

    def start(self, ins, outs, sems):
        for cp in self.build(ins, outs, *sems):
            cp.start()

    def wait(self, ins, outs, sems):
        for cp in self.build(ins, outs, *sems):
            cp.wait()

    def run(self, name):
        n_in, n_out = len(self.arrays), len(self.out_shape)
        hbm = pl.BlockSpec(memory_space=pltpu.HBM)

        def body(*refs):
            ins, outs, sems = refs[:n_in], refs[n_in:n_in + n_out], refs[n_in + n_out:]
            self.start(ins, outs, sems)
            self.wait(ins, outs, sems)

        return pl.pallas_call(
            body, name=name, in_specs=[hbm] * n_in, out_specs=[hbm] * n_out,
            out_shape=self.out_shape, scratch_shapes=self.scratch(),
        )(*self.arrays)


def _copies_to_all(src_of, dst_of, n, send, recv, local):
    me = _my_place()
    copies = []
    for a in range(n):
        copies.append(pltpu.make_async_copy(src_of(a, _logical(me)), dst_of(a), local.at[a]))
        for k in range(1, N_DEV):
            peer = _peer(me, k)
            copies.append(pltpu.make_async_remote_copy(
                src_ref=src_of(a, _logical(peer)), dst_ref=dst_of(a),
                send_sem=send.at[a * N_DEV + k], recv_sem=recv.at[a * N_DEV + k],
                device_id=peer, device_id_type=MESH))
    return copies


def _gather_job(shards, modes):
    def build(ins, outs, send, recv, local):
        my_id = _logical(_my_place())
        return _copies_to_all(lambda a, dev: ins[a], lambda a: _block_of(outs[a], modes[a], my_id),
                              len(shards), send, recv, local)

    return _Exchange(shards, [jax.ShapeDtypeStruct(_full_shape(s, m), s.dtype) for s, m in zip(shards, modes)], build)


def _grad_block(ref, mode, idx):
    if mode == "blk":
        return ref.at[idx]
    if mode == "col":
        size = ref.shape[1] // N_DEV
        return ref.at[:, pl.ds(pl.multiple_of(idx * size, size), size)]
    size = ref.shape[0] // N_DEV
    return ref.at[pl.ds(pl.multiple_of(idx * size, size), size), :]


def _grad_shard_shape(g, mode):
    if mode == "blk":
        return g.shape[1:]
    if mode == "col":
        return (g.shape[0], g.shape[1] // N_DEV)
    return (g.shape[0] // N_DEV, g.shape[1])


def _grads_job(groups, modes):
    flat = [(g, w, l) for w, per_w in enumerate(groups) for l, g in enumerate(per_w)]

    def build(ins, outs, send, recv, local):
        my_id = _logical(_my_place())
        return _copies_to_all(lambda a, dev: _grad_block(ins[a], modes[flat[a][1]], dev),
                              lambda a: outs[flat[a][1]].at[my_id, flat[a][2]],
                              len(flat), send, recv, local)

    out_shape = [jax.ShapeDtypeStruct((N_DEV, len(per_w)) + _grad_shard_shape(per_w[0], m), per_w[0].dtype)
                 for per_w, m in zip(groups, modes)]
    return _Exchange([g for g, _, _ in flat], out_shape, build)


def _adamw(w, g, m, v):
    m = ADAM_B1 * m + (1.0 - ADAM_B1) * g
    v = ADAM_B2 * v + (1.0 - ADAM_B2) * (g * g)
    m_hat = m / (1.0 - ADAM_B1 ** ADAM_STEP)
    v_hat = v / (1.0 - ADAM_B2 ** ADAM_STEP)
    delta = -ADAM_LR * (m_hat / (jnp.sqrt(v_hat) + ADAM_EPS) + ADAM_WD * w)
    return delta, m, v


def _sum_slots_adamw(slots, w, m, v, *, name):
    n_l = len(slots)
    R, C = slots[0].shape[1:]
    tr = _tile(R, 256)
    n_r = R // tr

    def body(*refs):
        s_refs = refs[:n_l]
        w_ref, m_ref, v_ref, g_out, d_out, m_out, v_out = refs[n_l:]
        for layer in range(n_l):
            @pl.when(pl.program_id(0) == layer)
            def _(s_ref=s_refs[layer]):
                g = s_ref[0].astype(F32)
                for s in range(1, N_DEV):
                    g = g + s_ref[s].astype(F32)
                delta, m_new, v_new = _adamw(w_ref[...], g, m_ref[...], v_ref[...])
                g_out[...] = g
                d_out[...] = delta
                m_out[...] = m_new
                v_out[...] = v_new

    slot_spec = lambda layer: pl.BlockSpec((N_DEV, tr, C), lambda l, i: (0, jnp.where(l == layer, i, 0), 0))
    spec = pl.BlockSpec((tr, C), lambda l, i: (l * n_r + i, 0))
    return pl.pallas_call(
        body, name=name, grid=(n_l, n_r),
        in_specs=[slot_spec(layer) for layer in range(n_l)] + [spec, spec, spec],
        out_specs=[spec] * 4,
        out_shape=[jax.ShapeDtypeStruct((n_l * R, C), F32)] * 4,
        compiler_params=_cparams(("arbitrary", "arbitrary")),
    )(*slots, w, m, v)


def _small_allreduce_adamw(g, w, m, v, *, name):
    R = g.shape[0]
    vmem = pl.BlockSpec(memory_space=pltpu.VMEM)

    def body(g_ref, w_ref, m_ref, v_ref, g_out, d_out, m_out, v_out, slots, send, recv):
        me = _my_place()
        my_id = _logical(me)
        slots[my_id] = g_ref[...]
        copies = []
        for k in range(1, N_DEV):
            cp = pltpu.make_async_remote_copy(
                src_ref=g_ref, dst_ref=slots.at[my_id], send_sem=send.at[k], recv_sem=recv.at[k],
                device_id=_peer(me, k), device_id_type=MESH)
            cp.start()
            copies.append(cp)
        for cp in copies:
            cp.wait()
        total = slots[0]
        for s in range(1, N_DEV):
            total = total + slots[s]
        delta, m_new, v_new = _adamw(w_ref[...], total, m_ref[...], v_ref[...])
        g_out[...] = total
        d_out[...] = delta
        m_out[...] = m_new
        v_out[...] = v_new

    return pl.pallas_call(
        body, name=name,
        in_specs=[vmem] * 4, out_specs=[vmem] * 4,
        out_shape=[jax.ShapeDtypeStruct((R, LANES), F32)] * 4,
        scratch_shapes=[pltpu.VMEM((N_DEV, R, LANES), F32),
                        pltpu.SemaphoreType.DMA((N_DEV,)), pltpu.SemaphoreType.DMA((N_DEV,))],
    )(g, w, m, v)


def _pack(parts):
    flat = jnp.concatenate([p.reshape(-1) for p in parts])
    rows = -(-flat.shape[0] // (8 * LANES)) * 8
    return jnp.pad(flat, (0, rows * LANES - flat.shape[0])).reshape(rows, LANES)


def _unpack(packed, like):
    flat = packed.reshape(-1)
    out, pos = [], 0
    for p in like:
        out.append(flat[pos:pos + p.size].reshape(p.shape))
        pos += p.size
    return out


def kernel(x, w_in, b_gate, rel_bias, w_proj_a, w_proj_b, w_out, ln1_g, ln1_b, w_ffn_in, w_ffn_out, ln2_g, ln2_b, loss_target, m_w_in, m_b_gate, m_rel_bias, m_w_proj_a, m_w_proj_b, m_w_out, m_ln1_g, m_ln1_b, m_w_ffn_in, m_w_ffn_out, m_ln2_g, m_ln2_b, v_w_in, v_b_gate, v_rel_bias, v_w_proj_a, v_w_proj_b, v_w_out, v_ln1_g, v_ln1_b, v_w_ffn_in, v_w_ffn_out, v_ln2_g, v_ln2_b):
    L = w_in.shape[0]
    T, D = x.shape[1], x.shape[2]
    alpha = float((2 * L) ** 0.25)
    n_qkv = 6 * WIDTH

    big = [w_in, w_proj_a, w_proj_b, w_out, w_ffn_in, w_ffn_out]
    kinds = ["in", "pa", "pb", "o", "fi", "fo"]
    modes = ["col", "col", "col", "row", "blk", "row"]
    mode_of = dict(zip(kinds, modes))
    w_bf = dict(zip(kinds, [w.astype(BF16) for w in big]))

    def gather_of(ks, l):
        return _gather_job([w_bf[k][l:l + 1] for k in ks], [mode_of[k] for k in ks])

    W = [dict() for _ in range(L)]
    (W[0]["in"],) = gather_of(["in"], 0).run("gather_w_in_first")
    vec3 = lambda a: a[:, None, :]
    bg3, l1g, l1b, l2g, l2b = vec3(b_gate), vec3(ln1_g), vec3(ln1_b), vec3(ln2_g), vec3(ln2_b)
    b_col0 = 3 * WIDTH // LANES

    h = x[0]
    saved = []
    for l in range(L):
        ahead = l + 1 < L
        soon = ["pa", "pb", "o", "fo"]
        (qkv, gates), got = _in_proj(h, W[l]["in"], 0, n_qkv=n_qkv, name=f"in_proj_{l}",
                                     job=gather_of(soon, 0) if l == 0 else None)
        W[l].update(zip(soon, got))
        kvpad = jnp.pad(qkv[:, WIDTH:3 * WIDTH], ((A_WIN - A_TQ, 0), (0, 0)))
        bias = _toeplitz_bias(rel_bias[l])
        oa, got = _attn_a_fwd(qkv, kvpad, bias, name=f"attn_a_fwd_{l}", job=gather_of(["fi"], 0) if l == 0 else None)
        W[l].update(zip(["fi"], got))
        early = ["in", "pa", "pb", "o"]
        ob, got = _attn_b_fwd(qkv, col0=b_col0, name=f"attn_b_fwd_{l}", job=gather_of(early, l + 1) if ahead else None)
        W[l + 1 if ahead else l].update(zip(early, got))
        x1, r1, x1b = _proj_fwd(oa, ob, gates, bg3, W[l]["pa"], W[l]["pb"], W[l]["o"], 0, h, l1g, l1b, l,
                           alpha=alpha, name=f"proj_fwd_{l}")
        (gs, us, r2, x2), got = _ffn_fwd(x1, W[l]["fi"], W[l]["fo"], 0, l2g, l2b, l, alpha=alpha, name=f"ffn_fwd_{l}",
                                         job=gather_of(["fi", "fo"], l + 1) if ahead else None)
        W[l + 1 if ahead else l].update(zip(["fi", "fo"], got))
        saved.append((h, qkv, gates, kvpad, bias, oa, ob, x1b, r1, gs, us, r2))
        h = x2

    d_h, sq = _loss_head(h, loss_target[0], name="loss_head")
    loss = lax.psum((0.5 / D) * jnp.sum(sq), ("x", "y", "c"))

    g_bg, g_rb, g_l1g, g_l1b, g_l2g, g_l2b = ([None] * L for _ in range(6))
    slot = {k: [None] * L for k in kinds}

    def exchange_of(ks, grads):
        return _grads_job([[g] for g in grads], [mode_of[k] for k in ks])

    w_in_above = None
    for l in reversed(range(L)):
        xin, qkv, gates, kvpad, bias, oa, ob, x1b, r1, gs, us, r2 = saved[l]
        (dr2, act, dgt, dup, dx1, g_l2g[l], g_l2b[l]), got = _ffn_bwd(
            d_h, r2, l2g, gs, us, W[l]["fi"], W[l]["fo"], 0, l, alpha=alpha, name=f"ffn_bwd_{l}", job=w_in_above)
        if w_in_above is not None:
            (slot["in"][l + 1],) = got
        g_fo = _mm_tn_blocked_a(act, dr2, name=f"grad_w_ffn_out_{l}").reshape(-1, D)
        g_fi = jnp.concatenate(_mm_tn_blocked_pair(x1b, dgt, dup, name=f"grad_w_ffn_in_{l}"), axis=0)
        (dr1, mixin, dya, dyb, dgates, doa, dob, g_l1g[l], g_l1b[l], g_bg[l]) = _proj_bwd(
            dx1, r1, l1g, oa, ob, gates, bg3, W[l]["pa"], W[l]["pb"], W[l]["o"], 0, l, name=f"proj_bwd_{l}")
        g_o = _mm_tn(mixin, dr1, tm=_tile(D, 1024), tn=_tile(D, 1024), name=f"grad_w_out_{l}")
        g_pa = _mm_tn(oa, dya, tm=WIDTH, tn=_tile(D, 1024), name=f"grad_w_proj_a_{l}")
        g_pb = _mm_tn(ob, dyb, tm=WIDTH, tn=_tile(D, 1024), name=f"grad_w_proj_b_{l}")
        (dqa, dka, dva, dbias), (slot["fi"][l], slot["fo"][l]) = _attn_a_bwd(
            qkv, kvpad, bias, doa, name=f"attn_a_bwd_{l}", job=exchange_of(["fi", "fo"], [g_fi, g_fo]))
        g_rb[l] = _toeplitz_bias_grad(dbias)
        (dqb, dkb, dvb), (slot["pa"][l], slot["pb"][l], slot["o"][l]) = _attn_b_bwd(
            qkv, ob, dob, col0=b_col0, name=f"attn_b_bwd_{l}", job=exchange_of(["pa", "pb", "o"], [g_pa, g_pb, g_o]))
        d_pre = jnp.concatenate([dqa, dka, dva, dqb, dkb, dvb, dgates], axis=1)
        g_in = _mm_tn(xin, d_pre, tm=D, tn=4 * w_in.shape[2], name=f"grad_w_in_{l}")
        w_in_above = exchange_of(["in"], [g_in])
        d_h, got = _mm_nt_add(d_pre, W[l]["in"], 0, dr1, alpha, name=f"grad_x_{l}", job=w_in_above if l == 0 else None)
        if l == 0:
            (slot["in"][0],) = got
    grad_x = d_h[None]

    moments_m = [m_w_in, m_w_proj_a, m_w_proj_b, m_w_out, m_w_ffn_in, m_w_ffn_out]
    moments_v = [v_w_in, v_w_proj_a, v_w_proj_b, v_w_out, v_w_ffn_in, v_w_ffn_out]
    names = ["w_in", "w_proj_a", "w_proj_b", "w_out", "w_ffn_in", "w_ffn_out"]
    big_out = {}
    for nm, k, w, m, v in zip(names, kinds, big, moments_m, moments_v):
        two = lambda a: a.reshape(-1, a.shape[-1])
        per_layer = [s.reshape(N_DEV, -1, s.shape[-1]) for s in slot[k]]
        res = _sum_slots_adamw(per_layer, two(w), two(m), two(v), name=f"adamw_{nm}")
        big_out[nm] = [r.reshape(w.shape) for r in res]

    small_w = [b_gate, rel_bias, ln1_g, ln1_b, ln2_g, ln2_b]
    small_g = [jnp.stack(g) for g in (g_bg, g_rb, g_l1g, g_l1b, g_l2g, g_l2b)]
    small_m = [m_b_gate, m_rel_bias, m_ln1_g, m_ln1_b, m_ln2_g, m_ln2_b]
    small_v = [v_b_gate, v_rel_bias, v_ln1_g, v_ln1_b, v_ln2_g, v_ln2_b]
    res = _small_allreduce_adamw(_pack(small_g), _pack(small_w), _pack(small_m), _pack(small_v),
                                 name="allreduce_small_adamw")
    small_names = ["b_gate", "rel_bias", "ln1_g", "ln1_b", "ln2_g", "ln2_b"]
    small_out = {nm: [] for nm in small_names}
    for packed in res:
        for nm, arr in zip(small_names, _unpack(packed, small_w)):
            small_out[nm].append(arr)

    order = ["w_in", "b_gate", "rel_bias", "w_proj_a", "w_proj_b", "w_out", "ln1_g", "ln1_b",
             "w_ffn_in", "w_ffn_out", "ln2_g", "ln2_b"]
    every = {**big_out, **small_out}
    outs = [loss, grad_x]
    for kind in range(4):
        outs += [every[nm][kind] for nm in order]
    return tuple(outs)
```

```python
import functools
import math

import jax
import jax.numpy as jnp
import numpy as np
from jax import lax
from jax.experimental import pallas as pl
from jax.experimental.pallas import tpu as pltpu

F32 = jnp.float32
BF16 = jnp.bfloat16

HEAD_DIM = 64
CHUNK = 64
LEFT_CHUNKS = 8
REL_CLIP = 256
N_REL = 2 * REL_CLIP + 1
WIDTH = 512
LANES = 128
A_TQ = 256
A_WIN = A_TQ + LEFT_CHUNKS * CHUNK
A_STRIP = 128
B_TQ = 512
B_TS = 256
B_PIECE = 64
B_DEAD = -160.0
LN_EPS = 1e-5
QK_SCALE = 1.0 / math.sqrt(HEAD_DIM)
LOG2E = 1.4426950408889634
NEG = -1e30

ADAM_LR = 0.001
ADAM_B1 = 0.9
ADAM_B2 = 0.999
ADAM_EPS = 1e-08
ADAM_WD = 0.01
ADAM_STEP = 10

N_DEV = 8
MESH = pl.DeviceIdType.MESH
MIB = 1024 * 1024


def _cparams(sem=None, vmem_mib=48):
    return pltpu.CompilerParams(dimension_semantics=sem, vmem_limit_bytes=vmem_mib * MIB)


def _dot(a, b):
    return jnp.dot(a, b, preferred_element_type=F32)


def _dot_nt(a, b):
    return lax.dot_general(a, b, (((1,), (1,)), ((), ())), preferred_element_type=F32)


def _dot_tn(a, b):
    return lax.dot_general(a, b, (((0,), (0,)), ((), ())), preferred_element_type=F32)


def _tile(n, pref):
    if n <= pref:
        return n
    for t in range(pref - pref % 8, 0, -8):
        if n % t == 0:
            return t
    raise ValueError((n, pref))


def _in_proj(a, w, layer, *, n_qkv, name, job=None):
    M, K = a.shape
    N = w.shape[2]
    tm = _tile(M, 1024)
    tn = 1024
    assert n_qkv % tn == 0 and (N - n_qkv) % tn == 0
    n_q = n_qkv // tn

    def body(a_ref, w_ref, q_ref, g_ref, ab_ref):
        j = pl.program_id(1)

        @pl.when(j == 0)
        def _():
            ab_ref[...] = a_ref[...].astype(BF16)

        res = _dot(ab_ref[...], w_ref[...])

        @pl.when(j < n_q)
        def _():
            q_ref[...] = res.astype(BF16)

        @pl.when(j >= n_q)
        def _():
            g_ref[...] = res

    return _call_carrying(
        job, body, name=name, grid=(M // tm, N // tn),
        in_specs=[pl.BlockSpec((tm, K), lambda i, j: (i, 0)),
                  pl.BlockSpec((None, K, tn), lambda i, j: (layer, 0, j))],
        out_specs=[pl.BlockSpec((tm, tn), lambda i, j: (i, jnp.minimum(j, n_q - 1))),
                   pl.BlockSpec((tm, tn), lambda i, j: (i, jnp.maximum(j - n_q, 0)))],
        out_shape=[jax.ShapeDtypeStruct((M, n_qkv), BF16), jax.ShapeDtypeStruct((M, N - n_qkv), F32)],
        scratch_shapes=[pltpu.VMEM((tm, K), BF16)], vmem_mib=48, args=(a, w))


def _mm_nt_add(a, w, layer, add, add_scale, *, name, job=None):
    M, K = a.shape
    N = w.shape[1]
    tm = _tile(M, 1024)
    tk = _tile(K, 1024)

    def body(a_ref, w_ref, add_ref, o_ref):
        @pl.when(pl.program_id(1) == 0)
        def _():
            o_ref[...] = add_scale * add_ref[...]
        o_ref[...] += _dot_nt(a_ref[...], w_ref[...])

    (out,), rode = _call_carrying(
        job, body, name=name, grid=(M // tm, K // tk),
        in_specs=[pl.BlockSpec((tm, tk), lambda i, k: (i, k)),
                  pl.BlockSpec((None, N, tk), lambda i, k: (layer, 0, k)),
                  pl.BlockSpec((tm, N), lambda i, k: (i, 0))],
        out_specs=[pl.BlockSpec((tm, N), lambda i, k: (i, 0))],
        out_shape=[jax.ShapeDtypeStruct((M, N), F32)],
        scratch_shapes=[], vmem_mib=48, args=(a, w, add))
    return out, rode


def _tn_body(k_axis, n_k):
    def body(a_ref, b_ref, o_ref, acc_ref):
        k = pl.program_id(k_axis)

        @pl.when(k == 0)
        def _():
            acc_ref[...] = jnp.zeros_like(acc_ref)
        acc_ref[...] += _dot_tn(a_ref[...].astype(BF16), b_ref[...].astype(BF16))

        @pl.when(k == n_k - 1)
        def _():
            o_ref[...] = acc_ref[...].astype(o_ref.dtype)
    return body


def _mm_tn(a, b, *, tm, tn, name):
    T, M = a.shape
    N = b.shape[1]
    tk = _tile(T, 1024)
    return pl.pallas_call(
        _tn_body(2, T // tk), name=name, grid=(M // tm, N // tn, T // tk),
        in_specs=[pl.BlockSpec((tk, tm), lambda i, j, k: (k, i)),
                  pl.BlockSpec((tk, tn), lambda i, j, k: (k, j))],
        out_specs=pl.BlockSpec((tm, tn), lambda i, j, k: (i, j)),
        out_shape=jax.ShapeDtypeStruct((M, N), BF16),
        scratch_shapes=[pltpu.VMEM((tm, tn), F32)],
        compiler_params=_cparams(("parallel", "parallel", "arbitrary")),
    )(a, b)


def _mm_tn_blocked_pair(a, b1, b2, *, name):
    T, M = a.shape
    S, _, N = b1.shape
    tk = _tile(T, 1024)
    n_k = T // tk

    def body(a_ref, b1_ref, b2_ref, o1_ref, o2_ref, acc1_ref, acc2_ref):
        k = pl.program_id(1)

        @pl.when(k == 0)
        def _():
            acc1_ref[...] = jnp.zeros_like(acc1_ref)
            acc2_ref[...] = jnp.zeros_like(acc2_ref)

        a_t = a_ref[...].astype(BF16)
        acc1_ref[...] += _dot_tn(a_t, b1_ref[...])
        acc2_ref[...] += _dot_tn(a_t, b2_ref[...])

        @pl.when(k == n_k - 1)
        def _():
            o1_ref[...] = acc1_ref[...].astype(o1_ref.dtype)
            o2_ref[...] = acc2_ref[...].astype(o2_ref.dtype)

    blk = lambda: pl.BlockSpec((None, tk, N), lambda s, k: (s, k, 0))
    out = lambda: pl.BlockSpec((None, M, N), lambda s, k: (s, 0, 0))
    return pl.pallas_call(
        body, name=name, grid=(S, n_k),
        in_specs=[pl.BlockSpec((tk, M), lambda s, k: (k, 0)), blk(), blk()],
        out_specs=[out(), out()],
        out_shape=[jax.ShapeDtypeStruct((S, M, N), BF16)] * 2,
        scratch_shapes=[pltpu.VMEM((M, N), F32), pltpu.VMEM((M, N), F32)],
        compiler_params=_cparams(("parallel", "arbitrary")),
    )(a, b1, b2)


def _mm_tn_blocked_a(a, b, *, name):
    S, T, M = a.shape
    N = b.shape[1]
    tk = _tile(T, 1024)
    return pl.pallas_call(
        _tn_body(1, T // tk), name=name, grid=(S, T // tk),
        in_specs=[pl.BlockSpec((None, tk, M), lambda s, k: (s, k, 0)),
                  pl.BlockSpec((tk, N), lambda s, k: (k, 0))],
        out_specs=pl.BlockSpec((None, M, N), lambda s, k: (s, 0, 0)),
        out_shape=jax.ShapeDtypeStruct((S, M, N), BF16),
        scratch_shapes=[pltpu.VMEM((M, N), F32)],
        compiler_params=_cparams(("parallel", "arbitrary")),
    )(a, b)


def _ln_fwd(r, g, b):
    mu = jnp.mean(r, axis=-1, keepdims=True)
    xc = r - mu
    var = jnp.mean(xc * xc, axis=-1, keepdims=True)
    return xc * lax.rsqrt(var + LN_EPS) * g + b


def _ln_bwd(dy, r, g):
    mu = jnp.mean(r, axis=-1, keepdims=True)
    xc = r - mu
    var = jnp.mean(xc * xc, axis=-1, keepdims=True)
    rstd = lax.rsqrt(var + LN_EPS)
    xhat = xc * rstd
    dxh = dy * g
    m1 = jnp.mean(dxh, axis=-1, keepdims=True)
    m2 = jnp.mean(dxh * xhat, axis=-1, keepdims=True)
    return rstd * (dxh - m1 - xhat * m2), xhat


def _lane_is_head0():
    return lax.broadcasted_iota(jnp.int32, (1, LANES), 1) < HEAD_DIM


def _band_shape():
    a = np.arange(A_TQ)[:, None] // CHUNK
    b = np.arange(A_WIN)[None, :] // CHUNK
    return (b >= a) & (b <= a + LEFT_CHUNKS)


def _band_streams(strip):
    return [dict(h=h, n=strip, rows=pl.ds(r0, strip)) for h in range(2) for r0 in range(0, A_TQ, strip)]


def _band_scores(st, i, qh_ref, k2, bias_ref):
    s = _dot_nt(qh_ref[st["h"], st["rows"], :], k2) * QK_SCALE + bias_ref[st["h"], st["rows"], :]
    c = lax.broadcasted_iota(jnp.int32, (st["n"], A_WIN), 1)
    st["s"] = jnp.where(c >= LEFT_CHUNKS * CHUNK - i * A_TQ, s, NEG)


def _band_softmax(st):
    s = st.pop("s")
    e = jnp.exp(s - jnp.max(s, axis=1, keepdims=True))
    st["p"] = e / jnp.sum(e, axis=1, keepdims=True)


def _attn_a_fwd(qkv, kvpad, bias, *, name, job=None):
    T = qkv.shape[0]
    n_hp = WIDTH // LANES

    def body(q_ref, k_ref, v_ref, bias_ref, o_ref, qh_ref, acc_ref):
        i = pl.program_id(1)
        row0 = pl.multiple_of(i * A_TQ, A_TQ)
        q2 = q_ref[...]
        k2 = k_ref[pl.ds(row0, A_WIN), :]
        v2 = v_ref[pl.ds(row0, A_WIN), :]
        head0 = _lane_is_head0()
        qh_ref[0] = jnp.where(head0, q2, jnp.zeros_like(q2))
        qh_ref[1] = jnp.where(head0, jnp.zeros_like(q2), q2)

        def scores(st):
            _band_scores(st, i, qh_ref, k2, bias_ref)

        def values(st):
            acc_ref[st["h"], st["rows"], :] = _dot(st.pop("p").astype(BF16), v2)

        _skewed(_band_streams(A_STRIP), [scores, _band_softmax, values])
        o_ref[...] = jnp.where(head0, acc_ref[0], acc_ref[1]).astype(o_ref.dtype)

    (out,), rode = _call_carrying(
        job, body, name=name, grid=(n_hp, T // A_TQ),
        in_specs=[pl.BlockSpec((A_TQ, LANES), lambda hp, i: (i, hp)),
                  pl.BlockSpec((T + A_WIN - A_TQ, LANES), lambda hp, i: (0, hp)),
                  pl.BlockSpec((T + A_WIN - A_TQ, LANES), lambda hp, i: (0, hp + n_hp)),
                  pl.BlockSpec((2, A_TQ, A_WIN), lambda hp, i: (hp, 0, 0))],
        out_specs=[pl.BlockSpec((A_TQ, LANES), lambda hp, i: (i, hp))],
        out_shape=[jax.ShapeDtypeStruct((T, WIDTH), BF16)],
        scratch_shapes=[pltpu.VMEM((2, A_TQ, LANES), BF16), pltpu.VMEM((2, A_TQ, LANES), F32)],
        vmem_mib=48, args=(qkv, kvpad, kvpad, bias))
    return out, rode


def _attn_a_bwd(qkv, kvpad, bias, do, *, name, job=None):
    T = qkv.shape[0]
    TP = T + A_WIN - A_TQ
    n_hp = WIDTH // LANES

    def body(q_ref, k_ref, v_ref, bias_ref, do_ref, dq_ref, dko_ref, dvo_ref, db_ref,
             qh_ref, doh_ref, dqa_ref, dk_ref, dv_ref):
        i = pl.program_id(1)

        @pl.when(i == 0)
        def _():
            dk_ref[...] = jnp.zeros_like(dk_ref)
            dv_ref[...] = jnp.zeros_like(dv_ref)
            db_ref[...] = jnp.zeros_like(db_ref)

        row0 = pl.multiple_of(i * A_TQ, A_TQ)
        window = pl.ds(row0, A_WIN)
        q2 = q_ref[...]
        do2 = do_ref[...]
        k2 = k_ref[window, :]
        v2 = v_ref[window, :]
        head0 = _lane_is_head0()
        zero = jnp.zeros_like(q2)
        qh_ref[0] = jnp.where(head0, q2, zero)
        qh_ref[1] = jnp.where(head0, zero, q2)
        doh_ref[0] = jnp.where(head0, do2, zero)
        doh_ref[1] = jnp.where(head0, zero, do2)
        dk = [[], []]
        dv = [[], []]

        def scores(st):
            _band_scores(st, i, qh_ref, k2, bias_ref)
            st["dp"] = _dot_nt(doh_ref[st["h"], st["rows"], :], v2)

        def dscores(st):
            _band_softmax(st)
            p, dp = st.pop("p"), st.pop("dp")
            ds = p * (dp - jnp.sum(p * dp, axis=1, keepdims=True))
            db_ref[st["h"], st["rows"], :] += ds
            st["dsb"] = (ds * QK_SCALE).astype(BF16)
            st["pb"] = p.astype(BF16)

        def grads(st):
            h, rows = st["h"], st["rows"]
            dsb = st.pop("dsb")
            dqa_ref[h, rows, :] = _dot(dsb, k2)
            dk[h].append(_dot_tn(dsb, q_ref[rows, :]))
            dv[h].append(_dot_tn(st.pop("pb"), do_ref[rows, :]))

        _skewed(_band_streams(A_TQ), [scores, dscores, grads])
        dq_ref[...] = jnp.where(head0, dqa_ref[0], dqa_ref[1]).astype(dq_ref.dtype)
        dk_ref[window, :] += jnp.where(head0, sum(dk[0]), sum(dk[1]))
        dv_ref[window, :] += jnp.where(head0, sum(dv[0]), sum(dv[1]))

        @pl.when(i == T // A_TQ - 1)
        def _():
            dko_ref[...] = dk_ref[TP - T:, :].astype(BF16)
            dvo_ref[...] = dv_ref[TP - T:, :].astype(BF16)

    return _call_carrying(
        job, body, name=name, grid=(n_hp, T // A_TQ),
        in_specs=[pl.BlockSpec((A_TQ, LANES), lambda hp, i: (i, hp)),
                  pl.BlockSpec((TP, LANES), lambda hp, i: (0, hp)),
                  pl.BlockSpec((TP, LANES), lambda hp, i: (0, hp + n_hp)),
                  pl.BlockSpec((2, A_TQ, A_WIN), lambda hp, i: (hp, 0, 0)),
                  pl.BlockSpec((A_TQ, LANES), lambda hp, i: (i, hp))],
        out_specs=[pl.BlockSpec((A_TQ, LANES), lambda hp, i: (i, hp)),
                   pl.BlockSpec((T, LANES), lambda hp, i: (0, hp)),
                   pl.BlockSpec((T, LANES), lambda hp, i: (0, hp)),
                   pl.BlockSpec((2, A_TQ, A_WIN), lambda hp, i: (hp, 0, 0))],
        out_shape=[jax.ShapeDtypeStruct((T, WIDTH), BF16),
                   jax.ShapeDtypeStruct((T, WIDTH), BF16),
                   jax.ShapeDtypeStruct((T, WIDTH), BF16),
                   jax.ShapeDtypeStruct((WIDTH // HEAD_DIM, A_TQ, A_WIN), F32)],
        scratch_shapes=[pltpu.VMEM((2, A_TQ, LANES), BF16), pltpu.VMEM((2, A_TQ, LANES), BF16),
                        pltpu.VMEM((2, A_TQ, LANES), F32),
                        pltpu.VMEM((TP, LANES), F32), pltpu.VMEM((TP, LANES), F32)],
        vmem_mib=56, args=(qkv, kvpad, kvpad, bias, do))


_N_DIAG = 2 * CHUNK - 1
_EXT_TOP = LEFT_CHUNKS * CHUNK + CHUNK - 1 + REL_CLIP


def _toeplitz_bias(rb):
    H = rb.shape[0]
    ext = jnp.concatenate([rb, jnp.broadcast_to(rb[:, N_REL - 1:], (H, _EXT_TOP + 1 - N_REL))], axis=1)
    vec = jnp.stack([ext[:, _EXT_TOP - (_N_DIAG - 1) - CHUNK * k:_EXT_TOP - CHUNK * k + 1]
                     for k in range(LEFT_CHUNKS + 1)], axis=1)
    rev = jnp.pad(vec[:, :, ::-1], ((0, 0), (0, 0), (0, 1)))
    flat = jnp.broadcast_to(rev[:, :, None, :], (H, LEFT_CHUNKS + 1, CHUNK, _N_DIAG + 1))
    skew = flat.reshape(H, LEFT_CHUNKS + 1, -1)[:, :, :CHUNK * _N_DIAG].reshape(H, LEFT_CHUNKS + 1, CHUNK, _N_DIAG)
    blocks = skew[:, :, :, CHUNK - 1:]
    neg = jnp.full((H, CHUNK, CHUNK), NEG, F32)
    rows = [jnp.concatenate([blocks[:, b - a] if 0 <= b - a <= LEFT_CHUNKS else neg for b in range(A_WIN // CHUNK)],
                            axis=2) for a in range(A_TQ // CHUNK)]
    return jnp.concatenate(rows, axis=1)


def _toeplitz_bias_grad(db):
    H = db.shape[0]
    d5 = db.reshape(H, A_TQ // CHUNK, CHUNK, A_WIN // CHUNK, CHUNK)
    g_blocks = jnp.stack([sum(d5[:, a, :, a + k, :] for a in range(A_TQ // CHUNK))
                          for k in range(LEFT_CHUNKS + 1)], axis=1)
    d_skew = jnp.pad(g_blocks, ((0, 0), (0, 0), (0, 0), (CHUNK - 1, 0)))
    d_flat = jnp.pad(d_skew.reshape(H, LEFT_CHUNKS + 1, CHUNK * _N_DIAG), ((0, 0), (0, 0), (0, CHUNK)))
    g_vec = jnp.sum(d_flat.reshape(H, LEFT_CHUNKS + 1, CHUNK, _N_DIAG + 1), axis=2)[:, :, :_N_DIAG][:, :, ::-1]
    g_ext = sum(jnp.pad(g_vec[:, k], ((0, 0), (_EXT_TOP - (_N_DIAG - 1) - CHUNK * k, CHUNK * k)))
                for k in range(LEFT_CHUNKS + 1))
    return jnp.concatenate([g_ext[:, :N_REL - 1], jnp.sum(g_ext[:, N_REL - 1:], axis=1, keepdims=True)], axis=1)


def _split_bf16(x):
    hi = x.astype(BF16)
    lo = (x - hi.astype(F32)).astype(BF16)
    return hi, lo


def _sb_streams(d, strips=None, **tile):
    out = []
    for h in range(2):
        for r in (range(B_TQ // B_TS) if strips is None else strips):
            if d is not None and d > r:
                continue
            out.append(dict(h=h, r=r, rows=pl.ds(r * B_TS, B_TS), diag=(d is not None and d == r), **tile))
    return out


def _sb_sweep(i, car_ref, streams_of, run):
    sub = B_TQ // B_TS
    run([st for d in reversed(range(sub)) for st in streams_of(i * sub + d, d, None)])

    def alive(c):
        return (c[0] < i * sub) & (c[1] > B_DEAD)

    def step(c):
        kb = i * sub - 1 - c[0]
        if sub > 1:
            lower_alive = jnp.max(car_ref[:, B_TS:, :]) > B_DEAD
            lax.cond(lower_alive, lambda: run(streams_of(kb, None, None)), lambda: run(streams_of(kb, None, [0])))
        else:
            run(streams_of(kb, None, None))
        return c[0] + 1, jnp.max(car_ref[...])

    lax.while_loop(alive, step, (jnp.int32(0), jnp.float32(0.0)))


def _piece_rows(st, p):
    return pl.ds(st["r"] * B_TS + p, B_PIECE)


def _rows_cat(parts):
    return jnp.concatenate(parts, axis=0)


def _skewed(streams, stages):
    for t in range(len(streams) + len(stages) - 1):
        for s, st in enumerate(streams):
            if 0 <= t - s < len(stages):
                stages[t - s](st)


def _sb_logs(st, z2):
    log_beta, log_keep, keep_bf = [], [], []
    for p in range(0, B_TS, B_PIECE):
        z = z2[p:p + B_PIECE]
        lp2 = jnp.log(1.0 + jnp.exp2(-jnp.abs(z))) * LOG2E
        lb = jnp.minimum(z, 0.0) - lp2
        lk = lb - z
        if st["diag"]:
            lk = jnp.where(_strict_lower(p), lk, 0.0)
        log_beta.append(lb)
        log_keep.append(lk)
        keep_bf.append(lk.astype(BF16))
    st["log_beta"] = _rows_cat(log_beta)
    st["log_keep"] = _rows_cat(log_keep)
    st["keep_bf"] = _rows_cat(keep_bf)


def _strict_lower(p):
    t = p + lax.broadcasted_iota(jnp.int32, (B_PIECE, B_TS), 0)
    s = lax.broadcasted_iota(jnp.int32, (B_PIECE, B_TS), 1)
    return s < t


def _tri(strict):
    j = lax.broadcasted_iota(jnp.int32, (B_TS, B_TS), 0)
    s = lax.broadcasted_iota(jnp.int32, (B_TS, B_TS), 1)
    return jnp.where(j > s if strict else j >= s, 1.0, 0.0).astype(BF16)


def _call_carrying(job, body, *, name, grid, in_specs, out_specs, out_shape, scratch_shapes, vmem_mib, args):
    n_in, n_out, n_scr = len(in_specs), len(out_specs), len(scratch_shapes)
    if job is None:
        res = pl.pallas_call(body, name=name, grid=grid, in_specs=in_specs, out_specs=out_specs,
                             out_shape=out_shape, scratch_shapes=scratch_shapes,
                             compiler_params=_cparams(("arbitrary",) * len(grid), vmem_mib))(*args)
        return res, []
    j_in, j_out = len(job.arrays), len(job.out_shape)
    hbm = pl.BlockSpec(memory_space=pltpu.HBM)

    def carrying(*refs):
        refs = list(refs)
        ins, refs = refs[:n_in], refs[n_in:]
        j_ins, refs = refs[:j_in], refs[j_in:]
        outs, refs = refs[:n_out], refs[n_out:]
        j_outs, refs = refs[:j_out], refs[j_out:]
        scr, sems = refs[:n_scr], refs[n_scr:]
        first = functools.reduce(jnp.logical_and, [pl.program_id(d) == 0 for d in range(len(grid))])
        last = functools.reduce(jnp.logical_and, [pl.program_id(d) == grid[d] - 1 for d in range(len(grid))])

        @pl.when(first)
        def _():
            job.start(j_ins, j_outs, sems)

        body(*ins, *outs, *scr)

        @pl.when(last)
        def _():
            job.wait(j_ins, j_outs, sems)

    res = pl.pallas_call(
        carrying, name=name, grid=grid,
        in_specs=list(in_specs) + [hbm] * j_in, out_specs=list(out_specs) + [hbm] * j_out,
        out_shape=list(out_shape) + job.out_shape, scratch_shapes=list(scratch_shapes) + job.scratch(),
        compiler_params=_cparams(("arbitrary",) * len(grid), vmem_mib))(*args, *job.arrays)
    return res[:n_out], res[n_out:]


def _attn_b_fwd(qkv, *, col0, name, job=None):
    T = qkv.shape[0]
    n_hp = WIDTH // LANES
    sub = B_TQ // B_TS

    def body(q_ref, k_ref, v_ref, o_ref, acc_ref, car_ref, qh_ref):
        i = pl.program_id(1)
        q2 = q_ref[...]
        head0 = _lane_is_head0()
        qh_ref[0] = jnp.where(head0, q2, jnp.zeros_like(q2))
        qh_ref[1] = jnp.where(head0, jnp.zeros_like(q2), q2)
        tri_s = _tri(True)
        acc_ref[...] = jnp.zeros_like(acc_ref)
        car_ref[...] = jnp.zeros_like(car_ref)

        def streams_of(kb, d, strips):
            keys = pl.ds(pl.multiple_of(kb * B_TS, B_TS), B_TS)
            return _sb_streams(d, strips, k2=k_ref[keys, :], v2=v_ref[keys, :])

        def scores(st):
            st["z2"] = _dot_nt(qh_ref[st["h"], st["rows"], :], st.pop("k2")) * (QK_SCALE * LOG2E)

        def logs(st):
            _sb_logs(st, st.pop("z2"))

        def suffix(st):
            st["suffix"] = _dot(st.pop("keep_bf"), tri_s)

        def weights(st):
            log_beta, suffix, log_keep = st.pop("log_beta"), st.pop("suffix"), st.pop("log_keep")
            wb = []
            for p in range(0, B_TS, B_PIECE):
                rows = _piece_rows(st, p)
                car = car_ref[st["h"], rows, :]
                w = jnp.exp2(log_beta[p:p + B_PIECE] + suffix[p:p + B_PIECE] + car)
                if st["diag"]:
                    w = jnp.where(_strict_lower(p), w, 0.0)
                wb.append(w.astype(BF16))
                car_ref[st["h"], rows, :] = car + jnp.sum(log_keep[p:p + B_PIECE], axis=1, keepdims=True)
            st["wb"] = _rows_cat(wb)

        def values(st):
            acc_ref[st["h"], st["rows"], :] += _dot(st.pop("wb"), st.pop("v2"))

        _sb_sweep(i, car_ref, streams_of, lambda sts: _skewed(sts, [scores, logs, suffix, weights, values]))
        o_ref[...] = jnp.where(head0, acc_ref[0], acc_ref[1])

    (out,), rode = _call_carrying(
        job, body, name=name, grid=(n_hp, T // B_TQ),
        in_specs=[pl.BlockSpec((B_TQ, LANES), lambda hp, i: (i, hp + col0)),
                  pl.BlockSpec((T, LANES), lambda hp, i: (0, hp + col0 + n_hp)),
                  pl.BlockSpec((T, LANES), lambda hp, i: (0, hp + col0 + 2 * n_hp))],
        out_specs=[pl.BlockSpec((B_TQ, LANES), lambda hp, i: (i, hp))],
        out_shape=[jax.ShapeDtypeStruct((T, WIDTH), F32)],
        scratch_shapes=[pltpu.VMEM((2, B_TQ, LANES), F32), pltpu.VMEM((2, B_TQ, 1), F32),
                        pltpu.VMEM((2, B_TQ, LANES), BF16)],
        vmem_mib=48, args=(qkv, qkv, qkv))
    return out, rode


def _attn_b_bwd(qkv, out, do, *, col0, name, job=None):
    T = qkv.shape[0]
    n_hp = WIDTH // LANES
    sub = B_TQ // B_TS

    def body(q_ref, k_ref, v_ref, o_ref, do_ref, dq_ref, dko_ref, dvo_ref,
             dqa_ref, car_ref, carr_ref, tot_ref, qh_ref, doh_ref, qs_ref, dk_ref, dv_ref):
        i = pl.program_id(1)

        @pl.when(i == 0)
        def _():
            dk_ref[...] = jnp.zeros_like(dk_ref)
            dv_ref[...] = jnp.zeros_like(dv_ref)

        q2 = q_ref[...]
        do2 = do_ref[...]
        head0 = _lane_is_head0()
        zero = jnp.zeros_like(q2)
        qh_ref[0] = jnp.where(head0, q2, zero)
        qh_ref[1] = jnp.where(head0, zero, q2)
        doh_ref[0] = jnp.where(head0, do2, zero)
        doh_ref[1] = jnp.where(head0, zero, do2)
        scale = jnp.asarray(QK_SCALE, BF16)
        qs_ref[...] = q2 * scale
        tri_s = _tri(True)
        tri_i = _tri(False)
        prod = do2.astype(F32) * o_ref[...]
        tot_ref[0] = jnp.sum(jnp.where(head0, prod, 0.0), axis=1, keepdims=True)
        tot_ref[1] = jnp.sum(jnp.where(head0, 0.0, prod), axis=1, keepdims=True)
        dqa_ref[...] = jnp.zeros_like(dqa_ref)
        car_ref[...] = jnp.zeros_like(car_ref)
        carr_ref[...] = jnp.zeros_like(carr_ref)

        def streams_of(kb, d, strips):
            keys = pl.ds(pl.multiple_of(kb * B_TS, B_TS), B_TS)
            k2 = k_ref[keys, :]
            return _sb_streams(d, strips, keys=keys, k2=k2, v2=v_ref[keys, :], k2s=k2 * scale)

        def run(streams):
            def scores(st):
                st["z2"] = _dot_nt(qh_ref[st["h"], st["rows"], :], st.pop("k2")) * (QK_SCALE * LOG2E)
                st["dw"] = _dot_nt(doh_ref[st["h"], st["rows"], :], st.pop("v2"))

            def logs(st):
                _sb_logs(st, st.pop("z2"))

            def suffix(st):
                st["suffix"] = _dot(st.pop("keep_bf"), tri_s)

            def weights(st):
                h = st["h"]
                suffix, dw = st.pop("suffix"), st.pop("dw")
                wb, dlog, hi, lo = [], [], [], []
                for p in range(0, B_TS, B_PIECE):
                    rows = _piece_rows(st, p)
                    car = car_ref[h, rows, :]
                    w = jnp.exp2(st["log_beta"][p:p + B_PIECE] + suffix[p:p + B_PIECE] + car)
                    if st["diag"]:
                        w = jnp.where(_strict_lower(p), w, 0.0)
                    w = w.astype(BF16)
                    dl = w.astype(F32) * dw[p:p + B_PIECE]
                    dl_hi, dl_lo = _split_bf16(dl)
                    wb.append(w)
                    dlog.append(dl)
                    hi.append(dl_hi)
                    lo.append(dl_lo)
                    car_ref[h, rows, :] = car + jnp.sum(st["log_keep"][p:p + B_PIECE], axis=1, keepdims=True)
                st["wb"], st["dlog"], st["hi"], st["lo"] = _rows_cat(wb), _rows_cat(dlog), _rows_cat(hi), _rows_cat(lo)

            def later(st):
                st["later"] = _dot(st.pop("hi"), tri_i) + _dot(st.pop("lo"), tri_i)

            def dscores(st):
                h = st["h"]
                later, dlog = st.pop("later"), st.pop("dlog")
                log_keep, log_beta = st.pop("log_keep"), st.pop("log_beta")
                dzb = []
                for p in range(0, B_TS, B_PIECE):
                    rows = _piece_rows(st, p)
                    pc = slice(p, p + B_PIECE)
                    carr = carr_ref[h, rows, :]
                    earlier = tot_ref[h, rows, :] - (later[pc] + carr)
                    dz = dlog[pc] * jnp.exp2(log_keep[pc]) - jnp.exp2(log_beta[pc]) * earlier
                    if st["diag"]:
                        dz = jnp.where(_strict_lower(p), dz, 0.0)
                    dzb.append(dz.astype(BF16))
                    carr_ref[h, rows, :] = carr + jnp.sum(dlog[pc], axis=1, keepdims=True)
                st["dzb"] = _rows_cat(dzb)

            def grads(st):
                h, rows, keys = st["h"], st["rows"], st["keys"]
                mine = head0 if h == 0 else jnp.logical_not(head0)
                dzb = st.pop("dzb")
                dqa_ref[h, rows, :] += _dot(dzb, st.pop("k2s"))
                dk_ref[keys, :] += jnp.where(mine, _dot_tn(dzb, qs_ref[rows, :]), 0.0)
                dv_ref[keys, :] += jnp.where(mine, _dot_tn(st.pop("wb"), do_ref[rows, :]), 0.0)

            _skewed(streams, [scores, logs, suffix, weights, later, dscores, grads])

        _sb_sweep(i, car_ref, streams_of, run)
        dq_ref[...] = jnp.where(head0, dqa_ref[0], dqa_ref[1]).astype(dq_ref.dtype)

        @pl.when(i == T // B_TQ - 1)
        def _():
            dko_ref[...] = dk_ref[...].astype(BF16)
            dvo_ref[...] = dv_ref[...].astype(BF16)

    return _call_carrying(
        job, body, name=name, grid=(n_hp, T // B_TQ),
        in_specs=[pl.BlockSpec((B_TQ, LANES), lambda hp, i: (i, hp + col0)),
                  pl.BlockSpec((T, LANES), lambda hp, i: (0, hp + col0 + n_hp)),
                  pl.BlockSpec((T, LANES), lambda hp, i: (0, hp + col0 + 2 * n_hp)),
                  pl.BlockSpec((B_TQ, LANES), lambda hp, i: (i, hp)),
                  pl.BlockSpec((B_TQ, LANES), lambda hp, i: (i, hp))],
        out_specs=[pl.BlockSpec((B_TQ, LANES), lambda hp, i: (i, hp)),
                   pl.BlockSpec((T, LANES), lambda hp, i: (0, hp)),
                   pl.BlockSpec((T, LANES), lambda hp, i: (0, hp))],
        out_shape=[jax.ShapeDtypeStruct((T, WIDTH), BF16),
                   jax.ShapeDtypeStruct((T, WIDTH), BF16),
                   jax.ShapeDtypeStruct((T, WIDTH), BF16)],
        scratch_shapes=[pltpu.VMEM((2, B_TQ, LANES), F32), pltpu.VMEM((2, B_TQ, 1), F32),
                        pltpu.VMEM((2, B_TQ, 1), F32), pltpu.VMEM((2, B_TQ, 1), F32),
                        pltpu.VMEM((2, B_TQ, LANES), BF16), pltpu.VMEM((2, B_TQ, LANES), BF16),
                        pltpu.VMEM((B_TQ, LANES), BF16),
                        pltpu.VMEM((T, LANES), F32), pltpu.VMEM((T, LANES), F32)],
        vmem_mib=56, args=(qkv, qkv, qkv, out, do))


def _gated_mix(oa_ref, ob_ref, g_ref, bg_ref, wpa_ref, wpb_ref, D):
    ya = _dot(oa_ref[...].astype(BF16), wpa_ref[...])
    yb = _dot(ob_ref[...].astype(BF16), wpb_ref[...])
    sa = jax.nn.sigmoid(g_ref[:, :D] + bg_ref[:, :D])
    sb = jax.nn.sigmoid(g_ref[:, D:] + bg_ref[:, D:])
    return ya, yb, sa, sb


def _proj_fwd(oa, ob, g, bg, wpa, wpb, wo, wl, xin, lng, lnb, layer, *, alpha, name):
    T, D = xin.shape
    tm = _tile(T, 512)
    row = lambda i: (i, 0)
    wspec = lambda r, c: pl.BlockSpec((None, r, c), lambda i: (wl, 0, 0))
    vec = lambda c: pl.BlockSpec((None, 1, c), lambda i: (layer, 0, 0))

    def body(oa_ref, ob_ref, g_ref, bg_ref, wpa_ref, wpb_ref, wo_ref, x_ref, lg_ref, lb_ref, x1_ref, r1_ref, x1b_ref):
        ya, yb, sa, sb = _gated_mix(oa_ref, ob_ref, g_ref, bg_ref, wpa_ref, wpb_ref, D)
        mix = _dot((sa * ya + sb * yb).astype(BF16), wo_ref[...])
        r1 = alpha * x_ref[...] + mix
        r1_ref[...] = r1
        x1 = _ln_fwd(r1, lg_ref[...], lb_ref[...])
        x1_ref[...] = x1
        x1b_ref[...] = x1.astype(BF16)

    return pl.pallas_call(
        body, name=name, grid=(T // tm,),
        in_specs=[pl.BlockSpec((tm, WIDTH), row), pl.BlockSpec((tm, WIDTH), row), pl.BlockSpec((tm, 2 * D), row),
                  vec(2 * D), wspec(WIDTH, D), wspec(WIDTH, D), wspec(D, D),
                  pl.BlockSpec((tm, D), row), vec(D), vec(D)],
        out_specs=[pl.BlockSpec((tm, D), row), pl.BlockSpec((tm, D), row), pl.BlockSpec((tm, D), row)],
        out_shape=[jax.ShapeDtypeStruct((T, D), F32), jax.ShapeDtypeStruct((T, D), F32),
                   jax.ShapeDtypeStruct((T, D), BF16)],
        compiler_params=_cparams(("arbitrary",), 56),
    )(oa, ob, g, bg, wpa, wpb, wo, xin, lng, lnb)


def _proj_bwd(dx1, r1, lng, oa, ob, g, bg, wpa, wpb, wo, wl, layer, *, name):
    T, D = dx1.shape
    tm = _tile(T, 512)
    row = lambda i: (i, 0)
    fixed = lambda i: (0, 0)
    wspec = lambda r, c: pl.BlockSpec((None, r, c), lambda i: (wl, 0, 0))
    vec = lambda c: pl.BlockSpec((None, 1, c), lambda i: (layer, 0, 0))

    def body(dx_ref, r1_ref, lg_ref, oa_ref, ob_ref, g_ref, bg_ref, wpa_ref, wpb_ref, wo_ref,
             dr_ref, mix_ref, dya_ref, dyb_ref, dg_ref, doa_ref, dob_ref, dlg_ref, dlb_ref, dbg_ref):
        @pl.when(pl.program_id(0) == 0)
        def _():
            dlg_ref[...] = jnp.zeros_like(dlg_ref)
            dlb_ref[...] = jnp.zeros_like(dlb_ref)
            dbg_ref[...] = jnp.zeros_like(dbg_ref)

        dx = dx_ref[...]
        dr, xhat = _ln_bwd(dx, r1_ref[...], lg_ref[...])
        dr_ref[...] = dr
        dlg_ref[...] += jnp.sum(dx * xhat, axis=0, keepdims=True)
        dlb_ref[...] += jnp.sum(dx, axis=0, keepdims=True)
        dmix = _dot_nt(dr.astype(BF16), wo_ref[...])
        ya, yb, sa, sb = _gated_mix(oa_ref, ob_ref, g_ref, bg_ref, wpa_ref, wpb_ref, D)
        mix_ref[...] = (sa * ya + sb * yb).astype(BF16)
        dya = (dmix * sa).astype(BF16)
        dyb = (dmix * sb).astype(BF16)
        dya_ref[...] = dya
        dyb_ref[...] = dyb
        dga = dmix * ya * (sa * (1.0 - sa))
        dgb = dmix * yb * (sb * (1.0 - sb))
        dg_ref[:, :D] = dga.astype(BF16)
        dg_ref[:, D:] = dgb.astype(BF16)
        dbg_ref[:, :D] += jnp.sum(dga, axis=0, keepdims=True)
        dbg_ref[:, D:] += jnp.sum(dgb, axis=0, keepdims=True)
        doa_ref[...] = _dot_nt(dya, wpa_ref[...]).astype(BF16)
        dob_ref[...] = _dot_nt(dyb, wpb_ref[...]).astype(BF16)

    return pl.pallas_call(
        body, name=name, grid=(T // tm,),
        in_specs=[pl.BlockSpec((tm, D), row), pl.BlockSpec((tm, D), row), vec(D),
                  pl.BlockSpec((tm, WIDTH), row), pl.BlockSpec((tm, WIDTH), row), pl.BlockSpec((tm, 2 * D), row),
                  vec(2 * D), wspec(WIDTH, D), wspec(WIDTH, D), wspec(D, D)],
        out_specs=[pl.BlockSpec((tm, D), row), pl.BlockSpec((tm, D), row), pl.BlockSpec((tm, D), row),
                   pl.BlockSpec((tm, D), row), pl.BlockSpec((tm, 2 * D), row),
                   pl.BlockSpec((tm, WIDTH), row), pl.BlockSpec((tm, WIDTH), row),
                   pl.BlockSpec((1, D), fixed), pl.BlockSpec((1, D), fixed), pl.BlockSpec((1, 2 * D), fixed)],
        out_shape=[jax.ShapeDtypeStruct((T, D), F32), jax.ShapeDtypeStruct((T, D), BF16),
                   jax.ShapeDtypeStruct((T, D), BF16), jax.ShapeDtypeStruct((T, D), BF16),
                   jax.ShapeDtypeStruct((T, 2 * D), BF16),
                   jax.ShapeDtypeStruct((T, WIDTH), BF16), jax.ShapeDtypeStruct((T, WIDTH), BF16),
                   jax.ShapeDtypeStruct((1, D), F32), jax.ShapeDtypeStruct((1, D), F32),
                   jax.ShapeDtypeStruct((1, 2 * D), F32)],
        compiler_params=_cparams(("arbitrary",), 56),
    )(dx1, r1, lng, oa, ob, g, bg, wpa, wpb, wo)


def _ffn_fwd(x1, wfi, wfo, wl, lng, lnb, layer, *, alpha, name, job=None):
    T, D = x1.shape
    tf = wfi.shape[-1]
    nj = wfi.shape[0] // 2
    tm = _tile(T, 1024)
    vec = lambda c: pl.BlockSpec((None, 1, c), lambda i, j: (layer, 0, 0))

    def body(x_ref, wg_ref, wu_ref, wo_ref, lg_ref, lb_ref, gs_ref, us_ref, r2_ref, x2_ref, acc_ref, xb_ref):
        j = pl.program_id(1)

        @pl.when(j == 0)
        def _():
            xb_ref[...] = x_ref[...].astype(BF16)
            acc_ref[...] = jnp.zeros_like(acc_ref)

        gv = _dot(xb_ref[...], wg_ref[...])
        uv = _dot(xb_ref[...], wu_ref[...])
        gs_ref[...] = gv.astype(gs_ref.dtype)
        us_ref[...] = uv.astype(us_ref.dtype)
        act = gv * jax.nn.sigmoid(gv) * uv
        acc_ref[...] += _dot(act.astype(BF16), wo_ref[...])

        @pl.when(j == nj - 1)
        def _():
            r2 = alpha * x_ref[...] + acc_ref[...]
            r2_ref[...] = r2
            x2_ref[...] = _ln_fwd(r2, lg_ref[...], lb_ref[...])

    return _call_carrying(
        job, body, name=name, grid=(T // tm, nj),
        in_specs=[pl.BlockSpec((tm, D), lambda i, j: (i, 0)),
                  pl.BlockSpec((None, None, D, tf), lambda i, j: (j, wl, 0, 0)),
                  pl.BlockSpec((None, None, D, tf), lambda i, j: (j + nj, wl, 0, 0)),
                  pl.BlockSpec((None, tf, D), lambda i, j: (wl, j, 0)),
                  vec(D), vec(D)],
        out_specs=[pl.BlockSpec((None, tm, tf), lambda i, j: (j, i, 0)),
                   pl.BlockSpec((None, tm, tf), lambda i, j: (j, i, 0)),
                   pl.BlockSpec((tm, D), lambda i, j: (i, 0)),
                   pl.BlockSpec((tm, D), lambda i, j: (i, 0))],
        out_shape=[jax.ShapeDtypeStruct((nj, T, tf), BF16), jax.ShapeDtypeStruct((nj, T, tf), BF16),
                   jax.ShapeDtypeStruct((T, D), F32), jax.ShapeDtypeStruct((T, D), F32)],
        scratch_shapes=[pltpu.VMEM((tm, D), F32), pltpu.VMEM((tm, D), BF16)],
        vmem_mib=56, args=(x1, wfi, wfi, wfo, lng, lnb))


def _ffn_bwd(dx2, r2, lng, gs, us, wfi, wfo, wl, layer, *, alpha, name, job=None):
    T, D = dx2.shape
    tf = wfi.shape[-1]
    nj = wfi.shape[0] // 2
    tm = _tile(T, 512)
    vec = lambda c: pl.BlockSpec((None, 1, c), lambda i, j: (layer, 0, 0))
    blk = lambda: pl.BlockSpec((None, tm, tf), lambda i, j: (j, i, 0))

    def body(dx_ref, r2_ref, lg_ref, gs_ref, us_ref, wg_ref, wu_ref, wo_ref,
             dr_ref, act_ref, dg_ref, du_ref, dx1_ref, dlg_ref, dlb_ref, acc_ref, drb_ref):
        i = pl.program_id(0)
        j = pl.program_id(1)

        @pl.when((i == 0) & (j == 0))
        def _():
            dlg_ref[...] = jnp.zeros_like(dlg_ref)
            dlb_ref[...] = jnp.zeros_like(dlb_ref)

        @pl.when(j == 0)
        def _():
            dx = dx_ref[...]
            dr, xhat = _ln_bwd(dx, r2_ref[...], lg_ref[...])
            dlg_ref[...] += jnp.sum(dx * xhat, axis=0, keepdims=True)
            dlb_ref[...] += jnp.sum(dx, axis=0, keepdims=True)
            drb_ref[...] = dr.astype(BF16)
            dr_ref[...] = dr.astype(BF16)
            acc_ref[...] = alpha * dr

        dact = _dot_nt(drb_ref[...], wo_ref[...])
        gv = gs_ref[...].astype(F32)
        uv = us_ref[...].astype(F32)
        s = jax.nn.sigmoid(gv)
        silu = gv * s
        act_ref[...] = (silu * uv).astype(BF16)
        dg = (dact * uv * (s * (1.0 + gv * (1.0 - s)))).astype(BF16)
        du = (dact * silu).astype(BF16)
        dg_ref[...] = dg
        du_ref[...] = du
        acc_ref[...] += _dot_nt(dg, wg_ref[...]) + _dot_nt(du, wu_ref[...])

        @pl.when(j == nj - 1)
        def _():
            dx1_ref[...] = acc_ref[...]

    return _call_carrying(
        job, body, name=name, grid=(T // tm, nj),
        in_specs=[pl.BlockSpec((tm, D), lambda i, j: (i, 0)), pl.BlockSpec((tm, D), lambda i, j: (i, 0)), vec(D),
                  blk(), blk(),
                  pl.BlockSpec((None, None, D, tf), lambda i, j: (j, wl, 0, 0)),
                  pl.BlockSpec((None, None, D, tf), lambda i, j: (j + nj, wl, 0, 0)),
                  pl.BlockSpec((None, tf, D), lambda i, j: (wl, j, 0))],
        out_specs=[pl.BlockSpec((tm, D), lambda i, j: (i, 0)), blk(), blk(), blk(),
                   pl.BlockSpec((tm, D), lambda i, j: (i, 0)),
                   pl.BlockSpec((1, D), lambda i, j: (0, 0)), pl.BlockSpec((1, D), lambda i, j: (0, 0))],
        out_shape=[jax.ShapeDtypeStruct((T, D), BF16),
                   jax.ShapeDtypeStruct((nj, T, tf), BF16), jax.ShapeDtypeStruct((nj, T, tf), BF16),
                   jax.ShapeDtypeStruct((nj, T, tf), BF16),
                   jax.ShapeDtypeStruct((T, D), F32),
                   jax.ShapeDtypeStruct((1, D), F32), jax.ShapeDtypeStruct((1, D), F32)],
        scratch_shapes=[pltpu.VMEM((tm, D), F32), pltpu.VMEM((tm, D), BF16)],
        vmem_mib=56, args=(dx2, r2, lng, gs, us, wfi, wfi, wfo))


def _loss_head(y, target, *, name):
    T, D = y.shape
    tm = _tile(T, 1024)

    def body(y_ref, t_ref, dy_ref, sq_ref):
        @pl.when(pl.program_id(0) == 0)
        def _():
            sq_ref[...] = jnp.zeros_like(sq_ref)
        err = y_ref[...] - t_ref[...]
        dy_ref[...] = err * (1.0 / D)
        sq_ref[...] += jnp.sum(err * err, axis=0, keepdims=True)

    return pl.pallas_call(
        body, name=name, grid=(T // tm,),
        in_specs=[pl.BlockSpec((tm, D), lambda i: (i, 0)), pl.BlockSpec((tm, D), lambda i: (i, 0))],
        out_specs=[pl.BlockSpec((tm, D), lambda i: (i, 0)), pl.BlockSpec((1, D), lambda i: (0, 0))],
        out_shape=[jax.ShapeDtypeStruct((T, D), F32), jax.ShapeDtypeStruct((1, D), F32)],
        compiler_params=_cparams(("arbitrary",)),
    )(y, target)


def _my_place():
    return lax.axis_index("x"), lax.axis_index("y"), lax.axis_index("c")


def _peer(place, k):
    x, y, c = place
    return (1 - x if k & 4 else x, 1 - y if k & 2 else y, 1 - c if k & 1 else c)


def _logical(place):
    x, y, c = place
    return 4 * x + 2 * y + c


def _block_of(ref, mode, idx):
    if mode == "blk":
        return ref.at[idx]
    if mode == "col":
        size = ref.shape[2] // N_DEV
        return ref.at[:, :, pl.ds(pl.multiple_of(idx * size, size), size)]
    size = ref.shape[1] // N_DEV
    return ref.at[:, pl.ds(pl.multiple_of(idx * size, size), size), :]


def _full_shape(shard, mode):
    if mode == "blk":
        return (N_DEV,) + shard.shape
    if mode == "col":
        return shard.shape[:2] + (N_DEV * shard.shape[2],)
    return (shard.shape[0], N_DEV * shard.shape[1], shard.shape[2])


class _Exchange:
    def __init__(self, arrays, out_shape, build):
        self.arrays = list(arrays)
        self.out_shape = list(out_shape)
        self.build = build

    def scratch(self):
        n = len(self.arrays)
        return [pltpu.SemaphoreType.DMA((n * N_DEV,)), pltpu.SemaphoreType.DMA((n * N_DEV,)),
                pltpu.SemaphoreType.DMA((n,))]

    def start(self, ins, outs, sems):
        for cp in self.build(ins, outs, *sems):
            cp.start()

    def wait(self, ins, outs, sems):
        for cp in self.build(ins, outs, *sems):
            cp.wait()

    def run(self, name):
        n_in, n_out = len(self.arrays), len(self.out_shape)
        hbm = pl.BlockSpec(memory_space=pltpu.HBM)

        def body(*refs):
            ins, outs, sems = refs[:n_in], refs[n_in:n_in + n_out], refs[n_in + n_out:]
            self.start(ins, outs, sems)
            self.wait(ins, outs, sems)

        return pl.pallas_call(
            body, name=name, in_specs=[hbm] * n_in, out_specs=[hbm] * n_out,
            out_shape=self.out_shape, scratch_shapes=self.scratch(),
        )(*self.arrays)


def _copies_to_all(src_of, dst_of, n, send, recv, local):
    me = _my_place()
    copies = []
    for a in range(n):
        copies.append(pltpu.make_async_copy(src_of(a, _logical(me)), dst_of(a), local.at[a]))
        for k in range(1, N_DEV):
            peer = _peer(me, k)
            copies.append(pltpu.make_async_remote_copy(
                src_ref=src_of(a, _logical(peer)), dst_ref=dst_of(a),
                send_sem=send.at[a * N_DEV + k], recv_sem=recv.at[a * N_DEV + k],
                device_id=peer, device_id_type=MESH))
    return copies


def _gather_job(shards, modes):
    def build(ins, outs, send, recv, local):
        my_id = _logical(_my_place())
        return _copies_to_all(lambda a, dev: ins[a], lambda a: _block_of(outs[a], modes[a], my_id),
                              len(shards), send, recv, local)

    return _Exchange(shards, [jax.ShapeDtypeStruct(_full_shape(s, m), s.dtype) for s, m in zip(shards, modes)], build)


def _gather_via_sibling(shard, mode, *, name):
    hbm = pl.BlockSpec(memory_space=pltpu.HBM)

    def body(x_ref, o_ref, send, recv, local):
        x, y, c = _my_place()
        me, sibling = (x, y, c), (x, y, 1 - c)
        chips = [(1 - x, y), (x, 1 - y), (1 - x, 1 - y)]

        def copy(k, block, to, src=None):
            dst = _block_of(o_ref, mode, _logical(block))
            return pltpu.make_async_remote_copy(src_ref=dst if src is None else src, dst_ref=dst,
                                                send_sem=send.at[k], recv_sem=recv.at[k],
                                                device_id=to, device_id_type=MESH)

        mine = pltpu.make_async_copy(x_ref, _block_of(o_ref, mode, _logical(me)), local)
        mine.start()
        first = [copy(0, me, sibling, src=x_ref)]
        first += [copy(1 + j, me, (*chip, c), src=x_ref) for j, chip in enumerate(chips)]
        for cp in first:
            cp.start()
        passed = [copy(4 + j, (*chip, c), sibling) for j, chip in enumerate(chips)]
        for j, chip in enumerate(chips):
            copy(1 + j, (*chip, c), me).wait_recv()
            passed[j].start()
        copy(0, sibling, me).wait_recv()
        for j, chip in enumerate(chips):
            copy(4 + j, (*chip, 1 - c), me).wait_recv()
        for cp in first + passed:
            cp.wait_send()
        mine.wait()

    return pl.pallas_call(
        body, name=name, in_specs=[hbm], out_specs=hbm,
        out_shape=jax.ShapeDtypeStruct(_full_shape(shard, mode), shard.dtype),
        scratch_shapes=[pltpu.SemaphoreType.DMA((7,)), pltpu.SemaphoreType.DMA((7,)), pltpu.SemaphoreType.DMA],
    )(shard)


def _grad_block(ref, mode, idx):
    if mode == "blk":
        return ref.at[idx]
    if mode == "col":
        size = ref.shape[1] // N_DEV
        return ref.at[:, pl.ds(pl.multiple_of(idx * size, size), size)]
    size = ref.shape[0] // N_DEV
    return ref.at[pl.ds(pl.multiple_of(idx * size, size), size), :]


def _grad_shard_shape(g, mode):
    if mode == "blk":
        return g.shape[1:]
    if mode == "col":
        return (g.shape[0], g.shape[1] // N_DEV)
    return (g.shape[0] // N_DEV, g.shape[1])


def _grads_job(groups, modes):
    flat = [(g, w, l) for w, per_w in enumerate(groups) for l, g in enumerate(per_w)]

    def build(ins, outs, send, recv, local):
        my_id = _logical(_my_place())
        return _copies_to_all(lambda a, dev: _grad_block(ins[a], modes[flat[a][1]], dev),
                              lambda a: outs[flat[a][1]].at[my_id, flat[a][2]],
                              len(flat), send, recv, local)

    out_shape = [jax.ShapeDtypeStruct((N_DEV, len(per_w)) + _grad_shard_shape(per_w[0], m), per_w[0].dtype)
                 for per_w, m in zip(groups, modes)]
    return _Exchange([g for g, _, _ in flat], out_shape, build)


def _adamw(w, g, m, v):
    m = ADAM_B1 * m + (1.0 - ADAM_B1) * g
    v = ADAM_B2 * v + (1.0 - ADAM_B2) * (g * g)
    m_hat = m / (1.0 - ADAM_B1 ** ADAM_STEP)
    v_hat = v / (1.0 - ADAM_B2 ** ADAM_STEP)
    delta = -ADAM_LR * (m_hat / (jnp.sqrt(v_hat) + ADAM_EPS) + ADAM_WD * w)
    return delta, m, v


def _sum_slots_adamw(slots, w, m, v, *, name):
    n_l = len(slots)
    R, C = slots[0].shape[1:]
    tr = _tile(R, 256)
    n_r = R // tr

    def body(*refs):
        s_refs = refs[:n_l]
        w_ref, m_ref, v_ref, g_out, d_out, m_out, v_out = refs[n_l:]
        for layer in range(n_l):
            @pl.when(pl.program_id(0) == layer)
            def _(s_ref=s_refs[layer]):
                g = s_ref[0].astype(F32)
                for s in range(1, N_DEV):
                    g = g + s_ref[s].astype(F32)
                delta, m_new, v_new = _adamw(w_ref[...], g, m_ref[...], v_ref[...])
                g_out[...] = g
                d_out[...] = delta
                m_out[...] = m_new
                v_out[...] = v_new

    slot_spec = lambda layer: pl.BlockSpec((N_DEV, tr, C), lambda l, i: (0, jnp.where(l == layer, i, 0), 0))
    spec = pl.BlockSpec((tr, C), lambda l, i: (l * n_r + i, 0))
    return pl.pallas_call(
        body, name=name, grid=(n_l, n_r),
        in_specs=[slot_spec(layer) for layer in range(n_l)] + [spec, spec, spec],
        out_specs=[spec] * 4,
        out_shape=[jax.ShapeDtypeStruct((n_l * R, C), F32)] * 4,
        compiler_params=_cparams(("arbitrary", "arbitrary")),
    )(*slots, w, m, v)


def _small_allreduce_adamw(g, w, m, v, *, name):
    R = g.shape[0]
    vmem = pl.BlockSpec(memory_space=pltpu.VMEM)

    def body(g_ref, w_ref, m_ref, v_ref, g_out, d_out, m_out, v_out, slots, send, recv):
        me = _my_place()
        my_id = _logical(me)
        slots[my_id] = g_ref[...]
        copies = []
        for k in range(1, N_DEV):
            cp = pltpu.make_async_remote_copy(
                src_ref=g_ref, dst_ref=slots.at[my_id], send_sem=send.at[k], recv_sem=recv.at[k],
                device_id=_peer(me, k), device_id_type=MESH)
            cp.start()
            copies.append(cp)
        for cp in copies:
            cp.wait()
        total = slots[0]
        for s in range(1, N_DEV):
            total = total + slots[s]
        delta, m_new, v_new = _adamw(w_ref[...], total, m_ref[...], v_ref[...])
        g_out[...] = total
        d_out[...] = delta
        m_out[...] = m_new
        v_out[...] = v_new

    return pl.pallas_call(
        body, name=name,
        in_specs=[vmem] * 4, out_specs=[vmem] * 4,
        out_shape=[jax.ShapeDtypeStruct((R, LANES), F32)] * 4,
        scratch_shapes=[pltpu.VMEM((N_DEV, R, LANES), F32),
                        pltpu.SemaphoreType.DMA((N_DEV,)), pltpu.SemaphoreType.DMA((N_DEV,))],
    )(g, w, m, v)


def _pack(parts):
    flat = jnp.concatenate([p.reshape(-1) for p in parts])
    rows = -(-flat.shape[0] // (8 * LANES)) * 8
    return jnp.pad(flat, (0, rows * LANES - flat.shape[0])).reshape(rows, LANES)


def _unpack(packed, like):
    flat = packed.reshape(-1)
    out, pos = [], 0
    for p in like:
        out.append(flat[pos:pos + p.size].reshape(p.shape))
        pos += p.size
    return out


def kernel(x, w_in, b_gate, rel_bias, w_proj_a, w_proj_b, w_out, ln1_g, ln1_b, w_ffn_in, w_ffn_out, ln2_g, ln2_b, loss_target, m_w_in, m_b_gate, m_rel_bias, m_w_proj_a, m_w_proj_b, m_w_out, m_ln1_g, m_ln1_b, m_w_ffn_in, m_w_ffn_out, m_ln2_g, m_ln2_b, v_w_in, v_b_gate, v_rel_bias, v_w_proj_a, v_w_proj_b, v_w_out, v_ln1_g, v_ln1_b, v_w_ffn_in, v_w_ffn_out, v_ln2_g, v_ln2_b):
    L = w_in.shape[0]
    T, D = x.shape[1], x.shape[2]
    alpha = float((2 * L) ** 0.25)
    n_qkv = 6 * WIDTH

    big = [w_in, w_proj_a, w_proj_b, w_out, w_ffn_in, w_ffn_out]
    kinds = ["in", "pa", "pb", "o", "fi", "fo"]
    modes = ["col", "col", "col", "row", "blk", "row"]
    mode_of = dict(zip(kinds, modes))
    w_bf = dict(zip(kinds, [w.astype(BF16) for w in big]))

    def gather_of(ks, l):
        return _gather_job([w_bf[k][l:l + 1] for k in ks], [mode_of[k] for k in ks])

    W = [dict() for _ in range(L)]
    W[0]["in"] = _gather_via_sibling(w_bf["in"][:1], mode_of["in"], name="gather_w_in_first")
    vec3 = lambda a: a[:, None, :]
    bg3, l1g, l1b, l2g, l2b = vec3(b_gate), vec3(ln1_g), vec3(ln1_b), vec3(ln2_g), vec3(ln2_b)
    b_col0 = 3 * WIDTH // LANES

    h = x[0]
    saved = []
    for l in range(L):
        ahead = l + 1 < L
        soon = ["pa", "pb", "o", "fo"]
        (qkv, gates), got = _in_proj(h, W[l]["in"], 0, n_qkv=n_qkv, name=f"in_proj_{l}",
                                     job=gather_of(soon, 0) if l == 0 else None)
        W[l].update(zip(soon, got))
        kvpad = jnp.pad(qkv[:, WIDTH:3 * WIDTH], ((A_WIN - A_TQ, 0), (0, 0)))
        bias = _toeplitz_bias(rel_bias[l])
        oa, got = _attn_a_fwd(qkv, kvpad, bias, name=f"attn_a_fwd_{l}", job=gather_of(["fi"], 0) if l == 0 else None)
        W[l].update(zip(["fi"], got))
        early = ["in", "pa", "pb", "o"]
        ob, got = _attn_b_fwd(qkv, col0=b_col0, name=f"attn_b_fwd_{l}", job=gather_of(early, l + 1) if ahead else None)
        W[l + 1 if ahead else l].update(zip(early, got))
        x1, r1, x1b = _proj_fwd(oa, ob, gates, bg3, W[l]["pa"], W[l]["pb"], W[l]["o"], 0, h, l1g, l1b, l,
                           alpha=alpha, name=f"proj_fwd_{l}")
        (gs, us, r2, x2), got = _ffn_fwd(x1, W[l]["fi"], W[l]["fo"], 0, l2g, l2b, l, alpha=alpha, name=f"ffn_fwd_{l}",
                                         job=gather_of(["fi", "fo"], l + 1) if ahead else None)
        W[l + 1 if ahead else l].update(zip(["fi", "fo"], got))
        saved.append((h, qkv, gates, kvpad, bias, oa, ob, x1b, r1, gs, us, r2))
        h = x2

    d_h, sq = _loss_head(h, loss_target[0], name="loss_head")
    loss = lax.psum((0.5 / D) * jnp.sum(sq), ("x", "y", "c"))

    g_bg, g_rb, g_l1g, g_l1b, g_l2g, g_l2b = ([None] * L for _ in range(6))
    slot = {k: [None] * L for k in kinds}

    def exchange_of(ks, grads):
        return _grads_job([[g] for g in grads], [mode_of[k] for k in ks])

    w_in_above = None
    for l in reversed(range(L)):
        xin, qkv, gates, kvpad, bias, oa, ob, x1b, r1, gs, us, r2 = saved[l]
        (dr2, act, dgt, dup, dx1, g_l2g[l], g_l2b[l]), got = _ffn_bwd(
            d_h, r2, l2g, gs, us, W[l]["fi"], W[l]["fo"], 0, l, alpha=alpha, name=f"ffn_bwd_{l}", job=w_in_above)
        if w_in_above is not None:
            (slot["in"][l + 1],) = got
        g_fo = _mm_tn_blocked_a(act, dr2, name=f"grad_w_ffn_out_{l}").reshape(-1, D)
        g_fi = jnp.concatenate(_mm_tn_blocked_pair(x1b, dgt, dup, name=f"grad_w_ffn_in_{l}"), axis=0)
        (dr1, mixin, dya, dyb, dgates, doa, dob, g_l1g[l], g_l1b[l], g_bg[l]) = _proj_bwd(
            dx1, r1, l1g, oa, ob, gates, bg3, W[l]["pa"], W[l]["pb"], W[l]["o"], 0, l, name=f"proj_bwd_{l}")
        g_o = _mm_tn(mixin, dr1, tm=_tile(D, 1024), tn=_tile(D, 1024), name=f"grad_w_out_{l}")
        g_pa = _mm_tn(oa, dya, tm=WIDTH, tn=_tile(D, 1024), name=f"grad_w_proj_a_{l}")
        g_pb = _mm_tn(ob, dyb, tm=WIDTH, tn=_tile(D, 1024), name=f"grad_w_proj_b_{l}")
        (dqa, dka, dva, dbias), (slot["fi"][l], slot["fo"][l]) = _attn_a_bwd(
            qkv, kvpad, bias, doa, name=f"attn_a_bwd_{l}", job=exchange_of(["fi", "fo"], [g_fi, g_fo]))
        g_rb[l] = _toeplitz_bias_grad(dbias)
        (dqb, dkb, dvb), (slot["pa"][l], slot["pb"][l], slot["o"][l]) = _attn_b_bwd(
            qkv, ob, dob, col0=b_col0, name=f"attn_b_bwd_{l}", job=exchange_of(["pa", "pb", "o"], [g_pa, g_pb, g_o]))
        d_pre = jnp.concatenate([dqa, dka, dva, dqb, dkb, dvb, dgates], axis=1)
        g_in = _mm_tn(xin, d_pre, tm=D, tn=4 * w_in.shape[2], name=f"grad_w_in_{l}")
        w_in_above = exchange_of(["in"], [g_in])
        d_h, got = _mm_nt_add(d_pre, W[l]["in"], 0, dr1, alpha, name=f"grad_x_{l}", job=w_in_above if l == 0 else None)
        if l == 0:
            (slot["in"][0],) = got
    grad_x = d_h[None]

    moments_m = [m_w_in, m_w_proj_a, m_w_proj_b, m_w_out, m_w_ffn_in, m_w_ffn_out]
    moments_v = [v_w_in, v_w_proj_a, v_w_proj_b, v_w_out, v_w_ffn_in, v_w_ffn_out]
    names = ["w_in", "w_proj_a", "w_proj_b", "w_out", "w_ffn_in", "w_ffn_out"]
    big_out = {}
    for nm, k, w, m, v in zip(names, kinds, big, moments_m, moments_v):
        two = lambda a: a.reshape(-1, a.shape[-1])
        per_layer = [s.reshape(N_DEV, -1, s.shape[-1]) for s in slot[k]]
        res = _sum_slots_adamw(per_layer, two(w), two(m), two(v), name=f"adamw_{nm}")
        big_out[nm] = [r.reshape(w.shape) for r in res]

    small_w = [b_gate, rel_bias, ln1_g, ln1_b, ln2_g, ln2_b]
    small_g = [jnp.stack(g) for g in (g_bg, g_rb, g_l1g, g_l1b, g_l2g, g_l2b)]
    small_m = [m_b_gate, m_rel_bias, m_ln1_g, m_ln1_b, m_ln2_g, m_ln2_b]
    small_v = [v_b_gate, v_rel_bias, v_ln1_g, v_ln1_b, v_ln2_g, v_ln2_b]
    res = _small_allreduce_adamw(_pack(small_g), _pack(small_w), _pack(small_m), _pack(small_v),
                                 name="allreduce_small_adamw")
    small_names = ["b_gate", "rel_bias", "ln1_g", "ln1_b", "ln2_g", "ln2_b"]
    small_out = {nm: [] for nm in small_names}
    for packed in res:
        for nm, arr in zip(small_names, _unpack(packed, small_w)):
            small_out[nm].append(arr)

    order = ["w_in", "b_gate", "rel_bias", "w_proj_a", "w_proj_b", "w_out", "ln1_g", "ln1_b",
             "w_ffn_in", "w_ffn_out", "ln2_g", "ln2_b"]
    every = {**big_out, **small_out}
    outs = [loss, grad_x]
    for kind in range(4):
        outs += [every[nm][kind] for nm in order]
    return tuple(outs)
```

```python
import functools
import math

import jax
import jax.numpy as jnp
import numpy as np
from jax import lax
from jax.experimental import pallas as pl
from jax.experimental.pallas import tpu as pltpu

F32 = jnp.float32
BF16 = jnp.bfloat16

HEAD_DIM = 64
CHUNK = 64
LEFT_CHUNKS = 8
REL_CLIP = 256
N_REL = 2 * REL_CLIP + 1
WIDTH = 512
LANES = 128
A_TQ = 256
A_WIN = A_TQ + LEFT_CHUNKS * CHUNK
A_STRIP = 128
B_TQ = 512
B_TS = 256
B_PIECE = 64
B_DEAD = -160.0
LN_EPS = 1e-5
QK_SCALE = 1.0 / math.sqrt(HEAD_DIM)
LOG2E = 1.4426950408889634
NEG = -1e30

ADAM_LR = 0.001
ADAM_B1 = 0.9
ADAM_B2 = 0.999
ADAM_EPS = 1e-08
ADAM_WD = 0.01
ADAM_STEP = 10

N_DEV = 8
MESH = pl.DeviceIdType.MESH
MIB = 1024 * 1024


def _cparams(sem=None, vmem_mib=48):
    return pltpu.CompilerParams(dimension_semantics=sem, vmem_limit_bytes=vmem_mib * MIB)


def _dot(a, b):
    return jnp.dot(a, b, preferred_element_type=F32)


def _dot_nt(a, b):
    return lax.dot_general(a, b, (((1,), (1,)), ((), ())), preferred_element_type=F32)


def _dot_tn(a, b):
    return lax.dot_general(a, b, (((0,), (0,)), ((), ())), preferred_element_type=F32)


def _tile(n, pref):
    if n <= pref:
        return n
    for t in range(pref - pref % 8, 0, -8):
        if n % t == 0:
            return t
    raise ValueError((n, pref))


def _in_proj(a, w, layer, *, n_qkv, name, job=None):
    M, K = a.shape
    N = w.shape[2]
    tm = _tile(M, 1024)
    tn = 1024
    assert n_qkv % tn == 0 and (N - n_qkv) % tn == 0
    n_q = n_qkv // tn

    def body(a_ref, w_ref, q_ref, g_ref, ab_ref):
        j = pl.program_id(1)

        @pl.when(j == 0)
        def _():
            ab_ref[...] = a_ref[...].astype(BF16)

        res = _dot(ab_ref[...], w_ref[...])

        @pl.when(j < n_q)
        def _():
            q_ref[...] = res.astype(BF16)

        @pl.when(j >= n_q)
        def _():
            g_ref[...] = res

    return _call_carrying(
        job, body, name=name, grid=(M // tm, N // tn),
        in_specs=[pl.BlockSpec((tm, K), lambda i, j: (i, 0)),
                  pl.BlockSpec((None, K, tn), lambda i, j: (layer, 0, j))],
        out_specs=[pl.BlockSpec((tm, tn), lambda i, j: (i, jnp.minimum(j, n_q - 1))),
                   pl.BlockSpec((tm, tn), lambda i, j: (i, jnp.maximum(j - n_q, 0)))],
        out_shape=[jax.ShapeDtypeStruct((M, n_qkv), BF16), jax.ShapeDtypeStruct((M, N - n_qkv), F32)],
        scratch_shapes=[pltpu.VMEM((tm, K), BF16)], vmem_mib=48, args=(a, w))


def _mm_nt_add(a, w, layer, add, add_scale, *, name, job=None):
    M, K = a.shape
    N = w.shape[1]
    tm = _tile(M, 1024)
    tk = _tile(K, 1024)

    def body(a_ref, w_ref, add_ref, o_ref):
        @pl.when(pl.program_id(1) == 0)
        def _():
            o_ref[...] = add_scale * add_ref[...]
        o_ref[...] += _dot_nt(a_ref[...], w_ref[...])

    (out,), rode = _call_carrying(
        job, body, name=name, grid=(M // tm, K // tk),
        in_specs=[pl.BlockSpec((tm, tk), lambda i, k: (i, k)),
                  pl.BlockSpec((None, N, tk), lambda i, k: (layer, 0, k)),
                  pl.BlockSpec((tm, N), lambda i, k: (i, 0))],
        out_specs=[pl.BlockSpec((tm, N), lambda i, k: (i, 0))],
        out_shape=[jax.ShapeDtypeStruct((M, N), F32)],
        scratch_shapes=[], vmem_mib=48, args=(a, w, add))
    return out, rode


def _tn_body(k_axis, n_k):
    def body(a_ref, b_ref, o_ref, acc_ref):
        k = pl.program_id(k_axis)

        @pl.when(k == 0)
        def _():
            acc_ref[...] = jnp.zeros_like(acc_ref)
        acc_ref[...] += _dot_tn(a_ref[...].astype(BF16), b_ref[...].astype(BF16))

        @pl.when(k == n_k - 1)
        def _():
            o_ref[...] = acc_ref[...].astype(o_ref.dtype)
    return body


def _mm_tn(a, b, *, tm, tn, name):
    T, M = a.shape
    N = b.shape[1]
    tk = _tile(T, 1024)
    return pl.pallas_call(
        _tn_body(2, T // tk), name=name, grid=(M // tm, N // tn, T // tk),
        in_specs=[pl.BlockSpec((tk, tm), lambda i, j, k: (k, i)),
                  pl.BlockSpec((tk, tn), lambda i, j, k: (k, j))],
        out_specs=pl.BlockSpec((tm, tn), lambda i, j, k: (i, j)),
        out_shape=jax.ShapeDtypeStruct((M, N), BF16),
        scratch_shapes=[pltpu.VMEM((tm, tn), F32)],
        compiler_params=_cparams(("parallel", "parallel", "arbitrary")),
    )(a, b)


def _mm_tn_blocked_pair(a, b1, b2, *, name):
    T, M = a.shape
    S, _, N = b1.shape
    tk = _tile(T, 1024)
    n_k = T // tk

    def body(a_ref, b1_ref, b2_ref, o1_ref, o2_ref, acc1_ref, acc2_ref):
        k = pl.program_id(1)

        @pl.when(k == 0)
        def _():
            acc1_ref[...] = jnp.zeros_like(acc1_ref)
            acc2_ref[...] = jnp.zeros_like(acc2_ref)

        a_t = a_ref[...].astype(BF16)
        acc1_ref[...] += _dot_tn(a_t, b1_ref[...])
        acc2_ref[...] += _dot_tn(a_t, b2_ref[...])

        @pl.when(k == n_k - 1)
        def _():
            o1_ref[...] = acc1_ref[...].astype(o1_ref.dtype)
            o2_ref[...] = acc2_ref[...].astype(o2_ref.dtype)

    blk = lambda: pl.BlockSpec((None, tk, N), lambda s, k: (s, k, 0))
    out = lambda: pl.BlockSpec((None, M, N), lambda s, k: (s, 0, 0))
    return pl.pallas_call(
        body, name=name, grid=(S, n_k),
        in_specs=[pl.BlockSpec((tk, M), lambda s, k: (k, 0)), blk(), blk()],
        out_specs=[out(), out()],
        out_shape=[jax.ShapeDtypeStruct((S, M, N), BF16)] * 2,
        scratch_shapes=[pltpu.VMEM((M, N), F32), pltpu.VMEM((M, N), F32)],
        compiler_params=_cparams(("parallel", "arbitrary")),
    )(a, b1, b2)


def _mm_tn_blocked_a(a, b, *, name):
    S, T, M = a.shape
    N = b.shape[1]
    tk = _tile(T, 1024)
    return pl.pallas_call(
        _tn_body(1, T // tk), name=name, grid=(S, T // tk),
        in_specs=[pl.BlockSpec((None, tk, M), lambda s, k: (s, k, 0)),
                  pl.BlockSpec((tk, N), lambda s, k: (k, 0))],
        out_specs=pl.BlockSpec((None, M, N), lambda s, k: (s, 0, 0)),
        out_shape=jax.ShapeDtypeStruct((S, M, N), BF16),
        scratch_shapes=[pltpu.VMEM((M, N), F32)],
        compiler_params=_cparams(("parallel", "arbitrary")),
    )(a, b)


def _ln_fwd(r, g, b):
    mu = jnp.mean(r, axis=-1, keepdims=True)
    xc = r - mu
    var = jnp.mean(xc * xc, axis=-1, keepdims=True)
    return xc * lax.rsqrt(var + LN_EPS) * g + b


def _ln_bwd(dy, r, g):
    mu = jnp.mean(r, axis=-1, keepdims=True)
    xc = r - mu
    var = jnp.mean(xc * xc, axis=-1, keepdims=True)
    rstd = lax.rsqrt(var + LN_EPS)
    xhat = xc * rstd
    dxh = dy * g
    m1 = jnp.mean(dxh, axis=-1, keepdims=True)
    m2 = jnp.mean(dxh * xhat, axis=-1, keepdims=True)
    return rstd * (dxh - m1 - xhat * m2), xhat


def _lane_is_head0():
    return lax.broadcasted_iota(jnp.int32, (1, LANES), 1) < HEAD_DIM


def _band_shape():
    a = np.arange(A_TQ)[:, None] // CHUNK
    b = np.arange(A_WIN)[None, :] // CHUNK
    return (b >= a) & (b <= a + LEFT_CHUNKS)


def _band_streams(strip):
    return [dict(h=h, n=strip, rows=pl.ds(r0, strip)) for h in range(2) for r0 in range(0, A_TQ, strip)]


def _band_scores(st, i, qh_ref, k2, bias_ref):
    s = _dot_nt(qh_ref[st["h"], st["rows"], :], k2) + bias_ref[st["h"], st["rows"], :]
    c = lax.broadcasted_iota(jnp.int32, (st["n"], A_WIN), 1)
    st["s"] = jnp.where(c >= LEFT_CHUNKS * CHUNK - i * A_TQ, s, NEG)


def _band_softmax(st):
    s = st.pop("s")
    e = jnp.exp(s - jnp.max(s, axis=1, keepdims=True))
    st["p"] = e * (1.0 / jnp.sum(e, axis=1, keepdims=True))


def _attn_a_fwd(qkv, kvpad, bias, *, name, job=None):
    T = qkv.shape[0]
    n_hp = WIDTH // LANES

    def body(q_ref, k_ref, v_ref, bias_ref, o_ref, qh_ref, acc_ref):
        i = pl.program_id(1)
        row0 = pl.multiple_of(i * A_TQ, A_TQ)
        q2 = q_ref[...] * jnp.asarray(QK_SCALE, BF16)
        k2 = k_ref[pl.ds(row0, A_WIN), :]
        v2 = v_ref[pl.ds(row0, A_WIN), :]
        head0 = _lane_is_head0()
        qh_ref[0] = jnp.where(head0, q2, jnp.zeros_like(q2))
        qh_ref[1] = jnp.where(head0, jnp.zeros_like(q2), q2)

        def scores(st):
            _band_scores(st, i, qh_ref, k2, bias_ref)

        def values(st):
            acc_ref[st["h"], st["rows"], :] = _dot(st.pop("p").astype(BF16), v2)

        _skewed(_band_streams(A_STRIP), [scores, _band_softmax, values])
        o_ref[...] = jnp.where(head0, acc_ref[0], acc_ref[1]).astype(o_ref.dtype)

    (out,), rode = _call_carrying(
        job, body, name=name, grid=(n_hp, T // A_TQ),
        in_specs=[pl.BlockSpec((A_TQ, LANES), lambda hp, i: (i, hp)),
                  pl.BlockSpec((T + A_WIN - A_TQ, LANES), lambda hp, i: (0, hp)),
                  pl.BlockSpec((T + A_WIN - A_TQ, LANES), lambda hp, i: (0, hp + n_hp)),
                  pl.BlockSpec((2, A_TQ, A_WIN), lambda hp, i: (hp, 0, 0))],
        out_specs=[pl.BlockSpec((A_TQ, LANES), lambda hp, i: (i, hp))],
        out_shape=[jax.ShapeDtypeStruct((T, WIDTH), BF16)],
        scratch_shapes=[pltpu.VMEM((2, A_TQ, LANES), BF16), pltpu.VMEM((2, A_TQ, LANES), F32)],
        vmem_mib=48, args=(qkv, kvpad, kvpad, bias))
    return out, rode


def _attn_a_bwd(qkv, kvpad, bias, do, *, name, job=None):
    T = qkv.shape[0]
    TP = T + A_WIN - A_TQ
    n_hp = WIDTH // LANES

    def body(q_ref, k_ref, v_ref, bias_ref, do_ref, dq_ref, dko_ref, dvo_ref, db_ref,
             qh_ref, doh_ref, dqa_ref, dk_ref, dv_ref, qs_ref):
        i = pl.program_id(1)

        @pl.when(i == 0)
        def _():
            dk_ref[...] = jnp.zeros_like(dk_ref)
            dv_ref[...] = jnp.zeros_like(dv_ref)
            db_ref[...] = jnp.zeros_like(db_ref)

        row0 = pl.multiple_of(i * A_TQ, A_TQ)
        window = pl.ds(row0, A_WIN)
        scale = jnp.asarray(QK_SCALE, BF16)
        q2 = q_ref[...] * scale
        do2 = do_ref[...]
        k2 = k_ref[window, :]
        k2s = k2 * scale
        v2 = v_ref[window, :]
        head0 = _lane_is_head0()
        zero = jnp.zeros_like(q2)
        qs_ref[...] = q2
        qh_ref[0] = jnp.where(head0, q2, zero)
        qh_ref[1] = jnp.where(head0, zero, q2)
        doh_ref[0] = jnp.where(head0, do2, zero)
        doh_ref[1] = jnp.where(head0, zero, do2)
        dk = [[], []]
        dv = [[], []]

        def scores(st):
            _band_scores(st, i, qh_ref, k2, bias_ref)
            st["dp"] = _dot_nt(doh_ref[st["h"], st["rows"], :], v2)

        def dscores(st):
            _band_softmax(st)
            p, dp = st.pop("p"), st.pop("dp")
            ds = p * (dp - jnp.sum(p * dp, axis=1, keepdims=True))
            db_ref[st["h"], st["rows"], :] += ds
            st["dsb"] = ds.astype(BF16)
            st["pb"] = p.astype(BF16)

        def grads(st):
            h, rows = st["h"], st["rows"]
            dsb = st.pop("dsb")
            dqa_ref[h, rows, :] = _dot(dsb, k2s)
            dk[h].append(_dot_tn(dsb, qs_ref[rows, :]))
            dv[h].append(_dot_tn(st.pop("pb"), do_ref[rows, :]))

        _skewed(_band_streams(A_TQ), [scores, dscores, grads])
        dq_ref[...] = jnp.where(head0, dqa_ref[0], dqa_ref[1]).astype(dq_ref.dtype)
        dk_ref[window, :] += jnp.where(head0, sum(dk[0]), sum(dk[1]))
        dv_ref[window, :] += jnp.where(head0, sum(dv[0]), sum(dv[1]))

        @pl.when(i == T // A_TQ - 1)
        def _():
            dko_ref[...] = dk_ref[TP - T:, :].astype(BF16)
            dvo_ref[...] = dv_ref[TP - T:, :].astype(BF16)

    return _call_carrying(
        job, body, name=name, grid=(n_hp, T // A_TQ),
        in_specs=[pl.BlockSpec((A_TQ, LANES), lambda hp, i: (i, hp)),
                  pl.BlockSpec((TP, LANES), lambda hp, i: (0, hp)),
                  pl.BlockSpec((TP, LANES), lambda hp, i: (0, hp + n_hp)),
                  pl.BlockSpec((2, A_TQ, A_WIN), lambda hp, i: (hp, 0, 0)),
                  pl.BlockSpec((A_TQ, LANES), lambda hp, i: (i, hp))],
        out_specs=[pl.BlockSpec((A_TQ, LANES), lambda hp, i: (i, hp)),
                   pl.BlockSpec((T, LANES), lambda hp, i: (0, hp)),
                   pl.BlockSpec((T, LANES), lambda hp, i: (0, hp)),
                   pl.BlockSpec((2, A_TQ, A_WIN), lambda hp, i: (hp, 0, 0))],
        out_shape=[jax.ShapeDtypeStruct((T, WIDTH), BF16),
                   jax.ShapeDtypeStruct((T, WIDTH), BF16),
                   jax.ShapeDtypeStruct((T, WIDTH), BF16),
                   jax.ShapeDtypeStruct((WIDTH // HEAD_DIM, A_TQ, A_WIN), F32)],
        scratch_shapes=[pltpu.VMEM((2, A_TQ, LANES), BF16), pltpu.VMEM((2, A_TQ, LANES), BF16),
                        pltpu.VMEM((2, A_TQ, LANES), F32),
                        pltpu.VMEM((TP, LANES), F32), pltpu.VMEM((TP, LANES), F32),
                        pltpu.VMEM((A_TQ, LANES), BF16)],
        vmem_mib=56, args=(qkv, kvpad, kvpad, bias, do))


_N_DIAG = 2 * CHUNK - 1
_EXT_TOP = LEFT_CHUNKS * CHUNK + CHUNK - 1 + REL_CLIP


def _toeplitz_bias(rb):
    H = rb.shape[0]
    ext = jnp.concatenate([rb, jnp.broadcast_to(rb[:, N_REL - 1:], (H, _EXT_TOP + 1 - N_REL))], axis=1)
    vec = jnp.stack([ext[:, _EXT_TOP - (_N_DIAG - 1) - CHUNK * k:_EXT_TOP - CHUNK * k + 1]
                     for k in range(LEFT_CHUNKS + 1)], axis=1)
    rev = jnp.pad(vec[:, :, ::-1], ((0, 0), (0, 0), (0, 1)))
    flat = jnp.broadcast_to(rev[:, :, None, :], (H, LEFT_CHUNKS + 1, CHUNK, _N_DIAG + 1))
    skew = flat.reshape(H, LEFT_CHUNKS + 1, -1)[:, :, :CHUNK * _N_DIAG].reshape(H, LEFT_CHUNKS + 1, CHUNK, _N_DIAG)
    blocks = skew[:, :, :, CHUNK - 1:]
    neg = jnp.full((H, CHUNK, CHUNK), NEG, F32)
    rows = [jnp.concatenate([blocks[:, b - a] if 0 <= b - a <= LEFT_CHUNKS else neg for b in range(A_WIN // CHUNK)],
                            axis=2) for a in range(A_TQ // CHUNK)]
    return jnp.concatenate(rows, axis=1)


def _toeplitz_bias_grad(db):
    H = db.shape[0]
    d5 = db.reshape(H, A_TQ // CHUNK, CHUNK, A_WIN // CHUNK, CHUNK)
    g_blocks = jnp.stack([sum(d5[:, a, :, a + k, :] for a in range(A_TQ // CHUNK))
                          for k in range(LEFT_CHUNKS + 1)], axis=1)
    d_skew = jnp.pad(g_blocks, ((0, 0), (0, 0), (0, 0), (CHUNK - 1, 0)))
    d_flat = jnp.pad(d_skew.reshape(H, LEFT_CHUNKS + 1, CHUNK * _N_DIAG), ((0, 0), (0, 0), (0, CHUNK)))
    g_vec = jnp.sum(d_flat.reshape(H, LEFT_CHUNKS + 1, CHUNK, _N_DIAG + 1), axis=2)[:, :, :_N_DIAG][:, :, ::-1]
    g_ext = sum(jnp.pad(g_vec[:, k], ((0, 0), (_EXT_TOP - (_N_DIAG - 1) - CHUNK * k, CHUNK * k)))
                for k in range(LEFT_CHUNKS + 1))
    return jnp.concatenate([g_ext[:, :N_REL - 1], jnp.sum(g_ext[:, N_REL - 1:], axis=1, keepdims=True)], axis=1)


def _split_bf16(x):
    hi = x.astype(BF16)
    lo = (x - hi.astype(F32)).astype(BF16)
    return hi, lo


def _sb_streams(d, strips=None, **tile):
    out = []
    for h in range(2):
        for r in (range(B_TQ // B_TS) if strips is None else strips):
            if d is not None and d > r:
                continue
            out.append(dict(h=h, r=r, rows=pl.ds(r * B_TS, B_TS), diag=(d is not None and d == r), **tile))
    return out


def _sb_sweep(i, car_ref, streams_of, run):
    sub = B_TQ // B_TS
    run([st for d in reversed(range(sub)) for st in streams_of(i * sub + d, d, None)])

    def alive(c):
        return (c[0] < i * sub) & (c[1] > B_DEAD)

    def step(c):
        kb = i * sub - 1 - c[0]
        if sub > 1:
            lower_alive = jnp.max(car_ref[:, B_TS:, :]) > B_DEAD
            lax.cond(lower_alive, lambda: run(streams_of(kb, None, None)), lambda: run(streams_of(kb, None, [0])))
        else:
            run(streams_of(kb, None, None))
        return c[0] + 1, jnp.max(car_ref[...])

    lax.while_loop(alive, step, (jnp.int32(0), jnp.float32(0.0)))


def _piece_rows(st, p):
    return pl.ds(st["r"] * B_TS + p, B_PIECE)


def _rows_cat(parts):
    return jnp.concatenate(parts, axis=0)


def _skewed(streams, stages):
    for t in range(len(streams) + len(stages) - 1):
        for s, st in enumerate(streams):
            if 0 <= t - s < len(stages):
                stages[t - s](st)


def _sb_logs(st, z2):
    log_beta, log_keep, keep_bf = [], [], []
    for p in range(0, B_TS, B_PIECE):
        z = z2[p:p + B_PIECE]
        lp2 = jnp.log(1.0 + jnp.exp2(-jnp.abs(z))) * LOG2E
        lb = jnp.minimum(z, 0.0) - lp2
        lk = lb - z
        if st["diag"]:
            lk = jnp.where(_strict_lower(p), lk, 0.0)
        log_beta.append(lb)
        log_keep.append(lk)
        keep_bf.append(lk.astype(BF16))
    st["log_beta"] = _rows_cat(log_beta)
    st["log_keep"] = _rows_cat(log_keep)
    st["keep_bf"] = _rows_cat(keep_bf)


def _strict_lower(p):
    t = p + lax.broadcasted_iota(jnp.int32, (B_PIECE, B_TS), 0)
    s = lax.broadcasted_iota(jnp.int32, (B_PIECE, B_TS), 1)
    return s < t


def _tri(strict):
    j = lax.broadcasted_iota(jnp.int32, (B_TS, B_TS), 0)
    s = lax.broadcasted_iota(jnp.int32, (B_TS, B_TS), 1)
    return jnp.where(j > s if strict else j >= s, 1.0, 0.0).astype(BF16)


def _call_carrying(job, body, *, name, grid, in_specs, out_specs, out_shape, scratch_shapes, vmem_mib, args):
    n_in, n_out, n_scr = len(in_specs), len(out_specs), len(scratch_shapes)
    if job is None:
        res = pl.pallas_call(body, name=name, grid=grid, in_specs=in_specs, out_specs=out_specs,
                             out_shape=out_shape, scratch_shapes=scratch_shapes,
                             compiler_params=_cparams(("arbitrary",) * len(grid), vmem_mib))(*args)
        return res, []
    j_in, j_out = len(job.arrays), len(job.out_shape)
    hbm = pl.BlockSpec(memory_space=pltpu.HBM)

    def carrying(*refs):
        refs = list(refs)
        ins, refs = refs[:n_in], refs[n_in:]
        j_ins, refs = refs[:j_in], refs[j_in:]
        outs, refs = refs[:n_out], refs[n_out:]
        j_outs, refs = refs[:j_out], refs[j_out:]
        scr, sems = refs[:n_scr], refs[n_scr:]
        first = functools.reduce(jnp.logical_and, [pl.program_id(d) == 0 for d in range(len(grid))])
        last = functools.reduce(jnp.logical_and, [pl.program_id(d) == grid[d] - 1 for d in range(len(grid))])

        @pl.when(first)
        def _():
            job.start(j_ins, j_outs, sems)

        body(*ins, *outs, *scr)

        @pl.when(last)
        def _():
            job.wait(j_ins, j_outs, sems)

    res = pl.pallas_call(
        carrying, name=name, grid=grid,
        in_specs=list(in_specs) + [hbm] * j_in, out_specs=list(out_specs) + [hbm] * j_out,
        out_shape=list(out_shape) + job.out_shape, scratch_shapes=list(scratch_shapes) + job.scratch(),
        compiler_params=_cparams(("arbitrary",) * len(grid), vmem_mib))(*args, *job.arrays)
    return res[:n_out], res[n_out:]


def _attn_b_fwd(qkv, *, col0, name, job=None):
    T = qkv.shape[0]
    n_hp = WIDTH // LANES
    sub = B_TQ // B_TS

    def body(q_ref, k_ref, v_ref, o_ref, acc_ref, car_ref, qh_ref):
        i = pl.program_id(1)
        q2 = q_ref[...]
        head0 = _lane_is_head0()
        qh_ref[0] = jnp.where(head0, q2, jnp.zeros_like(q2))
        qh_ref[1] = jnp.where(head0, jnp.zeros_like(q2), q2)
        tri_s = _tri(True)
        acc_ref[...] = jnp.zeros_like(acc_ref)
        car_ref[...] = jnp.zeros_like(car_ref)

        def streams_of(kb, d, strips):
            keys = pl.ds(pl.multiple_of(kb * B_TS, B_TS), B_TS)
            return _sb_streams(d, strips, k2=k_ref[keys, :], v2=v_ref[keys, :])

        def scores(st):
            st["z2"] = _dot_nt(qh_ref[st["h"], st["rows"], :], st.pop("k2")) * (QK_SCALE * LOG2E)

        def logs(st):
            _sb_logs(st, st.pop("z2"))

        def suffix(st):
            st["suffix"] = _dot(st.pop("keep_bf"), tri_s)

        def weights(st):
            log_beta, suffix, log_keep = st.pop("log_beta"), st.pop("suffix"), st.pop("log_keep")
            wb = []
            for p in range(0, B_TS, B_PIECE):
                rows = _piece_rows(st, p)
                car = car_ref[st["h"], rows, :]
                w = jnp.exp2(log_beta[p:p + B_PIECE] + suffix[p:p + B_PIECE] + car)
                if st["diag"]:
                    w = jnp.where(_strict_lower(p), w, 0.0)
                wb.append(w.astype(BF16))
                car_ref[st["h"], rows, :] = car + jnp.sum(log_keep[p:p + B_PIECE], axis=1, keepdims=True)
            st["wb"] = _rows_cat(wb)

        def values(st):
            acc_ref[st["h"], st["rows"], :] += _dot(st.pop("wb"), st.pop("v2"))

        _sb_sweep(i, car_ref, streams_of, lambda sts: _skewed(sts, [scores, logs, suffix, weights, values]))
        o_ref[...] = jnp.where(head0, acc_ref[0], acc_ref[1])

    (out,), rode = _call_carrying(
        job, body, name=name, grid=(n_hp, T // B_TQ),
        in_specs=[pl.BlockSpec((B_TQ, LANES), lambda hp, i: (i, hp + col0)),
                  pl.BlockSpec((T, LANES), lambda hp, i: (0, hp + col0 + n_hp)),
                  pl.BlockSpec((T, LANES), lambda hp, i: (0, hp + col0 + 2 * n_hp))],
        out_specs=[pl.BlockSpec((B_TQ, LANES), lambda hp, i: (i, hp))],
        out_shape=[jax.ShapeDtypeStruct((T, WIDTH), F32)],
        scratch_shapes=[pltpu.VMEM((2, B_TQ, LANES), F32), pltpu.VMEM((2, B_TQ, 1), F32),
                        pltpu.VMEM((2, B_TQ, LANES), BF16)],
        vmem_mib=48, args=(qkv, qkv, qkv))
    return out, rode


def _attn_b_bwd(qkv, out, do, *, col0, name, job=None):
    T = qkv.shape[0]
    n_hp = WIDTH // LANES
    sub = B_TQ // B_TS

    def body(q_ref, k_ref, v_ref, o_ref, do_ref, dq_ref, dko_ref, dvo_ref,
             dqa_ref, car_ref, carr_ref, tot_ref, qh_ref, doh_ref, qs_ref, dk_ref, dv_ref):
        i = pl.program_id(1)

        @pl.when(i == 0)
        def _():
            dk_ref[...] = jnp.zeros_like(dk_ref)
            dv_ref[...] = jnp.zeros_like(dv_ref)

        q2 = q_ref[...]
        do2 = do_ref[...]
        head0 = _lane_is_head0()
        zero = jnp.zeros_like(q2)
        qh_ref[0] = jnp.where(head0, q2, zero)
        qh_ref[1] = jnp.where(head0, zero, q2)
        doh_ref[0] = jnp.where(head0, do2, zero)
        doh_ref[1] = jnp.where(head0, zero, do2)
        scale = jnp.asarray(QK_SCALE, BF16)
        qs_ref[...] = q2 * scale
        tri_s = _tri(True)
        tri_i = _tri(False)
        prod = do2.astype(F32) * o_ref[...]
        tot_ref[0] = jnp.sum(jnp.where(head0, prod, 0.0), axis=1, keepdims=True)
        tot_ref[1] = jnp.sum(jnp.where(head0, 0.0, prod), axis=1, keepdims=True)
        dqa_ref[...] = jnp.zeros_like(dqa_ref)
        car_ref[...] = jnp.zeros_like(car_ref)
        carr_ref[...] = jnp.zeros_like(carr_ref)

        def streams_of(kb, d, strips):
            keys = pl.ds(pl.multiple_of(kb * B_TS, B_TS), B_TS)
            k2 = k_ref[keys, :]
            return _sb_streams(d, strips, keys=keys, k2=k2, v2=v_ref[keys, :], k2s=k2 * scale)

        def run(streams):
            def scores(st):
                st["z2"] = _dot_nt(qh_ref[st["h"], st["rows"], :], st.pop("k2")) * (QK_SCALE * LOG2E)
                st["dw"] = _dot_nt(doh_ref[st["h"], st["rows"], :], st.pop("v2"))

            def logs(st):
                _sb_logs(st, st.pop("z2"))

            def suffix(st):
                st["suffix"] = _dot(st.pop("keep_bf"), tri_s)

            def weights(st):
                h = st["h"]
                suffix, dw = st.pop("suffix"), st.pop("dw")
                wb, dlog, hi, lo = [], [], [], []
                for p in range(0, B_TS, B_PIECE):
                    rows = _piece_rows(st, p)
                    car = car_ref[h, rows, :]
                    w = jnp.exp2(st["log_beta"][p:p + B_PIECE] + suffix[p:p + B_PIECE] + car)
                    if st["diag"]:
                        w = jnp.where(_strict_lower(p), w, 0.0)
                    w = w.astype(BF16)
                    dl = w.astype(F32) * dw[p:p + B_PIECE]
                    dl_hi, dl_lo = _split_bf16(dl)
                    wb.append(w)
                    dlog.append(dl)
                    hi.append(dl_hi)
                    lo.append(dl_lo)
                    car_ref[h, rows, :] = car + jnp.sum(st["log_keep"][p:p + B_PIECE], axis=1, keepdims=True)
                st["wb"], st["dlog"], st["hi"], st["lo"] = _rows_cat(wb), _rows_cat(dlog), _rows_cat(hi), _rows_cat(lo)

            def later(st):
                st["later"] = _dot(st.pop("hi"), tri_i) + _dot(st.pop("lo"), tri_i)

            def dscores(st):
                h = st["h"]
                later, dlog = st.pop("later"), st.pop("dlog")
                log_keep, log_beta = st.pop("log_keep"), st.pop("log_beta")
                dzb = []
                for p in range(0, B_TS, B_PIECE):
                    rows = _piece_rows(st, p)
                    pc = slice(p, p + B_PIECE)
                    carr = carr_ref[h, rows, :]
                    earlier = tot_ref[h, rows, :] - (later[pc] + carr)
                    dz = dlog[pc] * jnp.exp2(log_keep[pc]) - jnp.exp2(log_beta[pc]) * earlier
                    if st["diag"]:
                        dz = jnp.where(_strict_lower(p), dz, 0.0)
                    dzb.append(dz.astype(BF16))
                    carr_ref[h, rows, :] = carr + jnp.sum(dlog[pc], axis=1, keepdims=True)
                st["dzb"] = _rows_cat(dzb)

            def grads(st):
                h, rows, keys = st["h"], st["rows"], st["keys"]
                mine = head0 if h == 0 else jnp.logical_not(head0)
                dzb = st.pop("dzb")
                dqa_ref[h, rows, :] += _dot(dzb, st.pop("k2s"))
                dk_ref[keys, :] += jnp.where(mine, _dot_tn(dzb, qs_ref[rows, :]), 0.0)
                dv_ref[keys, :] += jnp.where(mine, _dot_tn(st.pop("wb"), do_ref[rows, :]), 0.0)

            _skewed(streams, [scores, logs, suffix, weights, later, dscores, grads])

        _sb_sweep(i, car_ref, streams_of, run)
        dq_ref[...] = jnp.where(head0, dqa_ref[0], dqa_ref[1]).astype(dq_ref.dtype)

        @pl.when(i == T // B_TQ - 1)
        def _():
            dko_ref[...] = dk_ref[...].astype(BF16)
            dvo_ref[...] = dv_ref[...].astype(BF16)

    return _call_carrying(
        job, body, name=name, grid=(n_hp, T // B_TQ),
        in_specs=[pl.BlockSpec((B_TQ, LANES), lambda hp, i: (i, hp + col0)),
                  pl.BlockSpec((T, LANES), lambda hp, i: (0, hp + col0 + n_hp)),
                  pl.BlockSpec((T, LANES), lambda hp, i: (0, hp + col0 + 2 * n_hp)),
                  pl.BlockSpec((B_TQ, LANES), lambda hp, i: (i, hp)),
                  pl.BlockSpec((B_TQ, LANES), lambda hp, i: (i, hp))],
        out_specs=[pl.BlockSpec((B_TQ, LANES), lambda hp, i: (i, hp)),
                   pl.BlockSpec((T, LANES), lambda hp, i: (0, hp)),
                   pl.BlockSpec((T, LANES), lambda hp, i: (0, hp))],
        out_shape=[jax.ShapeDtypeStruct((T, WIDTH), BF16),
                   jax.ShapeDtypeStruct((T, WIDTH), BF16),
                   jax.ShapeDtypeStruct((T, WIDTH), BF16)],
        scratch_shapes=[pltpu.VMEM((2, B_TQ, LANES), F32), pltpu.VMEM((2, B_TQ, 1), F32),
                        pltpu.VMEM((2, B_TQ, 1), F32), pltpu.VMEM((2, B_TQ, 1), F32),
                        pltpu.VMEM((2, B_TQ, LANES), BF16), pltpu.VMEM((2, B_TQ, LANES), BF16),
                        pltpu.VMEM((B_TQ, LANES), BF16),
                        pltpu.VMEM((T, LANES), F32), pltpu.VMEM((T, LANES), F32)],
        vmem_mib=56, args=(qkv, qkv, qkv, out, do))


def _gated_mix(oa_ref, ob_ref, g_ref, bg_ref, wpa_ref, wpb_ref, D):
    ya = _dot(oa_ref[...].astype(BF16), wpa_ref[...])
    yb = _dot(ob_ref[...].astype(BF16), wpb_ref[...])
    sa = jax.nn.sigmoid(g_ref[:, :D] + bg_ref[:, :D])
    sb = jax.nn.sigmoid(g_ref[:, D:] + bg_ref[:, D:])
    return ya, yb, sa, sb


def _proj_fwd(oa, ob, g, bg, wpa, wpb, wo, wl, xin, lng, lnb, layer, *, alpha, name):
    T, D = xin.shape
    tm = _tile(T, 512)
    row = lambda i: (i, 0)
    wspec = lambda r, c: pl.BlockSpec((None, r, c), lambda i: (wl, 0, 0))
    vec = lambda c: pl.BlockSpec((None, 1, c), lambda i: (layer, 0, 0))

    def body(oa_ref, ob_ref, g_ref, bg_ref, wpa_ref, wpb_ref, wo_ref, x_ref, lg_ref, lb_ref, x1_ref, r1_ref, x1b_ref):
        ya, yb, sa, sb = _gated_mix(oa_ref, ob_ref, g_ref, bg_ref, wpa_ref, wpb_ref, D)
        mix = _dot((sa * ya + sb * yb).astype(BF16), wo_ref[...])
        r1 = alpha * x_ref[...] + mix
        r1_ref[...] = r1
        x1 = _ln_fwd(r1, lg_ref[...], lb_ref[...])
        x1_ref[...] = x1
        x1b_ref[...] = x1.astype(BF16)

    return pl.pallas_call(
        body, name=name, grid=(T // tm,),
        in_specs=[pl.BlockSpec((tm, WIDTH), row), pl.BlockSpec((tm, WIDTH), row), pl.BlockSpec((tm, 2 * D), row),
                  vec(2 * D), wspec(WIDTH, D), wspec(WIDTH, D), wspec(D, D),
                  pl.BlockSpec((tm, D), row), vec(D), vec(D)],
        out_specs=[pl.BlockSpec((tm, D), row), pl.BlockSpec((tm, D), row), pl.BlockSpec((tm, D), row)],
        out_shape=[jax.ShapeDtypeStruct((T, D), F32), jax.ShapeDtypeStruct((T, D), F32),
                   jax.ShapeDtypeStruct((T, D), BF16)],
        compiler_params=_cparams(("arbitrary",), 56),
    )(oa, ob, g, bg, wpa, wpb, wo, xin, lng, lnb)


def _proj_bwd(dx1, r1, lng, oa, ob, g, bg, wpa, wpb, wo, wl, layer, *, name):
    T, D = dx1.shape
    tm = _tile(T, 512)
    row = lambda i: (i, 0)
    fixed = lambda i: (0, 0)
    wspec = lambda r, c: pl.BlockSpec((None, r, c), lambda i: (wl, 0, 0))
    vec = lambda c: pl.BlockSpec((None, 1, c), lambda i: (layer, 0, 0))

    def body(dx_ref, r1_ref, lg_ref, oa_ref, ob_ref, g_ref, bg_ref, wpa_ref, wpb_ref, wo_ref,
             dr_ref, mix_ref, dya_ref, dyb_ref, dg_ref, doa_ref, dob_ref, dlg_ref, dlb_ref, dbg_ref):
        @pl.when(pl.program_id(0) == 0)
        def _():
            dlg_ref[...] = jnp.zeros_like(dlg_ref)
            dlb_ref[...] = jnp.zeros_like(dlb_ref)
            dbg_ref[...] = jnp.zeros_like(dbg_ref)

        dx = dx_ref[...]
        dr, xhat = _ln_bwd(dx, r1_ref[...], lg_ref[...])
        dr_ref[...] = dr
        dlg_ref[...] += jnp.sum(dx * xhat, axis=0, keepdims=True)
        dlb_ref[...] += jnp.sum(dx, axis=0, keepdims=True)
        dmix = _dot_nt(dr.astype(BF16), wo_ref[...])
        ya, yb, sa, sb = _gated_mix(oa_ref, ob_ref, g_ref, bg_ref, wpa_ref, wpb_ref, D)
        mix_ref[...] = (sa * ya + sb * yb).astype(BF16)
        dya = (dmix * sa).astype(BF16)
        dyb = (dmix * sb).astype(BF16)
        dya_ref[...] = dya
        dyb_ref[...] = dyb
        dga = dmix * ya * (sa * (1.0 - sa))
        dgb = dmix * yb * (sb * (1.0 - sb))
        dg_ref[:, :D] = dga.astype(BF16)
        dg_ref[:, D:] = dgb.astype(BF16)
        dbg_ref[:, :D] += jnp.sum(dga, axis=0, keepdims=True)
        dbg_ref[:, D:] += jnp.sum(dgb, axis=0, keepdims=True)
        doa_ref[...] = _dot_nt(dya, wpa_ref[...]).astype(BF16)
        dob_ref[...] = _dot_nt(dyb, wpb_ref[...]).astype(BF16)

    return pl.pallas_call(
        body, name=name, grid=(T // tm,),
        in_specs=[pl.BlockSpec((tm, D), row), pl.BlockSpec((tm, D), row), vec(D),
                  pl.BlockSpec((tm, WIDTH), row), pl.BlockSpec((tm, WIDTH), row), pl.BlockSpec((tm, 2 * D), row),
                  vec(2 * D), wspec(WIDTH, D), wspec(WIDTH, D), wspec(D, D)],
        out_specs=[pl.BlockSpec((tm, D), row), pl.BlockSpec((tm, D), row), pl.BlockSpec((tm, D), row),
                   pl.BlockSpec((tm, D), row), pl.BlockSpec((tm, 2 * D), row),
                   pl.BlockSpec((tm, WIDTH), row), pl.BlockSpec((tm, WIDTH), row),
                   pl.BlockSpec((1, D), fixed), pl.BlockSpec((1, D), fixed), pl.BlockSpec((1, 2 * D), fixed)],
        out_shape=[jax.ShapeDtypeStruct((T, D), F32), jax.ShapeDtypeStruct((T, D), BF16),
                   jax.ShapeDtypeStruct((T, D), BF16), jax.ShapeDtypeStruct((T, D), BF16),
                   jax.ShapeDtypeStruct((T, 2 * D), BF16),
                   jax.ShapeDtypeStruct((T, WIDTH), BF16), jax.ShapeDtypeStruct((T, WIDTH), BF16),
                   jax.ShapeDtypeStruct((1, D), F32), jax.ShapeDtypeStruct((1, D), F32),
                   jax.ShapeDtypeStruct((1, 2 * D), F32)],
        compiler_params=_cparams(("arbitrary",), 56),
    )(dx1, r1, lng, oa, ob, g, bg, wpa, wpb, wo)


def _ffn_fwd(x1, wfi, wfo, wl, lng, lnb, layer, *, alpha, name, job=None):
    T, D = x1.shape
    tf = wfi.shape[-1]
    nj = wfi.shape[0] // 2
    tm = _tile(T, 1024)
    vec = lambda c: pl.BlockSpec((None, 1, c), lambda i, j: (layer, 0, 0))

    def body(x_ref, wg_ref, wu_ref, wo_ref, lg_ref, lb_ref, gs_ref, us_ref, r2_ref, x2_ref, acc_ref, xb_ref):
        j = pl.program_id(1)

        @pl.when(j == 0)
        def _():
            xb_ref[...] = x_ref[...].astype(BF16)
            acc_ref[...] = jnp.zeros_like(acc_ref)

        gv = _dot(xb_ref[...], wg_ref[...])
        uv = _dot(xb_ref[...], wu_ref[...])
        gs_ref[...] = gv.astype(gs_ref.dtype)
        us_ref[...] = uv.astype(us_ref.dtype)
        act = gv * jax.nn.sigmoid(gv) * uv
        acc_ref[...] += _dot(act.astype(BF16), wo_ref[...])

        @pl.when(j == nj - 1)
        def _():
            r2 = alpha * x_ref[...] + acc_ref[...]
            r2_ref[...] = r2
            x2_ref[...] = _ln_fwd(r2, lg_ref[...], lb_ref[...])

    return _call_carrying(
        job, body, name=name, grid=(T // tm, nj),
        in_specs=[pl.BlockSpec((tm, D), lambda i, j: (i, 0)),
                  pl.BlockSpec((None, None, D, tf), lambda i, j: (j, wl, 0, 0)),
                  pl.BlockSpec((None, None, D, tf), lambda i, j: (j + nj, wl, 0, 0)),
                  pl.BlockSpec((None, tf, D), lambda i, j: (wl, j, 0)),
                  vec(D), vec(D)],
        out_specs=[pl.BlockSpec((None, tm, tf), lambda i, j: (j, i, 0)),
                   pl.BlockSpec((None, tm, tf), lambda i, j: (j, i, 0)),
                   pl.BlockSpec((tm, D), lambda i, j: (i, 0)),
                   pl.BlockSpec((tm, D), lambda i, j: (i, 0))],
        out_shape=[jax.ShapeDtypeStruct((nj, T, tf), BF16), jax.ShapeDtypeStruct((nj, T, tf), BF16),
                   jax.ShapeDtypeStruct((T, D), F32), jax.ShapeDtypeStruct((T, D), F32)],
        scratch_shapes=[pltpu.VMEM((tm, D), F32), pltpu.VMEM((tm, D), BF16)],
        vmem_mib=56, args=(x1, wfi, wfi, wfo, lng, lnb))


def _ffn_bwd(dx2, r2, lng, gs, us, wfi, wfo, wl, layer, *, alpha, name, job=None):
    T, D = dx2.shape
    tf = wfi.shape[-1]
    nj = wfi.shape[0] // 2
    tm = _tile(T, 512)
    vec = lambda c: pl.BlockSpec((None, 1, c), lambda i, j: (layer, 0, 0))
    blk = lambda: pl.BlockSpec((None, tm, tf), lambda i, j: (j, i, 0))

    def body(dx_ref, r2_ref, lg_ref, gs_ref, us_ref, wg_ref, wu_ref, wo_ref,
             dr_ref, act_ref, dg_ref, du_ref, dx1_ref, dlg_ref, dlb_ref, acc_ref, drb_ref):
        i = pl.program_id(0)
        j = pl.program_id(1)

        @pl.when((i == 0) & (j == 0))
        def _():
            dlg_ref[...] = jnp.zeros_like(dlg_ref)
            dlb_ref[...] = jnp.zeros_like(dlb_ref)

        @pl.when(j == 0)
        def _():
            dx = dx_ref[...]
            dr, xhat = _ln_bwd(dx, r2_ref[...], lg_ref[...])
            dlg_ref[...] += jnp.sum(dx * xhat, axis=0, keepdims=True)
            dlb_ref[...] += jnp.sum(dx, axis=0, keepdims=True)
            drb_ref[...] = dr.astype(BF16)
            dr_ref[...] = dr.astype(BF16)
            acc_ref[...] = alpha * dr

        dact = _dot_nt(drb_ref[...], wo_ref[...])
        gv = gs_ref[...].astype(F32)
        uv = us_ref[...].astype(F32)
        s = jax.nn.sigmoid(gv)
        silu = gv * s
        act_ref[...] = (silu * uv).astype(BF16)
        dg = (dact * uv * (s * (1.0 + gv * (1.0 - s)))).astype(BF16)
        du = (dact * silu).astype(BF16)
        dg_ref[...] = dg
        du_ref[...] = du
        acc_ref[...] += _dot_nt(dg, wg_ref[...]) + _dot_nt(du, wu_ref[...])

        @pl.when(j == nj - 1)
        def _():
            dx1_ref[...] = acc_ref[...]

    return _call_carrying(
        job, body, name=name, grid=(T // tm, nj),
        in_specs=[pl.BlockSpec((tm, D), lambda i, j: (i, 0)), pl.BlockSpec((tm, D), lambda i, j: (i, 0)), vec(D),
                  blk(), blk(),
                  pl.BlockSpec((None, None, D, tf), lambda i, j: (j, wl, 0, 0)),
                  pl.BlockSpec((None, None, D, tf), lambda i, j: (j + nj, wl, 0, 0)),
                  pl.BlockSpec((None, tf, D), lambda i, j: (wl, j, 0))],
        out_specs=[pl.BlockSpec((tm, D), lambda i, j: (i, 0)), blk(), blk(), blk(),
                   pl.BlockSpec((tm, D), lambda i, j: (i, 0)),
                   pl.BlockSpec((1, D), lambda i, j: (0, 0)), pl.BlockSpec((1, D), lambda i, j: (0, 0))],
        out_shape=[jax.ShapeDtypeStruct((T, D), BF16),
                   jax.ShapeDtypeStruct((nj, T, tf), BF16), jax.ShapeDtypeStruct((nj, T, tf), BF16),
                   jax.ShapeDtypeStruct((nj, T, tf), BF16),
                   jax.ShapeDtypeStruct((T, D), F32),
                   jax.ShapeDtypeStruct((1, D), F32), jax.ShapeDtypeStruct((1, D), F32)],
        scratch_shapes=[pltpu.VMEM((tm, D), F32), pltpu.VMEM((tm, D), BF16)],
        vmem_mib=56, args=(dx2, r2, lng, gs, us, wfi, wfi, wfo))


def _loss_head(y, target, *, name):
    T, D = y.shape
    tm = _tile(T, 1024)

    def body(y_ref, t_ref, dy_ref, sq_ref):
        @pl.when(pl.program_id(0) == 0)
        def _():
            sq_ref[...] = jnp.zeros_like(sq_ref)
        err = y_ref[...] - t_ref[...]
        dy_ref[...] = err * (1.0 / D)
        sq_ref[...] += jnp.sum(err * err, axis=0, keepdims=True)

    return pl.pallas_call(
        body, name=name, grid=(T // tm,),
        in_specs=[pl.BlockSpec((tm, D), lambda i: (i, 0)), pl.BlockSpec((tm, D), lambda i: (i, 0))],
        out_specs=[pl.BlockSpec((tm, D), lambda i: (i, 0)), pl.BlockSpec((1, D), lambda i: (0, 0))],
        out_shape=[jax.ShapeDtypeStruct((T, D), F32), jax.ShapeDtypeStruct((1, D), F32)],
        compiler_params=_cparams(("arbitrary",)),
    )(y, target)


def _my_place():
    return lax.axis_index("x"), lax.axis_index("y"), lax.axis_index("c")


def _peer(place, k):
    x, y, c = place
    return (1 - x if k & 4 else x, 1 - y if k & 2 else y, 1 - c if k & 1 else c)


def _logical(place):
    x, y, c = place
    return 4 * x + 2 * y + c


def _block_of(ref, mode, idx):
    if mode == "blk":
        return ref.at[idx]
    if mode == "col":
        size = ref.shape[2] // N_DEV
        return ref.at[:, :, pl.ds(pl.multiple_of(idx * size, size), size)]
    size = ref.shape[1] // N_DEV
    return ref.at[:, pl.ds(pl.multiple_of(idx * size, size), size), :]


def _full_shape(shard, mode):
    if mode == "blk":
        return (N_DEV,) + shard.shape
    if mode == "col":
        return shard.shape[:2] + (N_DEV * shard.shape[2],)
    return (shard.shape[0], N_DEV * shard.shape[1], shard.shape[2])


class _Exchange:
    def __init__(self, arrays, out_shape, build):
        self.arrays = list(arrays)
        self.out_shape = list(out_shape)
        self.build = build

    def scratch(self):
        n = len(self.arrays)
        return [pltpu.SemaphoreType.DMA((n * N_DEV,)), pltpu.SemaphoreType.DMA((n * N_DEV,)),
                pltpu.SemaphoreType.DMA((n,))]

    def start(self, ins, outs, sems):
        for cp in self.build(ins, outs, *sems):
            cp.start()

    def wait(self, ins, outs, sems):
        for cp in self.build(ins, outs, *sems):
            cp.wait()

    def run(self, name):
        n_in, n_out = len(self.arrays), len(self.out_shape)
        hbm = pl.BlockSpec(memory_space=pltpu.HBM)

        def body(*refs):
            ins, outs, sems = refs[:n_in], refs[n_in:n_in + n_out], refs[n_in + n_out:]
            self.start(ins, outs, sems)
            self.wait(ins, outs, sems)

        return pl.pallas_call(
            body, name=name, in_specs=[hbm] * n_in, out_specs=[hbm] * n_out,
            out_shape=self.out_shape, scratch_shapes=self.scratch(),
        )(*self.arrays)


def _copies_to_all(src_of, dst_of, n, send, recv, local):
    me = _my_place()
    copies = []
    for a in range(n):
        copies.append(pltpu.make_async_copy(src_of(a, _logical(me)), dst_of(a), local.at[a]))
        for k in range(1, N_DEV):
            peer = _peer(me, k)
            copies.append(pltpu.make_async_remote_copy(
                src_ref=src_of(a, _logical(peer)), dst_ref=dst_of(a),
                send_sem=send.at[a * N_DEV + k], recv_sem=recv.at[a * N_DEV + k],
                device_id=peer, device_id_type=MESH))
    return copies


def _gather_job(shards, modes):
    def build(ins, outs, send, recv, local):
        my_id = _logical(_my_place())
        return _copies_to_all(lambda a, dev: ins[a], lambda a: _block_of(outs[a], modes[a], my_id),
                              len(shards), send, recv, local)

    return _Exchange(shards, [jax.ShapeDtypeStruct(_full_shape(s, m), s.dtype) for s, m in zip(shards, modes)], build)


def _gather_via_sibling(shard, mode, *, name):
    hbm = pl.BlockSpec(memory_space=pltpu.HBM)

    def body(x_ref, o_ref, send, recv, local):
        x, y, c = _my_place()
        me, sibling = (x, y, c), (x, y, 1 - c)
        chips = [(1 - x, y), (x, 1 - y), (1 - x, 1 - y)]

        def copy(k, block, to, src=None):
            dst = _block_of(o_ref, mode, _logical(block))
            return pltpu.make_async_remote_copy(src_ref=dst if src is None else src, dst_ref=dst,
                                                send_sem=send.at[k], recv_sem=recv.at[k],
                                                device_id=to, device_id_type=MESH)

        mine = pltpu.make_async_copy(x_ref, _block_of(o_ref, mode, _logical(me)), local)
        mine.start()
        first = [copy(0, me, sibling, src=x_ref)]
        first += [copy(1 + j, me, (*chip, c), src=x_ref) for j, chip in enumerate(chips)]
        for cp in first:
            cp.start()
        passed = [copy(4 + j, (*chip, c), sibling) for j, chip in enumerate(chips)]
        for j, chip in enumerate(chips):
            copy(1 + j, (*chip, c), me).wait_recv()
            passed[j].start()
        copy(0, sibling, me).wait_recv()
        for j, chip in enumerate(chips):
            copy(4 + j, (*chip, 1 - c), me).wait_recv()
        for cp in first + passed:
            cp.wait_send()
        mine.wait()

    return pl.pallas_call(
        body, name=name, in_specs=[hbm], out_specs=hbm,
        out_shape=jax.ShapeDtypeStruct(_full_shape(shard, mode), shard.dtype),
        scratch_shapes=[pltpu.SemaphoreType.DMA((7,)), pltpu.SemaphoreType.DMA((7,)), pltpu.SemaphoreType.DMA],
    )(shard)


def _grad_block(ref, mode, idx):
    if mode == "blk":
        return ref.at[idx]
    if mode == "col":
        size = ref.shape[1] // N_DEV
        return ref.at[:, pl.ds(pl.multiple_of(idx * size, size), size)]
    size = ref.shape[0] // N_DEV
    return ref.at[pl.ds(pl.multiple_of(idx * size, size), size), :]


def _grad_shard_shape(g, mode):
    if mode == "blk":
        return g.shape[1:]
    if mode == "col":
        return (g.shape[0], g.shape[1] // N_DEV)
    return (g.shape[0] // N_DEV, g.shape[1])


def _grads_job(groups, modes):
    flat = [(g, w, l) for w, per_w in enumerate(groups) for l, g in enumerate(per_w)]

    def build(ins, outs, send, recv, local):
        my_id = _logical(_my_place())
        return _copies_to_all(lambda a, dev: _grad_block(ins[a], modes[flat[a][1]], dev),
                              lambda a: outs[flat[a][1]].at[my_id, flat[a][2]],
                              len(flat), send, recv, local)

    out_shape = [jax.ShapeDtypeStruct((N_DEV, len(per_w)) + _grad_shard_shape(per_w[0], m), per_w[0].dtype)
                 for per_w, m in zip(groups, modes)]
    return _Exchange([g for g, _, _ in flat], out_shape, build)


def _adamw(w, g, m, v):
    m = ADAM_B1 * m + (1.0 - ADAM_B1) * g
    v = ADAM_B2 * v + (1.0 - ADAM_B2) * (g * g)
    m_hat = m / (1.0 - ADAM_B1 ** ADAM_STEP)
    v_hat = v / (1.0 - ADAM_B2 ** ADAM_STEP)
    delta = -ADAM_LR * (m_hat / (jnp.sqrt(v_hat) + ADAM_EPS) + ADAM_WD * w)
    return delta, m, v


def _sum_slots_adamw(slots, w, m, v, *, name):
    n_l = len(slots)
    R, C = slots[0].shape[1:]
    tr = _tile(R, 256)
    n_r = R // tr

    def body(*refs):
        s_refs = refs[:n_l]
        w_ref, m_ref, v_ref, g_out, d_out, m_out, v_out = refs[n_l:]
        for layer in range(n_l):
            @pl.when(pl.program_id(0) == layer)
            def _(s_ref=s_refs[layer]):
                g = s_ref[0].astype(F32)
                for s in range(1, N_DEV):
                    g = g + s_ref[s].astype(F32)
                delta, m_new, v_new = _adamw(w_ref[...], g, m_ref[...], v_ref[...])
                g_out[...] = g
                d_out[...] = delta
                m_out[...] = m_new
                v_out[...] = v_new

    slot_spec = lambda layer: pl.BlockSpec((N_DEV, tr, C), lambda l, i: (0, jnp.where(l == layer, i, 0), 0))
    spec = pl.BlockSpec((tr, C), lambda l, i: (l * n_r + i, 0))
    return pl.pallas_call(
        body, name=name, grid=(n_l, n_r),
        in_specs=[slot_spec(layer) for layer in range(n_l)] + [spec, spec, spec],
        out_specs=[spec] * 4,
        out_shape=[jax.ShapeDtypeStruct((n_l * R, C), F32)] * 4,
        compiler_params=_cparams(("arbitrary", "arbitrary")),
    )(*slots, w, m, v)


def _small_allreduce_adamw(g, w, m, v, *, name):
    R = g.shape[0]
    vmem = pl.BlockSpec(memory_space=pltpu.VMEM)

    def body(g_ref, w_ref, m_ref, v_ref, g_out, d_out, m_out, v_out, slots, send, recv):
        me = _my_place()
        my_id = _logical(me)
        slots[my_id] = g_ref[...]
        copies = []
        for k in range(1, N_DEV):
            cp = pltpu.make_async_remote_copy(
                src_ref=g_ref, dst_ref=slots.at[my_id], send_sem=send.at[k], recv_sem=recv.at[k],
                device_id=_peer(me, k), device_id_type=MESH)
            cp.start()
            copies.append(cp)
        for cp in copies:
            cp.wait()
        total = slots[0]
        for s in range(1, N_DEV):
            total = total + slots[s]
        delta, m_new, v_new = _adamw(w_ref[...], total, m_ref[...], v_ref[...])
        g_out[...] = total
        d_out[...] = delta
        m_out[...] = m_new
        v_out[...] = v_new

    return pl.pallas_call(
        body, name=name,
        in_specs=[vmem] * 4, out_specs=[vmem] * 4,
        out_shape=[jax.ShapeDtypeStruct((R, LANES), F32)] * 4,
        scratch_shapes=[pltpu.VMEM((N_DEV, R, LANES), F32),
                        pltpu.SemaphoreType.DMA((N_DEV,)), pltpu.SemaphoreType.DMA((N_DEV,))],
    )(g, w, m, v)


def _pack(parts):
    flat = jnp.concatenate([p.reshape(-1) for p in parts])
    rows = -(-flat.shape[0] // (8 * LANES)) * 8
    return jnp.pad(flat, (0, rows * LANES - flat.shape[0])).reshape(rows, LANES)


def _unpack(packed, like):
    flat = packed.reshape(-1)
    out, pos = [], 0
    for p in like:
        out.append(flat[pos:pos + p.size].reshape(p.shape))
        pos += p.size
    return out


def kernel(x, w_in, b_gate, rel_bias, w_proj_a, w_proj_b, w_out, ln1_g, ln1_b, w_ffn_in, w_ffn_out, ln2_g, ln2_b, loss_target, m_w_in, m_b_gate, m_rel_bias, m_w_proj_a, m_w_proj_b, m_w_out, m_ln1_g, m_ln1_b, m_w_ffn_in, m_w_ffn_out, m_ln2_g, m_ln2_b, v_w_in, v_b_gate, v_rel_bias, v_w_proj_a, v_w_proj_b, v_w_out, v_ln1_g, v_ln1_b, v_w_ffn_in, v_w_ffn_out, v_ln2_g, v_ln2_b):
    L = w_in.shape[0]
    T, D = x.shape[1], x.shape[2]
    alpha = float((2 * L) ** 0.25)
    n_qkv = 6 * WIDTH

    big = [w_in, w_proj_a, w_proj_b, w_out, w_ffn_in, w_ffn_out]
    kinds = ["in", "pa", "pb", "o", "fi", "fo"]
    modes = ["col", "col", "col", "row", "blk", "row"]
    mode_of = dict(zip(kinds, modes))
    w_bf = dict(zip(kinds, [w.astype(BF16) for w in big]))

    def gather_of(ks, l):
        return _gather_job([w_bf[k][l:l + 1] for k in ks], [mode_of[k] for k in ks])

    W = [dict() for _ in range(L)]
    W[0]["in"] = _gather_via_sibling(w_bf["in"][:1], mode_of["in"], name="gather_w_in_first")
    vec3 = lambda a: a[:, None, :]
    bg3, l1g, l1b, l2g, l2b = vec3(b_gate), vec3(ln1_g), vec3(ln1_b), vec3(ln2_g), vec3(ln2_b)
    b_col0 = 3 * WIDTH // LANES

    h = x[0]
    saved = []
    for l in range(L):
        ahead = l + 1 < L
        soon = ["pa", "pb", "o", "fo"]
        (qkv, gates), got = _in_proj(h, W[l]["in"], 0, n_qkv=n_qkv, name=f"in_proj_{l}",
                                     job=gather_of(soon, 0) if l == 0 else None)
        W[l].update(zip(soon, got))
        kvpad = jnp.pad(qkv[:, WIDTH:3 * WIDTH], ((A_WIN - A_TQ, 0), (0, 0)))
        bias = _toeplitz_bias(rel_bias[l])
        oa, got = _attn_a_fwd(qkv, kvpad, bias, name=f"attn_a_fwd_{l}", job=gather_of(["fi"], 0) if l == 0 else None)
        W[l].update(zip(["fi"], got))
        early = ["in", "pa", "pb", "o"]
        ob, got = _attn_b_fwd(qkv, col0=b_col0, name=f"attn_b_fwd_{l}", job=gather_of(early, l + 1) if ahead else None)
        W[l + 1 if ahead else l].update(zip(early, got))
        x1, r1, x1b = _proj_fwd(oa, ob, gates, bg3, W[l]["pa"], W[l]["pb"], W[l]["o"], 0, h, l1g, l1b, l,
                           alpha=alpha, name=f"proj_fwd_{l}")
        (gs, us, r2, x2), got = _ffn_fwd(x1, W[l]["fi"], W[l]["fo"], 0, l2g, l2b, l, alpha=alpha, name=f"ffn_fwd_{l}",
                                         job=gather_of(["fi", "fo"], l + 1) if ahead else None)
        W[l + 1 if ahead else l].update(zip(["fi", "fo"], got))
        saved.append((h, qkv, gates, kvpad, bias, oa, ob, x1b, r1, gs, us, r2))
        h = x2

    d_h, sq = _loss_head(h, loss_target[0], name="loss_head")
    loss = lax.psum((0.5 / D) * jnp.sum(sq), ("x", "y", "c"))

    g_bg, g_rb, g_l1g, g_l1b, g_l2g, g_l2b = ([None] * L for _ in range(6))
    slot = {k: [None] * L for k in kinds}

    def exchange_of(ks, grads):
        return _grads_job([[g] for g in grads], [mode_of[k] for k in ks])

    w_in_above = None
    for l in reversed(range(L)):
        xin, qkv, gates, kvpad, bias, oa, ob, x1b, r1, gs, us, r2 = saved[l]
        (dr2, act, dgt, dup, dx1, g_l2g[l], g_l2b[l]), got = _ffn_bwd(
            d_h, r2, l2g, gs, us, W[l]["fi"], W[l]["fo"], 0, l, alpha=alpha, name=f"ffn_bwd_{l}", job=w_in_above)
        if w_in_above is not None:
            (slot["in"][l + 1],) = got
        g_fo = _mm_tn_blocked_a(act, dr2, name=f"grad_w_ffn_out_{l}").reshape(-1, D)
        g_fi = jnp.concatenate(_mm_tn_blocked_pair(x1b, dgt, dup, name=f"grad_w_ffn_in_{l}"), axis=0)
        (dr1, mixin, dya, dyb, dgates, doa, dob, g_l1g[l], g_l1b[l], g_bg[l]) = _proj_bwd(
            dx1, r1, l1g, oa, ob, gates, bg3, W[l]["pa"], W[l]["pb"], W[l]["o"], 0, l, name=f"proj_bwd_{l}")
        g_o = _mm_tn(mixin, dr1, tm=_tile(D, 1024), tn=_tile(D, 1024), name=f"grad_w_out_{l}")
        g_pa = _mm_tn(oa, dya, tm=WIDTH, tn=_tile(D, 1024), name=f"grad_w_proj_a_{l}")
        g_pb = _mm_tn(ob, dyb, tm=WIDTH, tn=_tile(D, 1024), name=f"grad_w_proj_b_{l}")
        (dqa, dka, dva, dbias), (slot["fi"][l], slot["fo"][l]) = _attn_a_bwd(
            qkv, kvpad, bias, doa, name=f"attn_a_bwd_{l}", job=exchange_of(["fi", "fo"], [g_fi, g_fo]))
        g_rb[l] = _toeplitz_bias_grad(dbias)
        (dqb, dkb, dvb), (slot["pa"][l], slot["pb"][l], slot["o"][l]) = _attn_b_bwd(
            qkv, ob, dob, col0=b_col0, name=f"attn_b_bwd_{l}", job=exchange_of(["pa", "pb", "o"], [g_pa, g_pb, g_o]))
        d_pre = jnp.concatenate([dqa, dka, dva, dqb, dkb, dvb, dgates], axis=1)
        g_in = _mm_tn(xin, d_pre, tm=D, tn=4 * w_in.shape[2], name=f"grad_w_in_{l}")
        w_in_above = exchange_of(["in"], [g_in])
        d_h, got = _mm_nt_add(d_pre, W[l]["in"], 0, dr1, alpha, name=f"grad_x_{l}", job=w_in_above if l == 0 else None)
        if l == 0:
            (slot["in"][0],) = got
    grad_x = d_h[None]

    moments_m = [m_w_in, m_w_proj_a, m_w_proj_b, m_w_out, m_w_ffn_in, m_w_ffn_out]
    moments_v = [v_w_in, v_w_proj_a, v_w_proj_b, v_w_out, v_w_ffn_in, v_w_ffn_out]
    names = ["w_in", "w_proj_a", "w_proj_b", "w_out", "w_ffn_in", "w_ffn_out"]
    big_out = {}
    for nm, k, w, m, v in zip(names, kinds, big, moments_m, moments_v):
        two = lambda a: a.reshape(-1, a.shape[-1])
        per_layer = [s.reshape(N_DEV, -1, s.shape[-1]) for s in slot[k]]
        res = _sum_slots_adamw(per_layer, two(w), two(m), two(v), name=f"adamw_{nm}")
        big_out[nm] = [r.reshape(w.shape) for r in res]

    small_w = [b_gate, rel_bias, ln1_g, ln1_b, ln2_g, ln2_b]
    small_g = [jnp.stack(g) for g in (g_bg, g_rb, g_l1g, g_l1b, g_l2g, g_l2b)]
    small_m = [m_b_gate, m_rel_bias, m_ln1_g, m_ln1_b, m_ln2_g, m_ln2_b]
    small_v = [v_b_gate, v_rel_bias, v_ln1_g, v_ln1_b, v_ln2_g, v_ln2_b]
    res = _small_allreduce_adamw(_pack(small_g), _pack(small_w), _pack(small_m), _pack(small_v),
                                 name="allreduce_small_adamw")
    small_names = ["b_gate", "rel_bias", "ln1_g", "ln1_b", "ln2_g", "ln2_b"]
    small_out = {nm: [] for nm in small_names}
    for packed in res:
        for nm, arr in zip(small_names, _unpack(packed, small_w)):
            small_out[nm].append(arr)

    order = ["w_in", "b_gate", "rel_bias", "w_proj_a", "w_proj_b", "w_out", "ln1_g", "ln1_b",
             "w_ffn_in", "w_ffn_out", "ln2_g", "ln2_b"]
    every = {**big_out, **small_out}
    outs = [loss, grad_x]
    for kind in range(4):
        outs += [every[nm][kind] for nm in order]
    return tuple(outs)
```

```python
import functools
import math

import jax
import jax.numpy as jnp
import numpy as np
from jax import lax
from jax.experimental import pallas as pl
from jax.experimental.pallas import tpu as pltpu

F32 = jnp.float32
BF16 = jnp.bfloat16

HEAD_DIM = 64
CHUNK = 64
LEFT_CHUNKS = 8
REL_CLIP = 256
N_REL = 2 * REL_CLIP + 1
WIDTH = 512
LANES = 128
A_TQ = 256
A_WIN = A_TQ + LEFT_CHUNKS * CHUNK
A_STRIP = 128
B_TQ = 512
B_TS = 256
B_PIECE = 64
B_DEAD = -160.0
LN_EPS = 1e-5
QK_SCALE = 1.0 / math.sqrt(HEAD_DIM)
LOG2E = 1.4426950408889634
NEG = -1e30

ADAM_LR = 0.001
ADAM_B1 = 0.9
ADAM_B2 = 0.999
ADAM_EPS = 1e-08
ADAM_WD = 0.01
ADAM_STEP = 10

N_DEV = 8
MESH = pl.DeviceIdType.MESH
MIB = 1024 * 1024


def _cparams(sem=None, vmem_mib=48):
    return pltpu.CompilerParams(dimension_semantics=sem, vmem_limit_bytes=vmem_mib * MIB)


def _dot(a, b):
    return jnp.dot(a, b, preferred_element_type=F32)


def _dot_nt(a, b):
    return lax.dot_general(a, b, (((1,), (1,)), ((), ())), preferred_element_type=F32)


def _dot_tn(a, b):
    return lax.dot_general(a, b, (((0,), (0,)), ((), ())), preferred_element_type=F32)


def _tile(n, pref):
    if n <= pref:
        return n
    for t in range(pref - pref % 8, 0, -8):
        if n % t == 0:
            return t
    raise ValueError((n, pref))


def _in_proj(a, w, layer, *, n_qkv, name, job=None):
    M, K = a.shape
    N = w.shape[2]
    tm = _tile(M, 1024)
    tn = 1024
    assert n_qkv % tn == 0 and (N - n_qkv) % tn == 0
    n_q = n_qkv // tn

    def body(a_ref, w_ref, q_ref, g_ref, ab_ref):
        j = pl.program_id(1)

        @pl.when(j == 0)
        def _():
            ab_ref[...] = a_ref[...].astype(BF16)

        res = _dot(ab_ref[...], w_ref[...])

        @pl.when(j < n_q)
        def _():
            q_ref[...] = res.astype(BF16)

        @pl.when(j >= n_q)
        def _():
            g_ref[...] = res

    return _call_carrying(
        job, body, name=name, grid=(M // tm, N // tn),
        in_specs=[pl.BlockSpec((tm, K), lambda i, j: (i, 0)),
                  pl.BlockSpec((None, K, tn), lambda i, j: (layer, 0, j))],
        out_specs=[pl.BlockSpec((tm, tn), lambda i, j: (i, jnp.minimum(j, n_q - 1))),
                   pl.BlockSpec((tm, tn), lambda i, j: (i, jnp.maximum(j - n_q, 0)))],
        out_shape=[jax.ShapeDtypeStruct((M, n_qkv), BF16), jax.ShapeDtypeStruct((M, N - n_qkv), F32)],
        scratch_shapes=[pltpu.VMEM((tm, K), BF16)], vmem_mib=48, args=(a, w))


def _mm_nt_add(a, w, layer, add, add_scale, *, name, job=None):
    M, K = a.shape
    N = w.shape[1]
    tm = _tile(M, 1024)
    tk = _tile(K, 1024)

    def body(a_ref, w_ref, add_ref, o_ref):
        @pl.when(pl.program_id(1) == 0)
        def _():
            o_ref[...] = add_scale * add_ref[...]
        o_ref[...] += _dot_nt(a_ref[...], w_ref[...])

    (out,), rode = _call_carrying(
        job, body, name=name, grid=(M // tm, K // tk),
        in_specs=[pl.BlockSpec((tm, tk), lambda i, k: (i, k)),
                  pl.BlockSpec((None, N, tk), lambda i, k: (layer, 0, k)),
                  pl.BlockSpec((tm, N), lambda i, k: (i, 0))],
        out_specs=[pl.BlockSpec((tm, N), lambda i, k: (i, 0))],
        out_shape=[jax.ShapeDtypeStruct((M, N), F32)],
        scratch_shapes=[], vmem_mib=48, args=(a, w, add))
    return out, rode


def _pieces_layout(pieces, wide):
    P = pieces[0].shape[1]
    assert all(p.shape[1] == P for p in pieces) and wide.shape[1] % P == 0
    n_blocks = len(pieces) + wide.shape[1] // P
    assert n_blocks % 2 == 0 and len(pieces) > n_blocks // 2
    return P, n_blocks // 2


def _mm_tn_pieces(a, pieces, wide, *, name):
    T, M = a.shape
    P, n_half = _pieces_layout(pieces, wide)
    n_first = n_half
    n_rest = len(pieces) - n_first
    tk = _tile(T, 1024)
    n_k = T // tk

    def body(*refs):
        a_ref, p_refs, wide_ref = refs[0], refs[1:1 + len(pieces)], refs[1 + len(pieces)]
        o_ref, acc_ref = refs[2 + len(pieces):]
        j, k = pl.program_id(0), pl.program_id(1)

        @pl.when(k == 0)
        def _():
            acc_ref[...] = jnp.zeros_like(acc_ref)

        x = a_ref[...].astype(BF16)

        @pl.when(j == 0)
        def _():
            for p in range(n_first):
                acc_ref[:, p * P:(p + 1) * P] += _dot_tn(x, p_refs[p][...])

        @pl.when(j == 1)
        def _():
            for p in range(n_rest):
                acc_ref[:, p * P:(p + 1) * P] += _dot_tn(x, p_refs[n_first + p][...])
            acc_ref[:, n_rest * P:] += _dot_tn(x, wide_ref[...])

        @pl.when(k == n_k - 1)
        def _():
            o_ref[...] = acc_ref[...].astype(o_ref.dtype)

    first = lambda: pl.BlockSpec((tk, P), lambda j, k: (jnp.where(j == 0, k, n_k - 1), 0))
    rest = lambda: pl.BlockSpec((tk, P), lambda j, k: (jnp.where(j == 1, k, 0), 0))
    return pl.pallas_call(
        body, name=name, grid=(2, n_k),
        in_specs=[pl.BlockSpec((tk, M), lambda j, k: (k, 0))] + [first() for _ in range(n_first)]
                 + [rest() for _ in range(n_rest)]
                 + [pl.BlockSpec((tk, wide.shape[1]), lambda j, k: (jnp.where(j == 1, k, 0), 0))],
        out_specs=pl.BlockSpec((M, n_half * P), lambda j, k: (0, j)),
        out_shape=jax.ShapeDtypeStruct((M, 2 * n_half * P), BF16),
        scratch_shapes=[pltpu.VMEM((M, n_half * P), F32)],
        compiler_params=_cparams(("arbitrary", "arbitrary"), 56),
    )(a, *pieces, wide)


def _mm_nt_pieces_add(pieces, wide, w, layer, add, add_scale, *, name, job=None):
    M = add.shape[0]
    N = w.shape[1]
    P, _ = _pieces_layout(pieces, wide)
    tm = _tile(M, 1024)
    tk = 2 * P
    n_pair = len(pieces) // 2
    assert len(pieces) % 2 == 0 and wide.shape[1] % tk == 0
    n_k = n_pair + wide.shape[1] // tk

    def body(*refs):
        p_refs, wide_ref, w_ref, add_ref, o_ref = refs[:len(pieces)], *refs[len(pieces):]
        k = pl.program_id(1)

        @pl.when(k == 0)
        def _():
            o_ref[...] = add_scale * add_ref[...]

        for pair in range(n_pair):
            @pl.when(k == pair)
            def _(pair=pair):
                o_ref[...] += (_dot_nt(p_refs[2 * pair][...], w_ref[:, :P])
                               + _dot_nt(p_refs[2 * pair + 1][...], w_ref[:, P:]))

        @pl.when(k >= n_pair)
        def _():
            o_ref[...] += _dot_nt(wide_ref[...], w_ref[...])

    (out,), rode = _call_carrying(
        job, body, name=name, grid=(M // tm, n_k),
        in_specs=[pl.BlockSpec((tm, P), lambda i, k: (i, 0)) for _ in pieces]
                 + [pl.BlockSpec((tm, tk), lambda i, k: (i, jnp.maximum(k - n_pair, 0))),
                    pl.BlockSpec((None, N, tk), lambda i, k: (layer, 0, k)),
                    pl.BlockSpec((tm, N), lambda i, k: (i, 0))],
        out_specs=[pl.BlockSpec((tm, N), lambda i, k: (i, 0))],
        out_shape=[jax.ShapeDtypeStruct((M, N), F32)],
        scratch_shapes=[], vmem_mib=56, args=(*pieces, wide, w, add))
    return out, rode


def _tn_body(k_axis, n_k):
    def body(a_ref, b_ref, o_ref, acc_ref):
        k = pl.program_id(k_axis)

        @pl.when(k == 0)
        def _():
            acc_ref[...] = jnp.zeros_like(acc_ref)
        acc_ref[...] += _dot_tn(a_ref[...].astype(BF16), b_ref[...].astype(BF16))

        @pl.when(k == n_k - 1)
        def _():
            o_ref[...] = acc_ref[...].astype(o_ref.dtype)
    return body


def _mm_tn(a, b, *, tm, tn, name):
    T, M = a.shape
    N = b.shape[1]
    tk = _tile(T, 1024)
    return pl.pallas_call(
        _tn_body(2, T // tk), name=name, grid=(M // tm, N // tn, T // tk),
        in_specs=[pl.BlockSpec((tk, tm), lambda i, j, k: (k, i)),
                  pl.BlockSpec((tk, tn), lambda i, j, k: (k, j))],
        out_specs=pl.BlockSpec((tm, tn), lambda i, j, k: (i, j)),
        out_shape=jax.ShapeDtypeStruct((M, N), BF16),
        scratch_shapes=[pltpu.VMEM((tm, tn), F32)],
        compiler_params=_cparams(("parallel", "parallel", "arbitrary")),
    )(a, b)


def _mm_tn_blocked_pair(a, b1, b2, *, name):
    T, M = a.shape
    S, _, N = b1.shape
    tk = _tile(T, 1024)
    n_k = T // tk

    def body(a_ref, b1_ref, b2_ref, o1_ref, o2_ref, acc1_ref, acc2_ref):
        k = pl.program_id(1)

        @pl.when(k == 0)
        def _():
            acc1_ref[...] = jnp.zeros_like(acc1_ref)
            acc2_ref[...] = jnp.zeros_like(acc2_ref)

        a_t = a_ref[...].astype(BF16)
        acc1_ref[...] += _dot_tn(a_t, b1_ref[...])
        acc2_ref[...] += _dot_tn(a_t, b2_ref[...])

        @pl.when(k == n_k - 1)
        def _():
            o1_ref[...] = acc1_ref[...].astype(o1_ref.dtype)
            o2_ref[...] = acc2_ref[...].astype(o2_ref.dtype)

    blk = lambda: pl.BlockSpec((None, tk, N), lambda s, k: (s, k, 0))
    out = lambda: pl.BlockSpec((None, M, N), lambda s, k: (s, 0, 0))
    return pl.pallas_call(
        body, name=name, grid=(S, n_k),
        in_specs=[pl.BlockSpec((tk, M), lambda s, k: (k, 0)), blk(), blk()],
        out_specs=[out(), out()],
        out_shape=[jax.ShapeDtypeStruct((S, M, N), BF16)] * 2,
        scratch_shapes=[pltpu.VMEM((M, N), F32), pltpu.VMEM((M, N), F32)],
        compiler_params=_cparams(("parallel", "arbitrary")),
    )(a, b1, b2)


def _mm_tn_blocked_a(a, b, *, name):
    S, T, M = a.shape
    N = b.shape[1]
    tk = _tile(T, 1024)
    return pl.pallas_call(
        _tn_body(1, T // tk), name=name, grid=(S, T // tk),
        in_specs=[pl.BlockSpec((None, tk, M), lambda s, k: (s, k, 0)),
                  pl.BlockSpec((tk, N), lambda s, k: (k, 0))],
        out_specs=pl.BlockSpec((None, M, N), lambda s, k: (s, 0, 0)),
        out_shape=jax.ShapeDtypeStruct((S, M, N), BF16),
        scratch_shapes=[pltpu.VMEM((M, N), F32)],
        compiler_params=_cparams(("parallel", "arbitrary")),
    )(a, b)


def _ln_fwd(r, g, b):
    mu = jnp.mean(r, axis=-1, keepdims=True)
    xc = r - mu
    var = jnp.mean(xc * xc, axis=-1, keepdims=True)
    return xc * lax.rsqrt(var + LN_EPS) * g + b


def _ln_bwd(dy, r, g):
    mu = jnp.mean(r, axis=-1, keepdims=True)
    xc = r - mu
    var = jnp.mean(xc * xc, axis=-1, keepdims=True)
    rstd = lax.rsqrt(var + LN_EPS)
    xhat = xc * rstd
    dxh = dy * g
    m1 = jnp.mean(dxh, axis=-1, keepdims=True)
    m2 = jnp.mean(dxh * xhat, axis=-1, keepdims=True)
    return rstd * (dxh - m1 - xhat * m2), xhat


def _lane_is_head0():
    return lax.broadcasted_iota(jnp.int32, (1, LANES), 1) < HEAD_DIM


def _band_shape():
    a = np.arange(A_TQ)[:, None] // CHUNK
    b = np.arange(A_WIN)[None, :] // CHUNK
    return (b >= a) & (b <= a + LEFT_CHUNKS)


def _band_streams(strip):
    return [dict(h=h, n=strip, rows=pl.ds(r0, strip)) for h in range(2) for r0 in range(0, A_TQ, strip)]


def _band_scores(st, i, qh_ref, k2, bias_ref):
    s = _dot_nt(qh_ref[st["h"], st["rows"], :], k2) + bias_ref[st["h"], st["rows"], :]
    c = lax.broadcasted_iota(jnp.int32, (st["n"], A_WIN), 1)
    st["s"] = jnp.where(c >= LEFT_CHUNKS * CHUNK - i * A_TQ, s, NEG)


def _band_softmax(st):
    s = st.pop("s")
    e = jnp.exp(s - jnp.max(s, axis=1, keepdims=True))
    st["p"] = e * (1.0 / jnp.sum(e, axis=1, keepdims=True))


def _attn_a_fwd(qkv, kvpad, bias, *, name, job=None):
    T = qkv.shape[0]
    n_hp = WIDTH // LANES

    def body(q_ref, k_ref, v_ref, bias_ref, o_ref, qh_ref, acc_ref):
        i = pl.program_id(1)
        row0 = pl.multiple_of(i * A_TQ, A_TQ)
        q2 = q_ref[...] * jnp.asarray(QK_SCALE, BF16)
        k2 = k_ref[pl.ds(row0, A_WIN), :]
        v2 = v_ref[pl.ds(row0, A_WIN), :]
        head0 = _lane_is_head0()
        qh_ref[0] = jnp.where(head0, q2, jnp.zeros_like(q2))
        qh_ref[1] = jnp.where(head0, jnp.zeros_like(q2), q2)

        def scores(st):
            _band_scores(st, i, qh_ref, k2, bias_ref)

        def values(st):
            acc_ref[st["h"], st["rows"], :] = _dot(st.pop("p").astype(BF16), v2)

        _skewed(_band_streams(A_STRIP), [scores, _band_softmax, values])
        o_ref[...] = jnp.where(head0, acc_ref[0], acc_ref[1]).astype(o_ref.dtype)

    (out,), rode = _call_carrying(
        job, body, name=name, grid=(n_hp, T // A_TQ),
        in_specs=[pl.BlockSpec((A_TQ, LANES), lambda hp, i: (i, hp)),
                  pl.BlockSpec((T + A_WIN - A_TQ, LANES), lambda hp, i: (0, hp)),
                  pl.BlockSpec((T + A_WIN - A_TQ, LANES), lambda hp, i: (0, hp + n_hp)),
                  pl.BlockSpec((2, A_TQ, A_WIN), lambda hp, i: (hp, 0, 0))],
        out_specs=[pl.BlockSpec((A_TQ, LANES), lambda hp, i: (i, hp))],
        out_shape=[jax.ShapeDtypeStruct((T, WIDTH), BF16)],
        scratch_shapes=[pltpu.VMEM((2, A_TQ, LANES), BF16), pltpu.VMEM((2, A_TQ, LANES), F32)],
        vmem_mib=48, args=(qkv, kvpad, kvpad, bias))
    return out, rode


def _attn_a_bwd(qkv, kvpad, bias, do, *, name, job=None):
    T = qkv.shape[0]
    TP = T + A_WIN - A_TQ
    n_hp = WIDTH // LANES

    def body(q_ref, k_ref, v_ref, bias_ref, do_ref, dq_ref, dko_ref, dvo_ref, db_ref,
             qh_ref, doh_ref, dqa_ref, dk_ref, dv_ref, qs_ref):
        i = pl.program_id(1)

        @pl.when(i == 0)
        def _():
            dk_ref[...] = jnp.zeros_like(dk_ref)
            dv_ref[...] = jnp.zeros_like(dv_ref)
            db_ref[...] = jnp.zeros_like(db_ref)

        row0 = pl.multiple_of(i * A_TQ, A_TQ)
        window = pl.ds(row0, A_WIN)
        scale = jnp.asarray(QK_SCALE, BF16)
        q2 = q_ref[...] * scale
        do2 = do_ref[...]
        k2 = k_ref[window, :]
        k2s = k2 * scale
        v2 = v_ref[window, :]
        head0 = _lane_is_head0()
        zero = jnp.zeros_like(q2)
        qs_ref[...] = q2
        qh_ref[0] = jnp.where(head0, q2, zero)
        qh_ref[1] = jnp.where(head0, zero, q2)
        doh_ref[0] = jnp.where(head0, do2, zero)
        doh_ref[1] = jnp.where(head0, zero, do2)
        dk = [[], []]
        dv = [[], []]

        def scores(st):
            _band_scores(st, i, qh_ref, k2, bias_ref)
            st["dp"] = _dot_nt(doh_ref[st["h"], st["rows"], :], v2)

        def dscores(st):
            _band_softmax(st)
            p, dp = st.pop("p"), st.pop("dp")
            ds = p * (dp - jnp.sum(p * dp, axis=1, keepdims=True))
            db_ref[st["h"], st["rows"], :] += ds
            st["dsb"] = ds.astype(BF16)
            st["pb"] = p.astype(BF16)

        def grads(st):
            h, rows = st["h"], st["rows"]
            dsb = st.pop("dsb")
            dqa_ref[h, rows, :] = _dot(dsb, k2s)
            dk[h].append(_dot_tn(dsb, qs_ref[rows, :]))
            dv[h].append(_dot_tn(st.pop("pb"), do_ref[rows, :]))

        _skewed(_band_streams(A_TQ), [scores, dscores, grads])
        dq_ref[...] = jnp.where(head0, dqa_ref[0], dqa_ref[1]).astype(dq_ref.dtype)
        dk_ref[window, :] += jnp.where(head0, sum(dk[0]), sum(dk[1]))
        dv_ref[window, :] += jnp.where(head0, sum(dv[0]), sum(dv[1]))

        @pl.when(i == T // A_TQ - 1)
        def _():
            dko_ref[...] = dk_ref[TP - T:, :].astype(BF16)
            dvo_ref[...] = dv_ref[TP - T:, :].astype(BF16)

    return _call_carrying(
        job, body, name=name, grid=(n_hp, T // A_TQ),
        in_specs=[pl.BlockSpec((A_TQ, LANES), lambda hp, i: (i, hp)),
                  pl.BlockSpec((TP, LANES), lambda hp, i: (0, hp)),
                  pl.BlockSpec((TP, LANES), lambda hp, i: (0, hp + n_hp)),
                  pl.BlockSpec((2, A_TQ, A_WIN), lambda hp, i: (hp, 0, 0)),
                  pl.BlockSpec((A_TQ, LANES), lambda hp, i: (i, hp))],
        out_specs=[pl.BlockSpec((A_TQ, LANES), lambda hp, i: (i, hp)),
                   pl.BlockSpec((T, LANES), lambda hp, i: (0, hp)),
                   pl.BlockSpec((T, LANES), lambda hp, i: (0, hp)),
                   pl.BlockSpec((2, A_TQ, A_WIN), lambda hp, i: (hp, 0, 0))],
        out_shape=[jax.ShapeDtypeStruct((T, WIDTH), BF16),
                   jax.ShapeDtypeStruct((T, WIDTH), BF16),
                   jax.ShapeDtypeStruct((T, WIDTH), BF16),
                   jax.ShapeDtypeStruct((WIDTH // HEAD_DIM, A_TQ, A_WIN), F32)],
        scratch_shapes=[pltpu.VMEM((2, A_TQ, LANES), BF16), pltpu.VMEM((2, A_TQ, LANES), BF16),
                        pltpu.VMEM((2, A_TQ, LANES), F32),
                        pltpu.VMEM((TP, LANES), F32), pltpu.VMEM((TP, LANES), F32),
                        pltpu.VMEM((A_TQ, LANES), BF16)],
        vmem_mib=56, args=(qkv, kvpad, kvpad, bias, do))


_N_DIAG = 2 * CHUNK - 1
_EXT_TOP = LEFT_CHUNKS * CHUNK + CHUNK - 1 + REL_CLIP


def _toeplitz_bias(rb):
    H = rb.shape[0]
    ext = jnp.concatenate([rb, jnp.broadcast_to(rb[:, N_REL - 1:], (H, _EXT_TOP + 1 - N_REL))], axis=1)
    vec = jnp.stack([ext[:, _EXT_TOP - (_N_DIAG - 1) - CHUNK * k:_EXT_TOP - CHUNK * k + 1]
                     for k in range(LEFT_CHUNKS + 1)], axis=1)
    rev = jnp.pad(vec[:, :, ::-1], ((0, 0), (0, 0), (0, 1)))
    flat = jnp.broadcast_to(rev[:, :, None, :], (H, LEFT_CHUNKS + 1, CHUNK, _N_DIAG + 1))
    skew = flat.reshape(H, LEFT_CHUNKS + 1, -1)[:, :, :CHUNK * _N_DIAG].reshape(H, LEFT_CHUNKS + 1, CHUNK, _N_DIAG)
    blocks = skew[:, :, :, CHUNK - 1:]
    neg = jnp.full((H, CHUNK, CHUNK), NEG, F32)
    rows = [jnp.concatenate([blocks[:, b - a] if 0 <= b - a <= LEFT_CHUNKS else neg for b in range(A_WIN // CHUNK)],
                            axis=2) for a in range(A_TQ // CHUNK)]
    return jnp.concatenate(rows, axis=1)


def _toeplitz_bias_grad(db):
    H = db.shape[0]
    d5 = db.reshape(H, A_TQ // CHUNK, CHUNK, A_WIN // CHUNK, CHUNK)
    g_blocks = jnp.stack([sum(d5[:, a, :, a + k, :] for a in range(A_TQ // CHUNK))
                          for k in range(LEFT_CHUNKS + 1)], axis=1)
    d_skew = jnp.pad(g_blocks, ((0, 0), (0, 0), (0, 0), (CHUNK - 1, 0)))
    d_flat = jnp.pad(d_skew.reshape(H, LEFT_CHUNKS + 1, CHUNK * _N_DIAG), ((0, 0), (0, 0), (0, CHUNK)))
    g_vec = jnp.sum(d_flat.reshape(H, LEFT_CHUNKS + 1, CHUNK, _N_DIAG + 1), axis=2)[:, :, :_N_DIAG][:, :, ::-1]
    g_ext = sum(jnp.pad(g_vec[:, k], ((0, 0), (_EXT_TOP - (_N_DIAG - 1) - CHUNK * k, CHUNK * k)))
                for k in range(LEFT_CHUNKS + 1))
    return jnp.concatenate([g_ext[:, :N_REL - 1], jnp.sum(g_ext[:, N_REL - 1:], axis=1, keepdims=True)], axis=1)


def _split_bf16(x):
    hi = x.astype(BF16)
    lo = (x - hi.astype(F32)).astype(BF16)
    return hi, lo


def _sb_streams(d, strips=None, **tile):
    out = []
    for h in range(2):
        for r in (range(B_TQ // B_TS) if strips is None else strips):
            if d is not None and d > r:
                continue
            out.append(dict(h=h, r=r, rows=pl.ds(r * B_TS, B_TS), diag=(d is not None and d == r), **tile))
    return out


def _sb_sweep(i, car_ref, streams_of, run):
    sub = B_TQ // B_TS
    run([st for d in reversed(range(sub)) for st in streams_of(i * sub + d, d, None)])

    def alive(c):
        return (c[0] < i * sub) & (c[1] > B_DEAD)

    def step(c):
        kb = i * sub - 1 - c[0]
        if sub > 1:
            lower_alive = jnp.max(car_ref[:, B_TS:, :]) > B_DEAD
            lax.cond(lower_alive, lambda: run(streams_of(kb, None, None)), lambda: run(streams_of(kb, None, [0])))
        else:
            run(streams_of(kb, None, None))
        return c[0] + 1, jnp.max(car_ref[...])

    lax.while_loop(alive, step, (jnp.int32(0), jnp.float32(0.0)))


def _piece_rows(st, p):
    return pl.ds(st["r"] * B_TS + p, B_PIECE)


def _rows_cat(parts):
    return jnp.concatenate(parts, axis=0)


def _skewed(streams, stages):
    for t in range(len(streams) + len(stages) - 1):
        for s, st in enumerate(streams):
            if 0 <= t - s < len(stages):
                stages[t - s](st)


def _sb_logs(st, z2):
    log_beta, log_keep, keep_bf = [], [], []
    for p in range(0, B_TS, B_PIECE):
        z = z2[p:p + B_PIECE]
        lp2 = jnp.log(1.0 + jnp.exp2(-jnp.abs(z))) * LOG2E
        lb = jnp.minimum(z, 0.0) - lp2
        lk = lb - z
        if st["diag"]:
            lk = jnp.where(_strict_lower(p), lk, 0.0)
        log_beta.append(lb)
        log_keep.append(lk)
        keep_bf.append(lk.astype(BF16))
    st["log_beta"] = _rows_cat(log_beta)
    st["log_keep"] = _rows_cat(log_keep)
    st["keep_bf"] = _rows_cat(keep_bf)


def _strict_lower(p):
    t = p + lax.broadcasted_iota(jnp.int32, (B_PIECE, B_TS), 0)
    s = lax.broadcasted_iota(jnp.int32, (B_PIECE, B_TS), 1)
    return s < t


def _tri(strict):
    j = lax.broadcasted_iota(jnp.int32, (B_TS, B_TS), 0)
    s = lax.broadcasted_iota(jnp.int32, (B_TS, B_TS), 1)
    return jnp.where(j > s if strict else j >= s, 1.0, 0.0).astype(BF16)


def _call_carrying(job, body, *, name, grid, in_specs, out_specs, out_shape, scratch_shapes, vmem_mib, args):
    n_in, n_out, n_scr = len(in_specs), len(out_specs), len(scratch_shapes)
    if job is None:
        res = pl.pallas_call(body, name=name, grid=grid, in_specs=in_specs, out_specs=out_specs,
                             out_shape=out_shape, scratch_shapes=scratch_shapes,
                             compiler_params=_cparams(("arbitrary",) * len(grid), vmem_mib))(*args)
        return res, []
    j_in, j_out = len(job.arrays), len(job.out_shape)
    hbm = pl.BlockSpec(memory_space=pltpu.HBM)

    def carrying(*refs):
        refs = list(refs)
        ins, refs = refs[:n_in], refs[n_in:]
        j_ins, refs = refs[:j_in], refs[j_in:]
        outs, refs = refs[:n_out], refs[n_out:]
        j_outs, refs = refs[:j_out], refs[j_out:]
        scr, sems = refs[:n_scr], refs[n_scr:]
        first = functools.reduce(jnp.logical_and, [pl.program_id(d) == 0 for d in range(len(grid))])
        last = functools.reduce(jnp.logical_and, [pl.program_id(d) == grid[d] - 1 for d in range(len(grid))])

        @pl.when(first)
        def _():
            job.start(j_ins, j_outs, sems)

        body(*ins, *outs, *scr)

        @pl.when(last)
        def _():
            job.wait(j_ins, j_outs, sems)

    res = pl.pallas_call(
        carrying, name=name, grid=grid,
        in_specs=list(in_specs) + [hbm] * j_in, out_specs=list(out_specs) + [hbm] * j_out,
        out_shape=list(out_shape) + job.out_shape, scratch_shapes=list(scratch_shapes) + job.scratch(),
        compiler_params=_cparams(("arbitrary",) * len(grid), vmem_mib))(*args, *job.arrays)
    return res[:n_out], res[n_out:]


def _attn_b_fwd(qkv, *, col0, name, job=None):
    T = qkv.shape[0]
    n_hp = WIDTH // LANES
    sub = B_TQ // B_TS

    def body(q_ref, k_ref, v_ref, o_ref, acc_ref, car_ref, qh_ref):
        i = pl.program_id(1)
        q2 = q_ref[...]
        head0 = _lane_is_head0()
        qh_ref[0] = jnp.where(head0, q2, jnp.zeros_like(q2))
        qh_ref[1] = jnp.where(head0, jnp.zeros_like(q2), q2)
        tri_s = _tri(True)
        acc_ref[...] = jnp.zeros_like(acc_ref)
        car_ref[...] = jnp.zeros_like(car_ref)

        def streams_of(kb, d, strips):
            keys = pl.ds(pl.multiple_of(kb * B_TS, B_TS), B_TS)
            return _sb_streams(d, strips, k2=k_ref[keys, :], v2=v_ref[keys, :])

        def scores(st):
            st["z2"] = _dot_nt(qh_ref[st["h"], st["rows"], :], st.pop("k2")) * (QK_SCALE * LOG2E)

        def logs(st):
            _sb_logs(st, st.pop("z2"))

        def suffix(st):
            st["suffix"] = _dot(st.pop("keep_bf"), tri_s)

        def weights(st):
            log_beta, suffix, log_keep = st.pop("log_beta"), st.pop("suffix"), st.pop("log_keep")
            wb = []
            for p in range(0, B_TS, B_PIECE):
                rows = _piece_rows(st, p)
                car = car_ref[st["h"], rows, :]
                w = jnp.exp2(log_beta[p:p + B_PIECE] + suffix[p:p + B_PIECE] + car)
                if st["diag"]:
                    w = jnp.where(_strict_lower(p), w, 0.0)
                wb.append(w.astype(BF16))
                car_ref[st["h"], rows, :] = car + jnp.sum(log_keep[p:p + B_PIECE], axis=1, keepdims=True)
            st["wb"] = _rows_cat(wb)

        def values(st):
            acc_ref[st["h"], st["rows"], :] += _dot(st.pop("wb"), st.pop("v2"))

        _sb_sweep(i, car_ref, streams_of, lambda sts: _skewed(sts, [scores, logs, suffix, weights, values]))
        o_ref[...] = jnp.where(head0, acc_ref[0], acc_ref[1])

    (out,), rode = _call_carrying(
        job, body, name=name, grid=(n_hp, T // B_TQ),
        in_specs=[pl.BlockSpec((B_TQ, LANES), lambda hp, i: (i, hp + col0)),
                  pl.BlockSpec((T, LANES), lambda hp, i: (0, hp + col0 + n_hp)),
                  pl.BlockSpec((T, LANES), lambda hp, i: (0, hp + col0 + 2 * n_hp))],
        out_specs=[pl.BlockSpec((B_TQ, LANES), lambda hp, i: (i, hp))],
        out_shape=[jax.ShapeDtypeStruct((T, WIDTH), F32)],
        scratch_shapes=[pltpu.VMEM((2, B_TQ, LANES), F32), pltpu.VMEM((2, B_TQ, 1), F32),
                        pltpu.VMEM((2, B_TQ, LANES), BF16)],
        vmem_mib=48, args=(qkv, qkv, qkv))
    return out, rode


def _attn_b_bwd(qkv, out, do, *, col0, name, job=None):
    T = qkv.shape[0]
    n_hp = WIDTH // LANES
    sub = B_TQ // B_TS

    def body(q_ref, k_ref, v_ref, o_ref, do_ref, dq_ref, dko_ref, dvo_ref,
             dqa_ref, car_ref, carr_ref, tot_ref, qh_ref, doh_ref, qs_ref, dk_ref, dv_ref):
        i = pl.program_id(1)

        @pl.when(i == 0)
        def _():
            dk_ref[...] = jnp.zeros_like(dk_ref)
            dv_ref[...] = jnp.zeros_like(dv_ref)

        q2 = q_ref[...]
        do2 = do_ref[...]
        head0 = _lane_is_head0()
        zero = jnp.zeros_like(q2)
        qh_ref[0] = jnp.where(head0, q2, zero)
        qh_ref[1] = jnp.where(head0, zero, q2)
        doh_ref[0] = jnp.where(head0, do2, zero)
        doh_ref[1] = jnp.where(head0, zero, do2)
        scale = jnp.asarray(QK_SCALE, BF16)
        qs_ref[...] = q2 * scale
        tri_s = _tri(True)
        tri_i = _tri(False)
        prod = do2.astype(F32) * o_ref[...]
        tot_ref[0] = jnp.sum(jnp.where(head0, prod, 0.0), axis=1, keepdims=True)
        tot_ref[1] = jnp.sum(jnp.where(head0, 0.0, prod), axis=1, keepdims=True)
        dqa_ref[...] = jnp.zeros_like(dqa_ref)
        car_ref[...] = jnp.zeros_like(car_ref)
        carr_ref[...] = jnp.zeros_like(carr_ref)

        def streams_of(kb, d, strips):
            keys = pl.ds(pl.multiple_of(kb * B_TS, B_TS), B_TS)
            k2 = k_ref[keys, :]
            return _sb_streams(d, strips, keys=keys, k2=k2, v2=v_ref[keys, :], k2s=k2 * scale)

        def run(streams):
            def scores(st):
                st["z2"] = _dot_nt(qh_ref[st["h"], st["rows"], :], st.pop("k2")) * (QK_SCALE * LOG2E)
                st["dw"] = _dot_nt(doh_ref[st["h"], st["rows"], :], st.pop("v2"))

            def logs(st):
                _sb_logs(st, st.pop("z2"))

            def suffix(st):
                st["suffix"] = _dot(st.pop("keep_bf"), tri_s)

            def weights(st):
                h = st["h"]
                suffix, dw = st.pop("suffix"), st.pop("dw")
                wb, dlog, hi, lo = [], [], [], []
                for p in range(0, B_TS, B_PIECE):
                    rows = _piece_rows(st, p)
                    car = car_ref[h, rows, :]
                    w = jnp.exp2(st["log_beta"][p:p + B_PIECE] + suffix[p:p + B_PIECE] + car)
                    if st["diag"]:
                        w = jnp.where(_strict_lower(p), w, 0.0)
                    w = w.astype(BF16)
                    dl = w.astype(F32) * dw[p:p + B_PIECE]
                    dl_hi, dl_lo = _split_bf16(dl)
                    wb.append(w)
                    dlog.append(dl)
                    hi.append(dl_hi)
                    lo.append(dl_lo)
                    car_ref[h, rows, :] = car + jnp.sum(st["log_keep"][p:p + B_PIECE], axis=1, keepdims=True)
                st["wb"], st["dlog"], st["hi"], st["lo"] = _rows_cat(wb), _rows_cat(dlog), _rows_cat(hi), _rows_cat(lo)

            def later(st):
                st["later"] = _dot(st.pop("hi"), tri_i) + _dot(st.pop("lo"), tri_i)

            def dscores(st):
                h = st["h"]
                later, dlog = st.pop("later"), st.pop("dlog")
                log_keep, log_beta = st.pop("log_keep"), st.pop("log_beta")
                dzb = []
                for p in range(0, B_TS, B_PIECE):
                    rows = _piece_rows(st, p)
                    pc = slice(p, p + B_PIECE)
                    carr = carr_ref[h, rows, :]
                    earlier = tot_ref[h, rows, :] - (later[pc] + carr)
                    dz = dlog[pc] * jnp.exp2(log_keep[pc]) - jnp.exp2(log_beta[pc]) * earlier
                    if st["diag"]:
                        dz = jnp.where(_strict_lower(p), dz, 0.0)
                    dzb.append(dz.astype(BF16))
                    carr_ref[h, rows, :] = carr + jnp.sum(dlog[pc], axis=1, keepdims=True)
                st["dzb"] = _rows_cat(dzb)

            def grads(st):
                h, rows, keys = st["h"], st["rows"], st["keys"]
                mine = head0 if h == 0 else jnp.logical_not(head0)
                dzb = st.pop("dzb")
                dqa_ref[h, rows, :] += _dot(dzb, st.pop("k2s"))
                dk_ref[keys, :] += jnp.where(mine, _dot_tn(dzb, qs_ref[rows, :]), 0.0)
                dv_ref[keys, :] += jnp.where(mine, _dot_tn(st.pop("wb"), do_ref[rows, :]), 0.0)

            _skewed(streams, [scores, logs, suffix, weights, later, dscores, grads])

        _sb_sweep(i, car_ref, streams_of, run)
        dq_ref[...] = jnp.where(head0, dqa_ref[0], dqa_ref[1]).astype(dq_ref.dtype)

        @pl.when(i == T // B_TQ - 1)
        def _():
            dko_ref[...] = dk_ref[...].astype(BF16)
            dvo_ref[...] = dv_ref[...].astype(BF16)

    return _call_carrying(
        job, body, name=name, grid=(n_hp, T // B_TQ),
        in_specs=[pl.BlockSpec((B_TQ, LANES), lambda hp, i: (i, hp + col0)),
                  pl.BlockSpec((T, LANES), lambda hp, i: (0, hp + col0 + n_hp)),
                  pl.BlockSpec((T, LANES), lambda hp, i: (0, hp + col0 + 2 * n_hp)),
                  pl.BlockSpec((B_TQ, LANES), lambda hp, i: (i, hp)),
                  pl.BlockSpec((B_TQ, LANES), lambda hp, i: (i, hp))],
        out_specs=[pl.BlockSpec((B_TQ, LANES), lambda hp, i: (i, hp)),
                   pl.BlockSpec((T, LANES), lambda hp, i: (0, hp)),
                   pl.BlockSpec((T, LANES), lambda hp, i: (0, hp))],
        out_shape=[jax.ShapeDtypeStruct((T, WIDTH), BF16),
                   jax.ShapeDtypeStruct((T, WIDTH), BF16),
                   jax.ShapeDtypeStruct((T, WIDTH), BF16)],
        scratch_shapes=[pltpu.VMEM((2, B_TQ, LANES), F32), pltpu.VMEM((2, B_TQ, 1), F32),
                        pltpu.VMEM((2, B_TQ, 1), F32), pltpu.VMEM((2, B_TQ, 1), F32),
                        pltpu.VMEM((2, B_TQ, LANES), BF16), pltpu.VMEM((2, B_TQ, LANES), BF16),
                        pltpu.VMEM((B_TQ, LANES), BF16),
                        pltpu.VMEM((T, LANES), F32), pltpu.VMEM((T, LANES), F32)],
        vmem_mib=56, args=(qkv, qkv, qkv, out, do))


def _gated_mix(oa_ref, ob_ref, g_ref, bg_ref, wpa_ref, wpb_ref, D):
    ya = _dot(oa_ref[...].astype(BF16), wpa_ref[...])
    yb = _dot(ob_ref[...].astype(BF16), wpb_ref[...])
    sa = jax.nn.sigmoid(g_ref[:, :D] + bg_ref[:, :D])
    sb = jax.nn.sigmoid(g_ref[:, D:] + bg_ref[:, D:])
    return ya, yb, sa, sb


def _proj_fwd(oa, ob, g, bg, wpa, wpb, wo, wl, xin, lng, lnb, layer, *, alpha, name):
    T, D = xin.shape
    tm = _tile(T, 512)
    row = lambda i: (i, 0)
    wspec = lambda r, c: pl.BlockSpec((None, r, c), lambda i: (wl, 0, 0))
    vec = lambda c: pl.BlockSpec((None, 1, c), lambda i: (layer, 0, 0))

    def body(oa_ref, ob_ref, g_ref, bg_ref, wpa_ref, wpb_ref, wo_ref, x_ref, lg_ref, lb_ref, x1_ref, r1_ref, x1b_ref):
        ya, yb, sa, sb = _gated_mix(oa_ref, ob_ref, g_ref, bg_ref, wpa_ref, wpb_ref, D)
        mix = _dot((sa * ya + sb * yb).astype(BF16), wo_ref[...])
        r1 = alpha * x_ref[...] + mix
        r1_ref[...] = r1
        x1 = _ln_fwd(r1, lg_ref[...], lb_ref[...])
        x1_ref[...] = x1
        x1b_ref[...] = x1.astype(BF16)

    return pl.pallas_call(
        body, name=name, grid=(T // tm,),
        in_specs=[pl.BlockSpec((tm, WIDTH), row), pl.BlockSpec((tm, WIDTH), row), pl.BlockSpec((tm, 2 * D), row),
                  vec(2 * D), wspec(WIDTH, D), wspec(WIDTH, D), wspec(D, D),
                  pl.BlockSpec((tm, D), row), vec(D), vec(D)],
        out_specs=[pl.BlockSpec((tm, D), row), pl.BlockSpec((tm, D), row), pl.BlockSpec((tm, D), row)],
        out_shape=[jax.ShapeDtypeStruct((T, D), F32), jax.ShapeDtypeStruct((T, D), F32),
                   jax.ShapeDtypeStruct((T, D), BF16)],
        compiler_params=_cparams(("arbitrary",), 56),
    )(oa, ob, g, bg, wpa, wpb, wo, xin, lng, lnb)


def _proj_bwd(dx1, r1, lng, oa, ob, g, bg, wpa, wpb, wo, wl, layer, *, name):
    T, D = dx1.shape
    tm = _tile(T, 512)
    row = lambda i: (i, 0)
    fixed = lambda i: (0, 0)
    wspec = lambda r, c: pl.BlockSpec((None, r, c), lambda i: (wl, 0, 0))
    vec = lambda c: pl.BlockSpec((None, 1, c), lambda i: (layer, 0, 0))

    def body(dx_ref, r1_ref, lg_ref, oa_ref, ob_ref, g_ref, bg_ref, wpa_ref, wpb_ref, wo_ref,
             dr_ref, mix_ref, dya_ref, dyb_ref, dg_ref, doa_ref, dob_ref, dlg_ref, dlb_ref, dbg_ref):
        @pl.when(pl.program_id(0) == 0)
        def _():
            dlg_ref[...] = jnp.zeros_like(dlg_ref)
            dlb_ref[...] = jnp.zeros_like(dlb_ref)
            dbg_ref[...] = jnp.zeros_like(dbg_ref)

        dx = dx_ref[...]
        dr, xhat = _ln_bwd(dx, r1_ref[...], lg_ref[...])
        dr_ref[...] = dr
        dlg_ref[...] += jnp.sum(dx * xhat, axis=0, keepdims=True)
        dlb_ref[...] += jnp.sum(dx, axis=0, keepdims=True)
        dmix = _dot_nt(dr.astype(BF16), wo_ref[...])
        ya, yb, sa, sb = _gated_mix(oa_ref, ob_ref, g_ref, bg_ref, wpa_ref, wpb_ref, D)
        mix_ref[...] = (sa * ya + sb * yb).astype(BF16)
        dya = (dmix * sa).astype(BF16)
        dyb = (dmix * sb).astype(BF16)
        dya_ref[...] = dya
        dyb_ref[...] = dyb
        dga = dmix * ya * (sa * (1.0 - sa))
        dgb = dmix * yb * (sb * (1.0 - sb))
        dg_ref[:, :D] = dga.astype(BF16)
        dg_ref[:, D:] = dgb.astype(BF16)
        dbg_ref[:, :D] += jnp.sum(dga, axis=0, keepdims=True)
        dbg_ref[:, D:] += jnp.sum(dgb, axis=0, keepdims=True)
        doa_ref[...] = _dot_nt(dya, wpa_ref[...]).astype(BF16)
        dob_ref[...] = _dot_nt(dyb, wpb_ref[...]).astype(BF16)

    return pl.pallas_call(
        body, name=name, grid=(T // tm,),
        in_specs=[pl.BlockSpec((tm, D), row), pl.BlockSpec((tm, D), row), vec(D),
                  pl.BlockSpec((tm, WIDTH), row), pl.BlockSpec((tm, WIDTH), row), pl.BlockSpec((tm, 2 * D), row),
                  vec(2 * D), wspec(WIDTH, D), wspec(WIDTH, D), wspec(D, D)],
        out_specs=[pl.BlockSpec((tm, D), row), pl.BlockSpec((tm, D), row), pl.BlockSpec((tm, D), row),
                   pl.BlockSpec((tm, D), row), pl.BlockSpec((tm, 2 * D), row),
                   pl.BlockSpec((tm, WIDTH), row), pl.BlockSpec((tm, WIDTH), row),
                   pl.BlockSpec((1, D), fixed), pl.BlockSpec((1, D), fixed), pl.BlockSpec((1, 2 * D), fixed)],
        out_shape=[jax.ShapeDtypeStruct((T, D), F32), jax.ShapeDtypeStruct((T, D), BF16),
                   jax.ShapeDtypeStruct((T, D), BF16), jax.ShapeDtypeStruct((T, D), BF16),
                   jax.ShapeDtypeStruct((T, 2 * D), BF16),
                   jax.ShapeDtypeStruct((T, WIDTH), BF16), jax.ShapeDtypeStruct((T, WIDTH), BF16),
                   jax.ShapeDtypeStruct((1, D), F32), jax.ShapeDtypeStruct((1, D), F32),
                   jax.ShapeDtypeStruct((1, 2 * D), F32)],
        compiler_params=_cparams(("arbitrary",), 56),
    )(dx1, r1, lng, oa, ob, g, bg, wpa, wpb, wo)


def _ffn_fwd(x1, wfi, wfo, wl, lng, lnb, layer, *, alpha, name, job=None):
    T, D = x1.shape
    tf = wfi.shape[-1]
    nj = wfi.shape[0] // 2
    tm = _tile(T, 1024)
    vec = lambda c: pl.BlockSpec((None, 1, c), lambda i, j: (layer, 0, 0))

    def body(x_ref, wg_ref, wu_ref, wo_ref, lg_ref, lb_ref, gs_ref, us_ref, r2_ref, x2_ref, acc_ref, xb_ref):
        j = pl.program_id(1)

        @pl.when(j == 0)
        def _():
            xb_ref[...] = x_ref[...].astype(BF16)
            acc_ref[...] = jnp.zeros_like(acc_ref)

        gv = _dot(xb_ref[...], wg_ref[...])
        uv = _dot(xb_ref[...], wu_ref[...])
        gs_ref[...] = gv.astype(gs_ref.dtype)
        us_ref[...] = uv.astype(us_ref.dtype)
        act = gv * jax.nn.sigmoid(gv) * uv
        acc_ref[...] += _dot(act.astype(BF16), wo_ref[...])

        @pl.when(j == nj - 1)
        def _():
            r2 = alpha * x_ref[...] + acc_ref[...]
            r2_ref[...] = r2
            x2_ref[...] = _ln_fwd(r2, lg_ref[...], lb_ref[...])

    return _call_carrying(
        job, body, name=name, grid=(T // tm, nj),
        in_specs=[pl.BlockSpec((tm, D), lambda i, j: (i, 0)),
                  pl.BlockSpec((None, None, D, tf), lambda i, j: (j, wl, 0, 0)),
                  pl.BlockSpec((None, None, D, tf), lambda i, j: (j + nj, wl, 0, 0)),
                  pl.BlockSpec((None, tf, D), lambda i, j: (wl, j, 0)),
                  vec(D), vec(D)],
        out_specs=[pl.BlockSpec((None, tm, tf), lambda i, j: (j, i, 0)),
                   pl.BlockSpec((None, tm, tf), lambda i, j: (j, i, 0)),
                   pl.BlockSpec((tm, D), lambda i, j: (i, 0)),
                   pl.BlockSpec((tm, D), lambda i, j: (i, 0))],
        out_shape=[jax.ShapeDtypeStruct((nj, T, tf), BF16), jax.ShapeDtypeStruct((nj, T, tf), BF16),
                   jax.ShapeDtypeStruct((T, D), F32), jax.ShapeDtypeStruct((T, D), F32)],
        scratch_shapes=[pltpu.VMEM((tm, D), F32), pltpu.VMEM((tm, D), BF16)],
        vmem_mib=56, args=(x1, wfi, wfi, wfo, lng, lnb))


def _ffn_bwd(dx2, r2, lng, gs, us, wfi, wfo, wl, layer, *, alpha, name, job=None):
    T, D = dx2.shape
    tf = wfi.shape[-1]
    nj = wfi.shape[0] // 2
    tm = _tile(T, 512)
    vec = lambda c: pl.BlockSpec((None, 1, c), lambda i, j: (layer, 0, 0))
    blk = lambda: pl.BlockSpec((None, tm, tf), lambda i, j: (j, i, 0))

    def body(dx_ref, r2_ref, lg_ref, gs_ref, us_ref, wg_ref, wu_ref, wo_ref,
             dr_ref, act_ref, dg_ref, du_ref, dx1_ref, dlg_ref, dlb_ref, acc_ref, drb_ref):
        i = pl.program_id(0)
        j = pl.program_id(1)

        @pl.when((i == 0) & (j == 0))
        def _():
            dlg_ref[...] = jnp.zeros_like(dlg_ref)
            dlb_ref[...] = jnp.zeros_like(dlb_ref)

        @pl.when(j == 0)
        def _():
            dx = dx_ref[...]
            dr, xhat = _ln_bwd(dx, r2_ref[...], lg_ref[...])
            dlg_ref[...] += jnp.sum(dx * xhat, axis=0, keepdims=True)
            dlb_ref[...] += jnp.sum(dx, axis=0, keepdims=True)
            drb_ref[...] = dr.astype(BF16)
            dr_ref[...] = dr.astype(BF16)
            acc_ref[...] = alpha * dr

        dact = _dot_nt(drb_ref[...], wo_ref[...])
        gv = gs_ref[...].astype(F32)
        uv = us_ref[...].astype(F32)
        s = jax.nn.sigmoid(gv)
        silu = gv * s
        act_ref[...] = (silu * uv).astype(BF16)
        dg = (dact * uv * (s * (1.0 + gv * (1.0 - s)))).astype(BF16)
        du = (dact * silu).astype(BF16)
        dg_ref[...] = dg
        du_ref[...] = du
        acc_ref[...] += _dot_nt(dg, wg_ref[...]) + _dot_nt(du, wu_ref[...])

        @pl.when(j == nj - 1)
        def _():
            dx1_ref[...] = acc_ref[...]

    return _call_carrying(
        job, body, name=name, grid=(T // tm, nj),
        in_specs=[pl.BlockSpec((tm, D), lambda i, j: (i, 0)), pl.BlockSpec((tm, D), lambda i, j: (i, 0)), vec(D),
                  blk(), blk(),
                  pl.BlockSpec((None, None, D, tf), lambda i, j: (j, wl, 0, 0)),
                  pl.BlockSpec((None, None, D, tf), lambda i, j: (j + nj, wl, 0, 0)),
                  pl.BlockSpec((None, tf, D), lambda i, j: (wl, j, 0))],
        out_specs=[pl.BlockSpec((tm, D), lambda i, j: (i, 0)), blk(), blk(), blk(),
                   pl.BlockSpec((tm, D), lambda i, j: (i, 0)),
                   pl.BlockSpec((1, D), lambda i, j: (0, 0)), pl.BlockSpec((1, D), lambda i, j: (0, 0))],
        out_shape=[jax.ShapeDtypeStruct((T, D), BF16),
                   jax.ShapeDtypeStruct((nj, T, tf), BF16), jax.ShapeDtypeStruct((nj, T, tf), BF16),
                   jax.ShapeDtypeStruct((nj, T, tf), BF16),
                   jax.ShapeDtypeStruct((T, D), F32),
                   jax.ShapeDtypeStruct((1, D), F32), jax.ShapeDtypeStruct((1, D), F32)],
        scratch_shapes=[pltpu.VMEM((tm, D), F32), pltpu.VMEM((tm, D), BF16)],
        vmem_mib=56, args=(dx2, r2, lng, gs, us, wfi, wfi, wfo))


def _loss_head(y, target, *, name):
    T, D = y.shape
    tm = _tile(T, 1024)

    def body(y_ref, t_ref, dy_ref, sq_ref):
        @pl.when(pl.program_id(0) == 0)
        def _():
            sq_ref[...] = jnp.zeros_like(sq_ref)
        err = y_ref[...] - t_ref[...]
        dy_ref[...] = err * (1.0 / D)
        sq_ref[...] += jnp.sum(err * err, axis=0, keepdims=True)

    return pl.pallas_call(
        body, name=name, grid=(T // tm,),
        in_specs=[pl.BlockSpec((tm, D), lambda i: (i, 0)), pl.BlockSpec((tm, D), lambda i: (i, 0))],
        out_specs=[pl.BlockSpec((tm, D), lambda i: (i, 0)), pl.BlockSpec((1, D), lambda i: (0, 0))],
        out_shape=[jax.ShapeDtypeStruct((T, D), F32), jax.ShapeDtypeStruct((1, D), F32)],
        compiler_params=_cparams(("arbitrary",)),
    )(y, target)


def _my_place():
    return lax.axis_index("x"), lax.axis_index("y"), lax.axis_index("c")


def _peer(place, k):
    x, y, c = place
    return (1 - x if k & 4 else x, 1 - y if k & 2 else y, 1 - c if k & 1 else c)


def _logical(place):
    x, y, c = place
    return 4 * x + 2 * y + c


def _block_of(ref, mode, idx):
    if mode == "blk":
        return ref.at[idx]
    if mode == "col":
        size = ref.shape[2] // N_DEV
        return ref.at[:, :, pl.ds(pl.multiple_of(idx * size, size), size)]
    size = ref.shape[1] // N_DEV
    return ref.at[:, pl.ds(pl.multiple_of(idx * size, size), size), :]


def _full_shape(shard, mode):
    if mode == "blk":
        return (N_DEV,) + shard.shape
    if mode == "col":
        return shard.shape[:2] + (N_DEV * shard.shape[2],)
    return (shard.shape[0], N_DEV * shard.shape[1], shard.shape[2])


class _Exchange:
    def __init__(self, arrays, out_shape, build):
        self.arrays = list(arrays)
        self.out_shape = list(out_shape)
        self.build = build

    def scratch(self):
        n = len(self.arrays)
        return [pltpu.SemaphoreType.DMA((n * N_DEV,)), pltpu.SemaphoreType.DMA((n * N_DEV,)),
                pltpu.SemaphoreType.DMA((n,))]

    def start(self, ins, outs, sems):
        for cp in self.build(ins, outs, *sems):
            cp.start()

    def wait(self, ins, outs, sems):
        for cp in self.build(ins, outs, *sems):
            cp.wait()

    def run(self, name):
        n_in, n_out = len(self.arrays), len(self.out_shape)
        hbm = pl.BlockSpec(memory_space=pltpu.HBM)

        def body(*refs):
            ins, outs, sems = refs[:n_in], refs[n_in:n_in + n_out], refs[n_in + n_out:]
            self.start(ins, outs, sems)
            self.wait(ins, outs, sems)

        return pl.pallas_call(
            body, name=name, in_specs=[hbm] * n_in, out_specs=[hbm] * n_out,
            out_shape=self.out_shape, scratch_shapes=self.scratch(),
        )(*self.arrays)


def _copies_to_all(src_of, dst_of, n, send, recv, local):
    me = _my_place()
    copies = []
    for a in range(n):
        copies.append(pltpu.make_async_copy(src_of(a, _logical(me)), dst_of(a), local.at[a]))
        for k in range(1, N_DEV):
            peer = _peer(me, k)
            copies.append(pltpu.make_async_remote_copy(
                src_ref=src_of(a, _logical(peer)), dst_ref=dst_of(a),
                send_sem=send.at[a * N_DEV + k], recv_sem=recv.at[a * N_DEV + k],
                device_id=peer, device_id_type=MESH))
    return copies


def _gather_job(shards, modes):
    def build(ins, outs, send, recv, local):
        my_id = _logical(_my_place())
        return _copies_to_all(lambda a, dev: ins[a], lambda a: _block_of(outs[a], modes[a], my_id),
                              len(shards), send, recv, local)

    return _Exchange(shards, [jax.ShapeDtypeStruct(_full_shape(s, m), s.dtype) for s, m in zip(shards, modes)], build)


def _gather_via_sibling(shard, mode, *, name):
    hbm = pl.BlockSpec(memory_space=pltpu.HBM)

    def body(x_ref, o_ref, send, recv, local):
        x, y, c = _my_place()
        me, sibling = (x, y, c), (x, y, 1 - c)
        chips = [(1 - x, y), (x, 1 - y), (1 - x, 1 - y)]

        def copy(k, block, to, src=None):
            dst = _block_of(o_ref, mode, _logical(block))
            return pltpu.make_async_remote_copy(src_ref=dst if src is None else src, dst_ref=dst,
                                                send_sem=send.at[k], recv_sem=recv.at[k],
                                                device_id=to, device_id_type=MESH)

        mine = pltpu.make_async_copy(x_ref, _block_of(o_ref, mode, _logical(me)), local)
        mine.start()
        first = [copy(0, me, sibling, src=x_ref)]
        first += [copy(1 + j, me, (*chip, c), src=x_ref) for j, chip in enumerate(chips)]
        for cp in first:
            cp.start()
        passed = [copy(4 + j, (*chip, c), sibling) for j, chip in enumerate(chips)]
        for j, chip in enumerate(chips):
            copy(1 + j, (*chip, c), me).wait_recv()
            passed[j].start()
        copy(0, sibling, me).wait_recv()
        for j, chip in enumerate(chips):
            copy(4 + j, (*chip, 1 - c), me).wait_recv()
        for cp in first + passed:
            cp.wait_send()
        mine.wait()

    return pl.pallas_call(
        body, name=name, in_specs=[hbm], out_specs=hbm,
        out_shape=jax.ShapeDtypeStruct(_full_shape(shard, mode), shard.dtype),
        scratch_shapes=[pltpu.SemaphoreType.DMA((7,)), pltpu.SemaphoreType.DMA((7,)), pltpu.SemaphoreType.DMA],
    )(shard)


def _grad_block(ref, mode, idx):
    if mode == "blk":
        return ref.at[idx]
    if mode == "col":
        size = ref.shape[1] // N_DEV
        return ref.at[:, pl.ds(pl.multiple_of(idx * size, size), size)]
    size = ref.shape[0] // N_DEV
    return ref.at[pl.ds(pl.multiple_of(idx * size, size), size), :]


def _grad_shard_shape(g, mode):
    if mode == "blk":
        return g.shape[1:]
    if mode == "col":
        return (g.shape[0], g.shape[1] // N_DEV)
    return (g.shape[0] // N_DEV, g.shape[1])


def _grads_job(groups, modes):
    flat = [(g, w, l) for w, per_w in enumerate(groups) for l, g in enumerate(per_w)]

    def build(ins, outs, send, recv, local):
        my_id = _logical(_my_place())
        return _copies_to_all(lambda a, dev: _grad_block(ins[a], modes[flat[a][1]], dev),
                              lambda a: outs[flat[a][1]].at[my_id, flat[a][2]],
                              len(flat), send, recv, local)

    out_shape = [jax.ShapeDtypeStruct((N_DEV, len(per_w)) + _grad_shard_shape(per_w[0], m), per_w[0].dtype)
                 for per_w, m in zip(groups, modes)]
    return _Exchange([g for g, _, _ in flat], out_shape, build)


def _adamw(w, g, m, v):
    m = ADAM_B1 * m + (1.0 - ADAM_B1) * g
    v = ADAM_B2 * v + (1.0 - ADAM_B2) * (g * g)
    m_hat = m / (1.0 - ADAM_B1 ** ADAM_STEP)
    v_hat = v / (1.0 - ADAM_B2 ** ADAM_STEP)
    delta = -ADAM_LR * (m_hat / (jnp.sqrt(v_hat) + ADAM_EPS) + ADAM_WD * w)
    return delta, m, v


def _sum_slots_adamw(slots, w, m, v, *, name):
    n_l = len(slots)
    R, C = slots[0].shape[1:]
    tr = _tile(R, 256)
    n_r = R // tr

    def body(*refs):
        s_refs = refs[:n_l]
        w_ref, m_ref, v_ref, g_out, d_out, m_out, v_out = refs[n_l:]
        for layer in range(n_l):
            @pl.when(pl.program_id(0) == layer)
            def _(s_ref=s_refs[layer]):
                g = s_ref[0].astype(F32)
                for s in range(1, N_DEV):
                    g = g + s_ref[s].astype(F32)
                delta, m_new, v_new = _adamw(w_ref[...], g, m_ref[...], v_ref[...])
                g_out[...] = g
                d_out[...] = delta
                m_out[...] = m_new
                v_out[...] = v_new

    slot_spec = lambda layer: pl.BlockSpec((N_DEV, tr, C), lambda l, i: (0, jnp.where(l == layer, i, 0), 0))
    spec = pl.BlockSpec((tr, C), lambda l, i: (l * n_r + i, 0))
    return pl.pallas_call(
        body, name=name, grid=(n_l, n_r),
        in_specs=[slot_spec(layer) for layer in range(n_l)] + [spec, spec, spec],
        out_specs=[spec] * 4,
        out_shape=[jax.ShapeDtypeStruct((n_l * R, C), F32)] * 4,
        compiler_params=_cparams(("arbitrary", "arbitrary")),
    )(*slots, w, m, v)


def _small_allreduce_adamw(g, w, m, v, *, name):
    R = g.shape[0]
    vmem = pl.BlockSpec(memory_space=pltpu.VMEM)

    def body(g_ref, w_ref, m_ref, v_ref, g_out, d_out, m_out, v_out, slots, send, recv):
        me = _my_place()
        my_id = _logical(me)
        slots[my_id] = g_ref[...]
        copies = []
        for k in range(1, N_DEV):
            cp = pltpu.make_async_remote_copy(
                src_ref=g_ref, dst_ref=slots.at[my_id], send_sem=send.at[k], recv_sem=recv.at[k],
                device_id=_peer(me, k), device_id_type=MESH)
            cp.start()
            copies.append(cp)
        for cp in copies:
            cp.wait()
        total = slots[0]
        for s in range(1, N_DEV):
            total = total + slots[s]
        delta, m_new, v_new = _adamw(w_ref[...], total, m_ref[...], v_ref[...])
        g_out[...] = total
        d_out[...] = delta
        m_out[...] = m_new
        v_out[...] = v_new

    return pl.pallas_call(
        body, name=name,
        in_specs=[vmem] * 4, out_specs=[vmem] * 4,
        out_shape=[jax.ShapeDtypeStruct((R, LANES), F32)] * 4,
        scratch_shapes=[pltpu.VMEM((N_DEV, R, LANES), F32),
                        pltpu.SemaphoreType.DMA((N_DEV,)), pltpu.SemaphoreType.DMA((N_DEV,))],
    )(g, w, m, v)


def _pack(parts):
    flat = jnp.concatenate([p.reshape(-1) for p in parts])
    rows = -(-flat.shape[0] // (8 * LANES)) * 8
    return jnp.pad(flat, (0, rows * LANES - flat.shape[0])).reshape(rows, LANES)


def _unpack(packed, like):
    flat = packed.reshape(-1)
    out, pos = [], 0
    for p in like:
        out.append(flat[pos:pos + p.size].reshape(p.shape))
        pos += p.size
    return out


def kernel(x, w_in, b_gate, rel_bias, w_proj_a, w_proj_b, w_out, ln1_g, ln1_b, w_ffn_in, w_ffn_out, ln2_g, ln2_b, loss_target, m_w_in, m_b_gate, m_rel_bias, m_w_proj_a, m_w_proj_b, m_w_out, m_ln1_g, m_ln1_b, m_w_ffn_in, m_w_ffn_out, m_ln2_g, m_ln2_b, v_w_in, v_b_gate, v_rel_bias, v_w_proj_a, v_w_proj_b, v_w_out, v_ln1_g, v_ln1_b, v_w_ffn_in, v_w_ffn_out, v_ln2_g, v_ln2_b):
    L = w_in.shape[0]
    T, D = x.shape[1], x.shape[2]
    alpha = float((2 * L) ** 0.25)
    n_qkv = 6 * WIDTH

    big = [w_in, w_proj_a, w_proj_b, w_out, w_ffn_in, w_ffn_out]
    kinds = ["in", "pa", "pb", "o", "fi", "fo"]
    modes = ["col", "col", "col", "row", "blk", "row"]
    mode_of = dict(zip(kinds, modes))
    w_bf = dict(zip(kinds, [w.astype(BF16) for w in big]))

    def gather_of(ks, l):
        return _gather_job([w_bf[k][l:l + 1] for k in ks], [mode_of[k] for k in ks])

    W = [dict() for _ in range(L)]
    W[0]["in"] = _gather_via_sibling(w_bf["in"][:1], mode_of["in"], name="gather_w_in_first")
    vec3 = lambda a: a[:, None, :]
    bg3, l1g, l1b, l2g, l2b = vec3(b_gate), vec3(ln1_g), vec3(ln1_b), vec3(ln2_g), vec3(ln2_b)
    b_col0 = 3 * WIDTH // LANES

    h = x[0]
    saved = []
    for l in range(L):
        ahead = l + 1 < L
        soon = ["pa", "pb", "o", "fo"]
        (qkv, gates), got = _in_proj(h, W[l]["in"], 0, n_qkv=n_qkv, name=f"in_proj_{l}",
                                     job=gather_of(soon, 0) if l == 0 else None)
        W[l].update(zip(soon, got))
        kvpad = jnp.pad(qkv[:, WIDTH:3 * WIDTH], ((A_WIN - A_TQ, 0), (0, 0)))
        bias = _toeplitz_bias(rel_bias[l])
        oa, got = _attn_a_fwd(qkv, kvpad, bias, name=f"attn_a_fwd_{l}", job=gather_of(["fi"], 0) if l == 0 else None)
        W[l].update(zip(["fi"], got))
        early = ["in", "pa", "pb", "o"]
        ob, got = _attn_b_fwd(qkv, col0=b_col0, name=f"attn_b_fwd_{l}", job=gather_of(early, l + 1) if ahead else None)
        W[l + 1 if ahead else l].update(zip(early, got))
        x1, r1, x1b = _proj_fwd(oa, ob, gates, bg3, W[l]["pa"], W[l]["pb"], W[l]["o"], 0, h, l1g, l1b, l,
                           alpha=alpha, name=f"proj_fwd_{l}")
        (gs, us, r2, x2), got = _ffn_fwd(x1, W[l]["fi"], W[l]["fo"], 0, l2g, l2b, l, alpha=alpha, name=f"ffn_fwd_{l}",
                                         job=gather_of(["fi", "fo"], l + 1) if ahead else None)
        W[l + 1 if ahead else l].update(zip(["fi", "fo"], got))
        saved.append((h, qkv, gates, kvpad, bias, oa, ob, x1b, r1, gs, us, r2))
        h = x2

    d_h, sq = _loss_head(h, loss_target[0], name="loss_head")
    loss = lax.psum((0.5 / D) * jnp.sum(sq), ("x", "y", "c"))

    g_bg, g_rb, g_l1g, g_l1b, g_l2g, g_l2b = ([None] * L for _ in range(6))
    slot = {k: [None] * L for k in kinds}

    def exchange_of(ks, grads):
        return _grads_job([[g] for g in grads], [mode_of[k] for k in ks])

    w_in_above = None
    for l in reversed(range(L)):
        xin, qkv, gates, kvpad, bias, oa, ob, x1b, r1, gs, us, r2 = saved[l]
        (dr2, act, dgt, dup, dx1, g_l2g[l], g_l2b[l]), got = _ffn_bwd(
            d_h, r2, l2g, gs, us, W[l]["fi"], W[l]["fo"], 0, l, alpha=alpha, name=f"ffn_bwd_{l}", job=w_in_above)
        if w_in_above is not None:
            (slot["in"][l + 1],) = got
        g_fo = _mm_tn_blocked_a(act, dr2, name=f"grad_w_ffn_out_{l}").reshape(-1, D)
        g_fi = jnp.concatenate(_mm_tn_blocked_pair(x1b, dgt, dup, name=f"grad_w_ffn_in_{l}"), axis=0)
        (dr1, mixin, dya, dyb, dgates, doa, dob, g_l1g[l], g_l1b[l], g_bg[l]) = _proj_bwd(
            dx1, r1, l1g, oa, ob, gates, bg3, W[l]["pa"], W[l]["pb"], W[l]["o"], 0, l, name=f"proj_bwd_{l}")
        g_o = _mm_tn(mixin, dr1, tm=_tile(D, 1024), tn=_tile(D, 1024), name=f"grad_w_out_{l}")
        g_pa = _mm_tn(oa, dya, tm=WIDTH, tn=_tile(D, 1024), name=f"grad_w_proj_a_{l}")
        g_pb = _mm_tn(ob, dyb, tm=WIDTH, tn=_tile(D, 1024), name=f"grad_w_proj_b_{l}")
        (dqa, dka, dva, dbias), (slot["fi"][l], slot["fo"][l]) = _attn_a_bwd(
            qkv, kvpad, bias, doa, name=f"attn_a_bwd_{l}", job=exchange_of(["fi", "fo"], [g_fi, g_fo]))
        g_rb[l] = _toeplitz_bias_grad(dbias)
        (dqb, dkb, dvb), (slot["pa"][l], slot["pb"][l], slot["o"][l]) = _attn_b_bwd(
            qkv, ob, dob, col0=b_col0, name=f"attn_b_bwd_{l}", job=exchange_of(["pa", "pb", "o"], [g_pa, g_pb, g_o]))
        d_qkv = [dqa, dka, dva, dqb, dkb, dvb]
        g_in = _mm_tn_pieces(xin, d_qkv, dgates, name=f"grad_w_in_{l}")
        w_in_above = exchange_of(["in"], [g_in])
        d_h, got = _mm_nt_pieces_add(d_qkv, dgates, W[l]["in"], 0, dr1, alpha, name=f"grad_x_{l}",
                                     job=w_in_above if l == 0 else None)
        if l == 0:
            (slot["in"][0],) = got
    grad_x = d_h[None]

    moments_m = [m_w_in, m_w_proj_a, m_w_proj_b, m_w_out, m_w_ffn_in, m_w_ffn_out]
    moments_v = [v_w_in, v_w_proj_a, v_w_proj_b, v_w_out, v_w_ffn_in, v_w_ffn_out]
    names = ["w_in", "w_proj_a", "w_proj_b", "w_out", "w_ffn_in", "w_ffn_out"]
    big_out = {}
    for nm, k, w, m, v in zip(names, kinds, big, moments_m, moments_v):
        two = lambda a: a.reshape(-1, a.shape[-1])
        per_layer = [s.reshape(N_DEV, -1, s.shape[-1]) for s in slot[k]]
        res = _sum_slots_adamw(per_layer, two(w), two(m), two(v), name=f"adamw_{nm}")
        big_out[nm] = [r.reshape(w.shape) for r in res]

    small_w = [b_gate, rel_bias, ln1_g, ln1_b, ln2_g, ln2_b]
    small_g = [jnp.stack(g) for g in (g_bg, g_rb, g_l1g, g_l1b, g_l2g, g_l2b)]
    small_m = [m_b_gate, m_rel_bias, m_ln1_g, m_ln1_b, m_ln2_g, m_ln2_b]
    small_v = [v_b_gate, v_rel_bias, v_ln1_g, v_ln1_b, v_ln2_g, v_ln2_b]
    res = _small_allreduce_adamw(_pack(small_g), _pack(small_w), _pack(small_m), _pack(small_v),
                                 name="allreduce_small_adamw")
    small_names = ["b_gate", "rel_bias", "ln1_g", "ln1_b", "ln2_g", "ln2_b"]
    small_out = {nm: [] for nm in small_names}
    for packed in res:
        for nm, arr in zip(small_names, _unpack(packed, small_w)):
            small_out[nm].append(arr)

    order = ["w_in", "b_gate", "rel_bias", "w_proj_a", "w_proj_b", "w_out", "ln1_g", "ln1_b",
             "w_ffn_in", "w_ffn_out", "ln2_g", "ln2_b"]
    every = {**big_out, **small_out}
    outs = [loss, grad_x]
    for kind in range(4):
        outs += [every[nm][kind] for nm in order]
    return tuple(outs)
```

```python
import functools
import math

import jax
import jax.numpy as jnp
import numpy as np
from jax import lax
from jax.experimental import pallas as pl
from jax.experimental.pallas import tpu as pltpu

F32 = jnp.float32
BF16 = jnp.bfloat16

HEAD_DIM = 64
CHUNK = 64
LEFT_CHUNKS = 8
REL_CLIP = 256
N_REL = 2 * REL_CLIP + 1
WIDTH = 512
LANES = 128
A_TQ = 256
A_WIN = A_TQ + LEFT_CHUNKS * CHUNK
A_STRIP = 128
B_TQ = 512
B_TS = 256
B_PIECE = 64
B_DEAD = -160.0
LN_EPS = 1e-5
QK_SCALE = 1.0 / math.sqrt(HEAD_DIM)
LOG2E = 1.4426950408889634
NEG = -1e30

ADAM_LR = 0.001
ADAM_B1 = 0.9
ADAM_B2 = 0.999
ADAM_EPS = 1e-08
ADAM_WD = 0.01
ADAM_STEP = 10

N_DEV = 8
MESH = pl.DeviceIdType.MESH
MIB = 1024 * 1024


def _cparams(sem=None, vmem_mib=48):
    return pltpu.CompilerParams(dimension_semantics=sem, vmem_limit_bytes=vmem_mib * MIB)


def _dot(a, b):
    return jnp.dot(a, b, preferred_element_type=F32)


def _dot_nt(a, b):
    return lax.dot_general(a, b, (((1,), (1,)), ((), ())), preferred_element_type=F32)


def _dot_tn(a, b):
    return lax.dot_general(a, b, (((0,), (0,)), ((), ())), preferred_element_type=F32)


def _tile(n, pref):
    if n <= pref:
        return n
    for t in range(pref - pref % 8, 0, -8):
        if n % t == 0:
            return t
    raise ValueError((n, pref))


def _in_proj(a, w, layer, *, n_qkv, name, job=None):
    M, K = a.shape
    N = w.shape[2]
    tm = _tile(M, 1024)
    tn = 1024
    assert n_qkv % tn == 0 and (N - n_qkv) % tn == 0
    n_q = n_qkv // tn

    def body(a_ref, w_ref, q_ref, g_ref, ab_ref):
        j = pl.program_id(1)

        @pl.when(j == 0)
        def _():
            ab_ref[...] = a_ref[...].astype(BF16)

        res = _dot(ab_ref[...], w_ref[...])

        @pl.when(j < n_q)
        def _():
            q_ref[...] = res.astype(BF16)

        @pl.when(j >= n_q)
        def _():
            g_ref[...] = res

    return _call_carrying(
        job, body, name=name, grid=(M // tm, N // tn),
        in_specs=[pl.BlockSpec((tm, K), lambda i, j: (i, 0)),
                  pl.BlockSpec((None, K, tn), lambda i, j: (layer, 0, j))],
        out_specs=[pl.BlockSpec((tm, tn), lambda i, j: (i, jnp.minimum(j, n_q - 1))),
                   pl.BlockSpec((tm, tn), lambda i, j: (i, jnp.maximum(j - n_q, 0)))],
        out_shape=[jax.ShapeDtypeStruct((M, n_qkv), BF16), jax.ShapeDtypeStruct((M, N - n_qkv), F32)],
        scratch_shapes=[pltpu.VMEM((tm, K), BF16)], vmem_mib=48, args=(a, w))


def _mm_nt_add(a, w, layer, add, add_scale, *, name, job=None):
    M, K = a.shape
    N = w.shape[1]
    tm = _tile(M, 1024)
    tk = _tile(K, 1024)

    def body(a_ref, w_ref, add_ref, o_ref):
        @pl.when(pl.program_id(1) == 0)
        def _():
            o_ref[...] = add_scale * add_ref[...]
        o_ref[...] += _dot_nt(a_ref[...], w_ref[...])

    (out,), rode = _call_carrying(
        job, body, name=name, grid=(M // tm, K // tk),
        in_specs=[pl.BlockSpec((tm, tk), lambda i, k: (i, k)),
                  pl.BlockSpec((None, N, tk), lambda i, k: (layer, 0, k)),
                  pl.BlockSpec((tm, N), lambda i, k: (i, 0))],
        out_specs=[pl.BlockSpec((tm, N), lambda i, k: (i, 0))],
        out_shape=[jax.ShapeDtypeStruct((M, N), F32)],
        scratch_shapes=[], vmem_mib=48, args=(a, w, add))
    return out, rode


def _pieces_layout(pieces, wide):
    P = pieces[0].shape[1]
    assert all(p.shape[1] == P for p in pieces) and wide.shape[1] % P == 0
    n_blocks = len(pieces) + wide.shape[1] // P
    assert n_blocks % 2 == 0 and len(pieces) > n_blocks // 2
    return P, n_blocks // 2


def _mm_tn_pieces(a, pieces, wide, *, name):
    T, M = a.shape
    P, n_half = _pieces_layout(pieces, wide)
    n_first = n_half
    n_rest = len(pieces) - n_first
    tk = _tile(T, 1024)
    n_k = T // tk

    def body(*refs):
        a_ref, p_refs, wide_ref = refs[0], refs[1:1 + len(pieces)], refs[1 + len(pieces)]
        o_ref, acc_ref = refs[2 + len(pieces):]
        j, k = pl.program_id(0), pl.program_id(1)

        @pl.when(k == 0)
        def _():
            acc_ref[...] = jnp.zeros_like(acc_ref)

        x = a_ref[...].astype(BF16)

        @pl.when(j == 0)
        def _():
            for p in range(n_first):
                acc_ref[:, p * P:(p + 1) * P] += _dot_tn(x, p_refs[p][...])

        @pl.when(j == 1)
        def _():
            for p in range(n_rest):
                acc_ref[:, p * P:(p + 1) * P] += _dot_tn(x, p_refs[n_first + p][...])
            acc_ref[:, n_rest * P:] += _dot_tn(x, wide_ref[...])

        @pl.when(k == n_k - 1)
        def _():
            o_ref[...] = acc_ref[...].astype(o_ref.dtype)

    first = lambda: pl.BlockSpec((tk, P), lambda j, k: (jnp.where(j == 0, k, n_k - 1), 0))
    rest = lambda: pl.BlockSpec((tk, P), lambda j, k: (jnp.where(j == 1, k, 0), 0))
    return pl.pallas_call(
        body, name=name, grid=(2, n_k),
        in_specs=[pl.BlockSpec((tk, M), lambda j, k: (k, 0))] + [first() for _ in range(n_first)]
                 + [rest() for _ in range(n_rest)]
                 + [pl.BlockSpec((tk, wide.shape[1]), lambda j, k: (jnp.where(j == 1, k, 0), 0))],
        out_specs=pl.BlockSpec((M, n_half * P), lambda j, k: (0, j)),
        out_shape=jax.ShapeDtypeStruct((M, 2 * n_half * P), BF16),
        scratch_shapes=[pltpu.VMEM((M, n_half * P), F32)],
        compiler_params=_cparams(("arbitrary", "arbitrary"), 56),
    )(a, *pieces, wide)


def _mm_nt_pieces_add(pieces, wide, w, layer, add, add_scale, *, name, job=None):
    M = add.shape[0]
    N = w.shape[1]
    P, _ = _pieces_layout(pieces, wide)
    tm = _tile(M, 1024)
    tk = 2 * P
    n_pair = len(pieces) // 2
    assert len(pieces) % 2 == 0 and wide.shape[1] % tk == 0
    n_k = n_pair + wide.shape[1] // tk

    def body(*refs):
        p_refs, wide_ref, w_ref, add_ref, o_ref = refs[:len(pieces)], *refs[len(pieces):]
        k = pl.program_id(1)

        @pl.when(k == 0)
        def _():
            o_ref[...] = add_scale * add_ref[...]

        for pair in range(n_pair):
            @pl.when(k == pair)
            def _(pair=pair):
                o_ref[...] += (_dot_nt(p_refs[2 * pair][...], w_ref[:, :P])
                               + _dot_nt(p_refs[2 * pair + 1][...], w_ref[:, P:]))

        @pl.when(k >= n_pair)
        def _():
            o_ref[...] += _dot_nt(wide_ref[...], w_ref[...])

    (out,), rode = _call_carrying(
        job, body, name=name, grid=(M // tm, n_k),
        in_specs=[pl.BlockSpec((tm, P), lambda i, k, pair=n // 2: (jnp.where(k >= pair, i, jnp.maximum(i - 1, 0)), 0))
                  for n in range(len(pieces))]
                 + [pl.BlockSpec((tm, tk), lambda i, k: (i, jnp.maximum(k - n_pair, 0))),
                    pl.BlockSpec((None, N, tk), lambda i, k: (layer, 0, k)),
                    pl.BlockSpec((tm, N), lambda i, k: (i, 0))],
        out_specs=[pl.BlockSpec((tm, N), lambda i, k: (i, 0))],
        out_shape=[jax.ShapeDtypeStruct((M, N), F32)],
        scratch_shapes=[], vmem_mib=56, args=(*pieces, wide, w, add))
    return out, rode


def _tn_body(k_axis, n_k):
    def body(a_ref, b_ref, o_ref, acc_ref):
        k = pl.program_id(k_axis)

        @pl.when(k == 0)
        def _():
            acc_ref[...] = jnp.zeros_like(acc_ref)
        acc_ref[...] += _dot_tn(a_ref[...].astype(BF16), b_ref[...].astype(BF16))

        @pl.when(k == n_k - 1)
        def _():
            o_ref[...] = acc_ref[...].astype(o_ref.dtype)
    return body


def _mm_tn(a, b, *, tm, tn, name):
    T, M = a.shape
    N = b.shape[1]
    tk = _tile(T, 1024)
    return pl.pallas_call(
        _tn_body(2, T // tk), name=name, grid=(M // tm, N // tn, T // tk),
        in_specs=[pl.BlockSpec((tk, tm), lambda i, j, k: (k, i)),
                  pl.BlockSpec((tk, tn), lambda i, j, k: (k, j))],
        out_specs=pl.BlockSpec((tm, tn), lambda i, j, k: (i, j)),
        out_shape=jax.ShapeDtypeStruct((M, N), BF16),
        scratch_shapes=[pltpu.VMEM((tm, tn), F32)],
        compiler_params=_cparams(("parallel", "parallel", "arbitrary")),
    )(a, b)


def _mm_tn_blocked_pair(a, b1, b2, *, name):
    T, M = a.shape
    S, _, N = b1.shape
    tk = _tile(T, 1024)
    n_k = T // tk

    def body(a_ref, b1_ref, b2_ref, o1_ref, o2_ref, acc1_ref, acc2_ref):
        k = pl.program_id(1)

        @pl.when(k == 0)
        def _():
            acc1_ref[...] = jnp.zeros_like(acc1_ref)
            acc2_ref[...] = jnp.zeros_like(acc2_ref)

        a_t = a_ref[...].astype(BF16)
        acc1_ref[...] += _dot_tn(a_t, b1_ref[...])
        acc2_ref[...] += _dot_tn(a_t, b2_ref[...])

        @pl.when(k == n_k - 1)
        def _():
            o1_ref[...] = acc1_ref[...].astype(o1_ref.dtype)
            o2_ref[...] = acc2_ref[...].astype(o2_ref.dtype)

    blk = lambda: pl.BlockSpec((None, tk, N), lambda s, k: (s, k, 0))
    out = lambda: pl.BlockSpec((None, M, N), lambda s, k: (s, 0, 0))
    return pl.pallas_call(
        body, name=name, grid=(S, n_k),
        in_specs=[pl.BlockSpec((tk, M), lambda s, k: (k, 0)), blk(), blk()],
        out_specs=[out(), out()],
        out_shape=[jax.ShapeDtypeStruct((S, M, N), BF16)] * 2,
        scratch_shapes=[pltpu.VMEM((M, N), F32), pltpu.VMEM((M, N), F32)],
        compiler_params=_cparams(("parallel", "arbitrary")),
    )(a, b1, b2)


def _mm_tn_blocked_a(a, b, *, name):
    S, T, M = a.shape
    N = b.shape[1]
    tk = _tile(T, 1024)
    return pl.pallas_call(
        _tn_body(1, T // tk), name=name, grid=(S, T // tk),
        in_specs=[pl.BlockSpec((None, tk, M), lambda s, k: (s, k, 0)),
                  pl.BlockSpec((tk, N), lambda s, k: (k, 0))],
        out_specs=pl.BlockSpec((None, M, N), lambda s, k: (s, 0, 0)),
        out_shape=jax.ShapeDtypeStruct((S, M, N), BF16),
        scratch_shapes=[pltpu.VMEM((M, N), F32)],
        compiler_params=_cparams(("parallel", "arbitrary")),
    )(a, b)


def _ln_fwd(r, g, b):
    mu = jnp.mean(r, axis=-1, keepdims=True)
    xc = r - mu
    var = jnp.mean(xc * xc, axis=-1, keepdims=True)
    return xc * lax.rsqrt(var + LN_EPS) * g + b


def _ln_bwd(dy, r, g):
    mu = jnp.mean(r, axis=-1, keepdims=True)
    xc = r - mu
    var = jnp.mean(xc * xc, axis=-1, keepdims=True)
    rstd = lax.rsqrt(var + LN_EPS)
    xhat = xc * rstd
    dxh = dy * g
    m1 = jnp.mean(dxh, axis=-1, keepdims=True)
    m2 = jnp.mean(dxh * xhat, axis=-1, keepdims=True)
    return rstd * (dxh - m1 - xhat * m2), xhat


def _lane_is_head0():
    return lax.broadcasted_iota(jnp.int32, (1, LANES), 1) < HEAD_DIM


def _band_shape():
    a = np.arange(A_TQ)[:, None] // CHUNK
    b = np.arange(A_WIN)[None, :] // CHUNK
    return (b >= a) & (b <= a + LEFT_CHUNKS)


def _band_streams(strip):
    return [dict(h=h, n=strip, rows=pl.ds(r0, strip)) for h in range(2) for r0 in range(0, A_TQ, strip)]


def _band_scores(st, i, qh_ref, k2, bias_ref):
    s = _dot_nt(qh_ref[st["h"], st["rows"], :], k2) + bias_ref[st["h"], st["rows"], :]
    c = lax.broadcasted_iota(jnp.int32, (st["n"], A_WIN), 1)
    st["s"] = jnp.where(c >= LEFT_CHUNKS * CHUNK - i * A_TQ, s, NEG)


def _band_softmax(st):
    s = st.pop("s")
    e = jnp.exp(s - jnp.max(s, axis=1, keepdims=True))
    st["p"] = e * (1.0 / jnp.sum(e, axis=1, keepdims=True))


def _attn_a_fwd(qkv, kvpad, bias, *, name, job=None):
    T = qkv.shape[0]
    n_hp = WIDTH // LANES

    def body(q_ref, k_ref, v_ref, bias_ref, o_ref, qh_ref, acc_ref):
        i = pl.program_id(1)
        row0 = pl.multiple_of(i * A_TQ, A_TQ)
        q2 = q_ref[...] * jnp.asarray(QK_SCALE, BF16)
        k2 = k_ref[pl.ds(row0, A_WIN), :]
        v2 = v_ref[pl.ds(row0, A_WIN), :]
        head0 = _lane_is_head0()
        qh_ref[0] = jnp.where(head0, q2, jnp.zeros_like(q2))
        qh_ref[1] = jnp.where(head0, jnp.zeros_like(q2), q2)

        def scores(st):
            _band_scores(st, i, qh_ref, k2, bias_ref)

        def values(st):
            acc_ref[st["h"], st["rows"], :] = _dot(st.pop("p").astype(BF16), v2)

        _skewed(_band_streams(A_STRIP), [scores, _band_softmax, values])
        o_ref[...] = jnp.where(head0, acc_ref[0], acc_ref[1]).astype(o_ref.dtype)

    (out,), rode = _call_carrying(
        job, body, name=name, grid=(n_hp, T // A_TQ),
        in_specs=[pl.BlockSpec((A_TQ, LANES), lambda hp, i: (i, hp)),
                  pl.BlockSpec((T + A_WIN - A_TQ, LANES), lambda hp, i: (0, hp)),
                  pl.BlockSpec((T + A_WIN - A_TQ, LANES), lambda hp, i: (0, hp + n_hp)),
                  pl.BlockSpec((2, A_TQ, A_WIN), lambda hp, i: (hp, 0, 0))],
        out_specs=[pl.BlockSpec((A_TQ, LANES), lambda hp, i: (i, hp))],
        out_shape=[jax.ShapeDtypeStruct((T, WIDTH), BF16)],
        scratch_shapes=[pltpu.VMEM((2, A_TQ, LANES), BF16), pltpu.VMEM((2, A_TQ, LANES), F32)],
        vmem_mib=48, args=(qkv, kvpad, kvpad, bias))
    return out, rode


def _attn_a_bwd(qkv, kvpad, bias, do, *, name, job=None):
    T = qkv.shape[0]
    TP = T + A_WIN - A_TQ
    n_hp = WIDTH // LANES

    def body(q_ref, k_ref, v_ref, bias_ref, do_ref, dq_ref, dko_ref, dvo_ref, db_ref,
             qh_ref, doh_ref, dqa_ref, dk_ref, dv_ref, qs_ref):
        i = pl.program_id(1)

        @pl.when(i == 0)
        def _():
            dk_ref[...] = jnp.zeros_like(dk_ref)
            dv_ref[...] = jnp.zeros_like(dv_ref)
            db_ref[...] = jnp.zeros_like(db_ref)

        row0 = pl.multiple_of(i * A_TQ, A_TQ)
        window = pl.ds(row0, A_WIN)
        scale = jnp.asarray(QK_SCALE, BF16)
        q2 = q_ref[...] * scale
        do2 = do_ref[...]
        k2 = k_ref[window, :]
        k2s = k2 * scale
        v2 = v_ref[window, :]
        head0 = _lane_is_head0()
        zero = jnp.zeros_like(q2)
        qs_ref[...] = q2
        qh_ref[0] = jnp.where(head0, q2, zero)
        qh_ref[1] = jnp.where(head0, zero, q2)
        doh_ref[0] = jnp.where(head0, do2, zero)
        doh_ref[1] = jnp.where(head0, zero, do2)
        dk = [[], []]
        dv = [[], []]

        def scores(st):
            _band_scores(st, i, qh_ref, k2, bias_ref)
            st["dp"] = _dot_nt(doh_ref[st["h"], st["rows"], :], v2)

        def dscores(st):
            _band_softmax(st)
            p, dp = st.pop("p"), st.pop("dp")
            ds = p * (dp - jnp.sum(p * dp, axis=1, keepdims=True))
            db_ref[st["h"], st["rows"], :] += ds
            st["dsb"] = ds.astype(BF16)
            st["pb"] = p.astype(BF16)

        def grads(st):
            h, rows = st["h"], st["rows"]
            dsb = st.pop("dsb")
            dqa_ref[h, rows, :] = _dot(dsb, k2s)
            dk[h].append(_dot_tn(dsb, qs_ref[rows, :]))
            dv[h].append(_dot_tn(st.pop("pb"), do_ref[rows, :]))

        _skewed(_band_streams(A_TQ), [scores, dscores, grads])
        dq_ref[...] = jnp.where(head0, dqa_ref[0], dqa_ref[1]).astype(dq_ref.dtype)
        dk_ref[window, :] += jnp.where(head0, sum(dk[0]), sum(dk[1]))
        dv_ref[window, :] += jnp.where(head0, sum(dv[0]), sum(dv[1]))

        @pl.when(i == T // A_TQ - 1)
        def _():
            dko_ref[...] = dk_ref[TP - T:, :].astype(BF16)
            dvo_ref[...] = dv_ref[TP - T:, :].astype(BF16)

    return _call_carrying(
        job, body, name=name, grid=(n_hp, T // A_TQ),
        in_specs=[pl.BlockSpec((A_TQ, LANES), lambda hp, i: (i, hp)),
                  pl.BlockSpec((TP, LANES), lambda hp, i: (0, hp)),
                  pl.BlockSpec((TP, LANES), lambda hp, i: (0, hp + n_hp)),
                  pl.BlockSpec((2, A_TQ, A_WIN), lambda hp, i: (hp, 0, 0)),
                  pl.BlockSpec((A_TQ, LANES), lambda hp, i: (i, hp))],
        out_specs=[pl.BlockSpec((A_TQ, LANES), lambda hp, i: (i, hp)),
                   pl.BlockSpec((T, LANES), lambda hp, i: (0, hp)),
                   pl.BlockSpec((T, LANES), lambda hp, i: (0, hp)),
                   pl.BlockSpec((2, A_TQ, A_WIN), lambda hp, i: (hp, 0, 0))],
        out_shape=[jax.ShapeDtypeStruct((T, WIDTH), BF16),
                   jax.ShapeDtypeStruct((T, WIDTH), BF16),
                   jax.ShapeDtypeStruct((T, WIDTH), BF16),
                   jax.ShapeDtypeStruct((WIDTH // HEAD_DIM, A_TQ, A_WIN), F32)],
        scratch_shapes=[pltpu.VMEM((2, A_TQ, LANES), BF16), pltpu.VMEM((2, A_TQ, LANES), BF16),
                        pltpu.VMEM((2, A_TQ, LANES), F32),
                        pltpu.VMEM((TP, LANES), F32), pltpu.VMEM((TP, LANES), F32),
                        pltpu.VMEM((A_TQ, LANES), BF16)],
        vmem_mib=56, args=(qkv, kvpad, kvpad, bias, do))


_N_DIAG = 2 * CHUNK - 1
_EXT_TOP = LEFT_CHUNKS * CHUNK + CHUNK - 1 + REL_CLIP


def _toeplitz_bias(rb):
    H = rb.shape[0]
    ext = jnp.concatenate([rb, jnp.broadcast_to(rb[:, N_REL - 1:], (H, _EXT_TOP + 1 - N_REL))], axis=1)
    vec = jnp.stack([ext[:, _EXT_TOP - (_N_DIAG - 1) - CHUNK * k:_EXT_TOP - CHUNK * k + 1]
                     for k in range(LEFT_CHUNKS + 1)], axis=1)
    rev = jnp.pad(vec[:, :, ::-1], ((0, 0), (0, 0), (0, 1)))
    flat = jnp.broadcast_to(rev[:, :, None, :], (H, LEFT_CHUNKS + 1, CHUNK, _N_DIAG + 1))
    skew = flat.reshape(H, LEFT_CHUNKS + 1, -1)[:, :, :CHUNK * _N_DIAG].reshape(H, LEFT_CHUNKS + 1, CHUNK, _N_DIAG)
    blocks = skew[:, :, :, CHUNK - 1:]
    neg = jnp.full((H, CHUNK, CHUNK), NEG, F32)
    rows = [jnp.concatenate([blocks[:, b - a] if 0 <= b - a <= LEFT_CHUNKS else neg for b in range(A_WIN // CHUNK)],
                            axis=2) for a in range(A_TQ // CHUNK)]
    return jnp.concatenate(rows, axis=1)


def _toeplitz_bias_grad(db):
    H = db.shape[0]
    d5 = db.reshape(H, A_TQ // CHUNK, CHUNK, A_WIN // CHUNK, CHUNK)
    g_blocks = jnp.stack([sum(d5[:, a, :, a + k, :] for a in range(A_TQ // CHUNK))
                          for k in range(LEFT_CHUNKS + 1)], axis=1)
    d_skew = jnp.pad(g_blocks, ((0, 0), (0, 0), (0, 0), (CHUNK - 1, 0)))
    d_flat = jnp.pad(d_skew.reshape(H, LEFT_CHUNKS + 1, CHUNK * _N_DIAG), ((0, 0), (0, 0), (0, CHUNK)))
    g_vec = jnp.sum(d_flat.reshape(H, LEFT_CHUNKS + 1, CHUNK, _N_DIAG + 1), axis=2)[:, :, :_N_DIAG][:, :, ::-1]
    g_ext = sum(jnp.pad(g_vec[:, k], ((0, 0), (_EXT_TOP - (_N_DIAG - 1) - CHUNK * k, CHUNK * k)))
                for k in range(LEFT_CHUNKS + 1))
    return jnp.concatenate([g_ext[:, :N_REL - 1], jnp.sum(g_ext[:, N_REL - 1:], axis=1, keepdims=True)], axis=1)


def _split_bf16(x):
    hi = x.astype(BF16)
    lo = (x - hi.astype(F32)).astype(BF16)
    return hi, lo


def _sb_streams(d, strips=None, **tile):
    out = []
    for h in range(2):
        for r in (range(B_TQ // B_TS) if strips is None else strips):
            if d is not None and d > r:
                continue
            out.append(dict(h=h, r=r, rows=pl.ds(r * B_TS, B_TS), diag=(d is not None and d == r), **tile))
    return out


def _sb_sweep(i, car_ref, streams_of, run):
    sub = B_TQ // B_TS
    run([st for d in reversed(range(sub)) for st in streams_of(i * sub + d, d, None)])

    def alive(c):
        return (c[0] < i * sub) & (c[1] > B_DEAD)

    def step(c):
        kb = i * sub - 1 - c[0]
        if sub > 1:
            lower_alive = jnp.max(car_ref[:, B_TS:, :]) > B_DEAD
            lax.cond(lower_alive, lambda: run(streams_of(kb, None, None)), lambda: run(streams_of(kb, None, [0])))
        else:
            run(streams_of(kb, None, None))
        return c[0] + 1, jnp.max(car_ref[...])

    lax.while_loop(alive, step, (jnp.int32(0), jnp.float32(0.0)))


def _piece_rows(st, p):
    return pl.ds(st["r"] * B_TS + p, B_PIECE)


def _rows_cat(parts):
    return jnp.concatenate(parts, axis=0)


def _skewed(streams, stages):
    for t in range(len(streams) + len(stages) - 1):
        for s, st in enumerate(streams):
            if 0 <= t - s < len(stages):
                stages[t - s](st)


def _sb_logs(st, z2):
    log_beta, log_keep, keep_bf = [], [], []
    for p in range(0, B_TS, B_PIECE):
        z = z2[p:p + B_PIECE]
        lp2 = jnp.log(1.0 + jnp.exp2(-jnp.abs(z))) * LOG2E
        lb = jnp.minimum(z, 0.0) - lp2
        lk = lb - z
        if st["diag"]:
            lk = jnp.where(_strict_lower(p), lk, 0.0)
        log_beta.append(lb)
        log_keep.append(lk)
        keep_bf.append(lk.astype(BF16))
    st["log_beta"] = _rows_cat(log_beta)
    st["log_keep"] = _rows_cat(log_keep)
    st["keep_bf"] = _rows_cat(keep_bf)


def _strict_lower(p):
    t = p + lax.broadcasted_iota(jnp.int32, (B_PIECE, B_TS), 0)
    s = lax.broadcasted_iota(jnp.int32, (B_PIECE, B_TS), 1)
    return s < t


def _tri(strict):
    j = lax.broadcasted_iota(jnp.int32, (B_TS, B_TS), 0)
    s = lax.broadcasted_iota(jnp.int32, (B_TS, B_TS), 1)
    return jnp.where(j > s if strict else j >= s, 1.0, 0.0).astype(BF16)


def _call_carrying(job, body, *, name, grid, in_specs, out_specs, out_shape, scratch_shapes, vmem_mib, args):
    n_in, n_out, n_scr = len(in_specs), len(out_specs), len(scratch_shapes)
    if job is None:
        res = pl.pallas_call(body, name=name, grid=grid, in_specs=in_specs, out_specs=out_specs,
                             out_shape=out_shape, scratch_shapes=scratch_shapes,
                             compiler_params=_cparams(("arbitrary",) * len(grid), vmem_mib))(*args)
        return res, []
    j_in, j_out = len(job.arrays), len(job.out_shape)
    hbm = pl.BlockSpec(memory_space=pltpu.HBM)

    def carrying(*refs):
        refs = list(refs)
        ins, refs = refs[:n_in], refs[n_in:]
        j_ins, refs = refs[:j_in], refs[j_in:]
        outs, refs = refs[:n_out], refs[n_out:]
        j_outs, refs = refs[:j_out], refs[j_out:]
        scr, sems = refs[:n_scr], refs[n_scr:]
        first = functools.reduce(jnp.logical_and, [pl.program_id(d) == 0 for d in range(len(grid))])
        last = functools.reduce(jnp.logical_and, [pl.program_id(d) == grid[d] - 1 for d in range(len(grid))])

        @pl.when(first)
        def _():
            job.start(j_ins, j_outs, sems)

        body(*ins, *outs, *scr)

        @pl.when(last)
        def _():
            job.wait(j_ins, j_outs, sems)

    res = pl.pallas_call(
        carrying, name=name, grid=grid,
        in_specs=list(in_specs) + [hbm] * j_in, out_specs=list(out_specs) + [hbm] * j_out,
        out_shape=list(out_shape) + job.out_shape, scratch_shapes=list(scratch_shapes) + job.scratch(),
        compiler_params=_cparams(("arbitrary",) * len(grid), vmem_mib))(*args, *job.arrays)
    return res[:n_out], res[n_out:]


def _attn_b_fwd(qkv, *, col0, name, job=None):
    T = qkv.shape[0]
    n_hp = WIDTH // LANES
    sub = B_TQ // B_TS

    def body(q_ref, k_ref, v_ref, o_ref, acc_ref, car_ref, qh_ref):
        i = pl.program_id(1)
        q2 = q_ref[...]
        head0 = _lane_is_head0()
        qh_ref[0] = jnp.where(head0, q2, jnp.zeros_like(q2))
        qh_ref[1] = jnp.where(head0, jnp.zeros_like(q2), q2)
        tri_s = _tri(True)
        acc_ref[...] = jnp.zeros_like(acc_ref)
        car_ref[...] = jnp.zeros_like(car_ref)

        def streams_of(kb, d, strips):
            keys = pl.ds(pl.multiple_of(kb * B_TS, B_TS), B_TS)
            return _sb_streams(d, strips, k2=k_ref[keys, :], v2=v_ref[keys, :])

        def scores(st):
            st["z2"] = _dot_nt(qh_ref[st["h"], st["rows"], :], st.pop("k2")) * (QK_SCALE * LOG2E)

        def logs(st):
            _sb_logs(st, st.pop("z2"))

        def suffix(st):
            st["suffix"] = _dot(st.pop("keep_bf"), tri_s)

        def weights(st):
            log_beta, suffix, log_keep = st.pop("log_beta"), st.pop("suffix"), st.pop("log_keep")
            wb = []
            for p in range(0, B_TS, B_PIECE):
                rows = _piece_rows(st, p)
                car = car_ref[st["h"], rows, :]
                w = jnp.exp2(log_beta[p:p + B_PIECE] + suffix[p:p + B_PIECE] + car)
                if st["diag"]:
                    w = jnp.where(_strict_lower(p), w, 0.0)
                wb.append(w.astype(BF16))
                car_ref[st["h"], rows, :] = car + jnp.sum(log_keep[p:p + B_PIECE], axis=1, keepdims=True)
            st["wb"] = _rows_cat(wb)

        def values(st):
            acc_ref[st["h"], st["rows"], :] += _dot(st.pop("wb"), st.pop("v2"))

        _sb_sweep(i, car_ref, streams_of, lambda sts: _skewed(sts, [scores, logs, suffix, weights, values]))
        o_ref[...] = jnp.where(head0, acc_ref[0], acc_ref[1])

    (out,), rode = _call_carrying(
        job, body, name=name, grid=(n_hp, T // B_TQ),
        in_specs=[pl.BlockSpec((B_TQ, LANES), lambda hp, i: (i, hp + col0)),
                  pl.BlockSpec((T, LANES), lambda hp, i: (0, hp + col0 + n_hp)),
                  pl.BlockSpec((T, LANES), lambda hp, i: (0, hp + col0 + 2 * n_hp))],
        out_specs=[pl.BlockSpec((B_TQ, LANES), lambda hp, i: (i, hp))],
        out_shape=[jax.ShapeDtypeStruct((T, WIDTH), F32)],
        scratch_shapes=[pltpu.VMEM((2, B_TQ, LANES), F32), pltpu.VMEM((2, B_TQ, 1), F32),
                        pltpu.VMEM((2, B_TQ, LANES), BF16)],
        vmem_mib=48, args=(qkv, qkv, qkv))
    return out, rode


def _attn_b_bwd(qkv, out, do, *, col0, name, job=None):
    T = qkv.shape[0]
    n_hp = WIDTH // LANES
    sub = B_TQ // B_TS

    def body(q_ref, k_ref, v_ref, o_ref, do_ref, dq_ref, dko_ref, dvo_ref,
             dqa_ref, car_ref, carr_ref, tot_ref, qh_ref, doh_ref, qs_ref, dk_ref, dv_ref):
        i = pl.program_id(1)

        @pl.when(i == 0)
        def _():
            dk_ref[...] = jnp.zeros_like(dk_ref)
            dv_ref[...] = jnp.zeros_like(dv_ref)

        q2 = q_ref[...]
        do2 = do_ref[...]
        head0 = _lane_is_head0()
        zero = jnp.zeros_like(q2)
        qh_ref[0] = jnp.where(head0, q2, zero)
        qh_ref[1] = jnp.where(head0, zero, q2)
        doh_ref[0] = jnp.where(head0, do2, zero)
        doh_ref[1] = jnp.where(head0, zero, do2)
        scale = jnp.asarray(QK_SCALE, BF16)
        qs_ref[...] = q2 * scale
        tri_s = _tri(True)
        tri_i = _tri(False)
        prod = do2.astype(F32) * o_ref[...]
        tot_ref[0] = jnp.sum(jnp.where(head0, prod, 0.0), axis=1, keepdims=True)
        tot_ref[1] = jnp.sum(jnp.where(head0, 0.0, prod), axis=1, keepdims=True)
        dqa_ref[...] = jnp.zeros_like(dqa_ref)
        car_ref[...] = jnp.zeros_like(car_ref)
        carr_ref[...] = jnp.zeros_like(carr_ref)

        def streams_of(kb, d, strips):
            keys = pl.ds(pl.multiple_of(kb * B_TS, B_TS), B_TS)
            k2 = k_ref[keys, :]
            return _sb_streams(d, strips, keys=keys, k2=k2, v2=v_ref[keys, :], k2s=k2 * scale)

        def run(streams):
            def scores(st):
                st["z2"] = _dot_nt(qh_ref[st["h"], st["rows"], :], st.pop("k2")) * (QK_SCALE * LOG2E)
                st["dw"] = _dot_nt(doh_ref[st["h"], st["rows"], :], st.pop("v2"))

            def logs(st):
                _sb_logs(st, st.pop("z2"))

            def suffix(st):
                st["suffix"] = _dot(st.pop("keep_bf"), tri_s)

            def weights(st):
                h = st["h"]
                suffix, dw = st.pop("suffix"), st.pop("dw")
                wb, dlog, hi, lo = [], [], [], []
                for p in range(0, B_TS, B_PIECE):
                    rows = _piece_rows(st, p)
                    car = car_ref[h, rows, :]
                    w = jnp.exp2(st["log_beta"][p:p + B_PIECE] + suffix[p:p + B_PIECE] + car)
                    if st["diag"]:
                        w = jnp.where(_strict_lower(p), w, 0.0)
                    w = w.astype(BF16)
                    dl = w.astype(F32) * dw[p:p + B_PIECE]
                    dl_hi, dl_lo = _split_bf16(dl)
                    wb.append(w)
                    dlog.append(dl)
                    hi.append(dl_hi)
                    lo.append(dl_lo)
                    car_ref[h, rows, :] = car + jnp.sum(st["log_keep"][p:p + B_PIECE], axis=1, keepdims=True)
                st["wb"], st["dlog"], st["hi"], st["lo"] = _rows_cat(wb), _rows_cat(dlog), _rows_cat(hi), _rows_cat(lo)

            def later(st):
                st["later"] = _dot(st.pop("hi"), tri_i) + _dot(st.pop("lo"), tri_i)

            def dscores(st):
                h = st["h"]
                later, dlog = st.pop("later"), st.pop("dlog")
                log_keep, log_beta = st.pop("log_keep"), st.pop("log_beta")
                dzb = []
                for p in range(0, B_TS, B_PIECE):
                    rows = _piece_rows(st, p)
                    pc = slice(p, p + B_PIECE)
                    carr = carr_ref[h, rows, :]
                    earlier = tot_ref[h, rows, :] - (later[pc] + carr)
                    dz = dlog[pc] * jnp.exp2(log_keep[pc]) - jnp.exp2(log_beta[pc]) * earlier
                    if st["diag"]:
                        dz = jnp.where(_strict_lower(p), dz, 0.0)
                    dzb.append(dz.astype(BF16))
                    carr_ref[h, rows, :] = carr + jnp.sum(dlog[pc], axis=1, keepdims=True)
                st["dzb"] = _rows_cat(dzb)

            def grads(st):
                h, rows, keys = st["h"], st["rows"], st["keys"]
                mine = head0 if h == 0 else jnp.logical_not(head0)
                dzb = st.pop("dzb")
                dqa_ref[h, rows, :] += _dot(dzb, st.pop("k2s"))
                dk_ref[keys, :] += jnp.where(mine, _dot_tn(dzb, qs_ref[rows, :]), 0.0)
                dv_ref[keys, :] += jnp.where(mine, _dot_tn(st.pop("wb"), do_ref[rows, :]), 0.0)

            _skewed(streams, [scores, logs, suffix, weights, later, dscores, grads])

        _sb_sweep(i, car_ref, streams_of, run)
        dq_ref[...] = jnp.where(head0, dqa_ref[0], dqa_ref[1]).astype(dq_ref.dtype)

        @pl.when(i == T // B_TQ - 1)
        def _():
            dko_ref[...] = dk_ref[...].astype(BF16)
            dvo_ref[...] = dv_ref[...].astype(BF16)

    return _call_carrying(
        job, body, name=name, grid=(n_hp, T // B_TQ),
        in_specs=[pl.BlockSpec((B_TQ, LANES), lambda hp, i: (i, hp + col0)),
                  pl.BlockSpec((T, LANES), lambda hp, i: (0, hp + col0 + n_hp)),
                  pl.BlockSpec((T, LANES), lambda hp, i: (0, hp + col0 + 2 * n_hp)),
                  pl.BlockSpec((B_TQ, LANES), lambda hp, i: (i, hp)),
                  pl.BlockSpec((B_TQ, LANES), lambda hp, i: (i, hp))],
        out_specs=[pl.BlockSpec((B_TQ, LANES), lambda hp, i: (i, hp)),
                   pl.BlockSpec((T, LANES), lambda hp, i: (0, hp)),
                   pl.BlockSpec((T, LANES), lambda hp, i: (0, hp))],
        out_shape=[jax.ShapeDtypeStruct((T, WIDTH), BF16),
                   jax.ShapeDtypeStruct((T, WIDTH), BF16),
                   jax.ShapeDtypeStruct((T, WIDTH), BF16)],
        scratch_shapes=[pltpu.VMEM((2, B_TQ, LANES), F32), pltpu.VMEM((2, B_TQ, 1), F32),
                        pltpu.VMEM((2, B_TQ, 1), F32), pltpu.VMEM((2, B_TQ, 1), F32),
                        pltpu.VMEM((2, B_TQ, LANES), BF16), pltpu.VMEM((2, B_TQ, LANES), BF16),
                        pltpu.VMEM((B_TQ, LANES), BF16),
                        pltpu.VMEM((T, LANES), F32), pltpu.VMEM((T, LANES), F32)],
        vmem_mib=56, args=(qkv, qkv, qkv, out, do))


def _gated_mix(oa_ref, ob_ref, g_ref, bg_ref, wpa_ref, wpb_ref, D):
    ya = _dot(oa_ref[...].astype(BF16), wpa_ref[...])
    yb = _dot(ob_ref[...].astype(BF16), wpb_ref[...])
    sa = jax.nn.sigmoid(g_ref[:, :D] + bg_ref[:, :D])
    sb = jax.nn.sigmoid(g_ref[:, D:] + bg_ref[:, D:])
    return ya, yb, sa, sb


def _proj_fwd(oa, ob, g, bg, wpa, wpb, wo, wl, xin, lng, lnb, layer, *, alpha, name):
    T, D = xin.shape
    tm = _tile(T, 512)
    row = lambda i: (i, 0)
    wspec = lambda r, c: pl.BlockSpec((None, r, c), lambda i: (wl, 0, 0))
    vec = lambda c: pl.BlockSpec((None, 1, c), lambda i: (layer, 0, 0))

    def body(oa_ref, ob_ref, g_ref, bg_ref, wpa_ref, wpb_ref, wo_ref, x_ref, lg_ref, lb_ref, x1_ref, r1_ref, x1b_ref):
        ya, yb, sa, sb = _gated_mix(oa_ref, ob_ref, g_ref, bg_ref, wpa_ref, wpb_ref, D)
        mix = _dot((sa * ya + sb * yb).astype(BF16), wo_ref[...])
        r1 = alpha * x_ref[...] + mix
        r1_ref[...] = r1
        x1 = _ln_fwd(r1, lg_ref[...], lb_ref[...])
        x1_ref[...] = x1
        x1b_ref[...] = x1.astype(BF16)

    return pl.pallas_call(
        body, name=name, grid=(T // tm,),
        in_specs=[pl.BlockSpec((tm, WIDTH), row), pl.BlockSpec((tm, WIDTH), row), pl.BlockSpec((tm, 2 * D), row),
                  vec(2 * D), wspec(WIDTH, D), wspec(WIDTH, D), wspec(D, D),
                  pl.BlockSpec((tm, D), row), vec(D), vec(D)],
        out_specs=[pl.BlockSpec((tm, D), row), pl.BlockSpec((tm, D), row), pl.BlockSpec((tm, D), row)],
        out_shape=[jax.ShapeDtypeStruct((T, D), F32), jax.ShapeDtypeStruct((T, D), F32),
                   jax.ShapeDtypeStruct((T, D), BF16)],
        compiler_params=_cparams(("arbitrary",), 56),
    )(oa, ob, g, bg, wpa, wpb, wo, xin, lng, lnb)


def _proj_bwd(dx1, r1, lng, oa, ob, g, bg, wpa, wpb, wo, wl, layer, *, name):
    T, D = dx1.shape
    tm = _tile(T, 512)
    row = lambda i: (i, 0)
    fixed = lambda i: (0, 0)
    wspec = lambda r, c: pl.BlockSpec((None, r, c), lambda i: (wl, 0, 0))
    vec = lambda c: pl.BlockSpec((None, 1, c), lambda i: (layer, 0, 0))

    def body(dx_ref, r1_ref, lg_ref, oa_ref, ob_ref, g_ref, bg_ref, wpa_ref, wpb_ref, wo_ref,
             dr_ref, mix_ref, dya_ref, dyb_ref, dg_ref, doa_ref, dob_ref, dlg_ref, dlb_ref, dbg_ref):
        @pl.when(pl.program_id(0) == 0)
        def _():
            dlg_ref[...] = jnp.zeros_like(dlg_ref)
            dlb_ref[...] = jnp.zeros_like(dlb_ref)
            dbg_ref[...] = jnp.zeros_like(dbg_ref)

        dx = dx_ref[...]
        dr, xhat = _ln_bwd(dx, r1_ref[...], lg_ref[...])
        dr_ref[...] = dr
        dlg_ref[...] += jnp.sum(dx * xhat, axis=0, keepdims=True)
        dlb_ref[...] += jnp.sum(dx, axis=0, keepdims=True)
        dmix = _dot_nt(dr.astype(BF16), wo_ref[...])
        ya, yb, sa, sb = _gated_mix(oa_ref, ob_ref, g_ref, bg_ref, wpa_ref, wpb_ref, D)
        mix_ref[...] = (sa * ya + sb * yb).astype(BF16)
        dya = (dmix * sa).astype(BF16)
        dyb = (dmix * sb).astype(BF16)
        dya_ref[...] = dya
        dyb_ref[...] = dyb
        dga = dmix * ya * (sa * (1.0 - sa))
        dgb = dmix * yb * (sb * (1.0 - sb))
        dg_ref[:, :D] = dga.astype(BF16)
        dg_ref[:, D:] = dgb.astype(BF16)
        dbg_ref[:, :D] += jnp.sum(dga, axis=0, keepdims=True)
        dbg_ref[:, D:] += jnp.sum(dgb, axis=0, keepdims=True)
        doa_ref[...] = _dot_nt(dya, wpa_ref[...]).astype(BF16)
        dob_ref[...] = _dot_nt(dyb, wpb_ref[...]).astype(BF16)

    return pl.pallas_call(
        body, name=name, grid=(T // tm,),
        in_specs=[pl.BlockSpec((tm, D), row), pl.BlockSpec((tm, D), row), vec(D),
                  pl.BlockSpec((tm, WIDTH), row), pl.BlockSpec((tm, WIDTH), row), pl.BlockSpec((tm, 2 * D), row),
                  vec(2 * D), wspec(WIDTH, D), wspec(WIDTH, D), wspec(D, D)],
        out_specs=[pl.BlockSpec((tm, D), row), pl.BlockSpec((tm, D), row), pl.BlockSpec((tm, D), row),
                   pl.BlockSpec((tm, D), row), pl.BlockSpec((tm, 2 * D), row),
                   pl.BlockSpec((tm, WIDTH), row), pl.BlockSpec((tm, WIDTH), row),
                   pl.BlockSpec((1, D), fixed), pl.BlockSpec((1, D), fixed), pl.BlockSpec((1, 2 * D), fixed)],
        out_shape=[jax.ShapeDtypeStruct((T, D), F32), jax.ShapeDtypeStruct((T, D), BF16),
                   jax.ShapeDtypeStruct((T, D), BF16), jax.ShapeDtypeStruct((T, D), BF16),
                   jax.ShapeDtypeStruct((T, 2 * D), BF16),
                   jax.ShapeDtypeStruct((T, WIDTH), BF16), jax.ShapeDtypeStruct((T, WIDTH), BF16),
                   jax.ShapeDtypeStruct((1, D), F32), jax.ShapeDtypeStruct((1, D), F32),
                   jax.ShapeDtypeStruct((1, 2 * D), F32)],
        compiler_params=_cparams(("arbitrary",), 56),
    )(dx1, r1, lng, oa, ob, g, bg, wpa, wpb, wo)


def _ffn_fwd(x1, wfi, wfo, wl, lng, lnb, layer, *, alpha, name, job=None):
    T, D = x1.shape
    tf = wfi.shape[-1]
    nj = wfi.shape[0] // 2
    tm = _tile(T, 1024)
    vec = lambda c: pl.BlockSpec((None, 1, c), lambda i, j: (layer, 0, 0))

    def body(x_ref, wg_ref, wu_ref, wo_ref, lg_ref, lb_ref, gs_ref, us_ref, r2_ref, x2_ref, acc_ref, xb_ref):
        j = pl.program_id(1)

        @pl.when(j == 0)
        def _():
            xb_ref[...] = x_ref[...].astype(BF16)
            acc_ref[...] = jnp.zeros_like(acc_ref)

        gv = _dot(xb_ref[...], wg_ref[...])
        uv = _dot(xb_ref[...], wu_ref[...])
        gs_ref[...] = gv.astype(gs_ref.dtype)
        us_ref[...] = uv.astype(us_ref.dtype)
        act = gv * jax.nn.sigmoid(gv) * uv
        acc_ref[...] += _dot(act.astype(BF16), wo_ref[...])

        @pl.when(j == nj - 1)
        def _():
            r2 = alpha * x_ref[...] + acc_ref[...]
            r2_ref[...] = r2
            x2_ref[...] = _ln_fwd(r2, lg_ref[...], lb_ref[...])

    return _call_carrying(
        job, body, name=name, grid=(T // tm, nj),
        in_specs=[pl.BlockSpec((tm, D), lambda i, j: (i, 0)),
                  pl.BlockSpec((None, None, D, tf), lambda i, j: (j, wl, 0, 0)),
                  pl.BlockSpec((None, None, D, tf), lambda i, j: (j + nj, wl, 0, 0)),
                  pl.BlockSpec((None, tf, D), lambda i, j: (wl, j, 0)),
                  vec(D), vec(D)],
        out_specs=[pl.BlockSpec((None, tm, tf), lambda i, j: (j, i, 0)),
                   pl.BlockSpec((None, tm, tf), lambda i, j: (j, i, 0)),
                   pl.BlockSpec((tm, D), lambda i, j: (i, 0)),
                   pl.BlockSpec((tm, D), lambda i, j: (i, 0))],
        out_shape=[jax.ShapeDtypeStruct((nj, T, tf), BF16), jax.ShapeDtypeStruct((nj, T, tf), BF16),
                   jax.ShapeDtypeStruct((T, D), F32), jax.ShapeDtypeStruct((T, D), F32)],
        scratch_shapes=[pltpu.VMEM((tm, D), F32), pltpu.VMEM((tm, D), BF16)],
        vmem_mib=56, args=(x1, wfi, wfi, wfo, lng, lnb))


def _ffn_bwd(dx2, r2, lng, gs, us, wfi, wfo, wl, layer, *, alpha, name, job=None):
    T, D = dx2.shape
    tf = wfi.shape[-1]
    nj = wfi.shape[0] // 2
    tm = _tile(T, 512)
    vec = lambda c: pl.BlockSpec((None, 1, c), lambda i, j: (layer, 0, 0))
    blk = lambda: pl.BlockSpec((None, tm, tf), lambda i, j: (j, i, 0))

    def body(dx_ref, r2_ref, lg_ref, gs_ref, us_ref, wg_ref, wu_ref, wo_ref,
             dr_ref, act_ref, dg_ref, du_ref, dx1_ref, dlg_ref, dlb_ref, acc_ref, drb_ref):
        i = pl.program_id(0)
        j = pl.program_id(1)

        @pl.when((i == 0) & (j == 0))
        def _():
            dlg_ref[...] = jnp.zeros_like(dlg_ref)
            dlb_ref[...] = jnp.zeros_like(dlb_ref)

        @pl.when(j == 0)
        def _():
            dx = dx_ref[...]
            dr, xhat = _ln_bwd(dx, r2_ref[...], lg_ref[...])
            dlg_ref[...] += jnp.sum(dx * xhat, axis=0, keepdims=True)
            dlb_ref[...] += jnp.sum(dx, axis=0, keepdims=True)
            drb_ref[...] = dr.astype(BF16)
            dr_ref[...] = dr.astype(BF16)
            acc_ref[...] = alpha * dr

        dact = _dot_nt(drb_ref[...], wo_ref[...])
        gv = gs_ref[...].astype(F32)
        uv = us_ref[...].astype(F32)
        s = jax.nn.sigmoid(gv)
        silu = gv * s
        act_ref[...] = (silu * uv).astype(BF16)
        dg = (dact * uv * (s * (1.0 + gv * (1.0 - s)))).astype(BF16)
        du = (dact * silu).astype(BF16)
        dg_ref[...] = dg
        du_ref[...] = du
        acc_ref[...] += _dot_nt(dg, wg_ref[...]) + _dot_nt(du, wu_ref[...])

        @pl.when(j == nj - 1)
        def _():
            dx1_ref[...] = acc_ref[...]

    return _call_carrying(
        job, body, name=name, grid=(T // tm, nj),
        in_specs=[pl.BlockSpec((tm, D), lambda i, j: (i, 0)), pl.BlockSpec((tm, D), lambda i, j: (i, 0)), vec(D),
                  blk(), blk(),
                  pl.BlockSpec((None, None, D, tf), lambda i, j: (j, wl, 0, 0)),
                  pl.BlockSpec((None, None, D, tf), lambda i, j: (j + nj, wl, 0, 0)),
                  pl.BlockSpec((None, tf, D), lambda i, j: (wl, j, 0))],
        out_specs=[pl.BlockSpec((tm, D), lambda i, j: (i, 0)), blk(), blk(), blk(),
                   pl.BlockSpec((tm, D), lambda i, j: (i, 0)),
                   pl.BlockSpec((1, D), lambda i, j: (0, 0)), pl.BlockSpec((1, D), lambda i, j: (0, 0))],
        out_shape=[jax.ShapeDtypeStruct((T, D), BF16),
                   jax.ShapeDtypeStruct((nj, T, tf), BF16), jax.ShapeDtypeStruct((nj, T, tf), BF16),
                   jax.ShapeDtypeStruct((nj, T, tf), BF16),
                   jax.ShapeDtypeStruct((T, D), F32),
                   jax.ShapeDtypeStruct((1, D), F32), jax.ShapeDtypeStruct((1, D), F32)],
        scratch_shapes=[pltpu.VMEM((tm, D), F32), pltpu.VMEM((tm, D), BF16)],
        vmem_mib=56, args=(dx2, r2, lng, gs, us, wfi, wfi, wfo))


def _loss_head(y, target, *, name):
    T, D = y.shape
    tm = _tile(T, 1024)

    def body(y_ref, t_ref, dy_ref, sq_ref):
        @pl.when(pl.program_id(0) == 0)
        def _():
            sq_ref[...] = jnp.zeros_like(sq_ref)
        err = y_ref[...] - t_ref[...]
        dy_ref[...] = err * (1.0 / D)
        sq_ref[...] += jnp.sum(err * err, axis=0, keepdims=True)

    return pl.pallas_call(
        body, name=name, grid=(T // tm,),
        in_specs=[pl.BlockSpec((tm, D), lambda i: (i, 0)), pl.BlockSpec((tm, D), lambda i: (i, 0))],
        out_specs=[pl.BlockSpec((tm, D), lambda i: (i, 0)), pl.BlockSpec((1, D), lambda i: (0, 0))],
        out_shape=[jax.ShapeDtypeStruct((T, D), F32), jax.ShapeDtypeStruct((1, D), F32)],
        compiler_params=_cparams(("arbitrary",)),
    )(y, target)


def _my_place():
    return lax.axis_index("x"), lax.axis_index("y"), lax.axis_index("c")


def _peer(place, k):
    x, y, c = place
    return (1 - x if k & 4 else x, 1 - y if k & 2 else y, 1 - c if k & 1 else c)


def _logical(place):
    x, y, c = place
    return 4 * x + 2 * y + c


def _block_of(ref, mode, idx):
    if mode == "blk":
        return ref.at[idx]
    if mode == "col":
        size = ref.shape[2] // N_DEV
        return ref.at[:, :, pl.ds(pl.multiple_of(idx * size, size), size)]
    size = ref.shape[1] // N_DEV
    return ref.at[:, pl.ds(pl.multiple_of(idx * size, size), size), :]


def _full_shape(shard, mode):
    if mode == "blk":
        return (N_DEV,) + shard.shape
    if mode == "col":
        return shard.shape[:2] + (N_DEV * shard.shape[2],)
    return (shard.shape[0], N_DEV * shard.shape[1], shard.shape[2])


class _Exchange:
    def __init__(self, arrays, out_shape, build):
        self.arrays = list(arrays)
        self.out_shape = list(out_shape)
        self.build = build

    def scratch(self):
        n = len(self.arrays)
        return [pltpu.SemaphoreType.DMA((n * N_DEV,)), pltpu.SemaphoreType.DMA((n * N_DEV,)),
                pltpu.SemaphoreType.DMA((n,))]

    def start(self, ins, outs, sems):
        for cp in self.build(ins, outs, *sems):
            cp.start()

    def wait(self, ins, outs, sems):
        for cp in self.build(ins, outs, *sems):
            cp.wait()

    def run(self, name):
        n_in, n_out = len(self.arrays), len(self.out_shape)
        hbm = pl.BlockSpec(memory_space=pltpu.HBM)

        def body(*refs):
            ins, outs, sems = refs[:n_in], refs[n_in:n_in + n_out], refs[n_in + n_out:]
            self.start(ins, outs, sems)
            self.wait(ins, outs, sems)

        return pl.pallas_call(
            body, name=name, in_specs=[hbm] * n_in, out_specs=[hbm] * n_out,
            out_shape=self.out_shape, scratch_shapes=self.scratch(),
        )(*self.arrays)


def _copies_to_all(src_of, dst_of, n, send, recv, local):
    me = _my_place()
    copies = []
    for a in range(n):
        copies.append(pltpu.make_async_copy(src_of(a, _logical(me)), dst_of(a), local.at[a]))
        for k in range(1, N_DEV):
            peer = _peer(me, k)
            copies.append(pltpu.make_async_remote_copy(
                src_ref=src_of(a, _logical(peer)), dst_ref=dst_of(a),
                send_sem=send.at[a * N_DEV + k], recv_sem=recv.at[a * N_DEV + k],
                device_id=peer, device_id_type=MESH))
    return copies


def _gather_job(shards, modes):
    def build(ins, outs, send, recv, local):
        my_id = _logical(_my_place())
        return _copies_to_all(lambda a, dev: ins[a], lambda a: _block_of(outs[a], modes[a], my_id),
                              len(shards), send, recv, local)

    return _Exchange(shards, [jax.ShapeDtypeStruct(_full_shape(s, m), s.dtype) for s, m in zip(shards, modes)], build)


def _gather_via_sibling(shard, mode, *, name):
    hbm = pl.BlockSpec(memory_space=pltpu.HBM)

    def body(x_ref, o_ref, send, recv, local):
        x, y, c = _my_place()
        me, sibling = (x, y, c), (x, y, 1 - c)
        chips = [(1 - x, y), (x, 1 - y), (1 - x, 1 - y)]

        def copy(k, block, to, src=None):
            dst = _block_of(o_ref, mode, _logical(block))
            return pltpu.make_async_remote_copy(src_ref=dst if src is None else src, dst_ref=dst,
                                                send_sem=send.at[k], recv_sem=recv.at[k],
                                                device_id=to, device_id_type=MESH)

        mine = pltpu.make_async_copy(x_ref, _block_of(o_ref, mode, _logical(me)), local)
        mine.start()
        first = [copy(0, me, sibling, src=x_ref)]
        first += [copy(1 + j, me, (*chip, c), src=x_ref) for j, chip in enumerate(chips)]
        for cp in first:
            cp.start()
        passed = [copy(4 + j, (*chip, c), sibling) for j, chip in enumerate(chips)]
        for j, chip in enumerate(chips):
            copy(1 + j, (*chip, c), me).wait_recv()
            passed[j].start()
        copy(0, sibling, me).wait_recv()
        for j, chip in enumerate(chips):
            copy(4 + j, (*chip, 1 - c), me).wait_recv()
        for cp in first + passed:
            cp.wait_send()
        mine.wait()

    return pl.pallas_call(
        body, name=name, in_specs=[hbm], out_specs=hbm,
        out_shape=jax.ShapeDtypeStruct(_full_shape(shard, mode), shard.dtype),
        scratch_shapes=[pltpu.SemaphoreType.DMA((7,)), pltpu.SemaphoreType.DMA((7,)), pltpu.SemaphoreType.DMA],
    )(shard)


def _grad_block(ref, mode, idx):
    if mode == "blk":
        return ref.at[idx]
    if mode == "col":
        size = ref.shape[1] // N_DEV
        return ref.at[:, pl.ds(pl.multiple_of(idx * size, size), size)]
    size = ref.shape[0] // N_DEV
    return ref.at[pl.ds(pl.multiple_of(idx * size, size), size), :]


def _grad_shard_shape(g, mode):
    if mode == "blk":
        return g.shape[1:]
    if mode == "col":
        return (g.shape[0], g.shape[1] // N_DEV)
    return (g.shape[0] // N_DEV, g.shape[1])


def _grads_job(groups, modes):
    flat = [(g, w, l) for w, per_w in enumerate(groups) for l, g in enumerate(per_w)]

    def build(ins, outs, send, recv, local):
        my_id = _logical(_my_place())
        return _copies_to_all(lambda a, dev: _grad_block(ins[a], modes[flat[a][1]], dev),
                              lambda a: outs[flat[a][1]].at[my_id, flat[a][2]],
                              len(flat), send, recv, local)

    out_shape = [jax.ShapeDtypeStruct((N_DEV, len(per_w)) + _grad_shard_shape(per_w[0], m), per_w[0].dtype)
                 for per_w, m in zip(groups, modes)]
    return _Exchange([g for g, _, _ in flat], out_shape, build)


def _adamw(w, g, m, v):
    m = ADAM_B1 * m + (1.0 - ADAM_B1) * g
    v = ADAM_B2 * v + (1.0 - ADAM_B2) * (g * g)
    m_hat = m / (1.0 - ADAM_B1 ** ADAM_STEP)
    v_hat = v / (1.0 - ADAM_B2 ** ADAM_STEP)
    delta = -ADAM_LR * (m_hat / (jnp.sqrt(v_hat) + ADAM_EPS) + ADAM_WD * w)
    return delta, m, v


def _sum_slots_adamw(slots, w, m, v, *, name):
    n_l = len(slots)
    R, C = slots[0].shape[1:]
    tr = _tile(R, 256)
    n_r = R // tr

    def body(*refs):
        s_refs = refs[:n_l]
        w_ref, m_ref, v_ref, g_out, d_out, m_out, v_out = refs[n_l:]
        for layer in range(n_l):
            @pl.when(pl.program_id(0) == layer)
            def _(s_ref=s_refs[layer]):
                g = s_ref[0].astype(F32)
                for s in range(1, N_DEV):
                    g = g + s_ref[s].astype(F32)
                delta, m_new, v_new = _adamw(w_ref[...], g, m_ref[...], v_ref[...])
                g_out[...] = g
                d_out[...] = delta
                m_out[...] = m_new
                v_out[...] = v_new

    slot_spec = lambda layer: pl.BlockSpec((N_DEV, tr, C), lambda l, i: (0, jnp.where(l == layer, i, 0), 0))
    spec = pl.BlockSpec((tr, C), lambda l, i: (l * n_r + i, 0))
    return pl.pallas_call(
        body, name=name, grid=(n_l, n_r),
        in_specs=[slot_spec(layer) for layer in range(n_l)] + [spec, spec, spec],
        out_specs=[spec] * 4,
        out_shape=[jax.ShapeDtypeStruct((n_l * R, C), F32)] * 4,
        compiler_params=_cparams(("arbitrary", "arbitrary")),
    )(*slots, w, m, v)


def _small_allreduce_adamw(g, w, m, v, *, name):
    R = g.shape[0]
    vmem = pl.BlockSpec(memory_space=pltpu.VMEM)

    def body(g_ref, w_ref, m_ref, v_ref, g_out, d_out, m_out, v_out, slots, send, recv):
        me = _my_place()
        my_id = _logical(me)
        slots[my_id] = g_ref[...]
        copies = []
        for k in range(1, N_DEV):
            cp = pltpu.make_async_remote_copy(
                src_ref=g_ref, dst_ref=slots.at[my_id], send_sem=send.at[k], recv_sem=recv.at[k],
                device_id=_peer(me, k), device_id_type=MESH)
            cp.start()
            copies.append(cp)
        for cp in copies:
            cp.wait()
        total = slots[0]
        for s in range(1, N_DEV):
            total = total + slots[s]
        delta, m_new, v_new = _adamw(w_ref[...], total, m_ref[...], v_ref[...])
        g_out[...] = total
        d_out[...] = delta
        m_out[...] = m_new
        v_out[...] = v_new

    return pl.pallas_call(
        body, name=name,
        in_specs=[vmem] * 4, out_specs=[vmem] * 4,
        out_shape=[jax.ShapeDtypeStruct((R, LANES), F32)] * 4,
        scratch_shapes=[pltpu.VMEM((N_DEV, R, LANES), F32),
                        pltpu.SemaphoreType.DMA((N_DEV,)), pltpu.SemaphoreType.DMA((N_DEV,))],
    )(g, w, m, v)


def _pack(parts):
    flat = jnp.concatenate([p.reshape(-1) for p in parts])
    rows = -(-flat.shape[0] // (8 * LANES)) * 8
    return jnp.pad(flat, (0, rows * LANES - flat.shape[0])).reshape(rows, LANES)


def _unpack(packed, like):
    flat = packed.reshape(-1)
    out, pos = [], 0
    for p in like:
        out.append(flat[pos:pos + p.size].reshape(p.shape))
        pos += p.size
    return out


def kernel(x, w_in, b_gate, rel_bias, w_proj_a, w_proj_b, w_out, ln1_g, ln1_b, w_ffn_in, w_ffn_out, ln2_g, ln2_b, loss_target, m_w_in, m_b_gate, m_rel_bias, m_w_proj_a, m_w_proj_b, m_w_out, m_ln1_g, m_ln1_b, m_w_ffn_in, m_w_ffn_out, m_ln2_g, m_ln2_b, v_w_in, v_b_gate, v_rel_bias, v_w_proj_a, v_w_proj_b, v_w_out, v_ln1_g, v_ln1_b, v_w_ffn_in, v_w_ffn_out, v_ln2_g, v_ln2_b):
    L = w_in.shape[0]
    T, D = x.shape[1], x.shape[2]
    alpha = float((2 * L) ** 0.25)
    n_qkv = 6 * WIDTH

    big = [w_in, w_proj_a, w_proj_b, w_out, w_ffn_in, w_ffn_out]
    kinds = ["in", "pa", "pb", "o", "fi", "fo"]
    modes = ["col", "col", "col", "row", "blk", "row"]
    mode_of = dict(zip(kinds, modes))
    w_bf = dict(zip(kinds, [w.astype(BF16) for w in big]))

    def gather_of(ks, l):
        return _gather_job([w_bf[k][l:l + 1] for k in ks], [mode_of[k] for k in ks])

    W = [dict() for _ in range(L)]
    W[0]["in"] = _gather_via_sibling(w_bf["in"][:1], mode_of["in"], name="gather_w_in_first")
    vec3 = lambda a: a[:, None, :]
    bg3, l1g, l1b, l2g, l2b = vec3(b_gate), vec3(ln1_g), vec3(ln1_b), vec3(ln2_g), vec3(ln2_b)
    b_col0 = 3 * WIDTH // LANES

    h = x[0]
    saved = []
    for l in range(L):
        ahead = l + 1 < L
        soon = ["pa", "pb", "o", "fo"]
        (qkv, gates), got = _in_proj(h, W[l]["in"], 0, n_qkv=n_qkv, name=f"in_proj_{l}",
                                     job=gather_of(soon, 0) if l == 0 else None)
        W[l].update(zip(soon, got))
        kvpad = jnp.pad(qkv[:, WIDTH:3 * WIDTH], ((A_WIN - A_TQ, 0), (0, 0)))
        bias = _toeplitz_bias(rel_bias[l])
        oa, got = _attn_a_fwd(qkv, kvpad, bias, name=f"attn_a_fwd_{l}", job=gather_of(["fi"], 0) if l == 0 else None)
        W[l].update(zip(["fi"], got))
        early = ["in", "pa", "pb", "o"]
        ob, got = _attn_b_fwd(qkv, col0=b_col0, name=f"attn_b_fwd_{l}", job=gather_of(early, l + 1) if ahead else None)
        W[l + 1 if ahead else l].update(zip(early, got))
        x1, r1, x1b = _proj_fwd(oa, ob, gates, bg3, W[l]["pa"], W[l]["pb"], W[l]["o"], 0, h, l1g, l1b, l,
                           alpha=alpha, name=f"proj_fwd_{l}")
        (gs, us, r2, x2), got = _ffn_fwd(x1, W[l]["fi"], W[l]["fo"], 0, l2g, l2b, l, alpha=alpha, name=f"ffn_fwd_{l}",
                                         job=gather_of(["fi", "fo"], l + 1) if ahead else None)
        W[l + 1 if ahead else l].update(zip(["fi", "fo"], got))
        saved.append((h, qkv, gates, kvpad, bias, oa, ob, x1b, r1, gs, us, r2))
        h = x2

    d_h, sq = _loss_head(h, loss_target[0], name="loss_head")
    loss = lax.psum((0.5 / D) * jnp.sum(sq), ("x", "y", "c"))

    g_bg, g_rb, g_l1g, g_l1b, g_l2g, g_l2b = ([None] * L for _ in range(6))
    slot = {k: [None] * L for k in kinds}

    def exchange_of(ks, grads):
        return _grads_job([[g] for g in grads], [mode_of[k] for k in ks])

    w_in_above = None
    for l in reversed(range(L)):
        xin, qkv, gates, kvpad, bias, oa, ob, x1b, r1, gs, us, r2 = saved[l]
        (dr2, act, dgt, dup, dx1, g_l2g[l], g_l2b[l]), got = _ffn_bwd(
            d_h, r2, l2g, gs, us, W[l]["fi"], W[l]["fo"], 0, l, alpha=alpha, name=f"ffn_bwd_{l}", job=w_in_above)
        if w_in_above is not None:
            (slot["in"][l + 1],) = got
        g_fo = _mm_tn_blocked_a(act, dr2, name=f"grad_w_ffn_out_{l}").reshape(-1, D)
        g_fi = jnp.concatenate(_mm_tn_blocked_pair(x1b, dgt, dup, name=f"grad_w_ffn_in_{l}"), axis=0)
        (dr1, mixin, dya, dyb, dgates, doa, dob, g_l1g[l], g_l1b[l], g_bg[l]) = _proj_bwd(
            dx1, r1, l1g, oa, ob, gates, bg3, W[l]["pa"], W[l]["pb"], W[l]["o"], 0, l, name=f"proj_bwd_{l}")
        g_o = _mm_tn(mixin, dr1, tm=_tile(D, 1024), tn=_tile(D, 1024), name=f"grad_w_out_{l}")
        g_pa = _mm_tn(oa, dya, tm=WIDTH, tn=_tile(D, 1024), name=f"grad_w_proj_a_{l}")
        g_pb = _mm_tn(ob, dyb, tm=WIDTH, tn=_tile(D, 1024), name=f"grad_w_proj_b_{l}")
        (dqa, dka, dva, dbias), (slot["fi"][l], slot["fo"][l]) = _attn_a_bwd(
            qkv, kvpad, bias, doa, name=f"attn_a_bwd_{l}", job=exchange_of(["fi", "fo"], [g_fi, g_fo]))
        g_rb[l] = _toeplitz_bias_grad(dbias)
        (dqb, dkb, dvb), (slot["pa"][l], slot["pb"][l], slot["o"][l]) = _attn_b_bwd(
            qkv, ob, dob, col0=b_col0, name=f"attn_b_bwd_{l}", job=exchange_of(["pa", "pb", "o"], [g_pa, g_pb, g_o]))
        d_qkv = [dqa, dka, dva, dqb, dkb, dvb]
        g_in = _mm_tn_pieces(xin, d_qkv, dgates, name=f"grad_w_in_{l}")
        w_in_above = exchange_of(["in"], [g_in])
        d_h, got = _mm_nt_pieces_add(d_qkv, dgates, W[l]["in"], 0, dr1, alpha, name=f"grad_x_{l}",
                                     job=w_in_above if l == 0 else None)
        if l == 0:
            (slot["in"][0],) = got
    grad_x = d_h[None]

    moments_m = [m_w_in, m_w_proj_a, m_w_proj_b, m_w_out, m_w_ffn_in, m_w_ffn_out]
    moments_v = [v_w_in, v_w_proj_a, v_w_proj_b, v_w_out, v_w_ffn_in, v_w_ffn_out]
    names = ["w_in", "w_proj_a", "w_proj_b", "w_out", "w_ffn_in", "w_ffn_out"]
    big_out = {}
    for nm, k, w, m, v in zip(names, kinds, big, moments_m, moments_v):
        two = lambda a: a.reshape(-1, a.shape[-1])
        per_layer = [s.reshape(N_DEV, -1, s.shape[-1]) for s in slot[k]]
        res = _sum_slots_adamw(per_layer, two(w), two(m), two(v), name=f"adamw_{nm}")
        big_out[nm] = [r.reshape(w.shape) for r in res]

    small_w = [b_gate, rel_bias, ln1_g, ln1_b, ln2_g, ln2_b]
    small_g = [jnp.stack(g) for g in (g_bg, g_rb, g_l1g, g_l1b, g_l2g, g_l2b)]
    small_m = [m_b_gate, m_rel_bias, m_ln1_g, m_ln1_b, m_ln2_g, m_ln2_b]
    small_v = [v_b_gate, v_rel_bias, v_ln1_g, v_ln1_b, v_ln2_g, v_ln2_b]
    res = _small_allreduce_adamw(_pack(small_g), _pack(small_w), _pack(small_m), _pack(small_v),
                                 name="allreduce_small_adamw")
    small_names = ["b_gate", "rel_bias", "ln1_g", "ln1_b", "ln2_g", "ln2_b"]
    small_out = {nm: [] for nm in small_names}
    for packed in res:
        for nm, arr in zip(small_names, _unpack(packed, small_w)):
            small_out[nm].append(arr)

    order = ["w_in", "b_gate", "rel_bias", "w_proj_a", "w_proj_b", "w_out", "ln1_g", "ln1_b",
             "w_ffn_in", "w_ffn_out", "ln2_g", "ln2_b"]
    every = {**big_out, **small_out}
    outs = [loss, grad_x]
    for kind in range(4):
        outs += [every[nm][kind] for nm in order]
    return tuple(outs)
```

```python
import functools
import math

import jax
import jax.numpy as jnp
import numpy as np
from jax import lax
from jax.experimental import pallas as pl
from jax.experimental.pallas import tpu as pltpu

F32 = jnp.float32
BF16 = jnp.bfloat16

HEAD_DIM = 64
CHUNK = 64
LEFT_CHUNKS = 8
REL_CLIP = 256
N_REL = 2 * REL_CLIP + 1
WIDTH = 512
LANES = 128
A_TQ = 256
A_WIN = A_TQ + LEFT_CHUNKS * CHUNK
A_STRIP = 128
B_TQ = 512
B_TS = 256
B_PIECE = 64
B_DEAD = -160.0
LN_EPS = 1e-5
QK_SCALE = 1.0 / math.sqrt(HEAD_DIM)
LOG2E = 1.4426950408889634
NEG = -1e30

ADAM_LR = 0.001
ADAM_B1 = 0.9
ADAM_B2 = 0.999
ADAM_EPS = 1e-08
ADAM_WD = 0.01
ADAM_STEP = 10

N_DEV = 8
MESH = pl.DeviceIdType.MESH
MIB = 1024 * 1024


def _cparams(sem=None, vmem_mib=48):
    return pltpu.CompilerParams(dimension_semantics=sem, vmem_limit_bytes=vmem_mib * MIB)


def _dot(a, b):
    return jnp.dot(a, b, preferred_element_type=F32)


def _dot_nt(a, b):
    return lax.dot_general(a, b, (((1,), (1,)), ((), ())), preferred_element_type=F32)


def _dot_tn(a, b):
    return lax.dot_general(a, b, (((0,), (0,)), ((), ())), preferred_element_type=F32)


def _tile(n, pref):
    if n <= pref:
        return n
    for t in range(pref - pref % 8, 0, -8):
        if n % t == 0:
            return t
    raise ValueError((n, pref))


def _in_proj(a, w, layer, *, n_qkv, name, job=None):
    M, K = a.shape
    N = w.shape[2]
    tm = _tile(M, 1024)
    tn = 1024
    assert n_qkv % tn == 0 and (N - n_qkv) % tn == 0
    n_q = n_qkv // tn

    def body(a_ref, w_ref, q_ref, g_ref, ab_ref):
        j = pl.program_id(1)

        @pl.when(j == 0)
        def _():
            ab_ref[...] = a_ref[...].astype(BF16)

        res = _dot(ab_ref[...], w_ref[...])

        @pl.when(j < n_q)
        def _():
            q_ref[...] = res.astype(BF16)

        @pl.when(j >= n_q)
        def _():
            g_ref[...] = res

    return _call_carrying(
        job, body, name=name, grid=(M // tm, N // tn),
        in_specs=[pl.BlockSpec((tm, K), lambda i, j: (i, 0)),
                  pl.BlockSpec((None, K, tn), lambda i, j: (layer, 0, j))],
        out_specs=[pl.BlockSpec((tm, tn), lambda i, j: (i, jnp.minimum(j, n_q - 1))),
                   pl.BlockSpec((tm, tn), lambda i, j: (i, jnp.maximum(j - n_q, 0)))],
        out_shape=[jax.ShapeDtypeStruct((M, n_qkv), BF16), jax.ShapeDtypeStruct((M, N - n_qkv), F32)],
        scratch_shapes=[pltpu.VMEM((tm, K), BF16)], vmem_mib=48, args=(a, w))


def _mm_nt_add(a, w, layer, add, add_scale, *, name, job=None):
    M, K = a.shape
    N = w.shape[1]
    tm = _tile(M, 1024)
    tk = _tile(K, 1024)

    def body(a_ref, w_ref, add_ref, o_ref):
        @pl.when(pl.program_id(1) == 0)
        def _():
            o_ref[...] = add_scale * add_ref[...]
        o_ref[...] += _dot_nt(a_ref[...], w_ref[...])

    (out,), rode = _call_carrying(
        job, body, name=name, grid=(M // tm, K // tk),
        in_specs=[pl.BlockSpec((tm, tk), lambda i, k: (i, k)),
                  pl.BlockSpec((None, N, tk), lambda i, k: (layer, 0, k)),
                  pl.BlockSpec((tm, N), lambda i, k: (i, 0))],
        out_specs=[pl.BlockSpec((tm, N), lambda i, k: (i, 0))],
        out_shape=[jax.ShapeDtypeStruct((M, N), F32)],
        scratch_shapes=[], vmem_mib=48, args=(a, w, add))
    return out, rode


def _pieces_layout(pieces, wide):
    P = pieces[0].shape[1]
    assert all(p.shape[1] == P for p in pieces) and wide.shape[1] % P == 0
    n_blocks = len(pieces) + wide.shape[1] // P
    assert n_blocks % 2 == 0 and len(pieces) > n_blocks // 2
    return P, n_blocks // 2


def _mm_tn_pieces(a, pieces, wide, *, name):
    T, M = a.shape
    P, n_half = _pieces_layout(pieces, wide)
    n_first = n_half
    n_rest = len(pieces) - n_first
    tk = _tile(T, 1024)
    n_k = T // tk

    def body(*refs):
        a_ref, p_refs, wide_ref = refs[0], refs[1:1 + len(pieces)], refs[1 + len(pieces)]
        o_ref, acc_ref = refs[2 + len(pieces):]
        j, k = pl.program_id(0), pl.program_id(1)

        @pl.when(k == 0)
        def _():
            acc_ref[...] = jnp.zeros_like(acc_ref)

        x = a_ref[...].astype(BF16)

        @pl.when(j == 0)
        def _():
            for p in range(n_first):
                acc_ref[:, p * P:(p + 1) * P] += _dot_tn(x, p_refs[p][...])

        @pl.when(j == 1)
        def _():
            for p in range(n_rest):
                acc_ref[:, p * P:(p + 1) * P] += _dot_tn(x, p_refs[n_first + p][...])
            acc_ref[:, n_rest * P:] += _dot_tn(x, wide_ref[...])

        @pl.when(k == n_k - 1)
        def _():
            o_ref[...] = acc_ref[...].astype(o_ref.dtype)

    first = lambda: pl.BlockSpec((tk, P), lambda j, k: (jnp.where(j == 0, k, n_k - 1), 0))
    rest = lambda: pl.BlockSpec((tk, P), lambda j, k: (jnp.where(j == 1, k, 0), 0))
    return pl.pallas_call(
        body, name=name, grid=(2, n_k),
        in_specs=[pl.BlockSpec((tk, M), lambda j, k: (k, 0))] + [first() for _ in range(n_first)]
                 + [rest() for _ in range(n_rest)]
                 + [pl.BlockSpec((tk, wide.shape[1]), lambda j, k: (jnp.where(j == 1, k, 0), 0))],
        out_specs=pl.BlockSpec((M, n_half * P), lambda j, k: (0, j)),
        out_shape=jax.ShapeDtypeStruct((M, 2 * n_half * P), BF16),
        scratch_shapes=[pltpu.VMEM((M, n_half * P), F32)],
        compiler_params=_cparams(("arbitrary", "arbitrary"), 56),
    )(a, *pieces, wide)


def _mm_nt_pieces_add(pieces, wide, w, layer, add, add_scale, *, name, job=None):
    M = add.shape[0]
    N = w.shape[1]
    P, _ = _pieces_layout(pieces, wide)
    tm = _tile(M, 1024)
    tk = 2 * P
    n_pair = len(pieces) // 2
    assert len(pieces) % 2 == 0 and wide.shape[1] % tk == 0
    n_k = n_pair + wide.shape[1] // tk

    def body(*refs):
        p_refs, wide_ref, w_ref, add_ref, o_ref = refs[:len(pieces)], *refs[len(pieces):]
        k = pl.program_id(1)

        @pl.when(k == 0)
        def _():
            o_ref[...] = add_scale * add_ref[...]

        for pair in range(n_pair):
            @pl.when(k == pair)
            def _(pair=pair):
                o_ref[...] += (_dot_nt(p_refs[2 * pair][...], w_ref[:, :P])
                               + _dot_nt(p_refs[2 * pair + 1][...], w_ref[:, P:]))

        @pl.when(k >= n_pair)
        def _():
            o_ref[...] += _dot_nt(wide_ref[...], w_ref[...])

    (out,), rode = _call_carrying(
        job, body, name=name, grid=(M // tm, n_k),
        in_specs=[pl.BlockSpec((tm, P), lambda i, k, pair=n // 2: (jnp.where(k >= pair, i, jnp.maximum(i - 1, 0)), 0))
                  for n in range(len(pieces))]
                 + [pl.BlockSpec((tm, tk), lambda i, k: (i, jnp.maximum(k - n_pair, 0))),
                    pl.BlockSpec((None, N, tk), lambda i, k: (layer, 0, k)),
                    pl.BlockSpec((tm, N), lambda i, k: (i, 0))],
        out_specs=[pl.BlockSpec((tm, N), lambda i, k: (i, 0))],
        out_shape=[jax.ShapeDtypeStruct((M, N), F32)],
        scratch_shapes=[], vmem_mib=56, args=(*pieces, wide, w, add))
    return out, rode


def _tn_body(k_axis, n_k):
    def body(a_ref, b_ref, o_ref, acc_ref):
        k = pl.program_id(k_axis)

        @pl.when(k == 0)
        def _():
            acc_ref[...] = jnp.zeros_like(acc_ref)
        acc_ref[...] += _dot_tn(a_ref[...].astype(BF16), b_ref[...].astype(BF16))

        @pl.when(k == n_k - 1)
        def _():
            o_ref[...] = acc_ref[...].astype(o_ref.dtype)
    return body


def _mm_tn(a, b, *, tm, tn, name):
    T, M = a.shape
    N = b.shape[1]
    tk = _tile(T, 1024)
    return pl.pallas_call(
        _tn_body(2, T // tk), name=name, grid=(M // tm, N // tn, T // tk),
        in_specs=[pl.BlockSpec((tk, tm), lambda i, j, k: (k, i)),
                  pl.BlockSpec((tk, tn), lambda i, j, k: (k, j))],
        out_specs=pl.BlockSpec((tm, tn), lambda i, j, k: (i, j)),
        out_shape=jax.ShapeDtypeStruct((M, N), BF16),
        scratch_shapes=[pltpu.VMEM((tm, tn), F32)],
        compiler_params=_cparams(("parallel", "parallel", "arbitrary")),
    )(a, b)


def _mm_tn_blocked_pair(a, b1, b2, *, name):
    T, M = a.shape
    S, _, N = b1.shape
    tk = _tile(T, 1024)
    n_k = T // tk

    def body(a_ref, b1_ref, b2_ref, o1_ref, o2_ref, acc1_ref, acc2_ref):
        k = pl.program_id(1)

        @pl.when(k == 0)
        def _():
            acc1_ref[...] = jnp.zeros_like(acc1_ref)
            acc2_ref[...] = jnp.zeros_like(acc2_ref)

        a_t = a_ref[...].astype(BF16)
        acc1_ref[...] += _dot_tn(a_t, b1_ref[...])
        acc2_ref[...] += _dot_tn(a_t, b2_ref[...])

        @pl.when(k == n_k - 1)
        def _():
            o1_ref[...] = acc1_ref[...].astype(o1_ref.dtype)
            o2_ref[...] = acc2_ref[...].astype(o2_ref.dtype)

    blk = lambda: pl.BlockSpec((None, tk, N), lambda s, k: (s, k, 0))
    out = lambda: pl.BlockSpec((None, M, N), lambda s, k: (s, 0, 0))
    return pl.pallas_call(
        body, name=name, grid=(S, n_k),
        in_specs=[pl.BlockSpec((tk, M), lambda s, k: (k, 0)), blk(), blk()],
        out_specs=[out(), out()],
        out_shape=[jax.ShapeDtypeStruct((S, M, N), BF16)] * 2,
        scratch_shapes=[pltpu.VMEM((M, N), F32), pltpu.VMEM((M, N), F32)],
        compiler_params=_cparams(("parallel", "arbitrary")),
    )(a, b1, b2)


def _mm_tn_blocked_a(a, b, *, name):
    S, T, M = a.shape
    N = b.shape[1]
    tk = _tile(T, 1024)
    return pl.pallas_call(
        _tn_body(1, T // tk), name=name, grid=(S, T // tk),
        in_specs=[pl.BlockSpec((None, tk, M), lambda s, k: (s, k, 0)),
                  pl.BlockSpec((tk, N), lambda s, k: (k, 0))],
        out_specs=pl.BlockSpec((None, M, N), lambda s, k: (s, 0, 0)),
        out_shape=jax.ShapeDtypeStruct((S, M, N), BF16),
        scratch_shapes=[pltpu.VMEM((M, N), F32)],
        compiler_params=_cparams(("parallel", "arbitrary")),
    )(a, b)


def _ln_fwd(r, g, b):
    mu = jnp.mean(r, axis=-1, keepdims=True)
    xc = r - mu
    var = jnp.mean(xc * xc, axis=-1, keepdims=True)
    return xc * lax.rsqrt(var + LN_EPS) * g + b


def _ln_bwd(dy, r, g):
    mu = jnp.mean(r, axis=-1, keepdims=True)
    xc = r - mu
    var = jnp.mean(xc * xc, axis=-1, keepdims=True)
    rstd = lax.rsqrt(var + LN_EPS)
    xhat = xc * rstd
    dxh = dy * g
    m1 = jnp.mean(dxh, axis=-1, keepdims=True)
    m2 = jnp.mean(dxh * xhat, axis=-1, keepdims=True)
    return rstd * (dxh - m1 - xhat * m2), xhat


def _lane_is_head0():
    return lax.broadcasted_iota(jnp.int32, (1, LANES), 1) < HEAD_DIM


def _band_shape():
    a = np.arange(A_TQ)[:, None] // CHUNK
    b = np.arange(A_WIN)[None, :] // CHUNK
    return (b >= a) & (b <= a + LEFT_CHUNKS)


def _band_streams(strip):
    return [dict(h=h, n=strip, rows=pl.ds(r0, strip)) for h in range(2) for r0 in range(0, A_TQ, strip)]


def _band_scores(st, i, qh_ref, k2, bias_ref):
    st["s"] = _dot_nt(qh_ref[st["h"], st["rows"], :], k2) + bias_ref[st["h"], st["rows"], :]


_BIAS_SHIFTS = tuple(range(A_WIN - A_TQ, -1, -A_TQ))


def _bias_variants(bias):
    H = bias.shape[0]
    return jnp.stack([jnp.concatenate([bias[:, :, s:], jnp.full((H, A_TQ, s), NEG, F32)], axis=2)
                      for s in _BIAS_SHIFTS])


def _bias_spec():
    last = len(_BIAS_SHIFTS) - 1
    return pl.BlockSpec((None, 2, A_TQ, A_WIN), lambda hp, i: (jnp.minimum(i, last), hp, 0, 0))


def _band_softmax(st):
    s = st.pop("s")
    e = jnp.exp(s - jnp.max(s, axis=1, keepdims=True))
    st["p"] = e * (1.0 / jnp.sum(e, axis=1, keepdims=True))


def _attn_a_fwd(qkv, bias, *, name, job=None):
    T = qkv.shape[0]
    n_hp = WIDTH // LANES

    def body(q_ref, k_ref, v_ref, bias_ref, o_ref, qh_ref, acc_ref):
        i = pl.program_id(1)
        row0 = pl.multiple_of(jnp.maximum(i * A_TQ - (A_WIN - A_TQ), 0), A_TQ)
        q2 = q_ref[...] * jnp.asarray(QK_SCALE, BF16)
        k2 = k_ref[pl.ds(row0, A_WIN), :]
        v2 = v_ref[pl.ds(row0, A_WIN), :]
        head0 = _lane_is_head0()
        qh_ref[0] = jnp.where(head0, q2, jnp.zeros_like(q2))
        qh_ref[1] = jnp.where(head0, jnp.zeros_like(q2), q2)

        def scores(st):
            _band_scores(st, i, qh_ref, k2, bias_ref)

        def values(st):
            acc_ref[st["h"], st["rows"], :] = _dot(st.pop("p").astype(BF16), v2)

        _skewed(_band_streams(A_STRIP), [scores, _band_softmax, values])
        o_ref[...] = jnp.where(head0, acc_ref[0], acc_ref[1]).astype(o_ref.dtype)

    (out,), rode = _call_carrying(
        job, body, name=name, grid=(n_hp, T // A_TQ),
        in_specs=[pl.BlockSpec((A_TQ, LANES), lambda hp, i: (i, hp)),
                  pl.BlockSpec((T, LANES), lambda hp, i: (0, hp + n_hp)),
                  pl.BlockSpec((T, LANES), lambda hp, i: (0, hp + 2 * n_hp)),
                  _bias_spec()],
        out_specs=[pl.BlockSpec((A_TQ, LANES), lambda hp, i: (i, hp))],
        out_shape=[jax.ShapeDtypeStruct((T, WIDTH), BF16)],
        scratch_shapes=[pltpu.VMEM((2, A_TQ, LANES), BF16), pltpu.VMEM((2, A_TQ, LANES), F32)],
        vmem_mib=48, args=(qkv, qkv, qkv, bias))
    return out, rode


def _attn_a_bwd(qkv, bias, do, *, name, job=None):
    T = qkv.shape[0]
    TP = T
    n_hp = WIDTH // LANES

    def body(q_ref, k_ref, v_ref, bias_ref, do_ref, dq_ref, dko_ref, dvo_ref, db_ref,
             qh_ref, doh_ref, dqa_ref, dk_ref, dv_ref, qs_ref):
        i = pl.program_id(1)

        @pl.when(i == 0)
        def _():
            dk_ref[...] = jnp.zeros_like(dk_ref)
            dv_ref[...] = jnp.zeros_like(dv_ref)
            db_ref[...] = jnp.zeros_like(db_ref)

        row0 = pl.multiple_of(jnp.maximum(i * A_TQ - (A_WIN - A_TQ), 0), A_TQ)
        window = pl.ds(row0, A_WIN)
        scale = jnp.asarray(QK_SCALE, BF16)
        q2 = q_ref[...] * scale
        do2 = do_ref[...]
        k2 = k_ref[window, :]
        k2s = k2 * scale
        v2 = v_ref[window, :]
        head0 = _lane_is_head0()
        zero = jnp.zeros_like(q2)
        qs_ref[...] = q2
        qh_ref[0] = jnp.where(head0, q2, zero)
        qh_ref[1] = jnp.where(head0, zero, q2)
        doh_ref[0] = jnp.where(head0, do2, zero)
        doh_ref[1] = jnp.where(head0, zero, do2)
        dk = [[], []]
        dv = [[], []]

        def scores(st):
            _band_scores(st, i, qh_ref, k2, bias_ref)
            st["dp"] = _dot_nt(doh_ref[st["h"], st["rows"], :], v2)

        def dscores(st):
            _band_softmax(st)
            p, dp = st.pop("p"), st.pop("dp")
            ds = p * (dp - jnp.sum(p * dp, axis=1, keepdims=True))
            for var, shift in enumerate(_BIAS_SHIFTS):
                @pl.when(jnp.minimum(i, len(_BIAS_SHIFTS) - 1) == var)
                def _(shift=shift):
                    db_ref[st["h"], st["rows"], shift:] += ds[:, :A_WIN - shift]
            st["dsb"] = ds.astype(BF16)
            st["pb"] = p.astype(BF16)

        def grads(st):
            h, rows = st["h"], st["rows"]
            dsb = st.pop("dsb")
            dqa_ref[h, rows, :] = _dot(dsb, k2s)
            dk[h].append(_dot_tn(dsb, qs_ref[rows, :]))
            dv[h].append(_dot_tn(st.pop("pb"), do_ref[rows, :]))

        _skewed(_band_streams(A_TQ), [scores, dscores, grads])
        dq_ref[...] = jnp.where(head0, dqa_ref[0], dqa_ref[1]).astype(dq_ref.dtype)
        dk_ref[window, :] += jnp.where(head0, sum(dk[0]), sum(dk[1]))
        dv_ref[window, :] += jnp.where(head0, sum(dv[0]), sum(dv[1]))

        @pl.when(i == T // A_TQ - 1)
        def _():
            dko_ref[...] = dk_ref[...].astype(BF16)
            dvo_ref[...] = dv_ref[...].astype(BF16)

    return _call_carrying(
        job, body, name=name, grid=(n_hp, T // A_TQ),
        in_specs=[pl.BlockSpec((A_TQ, LANES), lambda hp, i: (i, hp)),
                  pl.BlockSpec((T, LANES), lambda hp, i: (0, hp + n_hp)),
                  pl.BlockSpec((T, LANES), lambda hp, i: (0, hp + 2 * n_hp)),
                  _bias_spec(),
                  pl.BlockSpec((A_TQ, LANES), lambda hp, i: (i, hp))],
        out_specs=[pl.BlockSpec((A_TQ, LANES), lambda hp, i: (i, hp)),
                   pl.BlockSpec((T, LANES), lambda hp, i: (0, hp)),
                   pl.BlockSpec((T, LANES), lambda hp, i: (0, hp)),
                   pl.BlockSpec((2, A_TQ, A_WIN), lambda hp, i: (hp, 0, 0))],
        out_shape=[jax.ShapeDtypeStruct((T, WIDTH), BF16),
                   jax.ShapeDtypeStruct((T, WIDTH), BF16),
                   jax.ShapeDtypeStruct((T, WIDTH), BF16),
                   jax.ShapeDtypeStruct((WIDTH // HEAD_DIM, A_TQ, A_WIN), F32)],
        scratch_shapes=[pltpu.VMEM((2, A_TQ, LANES), BF16), pltpu.VMEM((2, A_TQ, LANES), BF16),
                        pltpu.VMEM((2, A_TQ, LANES), F32),
                        pltpu.VMEM((TP, LANES), F32), pltpu.VMEM((TP, LANES), F32),
                        pltpu.VMEM((A_TQ, LANES), BF16)],
        vmem_mib=56, args=(qkv, qkv, qkv, bias, do))


_N_DIAG = 2 * CHUNK - 1
_EXT_TOP = LEFT_CHUNKS * CHUNK + CHUNK - 1 + REL_CLIP


def _toeplitz_bias(rb):
    H = rb.shape[0]
    ext = jnp.concatenate([rb, jnp.broadcast_to(rb[:, N_REL - 1:], (H, _EXT_TOP + 1 - N_REL))], axis=1)
    vec = jnp.stack([ext[:, _EXT_TOP - (_N_DIAG - 1) - CHUNK * k:_EXT_TOP - CHUNK * k + 1]
                     for k in range(LEFT_CHUNKS + 1)], axis=1)
    rev = jnp.pad(vec[:, :, ::-1], ((0, 0), (0, 0), (0, 1)))
    flat = jnp.broadcast_to(rev[:, :, None, :], (H, LEFT_CHUNKS + 1, CHUNK, _N_DIAG + 1))
    skew = flat.reshape(H, LEFT_CHUNKS + 1, -1)[:, :, :CHUNK * _N_DIAG].reshape(H, LEFT_CHUNKS + 1, CHUNK, _N_DIAG)
    blocks = skew[:, :, :, CHUNK - 1:]
    neg = jnp.full((H, CHUNK, CHUNK), NEG, F32)
    rows = [jnp.concatenate([blocks[:, b - a] if 0 <= b - a <= LEFT_CHUNKS else neg for b in range(A_WIN // CHUNK)],
                            axis=2) for a in range(A_TQ // CHUNK)]
    return jnp.concatenate(rows, axis=1)


def _toeplitz_bias_grad(db):
    H = db.shape[0]
    d5 = db.reshape(H, A_TQ // CHUNK, CHUNK, A_WIN // CHUNK, CHUNK)
    g_blocks = jnp.stack([sum(d5[:, a, :, a + k, :] for a in range(A_TQ // CHUNK))
                          for k in range(LEFT_CHUNKS + 1)], axis=1)
    d_skew = jnp.pad(g_blocks, ((0, 0), (0, 0), (0, 0), (CHUNK - 1, 0)))
    d_flat = jnp.pad(d_skew.reshape(H, LEFT_CHUNKS + 1, CHUNK * _N_DIAG), ((0, 0), (0, 0), (0, CHUNK)))
    g_vec = jnp.sum(d_flat.reshape(H, LEFT_CHUNKS + 1, CHUNK, _N_DIAG + 1), axis=2)[:, :, :_N_DIAG][:, :, ::-1]
    g_ext = sum(jnp.pad(g_vec[:, k], ((0, 0), (_EXT_TOP - (_N_DIAG - 1) - CHUNK * k, CHUNK * k)))
                for k in range(LEFT_CHUNKS + 1))
    return jnp.concatenate([g_ext[:, :N_REL - 1], jnp.sum(g_ext[:, N_REL - 1:], axis=1, keepdims=True)], axis=1)


def _split_bf16(x):
    hi = x.astype(BF16)
    lo = (x - hi.astype(F32)).astype(BF16)
    return hi, lo


def _sb_streams(d, strips=None, **tile):
    out = []
    for h in range(2):
        for r in (range(B_TQ // B_TS) if strips is None else strips):
            if d is not None and d > r:
                continue
            out.append(dict(h=h, r=r, rows=pl.ds(r * B_TS, B_TS), diag=(d is not None and d == r), **tile))
    return out


def _sb_sweep(i, car_ref, streams_of, run):
    sub = B_TQ // B_TS
    run([st for d in reversed(range(sub)) for st in streams_of(i * sub + d, d, None)])

    def alive(c):
        return (c[0] < i * sub) & (c[1] > B_DEAD)

    def step(c):
        kb = i * sub - 1 - c[0]
        if sub > 1:
            lower_alive = jnp.max(car_ref[:, B_TS:, :]) > B_DEAD
            lax.cond(lower_alive, lambda: run(streams_of(kb, None, None)), lambda: run(streams_of(kb, None, [0])))
        else:
            run(streams_of(kb, None, None))
        return c[0] + 1, jnp.max(car_ref[...])

    lax.while_loop(alive, step, (jnp.int32(0), jnp.float32(0.0)))


def _piece_rows(st, p):
    return pl.ds(st["r"] * B_TS + p, B_PIECE)


def _rows_cat(parts):
    return jnp.concatenate(parts, axis=0)


def _skewed(streams, stages):
    for t in range(len(streams) + len(stages) - 1):
        for s, st in enumerate(streams):
            if 0 <= t - s < len(stages):
                stages[t - s](st)


def _sb_logs(st, z2):
    log_beta, log_keep, keep_bf = [], [], []
    for p in range(0, B_TS, B_PIECE):
        z = z2[p:p + B_PIECE]
        lp2 = jnp.log(1.0 + jnp.exp2(-jnp.abs(z))) * LOG2E
        lb = jnp.minimum(z, 0.0) - lp2
        lk = lb - z
        if st["diag"]:
            lk = jnp.where(_strict_lower(p), lk, 0.0)
        log_beta.append(lb)
        log_keep.append(lk)
        keep_bf.append(lk.astype(BF16))
    st["log_beta"] = _rows_cat(log_beta)
    st["log_keep"] = _rows_cat(log_keep)
    st["keep_bf"] = _rows_cat(keep_bf)


def _strict_lower(p):
    t = p + lax.broadcasted_iota(jnp.int32, (B_PIECE, B_TS), 0)
    s = lax.broadcasted_iota(jnp.int32, (B_PIECE, B_TS), 1)
    return s < t


def _tri(strict):
    j = lax.broadcasted_iota(jnp.int32, (B_TS, B_TS), 0)
    s = lax.broadcasted_iota(jnp.int32, (B_TS, B_TS), 1)
    return jnp.where(j > s if strict else j >= s, 1.0, 0.0).astype(BF16)


def _call_carrying(job, body, *, name, grid, in_specs, out_specs, out_shape, scratch_shapes, vmem_mib, args):
    n_in, n_out, n_scr = len(in_specs), len(out_specs), len(scratch_shapes)
    if job is None:
        res = pl.pallas_call(body, name=name, grid=grid, in_specs=in_specs, out_specs=out_specs,
                             out_shape=out_shape, scratch_shapes=scratch_shapes,
                             compiler_params=_cparams(("arbitrary",) * len(grid), vmem_mib))(*args)
        return res, []
    j_in, j_out = len(job.arrays), len(job.out_shape)
    hbm = pl.BlockSpec(memory_space=pltpu.HBM)

    def carrying(*refs):
        refs = list(refs)
        ins, refs = refs[:n_in], refs[n_in:]
        j_ins, refs = refs[:j_in], refs[j_in:]
        outs, refs = refs[:n_out], refs[n_out:]
        j_outs, refs = refs[:j_out], refs[j_out:]
        scr, sems = refs[:n_scr], refs[n_scr:]
        first = functools.reduce(jnp.logical_and, [pl.program_id(d) == 0 for d in range(len(grid))])
        last = functools.reduce(jnp.logical_and, [pl.program_id(d) == grid[d] - 1 for d in range(len(grid))])

        @pl.when(first)
        def _():
            job.start(j_ins, j_outs, sems)

        body(*ins, *outs, *scr)

        @pl.when(last)
        def _():
            job.wait(j_ins, j_outs, sems)

    res = pl.pallas_call(
        carrying, name=name, grid=grid,
        in_specs=list(in_specs) + [hbm] * j_in, out_specs=list(out_specs) + [hbm] * j_out,
        out_shape=list(out_shape) + job.out_shape, scratch_shapes=list(scratch_shapes) + job.scratch(),
        compiler_params=_cparams(("arbitrary",) * len(grid), vmem_mib))(*args, *job.arrays)
    return res[:n_out], res[n_out:]


def _attn_b_fwd(qkv, *, col0, name, job=None):
    T = qkv.shape[0]
    n_hp = WIDTH // LANES
    sub = B_TQ // B_TS

    def body(q_ref, k_ref, v_ref, o_ref, acc_ref, car_ref, qh_ref):
        i = pl.program_id(1)
        q2 = q_ref[...]
        head0 = _lane_is_head0()
        qh_ref[0] = jnp.where(head0, q2, jnp.zeros_like(q2))
        qh_ref[1] = jnp.where(head0, jnp.zeros_like(q2), q2)
        tri_s = _tri(True)
        acc_ref[...] = jnp.zeros_like(acc_ref)
        car_ref[...] = jnp.zeros_like(car_ref)

        def streams_of(kb, d, strips):
            keys = pl.ds(pl.multiple_of(kb * B_TS, B_TS), B_TS)
            return _sb_streams(d, strips, k2=k_ref[keys, :], v2=v_ref[keys, :])

        def scores(st):
            st["z2"] = _dot_nt(qh_ref[st["h"], st["rows"], :], st.pop("k2")) * (QK_SCALE * LOG2E)

        def logs(st):
            _sb_logs(st, st.pop("z2"))

        def suffix(st):
            st["suffix"] = _dot(st.pop("keep_bf"), tri_s)

        def weights(st):
            log_beta, suffix, log_keep = st.pop("log_beta"), st.pop("suffix"), st.pop("log_keep")
            wb = []
            for p in range(0, B_TS, B_PIECE):
                rows = _piece_rows(st, p)
                car = car_ref[st["h"], rows, :]
                w = jnp.exp2(log_beta[p:p + B_PIECE] + suffix[p:p + B_PIECE] + car)
                if st["diag"]:
                    w = jnp.where(_strict_lower(p), w, 0.0)
                wb.append(w.astype(BF16))
                car_ref[st["h"], rows, :] = car + jnp.sum(log_keep[p:p + B_PIECE], axis=1, keepdims=True)
            st["wb"] = _rows_cat(wb)

        def values(st):
            acc_ref[st["h"], st["rows"], :] += _dot(st.pop("wb"), st.pop("v2"))

        _sb_sweep(i, car_ref, streams_of, lambda sts: _skewed(sts, [scores, logs, suffix, weights, values]))
        o_ref[...] = jnp.where(head0, acc_ref[0], acc_ref[1])

    (out,), rode = _call_carrying(
        job, body, name=name, grid=(n_hp, T // B_TQ),
        in_specs=[pl.BlockSpec((B_TQ, LANES), lambda hp, i: (i, hp + col0)),
                  pl.BlockSpec((T, LANES), lambda hp, i: (0, hp + col0 + n_hp)),
                  pl.BlockSpec((T, LANES), lambda hp, i: (0, hp + col0 + 2 * n_hp))],
        out_specs=[pl.BlockSpec((B_TQ, LANES), lambda hp, i: (i, hp))],
        out_shape=[jax.ShapeDtypeStruct((T, WIDTH), F32)],
        scratch_shapes=[pltpu.VMEM((2, B_TQ, LANES), F32), pltpu.VMEM((2, B_TQ, 1), F32),
                        pltpu.VMEM((2, B_TQ, LANES), BF16)],
        vmem_mib=48, args=(qkv, qkv, qkv))
    return out, rode


def _attn_b_bwd(qkv, out, do, *, col0, name, job=None):
    T = qkv.shape[0]
    n_hp = WIDTH // LANES
    sub = B_TQ // B_TS

    def body(q_ref, k_ref, v_ref, o_ref, do_ref, dq_ref, dko_ref, dvo_ref,
             dqa_ref, car_ref, carr_ref, tot_ref, qh_ref, doh_ref, qs_ref, dk_ref, dv_ref):
        i = pl.program_id(1)

        @pl.when(i == 0)
        def _():
            dk_ref[...] = jnp.zeros_like(dk_ref)
            dv_ref[...] = jnp.zeros_like(dv_ref)

        q2 = q_ref[...]
        do2 = do_ref[...]
        head0 = _lane_is_head0()
        zero = jnp.zeros_like(q2)
        qh_ref[0] = jnp.where(head0, q2, zero)
        qh_ref[1] = jnp.where(head0, zero, q2)
        doh_ref[0] = jnp.where(head0, do2, zero)
        doh_ref[1] = jnp.where(head0, zero, do2)
        scale = jnp.asarray(QK_SCALE, BF16)
        qs_ref[...] = q2 * scale
        tri_s = _tri(True)
        tri_i = _tri(False)
        prod = do2.astype(F32) * o_ref[...]
        tot_ref[0] = jnp.sum(jnp.where(head0, prod, 0.0), axis=1, keepdims=True)
        tot_ref[1] = jnp.sum(jnp.where(head0, 0.0, prod), axis=1, keepdims=True)
        dqa_ref[...] = jnp.zeros_like(dqa_ref)
        car_ref[...] = jnp.zeros_like(car_ref)
        carr_ref[...] = jnp.zeros_like(carr_ref)

        def streams_of(kb, d, strips):
            keys = pl.ds(pl.multiple_of(kb * B_TS, B_TS), B_TS)
            k2 = k_ref[keys, :]
            return _sb_streams(d, strips, keys=keys, k2=k2, v2=v_ref[keys, :], k2s=k2 * scale)

        def run(streams):
            def scores(st):
                st["z2"] = _dot_nt(qh_ref[st["h"], st["rows"], :], st.pop("k2")) * (QK_SCALE * LOG2E)
                st["dw"] = _dot_nt(doh_ref[st["h"], st["rows"], :], st.pop("v2"))

            def logs(st):
                _sb_logs(st, st.pop("z2"))

            def suffix(st):
                st["suffix"] = _dot(st.pop("keep_bf"), tri_s)

            def weights(st):
                h = st["h"]
                suffix, dw = st.pop("suffix"), st.pop("dw")
                wb, dlog, hi, lo = [], [], [], []
                for p in range(0, B_TS, B_PIECE):
                    rows = _piece_rows(st, p)
                    car = car_ref[h, rows, :]
                    w = jnp.exp2(st["log_beta"][p:p + B_PIECE] + suffix[p:p + B_PIECE] + car)
                    if st["diag"]:
                        w = jnp.where(_strict_lower(p), w, 0.0)
                    w = w.astype(BF16)
                    dl = w.astype(F32) * dw[p:p + B_PIECE]
                    dl_hi, dl_lo = _split_bf16(dl)
                    wb.append(w)
                    dlog.append(dl)
                    hi.append(dl_hi)
                    lo.append(dl_lo)
                    car_ref[h, rows, :] = car + jnp.sum(st["log_keep"][p:p + B_PIECE], axis=1, keepdims=True)
                st["wb"], st["dlog"], st["hi"], st["lo"] = _rows_cat(wb), _rows_cat(dlog), _rows_cat(hi), _rows_cat(lo)

            def later(st):
                st["later"] = _dot(st.pop("hi"), tri_i) + _dot(st.pop("lo"), tri_i)

            def dscores(st):
                h = st["h"]
                later, dlog = st.pop("later"), st.pop("dlog")
                log_keep, log_beta = st.pop("log_keep"), st.pop("log_beta")
                dzb = []
                for p in range(0, B_TS, B_PIECE):
                    rows = _piece_rows(st, p)
                    pc = slice(p, p + B_PIECE)
                    carr = carr_ref[h, rows, :]
                    earlier = tot_ref[h, rows, :] - (later[pc] + carr)
                    dz = dlog[pc] * jnp.exp2(log_keep[pc]) - jnp.exp2(log_beta[pc]) * earlier
                    if st["diag"]:
                        dz = jnp.where(_strict_lower(p), dz, 0.0)
                    dzb.append(dz.astype(BF16))
                    carr_ref[h, rows, :] = carr + jnp.sum(dlog[pc], axis=1, keepdims=True)
                st["dzb"] = _rows_cat(dzb)

            def grads(st):
                h, rows, keys = st["h"], st["rows"], st["keys"]
                mine = head0 if h == 0 else jnp.logical_not(head0)
                dzb = st.pop("dzb")
                dqa_ref[h, rows, :] += _dot(dzb, st.pop("k2s"))
                dk_ref[keys, :] += jnp.where(mine, _dot_tn(dzb, qs_ref[rows, :]), 0.0)
                dv_ref[keys, :] += jnp.where(mine, _dot_tn(st.pop("wb"), do_ref[rows, :]), 0.0)

            _skewed(streams, [scores, logs, suffix, weights, later, dscores, grads])

        _sb_sweep(i, car_ref, streams_of, run)
        dq_ref[...] = jnp.where(head0, dqa_ref[0], dqa_ref[1]).astype(dq_ref.dtype)

        @pl.when(i == T // B_TQ - 1)
        def _():
            dko_ref[...] = dk_ref[...].astype(BF16)
            dvo_ref[...] = dv_ref[...].astype(BF16)

    return _call_carrying(
        job, body, name=name, grid=(n_hp, T // B_TQ),
        in_specs=[pl.BlockSpec((B_TQ, LANES), lambda hp, i: (i, hp + col0)),
                  pl.BlockSpec((T, LANES), lambda hp, i: (0, hp + col0 + n_hp)),
                  pl.BlockSpec((T, LANES), lambda hp, i: (0, hp + col0 + 2 * n_hp)),
                  pl.BlockSpec((B_TQ, LANES), lambda hp, i: (i, hp)),
                  pl.BlockSpec((B_TQ, LANES), lambda hp, i: (i, hp))],
        out_specs=[pl.BlockSpec((B_TQ, LANES), lambda hp, i: (i, hp)),
                   pl.BlockSpec((T, LANES), lambda hp, i: (0, hp)),
                   pl.BlockSpec((T, LANES), lambda hp, i: (0, hp))],
        out_shape=[jax.ShapeDtypeStruct((T, WIDTH), BF16),
                   jax.ShapeDtypeStruct((T, WIDTH), BF16),
                   jax.ShapeDtypeStruct((T, WIDTH), BF16)],
        scratch_shapes=[pltpu.VMEM((2, B_TQ, LANES), F32), pltpu.VMEM((2, B_TQ, 1), F32),
                        pltpu.VMEM((2, B_TQ, 1), F32), pltpu.VMEM((2, B_TQ, 1), F32),
                        pltpu.VMEM((2, B_TQ, LANES), BF16), pltpu.VMEM((2, B_TQ, LANES), BF16),
                        pltpu.VMEM((B_TQ, LANES), BF16),
                        pltpu.VMEM((T, LANES), F32), pltpu.VMEM((T, LANES), F32)],
        vmem_mib=56, args=(qkv, qkv, qkv, out, do))


def _gated_mix(oa_ref, ob_ref, g_ref, bg_ref, wpa_ref, wpb_ref, D):
    ya = _dot(oa_ref[...].astype(BF16), wpa_ref[...])
    yb = _dot(ob_ref[...].astype(BF16), wpb_ref[...])
    sa = jax.nn.sigmoid(g_ref[:, :D] + bg_ref[:, :D])
    sb = jax.nn.sigmoid(g_ref[:, D:] + bg_ref[:, D:])
    return ya, yb, sa, sb


def _proj_fwd(oa, ob, g, bg, wpa, wpb, wo, wl, xin, lng, lnb, layer, *, alpha, name):
    T, D = xin.shape
    tm = _tile(T, 512)
    row = lambda i: (i, 0)
    wspec = lambda r, c: pl.BlockSpec((None, r, c), lambda i: (wl, 0, 0))
    vec = lambda c: pl.BlockSpec((None, 1, c), lambda i: (layer, 0, 0))

    def body(oa_ref, ob_ref, g_ref, bg_ref, wpa_ref, wpb_ref, wo_ref, x_ref, lg_ref, lb_ref, x1_ref, r1_ref, x1b_ref):
        ya, yb, sa, sb = _gated_mix(oa_ref, ob_ref, g_ref, bg_ref, wpa_ref, wpb_ref, D)
        mix = _dot((sa * ya + sb * yb).astype(BF16), wo_ref[...])
        r1 = alpha * x_ref[...] + mix
        r1_ref[...] = r1
        x1 = _ln_fwd(r1, lg_ref[...], lb_ref[...])
        x1_ref[...] = x1
        x1b_ref[...] = x1.astype(BF16)

    return pl.pallas_call(
        body, name=name, grid=(T // tm,),
        in_specs=[pl.BlockSpec((tm, WIDTH), row), pl.BlockSpec((tm, WIDTH), row), pl.BlockSpec((tm, 2 * D), row),
                  vec(2 * D), wspec(WIDTH, D), wspec(WIDTH, D), wspec(D, D),
                  pl.BlockSpec((tm, D), row), vec(D), vec(D)],
        out_specs=[pl.BlockSpec((tm, D), row), pl.BlockSpec((tm, D), row), pl.BlockSpec((tm, D), row)],
        out_shape=[jax.ShapeDtypeStruct((T, D), F32), jax.ShapeDtypeStruct((T, D), F32),
                   jax.ShapeDtypeStruct((T, D), BF16)],
        compiler_params=_cparams(("arbitrary",), 56),
    )(oa, ob, g, bg, wpa, wpb, wo, xin, lng, lnb)


def _proj_bwd(dx1, r1, lng, oa, ob, g, bg, wpa, wpb, wo, wl, layer, *, name):
    T, D = dx1.shape
    tm = _tile(T, 512)
    row = lambda i: (i, 0)
    fixed = lambda i: (0, 0)
    wspec = lambda r, c: pl.BlockSpec((None, r, c), lambda i: (wl, 0, 0))
    vec = lambda c: pl.BlockSpec((None, 1, c), lambda i: (layer, 0, 0))

    def body(dx_ref, r1_ref, lg_ref, oa_ref, ob_ref, g_ref, bg_ref, wpa_ref, wpb_ref, wo_ref,
             dr_ref, mix_ref, dya_ref, dyb_ref, dg_ref, doa_ref, dob_ref, dlg_ref, dlb_ref, dbg_ref):
        @pl.when(pl.program_id(0) == 0)
        def _():
            dlg_ref[...] = jnp.zeros_like(dlg_ref)
            dlb_ref[...] = jnp.zeros_like(dlb_ref)
            dbg_ref[...] = jnp.zeros_like(dbg_ref)

        dx = dx_ref[...]
        dr, xhat = _ln_bwd(dx, r1_ref[...], lg_ref[...])
        dr_ref[...] = dr
        dlg_ref[...] += jnp.sum(dx * xhat, axis=0, keepdims=True)
        dlb_ref[...] += jnp.sum(dx, axis=0, keepdims=True)
        dmix = _dot_nt(dr.astype(BF16), wo_ref[...])
        ya, yb, sa, sb = _gated_mix(oa_ref, ob_ref, g_ref, bg_ref, wpa_ref, wpb_ref, D)
        mix_ref[...] = (sa * ya + sb * yb).astype(BF16)
        dya = (dmix * sa).astype(BF16)
        dyb = (dmix * sb).astype(BF16)
        dya_ref[...] = dya
        dyb_ref[...] = dyb
        dga = dmix * ya * (sa * (1.0 - sa))
        dgb = dmix * yb * (sb * (1.0 - sb))
        dg_ref[:, :D] = dga.astype(BF16)
        dg_ref[:, D:] = dgb.astype(BF16)
        dbg_ref[:, :D] += jnp.sum(dga, axis=0, keepdims=True)
        dbg_ref[:, D:] += jnp.sum(dgb, axis=0, keepdims=True)
        doa_ref[...] = _dot_nt(dya, wpa_ref[...]).astype(BF16)
        dob_ref[...] = _dot_nt(dyb, wpb_ref[...]).astype(BF16)

    return pl.pallas_call(
        body, name=name, grid=(T // tm,),
        in_specs=[pl.BlockSpec((tm, D), row), pl.BlockSpec((tm, D), row), vec(D),
                  pl.BlockSpec((tm, WIDTH), row), pl.BlockSpec((tm, WIDTH), row), pl.BlockSpec((tm, 2 * D), row),
                  vec(2 * D), wspec(WIDTH, D), wspec(WIDTH, D), wspec(D, D)],
        out_specs=[pl.BlockSpec((tm, D), row), pl.BlockSpec((tm, D), row), pl.BlockSpec((tm, D), row),
                   pl.BlockSpec((tm, D), row), pl.BlockSpec((tm, 2 * D), row),
                   pl.BlockSpec((tm, WIDTH), row), pl.BlockSpec((tm, WIDTH), row),
                   pl.BlockSpec((1, D), fixed), pl.BlockSpec((1, D), fixed), pl.BlockSpec((1, 2 * D), fixed)],
        out_shape=[jax.ShapeDtypeStruct((T, D), F32), jax.ShapeDtypeStruct((T, D), BF16),
                   jax.ShapeDtypeStruct((T, D), BF16), jax.ShapeDtypeStruct((T, D), BF16),
                   jax.ShapeDtypeStruct((T, 2 * D), BF16),
                   jax.ShapeDtypeStruct((T, WIDTH), BF16), jax.ShapeDtypeStruct((T, WIDTH), BF16),
                   jax.ShapeDtypeStruct((1, D), F32), jax.ShapeDtypeStruct((1, D), F32),
                   jax.ShapeDtypeStruct((1, 2 * D), F32)],
        compiler_params=_cparams(("arbitrary",), 56),
    )(dx1, r1, lng, oa, ob, g, bg, wpa, wpb, wo)


def _ffn_fwd(x1, wfi, wfo, wl, lng, lnb, layer, *, alpha, name, job=None):
    T, D = x1.shape
    tf = wfi.shape[-1]
    nj = wfi.shape[0] // 2
    tm = _tile(T, 1024)
    vec = lambda c: pl.BlockSpec((None, 1, c), lambda i, j: (layer, 0, 0))

    def body(x_ref, wg_ref, wu_ref, wo_ref, lg_ref, lb_ref, gs_ref, us_ref, r2_ref, x2_ref, acc_ref, xb_ref):
        j = pl.program_id(1)

        @pl.when(j == 0)
        def _():
            xb_ref[...] = x_ref[...].astype(BF16)
            acc_ref[...] = jnp.zeros_like(acc_ref)

        gv = _dot(xb_ref[...], wg_ref[...])
        uv = _dot(xb_ref[...], wu_ref[...])
        gs_ref[...] = gv.astype(gs_ref.dtype)
        us_ref[...] = uv.astype(us_ref.dtype)
        act = gv * jax.nn.sigmoid(gv) * uv
        acc_ref[...] += _dot(act.astype(BF16), wo_ref[...])

        @pl.when(j == nj - 1)
        def _():
            r2 = alpha * x_ref[...] + acc_ref[...]
            r2_ref[...] = r2
            x2_ref[...] = _ln_fwd(r2, lg_ref[...], lb_ref[...])

    return _call_carrying(
        job, body, name=name, grid=(T // tm, nj),
        in_specs=[pl.BlockSpec((tm, D), lambda i, j: (i, 0)),
                  pl.BlockSpec((None, None, D, tf), lambda i, j: (j, wl, 0, 0)),
                  pl.BlockSpec((None, None, D, tf), lambda i, j: (j + nj, wl, 0, 0)),
                  pl.BlockSpec((None, tf, D), lambda i, j: (wl, j, 0)),
                  vec(D), vec(D)],
        out_specs=[pl.BlockSpec((None, tm, tf), lambda i, j: (j, i, 0)),
                   pl.BlockSpec((None, tm, tf), lambda i, j: (j, i, 0)),
                   pl.BlockSpec((tm, D), lambda i, j: (i, 0)),
                   pl.BlockSpec((tm, D), lambda i, j: (i, 0))],
        out_shape=[jax.ShapeDtypeStruct((nj, T, tf), BF16), jax.ShapeDtypeStruct((nj, T, tf), BF16),
                   jax.ShapeDtypeStruct((T, D), F32), jax.ShapeDtypeStruct((T, D), F32)],
        scratch_shapes=[pltpu.VMEM((tm, D), F32), pltpu.VMEM((tm, D), BF16)],
        vmem_mib=56, args=(x1, wfi, wfi, wfo, lng, lnb))


def _ffn_bwd(dx2, r2, lng, gs, us, wfi, wfo, wl, layer, *, alpha, name, job=None):
    T, D = dx2.shape
    tf = wfi.shape[-1]
    nj = wfi.shape[0] // 2
    tm = _tile(T, 512)
    vec = lambda c: pl.BlockSpec((None, 1, c), lambda i, j: (layer, 0, 0))
    blk = lambda: pl.BlockSpec((None, tm, tf), lambda i, j: (j, i, 0))

    def body(dx_ref, r2_ref, lg_ref, gs_ref, us_ref, wg_ref, wu_ref, wo_ref,
             dr_ref, act_ref, dg_ref, du_ref, dx1_ref, dlg_ref, dlb_ref, acc_ref, drb_ref):
        i = pl.program_id(0)
        j = pl.program_id(1)

        @pl.when((i == 0) & (j == 0))
        def _():
            dlg_ref[...] = jnp.zeros_like(dlg_ref)
            dlb_ref[...] = jnp.zeros_like(dlb_ref)

        @pl.when(j == 0)
        def _():
            dx = dx_ref[...]
            dr, xhat = _ln_bwd(dx, r2_ref[...], lg_ref[...])
            dlg_ref[...] += jnp.sum(dx * xhat, axis=0, keepdims=True)
            dlb_ref[...] += jnp.sum(dx, axis=0, keepdims=True)
            drb_ref[...] = dr.astype(BF16)
            dr_ref[...] = dr.astype(BF16)
            acc_ref[...] = alpha * dr

        dact = _dot_nt(drb_ref[...], wo_ref[...])
        gv = gs_ref[...].astype(F32)
        uv = us_ref[...].astype(F32)
        s = jax.nn.sigmoid(gv)
        silu = gv * s
        act_ref[...] = (silu * uv).astype(BF16)
        dg = (dact * uv * (s * (1.0 + gv * (1.0 - s)))).astype(BF16)
        du = (dact * silu).astype(BF16)
        dg_ref[...] = dg
        du_ref[...] = du
        acc_ref[...] += _dot_nt(dg, wg_ref[...]) + _dot_nt(du, wu_ref[...])

        @pl.when(j == nj - 1)
        def _():
            dx1_ref[...] = acc_ref[...]

    return _call_carrying(
        job, body, name=name, grid=(T // tm, nj),
        in_specs=[pl.BlockSpec((tm, D), lambda i, j: (i, 0)), pl.BlockSpec((tm, D), lambda i, j: (i, 0)), vec(D),
                  blk(), blk(),
                  pl.BlockSpec((None, None, D, tf), lambda i, j: (j, wl, 0, 0)),
                  pl.BlockSpec((None, None, D, tf), lambda i, j: (j + nj, wl, 0, 0)),
                  pl.BlockSpec((None, tf, D), lambda i, j: (wl, j, 0))],
        out_specs=[pl.BlockSpec((tm, D), lambda i, j: (i, 0)), blk(), blk(), blk(),
                   pl.BlockSpec((tm, D), lambda i, j: (i, 0)),
                   pl.BlockSpec((1, D), lambda i, j: (0, 0)), pl.BlockSpec((1, D), lambda i, j: (0, 0))],
        out_shape=[jax.ShapeDtypeStruct((T, D), BF16),
                   jax.ShapeDtypeStruct((nj, T, tf), BF16), jax.ShapeDtypeStruct((nj, T, tf), BF16),
                   jax.ShapeDtypeStruct((nj, T, tf), BF16),
                   jax.ShapeDtypeStruct((T, D), F32),
                   jax.ShapeDtypeStruct((1, D), F32), jax.ShapeDtypeStruct((1, D), F32)],
        scratch_shapes=[pltpu.VMEM((tm, D), F32), pltpu.VMEM((tm, D), BF16)],
        vmem_mib=56, args=(dx2, r2, lng, gs, us, wfi, wfi, wfo))


def _loss_head(y, target, *, name):
    T, D = y.shape
    tm = _tile(T, 1024)

    def body(y_ref, t_ref, dy_ref, sq_ref):
        @pl.when(pl.program_id(0) == 0)
        def _():
            sq_ref[...] = jnp.zeros_like(sq_ref)
        err = y_ref[...] - t_ref[...]
        dy_ref[...] = err * (1.0 / D)
        sq_ref[...] += jnp.sum(err * err, axis=0, keepdims=True)

    return pl.pallas_call(
        body, name=name, grid=(T // tm,),
        in_specs=[pl.BlockSpec((tm, D), lambda i: (i, 0)), pl.BlockSpec((tm, D), lambda i: (i, 0))],
        out_specs=[pl.BlockSpec((tm, D), lambda i: (i, 0)), pl.BlockSpec((1, D), lambda i: (0, 0))],
        out_shape=[jax.ShapeDtypeStruct((T, D), F32), jax.ShapeDtypeStruct((1, D), F32)],
        compiler_params=_cparams(("arbitrary",)),
    )(y, target)


def _my_place():
    return lax.axis_index("x"), lax.axis_index("y"), lax.axis_index("c")


def _peer(place, k):
    x, y, c = place
    return (1 - x if k & 4 else x, 1 - y if k & 2 else y, 1 - c if k & 1 else c)


def _logical(place):
    x, y, c = place
    return 4 * x + 2 * y + c


def _block_of(ref, mode, idx):
    if mode == "blk":
        return ref.at[idx]
    if mode == "col":
        size = ref.shape[2] // N_DEV
        return ref.at[:, :, pl.ds(pl.multiple_of(idx * size, size), size)]
    size = ref.shape[1] // N_DEV
    return ref.at[:, pl.ds(pl.multiple_of(idx * size, size), size), :]


def _full_shape(shard, mode):
    if mode == "blk":
        return (N_DEV,) + shard.shape
    if mode == "col":
        return shard.shape[:2] + (N_DEV * shard.shape[2],)
    return (shard.shape[0], N_DEV * shard.shape[1], shard.shape[2])


class _Exchange:
    def __init__(self, arrays, out_shape, build):
        self.arrays = list(arrays)
        self.out_shape = list(out_shape)
        self.build = build

    def scratch(self):
        n = len(self.arrays)
        return [pltpu.SemaphoreType.DMA((n * N_DEV,)), pltpu.SemaphoreType.DMA((n * N_DEV,)),
                pltpu.SemaphoreType.DMA((n,))]

    def start(self, ins, outs, sems):
        for cp in self.build(ins, outs, *sems):
            cp.start()

    def wait(self, ins, outs, sems):
        for cp in self.build(ins, outs, *sems):
            cp.wait()

    def run(self, name):
        n_in, n_out = len(self.arrays), len(self.out_shape)
        hbm = pl.BlockSpec(memory_space=pltpu.HBM)

        def body(*refs):
            ins, outs, sems = refs[:n_in], refs[n_in:n_in + n_out], refs[n_in + n_out:]
            self.start(ins, outs, sems)
            self.wait(ins, outs, sems)

        return pl.pallas_call(
            body, name=name, in_specs=[hbm] * n_in, out_specs=[hbm] * n_out,
            out_shape=self.out_shape, scratch_shapes=self.scratch(),
        )(*self.arrays)


def _copies_to_all(src_of, dst_of, n, send, recv, local):
    me = _my_place()
    copies = []
    for a in range(n):
        copies.append(pltpu.make_async_copy(src_of(a, _logical(me)), dst_of(a), local.at[a]))
        for k in range(1, N_DEV):
            peer = _peer(me, k)
            copies.append(pltpu.make_async_remote_copy(
                src_ref=src_of(a, _logical(peer)), dst_ref=dst_of(a),
                send_sem=send.at[a * N_DEV + k], recv_sem=recv.at[a * N_DEV + k],
                device_id=peer, device_id_type=MESH))
    return copies


def _gather_job(shards, modes):
    def build(ins, outs, send, recv, local):
        my_id = _logical(_my_place())
        return _copies_to_all(lambda a, dev: ins[a], lambda a: _block_of(outs[a], modes[a], my_id),
                              len(shards), send, recv, local)

    return _Exchange(shards, [jax.ShapeDtypeStruct(_full_shape(s, m), s.dtype) for s, m in zip(shards, modes)], build)


def _gather_via_sibling(shard, mode, *, name):
    hbm = pl.BlockSpec(memory_space=pltpu.HBM)

    def body(x_ref, o_ref, send, recv, local):
        x, y, c = _my_place()
        me, sibling = (x, y, c), (x, y, 1 - c)
        chips = [(1 - x, y), (x, 1 - y), (1 - x, 1 - y)]

        def copy(k, block, to, src=None):
            dst = _block_of(o_ref, mode, _logical(block))
            return pltpu.make_async_remote_copy(src_ref=dst if src is None else src, dst_ref=dst,
                                                send_sem=send.at[k], recv_sem=recv.at[k],
                                                device_id=to, device_id_type=MESH)

        mine = pltpu.make_async_copy(x_ref, _block_of(o_ref, mode, _logical(me)), local)
        mine.start()
        first = [copy(0, me, sibling, src=x_ref)]
        first += [copy(1 + j, me, (*chip, c), src=x_ref) for j, chip in enumerate(chips)]
        for cp in first:
            cp.start()
        passed = [copy(4 + j, (*chip, c), sibling) for j, chip in enumerate(chips)]
        for j, chip in enumerate(chips):
            copy(1 + j, (*chip, c), me).wait_recv()
            passed[j].start()
        copy(0, sibling, me).wait_recv()
        for j, chip in enumerate(chips):
            copy(4 + j, (*chip, 1 - c), me).wait_recv()
        for cp in first + passed:
            cp.wait_send()
        mine.wait()

    return pl.pallas_call(
        body, name=name, in_specs=[hbm], out_specs=hbm,
        out_shape=jax.ShapeDtypeStruct(_full_shape(shard, mode), shard.dtype),
        scratch_shapes=[pltpu.SemaphoreType.DMA((7,)), pltpu.SemaphoreType.DMA((7,)), pltpu.SemaphoreType.DMA],
    )(shard)


def _grad_block(ref, mode, idx):
    if mode == "blk":
        return ref.at[idx]
    if mode == "col":
        size = ref.shape[1] // N_DEV
        return ref.at[:, pl.ds(pl.multiple_of(idx * size, size), size)]
    size = ref.shape[0] // N_DEV
    return ref.at[pl.ds(pl.multiple_of(idx * size, size), size), :]


def _grad_shard_shape(g, mode):
    if mode == "blk":
        return g.shape[1:]
    if mode == "col":
        return (g.shape[0], g.shape[1] // N_DEV)
    return (g.shape[0] // N_DEV, g.shape[1])


def _grads_job(groups, modes):
    flat = [(g, w, l) for w, per_w in enumerate(groups) for l, g in enumerate(per_w)]

    def build(ins, outs, send, recv, local):
        my_id = _logical(_my_place())
        return _copies_to_all(lambda a, dev: _grad_block(ins[a], modes[flat[a][1]], dev),
                              lambda a: outs[flat[a][1]].at[my_id, flat[a][2]],
                              len(flat), send, recv, local)

    out_shape = [jax.ShapeDtypeStruct((N_DEV, len(per_w)) + _grad_shard_shape(per_w[0], m), per_w[0].dtype)
                 for per_w, m in zip(groups, modes)]
    return _Exchange([g for g, _, _ in flat], out_shape, build)


def _adamw(w, g, m, v):
    m = ADAM_B1 * m + (1.0 - ADAM_B1) * g
    v = ADAM_B2 * v + (1.0 - ADAM_B2) * (g * g)
    m_hat = m / (1.0 - ADAM_B1 ** ADAM_STEP)
    v_hat = v / (1.0 - ADAM_B2 ** ADAM_STEP)
    delta = -ADAM_LR * (m_hat / (jnp.sqrt(v_hat) + ADAM_EPS) + ADAM_WD * w)
    return delta, m, v


def _sum_slots_adamw(slots, w, m, v, *, name):
    n_l = len(slots)
    R, C = slots[0].shape[1:]
    tr = _tile(R, 256)
    n_r = R // tr

    def body(*refs):
        s_refs = refs[:n_l]
        w_ref, m_ref, v_ref, g_out, d_out, m_out, v_out = refs[n_l:]
        for layer in range(n_l):
            @pl.when(pl.program_id(0) == layer)
            def _(s_ref=s_refs[layer]):
                g = s_ref[0].astype(F32)
                for s in range(1, N_DEV):
                    g = g + s_ref[s].astype(F32)
                delta, m_new, v_new = _adamw(w_ref[...], g, m_ref[...], v_ref[...])
                g_out[...] = g
                d_out[...] = delta
                m_out[...] = m_new
                v_out[...] = v_new

    slot_spec = lambda layer: pl.BlockSpec((N_DEV, tr, C), lambda l, i: (0, jnp.where(l == layer, i, 0), 0))
    spec = pl.BlockSpec((tr, C), lambda l, i: (l * n_r + i, 0))
    return pl.pallas_call(
        body, name=name, grid=(n_l, n_r),
        in_specs=[slot_spec(layer) for layer in range(n_l)] + [spec, spec, spec],
        out_specs=[spec] * 4,
        out_shape=[jax.ShapeDtypeStruct((n_l * R, C), F32)] * 4,
        compiler_params=_cparams(("arbitrary", "arbitrary")),
    )(*slots, w, m, v)


def _small_allreduce_adamw(g, w, m, v, *, name):
    R = g.shape[0]
    vmem = pl.BlockSpec(memory_space=pltpu.VMEM)

    def body(g_ref, w_ref, m_ref, v_ref, g_out, d_out, m_out, v_out, slots, send, recv):
        me = _my_place()
        my_id = _logical(me)
        slots[my_id] = g_ref[...]
        copies = []
        for k in range(1, N_DEV):
            cp = pltpu.make_async_remote_copy(
                src_ref=g_ref, dst_ref=slots.at[my_id], send_sem=send.at[k], recv_sem=recv.at[k],
                device_id=_peer(me, k), device_id_type=MESH)
            cp.start()
            copies.append(cp)
        for cp in copies:
            cp.wait()
        total = slots[0]
        for s in range(1, N_DEV):
            total = total + slots[s]
        delta, m_new, v_new = _adamw(w_ref[...], total, m_ref[...], v_ref[...])
        g_out[...] = total
        d_out[...] = delta
        m_out[...] = m_new
        v_out[...] = v_new

    return pl.pallas_call(
        body, name=name,
        in_specs=[vmem] * 4, out_specs=[vmem] * 4,
        out_shape=[jax.ShapeDtypeStruct((R, LANES), F32)] * 4,
        scratch_shapes=[pltpu.VMEM((N_DEV, R, LANES), F32),
                        pltpu.SemaphoreType.DMA((N_DEV,)), pltpu.SemaphoreType.DMA((N_DEV,))],
    )(g, w, m, v)


def _pack(parts):
    flat = jnp.concatenate([p.reshape(-1) for p in parts])
    rows = -(-flat.shape[0] // (8 * LANES)) * 8
    return jnp.pad(flat, (0, rows * LANES - flat.shape[0])).reshape(rows, LANES)


def _unpack(packed, like):
    flat = packed.reshape(-1)
    out, pos = [], 0
    for p in like:
        out.append(flat[pos:pos + p.size].reshape(p.shape))
        pos += p.size
    return out


def kernel(x, w_in, b_gate, rel_bias, w_proj_a, w_proj_b, w_out, ln1_g, ln1_b, w_ffn_in, w_ffn_out, ln2_g, ln2_b, loss_target, m_w_in, m_b_gate, m_rel_bias, m_w_proj_a, m_w_proj_b, m_w_out, m_ln1_g, m_ln1_b, m_w_ffn_in, m_w_ffn_out, m_ln2_g, m_ln2_b, v_w_in, v_b_gate, v_rel_bias, v_w_proj_a, v_w_proj_b, v_w_out, v_ln1_g, v_ln1_b, v_w_ffn_in, v_w_ffn_out, v_ln2_g, v_ln2_b):
    L = w_in.shape[0]
    T, D = x.shape[1], x.shape[2]
    alpha = float((2 * L) ** 0.25)
    n_qkv = 6 * WIDTH

    big = [w_in, w_proj_a, w_proj_b, w_out, w_ffn_in, w_ffn_out]
    kinds = ["in", "pa", "pb", "o", "fi", "fo"]
    modes = ["col", "col", "col", "row", "blk", "row"]
    mode_of = dict(zip(kinds, modes))
    w_bf = dict(zip(kinds, [w.astype(BF16) for w in big]))

    def gather_of(ks, l):
        return _gather_job([w_bf[k][l:l + 1] for k in ks], [mode_of[k] for k in ks])

    W = [dict() for _ in range(L)]
    W[0]["in"] = _gather_via_sibling(w_bf["in"][:1], mode_of["in"], name="gather_w_in_first")
    vec3 = lambda a: a[:, None, :]
    bg3, l1g, l1b, l2g, l2b = vec3(b_gate), vec3(ln1_g), vec3(ln1_b), vec3(ln2_g), vec3(ln2_b)
    b_col0 = 3 * WIDTH // LANES

    h = x[0]
    saved = []
    for l in range(L):
        ahead = l + 1 < L
        soon = ["pa", "pb", "o", "fo"]
        (qkv, gates), got = _in_proj(h, W[l]["in"], 0, n_qkv=n_qkv, name=f"in_proj_{l}",
                                     job=gather_of(soon, 0) if l == 0 else None)
        W[l].update(zip(soon, got))
        bias = _bias_variants(_toeplitz_bias(rel_bias[l]))
        oa, got = _attn_a_fwd(qkv, bias, name=f"attn_a_fwd_{l}", job=gather_of(["fi"], 0) if l == 0 else None)
        W[l].update(zip(["fi"], got))
        early = ["in", "pa", "pb", "o"]
        ob, got = _attn_b_fwd(qkv, col0=b_col0, name=f"attn_b_fwd_{l}", job=gather_of(early, l + 1) if ahead else None)
        W[l + 1 if ahead else l].update(zip(early, got))
        x1, r1, x1b = _proj_fwd(oa, ob, gates, bg3, W[l]["pa"], W[l]["pb"], W[l]["o"], 0, h, l1g, l1b, l,
                           alpha=alpha, name=f"proj_fwd_{l}")
        (gs, us, r2, x2), got = _ffn_fwd(x1, W[l]["fi"], W[l]["fo"], 0, l2g, l2b, l, alpha=alpha, name=f"ffn_fwd_{l}",
                                         job=gather_of(["fi", "fo"], l + 1) if ahead else None)
        W[l + 1 if ahead else l].update(zip(["fi", "fo"], got))
        saved.append((h, qkv, gates, bias, oa, ob, x1b, r1, gs, us, r2))
        h = x2

    d_h, sq = _loss_head(h, loss_target[0], name="loss_head")
    loss = lax.psum((0.5 / D) * jnp.sum(sq), ("x", "y", "c"))

    g_bg, g_rb, g_l1g, g_l1b, g_l2g, g_l2b = ([None] * L for _ in range(6))
    slot = {k: [None] * L for k in kinds}

    def exchange_of(ks, grads):
        return _grads_job([[g] for g in grads], [mode_of[k] for k in ks])

    w_in_above = None
    for l in reversed(range(L)):
        xin, qkv, gates, bias, oa, ob, x1b, r1, gs, us, r2 = saved[l]
        (dr2, act, dgt, dup, dx1, g_l2g[l], g_l2b[l]), got = _ffn_bwd(
            d_h, r2, l2g, gs, us, W[l]["fi"], W[l]["fo"], 0, l, alpha=alpha, name=f"ffn_bwd_{l}", job=w_in_above)
        if w_in_above is not None:
            (slot["in"][l + 1],) = got
        g_fo = _mm_tn_blocked_a(act, dr2, name=f"grad_w_ffn_out_{l}").reshape(-1, D)
        g_fi = jnp.concatenate(_mm_tn_blocked_pair(x1b, dgt, dup, name=f"grad_w_ffn_in_{l}"), axis=0)
        (dr1, mixin, dya, dyb, dgates, doa, dob, g_l1g[l], g_l1b[l], g_bg[l]) = _proj_bwd(
            dx1, r1, l1g, oa, ob, gates, bg3, W[l]["pa"], W[l]["pb"], W[l]["o"], 0, l, name=f"proj_bwd_{l}")
        g_o = _mm_tn(mixin, dr1, tm=_tile(D, 1024), tn=_tile(D, 1024), name=f"grad_w_out_{l}")
        g_pa = _mm_tn(oa, dya, tm=WIDTH, tn=_tile(D, 1024), name=f"grad_w_proj_a_{l}")
        g_pb = _mm_tn(ob, dyb, tm=WIDTH, tn=_tile(D, 1024), name=f"grad_w_proj_b_{l}")
        (dqa, dka, dva, dbias), (slot["fi"][l], slot["fo"][l]) = _attn_a_bwd(
            qkv, bias, doa, name=f"attn_a_bwd_{l}", job=exchange_of(["fi", "fo"], [g_fi, g_fo]))
        g_rb[l] = _toeplitz_bias_grad(dbias)
        (dqb, dkb, dvb), (slot["pa"][l], slot["pb"][l], slot["o"][l]) = _attn_b_bwd(
            qkv, ob, dob, col0=b_col0, name=f"attn_b_bwd_{l}", job=exchange_of(["pa", "pb", "o"], [g_pa, g_pb, g_o]))
        d_qkv = [dqa, dka, dva, dqb, dkb, dvb]
        g_in = _mm_tn_pieces(xin, d_qkv, dgates, name=f"grad_w_in_{l}")
        w_in_above = exchange_of(["in"], [g_in])
        d_h, got = _mm_nt_pieces_add(d_qkv, dgates, W[l]["in"], 0, dr1, alpha, name=f"grad_x_{l}",
                                     job=w_in_above if l == 0 else None)
        if l == 0:
            (slot["in"][0],) = got
    grad_x = d_h[None]

    moments_m = [m_w_in, m_w_proj_a, m_w_proj_b, m_w_out, m_w_ffn_in, m_w_ffn_out]
    moments_v = [v_w_in, v_w_proj_a, v_w_proj_b, v_w_out, v_w_ffn_in, v_w_ffn_out]
    names = ["w_in", "w_proj_a", "w_proj_b", "w_out", "w_ffn_in", "w_ffn_out"]
    big_out = {}
    for nm, k, w, m, v in zip(names, kinds, big, moments_m, moments_v):
        two = lambda a: a.reshape(-1, a.shape[-1])
        per_layer = [s.reshape(N_DEV, -1, s.shape[-1]) for s in slot[k]]
        res = _sum_slots_adamw(per_layer, two(w), two(m), two(v), name=f"adamw_{nm}")
        big_out[nm] = [r.reshape(w.shape) for r in res]

    small_w = [b_gate, rel_bias, ln1_g, ln1_b, ln2_g, ln2_b]
    small_g = [jnp.stack(g) for g in (g_bg, g_rb, g_l1g, g_l1b, g_l2g, g_l2b)]
    small_m = [m_b_gate, m_rel_bias, m_ln1_g, m_ln1_b, m_ln2_g, m_ln2_b]
    small_v = [v_b_gate, v_rel_bias, v_ln1_g, v_ln1_b, v_ln2_g, v_ln2_b]
    res = _small_allreduce_adamw(_pack(small_g), _pack(small_w), _pack(small_m), _pack(small_v),
                                 name="allreduce_small_adamw")
    small_names = ["b_gate", "rel_bias", "ln1_g", "ln1_b", "ln2_g", "ln2_b"]
    small_out = {nm: [] for nm in small_names}
    for packed in res:
        for nm, arr in zip(small_names, _unpack(packed, small_w)):
            small_out[nm].append(arr)

    order = ["w_in", "b_gate", "rel_bias", "w_proj_a", "w_proj_b", "w_out", "ln1_g", "ln1_b",
             "w_ffn_in", "w_ffn_out", "ln2_g", "ln2_b"]
    every = {**big_out, **small_out}
    outs = [loss, grad_x]
    for kind in range(4):
        outs += [every[nm][kind] for nm in order]
    return tuple(outs)
```

```python
import functools
import math

import jax
import jax.numpy as jnp
import numpy as np
from jax import lax
from jax.experimental import pallas as pl
from jax.experimental.pallas import tpu as pltpu

F32 = jnp.float32
BF16 = jnp.bfloat16

HEAD_DIM = 64
CHUNK = 64
LEFT_CHUNKS = 8
REL_CLIP = 256
N_REL = 2 * REL_CLIP + 1
WIDTH = 512
LANES = 128
A_TQ = 256
A_WIN = A_TQ + LEFT_CHUNKS * CHUNK
A_STRIP = 128
B_TQ = 512
B_TS = 256
B_PIECE = 64
B_DEAD = -160.0
LN_EPS = 1e-5
QK_SCALE = 1.0 / math.sqrt(HEAD_DIM)
LOG2E = 1.4426950408889634
NEG = -1e30

ADAM_LR = 0.001
ADAM_B1 = 0.9
ADAM_B2 = 0.999
ADAM_EPS = 1e-08
ADAM_WD = 0.01
ADAM_STEP = 10

N_DEV = 8
MESH = pl.DeviceIdType.MESH
MIB = 1024 * 1024


def _cparams(sem=None, vmem_mib=48):
    return pltpu.CompilerParams(dimension_semantics=sem, vmem_limit_bytes=vmem_mib * MIB)


def _dot(a, b):
    return jnp.dot(a, b, preferred_element_type=F32)


def _dot_nt(a, b):
    return lax.dot_general(a, b, (((1,), (1,)), ((), ())), preferred_element_type=F32)


def _dot_tn(a, b):
    return lax.dot_general(a, b, (((0,), (0,)), ((), ())), preferred_element_type=F32)


def _tile(n, pref):
    if n <= pref:
        return n
    for t in range(pref - pref % 8, 0, -8):
        if n % t == 0:
            return t
    raise ValueError((n, pref))


def _in_proj(a, w, layer, *, n_qkv, name, job=None):
    M, K = a.shape
    N = w.shape[2]
    tm = _tile(M, 1024)
    tn = 1024
    assert n_qkv % tn == 0 and (N - n_qkv) % tn == 0
    n_q = n_qkv // tn

    def body(a_ref, w_ref, q_ref, g_ref, ab_ref):
        j = pl.program_id(1)

        @pl.when(j == 0)
        def _():
            ab_ref[...] = a_ref[...].astype(BF16)

        res = _dot(ab_ref[...], w_ref[...])

        @pl.when(j < n_q)
        def _():
            q_ref[...] = res.astype(BF16)

        @pl.when(j >= n_q)
        def _():
            g_ref[...] = res

    return _call_carrying(
        job, body, name=name, grid=(M // tm, N // tn),
        in_specs=[pl.BlockSpec((tm, K), lambda i, j: (i, 0)),
                  pl.BlockSpec((None, K, tn), lambda i, j: (layer, 0, j))],
        out_specs=[pl.BlockSpec((tm, tn), lambda i, j: (i, jnp.minimum(j, n_q - 1))),
                   pl.BlockSpec((tm, tn), lambda i, j: (i, jnp.maximum(j - n_q, 0)))],
        out_shape=[jax.ShapeDtypeStruct((M, n_qkv), BF16), jax.ShapeDtypeStruct((M, N - n_qkv), F32)],
        scratch_shapes=[pltpu.VMEM((tm, K), BF16)], vmem_mib=48, args=(a, w))


def _mm_nt_add(a, w, layer, add, add_scale, *, name, job=None):
    M, K = a.shape
    N = w.shape[1]
    tm = _tile(M, 1024)
    tk = _tile(K, 1024)

    def body(a_ref, w_ref, add_ref, o_ref):
        @pl.when(pl.program_id(1) == 0)
        def _():
            o_ref[...] = add_scale * add_ref[...]
        o_ref[...] += _dot_nt(a_ref[...], w_ref[...])

    (out,), rode = _call_carrying(
        job, body, name=name, grid=(M // tm, K // tk),
        in_specs=[pl.BlockSpec((tm, tk), lambda i, k: (i, k)),
                  pl.BlockSpec((None, N, tk), lambda i, k: (layer, 0, k)),
                  pl.BlockSpec((tm, N), lambda i, k: (i, 0))],
        out_specs=[pl.BlockSpec((tm, N), lambda i, k: (i, 0))],
        out_shape=[jax.ShapeDtypeStruct((M, N), F32)],
        scratch_shapes=[], vmem_mib=48, args=(a, w, add))
    return out, rode


def _pieces_layout(pieces, wide):
    P = pieces[0].shape[1]
    assert all(p.shape[1] == P for p in pieces) and wide.shape[1] % P == 0
    n_blocks = len(pieces) + wide.shape[1] // P
    assert n_blocks % 2 == 0 and len(pieces) > n_blocks // 2
    return P, n_blocks // 2


def _mm_tn_pieces(a, pieces, wide, *, name):
    T, M = a.shape
    P, n_half = _pieces_layout(pieces, wide)
    n_first = n_half
    n_rest = len(pieces) - n_first
    tk = _tile(T, 1024)
    n_k = T // tk

    def body(*refs):
        a_ref, p_refs, wide_ref = refs[0], refs[1:1 + len(pieces)], refs[1 + len(pieces)]
        o_ref, acc_ref = refs[2 + len(pieces):]
        j, k = pl.program_id(0), pl.program_id(1)

        @pl.when(k == 0)
        def _():
            acc_ref[...] = jnp.zeros_like(acc_ref)

        x = a_ref[...].astype(BF16)

        @pl.when(j == 0)
        def _():
            for p in range(n_first):
                acc_ref[:, p * P:(p + 1) * P] += _dot_tn(x, p_refs[p][...])

        @pl.when(j == 1)
        def _():
            for p in range(n_rest):
                acc_ref[:, p * P:(p + 1) * P] += _dot_tn(x, p_refs[n_first + p][...])
            acc_ref[:, n_rest * P:] += _dot_tn(x, wide_ref[...])

        @pl.when(k == n_k - 1)
        def _():
            o_ref[...] = acc_ref[...].astype(o_ref.dtype)

    first = lambda: pl.BlockSpec((tk, P), lambda j, k: (jnp.where(j == 0, k, n_k - 1), 0))
    rest = lambda: pl.BlockSpec((tk, P), lambda j, k: (jnp.where(j == 1, k, 0), 0))
    return pl.pallas_call(
        body, name=name, grid=(2, n_k),
        in_specs=[pl.BlockSpec((tk, M), lambda j, k: (k, 0))] + [first() for _ in range(n_first)]
                 + [rest() for _ in range(n_rest)]
                 + [pl.BlockSpec((tk, wide.shape[1]), lambda j, k: (jnp.where(j == 1, k, 0), 0))],
        out_specs=pl.BlockSpec((M, n_half * P), lambda j, k: (0, j)),
        out_shape=jax.ShapeDtypeStruct((M, 2 * n_half * P), BF16),
        scratch_shapes=[pltpu.VMEM((M, n_half * P), F32)],
        compiler_params=_cparams(("arbitrary", "arbitrary"), 56),
    )(a, *pieces, wide)


def _mm_nt_pieces_add(pieces, wide, w, layer, add, add_scale, *, name, job=None):
    M = add.shape[0]
    N = w.shape[1]
    P, _ = _pieces_layout(pieces, wide)
    tm = _tile(M, 1024)
    tk = 2 * P
    n_pair = len(pieces) // 2
    assert len(pieces) % 2 == 0 and wide.shape[1] % tk == 0
    n_k = n_pair + wide.shape[1] // tk

    def body(*refs):
        p_refs, wide_ref, w_ref, add_ref, o_ref = refs[:len(pieces)], *refs[len(pieces):]
        k = pl.program_id(1)

        @pl.when(k == 0)
        def _():
            o_ref[...] = add_scale * add_ref[...]

        for pair in range(n_pair):
            @pl.when(k == pair)
            def _(pair=pair):
                o_ref[...] += (_dot_nt(p_refs[2 * pair][...], w_ref[:, :P])
                               + _dot_nt(p_refs[2 * pair + 1][...], w_ref[:, P:]))

        @pl.when(k >= n_pair)
        def _():
            o_ref[...] += _dot_nt(wide_ref[...], w_ref[...])

    (out,), rode = _call_carrying(
        job, body, name=name, grid=(M // tm, n_k),
        in_specs=[pl.BlockSpec((tm, P), lambda i, k, pair=n // 2: (jnp.where(k >= pair, i, jnp.maximum(i - 1, 0)), 0))
                  for n in range(len(pieces))]
                 + [pl.BlockSpec((tm, tk), lambda i, k: (i, jnp.maximum(k - n_pair, 0))),
                    pl.BlockSpec((None, N, tk), lambda i, k: (layer, 0, k)),
                    pl.BlockSpec((tm, N), lambda i, k: (i, 0))],
        out_specs=[pl.BlockSpec((tm, N), lambda i, k: (i, 0))],
        out_shape=[jax.ShapeDtypeStruct((M, N), F32)],
        scratch_shapes=[], vmem_mib=56, args=(*pieces, wide, w, add))
    return out, rode


def _tn_body(k_axis, n_k):
    def body(a_ref, b_ref, o_ref, acc_ref):
        k = pl.program_id(k_axis)

        @pl.when(k == 0)
        def _():
            acc_ref[...] = jnp.zeros_like(acc_ref)
        acc_ref[...] += _dot_tn(a_ref[...].astype(BF16), b_ref[...].astype(BF16))

        @pl.when(k == n_k - 1)
        def _():
            o_ref[...] = acc_ref[...].astype(o_ref.dtype)
    return body


def _mm_tn(a, b, *, tm, tn, name):
    T, M = a.shape
    N = b.shape[1]
    tk = _tile(T, 1024)
    return pl.pallas_call(
        _tn_body(2, T // tk), name=name, grid=(M // tm, N // tn, T // tk),
        in_specs=[pl.BlockSpec((tk, tm), lambda i, j, k: (k, i)),
                  pl.BlockSpec((tk, tn), lambda i, j, k: (k, j))],
        out_specs=pl.BlockSpec((tm, tn), lambda i, j, k: (i, j)),
        out_shape=jax.ShapeDtypeStruct((M, N), BF16),
        scratch_shapes=[pltpu.VMEM((tm, tn), F32)],
        compiler_params=_cparams(("parallel", "parallel", "arbitrary")),
    )(a, b)


def _mm_tn_blocked_pair(a, b1, b2, *, name):
    T, M = a.shape
    S, _, N = b1.shape
    tk = _tile(T, 1024)
    n_k = T // tk

    def body(a_ref, b1_ref, b2_ref, o1_ref, o2_ref, acc1_ref, acc2_ref):
        k = pl.program_id(1)

        @pl.when(k == 0)
        def _():
            acc1_ref[...] = jnp.zeros_like(acc1_ref)
            acc2_ref[...] = jnp.zeros_like(acc2_ref)

        a_t = a_ref[...].astype(BF16)
        acc1_ref[...] += _dot_tn(a_t, b1_ref[...])
        acc2_ref[...] += _dot_tn(a_t, b2_ref[...])

        @pl.when(k == n_k - 1)
        def _():
            o1_ref[...] = acc1_ref[...].astype(o1_ref.dtype)
            o2_ref[...] = acc2_ref[...].astype(o2_ref.dtype)

    blk = lambda: pl.BlockSpec((None, tk, N), lambda s, k: (s, k, 0))
    out = lambda: pl.BlockSpec((None, M, N), lambda s, k: (s, 0, 0))
    return pl.pallas_call(
        body, name=name, grid=(S, n_k),
        in_specs=[pl.BlockSpec((tk, M), lambda s, k: (k, 0)), blk(), blk()],
        out_specs=[out(), out()],
        out_shape=[jax.ShapeDtypeStruct((S, M, N), BF16)] * 2,
        scratch_shapes=[pltpu.VMEM((M, N), F32), pltpu.VMEM((M, N), F32)],
        compiler_params=_cparams(("parallel", "arbitrary")),
    )(a, b1, b2)


def _mm_tn_blocked_a(a, b, *, name):
    S, T, M = a.shape
    N = b.shape[1]
    tk = _tile(T, 1024)
    return pl.pallas_call(
        _tn_body(1, T // tk), name=name, grid=(S, T // tk),
        in_specs=[pl.BlockSpec((None, tk, M), lambda s, k: (s, k, 0)),
                  pl.BlockSpec((tk, N), lambda s, k: (k, 0))],
        out_specs=pl.BlockSpec((None, M, N), lambda s, k: (s, 0, 0)),
        out_shape=jax.ShapeDtypeStruct((S, M, N), BF16),
        scratch_shapes=[pltpu.VMEM((M, N), F32)],
        compiler_params=_cparams(("parallel", "arbitrary")),
    )(a, b)


def _ln_fwd(r, g, b):
    mu = jnp.mean(r, axis=-1, keepdims=True)
    xc = r - mu
    var = jnp.mean(xc * xc, axis=-1, keepdims=True)
    return xc * lax.rsqrt(var + LN_EPS) * g + b


def _ln_bwd(dy, r, g):
    mu = jnp.mean(r, axis=-1, keepdims=True)
    xc = r - mu
    var = jnp.mean(xc * xc, axis=-1, keepdims=True)
    rstd = lax.rsqrt(var + LN_EPS)
    xhat = xc * rstd
    dxh = dy * g
    m1 = jnp.mean(dxh, axis=-1, keepdims=True)
    m2 = jnp.mean(dxh * xhat, axis=-1, keepdims=True)
    return rstd * (dxh - m1 - xhat * m2), xhat


def _lane_is_head0():
    return lax.broadcasted_iota(jnp.int32, (1, LANES), 1) < HEAD_DIM


def _band_shape():
    a = np.arange(A_TQ)[:, None] // CHUNK
    b = np.arange(A_WIN)[None, :] // CHUNK
    return (b >= a) & (b <= a + LEFT_CHUNKS)


def _band_streams(strip):
    return [dict(h=h, n=strip, rows=pl.ds(r0, strip)) for h in range(2) for r0 in range(0, A_TQ, strip)]


def _band_scores(st, i, qh_ref, k2, bias_ref):
    st["s"] = _dot_nt(qh_ref[st["h"], st["rows"], :], k2) + bias_ref[st["h"], st["rows"], :]


_BIAS_SHIFTS = tuple(range(A_WIN - A_TQ, -1, -A_TQ))


def _bias_variants(bias):
    H = bias.shape[0]
    return jnp.stack([jnp.concatenate([bias[:, :, s:], jnp.full((H, A_TQ, s), NEG, F32)], axis=2)
                      for s in _BIAS_SHIFTS])


def _bias_variants_grad(db):
    return sum(jnp.pad(db[v, :, :, :A_WIN - s], ((0, 0), (0, 0), (s, 0))) for v, s in enumerate(_BIAS_SHIFTS))


def _bias_spec():
    last = len(_BIAS_SHIFTS) - 1
    return pl.BlockSpec((None, 2, A_TQ, A_WIN), lambda hp, i: (jnp.minimum(i, last), hp, 0, 0))


def _band_softmax(st):
    s = st.pop("s")
    e = jnp.exp(s - jnp.max(s, axis=1, keepdims=True))
    st["p"] = e * (1.0 / jnp.sum(e, axis=1, keepdims=True))


def _attn_a_fwd(qkv, bias, *, name, job=None):
    T = qkv.shape[0]
    n_hp = WIDTH // LANES

    def body(q_ref, k_ref, v_ref, bias_ref, o_ref, qh_ref, acc_ref):
        i = pl.program_id(1)
        row0 = pl.multiple_of(jnp.maximum(i * A_TQ - (A_WIN - A_TQ), 0), A_TQ)
        q2 = q_ref[...] * jnp.asarray(QK_SCALE, BF16)
        k2 = k_ref[pl.ds(row0, A_WIN), :]
        v2 = v_ref[pl.ds(row0, A_WIN), :]
        head0 = _lane_is_head0()
        qh_ref[0] = jnp.where(head0, q2, jnp.zeros_like(q2))
        qh_ref[1] = jnp.where(head0, jnp.zeros_like(q2), q2)

        def scores(st):
            _band_scores(st, i, qh_ref, k2, bias_ref)

        def values(st):
            acc_ref[st["h"], st["rows"], :] = _dot(st.pop("p").astype(BF16), v2)

        _skewed(_band_streams(A_STRIP), [scores, _band_softmax, values])
        o_ref[...] = jnp.where(head0, acc_ref[0], acc_ref[1]).astype(o_ref.dtype)

    (out,), rode = _call_carrying(
        job, body, name=name, grid=(n_hp, T // A_TQ),
        in_specs=[pl.BlockSpec((A_TQ, LANES), lambda hp, i: (i, hp)),
                  pl.BlockSpec((T, LANES), lambda hp, i: (0, hp + n_hp)),
                  pl.BlockSpec((T, LANES), lambda hp, i: (0, hp + 2 * n_hp)),
                  _bias_spec()],
        out_specs=[pl.BlockSpec((A_TQ, LANES), lambda hp, i: (i, hp))],
        out_shape=[jax.ShapeDtypeStruct((T, WIDTH), BF16)],
        scratch_shapes=[pltpu.VMEM((2, A_TQ, LANES), BF16), pltpu.VMEM((2, A_TQ, LANES), F32)],
        vmem_mib=48, args=(qkv, qkv, qkv, bias))
    return out, rode


def _attn_a_bwd(qkv, bias, do, *, name, job=None):
    T = qkv.shape[0]
    TP = T
    n_hp = WIDTH // LANES

    def body(q_ref, k_ref, v_ref, bias_ref, do_ref, dq_ref, dko_ref, dvo_ref, db_ref,
             qh_ref, doh_ref, dqa_ref, dk_ref, dv_ref, qs_ref):
        i = pl.program_id(1)

        @pl.when(i == 0)
        def _():
            dk_ref[...] = jnp.zeros_like(dk_ref)
            dv_ref[...] = jnp.zeros_like(dv_ref)

        @pl.when(i < len(_BIAS_SHIFTS))
        def _():
            db_ref[...] = jnp.zeros_like(db_ref)

        row0 = pl.multiple_of(jnp.maximum(i * A_TQ - (A_WIN - A_TQ), 0), A_TQ)
        window = pl.ds(row0, A_WIN)
        scale = jnp.asarray(QK_SCALE, BF16)
        q2 = q_ref[...] * scale
        do2 = do_ref[...]
        k2 = k_ref[window, :]
        k2s = k2 * scale
        v2 = v_ref[window, :]
        head0 = _lane_is_head0()
        zero = jnp.zeros_like(q2)
        qs_ref[...] = q2
        qh_ref[0] = jnp.where(head0, q2, zero)
        qh_ref[1] = jnp.where(head0, zero, q2)
        doh_ref[0] = jnp.where(head0, do2, zero)
        doh_ref[1] = jnp.where(head0, zero, do2)
        dk = [[], []]
        dv = [[], []]

        def scores(st):
            _band_scores(st, i, qh_ref, k2, bias_ref)
            st["dp"] = _dot_nt(doh_ref[st["h"], st["rows"], :], v2)

        def dscores(st):
            _band_softmax(st)
            p, dp = st.pop("p"), st.pop("dp")
            ds = p * (dp - jnp.sum(p * dp, axis=1, keepdims=True))
            db_ref[st["h"], st["rows"], :] += ds
            st["dsb"] = ds.astype(BF16)
            st["pb"] = p.astype(BF16)

        def grads(st):
            h, rows = st["h"], st["rows"]
            dsb = st.pop("dsb")
            dqa_ref[h, rows, :] = _dot(dsb, k2s)
            dk[h].append(_dot_tn(dsb, qs_ref[rows, :]))
            dv[h].append(_dot_tn(st.pop("pb"), do_ref[rows, :]))

        _skewed(_band_streams(A_TQ), [scores, dscores, grads])
        dq_ref[...] = jnp.where(head0, dqa_ref[0], dqa_ref[1]).astype(dq_ref.dtype)
        dk_ref[window, :] += jnp.where(head0, sum(dk[0]), sum(dk[1]))
        dv_ref[window, :] += jnp.where(head0, sum(dv[0]), sum(dv[1]))

        @pl.when(i == T // A_TQ - 1)
        def _():
            dko_ref[...] = dk_ref[...].astype(BF16)
            dvo_ref[...] = dv_ref[...].astype(BF16)

    return _call_carrying(
        job, body, name=name, grid=(n_hp, T // A_TQ),
        in_specs=[pl.BlockSpec((A_TQ, LANES), lambda hp, i: (i, hp)),
                  pl.BlockSpec((T, LANES), lambda hp, i: (0, hp + n_hp)),
                  pl.BlockSpec((T, LANES), lambda hp, i: (0, hp + 2 * n_hp)),
                  _bias_spec(),
                  pl.BlockSpec((A_TQ, LANES), lambda hp, i: (i, hp))],
        out_specs=[pl.BlockSpec((A_TQ, LANES), lambda hp, i: (i, hp)),
                   pl.BlockSpec((T, LANES), lambda hp, i: (0, hp)),
                   pl.BlockSpec((T, LANES), lambda hp, i: (0, hp)),
                   _bias_spec()],
        out_shape=[jax.ShapeDtypeStruct((T, WIDTH), BF16),
                   jax.ShapeDtypeStruct((T, WIDTH), BF16),
                   jax.ShapeDtypeStruct((T, WIDTH), BF16),
                   jax.ShapeDtypeStruct((len(_BIAS_SHIFTS), WIDTH // HEAD_DIM, A_TQ, A_WIN), F32)],
        scratch_shapes=[pltpu.VMEM((2, A_TQ, LANES), BF16), pltpu.VMEM((2, A_TQ, LANES), BF16),
                        pltpu.VMEM((2, A_TQ, LANES), F32),
                        pltpu.VMEM((TP, LANES), F32), pltpu.VMEM((TP, LANES), F32),
                        pltpu.VMEM((A_TQ, LANES), BF16)],
        vmem_mib=56, args=(qkv, qkv, qkv, bias, do))


_N_DIAG = 2 * CHUNK - 1
_EXT_TOP = LEFT_CHUNKS * CHUNK + CHUNK - 1 + REL_CLIP


def _toeplitz_bias(rb):
    H = rb.shape[0]
    ext = jnp.concatenate([rb, jnp.broadcast_to(rb[:, N_REL - 1:], (H, _EXT_TOP + 1 - N_REL))], axis=1)
    vec = jnp.stack([ext[:, _EXT_TOP - (_N_DIAG - 1) - CHUNK * k:_EXT_TOP - CHUNK * k + 1]
                     for k in range(LEFT_CHUNKS + 1)], axis=1)
    rev = jnp.pad(vec[:, :, ::-1], ((0, 0), (0, 0), (0, 1)))
    flat = jnp.broadcast_to(rev[:, :, None, :], (H, LEFT_CHUNKS + 1, CHUNK, _N_DIAG + 1))
    skew = flat.reshape(H, LEFT_CHUNKS + 1, -1)[:, :, :CHUNK * _N_DIAG].reshape(H, LEFT_CHUNKS + 1, CHUNK, _N_DIAG)
    blocks = skew[:, :, :, CHUNK - 1:]
    neg = jnp.full((H, CHUNK, CHUNK), NEG, F32)
    rows = [jnp.concatenate([blocks[:, b - a] if 0 <= b - a <= LEFT_CHUNKS else neg for b in range(A_WIN // CHUNK)],
                            axis=2) for a in range(A_TQ // CHUNK)]
    return jnp.concatenate(rows, axis=1)


def _toeplitz_bias_grad(db):
    H = db.shape[0]
    d5 = db.reshape(H, A_TQ // CHUNK, CHUNK, A_WIN // CHUNK, CHUNK)
    g_blocks = jnp.stack([sum(d5[:, a, :, a + k, :] for a in range(A_TQ // CHUNK))
                          for k in range(LEFT_CHUNKS + 1)], axis=1)
    d_skew = jnp.pad(g_blocks, ((0, 0), (0, 0), (0, 0), (CHUNK - 1, 0)))
    d_flat = jnp.pad(d_skew.reshape(H, LEFT_CHUNKS + 1, CHUNK * _N_DIAG), ((0, 0), (0, 0), (0, CHUNK)))
    g_vec = jnp.sum(d_flat.reshape(H, LEFT_CHUNKS + 1, CHUNK, _N_DIAG + 1), axis=2)[:, :, :_N_DIAG][:, :, ::-1]
    g_ext = sum(jnp.pad(g_vec[:, k], ((0, 0), (_EXT_TOP - (_N_DIAG - 1) - CHUNK * k, CHUNK * k)))
                for k in range(LEFT_CHUNKS + 1))
    return jnp.concatenate([g_ext[:, :N_REL - 1], jnp.sum(g_ext[:, N_REL - 1:], axis=1, keepdims=True)], axis=1)


def _split_bf16(x):
    hi = x.astype(BF16)
    lo = (x - hi.astype(F32)).astype(BF16)
    return hi, lo


def _sb_streams(d, strips=None, **tile):
    out = []
    for h in range(2):
        for r in (range(B_TQ // B_TS) if strips is None else strips):
            if d is not None and d > r:
                continue
            out.append(dict(h=h, r=r, rows=pl.ds(r * B_TS, B_TS), diag=(d is not None and d == r), **tile))
    return out


def _sb_sweep(i, car_ref, streams_of, run):
    sub = B_TQ // B_TS
    run([st for d in reversed(range(sub)) for st in streams_of(i * sub + d, d, None)])

    def alive(c):
        return (c[0] < i * sub) & (c[1] > B_DEAD)

    def step(c):
        kb = i * sub - 1 - c[0]
        if sub > 1:
            lower_alive = jnp.max(car_ref[:, B_TS:, :]) > B_DEAD
            lax.cond(lower_alive, lambda: run(streams_of(kb, None, None)), lambda: run(streams_of(kb, None, [0])))
        else:
            run(streams_of(kb, None, None))
        return c[0] + 1, jnp.max(car_ref[...])

    lax.while_loop(alive, step, (jnp.int32(0), jnp.float32(0.0)))


def _piece_rows(st, p):
    return pl.ds(st["r"] * B_TS + p, B_PIECE)


def _rows_cat(parts):
    return jnp.concatenate(parts, axis=0)


def _skewed(streams, stages):
    for t in range(len(streams) + len(stages) - 1):
        for s, st in enumerate(streams):
            if 0 <= t - s < len(stages):
                stages[t - s](st)


def _sb_logs(st, z2):
    log_beta, log_keep, keep_bf = [], [], []
    for p in range(0, B_TS, B_PIECE):
        z = z2[p:p + B_PIECE]
        lp2 = jnp.log(1.0 + jnp.exp2(-jnp.abs(z))) * LOG2E
        lb = jnp.minimum(z, 0.0) - lp2
        lk = lb - z
        if st["diag"]:
            lk = jnp.where(_strict_lower(p), lk, 0.0)
        log_beta.append(lb)
        log_keep.append(lk)
        keep_bf.append(lk.astype(BF16))
    st["log_beta"] = _rows_cat(log_beta)
    st["log_keep"] = _rows_cat(log_keep)
    st["keep_bf"] = _rows_cat(keep_bf)


def _strict_lower(p):
    t = p + lax.broadcasted_iota(jnp.int32, (B_PIECE, B_TS), 0)
    s = lax.broadcasted_iota(jnp.int32, (B_PIECE, B_TS), 1)
    return s < t


def _tri(strict):
    j = lax.broadcasted_iota(jnp.int32, (B_TS, B_TS), 0)
    s = lax.broadcasted_iota(jnp.int32, (B_TS, B_TS), 1)
    return jnp.where(j > s if strict else j >= s, 1.0, 0.0).astype(BF16)


def _call_carrying(job, body, *, name, grid, in_specs, out_specs, out_shape, scratch_shapes, vmem_mib, args):
    n_in, n_out, n_scr = len(in_specs), len(out_specs), len(scratch_shapes)
    if job is None:
        res = pl.pallas_call(body, name=name, grid=grid, in_specs=in_specs, out_specs=out_specs,
                             out_shape=out_shape, scratch_shapes=scratch_shapes,
                             compiler_params=_cparams(("arbitrary",) * len(grid), vmem_mib))(*args)
        return res, []
    j_in, j_out = len(job.arrays), len(job.out_shape)
    hbm = pl.BlockSpec(memory_space=pltpu.HBM)

    def carrying(*refs):
        refs = list(refs)
        ins, refs = refs[:n_in], refs[n_in:]
        j_ins, refs = refs[:j_in], refs[j_in:]
        outs, refs = refs[:n_out], refs[n_out:]
        j_outs, refs = refs[:j_out], refs[j_out:]
        scr, sems = refs[:n_scr], refs[n_scr:]
        first = functools.reduce(jnp.logical_and, [pl.program_id(d) == 0 for d in range(len(grid))])
        last = functools.reduce(jnp.logical_and, [pl.program_id(d) == grid[d] - 1 for d in range(len(grid))])

        @pl.when(first)
        def _():
            job.start(j_ins, j_outs, sems)

        body(*ins, *outs, *scr)

        @pl.when(last)
        def _():
            job.wait(j_ins, j_outs, sems)

    res = pl.pallas_call(
        carrying, name=name, grid=grid,
        in_specs=list(in_specs) + [hbm] * j_in, out_specs=list(out_specs) + [hbm] * j_out,
        out_shape=list(out_shape) + job.out_shape, scratch_shapes=list(scratch_shapes) + job.scratch(),
        compiler_params=_cparams(("arbitrary",) * len(grid), vmem_mib))(*args, *job.arrays)
    return res[:n_out], res[n_out:]


def _attn_b_fwd(qkv, *, col0, name, job=None):
    T = qkv.shape[0]
    n_hp = WIDTH // LANES
    sub = B_TQ // B_TS

    def body(q_ref, k_ref, v_ref, o_ref, acc_ref, car_ref, qh_ref):
        i = pl.program_id(1)
        q2 = q_ref[...]
        head0 = _lane_is_head0()
        qh_ref[0] = jnp.where(head0, q2, jnp.zeros_like(q2))
        qh_ref[1] = jnp.where(head0, jnp.zeros_like(q2), q2)
        tri_s = _tri(True)
        acc_ref[...] = jnp.zeros_like(acc_ref)
        car_ref[...] = jnp.zeros_like(car_ref)

        def streams_of(kb, d, strips):
            keys = pl.ds(pl.multiple_of(kb * B_TS, B_TS), B_TS)
            return _sb_streams(d, strips, k2=k_ref[keys, :], v2=v_ref[keys, :])

        def scores(st):
            st["z2"] = _dot_nt(qh_ref[st["h"], st["rows"], :], st.pop("k2")) * (QK_SCALE * LOG2E)

        def logs(st):
            _sb_logs(st, st.pop("z2"))

        def suffix(st):
            st["suffix"] = _dot(st.pop("keep_bf"), tri_s)

        def weights(st):
            log_beta, suffix, log_keep = st.pop("log_beta"), st.pop("suffix"), st.pop("log_keep")
            wb = []
            for p in range(0, B_TS, B_PIECE):
                rows = _piece_rows(st, p)
                car = car_ref[st["h"], rows, :]
                w = jnp.exp2(log_beta[p:p + B_PIECE] + suffix[p:p + B_PIECE] + car)
                if st["diag"]:
                    w = jnp.where(_strict_lower(p), w, 0.0)
                wb.append(w.astype(BF16))
                car_ref[st["h"], rows, :] = car + jnp.sum(log_keep[p:p + B_PIECE], axis=1, keepdims=True)
            st["wb"] = _rows_cat(wb)

        def values(st):
            acc_ref[st["h"], st["rows"], :] += _dot(st.pop("wb"), st.pop("v2"))

        _sb_sweep(i, car_ref, streams_of, lambda sts: _skewed(sts, [scores, logs, suffix, weights, values]))
        o_ref[...] = jnp.where(head0, acc_ref[0], acc_ref[1])

    (out,), rode = _call_carrying(
        job, body, name=name, grid=(n_hp, T // B_TQ),
        in_specs=[pl.BlockSpec((B_TQ, LANES), lambda hp, i: (i, hp + col0)),
                  pl.BlockSpec((T, LANES), lambda hp, i: (0, hp + col0 + n_hp)),
                  pl.BlockSpec((T, LANES), lambda hp, i: (0, hp + col0 + 2 * n_hp))],
        out_specs=[pl.BlockSpec((B_TQ, LANES), lambda hp, i: (i, hp))],
        out_shape=[jax.ShapeDtypeStruct((T, WIDTH), F32)],
        scratch_shapes=[pltpu.VMEM((2, B_TQ, LANES), F32), pltpu.VMEM((2, B_TQ, 1), F32),
                        pltpu.VMEM((2, B_TQ, LANES), BF16)],
        vmem_mib=48, args=(qkv, qkv, qkv))
    return out, rode


def _attn_b_bwd(qkv, out, do, *, col0, name, job=None):
    T = qkv.shape[0]
    n_hp = WIDTH // LANES
    sub = B_TQ // B_TS

    def body(q_ref, k_ref, v_ref, o_ref, do_ref, dq_ref, dko_ref, dvo_ref,
             dqa_ref, car_ref, carr_ref, tot_ref, qh_ref, doh_ref, qs_ref, dk_ref, dv_ref):
        i = pl.program_id(1)

        @pl.when(i == 0)
        def _():
            dk_ref[...] = jnp.zeros_like(dk_ref)
            dv_ref[...] = jnp.zeros_like(dv_ref)

        q2 = q_ref[...]
        do2 = do_ref[...]
        head0 = _lane_is_head0()
        zero = jnp.zeros_like(q2)
        qh_ref[0] = jnp.where(head0, q2, zero)
        qh_ref[1] = jnp.where(head0, zero, q2)
        doh_ref[0] = jnp.where(head0, do2, zero)
        doh_ref[1] = jnp.where(head0, zero, do2)
        scale = jnp.asarray(QK_SCALE, BF16)
        qs_ref[...] = q2 * scale
        tri_s = _tri(True)
        tri_i = _tri(False)
        prod = do2.astype(F32) * o_ref[...]
        tot_ref[0] = jnp.sum(jnp.where(head0, prod, 0.0), axis=1, keepdims=True)
        tot_ref[1] = jnp.sum(jnp.where(head0, 0.0, prod), axis=1, keepdims=True)
        dqa_ref[...] = jnp.zeros_like(dqa_ref)
        car_ref[...] = jnp.zeros_like(car_ref)
        carr_ref[...] = jnp.zeros_like(carr_ref)

        def streams_of(kb, d, strips):
            keys = pl.ds(pl.multiple_of(kb * B_TS, B_TS), B_TS)
            k2 = k_ref[keys, :]
            return _sb_streams(d, strips, keys=keys, k2=k2, v2=v_ref[keys, :], k2s=k2 * scale)

        def run(streams):
            def scores(st):
                st["z2"] = _dot_nt(qh_ref[st["h"], st["rows"], :], st.pop("k2")) * (QK_SCALE * LOG2E)
                st["dw"] = _dot_nt(doh_ref[st["h"], st["rows"], :], st.pop("v2"))

            def logs(st):
                _sb_logs(st, st.pop("z2"))

            def suffix(st):
                st["suffix"] = _dot(st.pop("keep_bf"), tri_s)

            def weights(st):
                h = st["h"]
                suffix, dw = st.pop("suffix"), st.pop("dw")
                wb, dlog, hi, lo = [], [], [], []
                for p in range(0, B_TS, B_PIECE):
                    rows = _piece_rows(st, p)
                    car = car_ref[h, rows, :]
                    w = jnp.exp2(st["log_beta"][p:p + B_PIECE] + suffix[p:p + B_PIECE] + car)
                    if st["diag"]:
                        w = jnp.where(_strict_lower(p), w, 0.0)
                    w = w.astype(BF16)
                    dl = w.astype(F32) * dw[p:p + B_PIECE]
                    dl_hi, dl_lo = _split_bf16(dl)
                    wb.append(w)
                    dlog.append(dl)
                    hi.append(dl_hi)
                    lo.append(dl_lo)
                    car_ref[h, rows, :] = car + jnp.sum(st["log_keep"][p:p + B_PIECE], axis=1, keepdims=True)
                st["wb"], st["dlog"], st["hi"], st["lo"] = _rows_cat(wb), _rows_cat(dlog), _rows_cat(hi), _rows_cat(lo)

            def later(st):
                st["later"] = _dot(st.pop("hi"), tri_i) + _dot(st.pop("lo"), tri_i)

            def dscores(st):
                h = st["h"]
                later, dlog = st.pop("later"), st.pop("dlog")
                log_keep, log_beta = st.pop("log_keep"), st.pop("log_beta")
                dzb = []
                for p in range(0, B_TS, B_PIECE):
                    rows = _piece_rows(st, p)
                    pc = slice(p, p + B_PIECE)
                    carr = carr_ref[h, rows, :]
                    earlier = tot_ref[h, rows, :] - (later[pc] + carr)
                    dz = dlog[pc] * jnp.exp2(log_keep[pc]) - jnp.exp2(log_beta[pc]) * earlier
                    if st["diag"]:
                        dz = jnp.where(_strict_lower(p), dz, 0.0)
                    dzb.append(dz.astype(BF16))
                    carr_ref[h, rows, :] = carr + jnp.sum(dlog[pc], axis=1, keepdims=True)
                st["dzb"] = _rows_cat(dzb)

            def grads(st):
                h, rows, keys = st["h"], st["rows"], st["keys"]
                mine = head0 if h == 0 else jnp.logical_not(head0)
                dzb = st.pop("dzb")
                dqa_ref[h, rows, :] += _dot(dzb, st.pop("k2s"))
                dk_ref[keys, :] += jnp.where(mine, _dot_tn(dzb, qs_ref[rows, :]), 0.0)
                dv_ref[keys, :] += jnp.where(mine, _dot_tn(st.pop("wb"), do_ref[rows, :]), 0.0)

            _skewed(streams, [scores, logs, suffix, weights, later, dscores, grads])

        _sb_sweep(i, car_ref, streams_of, run)
        dq_ref[...] = jnp.where(head0, dqa_ref[0], dqa_ref[1]).astype(dq_ref.dtype)

        @pl.when(i == T // B_TQ - 1)
        def _():
            dko_ref[...] = dk_ref[...].astype(BF16)
            dvo_ref[...] = dv_ref[...].astype(BF16)

    return _call_carrying(
        job, body, name=name, grid=(n_hp, T // B_TQ),
        in_specs=[pl.BlockSpec((B_TQ, LANES), lambda hp, i: (i, hp + col0)),
                  pl.BlockSpec((T, LANES), lambda hp, i: (0, hp + col0 + n_hp)),
                  pl.BlockSpec((T, LANES), lambda hp, i: (0, hp + col0 + 2 * n_hp)),
                  pl.BlockSpec((B_TQ, LANES), lambda hp, i: (i, hp)),
                  pl.BlockSpec((B_TQ, LANES), lambda hp, i: (i, hp))],
        out_specs=[pl.BlockSpec((B_TQ, LANES), lambda hp, i: (i, hp)),
                   pl.BlockSpec((T, LANES), lambda hp, i: (0, hp)),
                   pl.BlockSpec((T, LANES), lambda hp, i: (0, hp))],
        out_shape=[jax.ShapeDtypeStruct((T, WIDTH), BF16),
                   jax.ShapeDtypeStruct((T, WIDTH), BF16),
                   jax.ShapeDtypeStruct((T, WIDTH), BF16)],
        scratch_shapes=[pltpu.VMEM((2, B_TQ, LANES), F32), pltpu.VMEM((2, B_TQ, 1), F32),
                        pltpu.VMEM((2, B_TQ, 1), F32), pltpu.VMEM((2, B_TQ, 1), F32),
                        pltpu.VMEM((2, B_TQ, LANES), BF16), pltpu.VMEM((2, B_TQ, LANES), BF16),
                        pltpu.VMEM((B_TQ, LANES), BF16),
                        pltpu.VMEM((T, LANES), F32), pltpu.VMEM((T, LANES), F32)],
        vmem_mib=56, args=(qkv, qkv, qkv, out, do))


def _gated_mix(oa_ref, ob_ref, g_ref, bg_ref, wpa_ref, wpb_ref, D):
    ya = _dot(oa_ref[...].astype(BF16), wpa_ref[...])
    yb = _dot(ob_ref[...].astype(BF16), wpb_ref[...])
    sa = jax.nn.sigmoid(g_ref[:, :D] + bg_ref[:, :D])
    sb = jax.nn.sigmoid(g_ref[:, D:] + bg_ref[:, D:])
    return ya, yb, sa, sb


def _proj_fwd(oa, ob, g, bg, wpa, wpb, wo, wl, xin, lng, lnb, layer, *, alpha, name):
    T, D = xin.shape
    tm = _tile(T, 512)
    row = lambda i: (i, 0)
    wspec = lambda r, c: pl.BlockSpec((None, r, c), lambda i: (wl, 0, 0))
    vec = lambda c: pl.BlockSpec((None, 1, c), lambda i: (layer, 0, 0))

    def body(oa_ref, ob_ref, g_ref, bg_ref, wpa_ref, wpb_ref, wo_ref, x_ref, lg_ref, lb_ref, x1_ref, r1_ref, x1b_ref):
        ya, yb, sa, sb = _gated_mix(oa_ref, ob_ref, g_ref, bg_ref, wpa_ref, wpb_ref, D)
        mix = _dot((sa * ya + sb * yb).astype(BF16), wo_ref[...])
        r1 = alpha * x_ref[...] + mix
        r1_ref[...] = r1
        x1 = _ln_fwd(r1, lg_ref[...], lb_ref[...])
        x1_ref[...] = x1
        x1b_ref[...] = x1.astype(BF16)

    return pl.pallas_call(
        body, name=name, grid=(T // tm,),
        in_specs=[pl.BlockSpec((tm, WIDTH), row), pl.BlockSpec((tm, WIDTH), row), pl.BlockSpec((tm, 2 * D), row),
                  vec(2 * D), wspec(WIDTH, D), wspec(WIDTH, D), wspec(D, D),
                  pl.BlockSpec((tm, D), row), vec(D), vec(D)],
        out_specs=[pl.BlockSpec((tm, D), row), pl.BlockSpec((tm, D), row), pl.BlockSpec((tm, D), row)],
        out_shape=[jax.ShapeDtypeStruct((T, D), F32), jax.ShapeDtypeStruct((T, D), F32),
                   jax.ShapeDtypeStruct((T, D), BF16)],
        compiler_params=_cparams(("arbitrary",), 56),
    )(oa, ob, g, bg, wpa, wpb, wo, xin, lng, lnb)


def _proj_bwd(dx1, r1, lng, oa, ob, g, bg, wpa, wpb, wo, wl, layer, *, name):
    T, D = dx1.shape
    tm = _tile(T, 512)
    row = lambda i: (i, 0)
    fixed = lambda i: (0, 0)
    wspec = lambda r, c: pl.BlockSpec((None, r, c), lambda i: (wl, 0, 0))
    vec = lambda c: pl.BlockSpec((None, 1, c), lambda i: (layer, 0, 0))

    def body(dx_ref, r1_ref, lg_ref, oa_ref, ob_ref, g_ref, bg_ref, wpa_ref, wpb_ref, wo_ref,
             dr_ref, mix_ref, dya_ref, dyb_ref, dg_ref, doa_ref, dob_ref, dlg_ref, dlb_ref, dbg_ref):
        @pl.when(pl.program_id(0) == 0)
        def _():
            dlg_ref[...] = jnp.zeros_like(dlg_ref)
            dlb_ref[...] = jnp.zeros_like(dlb_ref)
            dbg_ref[...] = jnp.zeros_like(dbg_ref)

        dx = dx_ref[...]
        dr, xhat = _ln_bwd(dx, r1_ref[...], lg_ref[...])
        dr_ref[...] = dr
        dlg_ref[...] += jnp.sum(dx * xhat, axis=0, keepdims=True)
        dlb_ref[...] += jnp.sum(dx, axis=0, keepdims=True)
        dmix = _dot_nt(dr.astype(BF16), wo_ref[...])
        ya, yb, sa, sb = _gated_mix(oa_ref, ob_ref, g_ref, bg_ref, wpa_ref, wpb_ref, D)
        mix_ref[...] = (sa * ya + sb * yb).astype(BF16)
        dya = (dmix * sa).astype(BF16)
        dyb = (dmix * sb).astype(BF16)
        dya_ref[...] = dya
        dyb_ref[...] = dyb
        dga = dmix * ya * (sa * (1.0 - sa))
        dgb = dmix * yb * (sb * (1.0 - sb))
        dg_ref[:, :D] = dga.astype(BF16)
        dg_ref[:, D:] = dgb.astype(BF16)
        dbg_ref[:, :D] += jnp.sum(dga, axis=0, keepdims=True)
        dbg_ref[:, D:] += jnp.sum(dgb, axis=0, keepdims=True)
        doa_ref[...] = _dot_nt(dya, wpa_ref[...]).astype(BF16)
        dob_ref[...] = _dot_nt(dyb, wpb_ref[...]).astype(BF16)

    return pl.pallas_call(
        body, name=name, grid=(T // tm,),
        in_specs=[pl.BlockSpec((tm, D), row), pl.BlockSpec((tm, D), row), vec(D),
                  pl.BlockSpec((tm, WIDTH), row), pl.BlockSpec((tm, WIDTH), row), pl.BlockSpec((tm, 2 * D), row),
                  vec(2 * D), wspec(WIDTH, D), wspec(WIDTH, D), wspec(D, D)],
        out_specs=[pl.BlockSpec((tm, D), row), pl.BlockSpec((tm, D), row), pl.BlockSpec((tm, D), row),
                   pl.BlockSpec((tm, D), row), pl.BlockSpec((tm, 2 * D), row),
                   pl.BlockSpec((tm, WIDTH), row), pl.BlockSpec((tm, WIDTH), row),
                   pl.BlockSpec((1, D), fixed), pl.BlockSpec((1, D), fixed), pl.BlockSpec((1, 2 * D), fixed)],
        out_shape=[jax.ShapeDtypeStruct((T, D), F32), jax.ShapeDtypeStruct((T, D), BF16),
                   jax.ShapeDtypeStruct((T, D), BF16), jax.ShapeDtypeStruct((T, D), BF16),
                   jax.ShapeDtypeStruct((T, 2 * D), BF16),
                   jax.ShapeDtypeStruct((T, WIDTH), BF16), jax.ShapeDtypeStruct((T, WIDTH), BF16),
                   jax.ShapeDtypeStruct((1, D), F32), jax.ShapeDtypeStruct((1, D), F32),
                   jax.ShapeDtypeStruct((1, 2 * D), F32)],
        compiler_params=_cparams(("arbitrary",), 56),
    )(dx1, r1, lng, oa, ob, g, bg, wpa, wpb, wo)


def _ffn_fwd(x1, wfi, wfo, wl, lng, lnb, layer, *, alpha, name, job=None):
    T, D = x1.shape
    tf = wfi.shape[-1]
    nj = wfi.shape[0] // 2
    tm = _tile(T, 1024)
    vec = lambda c: pl.BlockSpec((None, 1, c), lambda i, j: (layer, 0, 0))

    def body(x_ref, wg_ref, wu_ref, wo_ref, lg_ref, lb_ref, gs_ref, us_ref, r2_ref, x2_ref, acc_ref, xb_ref):
        j = pl.program_id(1)

        @pl.when(j == 0)
        def _():
            xb_ref[...] = x_ref[...].astype(BF16)
            acc_ref[...] = jnp.zeros_like(acc_ref)

        gv = _dot(xb_ref[...], wg_ref[...])
        uv = _dot(xb_ref[...], wu_ref[...])
        gs_ref[...] = gv.astype(gs_ref.dtype)
        us_ref[...] = uv.astype(us_ref.dtype)
        act = gv * jax.nn.sigmoid(gv) * uv
        acc_ref[...] += _dot(act.astype(BF16), wo_ref[...])

        @pl.when(j == nj - 1)
        def _():
            r2 = alpha * x_ref[...] + acc_ref[...]
            r2_ref[...] = r2
            x2_ref[...] = _ln_fwd(r2, lg_ref[...], lb_ref[...])

    return _call_carrying(
        job, body, name=name, grid=(T // tm, nj),
        in_specs=[pl.BlockSpec((tm, D), lambda i, j: (i, 0)),
                  pl.BlockSpec((None, None, D, tf), lambda i, j: (j, wl, 0, 0)),
                  pl.BlockSpec((None, None, D, tf), lambda i, j: (j + nj, wl, 0, 0)),
                  pl.BlockSpec((None, tf, D), lambda i, j: (wl, j, 0)),
                  vec(D), vec(D)],
        out_specs=[pl.BlockSpec((None, tm, tf), lambda i, j: (j, i, 0)),
                   pl.BlockSpec((None, tm, tf), lambda i, j: (j, i, 0)),
                   pl.BlockSpec((tm, D), lambda i, j: (i, 0)),
                   pl.BlockSpec((tm, D), lambda i, j: (i, 0))],
        out_shape=[jax.ShapeDtypeStruct((nj, T, tf), BF16), jax.ShapeDtypeStruct((nj, T, tf), BF16),
                   jax.ShapeDtypeStruct((T, D), F32), jax.ShapeDtypeStruct((T, D), F32)],
        scratch_shapes=[pltpu.VMEM((tm, D), F32), pltpu.VMEM((tm, D), BF16)],
        vmem_mib=56, args=(x1, wfi, wfi, wfo, lng, lnb))


def _ffn_bwd(dx2, r2, lng, gs, us, wfi, wfo, wl, layer, *, alpha, name, job=None):
    T, D = dx2.shape
    tf = wfi.shape[-1]
    nj = wfi.shape[0] // 2
    tm = _tile(T, 512)
    vec = lambda c: pl.BlockSpec((None, 1, c), lambda i, j: (layer, 0, 0))
    blk = lambda: pl.BlockSpec((None, tm, tf), lambda i, j: (j, i, 0))

    def body(dx_ref, r2_ref, lg_ref, gs_ref, us_ref, wg_ref, wu_ref, wo_ref,
             dr_ref, act_ref, dg_ref, du_ref, dx1_ref, dlg_ref, dlb_ref, acc_ref, drb_ref):
        i = pl.program_id(0)
        j = pl.program_id(1)

        @pl.when((i == 0) & (j == 0))
        def _():
            dlg_ref[...] = jnp.zeros_like(dlg_ref)
            dlb_ref[...] = jnp.zeros_like(dlb_ref)

        @pl.when(j == 0)
        def _():
            dx = dx_ref[...]
            dr, xhat = _ln_bwd(dx, r2_ref[...], lg_ref[...])
            dlg_ref[...] += jnp.sum(dx * xhat, axis=0, keepdims=True)
            dlb_ref[...] += jnp.sum(dx, axis=0, keepdims=True)
            drb_ref[...] = dr.astype(BF16)
            dr_ref[...] = dr.astype(BF16)
            acc_ref[...] = alpha * dr

        dact = _dot_nt(drb_ref[...], wo_ref[...])
        gv = gs_ref[...].astype(F32)
        uv = us_ref[...].astype(F32)
        s = jax.nn.sigmoid(gv)
        silu = gv * s
        act_ref[...] = (silu * uv).astype(BF16)
        dg = (dact * uv * (s * (1.0 + gv * (1.0 - s)))).astype(BF16)
        du = (dact * silu).astype(BF16)
        dg_ref[...] = dg
        du_ref[...] = du
        acc_ref[...] += _dot_nt(dg, wg_ref[...]) + _dot_nt(du, wu_ref[...])

        @pl.when(j == nj - 1)
        def _():
            dx1_ref[...] = acc_ref[...]

    return _call_carrying(
        job, body, name=name, grid=(T // tm, nj),
        in_specs=[pl.BlockSpec((tm, D), lambda i, j: (i, 0)), pl.BlockSpec((tm, D), lambda i, j: (i, 0)), vec(D),
                  blk(), blk(),
                  pl.BlockSpec((None, None, D, tf), lambda i, j: (j, wl, 0, 0)),
                  pl.BlockSpec((None, None, D, tf), lambda i, j: (j + nj, wl, 0, 0)),
                  pl.BlockSpec((None, tf, D), lambda i, j: (wl, j, 0))],
        out_specs=[pl.BlockSpec((tm, D), lambda i, j: (i, 0)), blk(), blk(), blk(),
                   pl.BlockSpec((tm, D), lambda i, j: (i, 0)),
                   pl.BlockSpec((1, D), lambda i, j: (0, 0)), pl.BlockSpec((1, D), lambda i, j: (0, 0))],
        out_shape=[jax.ShapeDtypeStruct((T, D), BF16),
                   jax.ShapeDtypeStruct((nj, T, tf), BF16), jax.ShapeDtypeStruct((nj, T, tf), BF16),
                   jax.ShapeDtypeStruct((nj, T, tf), BF16),
                   jax.ShapeDtypeStruct((T, D), F32),
                   jax.ShapeDtypeStruct((1, D), F32), jax.ShapeDtypeStruct((1, D), F32)],
        scratch_shapes=[pltpu.VMEM((tm, D), F32), pltpu.VMEM((tm, D), BF16)],
        vmem_mib=56, args=(dx2, r2, lng, gs, us, wfi, wfi, wfo))


def _loss_head(y, target, *, name):
    T, D = y.shape
    tm = _tile(T, 1024)

    def body(y_ref, t_ref, dy_ref, sq_ref):
        @pl.when(pl.program_id(0) == 0)
        def _():
            sq_ref[...] = jnp.zeros_like(sq_ref)
        err = y_ref[...] - t_ref[...]
        dy_ref[...] = err * (1.0 / D)
        sq_ref[...] += jnp.sum(err * err, axis=0, keepdims=True)

    return pl.pallas_call(
        body, name=name, grid=(T // tm,),
        in_specs=[pl.BlockSpec((tm, D), lambda i: (i, 0)), pl.BlockSpec((tm, D), lambda i: (i, 0))],
        out_specs=[pl.BlockSpec((tm, D), lambda i: (i, 0)), pl.BlockSpec((1, D), lambda i: (0, 0))],
        out_shape=[jax.ShapeDtypeStruct((T, D), F32), jax.ShapeDtypeStruct((1, D), F32)],
        compiler_params=_cparams(("arbitrary",)),
    )(y, target)


def _my_place():
    return lax.axis_index("x"), lax.axis_index("y"), lax.axis_index("c")


def _peer(place, k):
    x, y, c = place
    return (1 - x if k & 4 else x, 1 - y if k & 2 else y, 1 - c if k & 1 else c)


def _logical(place):
    x, y, c = place
    return 4 * x + 2 * y + c


def _block_of(ref, mode, idx):
    if mode == "blk":
        return ref.at[idx]
    if mode == "col":
        size = ref.shape[2] // N_DEV
        return ref.at[:, :, pl.ds(pl.multiple_of(idx * size, size), size)]
    size = ref.shape[1] // N_DEV
    return ref.at[:, pl.ds(pl.multiple_of(idx * size, size), size), :]


def _full_shape(shard, mode):
    if mode == "blk":
        return (N_DEV,) + shard.shape
    if mode == "col":
        return shard.shape[:2] + (N_DEV * shard.shape[2],)
    return (shard.shape[0], N_DEV * shard.shape[1], shard.shape[2])


class _Exchange:
    def __init__(self, arrays, out_shape, build):
        self.arrays = list(arrays)
        self.out_shape = list(out_shape)
        self.build = build

    def scratch(self):
        n = len(self.arrays)
        return [pltpu.SemaphoreType.DMA((n * N_DEV,)), pltpu.SemaphoreType.DMA((n * N_DEV,)),
                pltpu.SemaphoreType.DMA((n,))]

    def start(self, ins, outs, sems):
        for cp in self.build(ins, outs, *sems):
            cp.start()

    def wait(self, ins, outs, sems):
        for cp in self.build(ins, outs, *sems):
            cp.wait()

    def run(self, name):
        n_in, n_out = len(self.arrays), len(self.out_shape)
        hbm = pl.BlockSpec(memory_space=pltpu.HBM)

        def body(*refs):
            ins, outs, sems = refs[:n_in], refs[n_in:n_in + n_out], refs[n_in + n_out:]
            self.start(ins, outs, sems)
            self.wait(ins, outs, sems)

        return pl.pallas_call(
            body, name=name, in_specs=[hbm] * n_in, out_specs=[hbm] * n_out,
            out_shape=self.out_shape, scratch_shapes=self.scratch(),
        )(*self.arrays)


def _copies_to_all(src_of, dst_of, n, send, recv, local):
    me = _my_place()
    copies = []
    for a in range(n):
        copies.append(pltpu.make_async_copy(src_of(a, _logical(me)), dst_of(a), local.at[a]))
        for k in range(1, N_DEV):
            peer = _peer(me, k)
            copies.append(pltpu.make_async_remote_copy(
                src_ref=src_of(a, _logical(peer)), dst_ref=dst_of(a),
                send_sem=send.at[a * N_DEV + k], recv_sem=recv.at[a * N_DEV + k],
                device_id=peer, device_id_type=MESH))
    return copies


def _gather_job(shards, modes):
    def build(ins, outs, send, recv, local):
        my_id = _logical(_my_place())
        return _copies_to_all(lambda a, dev: ins[a], lambda a: _block_of(outs[a], modes[a], my_id),
                              len(shards), send, recv, local)

    return _Exchange(shards, [jax.ShapeDtypeStruct(_full_shape(s, m), s.dtype) for s, m in zip(shards, modes)], build)


def _gather_via_sibling(shard, mode, *, name):
    hbm = pl.BlockSpec(memory_space=pltpu.HBM)

    def body(x_ref, o_ref, send, recv, local):
        x, y, c = _my_place()
        me, sibling = (x, y, c), (x, y, 1 - c)
        chips = [(1 - x, y), (x, 1 - y), (1 - x, 1 - y)]

        def copy(k, block, to, src=None):
            dst = _block_of(o_ref, mode, _logical(block))
            return pltpu.make_async_remote_copy(src_ref=dst if src is None else src, dst_ref=dst,
                                                send_sem=send.at[k], recv_sem=recv.at[k],
                                                device_id=to, device_id_type=MESH)

        mine = pltpu.make_async_copy(x_ref, _block_of(o_ref, mode, _logical(me)), local)
        mine.start()
        first = [copy(0, me, sibling, src=x_ref)]
        first += [copy(1 + j, me, (*chip, c), src=x_ref) for j, chip in enumerate(chips)]
        for cp in first:
            cp.start()
        passed = [copy(4 + j, (*chip, c), sibling) for j, chip in enumerate(chips)]
        for j, chip in enumerate(chips):
            copy(1 + j, (*chip, c), me).wait_recv()
            passed[j].start()
        copy(0, sibling, me).wait_recv()
        for j, chip in enumerate(chips):
            copy(4 + j, (*chip, 1 - c), me).wait_recv()
        for cp in first + passed:
            cp.wait_send()
        mine.wait()

    return pl.pallas_call(
        body, name=name, in_specs=[hbm], out_specs=hbm,
        out_shape=jax.ShapeDtypeStruct(_full_shape(shard, mode), shard.dtype),
        scratch_shapes=[pltpu.SemaphoreType.DMA((7,)), pltpu.SemaphoreType.DMA((7,)), pltpu.SemaphoreType.DMA],
    )(shard)


def _grad_block(ref, mode, idx):
    if mode == "blk":
        return ref.at[idx]
    if mode == "col":
        size = ref.shape[1] // N_DEV
        return ref.at[:, pl.ds(pl.multiple_of(idx * size, size), size)]
    size = ref.shape[0] // N_DEV
    return ref.at[pl.ds(pl.multiple_of(idx * size, size), size), :]


def _grad_shard_shape(g, mode):
    if mode == "blk":
        return g.shape[1:]
    if mode == "col":
        return (g.shape[0], g.shape[1] // N_DEV)
    return (g.shape[0] // N_DEV, g.shape[1])


def _grads_job(groups, modes):
    flat = [(g, w, l) for w, per_w in enumerate(groups) for l, g in enumerate(per_w)]

    def build(ins, outs, send, recv, local):
        my_id = _logical(_my_place())
        return _copies_to_all(lambda a, dev: _grad_block(ins[a], modes[flat[a][1]], dev),
                              lambda a: outs[flat[a][1]].at[my_id, flat[a][2]],
                              len(flat), send, recv, local)

    out_shape = [jax.ShapeDtypeStruct((N_DEV, len(per_w)) + _grad_shard_shape(per_w[0], m), per_w[0].dtype)
                 for per_w, m in zip(groups, modes)]
    return _Exchange([g for g, _, _ in flat], out_shape, build)


def _adamw(w, g, m, v):
    m = ADAM_B1 * m + (1.0 - ADAM_B1) * g
    v = ADAM_B2 * v + (1.0 - ADAM_B2) * (g * g)
    m_hat = m / (1.0 - ADAM_B1 ** ADAM_STEP)
    v_hat = v / (1.0 - ADAM_B2 ** ADAM_STEP)
    delta = -ADAM_LR * (m_hat / (jnp.sqrt(v_hat) + ADAM_EPS) + ADAM_WD * w)
    return delta, m, v


def _sum_slots_adamw(slots, w, m, v, *, name):
    n_l = len(slots)
    R, C = slots[0].shape[1:]
    tr = _tile(R, 256)
    n_r = R // tr

    def body(*refs):
        s_refs = refs[:n_l]
        w_ref, m_ref, v_ref, g_out, d_out, m_out, v_out = refs[n_l:]
        for layer in range(n_l):
            @pl.when(pl.program_id(0) == layer)
            def _(s_ref=s_refs[layer]):
                g = s_ref[0].astype(F32)
                for s in range(1, N_DEV):
                    g = g + s_ref[s].astype(F32)
                delta, m_new, v_new = _adamw(w_ref[...], g, m_ref[...], v_ref[...])
                g_out[...] = g
                d_out[...] = delta
                m_out[...] = m_new
                v_out[...] = v_new

    slot_spec = lambda layer: pl.BlockSpec((N_DEV, tr, C), lambda l, i: (0, jnp.where(l == layer, i, 0), 0))
    spec = pl.BlockSpec((tr, C), lambda l, i: (l * n_r + i, 0))
    return pl.pallas_call(
        body, name=name, grid=(n_l, n_r),
        in_specs=[slot_spec(layer) for layer in range(n_l)] + [spec, spec, spec],
        out_specs=[spec] * 4,
        out_shape=[jax.ShapeDtypeStruct((n_l * R, C), F32)] * 4,
        compiler_params=_cparams(("arbitrary", "arbitrary")),
    )(*slots, w, m, v)


def _small_allreduce_adamw(g, w, m, v, *, name):
    R = g.shape[0]
    vmem = pl.BlockSpec(memory_space=pltpu.VMEM)

    def body(g_ref, w_ref, m_ref, v_ref, g_out, d_out, m_out, v_out, slots, send, recv):
        me = _my_place()
        my_id = _logical(me)
        slots[my_id] = g_ref[...]
        copies = []
        for k in range(1, N_DEV):
            cp = pltpu.make_async_remote_copy(
                src_ref=g_ref, dst_ref=slots.at[my_id], send_sem=send.at[k], recv_sem=recv.at[k],
                device_id=_peer(me, k), device_id_type=MESH)
            cp.start()
            copies.append(cp)
        for cp in copies:
            cp.wait()
        total = slots[0]
        for s in range(1, N_DEV):
            total = total + slots[s]
        delta, m_new, v_new = _adamw(w_ref[...], total, m_ref[...], v_ref[...])
        g_out[...] = total
        d_out[...] = delta
        m_out[...] = m_new
        v_out[...] = v_new

    return pl.pallas_call(
        body, name=name,
        in_specs=[vmem] * 4, out_specs=[vmem] * 4,
        out_shape=[jax.ShapeDtypeStruct((R, LANES), F32)] * 4,
        scratch_shapes=[pltpu.VMEM((N_DEV, R, LANES), F32),
                        pltpu.SemaphoreType.DMA((N_DEV,)), pltpu.SemaphoreType.DMA((N_DEV,))],
    )(g, w, m, v)


def _pack(parts):
    flat = jnp.concatenate([p.reshape(-1) for p in parts])
    rows = -(-flat.shape[0] // (8 * LANES)) * 8
    return jnp.pad(flat, (0, rows * LANES - flat.shape[0])).reshape(rows, LANES)


def _unpack(packed, like):
    flat = packed.reshape(-1)
    out, pos = [], 0
    for p in like:
        out.append(flat[pos:pos + p.size].reshape(p.shape))
        pos += p.size
    return out


def kernel(x, w_in, b_gate, rel_bias, w_proj_a, w_proj_b, w_out, ln1_g, ln1_b, w_ffn_in, w_ffn_out, ln2_g, ln2_b, loss_target, m_w_in, m_b_gate, m_rel_bias, m_w_proj_a, m_w_proj_b, m_w_out, m_ln1_g, m_ln1_b, m_w_ffn_in, m_w_ffn_out, m_ln2_g, m_ln2_b, v_w_in, v_b_gate, v_rel_bias, v_w_proj_a, v_w_proj_b, v_w_out, v_ln1_g, v_ln1_b, v_w_ffn_in, v_w_ffn_out, v_ln2_g, v_ln2_b):
    L = w_in.shape[0]
    T, D = x.shape[1], x.shape[2]
    alpha = float((2 * L) ** 0.25)
    n_qkv = 6 * WIDTH

    big = [w_in, w_proj_a, w_proj_b, w_out, w_ffn_in, w_ffn_out]
    kinds = ["in", "pa", "pb", "o", "fi", "fo"]
    modes = ["col", "col", "col", "row", "blk", "row"]
    mode_of = dict(zip(kinds, modes))
    w_bf = dict(zip(kinds, [w.astype(BF16) for w in big]))

    def gather_of(ks, l):
        return _gather_job([w_bf[k][l:l + 1] for k in ks], [mode_of[k] for k in ks])

    W = [dict() for _ in range(L)]
    W[0]["in"] = _gather_via_sibling(w_bf["in"][:1], mode_of["in"], name="gather_w_in_first")
    vec3 = lambda a: a[:, None, :]
    bg3, l1g, l1b, l2g, l2b = vec3(b_gate), vec3(ln1_g), vec3(ln1_b), vec3(ln2_g), vec3(ln2_b)
    b_col0 = 3 * WIDTH // LANES

    h = x[0]
    saved = []
    for l in range(L):
        ahead = l + 1 < L
        soon = ["pa", "pb", "o", "fo"]
        (qkv, gates), got = _in_proj(h, W[l]["in"], 0, n_qkv=n_qkv, name=f"in_proj_{l}",
                                     job=gather_of(soon, 0) if l == 0 else None)
        W[l].update(zip(soon, got))
        bias = _bias_variants(_toeplitz_bias(rel_bias[l]))
        oa, got = _attn_a_fwd(qkv, bias, name=f"attn_a_fwd_{l}", job=gather_of(["fi"], 0) if l == 0 else None)
        W[l].update(zip(["fi"], got))
        early = ["in", "pa", "pb", "o"]
        ob, got = _attn_b_fwd(qkv, col0=b_col0, name=f"attn_b_fwd_{l}", job=gather_of(early, l + 1) if ahead else None)
        W[l + 1 if ahead else l].update(zip(early, got))
        x1, r1, x1b = _proj_fwd(oa, ob, gates, bg3, W[l]["pa"], W[l]["pb"], W[l]["o"], 0, h, l1g, l1b, l,
                           alpha=alpha, name=f"proj_fwd_{l}")
        (gs, us, r2, x2), got = _ffn_fwd(x1, W[l]["fi"], W[l]["fo"], 0, l2g, l2b, l, alpha=alpha, name=f"ffn_fwd_{l}",
                                         job=gather_of(["fi", "fo"], l + 1) if ahead else None)
        W[l + 1 if ahead else l].update(zip(["fi", "fo"], got))
        saved.append((h, qkv, gates, bias, oa, ob, x1b, r1, gs, us, r2))
        h = x2

    d_h, sq = _loss_head(h, loss_target[0], name="loss_head")
    loss = lax.psum((0.5 / D) * jnp.sum(sq), ("x", "y", "c"))

    g_bg, g_rb, g_l1g, g_l1b, g_l2g, g_l2b = ([None] * L for _ in range(6))
    slot = {k: [None] * L for k in kinds}

    def exchange_of(ks, grads):
        return _grads_job([[g] for g in grads], [mode_of[k] for k in ks])

    w_in_above = None
    for l in reversed(range(L)):
        xin, qkv, gates, bias, oa, ob, x1b, r1, gs, us, r2 = saved[l]
        (dr2, act, dgt, dup, dx1, g_l2g[l], g_l2b[l]), got = _ffn_bwd(
            d_h, r2, l2g, gs, us, W[l]["fi"], W[l]["fo"], 0, l, alpha=alpha, name=f"ffn_bwd_{l}", job=w_in_above)
        if w_in_above is not None:
            (slot["in"][l + 1],) = got
        g_fo = _mm_tn_blocked_a(act, dr2, name=f"grad_w_ffn_out_{l}").reshape(-1, D)
        g_fi = jnp.concatenate(_mm_tn_blocked_pair(x1b, dgt, dup, name=f"grad_w_ffn_in_{l}"), axis=0)
        (dr1, mixin, dya, dyb, dgates, doa, dob, g_l1g[l], g_l1b[l], g_bg[l]) = _proj_bwd(
            dx1, r1, l1g, oa, ob, gates, bg3, W[l]["pa"], W[l]["pb"], W[l]["o"], 0, l, name=f"proj_bwd_{l}")
        g_o = _mm_tn(mixin, dr1, tm=_tile(D, 1024), tn=_tile(D, 1024), name=f"grad_w_out_{l}")
        g_pa = _mm_tn(oa, dya, tm=WIDTH, tn=_tile(D, 1024), name=f"grad_w_proj_a_{l}")
        g_pb = _mm_tn(ob, dyb, tm=WIDTH, tn=_tile(D, 1024), name=f"grad_w_proj_b_{l}")
        (dqa, dka, dva, dbias), (slot["fi"][l], slot["fo"][l]) = _attn_a_bwd(
            qkv, bias, doa, name=f"attn_a_bwd_{l}", job=exchange_of(["fi", "fo"], [g_fi, g_fo]))
        g_rb[l] = _toeplitz_bias_grad(_bias_variants_grad(dbias))
        (dqb, dkb, dvb), (slot["pa"][l], slot["pb"][l], slot["o"][l]) = _attn_b_bwd(
            qkv, ob, dob, col0=b_col0, name=f"attn_b_bwd_{l}", job=exchange_of(["pa", "pb", "o"], [g_pa, g_pb, g_o]))
        d_qkv = [dqa, dka, dva, dqb, dkb, dvb]
        g_in = _mm_tn_pieces(xin, d_qkv, dgates, name=f"grad_w_in_{l}")
        w_in_above = exchange_of(["in"], [g_in])
        d_h, got = _mm_nt_pieces_add(d_qkv, dgates, W[l]["in"], 0, dr1, alpha, name=f"grad_x_{l}",
                                     job=w_in_above if l == 0 else None)
        if l == 0:
            (slot["in"][0],) = got
    grad_x = d_h[None]

    moments_m = [m_w_in, m_w_proj_a, m_w_proj_b, m_w_out, m_w_ffn_in, m_w_ffn_out]
    moments_v = [v_w_in, v_w_proj_a, v_w_proj_b, v_w_out, v_w_ffn_in, v_w_ffn_out]
    names = ["w_in", "w_proj_a", "w_proj_b", "w_out", "w_ffn_in", "w_ffn_out"]
    big_out = {}
    for nm, k, w, m, v in zip(names, kinds, big, moments_m, moments_v):
        two = lambda a: a.reshape(-1, a.shape[-1])
        per_layer = [s.reshape(N_DEV, -1, s.shape[-1]) for s in slot[k]]
        res = _sum_slots_adamw(per_layer, two(w), two(m), two(v), name=f"adamw_{nm}")
        big_out[nm] = [r.reshape(w.shape) for r in res]

    small_w = [b_gate, rel_bias, ln1_g, ln1_b, ln2_g, ln2_b]
    small_g = [jnp.stack(g) for g in (g_bg, g_rb, g_l1g, g_l1b, g_l2g, g_l2b)]
    small_m = [m_b_gate, m_rel_bias, m_ln1_g, m_ln1_b, m_ln2_g, m_ln2_b]
    small_v = [v_b_gate, v_rel_bias, v_ln1_g, v_ln1_b, v_ln2_g, v_ln2_b]
    res = _small_allreduce_adamw(_pack(small_g), _pack(small_w), _pack(small_m), _pack(small_v),
                                 name="allreduce_small_adamw")
    small_names = ["b_gate", "rel_bias", "ln1_g", "ln1_b", "ln2_g", "ln2_b"]
    small_out = {nm: [] for nm in small_names}
    for packed in res:
        for nm, arr in zip(small_names, _unpack(packed, small_w)):
            small_out[nm].append(arr)

    order = ["w_in", "b_gate", "rel_bias", "w_proj_a", "w_proj_b", "w_out", "ln1_g", "ln1_b",
             "w_ffn_in", "w_ffn_out", "ln2_g", "ln2_b"]
    every = {**big_out, **small_out}
    outs = [loss, grad_x]
    for kind in range(4):
        outs += [every[nm][kind] for nm in order]
    return tuple(outs)
```
